```python
import math
import jax
import jax.numpy as jnp
from jax import lax
import numpy as np

D_MODEL = 1024
BATCH = 1
SEQ = 16384
DEPTH = 2

RW_HEADS = 8
RW_HEAD_DIM = 64
RW_WIDTH = RW_HEADS * RW_HEAD_DIM
RW_DECAY_LORA = 64
RW_AAA_LORA = 64
RW_VRES_LORA = 32
RW_GATE_LORA = 160
RW_SPLITS = (RW_WIDTH, RW_WIDTH, RW_WIDTH, RW_DECAY_LORA, RW_AAA_LORA, RW_GATE_LORA)
RW_COLS = 3 * RW_WIDTH + RW_DECAY_LORA + RW_AAA_LORA + RW_GATE_LORA
GN_EPS = 64e-5

SSD_HEADS = 16
SSD_HEAD_DIM = 64
SSD_WIDTH = SSD_HEADS * SSD_HEAD_DIM
SSD_GROUPS = 4
SSD_STATE = 128
SSD_CONV = 4
SSD_CHUNK = 128
SSD_XBC = SSD_WIDTH + 2 * SSD_GROUPS * SSD_STATE
SSD_SPLITS = (SSD_WIDTH, SSD_XBC, SSD_HEADS)
SSD_COLS = SSD_WIDTH + SSD_XBC + SSD_HEADS

FOX_HEADS = 8
FOX_HEAD_DIM = 64
FOX_WIDTH = FOX_HEADS * FOX_HEAD_DIM
FOX_BLOCK = 128
FOX_SPLITS = (FOX_WIDTH, FOX_WIDTH, FOX_WIDTH, FOX_HEADS)
FOX_COLS = 3 * FOX_WIDTH + FOX_HEADS

N_BRANCH = 3
GATE_COLS = N_BRANCH * D_MODEL
IN_SPLITS = (RW_COLS, SSD_COLS, FOX_COLS, GATE_COLS)
IN_COLS = RW_COLS + SSD_COLS + FOX_COLS + GATE_COLS

FFN_DENSE = 2816
N_EXPERTS = 8
TOP_K = 2
FFN_EXPERT = 3584
EPS = 1e-6

kernel_name = 'hybrid_rwkv7_ssd_fox_moe_adaln'


def _split(t, sizes):
    idx, acc = [], 0
    for s in sizes[:-1]:
        acc += s
        idx.append(acc)
    return jnp.split(t, idx, axis=-1)


def rms_norm(x, gain, eps=EPS):
    xf = x.astype(jnp.float32)
    y = xf * lax.rsqrt(jnp.mean(xf * xf, axis=-1, keepdims=True) + eps)
    return (y * gain.astype(jnp.float32)).astype(x.dtype)


def token_shift(p, mu):
    prev = jnp.pad(p, ((0, 0), (1, 0), (0, 0)))[:, :-1]
    return p + (prev - p) * mu


def wkv7_scan(r, decay, k, v, a_vec, b_vec):
    Bsz, L, H, N = r.shape

    def step(S, inp):
        r_t, w_t, k_t, v_t, a_t, b_t = inp
        sa = jnp.einsum('bhvk,bhk->bhv', S, a_t)
        S = S * w_t[:, :, None, :] + sa[..., None] * b_t[:, :, None, :] + v_t[..., None] * k_t[:, :, None, :]
        return S, jnp.einsum('bhvk,bhk->bhv', S, r_t)

    seq = tuple(jnp.moveaxis(t, 1, 0) for t in (r, decay, k, v, a_vec, b_vec))
    _, y = lax.scan(step, jnp.zeros((Bsz, H, N, N), jnp.float32), seq)
    return jnp.moveaxis(y, 0, 1)


def rwkv7_mixer(h, cols, mu, w0, w_up, a0, a_up, g_up, k_k, k_a, r_k, lnx_w, lnx_b, v_first, v_res):
    Bsz, L, _ = h.shape
    f32 = jnp.float32
    cols = token_shift(cols, mu)
    r, k, v, w_lo, a_lo, g_lo = _split(cols, RW_SPLITS)
    w = -jax.nn.softplus(-(w0 + jnp.tanh(w_lo) @ w_up)) - 0.5
    a = jax.nn.sigmoid(a0 + a_lo @ a_up)
    g = jax.nn.sigmoid(g_lo) @ g_up
    if v_res is None:
        v_first = v
    else:
        v0, v_down, v_up = v_res
        v = v + (v_first - v) * jax.nn.sigmoid(v0 + (h @ v_down) @ v_up)
    heads = lambda t: t.reshape(Bsz, L, RW_HEADS, RW_HEAD_DIM).astype(f32)
    kk = heads(k * k_k)
    kk = kk / jnp.maximum(jnp.sqrt(jnp.sum(kk * kk, axis=-1, keepdims=True)), 1e-12)
    k = k * (1 + (a - 1) * k_a)
    rh, kh, vh, ah = heads(r), heads(k), heads(v), heads(a)
    decay = jnp.exp(-jnp.exp(heads(w)))
    y = wkv7_scan(rh, decay, kh, vh, -kk, kk * ah)
    mean = jnp.mean(y, axis=-1, keepdims=True)
    var = jnp.mean(jnp.square(y - mean), axis=-1, keepdims=True)
    y = ((y - mean) * lax.rsqrt(var + GN_EPS)).reshape(Bsz, L, RW_WIDTH) * lnx_w + lnx_b
    bonus = jnp.sum(rh * kh * r_k, axis=-1, keepdims=True) * vh
    y = (y + bonus.reshape(Bsz, L, RW_WIDTH)).astype(h.dtype) * g
    return y, v_first


def causal_depthwise_conv(x, w, b):
    C = x.shape[-1]
    out = lax.conv_general_dilated(x, w[:, None, :].astype(x.dtype), window_strides=(1,),
                                   padding=((SSD_CONV - 1, 0),), dimension_numbers=('NWC', 'WIO', 'NWC'),
                                   feature_group_count=C)
    return out + b


def ssd_chunked(x, dt, A, Bm, Cm):
    f32 = jnp.float32
    Bsz, L = x.shape[:2]
    nc = L // SSD_CHUNK
    E = SSD_HEADS // SSD_GROUPS
    Q = SSD_CHUNK
    x = x.astype(f32).reshape(Bsz, nc, Q, SSD_GROUPS, E, SSD_HEAD_DIM)
    dt = dt.reshape(Bsz, nc, Q, SSD_GROUPS, E)
    Bm = Bm.astype(f32).reshape(Bsz, nc, Q, SSD_GROUPS, SSD_STATE)
    Cm = Cm.astype(f32).reshape(Bsz, nc, Q, SSD_GROUPS, SSD_STATE)
    a_cum = jnp.cumsum(dt * A.reshape(SSD_GROUPS, E), axis=2)
    xdt = x * dt[..., None]
    seg = a_cum[:, :, :, None] - a_cum[:, :, None, :]
    causal = jnp.tril(jnp.ones((Q, Q), dtype=bool))[None, None, :, :, None, None]
    decay_ij = jnp.exp(jnp.where(causal, seg, -jnp.inf))
    cb = jnp.einsum('bcign,bcjgn->bcijg', Cm, Bm)
    y_diag = jnp.einsum('bcijg,bcijge,bcjgep->bcigep', cb, decay_ij, xdt)
    decay_to_end = jnp.exp(a_cum[:, :, -1:] - a_cum)
    states = jnp.einsum('bcjgn,bcjge,bcjgep->bcgepn', Bm, decay_to_end, xdt)
    chunk_decay = jnp.exp(a_cum[:, :, -1])

    def step(S, inp):
        st, dec = inp
        return S * dec[..., None, None] + st, S

    S0 = jnp.zeros((Bsz, SSD_GROUPS, E, SSD_HEAD_DIM, SSD_STATE), f32)
    _, prev = lax.scan(step, S0, (jnp.moveaxis(states, 1, 0), jnp.moveaxis(chunk_decay, 1, 0)))
    prev = jnp.moveaxis(prev, 0, 1)
    y_off = jnp.einsum('bcign,bcgepn,bcige->bcigep', Cm, prev, jnp.exp(a_cum))
    return (y_diag + y_off).reshape(Bsz, L, SSD_HEADS, SSD_HEAD_DIM)


def ssd_mixer(cols, conv_w, conv_b, dt_bias, a_log, d_skip, norm_w):
    f32 = jnp.float32
    Bsz, L, _ = cols.shape
    z, xbc, dt_raw = _split(cols, SSD_SPLITS)
    xbc = jax.nn.silu(causal_depthwise_conv(xbc, conv_w, conv_b))
    xs, Bm, Cm = _split(xbc, (SSD_WIDTH, SSD_GROUPS * SSD_STATE, SSD_GROUPS * SSD_STATE))
    dt = jax.nn.softplus((dt_raw + dt_bias).astype(f32))
    A = -jnp.exp(a_log.astype(f32))
    xh = xs.reshape(Bsz, L, SSD_HEADS, SSD_HEAD_DIM)
    y = ssd_chunked(xh, dt, A,
                    Bm.reshape(Bsz, L, SSD_GROUPS, SSD_STATE), Cm.reshape(Bsz, L, SSD_GROUPS, SSD_STATE))
    y = y + d_skip.astype(f32)[:, None] * xh.astype(f32)
    y = y.reshape(Bsz, L, SSD_WIDTH) * jax.nn.silu(z.astype(f32))
    yg = y.reshape(Bsz, L, SSD_GROUPS, SSD_WIDTH // SSD_GROUPS)
    yg = yg * lax.rsqrt(jnp.mean(yg * yg, axis=-1, keepdims=True) + EPS)
    return (yg.reshape(Bsz, L, SSD_WIDTH) * norm_w.astype(f32)).astype(cols.dtype)


def fox_mixer(cols, f_bias, q_gain, k_gain):
    f32 = jnp.float32
    Bsz, L, _ = cols.shape
    q, k, v, f_logit = _split(cols, FOX_SPLITS)
    shp = (Bsz, L, FOX_HEADS, FOX_HEAD_DIM)
    q = rms_norm(q.reshape(shp), q_gain).astype(f32)
    k = rms_norm(k.reshape(shp), k_gain).astype(f32)
    v = v.reshape(shp).astype(f32)
    F = jnp.cumsum(jax.nn.log_sigmoid((f_logit + f_bias).astype(f32)), axis=1)
    Fk = jnp.transpose(F, (0, 2, 1))
    kpos = jnp.arange(L)
    scale = FOX_HEAD_DIM ** -0.5

    def block(i):
        start = i * FOX_BLOCK
        qb = lax.dynamic_slice_in_dim(q, start, FOX_BLOCK, axis=1)
        Fq = jnp.transpose(lax.dynamic_slice_in_dim(F, start, FOX_BLOCK, axis=1), (0, 2, 1))
        s = jnp.einsum('bqhd,bkhd->bhqk', qb, k) * scale + Fq[..., None] - Fk[:, :, None, :]
        qpos = start + jnp.arange(FOX_BLOCK)
        s = jnp.where(kpos[None, :] <= qpos[:, None], s, -jnp.inf)
        p = jax.nn.softmax(s, axis=-1)
        return jnp.einsum('bhqk,bkhd->bqhd', p, v)

    out = lax.map(block, jnp.arange(L // FOX_BLOCK))
    return jnp.moveaxis(out, 0, 1).reshape(Bsz, L, FOX_WIDTH).astype(cols.dtype)


def swiglu(h, w_gu, w_down):
    g, u = jnp.split(h @ w_gu, 2, axis=-1)
    return (jax.nn.silu(g) * u) @ w_down


def moe_swiglu(h, router, w_gu, w_down):
    logits = (h @ router).astype(jnp.float32)
    top_v, top_i = lax.top_k(logits, TOP_K)
    top_w = jax.nn.softmax(top_v, axis=-1)
    combine = jnp.sum(jax.nn.one_hot(top_i, N_EXPERTS, dtype=jnp.float32) * top_w[..., None], axis=-2)
    out = jnp.zeros_like(h)
    for e in range(N_EXPERTS):
        out = out + combine[..., e:e + 1].astype(h.dtype) * swiglu(h, w_gu[e], w_down[e])
    return out


def setup_inputs(seed: int = 0) -> dict:
    key = jax.random.key(seed)
    ks = iter(jax.random.split(key, 64))
    f32 = jnp.float32
    D = D_MODEL
    n_dense = (DEPTH + 1) // 2
    n_moe = DEPTH // 2
    n_vres = DEPTH - 1

    def nrm(shape, scale):
        return jax.random.normal(next(ks), shape, f32) * scale

    def uni(shape, lo, hi):
        return jax.random.uniform(next(ks), shape, f32, lo, hi)

    dt0 = jnp.exp(uni((DEPTH, SSD_HEADS), math.log(1e-3), math.log(1e-1)))
    return {
        'x': nrm((BATCH, SEQ, D), 1.0),
        'c': nrm((BATCH, D), 1.0),
        'ada_w': nrm((DEPTH, D, 6 * D), 0.5 * D ** -0.5),
        'ada_b': nrm((DEPTH, 6 * D), 0.02),
        'norm_mix': 1.0 + nrm((DEPTH, D), 0.05),
        'norm_ffn': 1.0 + nrm((DEPTH, D), 0.05),
        'w_in': nrm((DEPTH, D, IN_COLS), D ** -0.5),
        'rw_mu': uni((DEPTH, RW_COLS), 0.0, 1.0),
        'rw_w0': uni((DEPTH, RW_WIDTH), -6.0, -1.0),
        'rw_w_up': nrm((DEPTH, RW_DECAY_LORA, RW_WIDTH), 0.5 * RW_DECAY_LORA ** -0.5),
        'rw_a0': nrm((DEPTH, RW_WIDTH), 0.5),
        'rw_a_up': nrm((DEPTH, RW_AAA_LORA, RW_WIDTH), 0.5 * RW_AAA_LORA ** -0.5),
        'rw_g_up': nrm((DEPTH, RW_GATE_LORA, RW_WIDTH), RW_GATE_LORA ** -0.5),
        'rw_k_k': 0.85 + nrm((DEPTH, RW_WIDTH), 0.1),
        'rw_k_a': 1.0 + nrm((DEPTH, RW_WIDTH), 0.1),
        'rw_r_k': nrm((DEPTH, RW_HEADS, RW_HEAD_DIM), 0.1),
        'rw_lnx_w': 1.0 + nrm((DEPTH, RW_WIDTH), 0.05),
        'rw_lnx_b': nrm((DEPTH, RW_WIDTH), 0.02),
        'rw_v0': nrm((n_vres, RW_WIDTH), 0.5),
        'rw_v_down': nrm((n_vres, D, RW_VRES_LORA), D ** -0.5),
        'rw_v_up': nrm((n_vres, RW_VRES_LORA, RW_WIDTH), 0.5 * RW_VRES_LORA ** -0.5),
        'ssd_conv_w': nrm((DEPTH, SSD_CONV, SSD_XBC), SSD_CONV ** -0.5),
        'ssd_conv_b': nrm((DEPTH, SSD_XBC), 0.02),
        'ssd_dt_bias': dt0 + jnp.log(-jnp.expm1(-dt0)),
        'ssd_a_log': jnp.log(uni((DEPTH, SSD_HEADS), 1.0, 16.0)),
        'ssd_d': 1.0 + nrm((DEPTH, SSD_HEADS), 0.1),
        'ssd_norm': 1.0 + nrm((DEPTH, SSD_WIDTH), 0.05),
        'fox_f_bias': 2.0 + nrm((DEPTH, FOX_HEADS), 0.5),
        'fox_q_gain': 1.0 + nrm((DEPTH, FOX_HEAD_DIM), 0.05),
        'fox_k_gain': 1.0 + nrm((DEPTH, FOX_HEAD_DIM), 0.05),
        'gate_b': nrm((DEPTH, GATE_COLS), 0.02),
        'proj_rw': nrm((DEPTH, RW_WIDTH, D), RW_WIDTH ** -0.5),
        'proj_ssd': nrm((DEPTH, SSD_WIDTH, D), SSD_WIDTH ** -0.5),
        'proj_fox': nrm((DEPTH, FOX_WIDTH, D), FOX_WIDTH ** -0.5),
        'w_out': nrm((DEPTH, D, D), D ** -0.5),
        'ffn_w_gu': nrm((n_dense, D, 2 * FFN_DENSE), D ** -0.5),
        'ffn_w_down': nrm((n_dense, FFN_DENSE, D), FFN_DENSE ** -0.5),
        'moe_router': nrm((n_moe, D, N_EXPERTS), D ** -0.5),
        'moe_w_gu': nrm((n_moe, N_EXPERTS, D, 2 * FFN_EXPERT), D ** -0.5),
        'moe_w_down': nrm((n_moe, N_EXPERTS, FFN_EXPERT, D), FFN_EXPERT ** -0.5),
    }


def reference(x, c, ada_w, ada_b, norm_mix, norm_ffn, w_in, rw_mu, rw_w0, rw_w_up, rw_a0, rw_a_up,
              rw_g_up, rw_k_k, rw_k_a, rw_r_k, rw_lnx_w, rw_lnx_b, rw_v0, rw_v_down, rw_v_up,
              ssd_conv_w, ssd_conv_b, ssd_dt_bias, ssd_a_log, ssd_d, ssd_norm,
              fox_f_bias, fox_q_gain, fox_k_gain, gate_b, proj_rw, proj_ssd, proj_fox, w_out,
              ffn_w_gu, ffn_w_down, moe_router, moe_w_gu, moe_w_down):
    v_first = None
    for l in range(DEPTH):
        mod = jax.nn.silu(c) @ ada_w[l] + ada_b[l]
        sh_m, sc_m, g_m, sh_f, sc_f, g_f = jnp.split(mod[:, None, :], 6, axis=-1)
        h = rms_norm(x, norm_mix[l]) * (1 + sc_m) + sh_m
        proj = h @ w_in[l]
        rw_cols, ssd_cols, fox_cols, gate_logits = _split(proj, IN_SPLITS)
        v_res = None if l == 0 else (rw_v0[l - 1], rw_v_down[l - 1], rw_v_up[l - 1])
        y_rw, v_first = rwkv7_mixer(h, rw_cols, rw_mu[l], rw_w0[l], rw_w_up[l], rw_a0[l], rw_a_up[l],
                                    rw_g_up[l], rw_k_k[l], rw_k_a[l], rw_r_k[l], rw_lnx_w[l], rw_lnx_b[l],
                                    v_first, v_res)
        y_ssd = ssd_mixer(ssd_cols, ssd_conv_w[l], ssd_conv_b[l], ssd_dt_bias[l], ssd_a_log[l], ssd_d[l],
                          ssd_norm[l])
        y_fox = fox_mixer(fox_cols, fox_f_bias[l], fox_q_gain[l], fox_k_gain[l])
        g_rw, g_ssd, g_fox = jnp.split(jax.nn.sigmoid(gate_logits + gate_b[l]), N_BRANCH, axis=-1)
        merged = g_rw * (y_rw @ proj_rw[l]) + g_ssd * (y_ssd @ proj_ssd[l]) + g_fox * (y_fox @ proj_fox[l])
        x = x + g_m * (merged @ w_out[l])
        h = rms_norm(x, norm_ffn[l]) * (1 + sc_f) + sh_f
        if l % 2 == 0:
            f = swiglu(h, ffn_w_gu[l // 2], ffn_w_down[l // 2])
        else:
            f = moe_swiglu(h, moe_router[l // 2], moe_w_gu[l // 2], moe_w_down[l // 2])
        x = x + g_f * f
    return x
```

```python
import functools
import math

import jax
import jax.numpy as jnp
from jax import lax
from jax.experimental import pallas as pl
from jax.experimental.pallas import tpu as pltpu

F32 = jnp.float32
BF16 = jnp.bfloat16
HIGHEST = lax.Precision.HIGHEST

D_MODEL = 1024
LANES = 128
SUBLANES = 8

RW_HEADS = 8
RW_HEAD_DIM = 64
RW_WIDTH = RW_HEADS * RW_HEAD_DIM
RW_DECAY_LORA = 64
RW_AAA_LORA = 64
RW_VRES_LORA = 32
RW_GATE_LORA = 160
GN_EPS = 64e-5
RW_OFF_WLO = 3 * RW_WIDTH
RW_OFF_ALO = RW_OFF_WLO + LANES
RW_OFF_GLO = RW_OFF_ALO + LANES
RW_OFF_HV = RW_OFF_GLO + 2 * LANES
RW_PAD_COLS = RW_OFF_HV + LANES
RW_CHUNK = 64

SSD_HEADS = 16
SSD_HEAD_DIM = 64
SSD_WIDTH = SSD_HEADS * SSD_HEAD_DIM
SSD_GROUPS = 4
SSD_STATE = 128
SSD_CONV = 4
SSD_CHUNK = 128
SSD_XBC = SSD_WIDTH + 2 * SSD_GROUPS * SSD_STATE
SSD_GROUP_WIDTH = SSD_WIDTH // SSD_GROUPS

FOX_HEADS = 8
FOX_HEAD_DIM = 64
FOX_WIDTH = FOX_HEADS * FOX_HEAD_DIM

FFN_DENSE = 2816
N_EXPERTS = 8
FFN_EXPERT = 3584
EPS = 1e-6

VMEM_LIMIT = 56 * 1024 * 1024


def _cparams(*sem):
    return pltpu.CompilerParams(dimension_semantics=sem, vmem_limit_bytes=VMEM_LIMIT)


def _sigmoid(x):
    return 1.0 / (1.0 + jnp.exp(-x))


def _softplus(x):
    return jnp.maximum(x, 0.0) + jnp.log(1.0 + jnp.exp(-jnp.abs(x)))


def _dot(a, b):
    return jnp.dot(a, b, preferred_element_type=F32)


def _dot_bf16(a, b):
    return jnp.dot(a.astype(BF16), b.astype(BF16), preferred_element_type=F32)


def _dot_hi(a, b):
    return jnp.dot(a, b, precision=HIGHEST, preferred_element_type=F32)


def _dot_nt(a, b, precision=None):
    return lax.dot_general(a, b, (((1,), (1,)), ((), ())), precision=precision, preferred_element_type=F32)


def _dot_tn(a, b, precision=None):
    return lax.dot_general(a, b, (((0,), (0,)), ((), ())), precision=precision, preferred_element_type=F32)


def _split_dot(x, ones_bf16):
    hi = x.astype(BF16)
    lo = (x - hi.astype(F32)).astype(BF16)
    return _dot(hi, ones_bf16) + _dot(lo, ones_bf16)


def _tril(n, strict=False):
    r = lax.broadcasted_iota(jnp.int32, (n, n), 0)
    c = lax.broadcasted_iota(jnp.int32, (n, n), 1)
    return (r > c) if strict else (r >= c)


def _mod_kernel(c_ref, w_ref, b_ref, o_ref):
    c = c_ref[...]
    o_ref[...] = _dot_hi(c * _sigmoid(c), w_ref[...]) + b_ref[...]


def adaln_mod(c, w, b):
    d, n = w.shape
    tn = 1024
    c8 = jnp.broadcast_to(c, (SUBLANES, d))
    out = pl.pallas_call(
        _mod_kernel,
        grid=(n // tn,),
        in_specs=[pl.BlockSpec((SUBLANES, d), lambda j: (0, 0)),
                  pl.BlockSpec((d, tn), lambda j: (0, j)),
                  pl.BlockSpec((1, tn), lambda j: (0, j))],
        out_specs=pl.BlockSpec((SUBLANES, tn), lambda j: (0, j)),
        out_shape=jax.ShapeDtypeStruct((SUBLANES, n), F32),
        compiler_params=_cparams("arbitrary"),
    )(c8, w, b.reshape(1, n))
    return out[:1]


def _norm_mod(x, gain, sc, sh):
    y = x * lax.rsqrt(jnp.mean(x * x, axis=-1, keepdims=True) + EPS)
    return y * gain * (1.0 + sc) + sh


def _norm_kernel(x_ref, gain_ref, sc_ref, sh_ref, h_ref):
    h_ref[...] = _norm_mod(x_ref[...], gain_ref[...], sc_ref[...], sh_ref[...]).astype(h_ref.dtype)


def norm_mod(x, gain, sc, sh, tm=512):
    L, d = x.shape
    row = pl.BlockSpec((1, d), lambda i: (0, 0))
    return pl.pallas_call(
        _norm_kernel,
        grid=(L // tm,),
        in_specs=[pl.BlockSpec((tm, d), lambda i: (i, 0)), row, row, row],
        out_specs=pl.BlockSpec((tm, d), lambda i: (i, 0)),
        out_shape=jax.ShapeDtypeStruct((L, d), BF16),
        compiler_params=_cparams("arbitrary"),
    )(x, gain, sc, sh)


def _mm_kernel(x_ref, w_ref, o_ref):
    o_ref[...] = _dot(x_ref[...], w_ref[...]).astype(o_ref.dtype)


def matmul(x, w, out_dtype=F32, tm=512):
    M, K = x.shape
    N = w.shape[1]
    return pl.pallas_call(
        _mm_kernel,
        grid=(M // tm,),
        in_specs=[pl.BlockSpec((tm, K), lambda i: (i, 0)),
                  pl.BlockSpec((K, N), lambda i: (0, 0))],
        out_specs=pl.BlockSpec((tm, N), lambda i: (i, 0)),
        out_shape=jax.ShapeDtypeStruct((M, N), out_dtype),
        compiler_params=_cparams("arbitrary"),
    )(x, w)


def _rw_pre_kernel(has_vres, *refs):
    if has_vres:
        (cols_ref, mu_ref, w0_ref, wup_ref, a0_ref, aup_ref, gup_ref, kk_ref, ka_ref, rk_ref, seg_ref,
         vfirst_ref, v0_ref, vup_ref,
         r_out, k_out, v_out, lw_out, a_out, b_out, g_out, bonus_out, ext_scr) = refs
    else:
        (cols_ref, mu_ref, w0_ref, wup_ref, a0_ref, aup_ref, gup_ref, kk_ref, ka_ref, rk_ref, seg_ref,
         r_out, k_out, v_out, lw_out, a_out, b_out, g_out, bonus_out, ext_scr) = refs
    T = cols_ref.shape[0]

    @pl.when(pl.program_id(0) == 0)
    def _():
        ext_scr[0:SUBLANES, :] = jnp.zeros((SUBLANES, ext_scr.shape[1]), F32)

    cur = cols_ref[...]
    ext_scr[SUBLANES:SUBLANES + T, :] = cur
    prev = ext_scr[SUBLANES - 1:SUBLANES - 1 + T, :]
    ext_scr[0:SUBLANES, :] = cur[T - SUBLANES:T, :]
    s = cur + (prev - cur) * mu_ref[...]

    W = RW_WIDTH
    r = s[:, 0:W]
    k = s[:, W:2 * W]
    v = s[:, 2 * W:3 * W]
    w_lo = s[:, RW_OFF_WLO:RW_OFF_WLO + LANES]
    a_lo = s[:, RW_OFF_ALO:RW_OFF_ALO + LANES]
    g_lo = s[:, RW_OFF_GLO:RW_OFF_GLO + 2 * LANES]
    seg = seg_ref[...]

    wlog = -_softplus(-(w0_ref[...] + _dot_bf16(jnp.tanh(w_lo), wup_ref[...]))) - 0.5
    a = _sigmoid(a0_ref[...] + _dot_bf16(a_lo, aup_ref[...]))
    g = _dot_bf16(_sigmoid(g_lo), gup_ref[...])
    if has_vres:
        hv = s[:, RW_OFF_HV:RW_OFF_HV + LANES]
        v = v + (vfirst_ref[...] - v) * _sigmoid(v0_ref[...] + _dot_bf16(hv, vup_ref[...]))
    kk = k * kk_ref[...]
    kk = kk / jnp.maximum(jnp.sqrt(_split_dot(kk * kk, seg)), 1e-12)
    k = k * (1.0 + (a - 1.0) * ka_ref[...])
    r_out[...] = r
    k_out[...] = k
    v_out[...] = v
    lw_out[...] = -jnp.exp(wlog)
    a_out[...] = -kk
    b_out[...] = kk * a
    g_out[...] = g
    bonus_out[...] = _split_dot(r * k * rk_ref[...], seg) * v


def rw_pre(cols, p, v_first, tm=256):
    L = cols.shape[0]
    has_vres = v_first is not None
    full = lambda a: pl.BlockSpec(a.shape, lambda i: (0,) * a.ndim)
    rows = lambda n: pl.BlockSpec((tm, n), lambda i: (i, 0))
    args = [cols, p["mu"], p["w0"], p["w_up"], p["a0"], p["a_up"], p["g_up"], p["k_k"], p["k_a"], p["r_k"],
            p["seg"]]
    specs = [rows(RW_PAD_COLS)] + [full(a) for a in args[1:]]
    if has_vres:
        args += [v_first, p["v0"], p["v_up"]]
        specs += [rows(RW_WIDTH), full(p["v0"]), full(p["v_up"])]
    return pl.pallas_call(
        functools.partial(_rw_pre_kernel, has_vres),
        grid=(L // tm,),
        in_specs=specs,
        out_specs=[rows(RW_WIDTH)] * 8,
        out_shape=[jax.ShapeDtypeStruct((L, RW_WIDTH), F32)] * 8,
        scratch_shapes=[pltpu.VMEM((tm + SUBLANES, RW_PAD_COLS), F32)],
        compiler_params=_cparams("arbitrary"),
    )(*args)


def _wkv_kernel(r_ref, lw_ref, k_ref, v_ref, a_ref, b_ref, y_ref, h_scr):
    C = RW_CHUNK
    n_chunks = r_ref.shape[1] // C

    @pl.when(pl.program_id(1) == 0)
    def _():
        h_scr[...] = jnp.zeros_like(h_scr)

    incl = _tril(C)
    strict = _tril(C, strict=True)
    tri_f = incl.astype(F32)
    eye = (lax.broadcasted_iota(jnp.int32, (C, C), 0) == lax.broadcasted_iota(jnp.int32, (C, C), 1)).astype(F32)
    eye_k = eye if C == RW_HEAD_DIM else None

    for c in range(n_chunks):
        sl = pl.ds(c * C, C)
        r = r_ref[0, sl, :]
        lw = lw_ref[0, sl, :]
        k = k_ref[0, sl, :]
        v = v_ref[0, sl, :]
        a = a_ref[0, sl, :]
        b = b_ref[0, sl, :]
        cs = _dot_hi(tri_f, lw)
        cs_end = cs[C - 1:C, :]
        e_neg = jnp.exp(-cs)
        e_end = jnp.exp(cs_end - cs)
        at = a * jnp.exp(cs - lw)
        rt = r * jnp.exp(cs)
        bt = b * e_neg
        kt = k * e_neg
        bh = b * e_end
        kh = k * e_end
        n_ab = jnp.where(strict, _dot_nt(at, bt, HIGHEST), 0.0)
        a_ak = jnp.where(strict, _dot_nt(at, kt, HIGHEST), 0.0)
        m_rb = jnp.where(incl, _dot_nt(rt, bt, HIGHEST), 0.0)
        m_rk = jnp.where(incl, _dot_nt(rt, kt, HIGHEST), 0.0)
        t_inv = eye + n_ab
        n_pow = n_ab
        for _ in range(int(math.log2(C)) - 1):
            n_pow = _dot_hi(n_pow, n_pow)
            t_inv = t_inv + _dot_hi(t_inv, n_pow)
        ah = _dot_hi(t_inv, at)
        u_loc = _dot_hi(t_inv, _dot_hi(a_ak, v))
        rh = rt + _dot_hi(m_rb, ah)
        y_loc = _dot_hi(m_rb, u_loc) + _dot_hi(m_rk, v)
        g_mat = eye_k * jnp.exp(cs_end) + _dot_tn(bh, ah, HIGHEST)
        h_loc = _dot_tn(bh, u_loc, HIGHEST) + _dot_tn(kh, v, HIGHEST)
        h0 = h_scr[...]
        y = _dot_hi(rh, h0) + y_loc
        h_scr[...] = _dot_hi(g_mat, h0) + h_loc
        mean = jnp.mean(y, axis=-1, keepdims=True)
        yc = y - mean
        var = jnp.mean(yc * yc, axis=-1, keepdims=True)
        y_ref[0, sl, :] = yc * lax.rsqrt(var + GN_EPS)


def wkv7(r, lw, k, v, a, b, rows=512):
    H, L, N = r.shape
    spec = pl.BlockSpec((1, rows, N), lambda h, i: (h, i, 0))
    return pl.pallas_call(
        _wkv_kernel,
        grid=(H, L // rows),
        in_specs=[spec] * 6,
        out_specs=spec,
        out_shape=jax.ShapeDtypeStruct((H, L, N), F32),
        scratch_shapes=[pltpu.VMEM((N, N), F32)],
        compiler_params=_cparams("arbitrary", "arbitrary"),
    )(r, lw, k, v, a, b)


def _ssd_conv_kernel(x_ref, w_ref, b_ref, o_ref, ext_scr):
    T = x_ref.shape[0]

    @pl.when(pl.program_id(0) == 0)
    def _():
        ext_scr[0:SUBLANES, :] = jnp.zeros((SUBLANES, ext_scr.shape[1]), F32)

    cur = x_ref[...]
    ext_scr[SUBLANES:SUBLANES + T, :] = cur
    acc = cur * w_ref[SSD_CONV - 1:SSD_CONV, :] + b_ref[...]
    for j in range(1, SSD_CONV):
        acc = acc + ext_scr[SUBLANES - j:SUBLANES - j + T, :] * w_ref[SSD_CONV - 1 - j:SSD_CONV - j, :]
    ext_scr[0:SUBLANES, :] = cur[T - SUBLANES:T, :]
    o_ref[...] = acc * _sigmoid(acc)


def ssd_conv(ssd_cols, conv_w, conv_b, tm=512):
    L = ssd_cols.shape[0]
    return pl.pallas_call(
        _ssd_conv_kernel,
        grid=(L // tm,),
        in_specs=[pl.BlockSpec((tm, SSD_XBC), lambda i: (i, 0)),
                  pl.BlockSpec((SSD_CONV, SSD_XBC), lambda i: (0, 0)),
                  pl.BlockSpec((1, SSD_XBC), lambda i: (0, 0))],
        out_specs=pl.BlockSpec((tm, SSD_XBC), lambda i: (i, 0)),
        out_shape=jax.ShapeDtypeStruct((L, SSD_XBC), F32),
        scratch_shapes=[pltpu.VMEM((tm + SUBLANES, SSD_XBC), F32)],
        compiler_params=_cparams("arbitrary"),
    )(ssd_cols, conv_w, conv_b)


def _ssd_kernel(xbc_ref, z_ref, dt_ref, dtT_ref, bias_ref, biasT_ref, A_ref, AT_ref, exp_ref, dskip_ref,
                nw_ref, o_ref, s_scr):
    Q = SSD_CHUNK
    GW = SSD_GROUP_WIDTH
    NS = SSD_STATE

    @pl.when(pl.program_id(0) == 0)
    def _():
        s_scr[...] = jnp.zeros_like(s_scr)

    incl = _tril(Q)
    tri_f = incl.astype(F32)
    xs = xbc_ref[:, 0:SSD_WIDTH]
    dt = _softplus(dt_ref[...] + bias_ref[...])
    a = dt * A_ref[...]
    a_cum = _dot_hi(tri_f, a)
    a_end = a_cum[Q - 1:Q, :]
    aT = _softplus(dtT_ref[...] + biasT_ref[...]) * AT_ref[...]
    a_cumT = _dot_nt(aT, tri_f, HIGHEST)
    expand = exp_ref[...]
    dt_x = _dot_hi(dt, expand)
    eac_x = _dot_hi(jnp.exp(a_cum), expand)
    dte_x = _dot_hi(jnp.exp(a_end - a_cum), expand)
    xdt = xs * dt_x
    lane = lax.broadcasted_iota(jnp.int32, (Q, LANES), 1)
    first_half = lane < SSD_HEAD_DIM
    heads_per_group = SSD_HEADS // SSD_GROUPS
    z = z_ref[...]
    for g in range(SSD_GROUPS):
        Bg = xbc_ref[:, SSD_WIDTH + g * NS:SSD_WIDTH + (g + 1) * NS]
        Cg = xbc_ref[:, SSD_WIDTH + SSD_GROUPS * NS + g * NS:SSD_WIDTH + SSD_GROUPS * NS + (g + 1) * NS]
        cb = _dot_nt(Cg.astype(BF16), Bg.astype(BF16))
        gs = slice(g * GW, (g + 1) * GW)
        s_prev = s_scr[:, gs]
        y_g = _dot_bf16(Cg, s_prev) * eac_x[:, gs]
        pieces = []
        for pr in range(heads_per_group // 2):
            ps = slice(g * GW + pr * LANES, g * GW + (pr + 1) * LANES)
            xdt_p = xdt[:, ps]
            acc = None
            for e in range(2):
                h = g * heads_per_group + pr * 2 + e
                seg = a_cum[:, h:h + 1] - a_cumT[h:h + 1, :]
                m = cb * jnp.exp(jnp.where(incl, seg, -jnp.inf))
                xm = jnp.where(first_half if e == 0 else jnp.logical_not(first_half), xdt_p, 0.0)
                t = _dot_bf16(m, xm)
                acc = t if acc is None else acc + t
            pieces.append(acc)
        y_g = y_g + jnp.concatenate(pieces, axis=1)
        s_scr[:, gs] = s_prev * eac_x[Q - 1:Q, gs] + _dot_tn(Bg.astype(BF16), (dte_x[:, gs] * xdt[:, gs]).astype(BF16))
        y_g = y_g + dskip_ref[:, gs] * xs[:, gs]
        zg = z[:, gs]
        y_g = y_g * (zg * _sigmoid(zg))
        y_g = y_g * lax.rsqrt(jnp.mean(y_g * y_g, axis=-1, keepdims=True) + EPS)
        o_ref[:, gs] = y_g * nw_ref[:, gs]


def ssd_scan(xbc, ssd_cols, dtT, p):
    L = xbc.shape[0]
    Q = SSD_CHUNK
    full = lambda a: pl.BlockSpec(a.shape, lambda i: (0,) * a.ndim)
    z_blk = SSD_XBC // SSD_WIDTH
    dt_blk = (SSD_XBC + SSD_WIDTH) // LANES
    small = [p["dt_bias"], p["dt_biasT"], p["A"], p["AT"], p["expand"], p["d_skip"], p["norm_w"]]
    return pl.pallas_call(
        _ssd_kernel,
        grid=(L // Q,),
        in_specs=[pl.BlockSpec((Q, SSD_XBC), lambda i: (i, 0)),
                  pl.BlockSpec((Q, SSD_WIDTH), lambda i: (i, z_blk)),
                  pl.BlockSpec((Q, LANES), lambda i: (i, dt_blk)),
                  pl.BlockSpec((SSD_HEADS, Q), lambda i: (0, i))] + [full(a) for a in small],
        out_specs=pl.BlockSpec((Q, SSD_WIDTH), lambda i: (i, 0)),
        out_shape=jax.ShapeDtypeStruct((L, SSD_WIDTH), F32),
        scratch_shapes=[pltpu.VMEM((SSD_STATE, SSD_WIDTH), F32)],
        compiler_params=_cparams("arbitrary"),
    )(xbc, ssd_cols, ssd_cols, dtT, *small)


def _fox_pre_kernel(cols_ref, fb_ref, qg_ref, kg_ref, seg_ref, q_out, k_out, v_out, f_out, carry_scr):
    T = cols_ref.shape[0]
    W = FOX_WIDTH

    @pl.when(pl.program_id(0) == 0)
    def _():
        carry_scr[...] = jnp.zeros_like(carry_scr)

    seg = seg_ref[...]
    q = cols_ref[:, 0:W]
    k = cols_ref[:, W:2 * W]
    inv_d = 1.0 / FOX_HEAD_DIM
    qn = q * lax.rsqrt(_split_dot(q * q, seg) * inv_d + EPS) * qg_ref[...]
    kn = k * lax.rsqrt(_split_dot(k * k, seg) * inv_d + EPS) * kg_ref[...]
    q_out[...] = qn.astype(q_out.dtype)
    k_out[...] = kn.astype(k_out.dtype)
    v_out[...] = cols_ref[:, 2 * W:3 * W].astype(v_out.dtype)
    f = cols_ref[:, 3 * W:3 * W + LANES] + fb_ref[...]
    ls = -_softplus(-f)
    cum = _dot_hi(_tril(T).astype(F32), ls) + carry_scr[0:1, :]
    f_out[...] = cum
    carry_scr[...] = jnp.broadcast_to(cum[T - 1:T, :], carry_scr.shape)


def fox_pre(cols, p, tm=512):
    L = cols.shape[0]
    full = lambda a: pl.BlockSpec(a.shape, lambda i: (0,) * a.ndim)
    rows = lambda n: pl.BlockSpec((tm, n), lambda i: (i, 0))
    small = [p["f_bias"], p["q_gain"], p["k_gain"], p["seg"]]
    return pl.pallas_call(
        _fox_pre_kernel,
        grid=(L // tm,),
        in_specs=[rows(cols.shape[1])] + [full(a) for a in small],
        out_specs=[rows(FOX_WIDTH)] * 3 + [rows(LANES)],
        out_shape=[jax.ShapeDtypeStruct((L, FOX_WIDTH), BF16)] * 3 + [jax.ShapeDtypeStruct((L, LANES), F32)],
        scratch_shapes=[pltpu.VMEM((SUBLANES, LANES), F32)],
        compiler_params=_cparams("arbitrary"),
    )(cols, *small)


def _fox_kernel(q_ref, k_ref, v_ref, fq_ref, fk_ref, o_ref, m_scr, l_scr, acc_scr):
    tq = q_ref.shape[1]
    h = pl.program_id(0)
    qi = pl.program_id(1)
    lane = lax.broadcasted_iota(jnp.int32, (tq, LANES), 1)
    fq = jnp.sum(jnp.where(lane == h, fq_ref[...], 0.0), axis=-1, keepdims=True)
    q = q_ref[0]
    m_scr[...] = jnp.full(m_scr.shape, -jnp.inf, F32)
    l_scr[...] = jnp.zeros_like(l_scr)
    acc_scr[...] = jnp.zeros_like(acc_scr)
    rel = lax.broadcasted_iota(jnp.int32, (tq, tq), 0) - lax.broadcasted_iota(jnp.int32, (tq, tq), 1)

    def body(kb, carry):
        ks = pl.ds(pl.multiple_of(kb * tq, tq), tq)
        s = _dot_nt(q, k_ref[0, ks, :]) + fq - fk_ref[pl.ds(h, 1), ks]
        s = jnp.where(rel >= (kb - qi) * tq, s, -jnp.inf)
        m_old = m_scr[...]
        m_new = jnp.maximum(m_old, jnp.max(s, axis=-1, keepdims=True))
        pexp = jnp.exp(s - m_new)
        alpha = jnp.exp(m_old - m_new)
        l_scr[...] = alpha * l_scr[...] + jnp.sum(pexp, axis=-1, keepdims=True)
        acc_scr[...] = alpha * acc_scr[...] + _dot(pexp.astype(BF16), v_ref[0, ks, :])
        m_scr[...] = m_new
        return carry

    lax.fori_loop(0, qi + 1, body, 0)
    o_ref[0] = acc_scr[...] / l_scr[...]


def fox_attention(q, k, v, F, FT, tq=512):
    H, L, N = q.shape
    return pl.pallas_call(
        _fox_kernel,
        grid=(H, L // tq),
        in_specs=[pl.BlockSpec((1, tq, N), lambda h, i: (h, i, 0)),
                  pl.BlockSpec((1, L, N), lambda h, i: (h, 0, 0)),
                  pl.BlockSpec((1, L, N), lambda h, i: (h, 0, 0)),
                  pl.BlockSpec((tq, LANES), lambda h, i: (i, 0)),
                  pl.BlockSpec((H, L), lambda h, i: (0, 0))],
        out_specs=pl.BlockSpec((1, tq, N), lambda h, i: (h, i, 0)),
        out_shape=jax.ShapeDtypeStruct((H, L, N), F32),
        scratch_shapes=[pltpu.VMEM((tq, 1), F32), pltpu.VMEM((tq, 1), F32), pltpu.VMEM((tq, N), F32)],
        compiler_params=_cparams("arbitrary", "arbitrary"),
    )(q, k, v, F, FT)


def _merge_kernel(x_ref, yn_ref, bonus_ref, g_ref, yssd_ref, yfox_ref, gl_ref, gb_ref, lnw_ref, lnb_ref,
                  prw_ref, pssd_ref, pfox_ref, wout_ref, gm_ref, o_ref):
    D = D_MODEL
    y_rw = (yn_ref[...] * lnw_ref[...] + lnb_ref[...] + bonus_ref[...]) * g_ref[...]
    gates = _sigmoid(gl_ref[...] + gb_ref[...])
    merged = gates[:, 0:D] * _dot_bf16(y_rw, prw_ref[...])
    merged = merged + gates[:, D:2 * D] * _dot_bf16(yssd_ref[...], pssd_ref[...])
    merged = merged + gates[:, 2 * D:3 * D] * _dot_bf16(yfox_ref[...], pfox_ref[...])
    o_ref[...] = x_ref[...] + gm_ref[...] * _dot_bf16(merged, wout_ref[...])


def merge(x, yn, bonus, g, y_ssd, y_fox, gate_logits, p, tm=256):
    L, D = x.shape
    full = lambda a: pl.BlockSpec(a.shape, lambda i: (0,) * a.ndim)
    rows = lambda n: pl.BlockSpec((tm, n), lambda i: (i, 0))
    small = [p["gate_b"], p["lnx_w"], p["lnx_b"], p["proj_rw"], p["proj_ssd"], p["proj_fox"], p["w_out"], p["g_m"]]
    return pl.pallas_call(
        _merge_kernel,
        grid=(L // tm,),
        in_specs=[rows(D), rows(RW_WIDTH), rows(RW_WIDTH), rows(RW_WIDTH), rows(SSD_WIDTH), rows(FOX_WIDTH),
                  rows(3 * D)] + [full(a) for a in small],
        out_specs=rows(D),
        out_shape=jax.ShapeDtypeStruct((L, D), F32),
        compiler_params=_cparams("arbitrary"),
    )(x, yn, bonus, g, y_ssd, y_fox, gate_logits, *small)


def _ffn_kernel(h_ref, x_ref, wg_ref, wu_ref, wd_ref, gf_ref, o_ref, acc_scr):
    j = pl.program_id(1)

    @pl.when(j == 0)
    def _():
        acc_scr[...] = jnp.zeros_like(acc_scr)

    h = h_ref[...]
    gte = _dot(h, wg_ref[...])
    up = _dot(h, wu_ref[...])
    act = gte * _sigmoid(gte) * up
    acc_scr[...] += _dot(act.astype(BF16), wd_ref[...])

    @pl.when(j == pl.num_programs(1) - 1)
    def _():
        o_ref[...] = x_ref[...] + gf_ref[...] * acc_scr[...]


def ffn_dense(h, x, w_gu, w_down, g_f, tm=1024, tf=256):
    L, D = x.shape
    Fh = w_down.shape[0]
    nf = Fh // tf
    return pl.pallas_call(
        _ffn_kernel,
        grid=(L // tm, nf),
        in_specs=[pl.BlockSpec((tm, D), lambda i, j: (i, 0)),
                  pl.BlockSpec((tm, D), lambda i, j: (i, 0)),
                  pl.BlockSpec((D, tf), lambda i, j: (0, j)),
                  pl.BlockSpec((D, tf), lambda i, j: (0, j + nf)),
                  pl.BlockSpec((tf, D), lambda i, j: (j, 0)),
                  pl.BlockSpec((1, D), lambda i, j: (0, 0))],
        out_specs=pl.BlockSpec((tm, D), lambda i, j: (i, 0)),
        out_shape=jax.ShapeDtypeStruct((L, D), F32),
        scratch_shapes=[pltpu.VMEM((tm, D), F32)],
        compiler_params=_cparams("arbitrary", "arbitrary"),
    )(h, x, w_gu, w_gu, w_down, g_f)


def _router_kernel(x_ref, gain_ref, sc_ref, sh_ref, wr_ref, h_ref, comb_ref):
    h = _norm_mod(x_ref[...], gain_ref[...], sc_ref[...], sh_ref[...])
    h_ref[...] = h.astype(h_ref.dtype)
    T = h.shape[0]
    lane = lax.broadcasted_iota(jnp.int32, (T, LANES), 1)
    logits = jnp.where(lane < N_EXPERTS, _dot_hi(h, wr_ref[...]), -jnp.inf)
    m1 = jnp.max(logits, axis=-1, keepdims=True)
    i1 = jnp.min(jnp.where(logits == m1, lane, LANES), axis=-1, keepdims=True)
    rest = jnp.where(lane == i1, -jnp.inf, logits)
    m2 = jnp.max(rest, axis=-1, keepdims=True)
    i2 = jnp.min(jnp.where(rest == m2, lane, LANES), axis=-1, keepdims=True)
    e2 = jnp.exp(m2 - m1)
    w1 = 1.0 / (1.0 + e2)
    w2 = e2 / (1.0 + e2)
    comb_ref[...] = jnp.where(lane == i1, w1, 0.0) + jnp.where(lane == i2, w2, 0.0)


def norm_router(x, gain, sc, sh, router_pad, tm=512):
    L, d = x.shape
    row = pl.BlockSpec((1, d), lambda i: (0, 0))
    return pl.pallas_call(
        _router_kernel,
        grid=(L // tm,),
        in_specs=[pl.BlockSpec((tm, d), lambda i: (i, 0)), row, row, row,
                  pl.BlockSpec((d, LANES), lambda i: (0, 0))],
        out_specs=[pl.BlockSpec((tm, d), lambda i: (i, 0)), pl.BlockSpec((tm, LANES), lambda i: (i, 0))],
        out_shape=[jax.ShapeDtypeStruct((L, d), BF16), jax.ShapeDtypeStruct((L, LANES), F32)],
        compiler_params=_cparams("arbitrary"),
    )(x, gain, sc, sh, router_pad)


def _moe_kernel(h_ref, x_ref, comb_ref, wg_ref, wu_ref, wd_ref, gf_ref, o_ref, acc_scr):
    e = pl.program_id(1)
    j = pl.program_id(2)

    @pl.when(jnp.logical_and(e == 0, j == 0))
    def _():
        acc_scr[...] = jnp.zeros_like(acc_scr)

    h = h_ref[...]
    lane = lax.broadcasted_iota(jnp.int32, comb_ref.shape, 1)
    ce = jnp.sum(jnp.where(lane == e, comb_ref[...], 0.0), axis=-1, keepdims=True)
    gte = _dot(h, wg_ref[0])
    up = _dot(h, wu_ref[0])
    act = gte * _sigmoid(gte) * up
    acc_scr[...] += ce * _dot(act.astype(BF16), wd_ref[0])

    @pl.when(jnp.logical_and(e == pl.num_programs(1) - 1, j == pl.num_programs(2) - 1))
    def _():
        o_ref[...] = x_ref[...] + gf_ref[...] * acc_scr[...]


def ffn_moe(h, x, comb, w_gu, w_down, g_f, tm=1024, tf=256):
    L, D = x.shape
    E, Fh, _ = w_down.shape
    nf = Fh // tf
    return pl.pallas_call(
        _moe_kernel,
        grid=(L // tm, E, nf),
        in_specs=[pl.BlockSpec((tm, D), lambda i, e, j: (i, 0)),
                  pl.BlockSpec((tm, D), lambda i, e, j: (i, 0)),
                  pl.BlockSpec((tm, LANES), lambda i, e, j: (i, 0)),
                  pl.BlockSpec((1, D, tf), lambda i, e, j: (e, 0, j)),
                  pl.BlockSpec((1, D, tf), lambda i, e, j: (e, 0, j + nf)),
                  pl.BlockSpec((1, tf, D), lambda i, e, j: (e, j, 0)),
                  pl.BlockSpec((1, D), lambda i, e, j: (0, 0))],
        out_specs=pl.BlockSpec((tm, D), lambda i, e, j: (i, 0)),
        out_shape=jax.ShapeDtypeStruct((L, D), F32),
        scratch_shapes=[pltpu.VMEM((tm, D), F32)],
        compiler_params=_cparams("arbitrary", "arbitrary", "arbitrary"),
    )(h, x, comb, w_gu, w_gu, w_down, g_f)


def _seg_matrix(width, head_dim):
    idx = jnp.arange(width) // head_dim
    return (idx[:, None] == idx[None, :]).astype(BF16)


def _pad_cols(w, n):
    return jnp.pad(w, ((0, 0), (0, n - w.shape[1])))


def _pad_rows(w, n):
    return jnp.pad(w, ((0, n - w.shape[0]), (0, 0)))


def _to_heads(t, heads):
    L = t.shape[0]
    return jnp.transpose(t.reshape(L, heads, t.shape[1] // heads), (1, 0, 2))


def _from_heads(t):
    H, L, N = t.shape
    return jnp.transpose(t, (1, 0, 2)).reshape(L, H * N)


def rwkv_branch(h, w_rw, p, v_first):
    cols = matmul(h, w_rw)
    r, k, v, lw, a, b, g, bonus = rw_pre(cols, p, v_first)
    hm = lambda t: _to_heads(t, RW_HEADS)
    yn = _from_heads(wkv7(hm(r), hm(lw), hm(k), hm(v), hm(a), hm(b)))
    return yn, bonus, g, v


def ssd_branch(h, w_ssd, p):
    cols = matmul(h, w_ssd)
    xbc = ssd_conv(cols, p["conv_w"], p["conv_b"])
    dtT = jnp.transpose(cols[:, SSD_XBC + SSD_WIDTH:SSD_XBC + SSD_WIDTH + SSD_HEADS])
    return ssd_scan(xbc, cols, dtT, p)


def fox_branch(h, w_fox, p):
    cols = matmul(h, w_fox)
    q, k, v, F = fox_pre(cols, p)
    hm = lambda t: _to_heads(t, FOX_HEADS)
    FT = jnp.transpose(F[:, :FOX_HEADS])
    return _from_heads(fox_attention(hm(q), hm(k), hm(v), F, FT))


def kernel(x, c, ada_w, ada_b, norm_mix, norm_ffn, w_in, rw_mu, rw_w0, rw_w_up, rw_a0, rw_a_up, rw_g_up, rw_k_k, rw_k_a, rw_r_k, rw_lnx_w, rw_lnx_b, rw_v0, rw_v_down, rw_v_up, ssd_conv_w, ssd_conv_b, ssd_dt_bias, ssd_a_log, ssd_d, ssd_norm, fox_f_bias, fox_q_gain, fox_k_gain, gate_b, proj_rw, proj_ssd, proj_fox, w_out, ffn_w_gu, ffn_w_down, moe_router, moe_w_gu, moe_w_down):
    depth = w_in.shape[0]
    D = D_MODEL
    xs = x[0]
    row = lambda t: t.reshape(1, -1).astype(F32)
    seg64 = _seg_matrix(RW_WIDTH, RW_HEAD_DIM)
    rw_cols = 3 * RW_WIDTH + RW_DECAY_LORA + RW_AAA_LORA + RW_GATE_LORA
    ssd_cols = SSD_WIDTH + SSD_XBC + SSD_HEADS
    fox_cols = 3 * FOX_WIDTH + FOX_HEADS
    expand = (jnp.arange(LANES)[:, None] == (jnp.arange(SSD_WIDTH) // SSD_HEAD_DIM)[None, :]).astype(F32)
    v_first = None
    for l in range(depth):
        mod = adaln_mod(c, ada_w[l], ada_b[l])
        sh_m, sc_m, g_m, sh_f, sc_f, g_f = [mod[:, i * D:(i + 1) * D] for i in range(6)]
        h = norm_mod(xs, row(norm_mix[l]), sc_m, sh_m)

        wl = w_in[l]
        o = 0
        w_r = wl[:, o:o + rw_cols]; o += rw_cols
        w_s = wl[:, o:o + ssd_cols]; o += ssd_cols
        w_f = wl[:, o:o + fox_cols]; o += fox_cols
        w_g = wl[:, o:]
        W3 = 3 * RW_WIDTH
        o_a = W3 + RW_DECAY_LORA
        o_g = o_a + RW_AAA_LORA

        def rw_layout(t, hv):
            parts = [t[:, :W3], _pad_cols(t[:, W3:o_a], LANES), _pad_cols(t[:, o_a:o_g], LANES),
                     _pad_cols(t[:, o_g:], 2 * LANES), _pad_cols(hv, LANES)]
            return jnp.concatenate(parts, axis=1)

        if l == 0:
            hv_w = jnp.zeros((D, RW_VRES_LORA), F32)
        else:
            hv_w = rw_v_down[l - 1]
        w_rw = rw_layout(w_r, hv_w).astype(BF16)
        mu = rw_layout(rw_mu[l].reshape(1, -1), jnp.zeros((1, RW_VRES_LORA), F32))
        rw_p = dict(mu=mu, w0=row(rw_w0[l]), w_up=_pad_rows(rw_w_up[l], LANES).astype(BF16), a0=row(rw_a0[l]),
                    a_up=_pad_rows(rw_a_up[l], LANES).astype(BF16),
                    g_up=_pad_rows(rw_g_up[l], 2 * LANES).astype(BF16),
                    k_k=row(rw_k_k[l]), k_a=row(rw_k_a[l]), r_k=row(rw_r_k[l]), seg=seg64)
        if l > 0:
            rw_p.update(v0=row(rw_v0[l - 1]), v_up=_pad_rows(rw_v_up[l - 1], LANES).astype(BF16))
        yn, bonus, g_rw, v_cur = rwkv_branch(h, w_rw, rw_p, v_first)
        if l == 0:
            v_first = v_cur

        w_ssd = jnp.concatenate([w_s[:, SSD_WIDTH:SSD_WIDTH + SSD_XBC], w_s[:, :SSD_WIDTH],
                                 _pad_cols(w_s[:, SSD_WIDTH + SSD_XBC:], LANES)], axis=1).astype(BF16)
        A = -jnp.exp(ssd_a_log[l].astype(F32))
        ssd_p = dict(conv_w=ssd_conv_w[l], conv_b=row(ssd_conv_b[l]),
                     dt_bias=_pad_cols(row(ssd_dt_bias[l]), LANES), dt_biasT=ssd_dt_bias[l].reshape(-1, 1),
                     A=_pad_cols(row(A), LANES), AT=A.reshape(-1, 1), expand=expand,
                     d_skip=row(jnp.repeat(ssd_d[l], SSD_HEAD_DIM)), norm_w=row(ssd_norm[l]))
        y_ssd = ssd_branch(h, w_ssd, ssd_p)

        w_fox = jnp.concatenate([w_f[:, :3 * FOX_WIDTH], _pad_cols(w_f[:, 3 * FOX_WIDTH:], LANES)],
                                axis=1).astype(BF16)
        fox_p = dict(f_bias=_pad_cols(row(fox_f_bias[l]), LANES),
                     q_gain=row(jnp.tile(fox_q_gain[l], FOX_HEADS)) * (FOX_HEAD_DIM ** -0.5),
                     k_gain=row(jnp.tile(fox_k_gain[l], FOX_HEADS)), seg=seg64)
        y_fox = fox_branch(h, w_fox, fox_p)

        gate_logits = matmul(h, w_g.astype(BF16))
        mp = dict(gate_b=row(gate_b[l]), lnx_w=row(rw_lnx_w[l]), lnx_b=row(rw_lnx_b[l]),
                  proj_rw=proj_rw[l].astype(BF16), proj_ssd=proj_ssd[l].astype(BF16),
                  proj_fox=proj_fox[l].astype(BF16), w_out=w_out[l].astype(BF16), g_m=g_m)
        xs = merge(xs, yn, bonus, g_rw, y_ssd, y_fox, gate_logits, mp)

        if l % 2 == 0:
            hf = norm_mod(xs, row(norm_ffn[l]), sc_f, sh_f)
            xs = ffn_dense(hf, xs, ffn_w_gu[l // 2].astype(BF16), ffn_w_down[l // 2].astype(BF16), g_f)
        else:
            hf, comb = norm_router(xs, row(norm_ffn[l]), sc_f, sh_f, _pad_cols(moe_router[l // 2], LANES))
            xs = ffn_moe(hf, xs, comb, moe_w_gu[l // 2].astype(BF16), moe_w_down[l // 2].astype(BF16), g_f)
    return xs[None]
```

```python
import functools
import math

import jax
import jax.numpy as jnp
from jax import lax
from jax.experimental import pallas as pl
from jax.experimental.pallas import tpu as pltpu

F32 = jnp.float32
BF16 = jnp.bfloat16
HIGHEST = lax.Precision.HIGHEST

D_MODEL = 1024
LANES = 128
SUBLANES = 8

RW_HEADS = 8
RW_HEAD_DIM = 64
RW_WIDTH = RW_HEADS * RW_HEAD_DIM
RW_DECAY_LORA = 64
RW_AAA_LORA = 64
RW_VRES_LORA = 32
RW_GATE_LORA = 160
GN_EPS = 64e-5
RW_OFF_WLO = 3 * RW_WIDTH
RW_OFF_ALO = RW_OFF_WLO + LANES
RW_OFF_GLO = RW_OFF_ALO + LANES
RW_OFF_HV = RW_OFF_GLO + 2 * LANES
RW_PAD_COLS = RW_OFF_HV + LANES
RW_CHUNK = 64

SSD_HEADS = 16
SSD_HEAD_DIM = 64
SSD_WIDTH = SSD_HEADS * SSD_HEAD_DIM
SSD_GROUPS = 4
SSD_STATE = 128
SSD_CONV = 4
SSD_CHUNK = 128
SSD_XBC = SSD_WIDTH + 2 * SSD_GROUPS * SSD_STATE
SSD_GROUP_WIDTH = SSD_WIDTH // SSD_GROUPS

FOX_HEADS = 8
FOX_HEAD_DIM = 64
FOX_WIDTH = FOX_HEADS * FOX_HEAD_DIM
FOX_TQ = 256
LOG2E = 1.4426950408889634

FFN_DENSE = 2816
N_EXPERTS = 8
FFN_EXPERT = 3584
EPS = 1e-6

VMEM_LIMIT = 56 * 1024 * 1024


def _cparams(*sem):
    return pltpu.CompilerParams(dimension_semantics=sem, vmem_limit_bytes=VMEM_LIMIT)


def _sigmoid(x):
    return 1.0 / (1.0 + jnp.exp(-x))


def _softplus(x):
    return jnp.maximum(x, 0.0) + jnp.log(1.0 + jnp.exp(-jnp.abs(x)))


def _dot(a, b):
    return jnp.dot(a, b, preferred_element_type=F32)


def _dot_bf16(a, b):
    return jnp.dot(a.astype(BF16), b.astype(BF16), preferred_element_type=F32)


def _dot_hi(a, b):
    return jnp.dot(a, b, precision=HIGHEST, preferred_element_type=F32)


def _dot_nt(a, b, precision=None):
    return lax.dot_general(a, b, (((1,), (1,)), ((), ())), precision=precision, preferred_element_type=F32)


def _dot_tn(a, b, precision=None):
    return lax.dot_general(a, b, (((0,), (0,)), ((), ())), precision=precision, preferred_element_type=F32)


def _split_dot(x, ones_bf16):
    hi = x.astype(BF16)
    lo = (x - hi.astype(F32)).astype(BF16)
    return _dot(hi, ones_bf16) + _dot(lo, ones_bf16)


def _tril(n, strict=False):
    r = lax.broadcasted_iota(jnp.int32, (n, n), 0)
    c = lax.broadcasted_iota(jnp.int32, (n, n), 1)
    return (r > c) if strict else (r >= c)


def _mod_kernel(c_ref, w_ref, b_ref, o_ref):
    c = c_ref[...]
    o_ref[...] = _dot_hi(c * _sigmoid(c), w_ref[...]) + b_ref[...]


def adaln_mod(c, w, b):
    d, n = w.shape
    tn = 1024
    c8 = jnp.broadcast_to(c, (SUBLANES, d))
    out = pl.pallas_call(
        _mod_kernel,
        grid=(n // tn,),
        in_specs=[pl.BlockSpec((SUBLANES, d), lambda j: (0, 0)),
                  pl.BlockSpec((d, tn), lambda j: (0, j)),
                  pl.BlockSpec((1, tn), lambda j: (0, j))],
        out_specs=pl.BlockSpec((SUBLANES, tn), lambda j: (0, j)),
        out_shape=jax.ShapeDtypeStruct((SUBLANES, n), F32),
        compiler_params=_cparams("arbitrary"),
    )(c8, w, b.reshape(1, n))
    return out[:1]


def _norm_mod(x, gain, sc, sh):
    y = x * lax.rsqrt(jnp.mean(x * x, axis=-1, keepdims=True) + EPS)
    return y * gain * (1.0 + sc) + sh


def _norm_kernel(x_ref, gain_ref, sc_ref, sh_ref, h_ref):
    h_ref[...] = _norm_mod(x_ref[...], gain_ref[...], sc_ref[...], sh_ref[...]).astype(h_ref.dtype)


def norm_mod(x, gain, sc, sh, tm=512):
    L, d = x.shape
    row = pl.BlockSpec((1, d), lambda i: (0, 0))
    return pl.pallas_call(
        _norm_kernel,
        grid=(L // tm,),
        in_specs=[pl.BlockSpec((tm, d), lambda i: (i, 0)), row, row, row],
        out_specs=pl.BlockSpec((tm, d), lambda i: (i, 0)),
        out_shape=jax.ShapeDtypeStruct((L, d), BF16),
        compiler_params=_cparams("arbitrary"),
    )(x, gain, sc, sh)


def _mm_kernel(x_ref, w_ref, o_ref):
    o_ref[...] = _dot(x_ref[...], w_ref[...]).astype(o_ref.dtype)


def matmul(x, w, out_dtype=F32, tm=512):
    M, K = x.shape
    N = w.shape[1]
    return pl.pallas_call(
        _mm_kernel,
        grid=(M // tm,),
        in_specs=[pl.BlockSpec((tm, K), lambda i: (i, 0)),
                  pl.BlockSpec((K, N), lambda i: (0, 0))],
        out_specs=pl.BlockSpec((tm, N), lambda i: (i, 0)),
        out_shape=jax.ShapeDtypeStruct((M, N), out_dtype),
        compiler_params=_cparams("arbitrary"),
    )(x, w)


def _rw_pre_kernel(has_vres, *refs):
    if has_vres:
        (cols_ref, mu_ref, w0_ref, wup_ref, a0_ref, aup_ref, gup_ref, kk_ref, ka_ref, rk_ref, seg_ref,
         vfirst_ref, v0_ref, vup_ref,
         r_out, k_out, v_out, lw_out, a_out, b_out, g_out, bonus_out, ext_scr) = refs
    else:
        (cols_ref, mu_ref, w0_ref, wup_ref, a0_ref, aup_ref, gup_ref, kk_ref, ka_ref, rk_ref, seg_ref,
         r_out, k_out, v_out, lw_out, a_out, b_out, g_out, bonus_out, ext_scr) = refs
    T = cols_ref.shape[0]

    @pl.when(pl.program_id(0) == 0)
    def _():
        ext_scr[0:SUBLANES, :] = jnp.zeros((SUBLANES, ext_scr.shape[1]), F32)

    cur = cols_ref[...]
    ext_scr[SUBLANES:SUBLANES + T, :] = cur
    prev = ext_scr[SUBLANES - 1:SUBLANES - 1 + T, :]
    ext_scr[0:SUBLANES, :] = cur[T - SUBLANES:T, :]
    s = cur + (prev - cur) * mu_ref[...]

    W = RW_WIDTH
    r = s[:, 0:W]
    k = s[:, W:2 * W]
    v = s[:, 2 * W:3 * W]
    w_lo = s[:, RW_OFF_WLO:RW_OFF_WLO + LANES]
    a_lo = s[:, RW_OFF_ALO:RW_OFF_ALO + LANES]
    g_lo = s[:, RW_OFF_GLO:RW_OFF_GLO + 2 * LANES]
    seg = seg_ref[...]

    wlog = -_softplus(-(w0_ref[...] + _dot_bf16(jnp.tanh(w_lo), wup_ref[...]))) - 0.5
    a = _sigmoid(a0_ref[...] + _dot_bf16(a_lo, aup_ref[...]))
    g = _dot_bf16(_sigmoid(g_lo), gup_ref[...])
    if has_vres:
        hv = s[:, RW_OFF_HV:RW_OFF_HV + LANES]
        v = v + (vfirst_ref[...] - v) * _sigmoid(v0_ref[...] + _dot_bf16(hv, vup_ref[...]))
    kk = k * kk_ref[...]
    kk = kk / jnp.maximum(jnp.sqrt(_split_dot(kk * kk, seg)), 1e-12)
    k = k * (1.0 + (a - 1.0) * ka_ref[...])
    r_out[...] = r
    k_out[...] = k
    v_out[...] = v
    lw_out[...] = -jnp.exp(wlog)
    a_out[...] = -kk
    b_out[...] = kk * a
    g_out[...] = g
    bonus_out[...] = _split_dot(r * k * rk_ref[...], seg) * v


def rw_pre(cols, p, v_first, tm=256):
    L = cols.shape[0]
    has_vres = v_first is not None
    full = lambda a: pl.BlockSpec(a.shape, lambda i: (0,) * a.ndim)
    rows = lambda n: pl.BlockSpec((tm, n), lambda i: (i, 0))
    args = [cols, p["mu"], p["w0"], p["w_up"], p["a0"], p["a_up"], p["g_up"], p["k_k"], p["k_a"], p["r_k"],
            p["seg"]]
    specs = [rows(RW_PAD_COLS)] + [full(a) for a in args[1:]]
    if has_vres:
        args += [v_first, p["v0"], p["v_up"]]
        specs += [rows(RW_WIDTH), full(p["v0"]), full(p["v_up"])]
    return pl.pallas_call(
        functools.partial(_rw_pre_kernel, has_vres),
        grid=(L // tm,),
        in_specs=specs,
        out_specs=[rows(RW_WIDTH)] * 8,
        out_shape=[jax.ShapeDtypeStruct((L, RW_WIDTH), F32)] * 8,
        scratch_shapes=[pltpu.VMEM((tm + SUBLANES, RW_PAD_COLS), F32)],
        compiler_params=_cparams("arbitrary"),
    )(*args)


def _wkv_kernel(r_ref, lw_ref, k_ref, v_ref, a_ref, b_ref, y_ref, h_scr):
    C = RW_CHUNK
    C2 = 2 * C
    n_chunks = r_ref.shape[0] // C

    @pl.when(pl.program_id(1) == 0)
    def _():
        h_scr[...] = jnp.zeros_like(h_scr)

    is_a = lax.broadcasted_iota(jnp.int32, (C, LANES), 1) < RW_HEAD_DIM
    ri = lax.broadcasted_iota(jnp.int32, (C2, C2), 0)
    ci = lax.broadcasted_iota(jnp.int32, (C2, C2), 1)
    same = (ri >= C) == (ci >= C)
    strict = jnp.logical_and(same, ri > ci)
    incl = jnp.logical_and(same, ri >= ci)
    eye2 = (ri == ci).astype(F32)
    tri_c = _tril(C).astype(BF16)

    def two(x):
        return jnp.concatenate([jnp.where(is_a, x, 0.0), jnp.where(is_a, 0.0, x)], axis=0)

    for c in range(n_chunks):
        sl = pl.ds(c * C, C)
        r = r_ref[sl, :]
        lw = lw_ref[sl, :]
        k = k_ref[sl, :]
        v = v_ref[sl, :]
        a = a_ref[sl, :]
        b = b_ref[sl, :]
        lw_hi = lw.astype(BF16)
        lw_lo = (lw - lw_hi.astype(F32)).astype(BF16)
        cs = _dot(tri_c, lw_hi) + _dot(tri_c, lw_lo)
        cs_end = cs[C - 1:C, :]
        e_neg = jnp.exp(-cs)
        e_end = jnp.exp(cs_end - cs)
        at2 = two(a * jnp.exp(cs - lw))
        rt2 = two(r * jnp.exp(cs))
        bt2 = two(b * e_neg)
        kt2 = two(k * e_neg)
        bh2 = two(b * e_end)
        kh2 = two(k * e_end)
        v2 = two(v)
        quad = _dot_nt(jnp.concatenate([at2, rt2], axis=0).astype(BF16),
                       jnp.concatenate([bt2, kt2], axis=0).astype(BF16))
        n_ab = jnp.where(strict, quad[0:C2, 0:C2], 0.0)
        a_ak = jnp.where(strict, quad[0:C2, C2:], 0.0)
        m_rb = jnp.where(incl, quad[C2:, 0:C2], 0.0)
        m_rk = jnp.where(incl, quad[C2:, C2:], 0.0)
        t_inv = eye2 + n_ab
        n_pow = n_ab
        for _ in range(int(math.log2(C)) - 1):
            n_pow = _dot_bf16(n_pow, n_pow)
            t_inv = t_inv + _dot_bf16(t_inv, n_pow)
        au = _dot_bf16(t_inv, jnp.concatenate([at2, _dot_bf16(a_ak, v2)], axis=1))
        mau = _dot_bf16(m_rb, au)
        rh2 = rt2 + mau[:, :LANES]
        y_loc2 = mau[:, LANES:] + _dot_bf16(m_rk, v2)
        gu = _dot_tn(bh2.astype(BF16), au.astype(BF16))
        g_mat = eye2 * jnp.exp(cs_end) + gu[:, :LANES]
        h_loc = gu[:, LANES:] + _dot_tn(kh2.astype(BF16), v2.astype(BF16))
        yh = _dot_bf16(jnp.concatenate([rh2, g_mat], axis=0), h_scr[...])
        h_scr[...] = yh[C2:, :] + h_loc
        y2 = yh[:C2, :] + y_loc2
        y = y2[:C, :] + y2[C:, :]
        inv_n = 1.0 / RW_HEAD_DIM
        s_all = jnp.sum(y, axis=-1, keepdims=True)
        s_a = jnp.sum(jnp.where(is_a, y, 0.0), axis=-1, keepdims=True)
        yc = y - jnp.where(is_a, s_a, s_all - s_a) * inv_n
        sq = yc * yc
        q_all = jnp.sum(sq, axis=-1, keepdims=True)
        q_a = jnp.sum(jnp.where(is_a, sq, 0.0), axis=-1, keepdims=True)
        y_ref[sl, :] = yc * lax.rsqrt(jnp.where(is_a, q_a, q_all - q_a) * inv_n + GN_EPS)


def wkv7(r, lw, k, v, a, b, rows=512):
    L, W = r.shape
    spec = pl.BlockSpec((rows, LANES), lambda p, i: (i, p))
    return pl.pallas_call(
        _wkv_kernel,
        grid=(W // LANES, L // rows),
        in_specs=[spec] * 6,
        out_specs=spec,
        out_shape=jax.ShapeDtypeStruct((L, W), F32),
        scratch_shapes=[pltpu.VMEM((LANES, LANES), F32)],
        compiler_params=_cparams("arbitrary", "arbitrary"),
    )(r, lw, k, v, a, b)


def _ssd_conv_kernel(x_ref, w_ref, b_ref, o_ref, ext_scr):
    T = x_ref.shape[0]

    @pl.when(pl.program_id(0) == 0)
    def _():
        ext_scr[0:SUBLANES, :] = jnp.zeros((SUBLANES, ext_scr.shape[1]), F32)

    cur = x_ref[...]
    ext_scr[SUBLANES:SUBLANES + T, :] = cur
    acc = cur * w_ref[SSD_CONV - 1:SSD_CONV, :] + b_ref[...]
    for j in range(1, SSD_CONV):
        acc = acc + ext_scr[SUBLANES - j:SUBLANES - j + T, :] * w_ref[SSD_CONV - 1 - j:SSD_CONV - j, :]
    ext_scr[0:SUBLANES, :] = cur[T - SUBLANES:T, :]
    o_ref[...] = acc * _sigmoid(acc)


def ssd_conv(ssd_cols, conv_w, conv_b, tm=512):
    L = ssd_cols.shape[0]
    return pl.pallas_call(
        _ssd_conv_kernel,
        grid=(L // tm,),
        in_specs=[pl.BlockSpec((tm, SSD_XBC), lambda i: (i, 0)),
                  pl.BlockSpec((SSD_CONV, SSD_XBC), lambda i: (0, 0)),
                  pl.BlockSpec((1, SSD_XBC), lambda i: (0, 0))],
        out_specs=pl.BlockSpec((tm, SSD_XBC), lambda i: (i, 0)),
        out_shape=jax.ShapeDtypeStruct((L, SSD_XBC), F32),
        scratch_shapes=[pltpu.VMEM((tm + SUBLANES, SSD_XBC), F32)],
        compiler_params=_cparams("arbitrary"),
    )(ssd_cols, conv_w, conv_b)


def _ssd_kernel(xbc_ref, z_ref, dt_ref, dtT_ref, bias_ref, biasT_ref, A_ref, AT_ref, exp_ref, dskip_ref,
                nw_ref, o_ref, s_scr):
    Q = SSD_CHUNK
    GW = SSD_GROUP_WIDTH
    NS = SSD_STATE

    @pl.when(pl.program_id(0) == 0)
    def _():
        s_scr[...] = jnp.zeros_like(s_scr)

    incl = _tril(Q)
    tri_f = incl.astype(F32)
    xs = xbc_ref[:, 0:SSD_WIDTH]
    dt = _softplus(dt_ref[...] + bias_ref[...])
    a = dt * A_ref[...]
    a_cum = _dot_hi(tri_f, a)
    a_end = a_cum[Q - 1:Q, :]
    aT = _softplus(dtT_ref[...] + biasT_ref[...]) * AT_ref[...]
    a_cumT = _dot_nt(aT, tri_f, HIGHEST)
    expand = exp_ref[...]
    dt_x = _dot_hi(dt, expand)
    eac_x = _dot_hi(jnp.exp(a_cum), expand)
    dte_x = _dot_hi(jnp.exp(a_end - a_cum), expand)
    xdt = xs * dt_x
    lane = lax.broadcasted_iota(jnp.int32, (Q, LANES), 1)
    first_half = lane < SSD_HEAD_DIM
    heads_per_group = SSD_HEADS // SSD_GROUPS
    z = z_ref[...]
    for g in range(SSD_GROUPS):
        Bg = xbc_ref[:, SSD_WIDTH + g * NS:SSD_WIDTH + (g + 1) * NS]
        Cg = xbc_ref[:, SSD_WIDTH + SSD_GROUPS * NS + g * NS:SSD_WIDTH + SSD_GROUPS * NS + (g + 1) * NS]
        cb = _dot_nt(Cg.astype(BF16), Bg.astype(BF16))
        gs = slice(g * GW, (g + 1) * GW)
        s_prev = s_scr[:, gs]
        y_g = _dot_bf16(Cg, s_prev) * eac_x[:, gs]
        pieces = []
        for pr in range(heads_per_group // 2):
            ps = slice(g * GW + pr * LANES, g * GW + (pr + 1) * LANES)
            xdt_p = xdt[:, ps]
            acc = None
            for e in range(2):
                h = g * heads_per_group + pr * 2 + e
                seg = a_cum[:, h:h + 1] - a_cumT[h:h + 1, :]
                m = cb * jnp.exp(jnp.where(incl, seg, -jnp.inf))
                xm = jnp.where(first_half if e == 0 else jnp.logical_not(first_half), xdt_p, 0.0)
                t = _dot_bf16(m, xm)
                acc = t if acc is None else acc + t
            pieces.append(acc)
        y_g = y_g + jnp.concatenate(pieces, axis=1)
        s_scr[:, gs] = s_prev * eac_x[Q - 1:Q, gs] + _dot_tn(Bg.astype(BF16), (dte_x[:, gs] * xdt[:, gs]).astype(BF16))
        y_g = y_g + dskip_ref[:, gs] * xs[:, gs]
        zg = z[:, gs]
        y_g = y_g * (zg * _sigmoid(zg))
        y_g = y_g * lax.rsqrt(jnp.mean(y_g * y_g, axis=-1, keepdims=True) + EPS)
        o_ref[:, gs] = y_g * nw_ref[:, gs]


def ssd_scan(xbc, ssd_cols, dtT, p):
    L = xbc.shape[0]
    Q = SSD_CHUNK
    full = lambda a: pl.BlockSpec(a.shape, lambda i: (0,) * a.ndim)
    z_blk = SSD_XBC // SSD_WIDTH
    dt_blk = (SSD_XBC + SSD_WIDTH) // LANES
    small = [p["dt_bias"], p["dt_biasT"], p["A"], p["AT"], p["expand"], p["d_skip"], p["norm_w"]]
    return pl.pallas_call(
        _ssd_kernel,
        grid=(L // Q,),
        in_specs=[pl.BlockSpec((Q, SSD_XBC), lambda i: (i, 0)),
                  pl.BlockSpec((Q, SSD_WIDTH), lambda i: (i, z_blk)),
                  pl.BlockSpec((Q, LANES), lambda i: (i, dt_blk)),
                  pl.BlockSpec((SSD_HEADS, Q), lambda i: (0, i))] + [full(a) for a in small],
        out_specs=pl.BlockSpec((Q, SSD_WIDTH), lambda i: (i, 0)),
        out_shape=jax.ShapeDtypeStruct((L, SSD_WIDTH), F32),
        scratch_shapes=[pltpu.VMEM((SSD_STATE, SSD_WIDTH), F32)],
        compiler_params=_cparams("arbitrary"),
    )(xbc, ssd_cols, ssd_cols, dtT, *small)


def _fox_pre_kernel(cols_ref, fb_ref, qg_ref, kg_ref, seg_ref, q_out, k_out, v_out, f_out, carry_scr):
    T = cols_ref.shape[0]
    W = FOX_WIDTH

    @pl.when(pl.program_id(0) == 0)
    def _():
        carry_scr[...] = jnp.zeros_like(carry_scr)

    seg = seg_ref[...]
    q = cols_ref[:, 0:W]
    k = cols_ref[:, W:2 * W]
    inv_d = 1.0 / FOX_HEAD_DIM
    qn = q * lax.rsqrt(_split_dot(q * q, seg) * inv_d + EPS) * qg_ref[...]
    kn = k * lax.rsqrt(_split_dot(k * k, seg) * inv_d + EPS) * kg_ref[...]
    q_out[...] = qn.astype(q_out.dtype)
    k_out[...] = kn.astype(k_out.dtype)
    v_out[...] = cols_ref[:, 2 * W:3 * W].astype(v_out.dtype)
    f = cols_ref[:, 3 * W:3 * W + LANES] + fb_ref[...]
    ls = -_softplus(-f)
    cum = _dot_hi(_tril(T).astype(F32), ls) + carry_scr[0:1, :]
    f_out[...] = cum * LOG2E
    carry_scr[...] = jnp.broadcast_to(cum[T - 1:T, :], carry_scr.shape)


def fox_pre(cols, p, tm=512):
    L = cols.shape[0]
    full = lambda a: pl.BlockSpec(a.shape, lambda i: (0,) * a.ndim)
    rows = lambda n: pl.BlockSpec((tm, n), lambda i: (i, 0))
    small = [p["f_bias"], p["q_gain"], p["k_gain"], p["seg"]]
    return pl.pallas_call(
        _fox_pre_kernel,
        grid=(L // tm,),
        in_specs=[rows(cols.shape[1])] + [full(a) for a in small],
        out_specs=[rows(FOX_WIDTH)] * 3 + [rows(LANES)],
        out_shape=[jax.ShapeDtypeStruct((L, FOX_WIDTH), BF16)] * 3 + [jax.ShapeDtypeStruct((L, LANES), F32)],
        scratch_shapes=[pltpu.VMEM((SUBLANES, LANES), F32)],
        compiler_params=_cparams("arbitrary"),
    )(cols, *small)


def _fox_kernel(lo_ref, q_ref, k_ref, v_ref, fk_ref, o_ref, m_scr, l_scr, acc_scr):
    tq = q_ref.shape[0]
    pair = pl.program_id(0)
    qi = pl.program_id(1)
    is_a = lax.broadcasted_iota(jnp.int32, (tq, LANES), 1) < FOX_HEAD_DIM
    q = q_ref[...]
    causal = _tril(tq)
    outs = []
    for e in range(2):
        h = 2 * pair + e
        qe = jnp.where(is_a if e == 0 else jnp.logical_not(is_a), q, jnp.zeros_like(q))
        m_scr[...] = jnp.full(m_scr.shape, -jnp.inf, F32)
        l_scr[...] = jnp.zeros_like(l_scr)
        acc_scr[...] = jnp.zeros_like(acc_scr)

        def block(kb, diagonal):
            ks = pl.ds(pl.multiple_of(kb * tq, tq), tq)
            s = _dot_nt(qe, k_ref[ks, :]) - fk_ref[pl.ds(h, 1), ks]
            if diagonal:
                s = jnp.where(causal, s, -jnp.inf)
            m_old = m_scr[...]
            m_new = jnp.maximum(m_old, jnp.max(s, axis=-1, keepdims=True))
            pexp = jnp.exp2(s - m_new)
            alpha = jnp.exp2(m_old - m_new)
            l_scr[...] = alpha * l_scr[...] + jnp.sum(pexp, axis=-1, keepdims=True)
            acc_scr[...] = alpha * acc_scr[...] + _dot(pexp.astype(BF16), v_ref[ks, :])
            m_scr[...] = m_new

        block(qi, True)

        def body(i, carry):
            block(qi - 1 - i, False)
            return carry

        lax.fori_loop(0, qi - lo_ref[h, qi], body, 0)
        outs.append(acc_scr[...] / l_scr[...])
    o_ref[...] = jnp.where(is_a, outs[0], outs[1])


def fox_attention(q, k, v, FT, lo, tq=256):
    L, W = q.shape
    H = FT.shape[0]
    grid_spec = pltpu.PrefetchScalarGridSpec(
        num_scalar_prefetch=1,
        grid=(W // LANES, L // tq),
        in_specs=[pl.BlockSpec((tq, LANES), lambda p, i, lo_r: (i, p)),
                  pl.BlockSpec((L, LANES), lambda p, i, lo_r: (0, p)),
                  pl.BlockSpec((L, LANES), lambda p, i, lo_r: (0, p)),
                  pl.BlockSpec((H, L), lambda p, i, lo_r: (0, 0))],
        out_specs=pl.BlockSpec((tq, LANES), lambda p, i, lo_r: (i, p)),
        scratch_shapes=[pltpu.VMEM((tq, 1), F32), pltpu.VMEM((tq, 1), F32), pltpu.VMEM((tq, LANES), F32)],
    )
    return pl.pallas_call(
        _fox_kernel,
        grid_spec=grid_spec,
        out_shape=jax.ShapeDtypeStruct((L, W), F32),
        compiler_params=_cparams("arbitrary", "arbitrary"),
    )(lo, q, k, v, FT)


def fox_first_block(FT, bound2, tq):
    H, L = FT.shape
    nq = L // tq
    f_first = FT[:, ::tq]
    f_last = FT[:, tq - 1::tq]
    gap = f_first[:, :, None] - f_last[:, None, :] + 2.0 * bound2
    kb = jnp.arange(nq)
    needed = jnp.logical_or(gap >= -152.0, kb[None, None, :] >= kb[None, :, None])
    return jnp.argmax(needed, axis=-1).astype(jnp.int32)


def _merge_kernel(x_ref, yn_ref, bonus_ref, g_ref, yssd_ref, yfox_ref, gl_ref, gb_ref, lnw_ref, lnb_ref,
                  prw_ref, pssd_ref, pfox_ref, wout_ref, gm_ref, o_ref):
    D = D_MODEL
    y_rw = (yn_ref[...] * lnw_ref[...] + lnb_ref[...] + bonus_ref[...]) * g_ref[...]
    gates = _sigmoid(gl_ref[...] + gb_ref[...])
    merged = gates[:, 0:D] * _dot_bf16(y_rw, prw_ref[...])
    merged = merged + gates[:, D:2 * D] * _dot_bf16(yssd_ref[...], pssd_ref[...])
    merged = merged + gates[:, 2 * D:3 * D] * _dot_bf16(yfox_ref[...], pfox_ref[...])
    o_ref[...] = x_ref[...] + gm_ref[...] * _dot_bf16(merged, wout_ref[...])


def merge(x, yn, bonus, g, y_ssd, y_fox, gate_logits, p, tm=256):
    L, D = x.shape
    full = lambda a: pl.BlockSpec(a.shape, lambda i: (0,) * a.ndim)
    rows = lambda n: pl.BlockSpec((tm, n), lambda i: (i, 0))
    small = [p["gate_b"], p["lnx_w"], p["lnx_b"], p["proj_rw"], p["proj_ssd"], p["proj_fox"], p["w_out"], p["g_m"]]
    return pl.pallas_call(
        _merge_kernel,
        grid=(L // tm,),
        in_specs=[rows(D), rows(RW_WIDTH), rows(RW_WIDTH), rows(RW_WIDTH), rows(SSD_WIDTH), rows(FOX_WIDTH),
                  rows(3 * D)] + [full(a) for a in small],
        out_specs=rows(D),
        out_shape=jax.ShapeDtypeStruct((L, D), F32),
        compiler_params=_cparams("arbitrary"),
    )(x, yn, bonus, g, y_ssd, y_fox, gate_logits, *small)


def _ffn_kernel(h_ref, x_ref, wg_ref, wu_ref, wd_ref, gf_ref, o_ref, acc_scr):
    j = pl.program_id(1)

    @pl.when(j == 0)
    def _():
        acc_scr[...] = jnp.zeros_like(acc_scr)

    h = h_ref[...]
    gte = _dot(h, wg_ref[...])
    up = _dot(h, wu_ref[...])
    act = gte * _sigmoid(gte) * up
    acc_scr[...] += _dot(act.astype(BF16), wd_ref[...])

    @pl.when(j == pl.num_programs(1) - 1)
    def _():
        o_ref[...] = x_ref[...] + gf_ref[...] * acc_scr[...]


def ffn_dense(h, x, w_gu, w_down, g_f, tm=1024, tf=256):
    L, D = x.shape
    Fh = w_down.shape[0]
    nf = Fh // tf
    return pl.pallas_call(
        _ffn_kernel,
        grid=(L // tm, nf),
        in_specs=[pl.BlockSpec((tm, D), lambda i, j: (i, 0)),
                  pl.BlockSpec((tm, D), lambda i, j: (i, 0)),
                  pl.BlockSpec((D, tf), lambda i, j: (0, j)),
                  pl.BlockSpec((D, tf), lambda i, j: (0, j + nf)),
                  pl.BlockSpec((tf, D), lambda i, j: (j, 0)),
                  pl.BlockSpec((1, D), lambda i, j: (0, 0))],
        out_specs=pl.BlockSpec((tm, D), lambda i, j: (i, 0)),
        out_shape=jax.ShapeDtypeStruct((L, D), F32),
        scratch_shapes=[pltpu.VMEM((tm, D), F32)],
        compiler_params=_cparams("arbitrary", "arbitrary"),
    )(h, x, w_gu, w_gu, w_down, g_f)


def _router_kernel(x_ref, gain_ref, sc_ref, sh_ref, wr_ref, h_ref, comb_ref):
    h = _norm_mod(x_ref[...], gain_ref[...], sc_ref[...], sh_ref[...])
    h_ref[...] = h.astype(h_ref.dtype)
    T = h.shape[0]
    lane = lax.broadcasted_iota(jnp.int32, (T, LANES), 1)
    logits = jnp.where(lane < N_EXPERTS, _dot_hi(h, wr_ref[...]), -jnp.inf)
    m1 = jnp.max(logits, axis=-1, keepdims=True)
    i1 = jnp.min(jnp.where(logits == m1, lane, LANES), axis=-1, keepdims=True)
    rest = jnp.where(lane == i1, -jnp.inf, logits)
    m2 = jnp.max(rest, axis=-1, keepdims=True)
    i2 = jnp.min(jnp.where(rest == m2, lane, LANES), axis=-1, keepdims=True)
    e2 = jnp.exp(m2 - m1)
    w1 = 1.0 / (1.0 + e2)
    w2 = e2 / (1.0 + e2)
    comb_ref[...] = jnp.where(lane == i1, w1, 0.0) + jnp.where(lane == i2, w2, 0.0)


def norm_router(x, gain, sc, sh, router_pad, tm=512):
    L, d = x.shape
    row = pl.BlockSpec((1, d), lambda i: (0, 0))
    return pl.pallas_call(
        _router_kernel,
        grid=(L // tm,),
        in_specs=[pl.BlockSpec((tm, d), lambda i: (i, 0)), row, row, row,
                  pl.BlockSpec((d, LANES), lambda i: (0, 0))],
        out_specs=[pl.BlockSpec((tm, d), lambda i: (i, 0)), pl.BlockSpec((tm, LANES), lambda i: (i, 0))],
        out_shape=[jax.ShapeDtypeStruct((L, d), BF16), jax.ShapeDtypeStruct((L, LANES), F32)],
        compiler_params=_cparams("arbitrary"),
    )(x, gain, sc, sh, router_pad)


def _moe_kernel(h_ref, x_ref, comb_ref, wg_ref, wu_ref, wd_ref, gf_ref, o_ref, acc_scr):
    e = pl.program_id(1)
    j = pl.program_id(2)

    @pl.when(jnp.logical_and(e == 0, j == 0))
    def _():
        acc_scr[...] = jnp.zeros_like(acc_scr)

    h = h_ref[...]
    lane = lax.broadcasted_iota(jnp.int32, comb_ref.shape, 1)
    ce = jnp.sum(jnp.where(lane == e, comb_ref[...], 0.0), axis=-1, keepdims=True)
    gte = _dot(h, wg_ref[0])
    up = _dot(h, wu_ref[0])
    act = gte * _sigmoid(gte) * up
    acc_scr[...] += ce * _dot(act.astype(BF16), wd_ref[0])

    @pl.when(jnp.logical_and(e == pl.num_programs(1) - 1, j == pl.num_programs(2) - 1))
    def _():
        o_ref[...] = x_ref[...] + gf_ref[...] * acc_scr[...]


def ffn_moe(h, x, comb, w_gu, w_down, g_f, tm=1024, tf=256):
    L, D = x.shape
    E, Fh, _ = w_down.shape
    nf = Fh // tf
    return pl.pallas_call(
        _moe_kernel,
        grid=(L // tm, E, nf),
        in_specs=[pl.BlockSpec((tm, D), lambda i, e, j: (i, 0)),
                  pl.BlockSpec((tm, D), lambda i, e, j: (i, 0)),
                  pl.BlockSpec((tm, LANES), lambda i, e, j: (i, 0)),
                  pl.BlockSpec((1, D, tf), lambda i, e, j: (e, 0, j)),
                  pl.BlockSpec((1, D, tf), lambda i, e, j: (e, 0, j + nf)),
                  pl.BlockSpec((1, tf, D), lambda i, e, j: (e, j, 0)),
                  pl.BlockSpec((1, D), lambda i, e, j: (0, 0))],
        out_specs=pl.BlockSpec((tm, D), lambda i, e, j: (i, 0)),
        out_shape=jax.ShapeDtypeStruct((L, D), F32),
        scratch_shapes=[pltpu.VMEM((tm, D), F32)],
        compiler_params=_cparams("arbitrary", "arbitrary", "arbitrary"),
    )(h, x, comb, w_gu, w_gu, w_down, g_f)


def _seg_matrix(width, head_dim):
    idx = jnp.arange(width) // head_dim
    return (idx[:, None] == idx[None, :]).astype(BF16)


def _pad_cols(w, n):
    return jnp.pad(w, ((0, 0), (0, n - w.shape[1])))


def _pad_rows(w, n):
    return jnp.pad(w, ((0, n - w.shape[0]), (0, 0)))


def rwkv_branch(h, w_rw, p, v_first):
    cols = matmul(h, w_rw)
    r, k, v, lw, a, b, g, bonus = rw_pre(cols, p, v_first)
    return wkv7(r, lw, k, v, a, b), bonus, g, v


def ssd_branch(h, w_ssd, p):
    cols = matmul(h, w_ssd)
    xbc = ssd_conv(cols, p["conv_w"], p["conv_b"])
    dtT = jnp.transpose(cols[:, SSD_XBC + SSD_WIDTH:SSD_XBC + SSD_WIDTH + SSD_HEADS])
    return ssd_scan(xbc, cols, dtT, p)


def fox_branch(h, w_fox, p):
    cols = matmul(h, w_fox)
    q, k, v, F = fox_pre(cols, p)
    FT = jnp.transpose(F[:, :FOX_HEADS])
    lo = fox_first_block(FT, p["bound2"], FOX_TQ)
    return fox_attention(q, k, v, FT, lo, tq=FOX_TQ)


def kernel(x, c, ada_w, ada_b, norm_mix, norm_ffn, w_in, rw_mu, rw_w0, rw_w_up, rw_a0, rw_a_up, rw_g_up, rw_k_k, rw_k_a, rw_r_k, rw_lnx_w, rw_lnx_b, rw_v0, rw_v_down, rw_v_up, ssd_conv_w, ssd_conv_b, ssd_dt_bias, ssd_a_log, ssd_d, ssd_norm, fox_f_bias, fox_q_gain, fox_k_gain, gate_b, proj_rw, proj_ssd, proj_fox, w_out, ffn_w_gu, ffn_w_down, moe_router, moe_w_gu, moe_w_down):
    depth = w_in.shape[0]
    D = D_MODEL
    xs = x[0]
    row = lambda t: t.reshape(1, -1).astype(F32)
    seg64 = _seg_matrix(RW_WIDTH, RW_HEAD_DIM)
    rw_cols = 3 * RW_WIDTH + RW_DECAY_LORA + RW_AAA_LORA + RW_GATE_LORA
    ssd_cols = SSD_WIDTH + SSD_XBC + SSD_HEADS
    fox_cols = 3 * FOX_WIDTH + FOX_HEADS
    expand = (jnp.arange(LANES)[:, None] == (jnp.arange(SSD_WIDTH) // SSD_HEAD_DIM)[None, :]).astype(F32)
    v_first = None
    for l in range(depth):
        mod = adaln_mod(c, ada_w[l], ada_b[l])
        sh_m, sc_m, g_m, sh_f, sc_f, g_f = [mod[:, i * D:(i + 1) * D] for i in range(6)]
        h = norm_mod(xs, row(norm_mix[l]), sc_m, sh_m)

        wl = w_in[l]
        o = 0
        w_r = wl[:, o:o + rw_cols]; o += rw_cols
        w_s = wl[:, o:o + ssd_cols]; o += ssd_cols
        w_f = wl[:, o:o + fox_cols]; o += fox_cols
        w_g = wl[:, o:]
        W3 = 3 * RW_WIDTH
        o_a = W3 + RW_DECAY_LORA
        o_g = o_a + RW_AAA_LORA

        def rw_layout(t, hv):
            parts = [t[:, :W3], _pad_cols(t[:, W3:o_a], LANES), _pad_cols(t[:, o_a:o_g], LANES),
                     _pad_cols(t[:, o_g:], 2 * LANES), _pad_cols(hv, LANES)]
            return jnp.concatenate(parts, axis=1)

        if l == 0:
            hv_w = jnp.zeros((D, RW_VRES_LORA), F32)
        else:
            hv_w = rw_v_down[l - 1]
        w_rw = rw_layout(w_r, hv_w).astype(BF16)
        mu = rw_layout(rw_mu[l].reshape(1, -1), jnp.zeros((1, RW_VRES_LORA), F32))
        rw_p = dict(mu=mu, w0=row(rw_w0[l]), w_up=_pad_rows(rw_w_up[l], LANES).astype(BF16), a0=row(rw_a0[l]),
                    a_up=_pad_rows(rw_a_up[l], LANES).astype(BF16),
                    g_up=_pad_rows(rw_g_up[l], 2 * LANES).astype(BF16),
                    k_k=row(rw_k_k[l]), k_a=row(rw_k_a[l]), r_k=row(rw_r_k[l]), seg=seg64)
        if l > 0:
            rw_p.update(v0=row(rw_v0[l - 1]), v_up=_pad_rows(rw_v_up[l - 1], LANES).astype(BF16))
        yn, bonus, g_rw, v_cur = rwkv_branch(h, w_rw, rw_p, v_first)
        if l == 0:
            v_first = v_cur

        w_ssd = jnp.concatenate([w_s[:, SSD_WIDTH:SSD_WIDTH + SSD_XBC], w_s[:, :SSD_WIDTH],
                                 _pad_cols(w_s[:, SSD_WIDTH + SSD_XBC:], LANES)], axis=1).astype(BF16)
        A = -jnp.exp(ssd_a_log[l].astype(F32))
        ssd_p = dict(conv_w=ssd_conv_w[l], conv_b=row(ssd_conv_b[l]),
                     dt_bias=_pad_cols(row(ssd_dt_bias[l]), LANES), dt_biasT=ssd_dt_bias[l].reshape(-1, 1),
                     A=_pad_cols(row(A), LANES), AT=A.reshape(-1, 1), expand=expand,
                     d_skip=row(jnp.repeat(ssd_d[l], SSD_HEAD_DIM)), norm_w=row(ssd_norm[l]))
        y_ssd = ssd_branch(h, w_ssd, ssd_p)

        w_fox = jnp.concatenate([w_f[:, :3 * FOX_WIDTH], _pad_cols(w_f[:, 3 * FOX_WIDTH:], LANES)],
                                axis=1).astype(BF16)
        bound2 = (1.02 * FOX_HEAD_DIM ** 0.5 * LOG2E) * jnp.max(jnp.abs(fox_q_gain[l])) * jnp.max(jnp.abs(fox_k_gain[l]))
        fox_p = dict(f_bias=_pad_cols(row(fox_f_bias[l]), LANES),
                     q_gain=row(jnp.tile(fox_q_gain[l], FOX_HEADS)) * (FOX_HEAD_DIM ** -0.5 * LOG2E),
                     k_gain=row(jnp.tile(fox_k_gain[l], FOX_HEADS)), seg=seg64, bound2=bound2)
        y_fox = fox_branch(h, w_fox, fox_p)

        gate_logits = matmul(h, w_g.astype(BF16))
        mp = dict(gate_b=row(gate_b[l]), lnx_w=row(rw_lnx_w[l]), lnx_b=row(rw_lnx_b[l]),
                  proj_rw=proj_rw[l].astype(BF16), proj_ssd=proj_ssd[l].astype(BF16),
                  proj_fox=proj_fox[l].astype(BF16), w_out=w_out[l].astype(BF16), g_m=g_m)
        xs = merge(xs, yn, bonus, g_rw, y_ssd, y_fox, gate_logits, mp)

        if l % 2 == 0:
            hf = norm_mod(xs, row(norm_ffn[l]), sc_f, sh_f)
            xs = ffn_dense(hf, xs, ffn_w_gu[l // 2].astype(BF16), ffn_w_down[l // 2].astype(BF16), g_f)
        else:
            hf, comb = norm_router(xs, row(norm_ffn[l]), sc_f, sh_f, _pad_cols(moe_router[l // 2], LANES))
            xs = ffn_moe(hf, xs, comb, moe_w_gu[l // 2].astype(BF16), moe_w_down[l // 2].astype(BF16), g_f)
    return xs[None]
```

```python
import functools
import math

import jax
import jax.numpy as jnp
from jax import lax
from jax.experimental import pallas as pl
from jax.experimental.pallas import tpu as pltpu

F32 = jnp.float32
BF16 = jnp.bfloat16
HIGHEST = lax.Precision.HIGHEST

D_MODEL = 1024
LANES = 128
SUBLANES = 8

RW_HEADS = 8
RW_HEAD_DIM = 64
RW_WIDTH = RW_HEADS * RW_HEAD_DIM
RW_DECAY_LORA = 64
RW_AAA_LORA = 64
RW_VRES_LORA = 32
RW_GATE_LORA = 160
GN_EPS = 64e-5
RW_OFF_WLO = 3 * RW_WIDTH
RW_OFF_ALO = RW_OFF_WLO + LANES
RW_OFF_GLO = RW_OFF_ALO + LANES
RW_OFF_HV = RW_OFF_GLO + 2 * LANES
RW_PAD_COLS = RW_OFF_HV + LANES
RW_CHUNK = 64

SSD_HEADS = 16
SSD_HEAD_DIM = 64
SSD_WIDTH = SSD_HEADS * SSD_HEAD_DIM
SSD_GROUPS = 4
SSD_STATE = 128
SSD_CONV = 4
SSD_CHUNK = 128
SSD_XBC = SSD_WIDTH + 2 * SSD_GROUPS * SSD_STATE
SSD_GROUP_WIDTH = SSD_WIDTH // SSD_GROUPS

FOX_HEADS = 8
FOX_HEAD_DIM = 64
FOX_WIDTH = FOX_HEADS * FOX_HEAD_DIM
FOX_TQ = 256
FOX_TK = 512
LOG2E = 1.4426950408889634

FFN_DENSE = 2816
N_EXPERTS = 8
FFN_EXPERT = 3584
EPS = 1e-6

VMEM_LIMIT = 56 * 1024 * 1024


def _cparams(*sem):
    return pltpu.CompilerParams(dimension_semantics=sem, vmem_limit_bytes=VMEM_LIMIT)


def _sigmoid(x):
    return 1.0 / (1.0 + jnp.exp(-x))


def _softplus(x):
    return jnp.maximum(x, 0.0) + jnp.log(1.0 + jnp.exp(-jnp.abs(x)))


def _dot(a, b):
    return jnp.dot(a, b, preferred_element_type=F32)


def _dot_bf16(a, b):
    return jnp.dot(a.astype(BF16), b.astype(BF16), preferred_element_type=F32)


def _dot_hi(a, b):
    return jnp.dot(a, b, precision=HIGHEST, preferred_element_type=F32)


def _dot_nt(a, b, precision=None):
    return lax.dot_general(a, b, (((1,), (1,)), ((), ())), precision=precision, preferred_element_type=F32)


def _dot_tn(a, b, precision=None):
    return lax.dot_general(a, b, (((0,), (0,)), ((), ())), precision=precision, preferred_element_type=F32)


def _split_dot(x, ones_bf16):
    hi = x.astype(BF16)
    lo = (x - hi.astype(F32)).astype(BF16)
    return _dot(hi, ones_bf16) + _dot(lo, ones_bf16)


def _split3(x):
    hi = x.astype(BF16)
    r1 = x - hi.astype(F32)
    mid = r1.astype(BF16)
    return hi, mid, (r1 - mid.astype(F32)).astype(BF16)


def _tril(n, strict=False):
    r = lax.broadcasted_iota(jnp.int32, (n, n), 0)
    c = lax.broadcasted_iota(jnp.int32, (n, n), 1)
    return (r > c) if strict else (r >= c)


def _mod_kernel(c_ref, w_ref, b_ref, o_ref):
    c = c_ref[...]
    o_ref[...] = _dot_hi(c * _sigmoid(c), w_ref[...]) + b_ref[...]


def adaln_mod(c, w, b):
    d, n = w.shape
    tn = 1024
    c8 = jnp.broadcast_to(c, (SUBLANES, d))
    out = pl.pallas_call(
        _mod_kernel,
        grid=(n // tn,),
        in_specs=[pl.BlockSpec((SUBLANES, d), lambda j: (0, 0)),
                  pl.BlockSpec((d, tn), lambda j: (0, j)),
                  pl.BlockSpec((1, tn), lambda j: (0, j))],
        out_specs=pl.BlockSpec((SUBLANES, tn), lambda j: (0, j)),
        out_shape=jax.ShapeDtypeStruct((SUBLANES, n), F32),
        compiler_params=_cparams("arbitrary"),
    )(c8, w, b.reshape(1, n))
    return out[:1]


def _norm_mod(x, gain, sc, sh):
    y = x * lax.rsqrt(jnp.mean(x * x, axis=-1, keepdims=True) + EPS)
    return y * gain * (1.0 + sc) + sh


def _norm_kernel(x_ref, gain_ref, sc_ref, sh_ref, h_ref):
    h_ref[...] = _norm_mod(x_ref[...], gain_ref[...], sc_ref[...], sh_ref[...]).astype(h_ref.dtype)


def norm_mod(x, gain, sc, sh, tm=512):
    L, d = x.shape
    row = pl.BlockSpec((1, d), lambda i: (0, 0))
    return pl.pallas_call(
        _norm_kernel,
        grid=(L // tm,),
        in_specs=[pl.BlockSpec((tm, d), lambda i: (i, 0)), row, row, row],
        out_specs=pl.BlockSpec((tm, d), lambda i: (i, 0)),
        out_shape=jax.ShapeDtypeStruct((L, d), BF16),
        compiler_params=_cparams("arbitrary"),
    )(x, gain, sc, sh)


def _mm_kernel(x_ref, w_ref, o_ref):
    o_ref[...] = _dot(x_ref[...], w_ref[...]).astype(o_ref.dtype)


def matmul(x, w, out_dtype=F32, tm=512):
    M, K = x.shape
    N = w.shape[1]
    return pl.pallas_call(
        _mm_kernel,
        grid=(M // tm,),
        in_specs=[pl.BlockSpec((tm, K), lambda i: (i, 0)),
                  pl.BlockSpec((K, N), lambda i: (0, 0))],
        out_specs=pl.BlockSpec((tm, N), lambda i: (i, 0)),
        out_shape=jax.ShapeDtypeStruct((M, N), out_dtype),
        compiler_params=_cparams("arbitrary"),
    )(x, w)


def _rw_pre_kernel(has_vres, *refs):
    if has_vres:
        (h_ref, w_ref, mu_ref, w0_ref, wup_ref, a0_ref, aup_ref, gup_ref, kk_ref, ka_ref, rk_ref, seg_ref,
         vfirst_ref, v0_ref, vup_ref,
         r_out, k_out, v_out, lw_out, a_out, b_out, g_out, bonus_out, ext_scr) = refs
    else:
        (h_ref, w_ref, mu_ref, w0_ref, wup_ref, a0_ref, aup_ref, gup_ref, kk_ref, ka_ref, rk_ref, seg_ref,
         r_out, k_out, v_out, lw_out, a_out, b_out, g_out, bonus_out, ext_scr) = refs
    T = h_ref.shape[0]

    @pl.when(pl.program_id(0) == 0)
    def _():
        ext_scr[0:SUBLANES, :] = jnp.zeros((SUBLANES, ext_scr.shape[1]), F32)

    cur = _dot(h_ref[...], w_ref[...])
    ext_scr[SUBLANES:SUBLANES + T, :] = cur
    prev = ext_scr[SUBLANES - 1:SUBLANES - 1 + T, :]
    ext_scr[0:SUBLANES, :] = cur[T - SUBLANES:T, :]
    s = cur + (prev - cur) * mu_ref[...]

    W = RW_WIDTH
    r = s[:, 0:W]
    k = s[:, W:2 * W]
    v = s[:, 2 * W:3 * W]
    w_lo = s[:, RW_OFF_WLO:RW_OFF_WLO + LANES]
    a_lo = s[:, RW_OFF_ALO:RW_OFF_ALO + LANES]
    g_lo = s[:, RW_OFF_GLO:RW_OFF_GLO + 2 * LANES]
    seg = seg_ref[...]

    wlog = -_softplus(-(w0_ref[...] + _dot_bf16(jnp.tanh(w_lo), wup_ref[...]))) - 0.5
    a = _sigmoid(a0_ref[...] + _dot_bf16(a_lo, aup_ref[...]))
    g = _dot_bf16(_sigmoid(g_lo), gup_ref[...])
    if has_vres:
        hv = s[:, RW_OFF_HV:RW_OFF_HV + LANES]
        v = v + (vfirst_ref[...].astype(F32) - v) * _sigmoid(v0_ref[...] + _dot_bf16(hv, vup_ref[...]))
    kk = k * kk_ref[...]
    kk = kk / jnp.maximum(jnp.sqrt(_split_dot(kk * kk, seg)), 1e-12)
    k = k * (1.0 + (a - 1.0) * ka_ref[...])
    r_out[...] = r.astype(r_out.dtype)
    k_out[...] = k.astype(k_out.dtype)
    v_out[...] = v.astype(v_out.dtype)
    lw_out[...] = -jnp.exp(wlog)
    a_out[...] = (-kk).astype(a_out.dtype)
    b_out[...] = (kk * a).astype(b_out.dtype)
    g_out[...] = g.astype(g_out.dtype)
    bonus_out[...] = (_split_dot(r * k * rk_ref[...], seg) * v).astype(bonus_out.dtype)


def rw_pre(h, w_rw, p, v_first, tm=512):
    L, D = h.shape
    has_vres = v_first is not None
    full = lambda a: pl.BlockSpec(a.shape, lambda i: (0,) * a.ndim)
    rows = lambda n: pl.BlockSpec((tm, n), lambda i: (i, 0))
    args = [h, w_rw, p["mu"], p["w0"], p["w_up"], p["a0"], p["a_up"], p["g_up"], p["k_k"], p["k_a"], p["r_k"],
            p["seg"]]
    specs = [rows(D)] + [full(a) for a in args[1:]]
    if has_vres:
        args += [v_first, p["v0"], p["v_up"]]
        specs += [rows(RW_WIDTH), full(p["v0"]), full(p["v_up"])]
    out_dtypes = [BF16, BF16, BF16, F32, BF16, BF16, BF16, BF16]
    return pl.pallas_call(
        functools.partial(_rw_pre_kernel, has_vres),
        grid=(L // tm,),
        in_specs=specs,
        out_specs=[rows(RW_WIDTH)] * 8,
        out_shape=[jax.ShapeDtypeStruct((L, RW_WIDTH), dt) for dt in out_dtypes],
        scratch_shapes=[pltpu.VMEM((tm + SUBLANES, RW_PAD_COLS), F32)],
        compiler_params=_cparams("arbitrary"),
    )(*args)


def _wkv_kernel(r_ref, lw_ref, k_ref, v_ref, a_ref, b_ref, y_ref, h_scr):
    C = RW_CHUNK
    C2 = 2 * C
    n_chunks = r_ref.shape[0] // C

    @pl.when(pl.program_id(1) == 0)
    def _():
        h_scr[...] = jnp.zeros_like(h_scr)

    is_a = lax.broadcasted_iota(jnp.int32, (C, LANES), 1) < RW_HEAD_DIM
    ri = lax.broadcasted_iota(jnp.int32, (C2, C2), 0)
    ci = lax.broadcasted_iota(jnp.int32, (C2, C2), 1)
    same = (ri >= C) == (ci >= C)
    strict = jnp.logical_and(same, ri > ci)
    incl = jnp.logical_and(same, ri >= ci)
    eye2 = (ri == ci).astype(F32)
    tri_c = _tril(C).astype(BF16)

    def two(x):
        return jnp.concatenate([jnp.where(is_a, x, 0.0), jnp.where(is_a, 0.0, x)], axis=0)

    chunks = range(n_chunks)
    at2, rt2, bh2, kh2, v2, n_ab, a_ak, m_rb, m_rk, p_end = ([] for _ in range(10))
    for c in chunks:
        sl = pl.ds(c * C, C)
        lw = lw_ref[sl, :]
        lw_hi = lw.astype(BF16)
        lw_lo = (lw - lw_hi.astype(F32)).astype(BF16)
        cs = _dot(tri_c, lw_hi) + _dot(tri_c, lw_lo)
        cs_end = cs[C - 1:C, :]
        e_neg = jnp.exp(-cs)
        e_end = jnp.exp(cs_end - cs)
        a = a_ref[sl, :].astype(F32)
        b = b_ref[sl, :].astype(F32)
        k = k_ref[sl, :].astype(F32)
        at2.append(two(a * jnp.exp(cs - lw)))
        rt2.append(two(r_ref[sl, :].astype(F32) * jnp.exp(cs)))
        bh2.append(two(b * e_end))
        kh2.append(two(k * e_end))
        v2.append(two(v_ref[sl, :].astype(F32)))
        p_end.append(jnp.exp(cs_end))
        quad = _dot_nt(jnp.concatenate([at2[c], rt2[c]], axis=0).astype(BF16),
                       jnp.concatenate([two(b * e_neg), two(k * e_neg)], axis=0).astype(BF16))
        n_ab.append(jnp.where(strict, quad[0:C2, 0:C2], 0.0))
        a_ak.append(jnp.where(strict, quad[0:C2, C2:], 0.0))
        m_rb.append(jnp.where(incl, quad[C2:, 0:C2], 0.0))
        m_rk.append(jnp.where(incl, quad[C2:, C2:], 0.0))
    t_inv = [eye2 + n for n in n_ab]
    n_pow = n_ab
    for _ in range(int(math.log2(C)) - 1):
        n_pow = [_dot_bf16(n, n) for n in n_pow]
        t_inv = [t + _dot_bf16(t, n) for t, n in zip(t_inv, n_pow)]
    akv = [_dot_bf16(a_ak[c], v2[c]) for c in chunks]
    au = [_dot_bf16(t_inv[c], jnp.concatenate([at2[c], akv[c]], axis=1)) for c in chunks]
    mau = [_dot_bf16(m_rb[c], au[c]) for c in chunks]
    mkv = [_dot_bf16(m_rk[c], v2[c]) for c in chunks]
    gu = [_dot_tn(bh2[c].astype(BF16), au[c].astype(BF16)) for c in chunks]
    kv = [_dot_tn(kh2[c].astype(BF16), v2[c].astype(BF16)) for c in chunks]
    inv_n = 1.0 / RW_HEAD_DIM
    h = h_scr[...]
    for c in chunks:
        rh2 = rt2[c] + mau[c][:, :LANES]
        g_mat = eye2 * p_end[c] + gu[c][:, :LANES]
        yh = _dot_bf16(jnp.concatenate([rh2, g_mat], axis=0), h)
        h = yh[C2:, :] + gu[c][:, LANES:] + kv[c]
        y2 = yh[:C2, :] + mau[c][:, LANES:] + mkv[c]
        y = y2[:C, :] + y2[C:, :]
        s_all = jnp.sum(y, axis=-1, keepdims=True)
        s_a = jnp.sum(jnp.where(is_a, y, 0.0), axis=-1, keepdims=True)
        yc = y - jnp.where(is_a, s_a, s_all - s_a) * inv_n
        sq = yc * yc
        q_all = jnp.sum(sq, axis=-1, keepdims=True)
        q_a = jnp.sum(jnp.where(is_a, sq, 0.0), axis=-1, keepdims=True)
        yn = yc * lax.rsqrt(jnp.where(is_a, q_a, q_all - q_a) * inv_n + GN_EPS)
        y_ref[pl.ds(c * C, C), :] = yn.astype(y_ref.dtype)
    h_scr[...] = h


def wkv7(r, lw, k, v, a, b, rows=512):
    L, W = r.shape
    spec = pl.BlockSpec((rows, LANES), lambda p, i: (i, p))
    return pl.pallas_call(
        _wkv_kernel,
        grid=(W // LANES, L // rows),
        in_specs=[spec] * 6,
        out_specs=spec,
        out_shape=jax.ShapeDtypeStruct((L, W), BF16),
        scratch_shapes=[pltpu.VMEM((LANES, LANES), F32)],
        compiler_params=_cparams("arbitrary", "arbitrary"),
    )(r, lw, k, v, a, b)


def _ssd_kernel(h_ref, w_ref, cw_ref, cb_ref, bias_ref, A_ref, Ax_ref, exp_ref, dskip_ref, nw_ref, o_ref,
                ext_scr, xbc_scr, s_scr):
    Q = SSD_CHUNK
    GW = SSD_GROUP_WIDTH
    NS = SSD_STATE
    R = h_ref.shape[0]

    @pl.when(pl.program_id(0) == 0)
    def _():
        s_scr[...] = jnp.zeros_like(s_scr)
        ext_scr[0:SUBLANES, :] = jnp.zeros((SUBLANES, ext_scr.shape[1]), F32)

    cols = _dot(h_ref[...], w_ref[...])
    ext_scr[SUBLANES:SUBLANES + R, :] = cols[:, 0:SSD_XBC]
    acc = cols[:, 0:SSD_XBC] * cw_ref[SSD_CONV - 1:SSD_CONV, :] + cb_ref[...]
    for j in range(1, SSD_CONV):
        acc = acc + ext_scr[SUBLANES - j:SUBLANES - j + R, :] * cw_ref[SSD_CONV - 1 - j:SSD_CONV - j, :]
    ext_scr[0:SUBLANES, :] = ext_scr[R:R + SUBLANES, :]
    xbc_scr[...] = acc * _sigmoid(acc)

    incl = _tril(Q)
    tri = incl.astype(BF16)
    triu = jnp.logical_not(_tril(Q, strict=True)).astype(BF16)
    expand = exp_ref[...]
    first_half = lax.broadcasted_iota(jnp.int32, (Q, LANES), 1) < SSD_HEAD_DIM
    heads_per_group = SSD_HEADS // SSD_GROUPS
    for c in range(R // Q):
        rows = slice(c * Q, (c + 1) * Q)
        xs = xbc_scr[rows, 0:SSD_WIDTH]
        z = cols[rows, SSD_XBC:SSD_XBC + SSD_WIDTH]
        dt = _softplus(cols[rows, SSD_XBC + SSD_WIDTH:] + bias_ref[...])
        a3 = _split3(dt * A_ref[...])
        a_cum = sum(_dot(tri, t) for t in a3)
        a_cumT = sum(_dot_tn(t, triu) for t in a3)
        dt_x = sum(_dot(t, expand) for t in _split3(dt))
        acum_x = sum(_dot(tri, t) for t in _split3(dt_x * Ax_ref[...]))
        eac_x = jnp.exp(acum_x)
        dte_x = jnp.exp(acum_x[Q - 1:Q, :] - acum_x)
        xdt = xs * dt_x
        for g in range(SSD_GROUPS):
            Bg = xbc_scr[rows, SSD_WIDTH + g * NS:SSD_WIDTH + (g + 1) * NS].astype(BF16)
            Cg = xbc_scr[rows, SSD_WIDTH + (SSD_GROUPS + g) * NS:SSD_WIDTH + (SSD_GROUPS + g + 1) * NS].astype(BF16)
            cb = _dot_nt(Cg, Bg)
            gs = slice(g * GW, (g + 1) * GW)
            s_prev = s_scr[:, gs]
            y_g = _dot_bf16(Cg, s_prev) * eac_x[:, gs]
            pieces = []
            for pr in range(heads_per_group // 2):
                ps = slice(g * GW + pr * LANES, g * GW + (pr + 1) * LANES)
                xdt_p = xdt[:, ps]
                acc_p = None
                for e in range(2):
                    hd = g * heads_per_group + pr * 2 + e
                    seg = a_cum[:, hd:hd + 1] - a_cumT[hd:hd + 1, :]
                    m = cb * jnp.exp(jnp.where(incl, seg, -jnp.inf))
                    xm = jnp.where(first_half if e == 0 else jnp.logical_not(first_half), xdt_p, 0.0)
                    t = _dot_bf16(m, xm)
                    acc_p = t if acc_p is None else acc_p + t
                pieces.append(acc_p)
            y_g = y_g + jnp.concatenate(pieces, axis=1)
            s_scr[:, gs] = s_prev * eac_x[Q - 1:Q, gs] + _dot_tn(Bg, (dte_x[:, gs] * xdt[:, gs]).astype(BF16))
            y_g = y_g + dskip_ref[:, gs] * xs[:, gs]
            zg = z[:, gs]
            y_g = y_g * (zg * _sigmoid(zg))
            y_g = y_g * lax.rsqrt(jnp.mean(y_g * y_g, axis=-1, keepdims=True) + EPS)
            o_ref[rows, gs] = (y_g * nw_ref[:, gs]).astype(o_ref.dtype)


def ssd_mixer(h, w_ssd, p, rows=512):
    L, D = h.shape
    N = w_ssd.shape[1]
    full = lambda a: pl.BlockSpec(a.shape, lambda i: (0,) * a.ndim)
    small = [p["conv_w"], p["conv_b"], p["dt_bias"], p["A"], p["A_x"], p["expand"], p["d_skip"], p["norm_w"]]
    return pl.pallas_call(
        _ssd_kernel,
        grid=(L // rows,),
        in_specs=[pl.BlockSpec((rows, D), lambda i: (i, 0)), full(w_ssd)] + [full(a) for a in small],
        out_specs=pl.BlockSpec((rows, SSD_WIDTH), lambda i: (i, 0)),
        out_shape=jax.ShapeDtypeStruct((L, SSD_WIDTH), BF16),
        scratch_shapes=[pltpu.VMEM((rows + SUBLANES, SSD_XBC), F32), pltpu.VMEM((rows, SSD_XBC), F32),
                        pltpu.VMEM((SSD_STATE, SSD_WIDTH), F32)],
        compiler_params=_cparams("arbitrary"),
    )(h, w_ssd, *small)


def _fox_pre_kernel(h_ref, w_ref, fb_ref, qg_ref, kg_ref, seg_ref, q_out, k_out, v_out, f_out, carry_scr):
    T = h_ref.shape[0]
    W = FOX_WIDTH

    @pl.when(pl.program_id(0) == 0)
    def _():
        carry_scr[...] = jnp.zeros_like(carry_scr)

    cols = _dot(h_ref[...], w_ref[...])
    seg = seg_ref[...]
    q = cols[:, 0:W]
    k = cols[:, W:2 * W]
    inv_d = 1.0 / FOX_HEAD_DIM
    qn = q * lax.rsqrt(_split_dot(q * q, seg) * inv_d + EPS) * qg_ref[...]
    kn = k * lax.rsqrt(_split_dot(k * k, seg) * inv_d + EPS) * kg_ref[...]
    q_out[...] = qn.astype(q_out.dtype)
    k_out[...] = kn.astype(k_out.dtype)
    v_out[...] = cols[:, 2 * W:3 * W].astype(v_out.dtype)
    f = cols[:, 3 * W:3 * W + LANES] + fb_ref[...]
    ls = -_softplus(-f)
    cum = _dot_hi(_tril(T).astype(F32), ls) + carry_scr[0:1, :]
    f_out[...] = cum * LOG2E
    carry_scr[...] = jnp.broadcast_to(cum[T - 1:T, :], carry_scr.shape)


def fox_pre(h, w_fox, p, tm=512):
    L, D = h.shape
    full = lambda a: pl.BlockSpec(a.shape, lambda i: (0,) * a.ndim)
    rows = lambda n: pl.BlockSpec((tm, n), lambda i: (i, 0))
    small = [w_fox, p["f_bias"], p["q_gain"], p["k_gain"], p["seg"]]
    return pl.pallas_call(
        _fox_pre_kernel,
        grid=(L // tm,),
        in_specs=[rows(D)] + [full(a) for a in small],
        out_specs=[rows(FOX_WIDTH)] * 3 + [rows(LANES)],
        out_shape=[jax.ShapeDtypeStruct((L, FOX_WIDTH), BF16)] * 3 + [jax.ShapeDtypeStruct((L, LANES), F32)],
        scratch_shapes=[pltpu.VMEM((SUBLANES, LANES), F32)],
        compiler_params=_cparams("arbitrary"),
    )(h, *small)


def _fox_kernel(lo_ref, q_ref, k_ref, v_ref, fk_ref, o_ref, m_scr, l_scr, acc_scr):
    tq = q_ref.shape[0]
    tk = FOX_TK
    pair = pl.program_id(0)
    qi = pl.program_id(1)
    is_a = lax.broadcasted_iota(jnp.int32, (tq, LANES), 1) < FOX_HEAD_DIM
    q = q_ref[...]
    zero = jnp.zeros_like(q)
    qs = (jnp.where(is_a, q, zero), jnp.where(is_a, zero, q))
    m_scr[...] = jnp.full(m_scr.shape, -jnp.inf, F32)
    l_scr[...] = jnp.zeros_like(l_scr)
    acc_scr[...] = jnp.zeros_like(acc_scr)
    rel = lax.broadcasted_iota(jnp.int32, (tq, tk), 0) - lax.broadcasted_iota(jnp.int32, (tq, tk), 1)
    last = (qi * tq) // tk

    def block(kb, diagonal):
        ks = pl.ds(pl.multiple_of(kb * tk, tk), tk)
        k_blk = k_ref[ks, :]
        v_blk = v_ref[ks, :]
        for e in range(2):
            s = _dot_nt(qs[e], k_blk) - fk_ref[pl.ds(2 * pair + e, 1), ks]
            if diagonal:
                s = jnp.where(rel >= kb * tk - qi * tq, s, -jnp.inf)
            m_old = m_scr[e]
            m_new = jnp.maximum(m_old, jnp.max(s, axis=-1, keepdims=True))
            pexp = jnp.exp2(s - m_new)
            alpha = jnp.exp2(m_old - m_new)
            l_scr[e] = alpha * l_scr[e] + jnp.sum(pexp, axis=-1, keepdims=True)
            acc_scr[e] = alpha * acc_scr[e] + _dot(pexp.astype(BF16), v_blk)
            m_scr[e] = m_new

    block(last, True)

    def body(i, carry):
        block(last - 1 - i, False)
        return carry

    lo = jnp.minimum(lo_ref[2 * pair, qi], lo_ref[2 * pair + 1, qi])
    lax.fori_loop(0, last - lo, body, 0)
    o_ref[...] = jnp.where(is_a, acc_scr[0] / l_scr[0], acc_scr[1] / l_scr[1]).astype(o_ref.dtype)


def fox_attention(q, k, v, FT, lo):
    L, W = q.shape
    H = FT.shape[0]
    tq = FOX_TQ
    grid_spec = pltpu.PrefetchScalarGridSpec(
        num_scalar_prefetch=1,
        grid=(W // LANES, L // tq),
        in_specs=[pl.BlockSpec((tq, LANES), lambda p, i, lo_r: (i, p)),
                  pl.BlockSpec((L, LANES), lambda p, i, lo_r: (0, p)),
                  pl.BlockSpec((L, LANES), lambda p, i, lo_r: (0, p)),
                  pl.BlockSpec((H, L), lambda p, i, lo_r: (0, 0))],
        out_specs=pl.BlockSpec((tq, LANES), lambda p, i, lo_r: (i, p)),
        scratch_shapes=[pltpu.VMEM((2, tq, 1), F32), pltpu.VMEM((2, tq, 1), F32),
                        pltpu.VMEM((2, tq, LANES), F32)],
    )
    return pl.pallas_call(
        _fox_kernel,
        grid_spec=grid_spec,
        out_shape=jax.ShapeDtypeStruct((L, W), BF16),
        compiler_params=_cparams("arbitrary", "arbitrary"),
    )(lo, q, k, v, FT)


def fox_first_block(FT, bound2):
    f_first = FT[:, ::FOX_TQ]
    f_last = FT[:, FOX_TK - 1::FOX_TK]
    gap = f_first[:, :, None] - f_last[:, None, :] + 2.0 * bound2
    last = (jnp.arange(f_first.shape[1]) * FOX_TQ) // FOX_TK
    needed = jnp.logical_or(gap >= -152.0, jnp.arange(f_last.shape[1])[None, None, :] >= last[None, :, None])
    return jnp.argmax(needed, axis=-1).astype(jnp.int32)


def _top2_combine(h, wr):
    T = h.shape[0]
    lane = lax.broadcasted_iota(jnp.int32, (T, LANES), 1)
    logits = jnp.where(lane < N_EXPERTS, _dot_hi(h, wr), -jnp.inf)
    m1 = jnp.max(logits, axis=-1, keepdims=True)
    i1 = jnp.min(jnp.where(logits == m1, lane, LANES), axis=-1, keepdims=True)
    rest = jnp.where(lane == i1, -jnp.inf, logits)
    m2 = jnp.max(rest, axis=-1, keepdims=True)
    i2 = jnp.min(jnp.where(rest == m2, lane, LANES), axis=-1, keepdims=True)
    e2 = jnp.exp(m2 - m1)
    return jnp.where(lane == i1, 1.0 / (1.0 + e2), 0.0) + jnp.where(lane == i2, e2 / (1.0 + e2), 0.0)


def _merge_kernel(with_router, *refs):
    (x_ref, yn_ref, bonus_ref, g_ref, yssd_ref, yfox_ref, gl_ref, gb_ref, lnw_ref, lnb_ref,
     prw_ref, pssd_ref, pfox_ref, wout_ref, gm_ref, gain_ref, sc_ref, sh_ref) = refs[:18]
    D = D_MODEL
    y_rw = (yn_ref[...] * lnw_ref[...] + lnb_ref[...] + bonus_ref[...]) * g_ref[...]
    gates = _sigmoid(gl_ref[...] + gb_ref[...])
    merged = gates[:, 0:D] * _dot_bf16(y_rw, prw_ref[...])
    merged = merged + gates[:, D:2 * D] * _dot_bf16(yssd_ref[...], pssd_ref[...])
    merged = merged + gates[:, 2 * D:3 * D] * _dot_bf16(yfox_ref[...], pfox_ref[...])
    x_new = x_ref[...] + gm_ref[...] * _dot_bf16(merged, wout_ref[...])
    hf = _norm_mod(x_new, gain_ref[...], sc_ref[...], sh_ref[...])
    if with_router:
        wr_ref, o_ref, h_ref, comb_ref = refs[18:]
        comb_ref[...] = _top2_combine(hf, wr_ref[...])
    else:
        o_ref, h_ref = refs[18:]
    o_ref[...] = x_new
    h_ref[...] = hf.astype(h_ref.dtype)


def merge(x, yn, bonus, g, y_ssd, y_fox, gate_logits, p, router=None, tm=512):
    L, D = x.shape
    full = lambda a: pl.BlockSpec(a.shape, lambda i: (0,) * a.ndim)
    rows = lambda n: pl.BlockSpec((tm, n), lambda i: (i, 0))
    small = [p["gate_b"], p["lnx_w"], p["lnx_b"], p["proj_rw"], p["proj_ssd"], p["proj_fox"], p["w_out"], p["g_m"],
             p["norm_gain"], p["sc_f"], p["sh_f"]]
    out_specs = [rows(D), rows(D)]
    out_shape = [jax.ShapeDtypeStruct((L, D), F32), jax.ShapeDtypeStruct((L, D), BF16)]
    if router is not None:
        small.append(router)
        out_specs.append(rows(LANES))
        out_shape.append(jax.ShapeDtypeStruct((L, LANES), F32))
    return pl.pallas_call(
        functools.partial(_merge_kernel, router is not None),
        grid=(L // tm,),
        in_specs=[rows(D), rows(RW_WIDTH), rows(RW_WIDTH), rows(RW_WIDTH), rows(SSD_WIDTH), rows(FOX_WIDTH),
                  rows(3 * D)] + [full(a) for a in small],
        out_specs=out_specs,
        out_shape=out_shape,
        compiler_params=_cparams("arbitrary"),
    )(x, yn, bonus, g, y_ssd, y_fox, gate_logits, *small)


def _ffn_kernel(with_next, *refs):
    h_ref, x_ref, wg_ref, wu_ref, wd_ref, gf_ref = refs[:6]
    j = pl.program_id(1)
    acc_scr = refs[-1]

    @pl.when(j == 0)
    def _():
        acc_scr[...] = jnp.zeros_like(acc_scr)

    h = h_ref[...]
    gte = _dot(h, wg_ref[...])
    up = _dot(h, wu_ref[...])
    act = gte * _sigmoid(gte) * up
    acc_scr[...] += _dot(act.astype(BF16), wd_ref[...])

    @pl.when(j == pl.num_programs(1) - 1)
    def _():
        x_new = x_ref[...] + gf_ref[...] * acc_scr[...]
        if with_next:
            gain_ref, sc_ref, sh_ref, o_ref, hn_ref = refs[6:11]
            hn_ref[...] = _norm_mod(x_new, gain_ref[...], sc_ref[...], sh_ref[...]).astype(hn_ref.dtype)
        else:
            o_ref = refs[6]
        o_ref[...] = x_new


def ffn_dense(h, x, w_gu, w_down, g_f, next_norm=None, tm=1024, tf=256):
    L, D = x.shape
    Fh = w_down.shape[0]
    nf = Fh // tf
    row = pl.BlockSpec((1, D), lambda i, j: (0, 0))
    tile = pl.BlockSpec((tm, D), lambda i, j: (i, 0))
    args = [h, x, w_gu, w_gu, w_down, g_f]
    in_specs = [tile, tile,
                pl.BlockSpec((D, tf), lambda i, j: (0, j)),
                pl.BlockSpec((D, tf), lambda i, j: (0, j + nf)),
                pl.BlockSpec((tf, D), lambda i, j: (j, 0)), row]
    out_specs = [tile]
    out_shape = [jax.ShapeDtypeStruct((L, D), F32)]
    if next_norm is not None:
        args += list(next_norm)
        in_specs += [row, row, row]
        out_specs.append(tile)
        out_shape.append(jax.ShapeDtypeStruct((L, D), BF16))
    out = pl.pallas_call(
        functools.partial(_ffn_kernel, next_norm is not None),
        grid=(L // tm, nf),
        in_specs=in_specs,
        out_specs=out_specs,
        out_shape=out_shape,
        scratch_shapes=[pltpu.VMEM((tm, D), F32)],
        compiler_params=_cparams("arbitrary", "arbitrary"),
    )(*args)
    return out if next_norm is not None else (out[0], None)


def _moe_kernel(h_ref, x_ref, comb_ref, wg_ref, wu_ref, wd_ref, gf_ref, o_ref, acc_scr):
    e = pl.program_id(1)
    j = pl.program_id(2)

    @pl.when(jnp.logical_and(e == 0, j == 0))
    def _():
        acc_scr[...] = jnp.zeros_like(acc_scr)

    h = h_ref[...]
    lane = lax.broadcasted_iota(jnp.int32, comb_ref.shape, 1)
    ce = jnp.sum(jnp.where(lane == e, comb_ref[...], 0.0), axis=-1, keepdims=True)
    gte = _dot(h, wg_ref[0])
    up = _dot(h, wu_ref[0])
    act = gte * _sigmoid(gte) * up
    acc_scr[...] += ce * _dot(act.astype(BF16), wd_ref[0])

    @pl.when(jnp.logical_and(e == pl.num_programs(1) - 1, j == pl.num_programs(2) - 1))
    def _():
        o_ref[...] = x_ref[...] + gf_ref[...] * acc_scr[...]


def ffn_moe(h, x, comb, w_gu, w_down, g_f, tm=1024, tf=256):
    L, D = x.shape
    E, Fh, _ = w_down.shape
    nf = Fh // tf
    return pl.pallas_call(
        _moe_kernel,
        grid=(L // tm, E, nf),
        in_specs=[pl.BlockSpec((tm, D), lambda i, e, j: (i, 0)),
                  pl.BlockSpec((tm, D), lambda i, e, j: (i, 0)),
                  pl.BlockSpec((tm, LANES), lambda i, e, j: (i, 0)),
                  pl.BlockSpec((1, D, tf), lambda i, e, j: (e, 0, j)),
                  pl.BlockSpec((1, D, tf), lambda i, e, j: (e, 0, j + nf)),
                  pl.BlockSpec((1, tf, D), lambda i, e, j: (e, j, 0)),
                  pl.BlockSpec((1, D), lambda i, e, j: (0, 0))],
        out_specs=pl.BlockSpec((tm, D), lambda i, e, j: (i, 0)),
        out_shape=jax.ShapeDtypeStruct((L, D), F32),
        scratch_shapes=[pltpu.VMEM((tm, D), F32)],
        compiler_params=_cparams("arbitrary", "arbitrary", "arbitrary"),
    )(h, x, comb, w_gu, w_gu, w_down, g_f)


def _seg_matrix(width, head_dim):
    idx = jnp.arange(width) // head_dim
    return (idx[:, None] == idx[None, :]).astype(BF16)


def _pad_cols(w, n):
    return jnp.pad(w, ((0, 0), (0, n - w.shape[1])))


def _pad_rows(w, n):
    return jnp.pad(w, ((0, n - w.shape[0]), (0, 0)))


def rwkv_branch(h, w_rw, p, v_first):
    r, k, v, lw, a, b, g, bonus = rw_pre(h, w_rw, p, v_first)
    return wkv7(r, lw, k, v, a, b), bonus, g, v


def fox_branch(h, w_fox, p):
    q, k, v, F = fox_pre(h, w_fox, p)
    FT = jnp.transpose(F[:, :FOX_HEADS])
    lo = fox_first_block(FT, p["bound2"])
    return fox_attention(q, k, v, FT, lo)


def kernel(x, c, ada_w, ada_b, norm_mix, norm_ffn, w_in, rw_mu, rw_w0, rw_w_up, rw_a0, rw_a_up, rw_g_up, rw_k_k, rw_k_a, rw_r_k, rw_lnx_w, rw_lnx_b, rw_v0, rw_v_down, rw_v_up, ssd_conv_w, ssd_conv_b, ssd_dt_bias, ssd_a_log, ssd_d, ssd_norm, fox_f_bias, fox_q_gain, fox_k_gain, gate_b, proj_rw, proj_ssd, proj_fox, w_out, ffn_w_gu, ffn_w_down, moe_router, moe_w_gu, moe_w_down):
    depth = w_in.shape[0]
    D = D_MODEL
    xs = x[0]
    row = lambda t: t.reshape(1, -1).astype(F32)
    seg64 = _seg_matrix(RW_WIDTH, RW_HEAD_DIM)
    rw_cols = 3 * RW_WIDTH + RW_DECAY_LORA + RW_AAA_LORA + RW_GATE_LORA
    ssd_cols = SSD_WIDTH + SSD_XBC + SSD_HEADS
    fox_cols = 3 * FOX_WIDTH + FOX_HEADS
    expand = (jnp.arange(LANES)[:, None] == (jnp.arange(SSD_WIDTH) // SSD_HEAD_DIM)[None, :]).astype(BF16)
    v_first = None
    mods = []
    for l in range(depth):
        mod = adaln_mod(c, ada_w[l], ada_b[l])
        mods.append([mod[:, i * D:(i + 1) * D] for i in range(6)])
    h = None
    for l in range(depth):
        sh_m, sc_m, g_m, sh_f, sc_f, g_f = mods[l]
        if h is None:
            h = norm_mod(xs, row(norm_mix[l]), sc_m, sh_m)

        wl = w_in[l]
        o = 0
        w_r = wl[:, o:o + rw_cols]; o += rw_cols
        w_s = wl[:, o:o + ssd_cols]; o += ssd_cols
        w_f = wl[:, o:o + fox_cols]; o += fox_cols
        w_g = wl[:, o:]
        W3 = 3 * RW_WIDTH
        o_a = W3 + RW_DECAY_LORA
        o_g = o_a + RW_AAA_LORA

        def rw_layout(t, hv):
            parts = [t[:, :W3], _pad_cols(t[:, W3:o_a], LANES), _pad_cols(t[:, o_a:o_g], LANES),
                     _pad_cols(t[:, o_g:], 2 * LANES), _pad_cols(hv, LANES)]
            return jnp.concatenate(parts, axis=1)

        if l == 0:
            hv_w = jnp.zeros((D, RW_VRES_LORA), F32)
        else:
            hv_w = rw_v_down[l - 1]
        w_rw = rw_layout(w_r, hv_w).astype(BF16)
        mu = rw_layout(rw_mu[l].reshape(1, -1), jnp.zeros((1, RW_VRES_LORA), F32))
        rw_p = dict(mu=mu, w0=row(rw_w0[l]), w_up=_pad_rows(rw_w_up[l], LANES).astype(BF16), a0=row(rw_a0[l]),
                    a_up=_pad_rows(rw_a_up[l], LANES).astype(BF16),
                    g_up=_pad_rows(rw_g_up[l], 2 * LANES).astype(BF16),
                    k_k=row(rw_k_k[l]), k_a=row(rw_k_a[l]), r_k=row(rw_r_k[l]), seg=seg64)
        if l > 0:
            rw_p.update(v0=row(rw_v0[l - 1]), v_up=_pad_rows(rw_v_up[l - 1], LANES).astype(BF16))
        yn, bonus, g_rw, v_cur = rwkv_branch(h, w_rw, rw_p, v_first)
        if l == 0:
            v_first = v_cur

        w_ssd = jnp.concatenate([w_s[:, SSD_WIDTH:SSD_WIDTH + SSD_XBC], w_s[:, :SSD_WIDTH],
                                 _pad_cols(w_s[:, SSD_WIDTH + SSD_XBC:], LANES)], axis=1).astype(BF16)
        A = -jnp.exp(ssd_a_log[l].astype(F32))
        ssd_p = dict(conv_w=ssd_conv_w[l], conv_b=row(ssd_conv_b[l]),
                     dt_bias=_pad_cols(row(ssd_dt_bias[l]), LANES),
                     A=_pad_cols(row(A), LANES), A_x=row(jnp.repeat(A, SSD_HEAD_DIM)), expand=expand,
                     d_skip=row(jnp.repeat(ssd_d[l], SSD_HEAD_DIM)), norm_w=row(ssd_norm[l]))
        y_ssd = ssd_mixer(h, w_ssd, ssd_p)

        w_fox = jnp.concatenate([w_f[:, :3 * FOX_WIDTH], _pad_cols(w_f[:, 3 * FOX_WIDTH:], LANES)],
                                axis=1).astype(BF16)
        bound2 = (1.02 * FOX_HEAD_DIM ** 0.5 * LOG2E) * jnp.max(jnp.abs(fox_q_gain[l])) * jnp.max(jnp.abs(fox_k_gain[l]))
        fox_p = dict(f_bias=_pad_cols(row(fox_f_bias[l]), LANES),
                     q_gain=row(jnp.tile(fox_q_gain[l], FOX_HEADS)) * (FOX_HEAD_DIM ** -0.5 * LOG2E),
                     k_gain=row(jnp.tile(fox_k_gain[l], FOX_HEADS)), seg=seg64, bound2=bound2)
        y_fox = fox_branch(h, w_fox, fox_p)

        gate_logits = matmul(h, w_g.astype(BF16), out_dtype=BF16)
        mp = dict(gate_b=row(gate_b[l]), lnx_w=row(rw_lnx_w[l]), lnx_b=row(rw_lnx_b[l]),
                  proj_rw=proj_rw[l].astype(BF16), proj_ssd=proj_ssd[l].astype(BF16),
                  proj_fox=proj_fox[l].astype(BF16), w_out=w_out[l].astype(BF16), g_m=g_m,
                  norm_gain=row(norm_ffn[l]), sc_f=sc_f, sh_f=sh_f)

        if l % 2 == 0:
            xs, hf = merge(xs, yn, bonus, g_rw, y_ssd, y_fox, gate_logits, mp)
            next_norm = None
            if l + 1 < depth:
                next_norm = (row(norm_mix[l + 1]), mods[l + 1][1], mods[l + 1][0])
            xs, h = ffn_dense(hf, xs, ffn_w_gu[l // 2].astype(BF16), ffn_w_down[l // 2].astype(BF16), g_f,
                              next_norm)
        else:
            xs, hf, comb = merge(xs, yn, bonus, g_rw, y_ssd, y_fox, gate_logits, mp,
                                 router=_pad_cols(moe_router[l // 2], LANES))
            xs = ffn_moe(hf, xs, comb, moe_w_gu[l // 2].astype(BF16), moe_w_down[l // 2].astype(BF16), g_f)
            h = None
    return xs[None]
```

```python
import functools
import math

import jax
import jax.numpy as jnp
from jax import lax
from jax.experimental import pallas as pl
from jax.experimental.pallas import tpu as pltpu

F32 = jnp.float32
BF16 = jnp.bfloat16
HIGHEST = lax.Precision.HIGHEST

D_MODEL = 1024
LANES = 128
SUBLANES = 8

RW_HEADS = 8
RW_HEAD_DIM = 64
RW_WIDTH = RW_HEADS * RW_HEAD_DIM
RW_DECAY_LORA = 64
RW_AAA_LORA = 64
RW_VRES_LORA = 32
RW_GATE_LORA = 160
GN_EPS = 64e-5
RW_OFF_WLO = 3 * RW_WIDTH
RW_OFF_ALO = RW_OFF_WLO + LANES
RW_OFF_GLO = RW_OFF_ALO + LANES
RW_OFF_HV = RW_OFF_GLO + 2 * LANES
RW_PAD_COLS = RW_OFF_HV + LANES
RW_CHUNK = 64

SSD_HEADS = 16
SSD_HEAD_DIM = 64
SSD_WIDTH = SSD_HEADS * SSD_HEAD_DIM
SSD_GROUPS = 4
SSD_STATE = 128
SSD_CONV = 4
SSD_CHUNK = 128
SSD_XBC = SSD_WIDTH + 2 * SSD_GROUPS * SSD_STATE
SSD_GROUP_WIDTH = SSD_WIDTH // SSD_GROUPS

FOX_HEADS = 8
FOX_HEAD_DIM = 64
FOX_WIDTH = FOX_HEADS * FOX_HEAD_DIM
FOX_TQ = 256
FOX_TK = 512
LOG2E = 1.4426950408889634

FFN_DENSE = 2816
N_EXPERTS = 8
FFN_EXPERT = 3584
MOE_SB = 128
EPS = 1e-6

VMEM_LIMIT = 56 * 1024 * 1024


def _cparams(*sem):
    return pltpu.CompilerParams(dimension_semantics=sem, vmem_limit_bytes=VMEM_LIMIT)


def _sigmoid(x):
    return 1.0 / (1.0 + jnp.exp(-x))


def _softplus(x):
    return jnp.maximum(x, 0.0) + jnp.log(1.0 + jnp.exp(-jnp.abs(x)))


def _dot(a, b):
    return jnp.dot(a, b, preferred_element_type=F32)


def _dot_bf16(a, b):
    return jnp.dot(a.astype(BF16), b.astype(BF16), preferred_element_type=F32)


def _dot_hi(a, b):
    return jnp.dot(a, b, precision=HIGHEST, preferred_element_type=F32)


def _dot_nt(a, b, precision=None):
    return lax.dot_general(a, b, (((1,), (1,)), ((), ())), precision=precision, preferred_element_type=F32)


def _dot_tn(a, b, precision=None):
    return lax.dot_general(a, b, (((0,), (0,)), ((), ())), precision=precision, preferred_element_type=F32)


def _split_dot(x, ones_bf16):
    hi = x.astype(BF16)
    lo = (x - hi.astype(F32)).astype(BF16)
    return _dot(hi, ones_bf16) + _dot(lo, ones_bf16)


def _split3(x):
    hi = x.astype(BF16)
    r1 = x - hi.astype(F32)
    mid = r1.astype(BF16)
    return hi, mid, (r1 - mid.astype(F32)).astype(BF16)


def _tril(n, strict=False):
    r = lax.broadcasted_iota(jnp.int32, (n, n), 0)
    c = lax.broadcasted_iota(jnp.int32, (n, n), 1)
    return (r > c) if strict else (r >= c)


def _mod_kernel(c_ref, w_ref, b_ref, o_ref):
    c = c_ref[...]
    o_ref[...] = _dot_hi(c * _sigmoid(c), w_ref[...]) + b_ref[...]


def adaln_mod(c, w, b):
    d, n = w.shape
    tn = 1024
    c8 = jnp.broadcast_to(c, (SUBLANES, d))
    out = pl.pallas_call(
        _mod_kernel,
        grid=(n // tn,),
        in_specs=[pl.BlockSpec((SUBLANES, d), lambda j: (0, 0)),
                  pl.BlockSpec((d, tn), lambda j: (0, j)),
                  pl.BlockSpec((1, tn), lambda j: (0, j))],
        out_specs=pl.BlockSpec((SUBLANES, tn), lambda j: (0, j)),
        out_shape=jax.ShapeDtypeStruct((SUBLANES, n), F32),
        compiler_params=_cparams("arbitrary"),
    )(c8, w, b.reshape(1, n))
    return out[:1]


def _norm_mod(x, gain, sc, sh):
    y = x * lax.rsqrt(jnp.mean(x * x, axis=-1, keepdims=True) + EPS)
    return y * gain * (1.0 + sc) + sh


def _norm_kernel(x_ref, gain_ref, sc_ref, sh_ref, h_ref):
    h_ref[...] = _norm_mod(x_ref[...], gain_ref[...], sc_ref[...], sh_ref[...]).astype(h_ref.dtype)


def norm_mod(x, gain, sc, sh, tm=512):
    L, d = x.shape
    row = pl.BlockSpec((1, d), lambda i: (0, 0))
    return pl.pallas_call(
        _norm_kernel,
        grid=(L // tm,),
        in_specs=[pl.BlockSpec((tm, d), lambda i: (i, 0)), row, row, row],
        out_specs=pl.BlockSpec((tm, d), lambda i: (i, 0)),
        out_shape=jax.ShapeDtypeStruct((L, d), BF16),
        compiler_params=_cparams("arbitrary"),
    )(x, gain, sc, sh)


def _mm_kernel(x_ref, w_ref, o_ref):
    o_ref[...] = _dot(x_ref[...], w_ref[...]).astype(o_ref.dtype)


def matmul(x, w, out_dtype=F32, tm=512):
    M, K = x.shape
    N = w.shape[1]
    return pl.pallas_call(
        _mm_kernel,
        grid=(M // tm,),
        in_specs=[pl.BlockSpec((tm, K), lambda i: (i, 0)),
                  pl.BlockSpec((K, N), lambda i: (0, 0))],
        out_specs=pl.BlockSpec((tm, N), lambda i: (i, 0)),
        out_shape=jax.ShapeDtypeStruct((M, N), out_dtype),
        compiler_params=_cparams("arbitrary"),
    )(x, w)


def _rw_pre_kernel(has_vres, *refs):
    if has_vres:
        (h_ref, w_ref, mu_ref, w0_ref, wup_ref, a0_ref, aup_ref, gup_ref, kk_ref, ka_ref, rk_ref, seg_ref,
         vfirst_ref, v0_ref, vup_ref,
         r_out, k_out, v_out, lw_out, a_out, b_out, g_out, bonus_out, ext_scr) = refs
    else:
        (h_ref, w_ref, mu_ref, w0_ref, wup_ref, a0_ref, aup_ref, gup_ref, kk_ref, ka_ref, rk_ref, seg_ref,
         r_out, k_out, v_out, lw_out, a_out, b_out, g_out, bonus_out, ext_scr) = refs
    T = h_ref.shape[0]

    @pl.when(pl.program_id(0) == 0)
    def _():
        ext_scr[0:SUBLANES, :] = jnp.zeros((SUBLANES, ext_scr.shape[1]), F32)

    cur = _dot(h_ref[...], w_ref[...])
    ext_scr[SUBLANES:SUBLANES + T, :] = cur
    prev = ext_scr[SUBLANES - 1:SUBLANES - 1 + T, :]
    ext_scr[0:SUBLANES, :] = cur[T - SUBLANES:T, :]
    s = cur + (prev - cur) * mu_ref[...]

    W = RW_WIDTH
    r = s[:, 0:W]
    k = s[:, W:2 * W]
    v = s[:, 2 * W:3 * W]
    w_lo = s[:, RW_OFF_WLO:RW_OFF_WLO + LANES]
    a_lo = s[:, RW_OFF_ALO:RW_OFF_ALO + LANES]
    g_lo = s[:, RW_OFF_GLO:RW_OFF_GLO + 2 * LANES]
    seg = seg_ref[...]

    wlog = -_softplus(-(w0_ref[...] + _dot_bf16(jnp.tanh(w_lo), wup_ref[...]))) - 0.5
    a = _sigmoid(a0_ref[...] + _dot_bf16(a_lo, aup_ref[...]))
    g = _dot_bf16(_sigmoid(g_lo), gup_ref[...])
    if has_vres:
        hv = s[:, RW_OFF_HV:RW_OFF_HV + LANES]
        v = v + (vfirst_ref[...].astype(F32) - v) * _sigmoid(v0_ref[...] + _dot_bf16(hv, vup_ref[...]))
    kk = k * kk_ref[...]
    kk = kk / jnp.maximum(jnp.sqrt(_split_dot(kk * kk, seg)), 1e-12)
    k = k * (1.0 + (a - 1.0) * ka_ref[...])
    r_out[...] = r.astype(r_out.dtype)
    k_out[...] = k.astype(k_out.dtype)
    v_out[...] = v.astype(v_out.dtype)
    lw_out[...] = -jnp.exp(wlog)
    a_out[...] = (-kk).astype(a_out.dtype)
    b_out[...] = (kk * a).astype(b_out.dtype)
    g_out[...] = g.astype(g_out.dtype)
    bonus_out[...] = (_split_dot(r * k * rk_ref[...], seg) * v).astype(bonus_out.dtype)


def rw_pre(h, w_rw, p, v_first, tm=512):
    L, D = h.shape
    has_vres = v_first is not None
    full = lambda a: pl.BlockSpec(a.shape, lambda i: (0,) * a.ndim)
    rows = lambda n: pl.BlockSpec((tm, n), lambda i: (i, 0))
    args = [h, w_rw, p["mu"], p["w0"], p["w_up"], p["a0"], p["a_up"], p["g_up"], p["k_k"], p["k_a"], p["r_k"],
            p["seg"]]
    specs = [rows(D)] + [full(a) for a in args[1:]]
    if has_vres:
        args += [v_first, p["v0"], p["v_up"]]
        specs += [rows(RW_WIDTH), full(p["v0"]), full(p["v_up"])]
    out_dtypes = [BF16, BF16, BF16, F32, BF16, BF16, BF16, BF16]
    return pl.pallas_call(
        functools.partial(_rw_pre_kernel, has_vres),
        grid=(L // tm,),
        in_specs=specs,
        out_specs=[rows(RW_WIDTH)] * 8,
        out_shape=[jax.ShapeDtypeStruct((L, RW_WIDTH), dt) for dt in out_dtypes],
        scratch_shapes=[pltpu.VMEM((tm + SUBLANES, RW_PAD_COLS), F32)],
        compiler_params=_cparams("arbitrary"),
    )(*args)


def _wkv_kernel(r_ref, lw_ref, k_ref, v_ref, a_ref, b_ref, y_ref, h_scr):
    C = RW_CHUNK
    C2 = 2 * C
    n_chunks = r_ref.shape[0] // C

    @pl.when(pl.program_id(1) == 0)
    def _():
        h_scr[...] = jnp.zeros_like(h_scr)

    is_a = lax.broadcasted_iota(jnp.int32, (C, LANES), 1) < RW_HEAD_DIM
    ri = lax.broadcasted_iota(jnp.int32, (C2, C2), 0)
    ci = lax.broadcasted_iota(jnp.int32, (C2, C2), 1)
    same = (ri >= C) == (ci >= C)
    strict = jnp.logical_and(same, ri > ci)
    incl = jnp.logical_and(same, ri >= ci)
    eye2 = (ri == ci).astype(F32)
    tri_c = _tril(C).astype(BF16)

    def two(x):
        return jnp.concatenate([jnp.where(is_a, x, 0.0), jnp.where(is_a, 0.0, x)], axis=0)

    chunks = range(n_chunks)
    at2, rt2, bh2, kh2, v2, n_ab, a_ak, m_rb, m_rk, p_end = ([] for _ in range(10))
    for c in chunks:
        sl = pl.ds(c * C, C)
        lw = lw_ref[sl, :]
        lw_hi = lw.astype(BF16)
        lw_lo = (lw - lw_hi.astype(F32)).astype(BF16)
        cs = _dot(tri_c, lw_hi) + _dot(tri_c, lw_lo)
        cs_end = cs[C - 1:C, :]
        e_neg = jnp.exp(-cs)
        e_end = jnp.exp(cs_end - cs)
        a = a_ref[sl, :].astype(F32)
        b = b_ref[sl, :].astype(F32)
        k = k_ref[sl, :].astype(F32)
        at2.append(two(a * jnp.exp(cs - lw)))
        rt2.append(two(r_ref[sl, :].astype(F32) * jnp.exp(cs)))
        bh2.append(two(b * e_end))
        kh2.append(two(k * e_end))
        v2.append(two(v_ref[sl, :].astype(F32)))
        p_end.append(jnp.exp(cs_end))
        quad = _dot_nt(jnp.concatenate([at2[c], rt2[c]], axis=0).astype(BF16),
                       jnp.concatenate([two(b * e_neg), two(k * e_neg)], axis=0).astype(BF16))
        n_ab.append(jnp.where(strict, quad[0:C2, 0:C2], 0.0))
        a_ak.append(jnp.where(strict, quad[0:C2, C2:], 0.0))
        m_rb.append(jnp.where(incl, quad[C2:, 0:C2], 0.0))
        m_rk.append(jnp.where(incl, quad[C2:, C2:], 0.0))
    t_inv = [eye2 + n for n in n_ab]
    n_pow = n_ab
    for _ in range(int(math.log2(C)) - 1):
        n_pow = [_dot_bf16(n, n) for n in n_pow]
        t_inv = [t + _dot_bf16(t, n) for t, n in zip(t_inv, n_pow)]
    akv = [_dot_bf16(a_ak[c], v2[c]) for c in chunks]
    au = [_dot_bf16(t_inv[c], jnp.concatenate([at2[c], akv[c]], axis=1)) for c in chunks]
    mau = [_dot_bf16(m_rb[c], au[c]) for c in chunks]
    mkv = [_dot_bf16(m_rk[c], v2[c]) for c in chunks]
    gu = [_dot_tn(bh2[c].astype(BF16), au[c].astype(BF16)) for c in chunks]
    kv = [_dot_tn(kh2[c].astype(BF16), v2[c].astype(BF16)) for c in chunks]
    inv_n = 1.0 / RW_HEAD_DIM
    h = h_scr[...]
    for c in chunks:
        rh2 = rt2[c] + mau[c][:, :LANES]
        g_mat = eye2 * p_end[c] + gu[c][:, :LANES]
        yh = _dot_bf16(jnp.concatenate([rh2, g_mat], axis=0), h)
        h = yh[C2:, :] + gu[c][:, LANES:] + kv[c]
        y2 = yh[:C2, :] + mau[c][:, LANES:] + mkv[c]
        y = y2[:C, :] + y2[C:, :]
        s_all = jnp.sum(y, axis=-1, keepdims=True)
        s_a = jnp.sum(jnp.where(is_a, y, 0.0), axis=-1, keepdims=True)
        yc = y - jnp.where(is_a, s_a, s_all - s_a) * inv_n
        sq = yc * yc
        q_all = jnp.sum(sq, axis=-1, keepdims=True)
        q_a = jnp.sum(jnp.where(is_a, sq, 0.0), axis=-1, keepdims=True)
        yn = yc * lax.rsqrt(jnp.where(is_a, q_a, q_all - q_a) * inv_n + GN_EPS)
        y_ref[pl.ds(c * C, C), :] = yn.astype(y_ref.dtype)
    h_scr[...] = h


def wkv7(r, lw, k, v, a, b, rows=512):
    L, W = r.shape
    spec = pl.BlockSpec((rows, LANES), lambda p, i: (i, p))
    return pl.pallas_call(
        _wkv_kernel,
        grid=(W // LANES, L // rows),
        in_specs=[spec] * 6,
        out_specs=spec,
        out_shape=jax.ShapeDtypeStruct((L, W), BF16),
        scratch_shapes=[pltpu.VMEM((LANES, LANES), F32)],
        compiler_params=_cparams("arbitrary", "arbitrary"),
    )(r, lw, k, v, a, b)


def _ssd_kernel(h_ref, w_ref, cw_ref, cb_ref, bias_ref, A_ref, Ax_ref, exp_ref, dskip_ref, nw_ref, o_ref,
                ext_scr, xbc_scr, s_scr):
    Q = SSD_CHUNK
    GW = SSD_GROUP_WIDTH
    NS = SSD_STATE
    R = h_ref.shape[0]

    @pl.when(pl.program_id(0) == 0)
    def _():
        s_scr[...] = jnp.zeros_like(s_scr)
        ext_scr[0:SUBLANES, :] = jnp.zeros((SUBLANES, ext_scr.shape[1]), F32)

    cols = _dot(h_ref[...], w_ref[...])
    ext_scr[SUBLANES:SUBLANES + R, :] = cols[:, 0:SSD_XBC]
    acc = cols[:, 0:SSD_XBC] * cw_ref[SSD_CONV - 1:SSD_CONV, :] + cb_ref[...]
    for j in range(1, SSD_CONV):
        acc = acc + ext_scr[SUBLANES - j:SUBLANES - j + R, :] * cw_ref[SSD_CONV - 1 - j:SSD_CONV - j, :]
    ext_scr[0:SUBLANES, :] = ext_scr[R:R + SUBLANES, :]
    xbc_scr[...] = acc * _sigmoid(acc)

    incl = _tril(Q)
    tri = incl.astype(BF16)
    triu = jnp.logical_not(_tril(Q, strict=True)).astype(BF16)
    expand = exp_ref[...]
    first_half = lax.broadcasted_iota(jnp.int32, (Q, LANES), 1) < SSD_HEAD_DIM
    heads_per_group = SSD_HEADS // SSD_GROUPS
    for c in range(R // Q):
        rows = slice(c * Q, (c + 1) * Q)
        xs = xbc_scr[rows, 0:SSD_WIDTH]
        z = cols[rows, SSD_XBC:SSD_XBC + SSD_WIDTH]
        dt = _softplus(cols[rows, SSD_XBC + SSD_WIDTH:] + bias_ref[...])
        a3 = _split3(dt * A_ref[...])
        a_cum = sum(_dot(tri, t) for t in a3)
        a_cumT = sum(_dot_tn(t, triu) for t in a3)
        dt_x = sum(_dot(t, expand) for t in _split3(dt))
        acum_x = sum(_dot(tri, t) for t in _split3(dt_x * Ax_ref[...]))
        eac_x = jnp.exp(acum_x)
        dte_x = jnp.exp(acum_x[Q - 1:Q, :] - acum_x)
        xdt = xs * dt_x
        for g in range(SSD_GROUPS):
            Bg = xbc_scr[rows, SSD_WIDTH + g * NS:SSD_WIDTH + (g + 1) * NS].astype(BF16)
            Cg = xbc_scr[rows, SSD_WIDTH + (SSD_GROUPS + g) * NS:SSD_WIDTH + (SSD_GROUPS + g + 1) * NS].astype(BF16)
            cb = _dot_nt(Cg, Bg)
            gs = slice(g * GW, (g + 1) * GW)
            s_prev = s_scr[:, gs]
            y_g = _dot_bf16(Cg, s_prev) * eac_x[:, gs]
            pieces = []
            for pr in range(heads_per_group // 2):
                ps = slice(g * GW + pr * LANES, g * GW + (pr + 1) * LANES)
                xdt_p = xdt[:, ps]
                acc_p = None
                for e in range(2):
                    hd = g * heads_per_group + pr * 2 + e
                    seg = a_cum[:, hd:hd + 1] - a_cumT[hd:hd + 1, :]
                    m = cb * jnp.exp(jnp.where(incl, seg, -jnp.inf))
                    xm = jnp.where(first_half if e == 0 else jnp.logical_not(first_half), xdt_p, 0.0)
                    t = _dot_bf16(m, xm)
                    acc_p = t if acc_p is None else acc_p + t
                pieces.append(acc_p)
            y_g = y_g + jnp.concatenate(pieces, axis=1)
            s_scr[:, gs] = s_prev * eac_x[Q - 1:Q, gs] + _dot_tn(Bg, (dte_x[:, gs] * xdt[:, gs]).astype(BF16))
            y_g = y_g + dskip_ref[:, gs] * xs[:, gs]
            zg = z[:, gs]
            y_g = y_g * (zg * _sigmoid(zg))
            y_g = y_g * lax.rsqrt(jnp.mean(y_g * y_g, axis=-1, keepdims=True) + EPS)
            o_ref[rows, gs] = (y_g * nw_ref[:, gs]).astype(o_ref.dtype)


def ssd_mixer(h, w_ssd, p, rows=512):
    L, D = h.shape
    N = w_ssd.shape[1]
    full = lambda a: pl.BlockSpec(a.shape, lambda i: (0,) * a.ndim)
    small = [p["conv_w"], p["conv_b"], p["dt_bias"], p["A"], p["A_x"], p["expand"], p["d_skip"], p["norm_w"]]
    return pl.pallas_call(
        _ssd_kernel,
        grid=(L // rows,),
        in_specs=[pl.BlockSpec((rows, D), lambda i: (i, 0)), full(w_ssd)] + [full(a) for a in small],
        out_specs=pl.BlockSpec((rows, SSD_WIDTH), lambda i: (i, 0)),
        out_shape=jax.ShapeDtypeStruct((L, SSD_WIDTH), BF16),
        scratch_shapes=[pltpu.VMEM((rows + SUBLANES, SSD_XBC), F32), pltpu.VMEM((rows, SSD_XBC), F32),
                        pltpu.VMEM((SSD_STATE, SSD_WIDTH), F32)],
        compiler_params=_cparams("arbitrary"),
    )(h, w_ssd, *small)


def _fox_pre_kernel(h_ref, w_ref, fb_ref, qg_ref, kg_ref, seg_ref, q_out, k_out, v_out, f_out, carry_scr):
    T = h_ref.shape[0]
    W = FOX_WIDTH

    @pl.when(pl.program_id(0) == 0)
    def _():
        carry_scr[...] = jnp.zeros_like(carry_scr)

    cols = _dot(h_ref[...], w_ref[...])
    seg = seg_ref[...]
    q = cols[:, 0:W]
    k = cols[:, W:2 * W]
    inv_d = 1.0 / FOX_HEAD_DIM
    qn = q * lax.rsqrt(_split_dot(q * q, seg) * inv_d + EPS) * qg_ref[...]
    kn = k * lax.rsqrt(_split_dot(k * k, seg) * inv_d + EPS) * kg_ref[...]
    q_out[...] = qn.astype(q_out.dtype)
    k_out[...] = kn.astype(k_out.dtype)
    v_out[...] = cols[:, 2 * W:3 * W].astype(v_out.dtype)
    f = cols[:, 3 * W:3 * W + LANES] + fb_ref[...]
    ls = -_softplus(-f)
    cum = _dot_hi(_tril(T).astype(F32), ls) + carry_scr[0:1, :]
    f_out[...] = cum * LOG2E
    carry_scr[...] = jnp.broadcast_to(cum[T - 1:T, :], carry_scr.shape)


def fox_pre(h, w_fox, p, tm=512):
    L, D = h.shape
    full = lambda a: pl.BlockSpec(a.shape, lambda i: (0,) * a.ndim)
    rows = lambda n: pl.BlockSpec((tm, n), lambda i: (i, 0))
    small = [w_fox, p["f_bias"], p["q_gain"], p["k_gain"], p["seg"]]
    return pl.pallas_call(
        _fox_pre_kernel,
        grid=(L // tm,),
        in_specs=[rows(D)] + [full(a) for a in small],
        out_specs=[rows(FOX_WIDTH)] * 3 + [rows(LANES)],
        out_shape=[jax.ShapeDtypeStruct((L, FOX_WIDTH), BF16)] * 3 + [jax.ShapeDtypeStruct((L, LANES), F32)],
        scratch_shapes=[pltpu.VMEM((SUBLANES, LANES), F32)],
        compiler_params=_cparams("arbitrary"),
    )(h, *small)


def _fox_kernel(lo_ref, q_ref, k_ref, v_ref, fk_ref, o_ref, m_scr, l_scr, acc_scr):
    tq = q_ref.shape[0]
    tk = FOX_TK
    pair = pl.program_id(0)
    qi = pl.program_id(1)
    is_a = lax.broadcasted_iota(jnp.int32, (tq, LANES), 1) < FOX_HEAD_DIM
    q = q_ref[...]
    zero = jnp.zeros_like(q)
    qs = (jnp.where(is_a, q, zero), jnp.where(is_a, zero, q))
    m_scr[...] = jnp.full(m_scr.shape, -jnp.inf, F32)
    l_scr[...] = jnp.zeros_like(l_scr)
    acc_scr[...] = jnp.zeros_like(acc_scr)
    rel = lax.broadcasted_iota(jnp.int32, (tq, tk), 0) - lax.broadcasted_iota(jnp.int32, (tq, tk), 1)
    last = (qi * tq) // tk

    def block(kb, diagonal):
        ks = pl.ds(pl.multiple_of(kb * tk, tk), tk)
        k_blk = k_ref[ks, :]
        v_blk = v_ref[ks, :]
        for e in range(2):
            s = _dot_nt(qs[e], k_blk) - fk_ref[pl.ds(2 * pair + e, 1), ks]
            if diagonal:
                s = jnp.where(rel >= kb * tk - qi * tq, s, -jnp.inf)
            m_old = m_scr[e]
            m_new = jnp.maximum(m_old, jnp.max(s, axis=-1, keepdims=True))
            pexp = jnp.exp2(s - m_new)
            alpha = jnp.exp2(m_old - m_new)
            l_scr[e] = alpha * l_scr[e] + jnp.sum(pexp, axis=-1, keepdims=True)
            acc_scr[e] = alpha * acc_scr[e] + _dot(pexp.astype(BF16), v_blk)
            m_scr[e] = m_new

    block(last, True)

    def body(i, carry):
        block(last - 1 - i, False)
        return carry

    lo = jnp.minimum(lo_ref[2 * pair, qi], lo_ref[2 * pair + 1, qi])
    lax.fori_loop(0, last - lo, body, 0)
    o_ref[...] = jnp.where(is_a, acc_scr[0] / l_scr[0], acc_scr[1] / l_scr[1]).astype(o_ref.dtype)


def fox_attention(q, k, v, FT, lo):
    L, W = q.shape
    H = FT.shape[0]
    tq = FOX_TQ
    grid_spec = pltpu.PrefetchScalarGridSpec(
        num_scalar_prefetch=1,
        grid=(W // LANES, L // tq),
        in_specs=[pl.BlockSpec((tq, LANES), lambda p, i, lo_r: (i, p)),
                  pl.BlockSpec((L, LANES), lambda p, i, lo_r: (0, p)),
                  pl.BlockSpec((L, LANES), lambda p, i, lo_r: (0, p)),
                  pl.BlockSpec((H, L), lambda p, i, lo_r: (0, 0))],
        out_specs=pl.BlockSpec((tq, LANES), lambda p, i, lo_r: (i, p)),
        scratch_shapes=[pltpu.VMEM((2, tq, 1), F32), pltpu.VMEM((2, tq, 1), F32),
                        pltpu.VMEM((2, tq, LANES), F32)],
    )
    return pl.pallas_call(
        _fox_kernel,
        grid_spec=grid_spec,
        out_shape=jax.ShapeDtypeStruct((L, W), BF16),
        compiler_params=_cparams("arbitrary", "arbitrary"),
    )(lo, q, k, v, FT)


def fox_first_block(FT, bound2):
    f_first = FT[:, ::FOX_TQ]
    f_last = FT[:, FOX_TK - 1::FOX_TK]
    gap = f_first[:, :, None] - f_last[:, None, :] + 2.0 * bound2
    last = (jnp.arange(f_first.shape[1]) * FOX_TQ) // FOX_TK
    needed = jnp.logical_or(gap >= -152.0, jnp.arange(f_last.shape[1])[None, None, :] >= last[None, :, None])
    return jnp.argmax(needed, axis=-1).astype(jnp.int32)


def _top2_combine(h, wr):
    T = h.shape[0]
    lane = lax.broadcasted_iota(jnp.int32, (T, LANES), 1)
    logits = jnp.where(lane < N_EXPERTS, _dot_hi(h, wr), -jnp.inf)
    m1 = jnp.max(logits, axis=-1, keepdims=True)
    i1 = jnp.min(jnp.where(logits == m1, lane, LANES), axis=-1, keepdims=True)
    rest = jnp.where(lane == i1, -jnp.inf, logits)
    m2 = jnp.max(rest, axis=-1, keepdims=True)
    i2 = jnp.min(jnp.where(rest == m2, lane, LANES), axis=-1, keepdims=True)
    e2 = jnp.exp(m2 - m1)
    return jnp.where(lane == i1, 1.0 / (1.0 + e2), 0.0) + jnp.where(lane == i2, e2 / (1.0 + e2), 0.0)


def _merge_kernel(with_router, *refs):
    (x_ref, yn_ref, bonus_ref, g_ref, yssd_ref, yfox_ref, gl_ref, gb_ref, lnw_ref, lnb_ref,
     prw_ref, pssd_ref, pfox_ref, wout_ref, gm_ref, gain_ref, sc_ref, sh_ref) = refs[:18]
    D = D_MODEL
    y_rw = (yn_ref[...] * lnw_ref[...] + lnb_ref[...] + bonus_ref[...]) * g_ref[...]
    gates = _sigmoid(gl_ref[...] + gb_ref[...])
    merged = gates[:, 0:D] * _dot_bf16(y_rw, prw_ref[...])
    merged = merged + gates[:, D:2 * D] * _dot_bf16(yssd_ref[...], pssd_ref[...])
    merged = merged + gates[:, 2 * D:3 * D] * _dot_bf16(yfox_ref[...], pfox_ref[...])
    x_new = x_ref[...] + gm_ref[...] * _dot_bf16(merged, wout_ref[...])
    hf = _norm_mod(x_new, gain_ref[...], sc_ref[...], sh_ref[...])
    if with_router:
        wr_ref, o_ref, h_ref, comb_ref = refs[18:]
        comb_ref[...] = _top2_combine(hf, wr_ref[...])
    else:
        o_ref, h_ref = refs[18:]
    o_ref[...] = x_new
    h_ref[...] = hf.astype(h_ref.dtype)


def merge(x, yn, bonus, g, y_ssd, y_fox, gate_logits, p, router=None, tm=512):
    L, D = x.shape
    full = lambda a: pl.BlockSpec(a.shape, lambda i: (0,) * a.ndim)
    rows = lambda n: pl.BlockSpec((tm, n), lambda i: (i, 0))
    small = [p["gate_b"], p["lnx_w"], p["lnx_b"], p["proj_rw"], p["proj_ssd"], p["proj_fox"], p["w_out"], p["g_m"],
             p["norm_gain"], p["sc_f"], p["sh_f"]]
    out_specs = [rows(D), rows(D)]
    out_shape = [jax.ShapeDtypeStruct((L, D), F32), jax.ShapeDtypeStruct((L, D), BF16)]
    if router is not None:
        small.append(router)
        out_specs.append(rows(LANES))
        out_shape.append(jax.ShapeDtypeStruct((L, LANES), F32))
    return pl.pallas_call(
        functools.partial(_merge_kernel, router is not None),
        grid=(L // tm,),
        in_specs=[rows(D), rows(RW_WIDTH), rows(RW_WIDTH), rows(RW_WIDTH), rows(SSD_WIDTH), rows(FOX_WIDTH),
                  rows(3 * D)] + [full(a) for a in small],
        out_specs=out_specs,
        out_shape=out_shape,
        compiler_params=_cparams("arbitrary"),
    )(x, yn, bonus, g, y_ssd, y_fox, gate_logits, *small)


def _ffn_kernel(with_next, *refs):
    h_ref, x_ref, wg_ref, wu_ref, wd_ref, gf_ref = refs[:6]
    j = pl.program_id(1)
    acc_scr = refs[-1]

    @pl.when(j == 0)
    def _():
        acc_scr[...] = jnp.zeros_like(acc_scr)

    h = h_ref[...]
    gte = _dot(h, wg_ref[...])
    up = _dot(h, wu_ref[...])
    act = gte * _sigmoid(gte) * up
    acc_scr[...] += _dot(act.astype(BF16), wd_ref[...])

    @pl.when(j == pl.num_programs(1) - 1)
    def _():
        x_new = x_ref[...] + gf_ref[...] * acc_scr[...]
        if with_next:
            gain_ref, sc_ref, sh_ref, o_ref, hn_ref = refs[6:11]
            hn_ref[...] = _norm_mod(x_new, gain_ref[...], sc_ref[...], sh_ref[...]).astype(hn_ref.dtype)
        else:
            o_ref = refs[6]
        o_ref[...] = x_new


def ffn_dense(h, x, w_gu, w_down, g_f, next_norm=None, tm=1024, tf=256):
    L, D = x.shape
    Fh = w_down.shape[0]
    nf = Fh // tf
    row = pl.BlockSpec((1, D), lambda i, j: (0, 0))
    tile = pl.BlockSpec((tm, D), lambda i, j: (i, 0))
    args = [h, x, w_gu, w_gu, w_down, g_f]
    in_specs = [tile, tile,
                pl.BlockSpec((D, tf), lambda i, j: (0, j)),
                pl.BlockSpec((D, tf), lambda i, j: (0, j + nf)),
                pl.BlockSpec((tf, D), lambda i, j: (j, 0)), row]
    out_specs = [tile]
    out_shape = [jax.ShapeDtypeStruct((L, D), F32)]
    if next_norm is not None:
        args += list(next_norm)
        in_specs += [row, row, row]
        out_specs.append(tile)
        out_shape.append(jax.ShapeDtypeStruct((L, D), BF16))
    out = pl.pallas_call(
        functools.partial(_ffn_kernel, next_norm is not None),
        grid=(L // tm, nf),
        in_specs=in_specs,
        out_specs=out_specs,
        out_shape=out_shape,
        scratch_shapes=[pltpu.VMEM((tm, D), F32)],
        compiler_params=_cparams("arbitrary", "arbitrary"),
    )(*args)
    return out if next_norm is not None else (out[0], None)


def _moe_kernel(nsb_ref, h_ref, x_ref, comb_ref, wg_ref, wu_ref, wd_ref, gf_ref, o_ref,
                rank_scr, rcol_scr, ccol_scr, xg_scr, acc_scr, out_scr):
    i = pl.program_id(0)
    e = pl.program_id(1)
    j = pl.program_id(2)
    last_j = pl.num_programs(2) - 1
    T = h_ref.shape[0]
    SB = MOE_SB
    lane_e = lax.broadcasted_iota(jnp.int32, (T, LANES), 1)

    @pl.when(jnp.logical_and(e == 0, j == 0))
    def _():
        out_scr[...] = jnp.zeros_like(out_scr)
        sel = jnp.where(comb_ref[...] > 0.0, 1.0, 0.0).astype(BF16)
        before = _tril(T, strict=True).astype(BF16)
        rank_scr[...] = _dot(before, sel)

    @pl.when(j == 0)
    def _():
        mine = lane_e == e
        rcol_scr[...] = jnp.sum(jnp.where(mine, rank_scr[...], 0.0), axis=-1, keepdims=True)
        ccol_scr[...] = jnp.sum(jnp.where(mine, comb_ref[...], 0.0), axis=-1, keepdims=True)

    def body(sb, carry):
        rows = pl.ds(pl.multiple_of(sb * SB, SB), SB)
        slot = (lax.broadcasted_iota(jnp.int32, (T, SB), 1) + sb * SB).astype(F32)
        pick = jnp.logical_and(rcol_scr[...] == slot, ccol_scr[...] > 0.0)
        pt = jnp.where(pick, 1.0, 0.0).astype(BF16)

        @pl.when(j == 0)
        def _():
            xg_scr[rows, :] = _dot_tn(pt, h_ref[...]).astype(BF16)

        xb = xg_scr[rows, :]
        gte = _dot(xb, wg_ref[0])
        up = _dot(xb, wu_ref[0])
        part = _dot((gte * _sigmoid(gte) * up).astype(BF16), wd_ref[0])

        @pl.when(j == 0)
        def _():
            acc_scr[rows, :] = part

        @pl.when(j > 0)
        def _():
            acc_scr[rows, :] += part

        @pl.when(j == last_j)
        def _():
            out_scr[...] += ccol_scr[...] * _dot(pt, acc_scr[rows, :].astype(BF16))

        return carry

    lax.fori_loop(0, nsb_ref[i, e], body, 0)

    @pl.when(jnp.logical_and(e == pl.num_programs(1) - 1, j == last_j))
    def _():
        o_ref[...] = x_ref[...] + gf_ref[...] * out_scr[...]


def ffn_moe(h, x, comb, w_gu, w_down, g_f, tm=1024, tf=512):
    L, D = x.shape
    E, Fh, _ = w_down.shape
    nf = Fh // tf
    nt = L // tm
    counts = jnp.sum((comb[:, :E] > 0.0).reshape(nt, tm, E), axis=1, dtype=jnp.int32)
    nsb = (counts + (MOE_SB - 1)) // MOE_SB
    grid_spec = pltpu.PrefetchScalarGridSpec(
        num_scalar_prefetch=1,
        grid=(nt, E, nf),
        in_specs=[pl.BlockSpec((tm, D), lambda i, e, j, n: (i, 0)),
                  pl.BlockSpec((tm, D), lambda i, e, j, n: (i, 0)),
                  pl.BlockSpec((tm, LANES), lambda i, e, j, n: (i, 0)),
                  pl.BlockSpec((1, D, tf), lambda i, e, j, n: (e, 0, j)),
                  pl.BlockSpec((1, D, tf), lambda i, e, j, n: (e, 0, j + nf)),
                  pl.BlockSpec((1, tf, D), lambda i, e, j, n: (e, j, 0)),
                  pl.BlockSpec((1, D), lambda i, e, j, n: (0, 0))],
        out_specs=pl.BlockSpec((tm, D), lambda i, e, j, n: (i, 0)),
        scratch_shapes=[pltpu.VMEM((tm, LANES), F32), pltpu.VMEM((tm, 1), F32), pltpu.VMEM((tm, 1), F32),
                        pltpu.VMEM((tm, D), BF16), pltpu.VMEM((tm, D), F32), pltpu.VMEM((tm, D), F32)],
    )
    return pl.pallas_call(
        _moe_kernel,
        grid_spec=grid_spec,
        out_shape=jax.ShapeDtypeStruct((L, D), F32),
        compiler_params=_cparams("arbitrary", "arbitrary", "arbitrary"),
    )(nsb, h, x, comb, w_gu, w_gu, w_down, g_f)


def _seg_matrix(width, head_dim):
    idx = jnp.arange(width) // head_dim
    return (idx[:, None] == idx[None, :]).astype(BF16)


def _pad_cols(w, n):
    return jnp.pad(w, ((0, 0), (0, n - w.shape[1])))


def _pad_rows(w, n):
    return jnp.pad(w, ((0, n - w.shape[0]), (0, 0)))


def rwkv_branch(h, w_rw, p, v_first):
    r, k, v, lw, a, b, g, bonus = rw_pre(h, w_rw, p, v_first)
    return wkv7(r, lw, k, v, a, b), bonus, g, v


def fox_branch(h, w_fox, p):
    q, k, v, F = fox_pre(h, w_fox, p)
    FT = jnp.transpose(F[:, :FOX_HEADS])
    lo = fox_first_block(FT, p["bound2"])
    return fox_attention(q, k, v, FT, lo)


def kernel(x, c, ada_w, ada_b, norm_mix, norm_ffn, w_in, rw_mu, rw_w0, rw_w_up, rw_a0, rw_a_up, rw_g_up, rw_k_k, rw_k_a, rw_r_k, rw_lnx_w, rw_lnx_b, rw_v0, rw_v_down, rw_v_up, ssd_conv_w, ssd_conv_b, ssd_dt_bias, ssd_a_log, ssd_d, ssd_norm, fox_f_bias, fox_q_gain, fox_k_gain, gate_b, proj_rw, proj_ssd, proj_fox, w_out, ffn_w_gu, ffn_w_down, moe_router, moe_w_gu, moe_w_down):
    depth = w_in.shape[0]
    D = D_MODEL
    xs = x[0]
    row = lambda t: t.reshape(1, -1).astype(F32)
    seg64 = _seg_matrix(RW_WIDTH, RW_HEAD_DIM)
    rw_cols = 3 * RW_WIDTH + RW_DECAY_LORA + RW_AAA_LORA + RW_GATE_LORA
    ssd_cols = SSD_WIDTH + SSD_XBC + SSD_HEADS
    fox_cols = 3 * FOX_WIDTH + FOX_HEADS
    expand = (jnp.arange(LANES)[:, None] == (jnp.arange(SSD_WIDTH) // SSD_HEAD_DIM)[None, :]).astype(BF16)
    v_first = None
    mods = []
    for l in range(depth):
        mod = adaln_mod(c, ada_w[l], ada_b[l])
        mods.append([mod[:, i * D:(i + 1) * D] for i in range(6)])
    h = None
    for l in range(depth):
        sh_m, sc_m, g_m, sh_f, sc_f, g_f = mods[l]
        if h is None:
            h = norm_mod(xs, row(norm_mix[l]), sc_m, sh_m)

        wl = w_in[l]
        o = 0
        w_r = wl[:, o:o + rw_cols]; o += rw_cols
        w_s = wl[:, o:o + ssd_cols]; o += ssd_cols
        w_f = wl[:, o:o + fox_cols]; o += fox_cols
        w_g = wl[:, o:]
        W3 = 3 * RW_WIDTH
        o_a = W3 + RW_DECAY_LORA
        o_g = o_a + RW_AAA_LORA

        def rw_layout(t, hv):
            parts = [t[:, :W3], _pad_cols(t[:, W3:o_a], LANES), _pad_cols(t[:, o_a:o_g], LANES),
                     _pad_cols(t[:, o_g:], 2 * LANES), _pad_cols(hv, LANES)]
            return jnp.concatenate(parts, axis=1)

        if l == 0:
            hv_w = jnp.zeros((D, RW_VRES_LORA), F32)
        else:
            hv_w = rw_v_down[l - 1]
        w_rw = rw_layout(w_r, hv_w).astype(BF16)
        mu = rw_layout(rw_mu[l].reshape(1, -1), jnp.zeros((1, RW_VRES_LORA), F32))
        rw_p = dict(mu=mu, w0=row(rw_w0[l]), w_up=_pad_rows(rw_w_up[l], LANES).astype(BF16), a0=row(rw_a0[l]),
                    a_up=_pad_rows(rw_a_up[l], LANES).astype(BF16),
                    g_up=_pad_rows(rw_g_up[l], 2 * LANES).astype(BF16),
                    k_k=row(rw_k_k[l]), k_a=row(rw_k_a[l]), r_k=row(rw_r_k[l]), seg=seg64)
        if l > 0:
            rw_p.update(v0=row(rw_v0[l - 1]), v_up=_pad_rows(rw_v_up[l - 1], LANES).astype(BF16))
        yn, bonus, g_rw, v_cur = rwkv_branch(h, w_rw, rw_p, v_first)
        if l == 0:
            v_first = v_cur

        w_ssd = jnp.concatenate([w_s[:, SSD_WIDTH:SSD_WIDTH + SSD_XBC], w_s[:, :SSD_WIDTH],
                                 _pad_cols(w_s[:, SSD_WIDTH + SSD_XBC:], LANES)], axis=1).astype(BF16)
        A = -jnp.exp(ssd_a_log[l].astype(F32))
        ssd_p = dict(conv_w=ssd_conv_w[l], conv_b=row(ssd_conv_b[l]),
                     dt_bias=_pad_cols(row(ssd_dt_bias[l]), LANES),
                     A=_pad_cols(row(A), LANES), A_x=row(jnp.repeat(A, SSD_HEAD_DIM)), expand=expand,
                     d_skip=row(jnp.repeat(ssd_d[l], SSD_HEAD_DIM)), norm_w=row(ssd_norm[l]))
        y_ssd = ssd_mixer(h, w_ssd, ssd_p)

        w_fox = jnp.concatenate([w_f[:, :3 * FOX_WIDTH], _pad_cols(w_f[:, 3 * FOX_WIDTH:], LANES)],
                                axis=1).astype(BF16)
        bound2 = (1.02 * FOX_HEAD_DIM ** 0.5 * LOG2E) * jnp.max(jnp.abs(fox_q_gain[l])) * jnp.max(jnp.abs(fox_k_gain[l]))
        fox_p = dict(f_bias=_pad_cols(row(fox_f_bias[l]), LANES),
                     q_gain=row(jnp.tile(fox_q_gain[l], FOX_HEADS)) * (FOX_HEAD_DIM ** -0.5 * LOG2E),
                     k_gain=row(jnp.tile(fox_k_gain[l], FOX_HEADS)), seg=seg64, bound2=bound2)
        y_fox = fox_branch(h, w_fox, fox_p)

        gate_logits = matmul(h, w_g.astype(BF16), out_dtype=BF16)
        mp = dict(gate_b=row(gate_b[l]), lnx_w=row(rw_lnx_w[l]), lnx_b=row(rw_lnx_b[l]),
                  proj_rw=proj_rw[l].astype(BF16), proj_ssd=proj_ssd[l].astype(BF16),
                  proj_fox=proj_fox[l].astype(BF16), w_out=w_out[l].astype(BF16), g_m=g_m,
                  norm_gain=row(norm_ffn[l]), sc_f=sc_f, sh_f=sh_f)

        if l % 2 == 0:
            xs, hf = merge(xs, yn, bonus, g_rw, y_ssd, y_fox, gate_logits, mp)
            next_norm = None
            if l + 1 < depth:
                next_norm = (row(norm_mix[l + 1]), mods[l + 1][1], mods[l + 1][0])
            xs, h = ffn_dense(hf, xs, ffn_w_gu[l // 2].astype(BF16), ffn_w_down[l // 2].astype(BF16), g_f,
                              next_norm)
        else:
            xs, hf, comb = merge(xs, yn, bonus, g_rw, y_ssd, y_fox, gate_logits, mp,
                                 router=_pad_cols(moe_router[l // 2], LANES))
            xs = ffn_moe(hf, xs, comb, moe_w_gu[l // 2].astype(BF16), moe_w_down[l // 2].astype(BF16), g_f)
            h = None
    return xs[None]
```

```python
import functools
import math

import jax
import jax.numpy as jnp
from jax import lax
from jax.experimental import pallas as pl
from jax.experimental.pallas import tpu as pltpu

F32 = jnp.float32
BF16 = jnp.bfloat16
HIGHEST = lax.Precision.HIGHEST

D_MODEL = 1024
LANES = 128
SUBLANES = 8

RW_HEADS = 8
RW_HEAD_DIM = 64
RW_WIDTH = RW_HEADS * RW_HEAD_DIM
RW_DECAY_LORA = 64
RW_AAA_LORA = 64
RW_VRES_LORA = 32
RW_GATE_LORA = 160
GN_EPS = 64e-5
RW_OFF_WLO = 3 * RW_WIDTH
RW_OFF_ALO = RW_OFF_WLO + LANES
RW_OFF_GLO = RW_OFF_ALO + LANES
RW_OFF_HV = RW_OFF_GLO + 2 * LANES
RW_PAD_COLS = RW_OFF_HV + LANES
RW_CHUNK = 64

SSD_HEADS = 16
SSD_HEAD_DIM = 64
SSD_WIDTH = SSD_HEADS * SSD_HEAD_DIM
SSD_GROUPS = 4
SSD_STATE = 128
SSD_CONV = 4
SSD_CHUNK = 128
SSD_XBC = SSD_WIDTH + 2 * SSD_GROUPS * SSD_STATE
SSD_GROUP_WIDTH = SSD_WIDTH // SSD_GROUPS

FOX_HEADS = 8
FOX_HEAD_DIM = 64
FOX_WIDTH = FOX_HEADS * FOX_HEAD_DIM
FOX_TQ = 256
FOX_TK = 512
LOG2E = 1.4426950408889634

FFN_DENSE = 2816
N_EXPERTS = 8
TOP_K = 2
FFN_EXPERT = 3584
MOE_SB = 128
MOE_CB = 512
EPS = 1e-6

VMEM_LIMIT = 56 * 1024 * 1024


def _cparams(*sem):
    return pltpu.CompilerParams(dimension_semantics=sem, vmem_limit_bytes=VMEM_LIMIT)


def _sigmoid(x):
    return 1.0 / (1.0 + jnp.exp(-x))


def _softplus(x):
    return jnp.maximum(x, 0.0) + jnp.log(1.0 + jnp.exp(-jnp.abs(x)))


def _dot(a, b):
    return jnp.dot(a, b, preferred_element_type=F32)


def _dot_bf16(a, b):
    return jnp.dot(a.astype(BF16), b.astype(BF16), preferred_element_type=F32)


def _dot_hi(a, b):
    return jnp.dot(a, b, precision=HIGHEST, preferred_element_type=F32)


def _dot_nt(a, b, precision=None):
    return lax.dot_general(a, b, (((1,), (1,)), ((), ())), precision=precision, preferred_element_type=F32)


def _dot_tn(a, b, precision=None):
    return lax.dot_general(a, b, (((0,), (0,)), ((), ())), precision=precision, preferred_element_type=F32)


def _split_dot(x, ones_bf16):
    hi = x.astype(BF16)
    lo = (x - hi.astype(F32)).astype(BF16)
    return _dot(hi, ones_bf16) + _dot(lo, ones_bf16)


def _split3(x):
    hi = x.astype(BF16)
    r1 = x - hi.astype(F32)
    mid = r1.astype(BF16)
    return hi, mid, (r1 - mid.astype(F32)).astype(BF16)


def _tril(n, strict=False):
    r = lax.broadcasted_iota(jnp.int32, (n, n), 0)
    c = lax.broadcasted_iota(jnp.int32, (n, n), 1)
    return (r > c) if strict else (r >= c)


def _mod_kernel(c_ref, w_ref, b_ref, o_ref):
    c = c_ref[...]
    o_ref[...] = _dot_hi(c * _sigmoid(c), w_ref[...]) + b_ref[...]


def adaln_mod(c, w, b):
    d, n = w.shape
    tn = 1024
    c8 = jnp.broadcast_to(c, (SUBLANES, d))
    out = pl.pallas_call(
        _mod_kernel,
        grid=(n // tn,),
        in_specs=[pl.BlockSpec((SUBLANES, d), lambda j: (0, 0)),
                  pl.BlockSpec((d, tn), lambda j: (0, j)),
                  pl.BlockSpec((1, tn), lambda j: (0, j))],
        out_specs=pl.BlockSpec((SUBLANES, tn), lambda j: (0, j)),
        out_shape=jax.ShapeDtypeStruct((SUBLANES, n), F32),
        compiler_params=_cparams("arbitrary"),
    )(c8, w, b.reshape(1, n))
    return out[:1]


def _norm_mod(x, gain, sc, sh):
    y = x * lax.rsqrt(jnp.mean(x * x, axis=-1, keepdims=True) + EPS)
    return y * gain * (1.0 + sc) + sh


def _norm_kernel(x_ref, gain_ref, sc_ref, sh_ref, h_ref):
    h_ref[...] = _norm_mod(x_ref[...], gain_ref[...], sc_ref[...], sh_ref[...]).astype(h_ref.dtype)


def norm_mod(x, gain, sc, sh, tm=512):
    L, d = x.shape
    row = pl.BlockSpec((1, d), lambda i: (0, 0))
    return pl.pallas_call(
        _norm_kernel,
        grid=(L // tm,),
        in_specs=[pl.BlockSpec((tm, d), lambda i: (i, 0)), row, row, row],
        out_specs=pl.BlockSpec((tm, d), lambda i: (i, 0)),
        out_shape=jax.ShapeDtypeStruct((L, d), BF16),
        compiler_params=_cparams("arbitrary"),
    )(x, gain, sc, sh)


def _mm_kernel(x_ref, w_ref, o_ref):
    o_ref[...] = _dot(x_ref[...], w_ref[...]).astype(o_ref.dtype)


def matmul(x, w, out_dtype=F32, tm=512):
    M, K = x.shape
    N = w.shape[1]
    return pl.pallas_call(
        _mm_kernel,
        grid=(M // tm,),
        in_specs=[pl.BlockSpec((tm, K), lambda i: (i, 0)),
                  pl.BlockSpec((K, N), lambda i: (0, 0))],
        out_specs=pl.BlockSpec((tm, N), lambda i: (i, 0)),
        out_shape=jax.ShapeDtypeStruct((M, N), out_dtype),
        compiler_params=_cparams("arbitrary"),
    )(x, w)


def _rw_pre_kernel(has_vres, *refs):
    if has_vres:
        (h_ref, w_ref, mu_ref, w0_ref, wup_ref, a0_ref, aup_ref, gup_ref, kk_ref, ka_ref, rk_ref, seg_ref,
         vfirst_ref, v0_ref, vup_ref,
         r_out, k_out, v_out, lw_out, a_out, b_out, g_out, bonus_out, ext_scr) = refs
    else:
        (h_ref, w_ref, mu_ref, w0_ref, wup_ref, a0_ref, aup_ref, gup_ref, kk_ref, ka_ref, rk_ref, seg_ref,
         r_out, k_out, v_out, lw_out, a_out, b_out, g_out, bonus_out, ext_scr) = refs
    T = h_ref.shape[0]

    @pl.when(pl.program_id(0) == 0)
    def _():
        ext_scr[0:SUBLANES, :] = jnp.zeros((SUBLANES, ext_scr.shape[1]), F32)

    cur = _dot(h_ref[...], w_ref[...])
    ext_scr[SUBLANES:SUBLANES + T, :] = cur
    prev = ext_scr[SUBLANES - 1:SUBLANES - 1 + T, :]
    ext_scr[0:SUBLANES, :] = cur[T - SUBLANES:T, :]
    s = cur + (prev - cur) * mu_ref[...]

    W = RW_WIDTH
    r = s[:, 0:W]
    k = s[:, W:2 * W]
    v = s[:, 2 * W:3 * W]
    w_lo = s[:, RW_OFF_WLO:RW_OFF_WLO + LANES]
    a_lo = s[:, RW_OFF_ALO:RW_OFF_ALO + LANES]
    g_lo = s[:, RW_OFF_GLO:RW_OFF_GLO + 2 * LANES]
    seg = seg_ref[...]

    wlog = -_softplus(-(w0_ref[...] + _dot_bf16(jnp.tanh(w_lo), wup_ref[...]))) - 0.5
    a = _sigmoid(a0_ref[...] + _dot_bf16(a_lo, aup_ref[...]))
    g = _dot_bf16(_sigmoid(g_lo), gup_ref[...])
    if has_vres:
        hv = s[:, RW_OFF_HV:RW_OFF_HV + LANES]
        v = v + (vfirst_ref[...].astype(F32) - v) * _sigmoid(v0_ref[...] + _dot_bf16(hv, vup_ref[...]))
    kk = k * kk_ref[...]
    kk = kk / jnp.maximum(jnp.sqrt(_split_dot(kk * kk, seg)), 1e-12)
    k = k * (1.0 + (a - 1.0) * ka_ref[...])
    r_out[...] = r.astype(r_out.dtype)
    k_out[...] = k.astype(k_out.dtype)
    v_out[...] = v.astype(v_out.dtype)
    lw_out[...] = -jnp.exp(wlog)
    a_out[...] = (-kk).astype(a_out.dtype)
    b_out[...] = (kk * a).astype(b_out.dtype)
    g_out[...] = g.astype(g_out.dtype)
    bonus_out[...] = (_split_dot(r * k * rk_ref[...], seg) * v).astype(bonus_out.dtype)


def rw_pre(h, w_rw, p, v_first, tm=512):
    L, D = h.shape
    has_vres = v_first is not None
    full = lambda a: pl.BlockSpec(a.shape, lambda i: (0,) * a.ndim)
    rows = lambda n: pl.BlockSpec((tm, n), lambda i: (i, 0))
    args = [h, w_rw, p["mu"], p["w0"], p["w_up"], p["a0"], p["a_up"], p["g_up"], p["k_k"], p["k_a"], p["r_k"],
            p["seg"]]
    specs = [rows(D)] + [full(a) for a in args[1:]]
    if has_vres:
        args += [v_first, p["v0"], p["v_up"]]
        specs += [rows(RW_WIDTH), full(p["v0"]), full(p["v_up"])]
    out_dtypes = [BF16, BF16, BF16, F32, BF16, BF16, BF16, BF16]
    return pl.pallas_call(
        functools.partial(_rw_pre_kernel, has_vres),
        grid=(L // tm,),
        in_specs=specs,
        out_specs=[rows(RW_WIDTH)] * 8,
        out_shape=[jax.ShapeDtypeStruct((L, RW_WIDTH), dt) for dt in out_dtypes],
        scratch_shapes=[pltpu.VMEM((tm + SUBLANES, RW_PAD_COLS), F32)],
        compiler_params=_cparams("arbitrary"),
    )(*args)


def _wkv_kernel(r_ref, lw_ref, k_ref, v_ref, a_ref, b_ref, y_ref, h_scr):
    C = RW_CHUNK
    C2 = 2 * C
    n_chunks = r_ref.shape[0] // C

    @pl.when(pl.program_id(1) == 0)
    def _():
        h_scr[...] = jnp.zeros_like(h_scr)

    is_a = lax.broadcasted_iota(jnp.int32, (C, LANES), 1) < RW_HEAD_DIM
    ri = lax.broadcasted_iota(jnp.int32, (C2, C2), 0)
    ci = lax.broadcasted_iota(jnp.int32, (C2, C2), 1)
    same = (ri >= C) == (ci >= C)
    strict = jnp.logical_and(same, ri > ci)
    incl = jnp.logical_and(same, ri >= ci)
    eye2 = (ri == ci).astype(F32)
    tri_c = _tril(C).astype(BF16)

    def two(x):
        return jnp.concatenate([jnp.where(is_a, x, 0.0), jnp.where(is_a, 0.0, x)], axis=0)

    chunks = range(n_chunks)
    at2, rt2, bh2, kh2, v2, n_ab, a_ak, m_rb, m_rk, p_end = ([] for _ in range(10))
    for c in chunks:
        sl = pl.ds(c * C, C)
        lw = lw_ref[sl, :]
        lw_hi = lw.astype(BF16)
        lw_lo = (lw - lw_hi.astype(F32)).astype(BF16)
        cs = _dot(tri_c, lw_hi) + _dot(tri_c, lw_lo)
        cs_end = cs[C - 1:C, :]
        e_neg = jnp.exp(-cs)
        e_end = jnp.exp(cs_end - cs)
        a = a_ref[sl, :].astype(F32)
        b = b_ref[sl, :].astype(F32)
        k = k_ref[sl, :].astype(F32)
        at2.append(two(a * jnp.exp(cs - lw)))
        rt2.append(two(r_ref[sl, :].astype(F32) * jnp.exp(cs)))
        bh2.append(two(b * e_end))
        kh2.append(two(k * e_end))
        v2.append(two(v_ref[sl, :].astype(F32)))
        p_end.append(jnp.exp(cs_end))
        quad = _dot_nt(jnp.concatenate([at2[c], rt2[c]], axis=0).astype(BF16),
                       jnp.concatenate([two(b * e_neg), two(k * e_neg)], axis=0).astype(BF16))
        n_ab.append(jnp.where(strict, quad[0:C2, 0:C2], 0.0))
        a_ak.append(jnp.where(strict, quad[0:C2, C2:], 0.0))
        m_rb.append(jnp.where(incl, quad[C2:, 0:C2], 0.0))
        m_rk.append(jnp.where(incl, quad[C2:, C2:], 0.0))
    t_inv = [eye2 + n for n in n_ab]
    n_pow = n_ab
    for _ in range(int(math.log2(C)) - 1):
        n_pow = [_dot_bf16(n, n) for n in n_pow]
        t_inv = [t + _dot_bf16(t, n) for t, n in zip(t_inv, n_pow)]
    akv = [_dot_bf16(a_ak[c], v2[c]) for c in chunks]
    au = [_dot_bf16(t_inv[c], jnp.concatenate([at2[c], akv[c]], axis=1)) for c in chunks]
    mau = [_dot_bf16(m_rb[c], au[c]) for c in chunks]
    mkv = [_dot_bf16(m_rk[c], v2[c]) for c in chunks]
    gu = [_dot_tn(bh2[c].astype(BF16), au[c].astype(BF16)) for c in chunks]
    kv = [_dot_tn(kh2[c].astype(BF16), v2[c].astype(BF16)) for c in chunks]
    inv_n = 1.0 / RW_HEAD_DIM
    h = h_scr[...]
    for c in chunks:
        rh2 = rt2[c] + mau[c][:, :LANES]
        g_mat = eye2 * p_end[c] + gu[c][:, :LANES]
        yh = _dot_bf16(jnp.concatenate([rh2, g_mat], axis=0), h)
        h = yh[C2:, :] + gu[c][:, LANES:] + kv[c]
        y2 = yh[:C2, :] + mau[c][:, LANES:] + mkv[c]
        y = y2[:C, :] + y2[C:, :]
        s_all = jnp.sum(y, axis=-1, keepdims=True)
        s_a = jnp.sum(jnp.where(is_a, y, 0.0), axis=-1, keepdims=True)
        yc = y - jnp.where(is_a, s_a, s_all - s_a) * inv_n
        sq = yc * yc
        q_all = jnp.sum(sq, axis=-1, keepdims=True)
        q_a = jnp.sum(jnp.where(is_a, sq, 0.0), axis=-1, keepdims=True)
        yn = yc * lax.rsqrt(jnp.where(is_a, q_a, q_all - q_a) * inv_n + GN_EPS)
        y_ref[pl.ds(c * C, C), :] = yn.astype(y_ref.dtype)
    h_scr[...] = h


def wkv7(r, lw, k, v, a, b, rows=512):
    L, W = r.shape
    spec = pl.BlockSpec((rows, LANES), lambda p, i: (i, p))
    return pl.pallas_call(
        _wkv_kernel,
        grid=(W // LANES, L // rows),
        in_specs=[spec] * 6,
        out_specs=spec,
        out_shape=jax.ShapeDtypeStruct((L, W), BF16),
        scratch_shapes=[pltpu.VMEM((LANES, LANES), F32)],
        compiler_params=_cparams("arbitrary", "arbitrary"),
    )(r, lw, k, v, a, b)


def _ssd_kernel(h_ref, w_ref, cw_ref, cb_ref, bias_ref, A_ref, Ax_ref, exp_ref, dskip_ref, nw_ref, o_ref,
                ext_scr, xbc_scr, s_scr):
    Q = SSD_CHUNK
    GW = SSD_GROUP_WIDTH
    NS = SSD_STATE
    R = h_ref.shape[0]

    @pl.when(pl.program_id(0) == 0)
    def _():
        s_scr[...] = jnp.zeros_like(s_scr)
        ext_scr[0:SUBLANES, :] = jnp.zeros((SUBLANES, ext_scr.shape[1]), F32)

    cols = _dot(h_ref[...], w_ref[...])
    ext_scr[SUBLANES:SUBLANES + R, :] = cols[:, 0:SSD_XBC]
    acc = cols[:, 0:SSD_XBC] * cw_ref[SSD_CONV - 1:SSD_CONV, :] + cb_ref[...]
    for j in range(1, SSD_CONV):
        acc = acc + ext_scr[SUBLANES - j:SUBLANES - j + R, :] * cw_ref[SSD_CONV - 1 - j:SSD_CONV - j, :]
    ext_scr[0:SUBLANES, :] = ext_scr[R:R + SUBLANES, :]
    xbc_scr[...] = acc * _sigmoid(acc)

    incl = _tril(Q)
    tri = incl.astype(BF16)
    triu = jnp.logical_not(_tril(Q, strict=True)).astype(BF16)
    expand = exp_ref[...]
    first_half = lax.broadcasted_iota(jnp.int32, (Q, LANES), 1) < SSD_HEAD_DIM
    heads_per_group = SSD_HEADS // SSD_GROUPS
    for c in range(R // Q):
        rows = slice(c * Q, (c + 1) * Q)
        xs = xbc_scr[rows, 0:SSD_WIDTH]
        z = cols[rows, SSD_XBC:SSD_XBC + SSD_WIDTH]
        dt = _softplus(cols[rows, SSD_XBC + SSD_WIDTH:] + bias_ref[...])
        a3 = _split3(dt * A_ref[...])
        a_cum = sum(_dot(tri, t) for t in a3)
        a_cumT = sum(_dot_tn(t, triu) for t in a3)
        dt_x = sum(_dot(t, expand) for t in _split3(dt))
        acum_x = sum(_dot(tri, t) for t in _split3(dt_x * Ax_ref[...]))
        eac_x = jnp.exp(acum_x)
        dte_x = jnp.exp(acum_x[Q - 1:Q, :] - acum_x)
        xdt = xs * dt_x
        for g in range(SSD_GROUPS):
            Bg = xbc_scr[rows, SSD_WIDTH + g * NS:SSD_WIDTH + (g + 1) * NS].astype(BF16)
            Cg = xbc_scr[rows, SSD_WIDTH + (SSD_GROUPS + g) * NS:SSD_WIDTH + (SSD_GROUPS + g + 1) * NS].astype(BF16)
            cb = _dot_nt(Cg, Bg)
            gs = slice(g * GW, (g + 1) * GW)
            s_prev = s_scr[:, gs]
            y_g = _dot_bf16(Cg, s_prev) * eac_x[:, gs]
            pieces = []
            for pr in range(heads_per_group // 2):
                ps = slice(g * GW + pr * LANES, g * GW + (pr + 1) * LANES)
                xdt_p = xdt[:, ps]
                acc_p = None
                for e in range(2):
                    hd = g * heads_per_group + pr * 2 + e
                    seg = a_cum[:, hd:hd + 1] - a_cumT[hd:hd + 1, :]
                    m = cb * jnp.exp(jnp.where(incl, seg, -jnp.inf))
                    xm = jnp.where(first_half if e == 0 else jnp.logical_not(first_half), xdt_p, 0.0)
                    t = _dot_bf16(m, xm)
                    acc_p = t if acc_p is None else acc_p + t
                pieces.append(acc_p)
            y_g = y_g + jnp.concatenate(pieces, axis=1)
            s_scr[:, gs] = s_prev * eac_x[Q - 1:Q, gs] + _dot_tn(Bg, (dte_x[:, gs] * xdt[:, gs]).astype(BF16))
            y_g = y_g + dskip_ref[:, gs] * xs[:, gs]
            zg = z[:, gs]
            y_g = y_g * (zg * _sigmoid(zg))
            y_g = y_g * lax.rsqrt(jnp.mean(y_g * y_g, axis=-1, keepdims=True) + EPS)
            o_ref[rows, gs] = (y_g * nw_ref[:, gs]).astype(o_ref.dtype)


def ssd_mixer(h, w_ssd, p, rows=512):
    L, D = h.shape
    N = w_ssd.shape[1]
    full = lambda a: pl.BlockSpec(a.shape, lambda i: (0,) * a.ndim)
    small = [p["conv_w"], p["conv_b"], p["dt_bias"], p["A"], p["A_x"], p["expand"], p["d_skip"], p["norm_w"]]
    return pl.pallas_call(
        _ssd_kernel,
        grid=(L // rows,),
        in_specs=[pl.BlockSpec((rows, D), lambda i: (i, 0)), full(w_ssd)] + [full(a) for a in small],
        out_specs=pl.BlockSpec((rows, SSD_WIDTH), lambda i: (i, 0)),
        out_shape=jax.ShapeDtypeStruct((L, SSD_WIDTH), BF16),
        scratch_shapes=[pltpu.VMEM((rows + SUBLANES, SSD_XBC), F32), pltpu.VMEM((rows, SSD_XBC), F32),
                        pltpu.VMEM((SSD_STATE, SSD_WIDTH), F32)],
        compiler_params=_cparams("arbitrary"),
    )(h, w_ssd, *small)


def _fox_pre_kernel(h_ref, w_ref, fb_ref, qg_ref, kg_ref, seg_ref, q_out, k_out, v_out, f_out, carry_scr):
    T = h_ref.shape[0]
    W = FOX_WIDTH

    @pl.when(pl.program_id(0) == 0)
    def _():
        carry_scr[...] = jnp.zeros_like(carry_scr)

    cols = _dot(h_ref[...], w_ref[...])
    seg = seg_ref[...]
    q = cols[:, 0:W]
    k = cols[:, W:2 * W]
    inv_d = 1.0 / FOX_HEAD_DIM
    qn = q * lax.rsqrt(_split_dot(q * q, seg) * inv_d + EPS) * qg_ref[...]
    kn = k * lax.rsqrt(_split_dot(k * k, seg) * inv_d + EPS) * kg_ref[...]
    q_out[...] = qn.astype(q_out.dtype)
    k_out[...] = kn.astype(k_out.dtype)
    v_out[...] = cols[:, 2 * W:3 * W].astype(v_out.dtype)
    f = cols[:, 3 * W:3 * W + LANES] + fb_ref[...]
    ls = -_softplus(-f)
    cum = _dot_hi(_tril(T).astype(F32), ls) + carry_scr[0:1, :]
    f_out[...] = cum * LOG2E
    carry_scr[...] = jnp.broadcast_to(cum[T - 1:T, :], carry_scr.shape)


def fox_pre(h, w_fox, p, tm=512):
    L, D = h.shape
    full = lambda a: pl.BlockSpec(a.shape, lambda i: (0,) * a.ndim)
    rows = lambda n: pl.BlockSpec((tm, n), lambda i: (i, 0))
    small = [w_fox, p["f_bias"], p["q_gain"], p["k_gain"], p["seg"]]
    return pl.pallas_call(
        _fox_pre_kernel,
        grid=(L // tm,),
        in_specs=[rows(D)] + [full(a) for a in small],
        out_specs=[rows(FOX_WIDTH)] * 3 + [rows(LANES)],
        out_shape=[jax.ShapeDtypeStruct((L, FOX_WIDTH), BF16)] * 3 + [jax.ShapeDtypeStruct((L, LANES), F32)],
        scratch_shapes=[pltpu.VMEM((SUBLANES, LANES), F32)],
        compiler_params=_cparams("arbitrary"),
    )(h, *small)


def _fox_kernel(lo_ref, q_ref, k_ref, v_ref, fk_ref, o_ref, m_scr, l_scr, acc_scr, s0_scr, s1_scr):
    tq = q_ref.shape[0]
    tk = FOX_TK
    pair = pl.program_id(0)
    qi = pl.program_id(1)
    is_a = lax.broadcasted_iota(jnp.int32, (tq, LANES), 1) < FOX_HEAD_DIM
    q = q_ref[...]
    zero = jnp.zeros_like(q)
    qs = (jnp.where(is_a, q, zero), jnp.where(is_a, zero, q))
    m_scr[...] = jnp.full(m_scr.shape, -jnp.inf, F32)
    l_scr[...] = jnp.zeros_like(l_scr)
    acc_scr[...] = jnp.zeros_like(acc_scr)
    rel = lax.broadcasted_iota(jnp.int32, (tq, tk), 0) - lax.broadcasted_iota(jnp.int32, (tq, tk), 1)
    last = (qi * tq) // tk

    def fill(s_ref, kb, diagonal=False):
        ks = pl.ds(pl.multiple_of(kb * tk, tk), tk)
        k_blk = k_ref[ks, :]
        for e in range(2):
            s = _dot_nt(qs[e], k_blk) - fk_ref[pl.ds(2 * pair + e, 1), ks]
            if diagonal:
                s = jnp.where(rel >= kb * tk - qi * tq, s, -jnp.inf)
            s_ref[e] = s

    def consume(s_ref, kb):
        v_blk = v_ref[pl.ds(pl.multiple_of(kb * tk, tk), tk), :]
        for e in range(2):
            s = s_ref[e]
            m_old = m_scr[e]
            m_new = jnp.maximum(m_old, jnp.max(s, axis=-1, keepdims=True))
            pexp = jnp.exp2(s - m_new)
            alpha = jnp.exp2(m_old - m_new)
            l_scr[e] = alpha * l_scr[e] + jnp.sum(pexp, axis=-1, keepdims=True)
            acc_scr[e] = alpha * acc_scr[e] + _dot(pexp.astype(BF16), v_blk)
            m_scr[e] = m_new

    n = last - jnp.minimum(lo_ref[2 * pair, qi], lo_ref[2 * pair + 1, qi])
    fill(s0_scr, last, diagonal=True)

    def two_steps(u, carry):
        kb = last - 2 * u
        fill(s1_scr, kb - 1)
        consume(s0_scr, kb)
        fill(s0_scr, kb - 2)
        consume(s1_scr, kb - 1)
        return carry

    lax.fori_loop(0, n // 2, two_steps, 0)

    @pl.when(n % 2 == 1)
    def _():
        fill(s1_scr, last - n)
        consume(s0_scr, last - n + 1)
        consume(s1_scr, last - n)

    @pl.when(n % 2 == 0)
    def _():
        consume(s0_scr, last - n)

    o_ref[...] = jnp.where(is_a, acc_scr[0] / l_scr[0], acc_scr[1] / l_scr[1]).astype(o_ref.dtype)


def fox_attention(q, k, v, FT, lo):
    L, W = q.shape
    H = FT.shape[0]
    tq = FOX_TQ
    grid_spec = pltpu.PrefetchScalarGridSpec(
        num_scalar_prefetch=1,
        grid=(W // LANES, L // tq),
        in_specs=[pl.BlockSpec((tq, LANES), lambda p, i, lo_r: (i, p)),
                  pl.BlockSpec((L, LANES), lambda p, i, lo_r: (0, p)),
                  pl.BlockSpec((L, LANES), lambda p, i, lo_r: (0, p)),
                  pl.BlockSpec((H, L), lambda p, i, lo_r: (0, 0))],
        out_specs=pl.BlockSpec((tq, LANES), lambda p, i, lo_r: (i, p)),
        scratch_shapes=[pltpu.VMEM((2, tq, 1), F32), pltpu.VMEM((2, tq, 1), F32),
                        pltpu.VMEM((2, tq, LANES), F32),
                        pltpu.VMEM((2, tq, FOX_TK), F32), pltpu.VMEM((2, tq, FOX_TK), F32)],
    )
    return pl.pallas_call(
        _fox_kernel,
        grid_spec=grid_spec,
        out_shape=jax.ShapeDtypeStruct((L, W), BF16),
        compiler_params=_cparams("arbitrary", "arbitrary"),
    )(lo, q, k, v, FT)


def fox_first_block(FT, bound2):
    f_first = FT[:, ::FOX_TQ]
    f_last = FT[:, FOX_TK - 1::FOX_TK]
    gap = f_first[:, :, None] - f_last[:, None, :] + 2.0 * bound2
    last = (jnp.arange(f_first.shape[1]) * FOX_TQ) // FOX_TK
    needed = jnp.logical_or(gap >= -152.0, jnp.arange(f_last.shape[1])[None, None, :] >= last[None, :, None])
    return jnp.argmax(needed, axis=-1).astype(jnp.int32)


def _top2_combine(h, wr):
    T = h.shape[0]
    lane = lax.broadcasted_iota(jnp.int32, (T, LANES), 1)
    h_hi = h.astype(BF16)
    h_lo = (h - h_hi.astype(F32)).astype(BF16)
    w_hi = wr.astype(BF16)
    w_lo = (wr - w_hi.astype(F32)).astype(BF16)
    logits = _dot(h_hi, w_hi) + (_dot(h_hi, w_lo) + _dot(h_lo, w_hi))
    logits = jnp.where(lane < N_EXPERTS, logits, -jnp.inf)
    m1 = jnp.max(logits, axis=-1, keepdims=True)
    i1 = jnp.min(jnp.where(logits == m1, lane, LANES), axis=-1, keepdims=True)
    rest = jnp.where(lane == i1, -jnp.inf, logits)
    m2 = jnp.max(rest, axis=-1, keepdims=True)
    i2 = jnp.min(jnp.where(rest == m2, lane, LANES), axis=-1, keepdims=True)
    e2 = jnp.exp(m2 - m1)
    return jnp.where(lane == i1, 1.0 / (1.0 + e2), 0.0) + jnp.where(lane == i2, e2 / (1.0 + e2), 0.0)


def _merge_kernel(with_router, *refs):
    (x_ref, yn_ref, bonus_ref, g_ref, yssd_ref, yfox_ref, gl_ref, gb_ref, lnw_ref, lnb_ref,
     prw_ref, pssd_ref, pfox_ref, wout_ref, gm_ref, gain_ref, sc_ref, sh_ref) = refs[:18]
    D = D_MODEL
    y_rw = (yn_ref[...] * lnw_ref[...] + lnb_ref[...] + bonus_ref[...]) * g_ref[...]
    gates = _sigmoid(gl_ref[...] + gb_ref[...])
    merged = gates[:, 0:D] * _dot_bf16(y_rw, prw_ref[...])
    merged = merged + gates[:, D:2 * D] * _dot_bf16(yssd_ref[...], pssd_ref[...])
    merged = merged + gates[:, 2 * D:3 * D] * _dot_bf16(yfox_ref[...], pfox_ref[...])
    x_new = x_ref[...] + gm_ref[...] * _dot_bf16(merged, wout_ref[...])
    hf = _norm_mod(x_new, gain_ref[...], sc_ref[...], sh_ref[...])
    if with_router:
        wr_ref, o_ref, h_ref, comb_ref = refs[18:]
        comb_ref[...] = _top2_combine(hf, wr_ref[...])
    else:
        o_ref, h_ref = refs[18:]
    o_ref[...] = x_new
    h_ref[...] = hf.astype(h_ref.dtype)


def merge(x, yn, bonus, g, y_ssd, y_fox, gate_logits, p, router=None, tm=512):
    L, D = x.shape
    full = lambda a: pl.BlockSpec(a.shape, lambda i: (0,) * a.ndim)
    rows = lambda n: pl.BlockSpec((tm, n), lambda i: (i, 0))
    small = [p["gate_b"], p["lnx_w"], p["lnx_b"], p["proj_rw"], p["proj_ssd"], p["proj_fox"], p["w_out"], p["g_m"],
             p["norm_gain"], p["sc_f"], p["sh_f"]]
    out_specs = [rows(D), rows(D)]
    out_shape = [jax.ShapeDtypeStruct((L, D), F32), jax.ShapeDtypeStruct((L, D), BF16)]
    if router is not None:
        small.append(router)
        out_specs.append(rows(LANES))
        out_shape.append(jax.ShapeDtypeStruct((L, LANES), F32))
    return pl.pallas_call(
        functools.partial(_merge_kernel, router is not None),
        grid=(L // tm,),
        in_specs=[rows(D), rows(RW_WIDTH), rows(RW_WIDTH), rows(RW_WIDTH), rows(SSD_WIDTH), rows(FOX_WIDTH),
                  rows(3 * D)] + [full(a) for a in small],
        out_specs=out_specs,
        out_shape=out_shape,
        compiler_params=_cparams("arbitrary"),
    )(x, yn, bonus, g, y_ssd, y_fox, gate_logits, *small)


def _ffn_kernel(with_next, *refs):
    h_ref, x_ref, wg_ref, wu_ref, wd_ref, gf_ref = refs[:6]
    j = pl.program_id(1)
    acc_scr = refs[-1]

    @pl.when(j == 0)
    def _():
        acc_scr[...] = jnp.zeros_like(acc_scr)

    h = h_ref[...]
    gte = _dot(h, wg_ref[...])
    up = _dot(h, wu_ref[...])
    act = gte * _sigmoid(gte) * up
    acc_scr[...] += _dot(act.astype(BF16), wd_ref[...])

    @pl.when(j == pl.num_programs(1) - 1)
    def _():
        x_new = x_ref[...] + gf_ref[...] * acc_scr[...]
        if with_next:
            gain_ref, sc_ref, sh_ref, o_ref, hn_ref = refs[6:11]
            hn_ref[...] = _norm_mod(x_new, gain_ref[...], sc_ref[...], sh_ref[...]).astype(hn_ref.dtype)
        else:
            o_ref = refs[6]
        o_ref[...] = x_new


def ffn_dense(h, x, w_gu, w_down, g_f, next_norm=None, tm=1024, tf=256):
    L, D = x.shape
    Fh = w_down.shape[0]
    nf = Fh // tf
    row = pl.BlockSpec((1, D), lambda i, j: (0, 0))
    tile = pl.BlockSpec((tm, D), lambda i, j: (i, 0))
    args = [h, x, w_gu, w_gu, w_down, g_f]
    in_specs = [tile, tile,
                pl.BlockSpec((D, tf), lambda i, j: (0, j)),
                pl.BlockSpec((D, tf), lambda i, j: (0, j + nf)),
                pl.BlockSpec((tf, D), lambda i, j: (j, 0)), row]
    out_specs = [tile]
    out_shape = [jax.ShapeDtypeStruct((L, D), F32)]
    if next_norm is not None:
        args += list(next_norm)
        in_specs += [row, row, row]
        out_specs.append(tile)
        out_shape.append(jax.ShapeDtypeStruct((L, D), BF16))
    out = pl.pallas_call(
        functools.partial(_ffn_kernel, next_norm is not None),
        grid=(L // tm, nf),
        in_specs=in_specs,
        out_specs=out_specs,
        out_shape=out_shape,
        scratch_shapes=[pltpu.VMEM((tm, D), F32)],
        compiler_params=_cparams("arbitrary", "arbitrary"),
    )(*args)
    return out if next_norm is not None else (out[0], None)


def _moe_slots(slot_scr, comb_ref):
    slotval = slot_scr[...]
    comb = comb_ref[...]
    s1 = jnp.max(slotval, axis=-1, keepdims=True)
    first = slotval == s1
    c1 = jnp.sum(jnp.where(first, comb, 0.0), axis=-1, keepdims=True)
    rest = jnp.where(first, -1.0, slotval)
    s2 = jnp.max(rest, axis=-1, keepdims=True)
    c2 = jnp.sum(jnp.where(jnp.logical_and(rest == s2, rest >= 0.0), comb, 0.0), axis=-1, keepdims=True)
    return s1, s2, c1, c2


def _moe_kernel(nsb_ref, off_ref, h_ref, x_ref, comb_ref, wg_ref, wu_ref, wd_ref, gf_ref, o_ref,
                slot_scr, xy_scr, acc_scr):
    i = pl.program_id(0)
    e = pl.program_id(1)
    j = pl.program_id(2)
    last_j = pl.num_programs(2) - 1
    T = h_ref.shape[0]
    S = xy_scr.shape[0]
    SB = MOE_SB

    @pl.when(jnp.logical_and(e == 0, j == 0))
    def _():
        lane = lax.broadcasted_iota(jnp.int32, (LANES, LANES), 1)
        off = jnp.zeros((LANES, LANES), F32)
        for ex in range(N_EXPERTS):
            off = jnp.where(lane == ex, off_ref[i, ex].astype(F32), off)
        before = _tril(LANES, strict=True).astype(BF16)
        ones = jnp.ones((LANES, LANES), BF16)
        for rb in range(T // LANES):
            rows = slice(rb * LANES, (rb + 1) * LANES)
            sel = comb_ref[rows, :] > 0.0
            sel_b = jnp.where(sel, 1.0, 0.0).astype(BF16)
            slot_scr[rows, :] = jnp.where(sel, _dot(before, sel_b) + off, -1.0)
            off = off + _dot(ones, sel_b)
        s1, s2, _, _ = _moe_slots(slot_scr, comb_ref)
        for cb in range(S // MOE_CB):
            slot = (lax.broadcasted_iota(jnp.int32, (T, MOE_CB), 1) + cb * MOE_CB).astype(F32)
            pt = jnp.where(jnp.logical_or(s1 == slot, s2 == slot), 1.0, 0.0).astype(BF16)
            xy_scr[cb * MOE_CB:(cb + 1) * MOE_CB, :] = _dot_tn(pt, h_ref[...]).astype(BF16)

    base = off_ref[i, e]
    nb = nsb_ref[i, e]

    def ffn_block(r0, n_rows):
        rows = pl.ds(pl.multiple_of(base + r0, SB), n_rows)
        arows = pl.ds(pl.multiple_of(r0, SB), n_rows)
        xb = xy_scr[rows, :]
        gte = _dot(xb, wg_ref[0])
        up = _dot(xb, wu_ref[0])
        part = _dot((gte * _sigmoid(gte) * up).astype(BF16), wd_ref[0])

        @pl.when(j == 0)
        def _():
            acc_scr[arows, :] = part

        @pl.when(jnp.logical_and(j > 0, j < last_j))
        def _():
            acc_scr[arows, :] += part

        @pl.when(j == last_j)
        def _():
            xy_scr[rows, :] = (acc_scr[arows, :] + part).astype(BF16)

    def pair_body(p, carry):
        ffn_block(p * (2 * SB), 2 * SB)
        return carry

    lax.fori_loop(0, nb // 2, pair_body, 0)

    @pl.when(nb % 2 == 1)
    def _():
        ffn_block((nb - 1) * SB, SB)

    @pl.when(jnp.logical_and(e == pl.num_programs(1) - 1, j == last_j))
    def _():
        s1, s2, c1, c2 = _moe_slots(slot_scr, comb_ref)
        total = jnp.zeros((T, xy_scr.shape[1]), F32)
        for cb in range(S // MOE_CB):
            slot = (lax.broadcasted_iota(jnp.int32, (T, MOE_CB), 1) + cb * MOE_CB).astype(F32)
            w = jnp.where(s1 == slot, c1, 0.0) + jnp.where(s2 == slot, c2, 0.0)
            w_hi = w.astype(BF16)
            w_lo = (w - w_hi.astype(F32)).astype(BF16)
            y = xy_scr[cb * MOE_CB:(cb + 1) * MOE_CB, :]
            total = total + (_dot(w_hi, y) + _dot(w_lo, y))
        o_ref[...] = x_ref[...] + gf_ref[...] * total


def ffn_moe(h, x, comb, w_gu, w_down, g_f, tm=1024, tf=896):
    L, D = x.shape
    E, Fh, _ = w_down.shape
    nf = Fh // tf
    assert nf >= 2 and E == N_EXPERTS
    nt = L // tm
    counts = jnp.sum((comb[:, :E] > 0.0).reshape(nt, tm, E), axis=1, dtype=jnp.int32)
    nsb = (counts + (MOE_SB - 1)) // MOE_SB
    off = (jnp.cumsum(nsb, axis=1) - nsb) * MOE_SB
    idx = lambda f: (lambda i, e, j, n, o: f(i, e, j))
    grid_spec = pltpu.PrefetchScalarGridSpec(
        num_scalar_prefetch=2,
        grid=(nt, E, nf),
        in_specs=[pl.BlockSpec((tm, D), idx(lambda i, e, j: (i, 0))),
                  pl.BlockSpec((tm, D), idx(lambda i, e, j: (i, 0))),
                  pl.BlockSpec((tm, LANES), idx(lambda i, e, j: (i, 0))),
                  pl.BlockSpec((1, D, tf), idx(lambda i, e, j: (e, 0, j))),
                  pl.BlockSpec((1, D, tf), idx(lambda i, e, j: (e, 0, j + nf))),
                  pl.BlockSpec((1, tf, D), idx(lambda i, e, j: (e, j, 0))),
                  pl.BlockSpec((1, D), idx(lambda i, e, j: (0, 0)))],
        out_specs=pl.BlockSpec((tm, D), idx(lambda i, e, j: (i, 0))),
        scratch_shapes=[pltpu.VMEM((tm, LANES), F32),
                        pltpu.VMEM((TOP_K * tm + E * MOE_SB, D), BF16),
                        pltpu.VMEM((tm, D), F32)],
    )
    return pl.pallas_call(
        _moe_kernel,
        grid_spec=grid_spec,
        out_shape=jax.ShapeDtypeStruct((L, D), F32),
        compiler_params=_cparams("arbitrary", "arbitrary", "arbitrary"),
    )(nsb, off, h, x, comb, w_gu, w_gu, w_down, g_f)


def _seg_matrix(width, head_dim):
    idx = jnp.arange(width) // head_dim
    return (idx[:, None] == idx[None, :]).astype(BF16)


def _pad_cols(w, n):
    return jnp.pad(w, ((0, 0), (0, n - w.shape[1])))


def _pad_rows(w, n):
    return jnp.pad(w, ((0, n - w.shape[0]), (0, 0)))


def rwkv_branch(h, w_rw, p, v_first):
    r, k, v, lw, a, b, g, bonus = rw_pre(h, w_rw, p, v_first)
    return wkv7(r, lw, k, v, a, b), bonus, g, v


def fox_branch(h, w_fox, p):
    q, k, v, F = fox_pre(h, w_fox, p)
    FT = jnp.transpose(F[:, :FOX_HEADS])
    lo = fox_first_block(FT, p["bound2"])
    return fox_attention(q, k, v, FT, lo)


def kernel(x, c, ada_w, ada_b, norm_mix, norm_ffn, w_in, rw_mu, rw_w0, rw_w_up, rw_a0, rw_a_up, rw_g_up, rw_k_k, rw_k_a, rw_r_k, rw_lnx_w, rw_lnx_b, rw_v0, rw_v_down, rw_v_up, ssd_conv_w, ssd_conv_b, ssd_dt_bias, ssd_a_log, ssd_d, ssd_norm, fox_f_bias, fox_q_gain, fox_k_gain, gate_b, proj_rw, proj_ssd, proj_fox, w_out, ffn_w_gu, ffn_w_down, moe_router, moe_w_gu, moe_w_down):
    depth = w_in.shape[0]
    D = D_MODEL
    xs = x[0]
    row = lambda t: t.reshape(1, -1).astype(F32)
    seg64 = _seg_matrix(RW_WIDTH, RW_HEAD_DIM)
    rw_cols = 3 * RW_WIDTH + RW_DECAY_LORA + RW_AAA_LORA + RW_GATE_LORA
    ssd_cols = SSD_WIDTH + SSD_XBC + SSD_HEADS
    fox_cols = 3 * FOX_WIDTH + FOX_HEADS
    expand = (jnp.arange(LANES)[:, None] == (jnp.arange(SSD_WIDTH) // SSD_HEAD_DIM)[None, :]).astype(BF16)
    v_first = None
    mods = []
    for l in range(depth):
        mod = adaln_mod(c, ada_w[l], ada_b[l])
        mods.append([mod[:, i * D:(i + 1) * D] for i in range(6)])
    h = None
    for l in range(depth):
        sh_m, sc_m, g_m, sh_f, sc_f, g_f = mods[l]
        if h is None:
            h = norm_mod(xs, row(norm_mix[l]), sc_m, sh_m)

        wl = w_in[l]
        o = 0
        w_r = wl[:, o:o + rw_cols]; o += rw_cols
        w_s = wl[:, o:o + ssd_cols]; o += ssd_cols
        w_f = wl[:, o:o + fox_cols]; o += fox_cols
        w_g = wl[:, o:]
        W3 = 3 * RW_WIDTH
        o_a = W3 + RW_DECAY_LORA
        o_g = o_a + RW_AAA_LORA

        def rw_layout(t, hv):
            parts = [t[:, :W3], _pad_cols(t[:, W3:o_a], LANES), _pad_cols(t[:, o_a:o_g], LANES),
                     _pad_cols(t[:, o_g:], 2 * LANES), _pad_cols(hv, LANES)]
            return jnp.concatenate(parts, axis=1)

        if l == 0:
            hv_w = jnp.zeros((D, RW_VRES_LORA), F32)
        else:
            hv_w = rw_v_down[l - 1]
        w_rw = rw_layout(w_r, hv_w).astype(BF16)
        mu = rw_layout(rw_mu[l].reshape(1, -1), jnp.zeros((1, RW_VRES_LORA), F32))
        rw_p = dict(mu=mu, w0=row(rw_w0[l]), w_up=_pad_rows(rw_w_up[l], LANES).astype(BF16), a0=row(rw_a0[l]),
                    a_up=_pad_rows(rw_a_up[l], LANES).astype(BF16),
                    g_up=_pad_rows(rw_g_up[l], 2 * LANES).astype(BF16),
                    k_k=row(rw_k_k[l]), k_a=row(rw_k_a[l]), r_k=row(rw_r_k[l]), seg=seg64)
        if l > 0:
            rw_p.update(v0=row(rw_v0[l - 1]), v_up=_pad_rows(rw_v_up[l - 1], LANES).astype(BF16))
        yn, bonus, g_rw, v_cur = rwkv_branch(h, w_rw, rw_p, v_first)
        if l == 0:
            v_first = v_cur

        w_ssd = jnp.concatenate([w_s[:, SSD_WIDTH:SSD_WIDTH + SSD_XBC], w_s[:, :SSD_WIDTH],
                                 _pad_cols(w_s[:, SSD_WIDTH + SSD_XBC:], LANES)], axis=1).astype(BF16)
        A = -jnp.exp(ssd_a_log[l].astype(F32))
        ssd_p = dict(conv_w=ssd_conv_w[l], conv_b=row(ssd_conv_b[l]),
                     dt_bias=_pad_cols(row(ssd_dt_bias[l]), LANES),
                     A=_pad_cols(row(A), LANES), A_x=row(jnp.repeat(A, SSD_HEAD_DIM)), expand=expand,
                     d_skip=row(jnp.repeat(ssd_d[l], SSD_HEAD_DIM)), norm_w=row(ssd_norm[l]))
        y_ssd = ssd_mixer(h, w_ssd, ssd_p)

        w_fox = jnp.concatenate([w_f[:, :3 * FOX_WIDTH], _pad_cols(w_f[:, 3 * FOX_WIDTH:], LANES)],
                                axis=1).astype(BF16)
        bound2 = (1.02 * FOX_HEAD_DIM ** 0.5 * LOG2E) * jnp.max(jnp.abs(fox_q_gain[l])) * jnp.max(jnp.abs(fox_k_gain[l]))
        fox_p = dict(f_bias=_pad_cols(row(fox_f_bias[l]), LANES),
                     q_gain=row(jnp.tile(fox_q_gain[l], FOX_HEADS)) * (FOX_HEAD_DIM ** -0.5 * LOG2E),
                     k_gain=row(jnp.tile(fox_k_gain[l], FOX_HEADS)), seg=seg64, bound2=bound2)
        y_fox = fox_branch(h, w_fox, fox_p)

        gate_logits = matmul(h, w_g.astype(BF16), out_dtype=BF16)
        mp = dict(gate_b=row(gate_b[l]), lnx_w=row(rw_lnx_w[l]), lnx_b=row(rw_lnx_b[l]),
                  proj_rw=proj_rw[l].astype(BF16), proj_ssd=proj_ssd[l].astype(BF16),
                  proj_fox=proj_fox[l].astype(BF16), w_out=w_out[l].astype(BF16), g_m=g_m,
                  norm_gain=row(norm_ffn[l]), sc_f=sc_f, sh_f=sh_f)

        if l % 2 == 0:
            xs, hf = merge(xs, yn, bonus, g_rw, y_ssd, y_fox, gate_logits, mp)
            next_norm = None
            if l + 1 < depth:
                next_norm = (row(norm_mix[l + 1]), mods[l + 1][1], mods[l + 1][0])
            xs, h = ffn_dense(hf, xs, ffn_w_gu[l // 2].astype(BF16), ffn_w_down[l // 2].astype(BF16), g_f,
                              next_norm)
        else:
            xs, hf, comb = merge(xs, yn, bonus, g_rw, y_ssd, y_fox, gate_logits, mp,
                                 router=_pad_cols(moe_router[l // 2], LANES))
            xs = ffn_moe(hf, xs, comb, moe_w_gu[l // 2].astype(BF16), moe_w_down[l // 2].astype(BF16), g_f)
            h = None
    return xs[None]
```

```python
import functools
import math

import jax
import jax.numpy as jnp
from jax import lax
from jax.experimental import pallas as pl
from jax.experimental.pallas import tpu as pltpu

F32 = jnp.float32
BF16 = jnp.bfloat16
HIGHEST = lax.Precision.HIGHEST

D_MODEL = 1024
LANES = 128
SUBLANES = 8

RW_HEADS = 8
RW_HEAD_DIM = 64
RW_WIDTH = RW_HEADS * RW_HEAD_DIM
RW_DECAY_LORA = 64
RW_AAA_LORA = 64
RW_VRES_LORA = 32
RW_GATE_LORA = 160
GN_EPS = 64e-5
RW_OFF_WLO = 3 * RW_WIDTH
RW_OFF_ALO = RW_OFF_WLO + LANES
RW_OFF_GLO = RW_OFF_ALO + LANES
RW_OFF_HV = RW_OFF_GLO + 2 * LANES
RW_PAD_COLS = RW_OFF_HV + LANES
RW_CHUNK = 64

SSD_HEADS = 16
SSD_HEAD_DIM = 64
SSD_WIDTH = SSD_HEADS * SSD_HEAD_DIM
SSD_GROUPS = 4
SSD_STATE = 128
SSD_CONV = 4
SSD_CHUNK = 128
SSD_XBC = SSD_WIDTH + 2 * SSD_GROUPS * SSD_STATE
SSD_GROUP_WIDTH = SSD_WIDTH // SSD_GROUPS

FOX_HEADS = 8
FOX_HEAD_DIM = 64
FOX_WIDTH = FOX_HEADS * FOX_HEAD_DIM
FOX_TQ = 256
FOX_TK = 512
LOG2E = 1.4426950408889634

FFN_DENSE = 2816
N_EXPERTS = 8
TOP_K = 2
FFN_EXPERT = 3584
MOE_SB = 128
MOE_CB = 512
EPS = 1e-6

VMEM_LIMIT = 56 * 1024 * 1024


def _cparams(*sem):
    return pltpu.CompilerParams(dimension_semantics=sem, vmem_limit_bytes=VMEM_LIMIT)


def _sigmoid(x):
    return 1.0 / (1.0 + jnp.exp(-x))


def _softplus(x):
    return jnp.maximum(x, 0.0) + jnp.log(1.0 + jnp.exp(-jnp.abs(x)))


def _dot(a, b):
    return jnp.dot(a, b, preferred_element_type=F32)


def _dot_bf16(a, b):
    return jnp.dot(a.astype(BF16), b.astype(BF16), preferred_element_type=F32)


def _dot_hi(a, b):
    return jnp.dot(a, b, precision=HIGHEST, preferred_element_type=F32)


def _dot_nt(a, b, precision=None):
    return lax.dot_general(a, b, (((1,), (1,)), ((), ())), precision=precision, preferred_element_type=F32)


def _dot_tn(a, b, precision=None):
    return lax.dot_general(a, b, (((0,), (0,)), ((), ())), precision=precision, preferred_element_type=F32)


def _split_dot(x, ones_bf16):
    hi = x.astype(BF16)
    lo = (x - hi.astype(F32)).astype(BF16)
    return _dot(hi, ones_bf16) + _dot(lo, ones_bf16)


def _split3(x):
    hi = x.astype(BF16)
    r1 = x - hi.astype(F32)
    mid = r1.astype(BF16)
    return hi, mid, (r1 - mid.astype(F32)).astype(BF16)


def _tril(n, strict=False):
    r = lax.broadcasted_iota(jnp.int32, (n, n), 0)
    c = lax.broadcasted_iota(jnp.int32, (n, n), 1)
    return (r > c) if strict else (r >= c)


def _mod_kernel(c_ref, w_ref, b_ref, o_ref):
    c = c_ref[...]
    o_ref[...] = _dot_hi(c * _sigmoid(c), w_ref[...]) + b_ref[...]


def adaln_mod(c, w, b):
    d, n = w.shape
    tn = 1024
    c8 = jnp.broadcast_to(c, (SUBLANES, d))
    out = pl.pallas_call(
        _mod_kernel,
        grid=(n // tn,),
        in_specs=[pl.BlockSpec((SUBLANES, d), lambda j: (0, 0)),
                  pl.BlockSpec((d, tn), lambda j: (0, j)),
                  pl.BlockSpec((1, tn), lambda j: (0, j))],
        out_specs=pl.BlockSpec((SUBLANES, tn), lambda j: (0, j)),
        out_shape=jax.ShapeDtypeStruct((SUBLANES, n), F32),
        compiler_params=_cparams("arbitrary"),
    )(c8, w, b.reshape(1, n))
    return out[:1]


def _norm_mod(x, gain, sc, sh):
    y = x * lax.rsqrt(jnp.mean(x * x, axis=-1, keepdims=True) + EPS)
    return y * gain * (1.0 + sc) + sh


def _norm_kernel(x_ref, gain_ref, sc_ref, sh_ref, h_ref):
    h_ref[...] = _norm_mod(x_ref[...], gain_ref[...], sc_ref[...], sh_ref[...]).astype(h_ref.dtype)


def norm_mod(x, gain, sc, sh, tm=512):
    L, d = x.shape
    row = pl.BlockSpec((1, d), lambda i: (0, 0))
    return pl.pallas_call(
        _norm_kernel,
        grid=(L // tm,),
        in_specs=[pl.BlockSpec((tm, d), lambda i: (i, 0)), row, row, row],
        out_specs=pl.BlockSpec((tm, d), lambda i: (i, 0)),
        out_shape=jax.ShapeDtypeStruct((L, d), BF16),
        compiler_params=_cparams("arbitrary"),
    )(x, gain, sc, sh)


def _mm_kernel(x_ref, w_ref, o_ref):
    o_ref[...] = _dot(x_ref[...], w_ref[...]).astype(o_ref.dtype)


def matmul(x, w, out_dtype=F32, tm=512):
    M, K = x.shape
    N = w.shape[1]
    return pl.pallas_call(
        _mm_kernel,
        grid=(M // tm,),
        in_specs=[pl.BlockSpec((tm, K), lambda i: (i, 0)),
                  pl.BlockSpec((K, N), lambda i: (0, 0))],
        out_specs=pl.BlockSpec((tm, N), lambda i: (i, 0)),
        out_shape=jax.ShapeDtypeStruct((M, N), out_dtype),
        compiler_params=_cparams("arbitrary"),
    )(x, w)


def _rw_pre_kernel(has_vres, *refs):
    if has_vres:
        (h_ref, w_ref, mu_ref, w0_ref, wup_ref, a0_ref, aup_ref, gup_ref, kk_ref, ka_ref, rk_ref, seg_ref,
         vfirst_ref, v0_ref, vup_ref,
         r_out, k_out, v_out, lw_out, a_out, b_out, g_out, bonus_out, ext_scr) = refs
    else:
        (h_ref, w_ref, mu_ref, w0_ref, wup_ref, a0_ref, aup_ref, gup_ref, kk_ref, ka_ref, rk_ref, seg_ref,
         r_out, k_out, v_out, lw_out, a_out, b_out, g_out, bonus_out, ext_scr) = refs
    T = h_ref.shape[0]

    @pl.when(pl.program_id(0) == 0)
    def _():
        ext_scr[0:SUBLANES, :] = jnp.zeros((SUBLANES, ext_scr.shape[1]), F32)

    cur = _dot(h_ref[...], w_ref[...])
    ext_scr[SUBLANES:SUBLANES + T, :] = cur
    prev = ext_scr[SUBLANES - 1:SUBLANES - 1 + T, :]
    ext_scr[0:SUBLANES, :] = cur[T - SUBLANES:T, :]
    s = cur + (prev - cur) * mu_ref[...]

    W = RW_WIDTH
    r = s[:, 0:W]
    k = s[:, W:2 * W]
    v = s[:, 2 * W:3 * W]
    w_lo = s[:, RW_OFF_WLO:RW_OFF_WLO + LANES]
    a_lo = s[:, RW_OFF_ALO:RW_OFF_ALO + LANES]
    g_lo = s[:, RW_OFF_GLO:RW_OFF_GLO + 2 * LANES]
    seg = seg_ref[...]

    wlog = -_softplus(-(w0_ref[...] + _dot_bf16(jnp.tanh(w_lo), wup_ref[...]))) - 0.5
    a = _sigmoid(a0_ref[...] + _dot_bf16(a_lo, aup_ref[...]))
    g = _dot_bf16(_sigmoid(g_lo), gup_ref[...])
    if has_vres:
        hv = s[:, RW_OFF_HV:RW_OFF_HV + LANES]
        v = v + (vfirst_ref[...].astype(F32) - v) * _sigmoid(v0_ref[...] + _dot_bf16(hv, vup_ref[...]))
    kk = k * kk_ref[...]
    kk = kk / jnp.maximum(jnp.sqrt(_split_dot(kk * kk, seg)), 1e-12)
    k = k * (1.0 + (a - 1.0) * ka_ref[...])
    r_out[...] = r.astype(r_out.dtype)
    k_out[...] = k.astype(k_out.dtype)
    v_out[...] = v.astype(v_out.dtype)
    lw_out[...] = -jnp.exp(wlog)
    a_out[...] = (-kk).astype(a_out.dtype)
    b_out[...] = (kk * a).astype(b_out.dtype)
    g_out[...] = g.astype(g_out.dtype)
    bonus_out[...] = (_split_dot(r * k * rk_ref[...], seg) * v).astype(bonus_out.dtype)


def rw_pre(h, w_rw, p, v_first, tm=512):
    L, D = h.shape
    has_vres = v_first is not None
    full = lambda a: pl.BlockSpec(a.shape, lambda i: (0,) * a.ndim)
    rows = lambda n: pl.BlockSpec((tm, n), lambda i: (i, 0))
    args = [h, w_rw, p["mu"], p["w0"], p["w_up"], p["a0"], p["a_up"], p["g_up"], p["k_k"], p["k_a"], p["r_k"],
            p["seg"]]
    specs = [rows(D)] + [full(a) for a in args[1:]]
    if has_vres:
        args += [v_first, p["v0"], p["v_up"]]
        specs += [rows(RW_WIDTH), full(p["v0"]), full(p["v_up"])]
    out_dtypes = [BF16, BF16, BF16, F32, BF16, BF16, BF16, BF16]
    return pl.pallas_call(
        functools.partial(_rw_pre_kernel, has_vres),
        grid=(L // tm,),
        in_specs=specs,
        out_specs=[rows(RW_WIDTH)] * 8,
        out_shape=[jax.ShapeDtypeStruct((L, RW_WIDTH), dt) for dt in out_dtypes],
        scratch_shapes=[pltpu.VMEM((tm + SUBLANES, RW_PAD_COLS), F32)],
        compiler_params=_cparams("arbitrary"),
    )(*args)


def _wkv_kernel(r_ref, lw_ref, k_ref, v_ref, a_ref, b_ref, y_ref, h_scr):
    C = RW_CHUNK
    C2 = 2 * C
    n_chunks = r_ref.shape[0] // C

    @pl.when(pl.program_id(1) == 0)
    def _():
        h_scr[...] = jnp.zeros_like(h_scr)

    is_a = lax.broadcasted_iota(jnp.int32, (C, LANES), 1) < RW_HEAD_DIM
    ri = lax.broadcasted_iota(jnp.int32, (C2, C2), 0)
    ci = lax.broadcasted_iota(jnp.int32, (C2, C2), 1)
    same = (ri >= C) == (ci >= C)
    strict = jnp.logical_and(same, ri > ci)
    incl = jnp.logical_and(same, ri >= ci)
    eye2 = (ri == ci).astype(F32)
    tri_c = _tril(C).astype(BF16)

    def two(x):
        return jnp.concatenate([jnp.where(is_a, x, 0.0), jnp.where(is_a, 0.0, x)], axis=0)

    chunks = range(n_chunks)
    at2, rt2, bh2, kh2, v2, n_ab, a_ak, m_rb, m_rk, p_end = ([] for _ in range(10))
    for c in chunks:
        sl = pl.ds(c * C, C)
        lw = lw_ref[sl, :]
        lw_hi = lw.astype(BF16)
        lw_lo = (lw - lw_hi.astype(F32)).astype(BF16)
        cs = _dot(tri_c, lw_hi) + _dot(tri_c, lw_lo)
        cs_end = cs[C - 1:C, :]
        e_neg = jnp.exp(-cs)
        e_end = jnp.exp(cs_end - cs)
        a = a_ref[sl, :].astype(F32)
        b = b_ref[sl, :].astype(F32)
        k = k_ref[sl, :].astype(F32)
        at2.append(two(a * jnp.exp(cs - lw)))
        rt2.append(two(r_ref[sl, :].astype(F32) * jnp.exp(cs)))
        bh2.append(two(b * e_end))
        kh2.append(two(k * e_end))
        v2.append(two(v_ref[sl, :].astype(F32)))
        p_end.append(jnp.exp(cs_end))
        quad = _dot_nt(jnp.concatenate([at2[c], rt2[c]], axis=0).astype(BF16),
                       jnp.concatenate([two(b * e_neg), two(k * e_neg)], axis=0).astype(BF16))
        n_ab.append(jnp.where(strict, quad[0:C2, 0:C2], 0.0))
        a_ak.append(jnp.where(strict, quad[0:C2, C2:], 0.0))
        m_rb.append(jnp.where(incl, quad[C2:, 0:C2], 0.0))
        m_rk.append(jnp.where(incl, quad[C2:, C2:], 0.0))
    t_inv = [eye2 + n for n in n_ab]
    n_pow = n_ab
    n_pow = [_dot_bf16(n, n) for n in n_pow]
    for _ in range(int(math.log2(C)) - 2):
        both = [_dot_bf16(jnp.concatenate([n, t], axis=0), n) for n, t in zip(n_pow, t_inv)]
        t_inv = [t + b[C2:, :] for t, b in zip(t_inv, both)]
        n_pow = [b[:C2, :] for b in both]
    t_inv = [t + _dot_bf16(t, n) for t, n in zip(t_inv, n_pow)]
    akv = [_dot_bf16(a_ak[c], v2[c]) for c in chunks]
    au = [_dot_bf16(t_inv[c], jnp.concatenate([at2[c], akv[c]], axis=1)) for c in chunks]
    mau = [_dot_bf16(m_rb[c], au[c]) for c in chunks]
    mkv = [_dot_bf16(m_rk[c], v2[c]) for c in chunks]
    gu = [_dot_tn(bh2[c].astype(BF16), au[c].astype(BF16)) for c in chunks]
    kv = [_dot_tn(kh2[c].astype(BF16), v2[c].astype(BF16)) for c in chunks]
    inv_n = 1.0 / RW_HEAD_DIM
    h = h_scr[...]
    for c in chunks:
        rh2 = rt2[c] + mau[c][:, :LANES]
        g_mat = eye2 * p_end[c] + gu[c][:, :LANES]
        yh = _dot_bf16(jnp.concatenate([rh2, g_mat], axis=0), h)
        h = yh[C2:, :] + gu[c][:, LANES:] + kv[c]
        y2 = yh[:C2, :] + mau[c][:, LANES:] + mkv[c]
        y = y2[:C, :] + y2[C:, :]
        s_all = jnp.sum(y, axis=-1, keepdims=True)
        s_a = jnp.sum(jnp.where(is_a, y, 0.0), axis=-1, keepdims=True)
        yc = y - jnp.where(is_a, s_a, s_all - s_a) * inv_n
        sq = yc * yc
        q_all = jnp.sum(sq, axis=-1, keepdims=True)
        q_a = jnp.sum(jnp.where(is_a, sq, 0.0), axis=-1, keepdims=True)
        yn = yc * lax.rsqrt(jnp.where(is_a, q_a, q_all - q_a) * inv_n + GN_EPS)
        y_ref[pl.ds(c * C, C), :] = yn.astype(y_ref.dtype)
    h_scr[...] = h


def wkv7(r, lw, k, v, a, b, rows=512):
    L, W = r.shape
    spec = pl.BlockSpec((rows, LANES), lambda p, i: (i, p))
    return pl.pallas_call(
        _wkv_kernel,
        grid=(W // LANES, L // rows),
        in_specs=[spec] * 6,
        out_specs=spec,
        out_shape=jax.ShapeDtypeStruct((L, W), BF16),
        scratch_shapes=[pltpu.VMEM((LANES, LANES), F32)],
        compiler_params=_cparams("arbitrary", "arbitrary"),
    )(r, lw, k, v, a, b)


def _ssd_kernel(h_ref, w_ref, cw_ref, cb_ref, bias_ref, A_ref, exp_ref, dskip_ref, nw_ref, o_ref,
                ext_scr, xbc_scr, s_scr):
    Q = SSD_CHUNK
    GW = SSD_GROUP_WIDTH
    NS = SSD_STATE
    R = h_ref.shape[0]

    @pl.when(pl.program_id(0) == 0)
    def _():
        s_scr[...] = jnp.zeros_like(s_scr)
        ext_scr[0:SUBLANES, :] = jnp.zeros((SUBLANES, ext_scr.shape[1]), F32)

    cols = _dot(h_ref[...], w_ref[...])
    ext_scr[SUBLANES:SUBLANES + R, :] = cols[:, 0:SSD_XBC]
    acc = cols[:, 0:SSD_XBC] * cw_ref[SSD_CONV - 1:SSD_CONV, :] + cb_ref[...]
    for j in range(1, SSD_CONV):
        acc = acc + ext_scr[SUBLANES - j:SUBLANES - j + R, :] * cw_ref[SSD_CONV - 1 - j:SSD_CONV - j, :]
    ext_scr[0:SUBLANES, :] = ext_scr[R:R + SUBLANES, :]
    xbc_scr[...] = acc * _sigmoid(acc)

    incl = _tril(Q)
    tri = incl.astype(BF16)
    triu = jnp.logical_not(_tril(Q, strict=True)).astype(BF16)
    expand = exp_ref[...]
    first_half = lax.broadcasted_iota(jnp.int32, (Q, LANES), 1) < SSD_HEAD_DIM
    heads_per_group = SSD_HEADS // SSD_GROUPS
    for c in range(R // Q):
        rows = slice(c * Q, (c + 1) * Q)
        xs = xbc_scr[rows, 0:SSD_WIDTH]
        z = cols[rows, SSD_XBC:SSD_XBC + SSD_WIDTH]
        dt = _softplus(cols[rows, SSD_XBC + SSD_WIDTH:] + bias_ref[...])
        a3 = _split3(dt * A_ref[...])
        a_cum = sum(_dot(tri, t) for t in a3)
        a_cumT = sum(_dot_tn(t, triu) for t in a3)
        a_end = a_cum[Q - 1:Q, :]
        dt_x = _dot(dt.astype(BF16), expand)
        eac_x = _dot(jnp.exp(a_cum).astype(BF16), expand)
        dte_x = _dot(jnp.exp(a_end - a_cum).astype(BF16), expand)
        cd_x = sum(_dot(t, expand) for t in _split3(jnp.broadcast_to(jnp.exp(a_end), (SUBLANES, LANES))))[0:1, :]
        xdt = xs * dt_x
        for g in range(SSD_GROUPS):
            Bg = xbc_scr[rows, SSD_WIDTH + g * NS:SSD_WIDTH + (g + 1) * NS].astype(BF16)
            Cg = xbc_scr[rows, SSD_WIDTH + (SSD_GROUPS + g) * NS:SSD_WIDTH + (SSD_GROUPS + g + 1) * NS].astype(BF16)
            cb = _dot_nt(Cg, Bg)
            gs = slice(g * GW, (g + 1) * GW)
            s_prev = s_scr[:, gs]
            y_g = _dot_bf16(Cg, s_prev) * eac_x[:, gs]
            pieces = []
            for pr in range(heads_per_group // 2):
                ps = slice(g * GW + pr * LANES, g * GW + (pr + 1) * LANES)
                xdt_p = xdt[:, ps]
                acc_p = None
                for e in range(2):
                    hd = g * heads_per_group + pr * 2 + e
                    seg = a_cum[:, hd:hd + 1] - a_cumT[hd:hd + 1, :]
                    m = cb * jnp.exp(jnp.where(incl, seg, -jnp.inf))
                    xm = jnp.where(first_half if e == 0 else jnp.logical_not(first_half), xdt_p, 0.0)
                    t = _dot_bf16(m, xm)
                    acc_p = t if acc_p is None else acc_p + t
                pieces.append(acc_p)
            y_g = y_g + jnp.concatenate(pieces, axis=1)
            s_scr[:, gs] = s_prev * cd_x[:, gs] +_dot_tn(Bg, (dte_x[:, gs] * xdt[:, gs]).astype(BF16))
            y_g = y_g + dskip_ref[:, gs] * xs[:, gs]
            zg = z[:, gs]
            y_g = y_g * (zg * _sigmoid(zg))
            y_g = y_g * lax.rsqrt(jnp.mean(y_g * y_g, axis=-1, keepdims=True) + EPS)
            o_ref[rows, gs] = (y_g * nw_ref[:, gs]).astype(o_ref.dtype)


def ssd_mixer(h, w_ssd, p, rows=512):
    L, D = h.shape
    N = w_ssd.shape[1]
    full = lambda a: pl.BlockSpec(a.shape, lambda i: (0,) * a.ndim)
    small = [p["conv_w"], p["conv_b"], p["dt_bias"], p["A"], p["expand"], p["d_skip"], p["norm_w"]]
    return pl.pallas_call(
        _ssd_kernel,
        grid=(L // rows,),
        in_specs=[pl.BlockSpec((rows, D), lambda i: (i, 0)), full(w_ssd)] + [full(a) for a in small],
        out_specs=pl.BlockSpec((rows, SSD_WIDTH), lambda i: (i, 0)),
        out_shape=jax.ShapeDtypeStruct((L, SSD_WIDTH), BF16),
        scratch_shapes=[pltpu.VMEM((rows + SUBLANES, SSD_XBC), F32), pltpu.VMEM((rows, SSD_XBC), F32),
                        pltpu.VMEM((SSD_STATE, SSD_WIDTH), F32)],
        compiler_params=_cparams("arbitrary"),
    )(h, w_ssd, *small)


def _fox_pre_kernel(h_ref, w_ref, fb_ref, qg_ref, kg_ref, seg_ref, q_out, k_out, v_out, f_out, carry_scr):
    T = h_ref.shape[0]
    W = FOX_WIDTH

    @pl.when(pl.program_id(0) == 0)
    def _():
        carry_scr[...] = jnp.zeros_like(carry_scr)

    cols = _dot(h_ref[...], w_ref[...])
    seg = seg_ref[...]
    q = cols[:, 0:W]
    k = cols[:, W:2 * W]
    inv_d = 1.0 / FOX_HEAD_DIM
    qn = q * lax.rsqrt(_split_dot(q * q, seg) * inv_d + EPS) * qg_ref[...]
    kn = k * lax.rsqrt(_split_dot(k * k, seg) * inv_d + EPS) * kg_ref[...]
    q_out[...] = qn.astype(q_out.dtype)
    k_out[...] = kn.astype(k_out.dtype)
    v_out[...] = cols[:, 2 * W:3 * W].astype(v_out.dtype)
    f = cols[:, 3 * W:3 * W + LANES] + fb_ref[...]
    ls = -_softplus(-f)
    cum = _dot_hi(_tril(T).astype(F32), ls) + carry_scr[0:1, :]
    f_out[...] = cum * LOG2E
    carry_scr[...] = jnp.broadcast_to(cum[T - 1:T, :], carry_scr.shape)


def fox_pre(h, w_fox, p, tm=512):
    L, D = h.shape
    full = lambda a: pl.BlockSpec(a.shape, lambda i: (0,) * a.ndim)
    rows = lambda n: pl.BlockSpec((tm, n), lambda i: (i, 0))
    small = [w_fox, p["f_bias"], p["q_gain"], p["k_gain"], p["seg"]]
    return pl.pallas_call(
        _fox_pre_kernel,
        grid=(L // tm,),
        in_specs=[rows(D)] + [full(a) for a in small],
        out_specs=[rows(FOX_WIDTH)] * 3 + [rows(LANES)],
        out_shape=[jax.ShapeDtypeStruct((L, FOX_WIDTH), BF16)] * 3 + [jax.ShapeDtypeStruct((L, LANES), F32)],
        scratch_shapes=[pltpu.VMEM((SUBLANES, LANES), F32)],
        compiler_params=_cparams("arbitrary"),
    )(h, *small)


def _fox_kernel(lo_ref, q_ref, k_ref, v_ref, fk_ref, o_ref, m_scr, acc_scr, s0_scr, s1_scr):
    tq = q_ref.shape[0]
    tk = FOX_TK
    pair = pl.program_id(0)
    qi = pl.program_id(1)
    is_a = lax.broadcasted_iota(jnp.int32, (tq, LANES), 1) < FOX_HEAD_DIM
    q = q_ref[...]
    zero = jnp.zeros_like(q)
    qs = (jnp.where(is_a, q, zero), jnp.where(is_a, zero, q))
    m_scr[...] = jnp.full(m_scr.shape, -jnp.inf, F32)
    key_is_a = lax.broadcasted_iota(jnp.int32, (tk, LANES), 1) < FOX_HEAD_DIM
    acc_scr[...] = jnp.zeros_like(acc_scr)
    rel = lax.broadcasted_iota(jnp.int32, (tq, tk), 0) - lax.broadcasted_iota(jnp.int32, (tq, tk), 1)
    last = (qi * tq) // tk

    def fill(s_ref, kb, diagonal=False):
        ks = pl.ds(pl.multiple_of(kb * tk, tk), tk)
        k_blk = k_ref[ks, :]
        for e in range(2):
            s = _dot_nt(qs[e], k_blk) - fk_ref[pl.ds(2 * pair + e, 1), ks]
            if diagonal:
                s = jnp.where(rel >= kb * tk - qi * tq, s, -jnp.inf)
            s_ref[e] = s

    def consume(s_ref, kb):
        v_blk = v_ref[pl.ds(pl.multiple_of(kb * tk, tk), tk), :]
        one = jnp.ones_like(v_blk)
        v_one = (jnp.where(key_is_a, v_blk, one), jnp.where(key_is_a, one, v_blk))
        for e in range(2):
            s = s_ref[e]
            m_old = m_scr[e]
            m_new = jnp.maximum(m_old, jnp.max(s, axis=-1, keepdims=True))
            pexp = jnp.exp2(s - m_new)
            acc_scr[e] = jnp.exp2(m_old - m_new) * acc_scr[e] + _dot(pexp.astype(BF16), v_one[e])
            m_scr[e] = m_new

    n = last - jnp.minimum(lo_ref[2 * pair, qi], lo_ref[2 * pair + 1, qi])
    fill(s0_scr, last, diagonal=True)

    def two_steps(u, carry):
        kb = last - 2 * u
        fill(s1_scr, kb - 1)
        consume(s0_scr, kb)
        fill(s0_scr, kb - 2)
        consume(s1_scr, kb - 1)
        return carry

    lax.fori_loop(0, n // 2, two_steps, 0)

    @pl.when(n % 2 == 1)
    def _():
        fill(s1_scr, last - n)
        consume(s0_scr, last - n + 1)
        consume(s1_scr, last - n)

    @pl.when(n % 2 == 0)
    def _():
        consume(s0_scr, last - n)

    acc_a = acc_scr[0]
    acc_b = acc_scr[1]
    half = FOX_HEAD_DIM
    o_ref[...] = jnp.where(is_a, acc_a / pltpu.roll(acc_a, half, 1),
                           acc_b / pltpu.roll(acc_b, half, 1)).astype(o_ref.dtype)


def fox_attention(q, k, v, FT, lo):
    L, W = q.shape
    H = FT.shape[0]
    tq = FOX_TQ
    grid_spec = pltpu.PrefetchScalarGridSpec(
        num_scalar_prefetch=1,
        grid=(W // LANES, L // tq),
        in_specs=[pl.BlockSpec((tq, LANES), lambda p, i, lo_r: (i, p)),
                  pl.BlockSpec((L, LANES), lambda p, i, lo_r: (0, p)),
                  pl.BlockSpec((L, LANES), lambda p, i, lo_r: (0, p)),
                  pl.BlockSpec((H, L), lambda p, i, lo_r: (0, 0))],
        out_specs=pl.BlockSpec((tq, LANES), lambda p, i, lo_r: (i, p)),
        scratch_shapes=[pltpu.VMEM((2, tq, 1), F32), pltpu.VMEM((2, tq, LANES), F32),
                        pltpu.VMEM((2, tq, FOX_TK), F32), pltpu.VMEM((2, tq, FOX_TK), F32)],
    )
    return pl.pallas_call(
        _fox_kernel,
        grid_spec=grid_spec,
        out_shape=jax.ShapeDtypeStruct((L, W), BF16),
        compiler_params=_cparams("arbitrary", "arbitrary"),
    )(lo, q, k, v, FT)


def fox_first_block(FT, bound2):
    f_first = FT[:, ::FOX_TQ]
    f_last = FT[:, FOX_TK - 1::FOX_TK]
    gap = f_first[:, :, None] - f_last[:, None, :] + 2.0 * bound2
    last = (jnp.arange(f_first.shape[1]) * FOX_TQ) // FOX_TK
    needed = jnp.logical_or(gap >= -152.0, jnp.arange(f_last.shape[1])[None, None, :] >= last[None, :, None])
    return jnp.argmax(needed, axis=-1).astype(jnp.int32)


def _top2_combine(h, wr):
    T = h.shape[0]
    lane = lax.broadcasted_iota(jnp.int32, (T, LANES), 1)
    h_hi = h.astype(BF16)
    h_lo = (h - h_hi.astype(F32)).astype(BF16)
    w_hi = wr.astype(BF16)
    w_lo = (wr - w_hi.astype(F32)).astype(BF16)
    logits = _dot(h_hi, w_hi) + (_dot(h_hi, w_lo) + _dot(h_lo, w_hi))
    logits = jnp.where(lane < N_EXPERTS, logits, -jnp.inf)
    m1 = jnp.max(logits, axis=-1, keepdims=True)
    i1 = jnp.min(jnp.where(logits == m1, lane, LANES), axis=-1, keepdims=True)
    rest = jnp.where(lane == i1, -jnp.inf, logits)
    m2 = jnp.max(rest, axis=-1, keepdims=True)
    i2 = jnp.min(jnp.where(rest == m2, lane, LANES), axis=-1, keepdims=True)
    e2 = jnp.exp(m2 - m1)
    return jnp.where(lane == i1, 1.0 / (1.0 + e2), 0.0) + jnp.where(lane == i2, e2 / (1.0 + e2), 0.0)


def _merge_kernel(with_router, *refs):
    (x_ref, yn_ref, bonus_ref, g_ref, yssd_ref, yfox_ref, gl_ref, gb_ref, lnw_ref, lnb_ref,
     prw_ref, pssd_ref, pfox_ref, wout_ref, gm_ref, gain_ref, sc_ref, sh_ref) = refs[:18]
    D = D_MODEL
    y_rw = (yn_ref[...] * lnw_ref[...] + lnb_ref[...] + bonus_ref[...]) * g_ref[...]
    gates = _sigmoid(gl_ref[...] + gb_ref[...])
    merged = gates[:, 0:D] * _dot_bf16(y_rw, prw_ref[...])
    merged = merged + gates[:, D:2 * D] * _dot_bf16(yssd_ref[...], pssd_ref[...])
    merged = merged + gates[:, 2 * D:3 * D] * _dot_bf16(yfox_ref[...], pfox_ref[...])
    x_new = x_ref[...] + gm_ref[...] * _dot_bf16(merged, wout_ref[...])
    hf = _norm_mod(x_new, gain_ref[...], sc_ref[...], sh_ref[...])
    if with_router:
        wr_ref, o_ref, h_ref, comb_ref = refs[18:]
        comb_ref[...] = _top2_combine(hf, wr_ref[...])
    else:
        o_ref, h_ref = refs[18:]
    o_ref[...] = x_new
    h_ref[...] = hf.astype(h_ref.dtype)


def merge(x, yn, bonus, g, y_ssd, y_fox, gate_logits, p, router=None, tm=512):
    L, D = x.shape
    full = lambda a: pl.BlockSpec(a.shape, lambda i: (0,) * a.ndim)
    rows = lambda n: pl.BlockSpec((tm, n), lambda i: (i, 0))
    small = [p["gate_b"], p["lnx_w"], p["lnx_b"], p["proj_rw"], p["proj_ssd"], p["proj_fox"], p["w_out"], p["g_m"],
             p["norm_gain"], p["sc_f"], p["sh_f"]]
    out_specs = [rows(D), rows(D)]
    out_shape = [jax.ShapeDtypeStruct((L, D), F32), jax.ShapeDtypeStruct((L, D), BF16)]
    if router is not None:
        small.append(router)
        out_specs.append(rows(LANES))
        out_shape.append(jax.ShapeDtypeStruct((L, LANES), F32))
    return pl.pallas_call(
        functools.partial(_merge_kernel, router is not None),
        grid=(L // tm,),
        in_specs=[rows(D), rows(RW_WIDTH), rows(RW_WIDTH), rows(RW_WIDTH), rows(SSD_WIDTH), rows(FOX_WIDTH),
                  rows(3 * D)] + [full(a) for a in small],
        out_specs=out_specs,
        out_shape=out_shape,
        compiler_params=_cparams("arbitrary"),
    )(x, yn, bonus, g, y_ssd, y_fox, gate_logits, *small)


def _ffn_kernel(with_next, *refs):
    h_ref, x_ref, wg_ref, wu_ref, wd_ref, gf_ref = refs[:6]
    j = pl.program_id(1)
    acc_scr = refs[-1]

    @pl.when(j == 0)
    def _():
        acc_scr[...] = jnp.zeros_like(acc_scr)

    h = h_ref[...]
    gte = _dot(h, wg_ref[...])
    up = _dot(h, wu_ref[...])
    act = gte * _sigmoid(gte) * up
    acc_scr[...] += _dot(act.astype(BF16), wd_ref[...])

    @pl.when(j == pl.num_programs(1) - 1)
    def _():
        x_new = x_ref[...] + gf_ref[...] * acc_scr[...]
        if with_next:
            gain_ref, sc_ref, sh_ref, o_ref, hn_ref = refs[6:11]
            hn_ref[...] = _norm_mod(x_new, gain_ref[...], sc_ref[...], sh_ref[...]).astype(hn_ref.dtype)
        else:
            o_ref = refs[6]
        o_ref[...] = x_new


def ffn_dense(h, x, w_gu, w_down, g_f, next_norm=None, tm=1024, tf=256):
    L, D = x.shape
    Fh = w_down.shape[0]
    nf = Fh // tf
    row = pl.BlockSpec((1, D), lambda i, j: (0, 0))
    tile = pl.BlockSpec((tm, D), lambda i, j: (i, 0))
    args = [h, x, w_gu, w_gu, w_down, g_f]
    in_specs = [tile, tile,
                pl.BlockSpec((D, tf), lambda i, j: (0, j)),
                pl.BlockSpec((D, tf), lambda i, j: (0, j + nf)),
                pl.BlockSpec((tf, D), lambda i, j: (j, 0)), row]
    out_specs = [tile]
    out_shape = [jax.ShapeDtypeStruct((L, D), F32)]
    if next_norm is not None:
        args += list(next_norm)
        in_specs += [row, row, row]
        out_specs.append(tile)
        out_shape.append(jax.ShapeDtypeStruct((L, D), BF16))
    out = pl.pallas_call(
        functools.partial(_ffn_kernel, next_norm is not None),
        grid=(L // tm, nf),
        in_specs=in_specs,
        out_specs=out_specs,
        out_shape=out_shape,
        scratch_shapes=[pltpu.VMEM((tm, D), F32)],
        compiler_params=_cparams("arbitrary", "arbitrary"),
    )(*args)
    return out if next_norm is not None else (out[0], None)


def _moe_slots(slot_scr, comb_ref):
    slotval = slot_scr[...]
    comb = comb_ref[...]
    s1 = jnp.max(slotval, axis=-1, keepdims=True)
    first = slotval == s1
    c1 = jnp.sum(jnp.where(first, comb, 0.0), axis=-1, keepdims=True)
    rest = jnp.where(first, -1.0, slotval)
    s2 = jnp.max(rest, axis=-1, keepdims=True)
    c2 = jnp.sum(jnp.where(jnp.logical_and(rest == s2, rest >= 0.0), comb, 0.0), axis=-1, keepdims=True)
    return s1, s2, c1, c2


def _moe_kernel(nsb_ref, off_ref, h_ref, x_ref, comb_ref, wg_ref, wu_ref, wd_ref, gf_ref, o_ref,
                slot_scr, xy_scr, acc_scr):
    i = pl.program_id(0)
    e = pl.program_id(1)
    j = pl.program_id(2)
    last_j = pl.num_programs(2) - 1
    T = h_ref.shape[0]
    S = xy_scr.shape[0]
    SB = MOE_SB

    @pl.when(jnp.logical_and(e == 0, j == 0))
    def _():
        lane = lax.broadcasted_iota(jnp.int32, (LANES, LANES), 1)
        off = jnp.zeros((LANES, LANES), F32)
        for ex in range(N_EXPERTS):
            off = jnp.where(lane == ex, off_ref[i, ex].astype(F32), off)
        before = _tril(LANES, strict=True).astype(BF16)
        ones = jnp.ones((LANES, LANES), BF16)
        for rb in range(T // LANES):
            rows = slice(rb * LANES, (rb + 1) * LANES)
            sel = comb_ref[rows, :] > 0.0
            sel_b = jnp.where(sel, 1.0, 0.0).astype(BF16)
            slot_scr[rows, :] = jnp.where(sel, _dot(before, sel_b) + off, -1.0)
            off = off + _dot(ones, sel_b)
        s1, s2, _, _ = _moe_slots(slot_scr, comb_ref)
        for cb in range(S // MOE_CB):
            slot = (lax.broadcasted_iota(jnp.int32, (T, MOE_CB), 1) + cb * MOE_CB).astype(F32)
            pt = jnp.where(jnp.logical_or(s1 == slot, s2 == slot), 1.0, 0.0).astype(BF16)
            xy_scr[cb * MOE_CB:(cb + 1) * MOE_CB, :] = _dot_tn(pt, h_ref[...]).astype(BF16)

    base = off_ref[i, e]
    nb = nsb_ref[i, e]

    def ffn_block(r0, n_rows):
        rows = pl.ds(pl.multiple_of(base + r0, SB), n_rows)
        arows = pl.ds(pl.multiple_of(r0, SB), n_rows)
        xb = xy_scr[rows, :]
        gte = _dot(xb, wg_ref[0])
        up = _dot(xb, wu_ref[0])
        part = _dot((gte * _sigmoid(gte) * up).astype(BF16), wd_ref[0])

        @pl.when(j == 0)
        def _():
            acc_scr[arows, :] = part

        @pl.when(jnp.logical_and(j > 0, j < last_j))
        def _():
            acc_scr[arows, :] += part

        @pl.when(j == last_j)
        def _():
            xy_scr[rows, :] = (acc_scr[arows, :] + part).astype(BF16)

    def pair_body(p, carry):
        ffn_block(p * (2 * SB), 2 * SB)
        return carry

    lax.fori_loop(0, nb // 2, pair_body, 0)

    @pl.when(nb % 2 == 1)
    def _():
        ffn_block((nb - 1) * SB, SB)

    @pl.when(jnp.logical_and(e == pl.num_programs(1) - 1, j == last_j))
    def _():
        s1, s2, c1, c2 = _moe_slots(slot_scr, comb_ref)
        total = jnp.zeros((T, xy_scr.shape[1]), F32)
        for cb in range(S // MOE_CB):
            slot = (lax.broadcasted_iota(jnp.int32, (T, MOE_CB), 1) + cb * MOE_CB).astype(F32)
            w = jnp.where(s1 == slot, c1, 0.0) + jnp.where(s2 == slot, c2, 0.0)
            w_hi = w.astype(BF16)
            w_lo = (w - w_hi.astype(F32)).astype(BF16)
            y = xy_scr[cb * MOE_CB:(cb + 1) * MOE_CB, :]
            total = total + (_dot(w_hi, y) + _dot(w_lo, y))
        o_ref[...] = x_ref[...] + gf_ref[...] * total


def ffn_moe(h, x, comb, w_gu, w_down, g_f, tm=1024, tf=896):
    L, D = x.shape
    E, Fh, _ = w_down.shape
    nf = Fh // tf
    assert nf >= 2 and E == N_EXPERTS
    nt = L // tm
    counts = jnp.sum((comb[:, :E] > 0.0).reshape(nt, tm, E), axis=1, dtype=jnp.int32)
    nsb = (counts + (MOE_SB - 1)) // MOE_SB
    off = (jnp.cumsum(nsb, axis=1) - nsb) * MOE_SB
    idx = lambda f: (lambda i, e, j, n, o: f(i, e, j))
    grid_spec = pltpu.PrefetchScalarGridSpec(
        num_scalar_prefetch=2,
        grid=(nt, E, nf),
        in_specs=[pl.BlockSpec((tm, D), idx(lambda i, e, j: (i, 0))),
                  pl.BlockSpec((tm, D), idx(lambda i, e, j: (i, 0))),
                  pl.BlockSpec((tm, LANES), idx(lambda i, e, j: (i, 0))),
                  pl.BlockSpec((1, D, tf), idx(lambda i, e, j: (e, 0, j))),
                  pl.BlockSpec((1, D, tf), idx(lambda i, e, j: (e, 0, j + nf))),
                  pl.BlockSpec((1, tf, D), idx(lambda i, e, j: (e, j, 0))),
                  pl.BlockSpec((1, D), idx(lambda i, e, j: (0, 0)))],
        out_specs=pl.BlockSpec((tm, D), idx(lambda i, e, j: (i, 0))),
        scratch_shapes=[pltpu.VMEM((tm, LANES), F32),
                        pltpu.VMEM((TOP_K * tm + E * MOE_SB, D), BF16),
                        pltpu.VMEM((tm, D), F32)],
    )
    return pl.pallas_call(
        _moe_kernel,
        grid_spec=grid_spec,
        out_shape=jax.ShapeDtypeStruct((L, D), F32),
        compiler_params=_cparams("arbitrary", "arbitrary", "arbitrary"),
    )(nsb, off, h, x, comb, w_gu, w_gu, w_down, g_f)


def _seg_matrix(width, head_dim):
    idx = jnp.arange(width) // head_dim
    return (idx[:, None] == idx[None, :]).astype(BF16)


def _pad_cols(w, n):
    return jnp.pad(w, ((0, 0), (0, n - w.shape[1])))


def _pad_rows(w, n):
    return jnp.pad(w, ((0, n - w.shape[0]), (0, 0)))


def rwkv_branch(h, w_rw, p, v_first):
    r, k, v, lw, a, b, g, bonus = rw_pre(h, w_rw, p, v_first)
    return wkv7(r, lw, k, v, a, b), bonus, g, v


def fox_branch(h, w_fox, p):
    q, k, v, F = fox_pre(h, w_fox, p)
    FT = jnp.transpose(F[:, :FOX_HEADS])
    lo = fox_first_block(FT, p["bound2"])
    return fox_attention(q, k, v, FT, lo)


def kernel(x, c, ada_w, ada_b, norm_mix, norm_ffn, w_in, rw_mu, rw_w0, rw_w_up, rw_a0, rw_a_up, rw_g_up, rw_k_k, rw_k_a, rw_r_k, rw_lnx_w, rw_lnx_b, rw_v0, rw_v_down, rw_v_up, ssd_conv_w, ssd_conv_b, ssd_dt_bias, ssd_a_log, ssd_d, ssd_norm, fox_f_bias, fox_q_gain, fox_k_gain, gate_b, proj_rw, proj_ssd, proj_fox, w_out, ffn_w_gu, ffn_w_down, moe_router, moe_w_gu, moe_w_down):
    depth = w_in.shape[0]
    D = D_MODEL
    xs = x[0]
    row = lambda t: t.reshape(1, -1).astype(F32)
    seg64 = _seg_matrix(RW_WIDTH, RW_HEAD_DIM)
    rw_cols = 3 * RW_WIDTH + RW_DECAY_LORA + RW_AAA_LORA + RW_GATE_LORA
    ssd_cols = SSD_WIDTH + SSD_XBC + SSD_HEADS
    fox_cols = 3 * FOX_WIDTH + FOX_HEADS
    expand = (jnp.arange(LANES)[:, None] == (jnp.arange(SSD_WIDTH) // SSD_HEAD_DIM)[None, :]).astype(BF16)
    v_first = None
    mods = []
    for l in range(depth):
        mod = adaln_mod(c, ada_w[l], ada_b[l])
        mods.append([mod[:, i * D:(i + 1) * D] for i in range(6)])
    h = None
    for l in range(depth):
        sh_m, sc_m, g_m, sh_f, sc_f, g_f = mods[l]
        if h is None:
            h = norm_mod(xs, row(norm_mix[l]), sc_m, sh_m)

        wl = w_in[l]
        o = 0
        w_r = wl[:, o:o + rw_cols]; o += rw_cols
        w_s = wl[:, o:o + ssd_cols]; o += ssd_cols
        w_f = wl[:, o:o + fox_cols]; o += fox_cols
        w_g = wl[:, o:]
        W3 = 3 * RW_WIDTH
        o_a = W3 + RW_DECAY_LORA
        o_g = o_a + RW_AAA_LORA

        def rw_layout(t, hv):
            parts = [t[:, :W3], _pad_cols(t[:, W3:o_a], LANES), _pad_cols(t[:, o_a:o_g], LANES),
                     _pad_cols(t[:, o_g:], 2 * LANES), _pad_cols(hv, LANES)]
            return jnp.concatenate(parts, axis=1)

        if l == 0:
            hv_w = jnp.zeros((D, RW_VRES_LORA), F32)
        else:
            hv_w = rw_v_down[l - 1]
        w_rw = rw_layout(w_r, hv_w).astype(BF16)
        mu = rw_layout(rw_mu[l].reshape(1, -1), jnp.zeros((1, RW_VRES_LORA), F32))
        rw_p = dict(mu=mu, w0=row(rw_w0[l]), w_up=_pad_rows(rw_w_up[l], LANES).astype(BF16), a0=row(rw_a0[l]),
                    a_up=_pad_rows(rw_a_up[l], LANES).astype(BF16),
                    g_up=_pad_rows(rw_g_up[l], 2 * LANES).astype(BF16),
                    k_k=row(rw_k_k[l]), k_a=row(rw_k_a[l]), r_k=row(rw_r_k[l]), seg=seg64)
        if l > 0:
            rw_p.update(v0=row(rw_v0[l - 1]), v_up=_pad_rows(rw_v_up[l - 1], LANES).astype(BF16))
        yn, bonus, g_rw, v_cur = rwkv_branch(h, w_rw, rw_p, v_first)
        if l == 0:
            v_first = v_cur

        w_ssd = jnp.concatenate([w_s[:, SSD_WIDTH:SSD_WIDTH + SSD_XBC], w_s[:, :SSD_WIDTH],
                                 _pad_cols(w_s[:, SSD_WIDTH + SSD_XBC:], LANES)], axis=1).astype(BF16)
        A = -jnp.exp(ssd_a_log[l].astype(F32))
        ssd_p = dict(conv_w=ssd_conv_w[l], conv_b=row(ssd_conv_b[l]),
                     dt_bias=_pad_cols(row(ssd_dt_bias[l]), LANES),
                     A=_pad_cols(row(A), LANES), expand=expand,
                     d_skip=row(jnp.repeat(ssd_d[l], SSD_HEAD_DIM)), norm_w=row(ssd_norm[l]))
        y_ssd = ssd_mixer(h, w_ssd, ssd_p)

        w_fox = jnp.concatenate([w_f[:, :3 * FOX_WIDTH], _pad_cols(w_f[:, 3 * FOX_WIDTH:], LANES)],
                                axis=1).astype(BF16)
        bound2 = (1.02 * FOX_HEAD_DIM ** 0.5 * LOG2E) * jnp.max(jnp.abs(fox_q_gain[l])) * jnp.max(jnp.abs(fox_k_gain[l]))
        fox_p = dict(f_bias=_pad_cols(row(fox_f_bias[l]), LANES),
                     q_gain=row(jnp.tile(fox_q_gain[l], FOX_HEADS)) * (FOX_HEAD_DIM ** -0.5 * LOG2E),
                     k_gain=row(jnp.tile(fox_k_gain[l], FOX_HEADS)), seg=seg64, bound2=bound2)
        y_fox = fox_branch(h, w_fox, fox_p)

        gate_logits = matmul(h, w_g.astype(BF16), out_dtype=BF16)
        mp = dict(gate_b=row(gate_b[l]), lnx_w=row(rw_lnx_w[l]), lnx_b=row(rw_lnx_b[l]),
                  proj_rw=proj_rw[l].astype(BF16), proj_ssd=proj_ssd[l].astype(BF16),
                  proj_fox=proj_fox[l].astype(BF16), w_out=w_out[l].astype(BF16), g_m=g_m,
                  norm_gain=row(norm_ffn[l]), sc_f=sc_f, sh_f=sh_f)

        if l % 2 == 0:
            xs, hf = merge(xs, yn, bonus, g_rw, y_ssd, y_fox, gate_logits, mp)
            next_norm = None
            if l + 1 < depth:
                next_norm = (row(norm_mix[l + 1]), mods[l + 1][1], mods[l + 1][0])
            xs, h = ffn_dense(hf, xs, ffn_w_gu[l // 2].astype(BF16), ffn_w_down[l // 2].astype(BF16), g_f,
                              next_norm)
        else:
            xs, hf, comb = merge(xs, yn, bonus, g_rw, y_ssd, y_fox, gate_logits, mp,
                                 router=_pad_cols(moe_router[l // 2], LANES))
            xs = ffn_moe(hf, xs, comb, moe_w_gu[l // 2].astype(BF16), moe_w_down[l // 2].astype(BF16), g_f)
            h = None
    return xs[None]
```

```python
import functools
import math

import jax
import jax.numpy as jnp
from jax import lax
from jax.experimental import pallas as pl
from jax.experimental.pallas import tpu as pltpu

F32 = jnp.float32
BF16 = jnp.bfloat16
HIGHEST = lax.Precision.HIGHEST

D_MODEL = 1024
LANES = 128
SUBLANES = 8

RW_HEADS = 8
RW_HEAD_DIM = 64
RW_WIDTH = RW_HEADS * RW_HEAD_DIM
RW_DECAY_LORA = 64
RW_AAA_LORA = 64
RW_VRES_LORA = 32
RW_GATE_LORA = 160
GN_EPS = 64e-5
RW_OFF_WLO = 3 * RW_WIDTH
RW_OFF_ALO = RW_OFF_WLO + LANES
RW_OFF_GLO = RW_OFF_ALO + LANES
RW_OFF_HV = RW_OFF_GLO + 2 * LANES
RW_PAD_COLS = RW_OFF_HV + LANES
RW_CHUNK = 64

SSD_HEADS = 16
SSD_HEAD_DIM = 64
SSD_WIDTH = SSD_HEADS * SSD_HEAD_DIM
SSD_GROUPS = 4
SSD_STATE = 128
SSD_CONV = 4
SSD_CHUNK = 128
SSD_XBC = SSD_WIDTH + 2 * SSD_GROUPS * SSD_STATE
SSD_GROUP_WIDTH = SSD_WIDTH // SSD_GROUPS

FOX_HEADS = 8
FOX_HEAD_DIM = 64
FOX_WIDTH = FOX_HEADS * FOX_HEAD_DIM
FOX_TQ = 256
FOX_TK = 512
LOG2E = 1.4426950408889634

FFN_DENSE = 2816
N_EXPERTS = 8
TOP_K = 2
FFN_EXPERT = 3584
MOE_SB = 128
MOE_CB = 512
EPS = 1e-6

VMEM_LIMIT = 56 * 1024 * 1024


def _cparams(*sem):
    return pltpu.CompilerParams(dimension_semantics=sem, vmem_limit_bytes=VMEM_LIMIT)


def _sigmoid(x):
    return 1.0 / (1.0 + jnp.exp(-x))


def _softplus(x):
    return jnp.maximum(x, 0.0) + jnp.log(1.0 + jnp.exp(-jnp.abs(x)))


def _dot(a, b):
    return jnp.dot(a, b, preferred_element_type=F32)


def _dot_bf16(a, b):
    return jnp.dot(a.astype(BF16), b.astype(BF16), preferred_element_type=F32)


def _dot_hi(a, b):
    return jnp.dot(a, b, precision=HIGHEST, preferred_element_type=F32)


def _dot_nt(a, b, precision=None):
    return lax.dot_general(a, b, (((1,), (1,)), ((), ())), precision=precision, preferred_element_type=F32)


def _dot_tn(a, b, precision=None):
    return lax.dot_general(a, b, (((0,), (0,)), ((), ())), precision=precision, preferred_element_type=F32)


def _split_dot(x, ones_bf16):
    return _dot(x.astype(BF16), ones_bf16)


def _split3(x):
    hi = x.astype(BF16)
    r1 = x - hi.astype(F32)
    mid = r1.astype(BF16)
    return hi, mid, (r1 - mid.astype(F32)).astype(BF16)


def _tril(n, strict=False):
    r = lax.broadcasted_iota(jnp.int32, (n, n), 0)
    c = lax.broadcasted_iota(jnp.int32, (n, n), 1)
    return (r > c) if strict else (r >= c)


def _mod_kernel(c_ref, w_ref, b_ref, o_ref):
    c = c_ref[...]
    o_ref[...] = _dot_hi(c * _sigmoid(c), w_ref[...]) + b_ref[...]


def adaln_mod(c, w, b):
    d, n = w.shape
    tn = 1024
    c8 = jnp.broadcast_to(c, (SUBLANES, d))
    out = pl.pallas_call(
        _mod_kernel,
        grid=(n // tn,),
        in_specs=[pl.BlockSpec((SUBLANES, d), lambda j: (0, 0)),
                  pl.BlockSpec((d, tn), lambda j: (0, j)),
                  pl.BlockSpec((1, tn), lambda j: (0, j))],
        out_specs=pl.BlockSpec((SUBLANES, tn), lambda j: (0, j)),
        out_shape=jax.ShapeDtypeStruct((SUBLANES, n), F32),
        compiler_params=_cparams("arbitrary"),
    )(c8, w, b.reshape(1, n))
    return out[:1]


def _norm_mod(x, gain, sc, sh):
    y = x * lax.rsqrt(jnp.mean(x * x, axis=-1, keepdims=True) + EPS)
    return y * gain * (1.0 + sc) + sh


def _norm_kernel(x_ref, gain_ref, sc_ref, sh_ref, h_ref):
    h_ref[...] = _norm_mod(x_ref[...], gain_ref[...], sc_ref[...], sh_ref[...]).astype(h_ref.dtype)


def norm_mod(x, gain, sc, sh, tm=512):
    L, d = x.shape
    row = pl.BlockSpec((1, d), lambda i: (0, 0))
    return pl.pallas_call(
        _norm_kernel,
        grid=(L // tm,),
        in_specs=[pl.BlockSpec((tm, d), lambda i: (i, 0)), row, row, row],
        out_specs=pl.BlockSpec((tm, d), lambda i: (i, 0)),
        out_shape=jax.ShapeDtypeStruct((L, d), BF16),
        compiler_params=_cparams("arbitrary"),
    )(x, gain, sc, sh)


def _rw_pre_kernel(has_vres, *refs):
    if has_vres:
        (h_ref, w_ref, mu_ref, w0_ref, wup_ref, a0_ref, aup_ref, gup_ref, kk_ref, ka_ref, rk_ref, seg_ref,
         vfirst_ref, v0_ref, vup_ref,
         r_out, k_out, v_out, lw_out, a_out, b_out, g_out, bonus_out, ext_scr) = refs
    else:
        (h_ref, w_ref, mu_ref, w0_ref, wup_ref, a0_ref, aup_ref, gup_ref, kk_ref, ka_ref, rk_ref, seg_ref,
         r_out, k_out, v_out, lw_out, a_out, b_out, g_out, bonus_out, ext_scr) = refs
    T = h_ref.shape[0]

    @pl.when(pl.program_id(0) == 0)
    def _():
        ext_scr[0:SUBLANES, :] = jnp.zeros((SUBLANES, ext_scr.shape[1]), F32)

    cur = _dot(h_ref[...], w_ref[...])
    ext_scr[SUBLANES:SUBLANES + T, :] = cur
    prev = ext_scr[SUBLANES - 1:SUBLANES - 1 + T, :]
    ext_scr[0:SUBLANES, :] = cur[T - SUBLANES:T, :]
    s = cur + (prev - cur) * mu_ref[...]

    W = RW_WIDTH
    r = s[:, 0:W]
    k = s[:, W:2 * W]
    v = s[:, 2 * W:3 * W]
    w_lo = s[:, RW_OFF_WLO:RW_OFF_WLO + LANES]
    a_lo = s[:, RW_OFF_ALO:RW_OFF_ALO + LANES]
    g_lo = s[:, RW_OFF_GLO:RW_OFF_GLO + 2 * LANES]
    seg = seg_ref[...]

    wlog = -_softplus(-(w0_ref[...] + _dot_bf16(jnp.tanh(w_lo), wup_ref[...]))) - 0.5
    a = _sigmoid(a0_ref[...] + _dot_bf16(a_lo, aup_ref[...]))
    g = _dot_bf16(_sigmoid(g_lo), gup_ref[...])
    if has_vres:
        hv = s[:, RW_OFF_HV:RW_OFF_HV + LANES]
        v = v + (vfirst_ref[...].astype(F32) - v) * _sigmoid(v0_ref[...] + _dot_bf16(hv, vup_ref[...]))
    kk = k * kk_ref[...]
    kk = kk / jnp.maximum(jnp.sqrt(_split_dot(kk * kk, seg)), 1e-12)
    k = k * (1.0 + (a - 1.0) * ka_ref[...])
    r_out[...] = r.astype(r_out.dtype)
    k_out[...] = k.astype(k_out.dtype)
    v_out[...] = v.astype(v_out.dtype)
    lw_out[...] = -jnp.exp(wlog)
    a_out[...] = (-kk).astype(a_out.dtype)
    b_out[...] = (kk * a).astype(b_out.dtype)
    g_out[...] = g.astype(g_out.dtype)
    bonus_out[...] = (_split_dot(r * k * rk_ref[...], seg) * v).astype(bonus_out.dtype)


def rw_pre(h, w_rw, p, v_first, tm=512):
    L, D = h.shape
    has_vres = v_first is not None
    full = lambda a: pl.BlockSpec(a.shape, lambda i: (0,) * a.ndim)
    rows = lambda n: pl.BlockSpec((tm, n), lambda i: (i, 0))
    args = [h, w_rw, p["mu"], p["w0"], p["w_up"], p["a0"], p["a_up"], p["g_up"], p["k_k"], p["k_a"], p["r_k"],
            p["seg"]]
    specs = [rows(D)] + [full(a) for a in args[1:]]
    if has_vres:
        args += [v_first, p["v0"], p["v_up"]]
        specs += [rows(RW_WIDTH), full(p["v0"]), full(p["v_up"])]
    out_dtypes = [BF16, BF16, BF16, F32, BF16, BF16, BF16, BF16]
    return pl.pallas_call(
        functools.partial(_rw_pre_kernel, has_vres),
        grid=(L // tm,),
        in_specs=specs,
        out_specs=[rows(RW_WIDTH)] * 8,
        out_shape=[jax.ShapeDtypeStruct((L, RW_WIDTH), dt) for dt in out_dtypes],
        scratch_shapes=[pltpu.VMEM((tm + SUBLANES, RW_PAD_COLS), F32)],
        compiler_params=_cparams("arbitrary"),
    )(*args)


def _wkv_kernel(r_ref, lw_ref, k_ref, v_ref, a_ref, b_ref, y_ref, h_scr):
    C = RW_CHUNK
    C2 = 2 * C
    n_chunks = r_ref.shape[0] // C

    @pl.when(pl.program_id(1) == 0)
    def _():
        h_scr[...] = jnp.zeros_like(h_scr)

    is_a = lax.broadcasted_iota(jnp.int32, (C, LANES), 1) < RW_HEAD_DIM
    ri = lax.broadcasted_iota(jnp.int32, (C2, C2), 0)
    ci = lax.broadcasted_iota(jnp.int32, (C2, C2), 1)
    same = (ri >= C) == (ci >= C)
    strict = jnp.logical_and(same, ri > ci)
    incl = jnp.logical_and(same, ri >= ci)
    eye2 = (ri == ci).astype(F32)
    tri_c = _tril(C).astype(BF16)

    def two(x):
        return jnp.concatenate([jnp.where(is_a, x, 0.0), jnp.where(is_a, 0.0, x)], axis=0)

    chunks = range(n_chunks)
    at2, rt2, bh2, kh2, v2, n_ab, a_ak, m_rb, m_rk, p_end = ([] for _ in range(10))
    for c in chunks:
        sl = pl.ds(c * C, C)
        lw = lw_ref[sl, :]
        lw_hi = lw.astype(BF16)
        lw_lo = (lw - lw_hi.astype(F32)).astype(BF16)
        cs = _dot(tri_c, lw_hi) + _dot(tri_c, lw_lo)
        cs_end = cs[C - 1:C, :]
        e_neg = jnp.exp(-cs)
        e_end = jnp.exp(cs_end - cs)
        a = a_ref[sl, :].astype(F32)
        b = b_ref[sl, :].astype(F32)
        k = k_ref[sl, :].astype(F32)
        at2.append(two(a * jnp.exp(cs - lw)))
        rt2.append(two(r_ref[sl, :].astype(F32) * jnp.exp(cs)))
        bh2.append(two(b * e_end))
        kh2.append(two(k * e_end))
        v2.append(two(v_ref[sl, :].astype(F32)))
        p_end.append(jnp.exp(cs_end))
        quad = _dot_nt(jnp.concatenate([at2[c], rt2[c]], axis=0).astype(BF16),
                       jnp.concatenate([two(b * e_neg), two(k * e_neg)], axis=0).astype(BF16))
        n_ab.append(jnp.where(strict, quad[0:C2, 0:C2], 0.0))
        a_ak.append(jnp.where(strict, quad[0:C2, C2:], 0.0))
        m_rb.append(jnp.where(incl, quad[C2:, 0:C2], 0.0))
        m_rk.append(jnp.where(incl, quad[C2:, C2:], 0.0))
    t_inv = [eye2 + n for n in n_ab]
    n_pow = n_ab
    n_pow = [_dot_bf16(n, n) for n in n_pow]
    for _ in range(int(math.log2(C)) - 2):
        both = [_dot_bf16(jnp.concatenate([n, t], axis=0), n) for n, t in zip(n_pow, t_inv)]
        t_inv = [t + b[C2:, :] for t, b in zip(t_inv, both)]
        n_pow = [b[:C2, :] for b in both]
    t_inv = [t + _dot_bf16(t, n) for t, n in zip(t_inv, n_pow)]
    akv = [_dot_bf16(a_ak[c], v2[c]) for c in chunks]
    au = [_dot_bf16(t_inv[c], jnp.concatenate([at2[c], akv[c]], axis=1)) for c in chunks]
    mau = [_dot_bf16(m_rb[c], au[c]) for c in chunks]
    mkv = [_dot_bf16(m_rk[c], v2[c]) for c in chunks]
    gu = [_dot_tn(bh2[c].astype(BF16), au[c].astype(BF16)) for c in chunks]
    kv = [_dot_tn(kh2[c].astype(BF16), v2[c].astype(BF16)) for c in chunks]
    inv_n = 1.0 / RW_HEAD_DIM
    h = h_scr[...]
    for c in chunks:
        rh2 = rt2[c] + mau[c][:, :LANES]
        g_mat = eye2 * p_end[c] + gu[c][:, :LANES]
        yh = _dot_bf16(jnp.concatenate([rh2, g_mat], axis=0), h)
        h = yh[C2:, :] + gu[c][:, LANES:] + kv[c]
        y2 = yh[:C2, :] + mau[c][:, LANES:] + mkv[c]
        y = y2[:C, :] + y2[C:, :]
        s_all = jnp.sum(y, axis=-1, keepdims=True)
        s_a = jnp.sum(jnp.where(is_a, y, 0.0), axis=-1, keepdims=True)
        yc = y - jnp.where(is_a, s_a, s_all - s_a) * inv_n
        sq = yc * yc
        q_all = jnp.sum(sq, axis=-1, keepdims=True)
        q_a = jnp.sum(jnp.where(is_a, sq, 0.0), axis=-1, keepdims=True)
        yn = yc * lax.rsqrt(jnp.where(is_a, q_a, q_all - q_a) * inv_n + GN_EPS)
        y_ref[pl.ds(c * C, C), :] = yn.astype(y_ref.dtype)
    h_scr[...] = h


def wkv7(r, lw, k, v, a, b, rows=512):
    L, W = r.shape
    spec = pl.BlockSpec((rows, LANES), lambda p, i: (i, p))
    return pl.pallas_call(
        _wkv_kernel,
        grid=(W // LANES, L // rows),
        in_specs=[spec] * 6,
        out_specs=spec,
        out_shape=jax.ShapeDtypeStruct((L, W), BF16),
        scratch_shapes=[pltpu.VMEM((LANES, LANES), F32)],
        compiler_params=_cparams("arbitrary", "arbitrary"),
    )(r, lw, k, v, a, b)


def _ssd_kernel(h_ref, w_ref, cw_ref, cb_ref, bias_ref, A_ref, exp_ref, dskip_ref, nw_ref, o_ref,
                ext_scr, xbc_scr, s_scr):
    Q = SSD_CHUNK
    GW = SSD_GROUP_WIDTH
    NS = SSD_STATE
    R = h_ref.shape[0]

    @pl.when(pl.program_id(0) == 0)
    def _():
        s_scr[...] = jnp.zeros_like(s_scr)
        ext_scr[0:SUBLANES, :] = jnp.zeros((SUBLANES, ext_scr.shape[1]), F32)

    cols = _dot(h_ref[...], w_ref[...])
    ext_scr[SUBLANES:SUBLANES + R, :] = cols[:, 0:SSD_XBC]
    acc = cols[:, 0:SSD_XBC] * cw_ref[SSD_CONV - 1:SSD_CONV, :] + cb_ref[...]
    for j in range(1, SSD_CONV):
        acc = acc + ext_scr[SUBLANES - j:SUBLANES - j + R, :] * cw_ref[SSD_CONV - 1 - j:SSD_CONV - j, :]
    ext_scr[0:SUBLANES, :] = ext_scr[R:R + SUBLANES, :]
    xbc_scr[...] = acc * _sigmoid(acc)

    incl = _tril(Q)
    tri = incl.astype(BF16)
    triu = jnp.logical_not(_tril(Q, strict=True)).astype(BF16)
    expand = exp_ref[...]
    first_half = lax.broadcasted_iota(jnp.int32, (Q, LANES), 1) < SSD_HEAD_DIM
    heads_per_group = SSD_HEADS // SSD_GROUPS
    for c in range(R // Q):
        rows = slice(c * Q, (c + 1) * Q)
        xs = xbc_scr[rows, 0:SSD_WIDTH]
        z = cols[rows, SSD_XBC:SSD_XBC + SSD_WIDTH]
        dt = _softplus(cols[rows, SSD_XBC + SSD_WIDTH:] + bias_ref[...])
        a3 = _split3(dt * A_ref[...])
        a_cum = sum(_dot(tri, t) for t in a3)
        a_cumT = sum(_dot_tn(t, triu) for t in a3)
        a_end = a_cum[Q - 1:Q, :]
        dt_x = _dot(dt.astype(BF16), expand)
        eac_x = _dot(jnp.exp(a_cum).astype(BF16), expand)
        dte_x = _dot(jnp.exp(a_end - a_cum).astype(BF16), expand)
        cd_x = sum(_dot(t, expand) for t in _split3(jnp.broadcast_to(jnp.exp(a_end), (SUBLANES, LANES))))[0:1, :]
        xdt = xs * dt_x
        for g in range(SSD_GROUPS):
            Bg = xbc_scr[rows, SSD_WIDTH + g * NS:SSD_WIDTH + (g + 1) * NS].astype(BF16)
            Cg = xbc_scr[rows, SSD_WIDTH + (SSD_GROUPS + g) * NS:SSD_WIDTH + (SSD_GROUPS + g + 1) * NS].astype(BF16)
            cb = _dot_nt(Cg, Bg)
            gs = slice(g * GW, (g + 1) * GW)
            s_prev = s_scr[:, gs]
            y_g = _dot_bf16(Cg, s_prev) * eac_x[:, gs]
            pieces = []
            for pr in range(heads_per_group // 2):
                ps = slice(g * GW + pr * LANES, g * GW + (pr + 1) * LANES)
                xdt_p = xdt[:, ps]
                acc_p = None
                for e in range(2):
                    hd = g * heads_per_group + pr * 2 + e
                    seg = a_cum[:, hd:hd + 1] - a_cumT[hd:hd + 1, :]
                    m = cb * jnp.exp(jnp.where(incl, seg, -jnp.inf))
                    xm = jnp.where(first_half if e == 0 else jnp.logical_not(first_half), xdt_p, 0.0)
                    t = _dot_bf16(m, xm)
                    acc_p = t if acc_p is None else acc_p + t
                pieces.append(acc_p)
            y_g = y_g + jnp.concatenate(pieces, axis=1)
            s_scr[:, gs] = s_prev * cd_x[:, gs] +_dot_tn(Bg, (dte_x[:, gs] * xdt[:, gs]).astype(BF16))
            y_g = y_g + dskip_ref[:, gs] * xs[:, gs]
            zg = z[:, gs]
            y_g = y_g * (zg * _sigmoid(zg))
            y_g = y_g * lax.rsqrt(jnp.mean(y_g * y_g, axis=-1, keepdims=True) + EPS)
            o_ref[rows, gs] = (y_g * nw_ref[:, gs]).astype(o_ref.dtype)


def ssd_mixer(h, w_ssd, p, rows=512):
    L, D = h.shape
    N = w_ssd.shape[1]
    full = lambda a: pl.BlockSpec(a.shape, lambda i: (0,) * a.ndim)
    small = [p["conv_w"], p["conv_b"], p["dt_bias"], p["A"], p["expand"], p["d_skip"], p["norm_w"]]
    return pl.pallas_call(
        _ssd_kernel,
        grid=(L // rows,),
        in_specs=[pl.BlockSpec((rows, D), lambda i: (i, 0)), full(w_ssd)] + [full(a) for a in small],
        out_specs=pl.BlockSpec((rows, SSD_WIDTH), lambda i: (i, 0)),
        out_shape=jax.ShapeDtypeStruct((L, SSD_WIDTH), BF16),
        scratch_shapes=[pltpu.VMEM((rows + SUBLANES, SSD_XBC), F32), pltpu.VMEM((rows, SSD_XBC), F32),
                        pltpu.VMEM((SSD_STATE, SSD_WIDTH), F32)],
        compiler_params=_cparams("arbitrary"),
    )(h, w_ssd, *small)


def _fox_pre_kernel(h_ref, w_ref, fb_ref, qg_ref, kg_ref, seg_ref, q_out, k_out, v_out, f_out, carry_scr):
    T = h_ref.shape[0]
    W = FOX_WIDTH

    @pl.when(pl.program_id(0) == 0)
    def _():
        carry_scr[...] = jnp.zeros_like(carry_scr)

    cols = _dot(h_ref[...], w_ref[...])
    seg = seg_ref[...]
    q = cols[:, 0:W]
    k = cols[:, W:2 * W]
    inv_d = 1.0 / FOX_HEAD_DIM
    qn = q * lax.rsqrt(_split_dot(q * q, seg) * inv_d + EPS) * qg_ref[...]
    kn = k * lax.rsqrt(_split_dot(k * k, seg) * inv_d + EPS) * kg_ref[...]
    q_out[...] = qn.astype(q_out.dtype)
    k_out[...] = kn.astype(k_out.dtype)
    v_out[...] = cols[:, 2 * W:3 * W].astype(v_out.dtype)
    f = cols[:, 3 * W:3 * W + LANES] + fb_ref[...]
    ls = -_softplus(-f)
    cum = _dot_hi(_tril(T).astype(F32), ls) + carry_scr[0:1, :]
    f_out[...] = cum * LOG2E
    carry_scr[...] = jnp.broadcast_to(cum[T - 1:T, :], carry_scr.shape)


def fox_pre(h, w_fox, p, tm=512):
    L, D = h.shape
    full = lambda a: pl.BlockSpec(a.shape, lambda i: (0,) * a.ndim)
    rows = lambda n: pl.BlockSpec((tm, n), lambda i: (i, 0))
    small = [w_fox, p["f_bias"], p["q_gain"], p["k_gain"], p["seg"]]
    return pl.pallas_call(
        _fox_pre_kernel,
        grid=(L // tm,),
        in_specs=[rows(D)] + [full(a) for a in small],
        out_specs=[rows(FOX_WIDTH)] * 3 + [rows(LANES)],
        out_shape=[jax.ShapeDtypeStruct((L, FOX_WIDTH), BF16)] * 3 + [jax.ShapeDtypeStruct((L, LANES), F32)],
        scratch_shapes=[pltpu.VMEM((SUBLANES, LANES), F32)],
        compiler_params=_cparams("arbitrary"),
    )(h, *small)


def _fox_kernel(lo_ref, q_ref, k_ref, v_ref, fk_ref, o_ref, m_scr, acc_scr, s0_scr, s1_scr):
    tq = q_ref.shape[0]
    tk = FOX_TK
    pair = pl.program_id(0)
    qi = pl.program_id(1)
    is_a = lax.broadcasted_iota(jnp.int32, (tq, LANES), 1) < FOX_HEAD_DIM
    q = q_ref[...]
    zero = jnp.zeros_like(q)
    qs = (jnp.where(is_a, q, zero), jnp.where(is_a, zero, q))
    m_scr[...] = jnp.full(m_scr.shape, -jnp.inf, F32)
    key_is_a = lax.broadcasted_iota(jnp.int32, (tk, LANES), 1) < FOX_HEAD_DIM
    acc_scr[...] = jnp.zeros_like(acc_scr)
    rel = lax.broadcasted_iota(jnp.int32, (tq, tk), 0) - lax.broadcasted_iota(jnp.int32, (tq, tk), 1)
    last = (qi * tq) // tk

    def fill(s_ref, kb, diagonal=False):
        ks = pl.ds(pl.multiple_of(kb * tk, tk), tk)
        k_blk = k_ref[ks, :]
        for e in range(2):
            s = _dot_nt(qs[e], k_blk) - fk_ref[pl.ds(2 * pair + e, 1), ks]
            if diagonal:
                s = jnp.where(rel >= kb * tk - qi * tq, s, -jnp.inf)
            s_ref[e] = s

    def consume(s_ref, kb):
        v_blk = v_ref[pl.ds(pl.multiple_of(kb * tk, tk), tk), :]
        one = jnp.ones_like(v_blk)
        v_one = (jnp.where(key_is_a, v_blk, one), jnp.where(key_is_a, one, v_blk))
        for e in range(2):
            s = s_ref[e]
            m_old = m_scr[e]
            m_new = jnp.maximum(m_old, jnp.max(s, axis=-1, keepdims=True))
            pexp = jnp.exp2(s - m_new)
            acc_scr[e] = jnp.exp2(m_old - m_new) * acc_scr[e] + _dot(pexp.astype(BF16), v_one[e])
            m_scr[e] = m_new

    n = last - jnp.minimum(lo_ref[2 * pair, qi], lo_ref[2 * pair + 1, qi])
    fill(s0_scr, last, diagonal=True)

    def two_steps(u, carry):
        kb = last - 2 * u
        fill(s1_scr, kb - 1)
        consume(s0_scr, kb)
        fill(s0_scr, kb - 2)
        consume(s1_scr, kb - 1)
        return carry

    lax.fori_loop(0, n // 2, two_steps, 0)

    @pl.when(n % 2 == 1)
    def _():
        fill(s1_scr, last - n)
        consume(s0_scr, last - n + 1)
        consume(s1_scr, last - n)

    @pl.when(n % 2 == 0)
    def _():
        consume(s0_scr, last - n)

    acc_a = acc_scr[0]
    acc_b = acc_scr[1]
    half = FOX_HEAD_DIM
    o_ref[...] = jnp.where(is_a, acc_a / pltpu.roll(acc_a, half, 1),
                           acc_b / pltpu.roll(acc_b, half, 1)).astype(o_ref.dtype)


def fox_attention(q, k, v, FT, lo):
    L, W = q.shape
    H = FT.shape[0]
    tq = FOX_TQ
    grid_spec = pltpu.PrefetchScalarGridSpec(
        num_scalar_prefetch=1,
        grid=(W // LANES, L // tq),
        in_specs=[pl.BlockSpec((tq, LANES), lambda p, i, lo_r: (i, p)),
                  pl.BlockSpec((L, LANES), lambda p, i, lo_r: (0, p)),
                  pl.BlockSpec((L, LANES), lambda p, i, lo_r: (0, p)),
                  pl.BlockSpec((H, L), lambda p, i, lo_r: (0, 0))],
        out_specs=pl.BlockSpec((tq, LANES), lambda p, i, lo_r: (i, p)),
        scratch_shapes=[pltpu.VMEM((2, tq, 1), F32), pltpu.VMEM((2, tq, LANES), F32),
                        pltpu.VMEM((2, tq, FOX_TK), F32), pltpu.VMEM((2, tq, FOX_TK), F32)],
    )
    return pl.pallas_call(
        _fox_kernel,
        grid_spec=grid_spec,
        out_shape=jax.ShapeDtypeStruct((L, W), BF16),
        compiler_params=_cparams("arbitrary", "arbitrary"),
    )(lo, q, k, v, FT)


def fox_first_block(FT, bound2):
    f_first = FT[:, ::FOX_TQ]
    f_last = FT[:, FOX_TK - 1::FOX_TK]
    gap = f_first[:, :, None] - f_last[:, None, :] + 2.0 * bound2
    last = (jnp.arange(f_first.shape[1]) * FOX_TQ) // FOX_TK
    needed = jnp.logical_or(gap >= -152.0, jnp.arange(f_last.shape[1])[None, None, :] >= last[None, :, None])
    return jnp.argmax(needed, axis=-1).astype(jnp.int32)


def _top2_combine(h, wr):
    T = h.shape[0]
    lane = lax.broadcasted_iota(jnp.int32, (T, LANES), 1)
    h_hi = h.astype(BF16)
    h_lo = (h - h_hi.astype(F32)).astype(BF16)
    w_hi = wr.astype(BF16)
    w_lo = (wr - w_hi.astype(F32)).astype(BF16)
    logits = _dot(h_hi, w_hi) + (_dot(h_hi, w_lo) + _dot(h_lo, w_hi))
    logits = jnp.where(lane < N_EXPERTS, logits, -jnp.inf)
    m1 = jnp.max(logits, axis=-1, keepdims=True)
    i1 = jnp.min(jnp.where(logits == m1, lane, LANES), axis=-1, keepdims=True)
    rest = jnp.where(lane == i1, -jnp.inf, logits)
    m2 = jnp.max(rest, axis=-1, keepdims=True)
    i2 = jnp.min(jnp.where(rest == m2, lane, LANES), axis=-1, keepdims=True)
    e2 = jnp.exp(m2 - m1)
    return jnp.where(lane == i1, 1.0 / (1.0 + e2), 0.0) + jnp.where(lane == i2, e2 / (1.0 + e2), 0.0)


def _merge_kernel(with_router, *refs):
    (x_ref, hm_ref, yn_ref, bonus_ref, g_ref, yssd_ref, yfox_ref, wgate_ref, gb_ref, lnw_ref, lnb_ref,
     prw_ref, pssd_ref, pfox_ref, wout_ref, gm_ref, gain_ref, sc_ref, sh_ref) = refs[:19]
    D = D_MODEL
    hm = hm_ref[...]
    y_rw = (yn_ref[...] * lnw_ref[...] + lnb_ref[...] + bonus_ref[...]) * g_ref[...]
    merged = None
    for b, (y_b, proj_ref) in enumerate(((y_rw, prw_ref), (yssd_ref[...], pssd_ref), (yfox_ref[...], pfox_ref))):
        cols = slice(b * D, (b + 1) * D)
        gate = _sigmoid(_dot(hm, wgate_ref[:, cols]) + gb_ref[:, cols])
        term = gate * _dot_bf16(y_b, proj_ref[...])
        merged = term if merged is None else merged + term
    x_new = x_ref[...] + gm_ref[...] * _dot_bf16(merged, wout_ref[...])
    hf = _norm_mod(x_new, gain_ref[...], sc_ref[...], sh_ref[...])
    if with_router:
        wr_ref, o_ref, h_ref, comb_ref = refs[19:]
        comb_ref[...] = _top2_combine(hf, wr_ref[...])
    else:
        o_ref, h_ref = refs[19:]
    o_ref[...] = x_new
    h_ref[...] = hf.astype(h_ref.dtype)


def merge(x, hm, yn, bonus, g, y_ssd, y_fox, p, router=None, tm=256):
    L, D = x.shape
    full = lambda a: pl.BlockSpec(a.shape, lambda i: (0,) * a.ndim)
    rows = lambda n: pl.BlockSpec((tm, n), lambda i: (i, 0))
    small = [p["w_gate"], p["gate_b"], p["lnx_w"], p["lnx_b"], p["proj_rw"], p["proj_ssd"], p["proj_fox"],
             p["w_out"], p["g_m"], p["norm_gain"], p["sc_f"], p["sh_f"]]
    out_specs = [rows(D), rows(D)]
    out_shape = [jax.ShapeDtypeStruct((L, D), F32), jax.ShapeDtypeStruct((L, D), BF16)]
    if router is not None:
        small.append(router)
        out_specs.append(rows(LANES))
        out_shape.append(jax.ShapeDtypeStruct((L, LANES), F32))
    return pl.pallas_call(
        functools.partial(_merge_kernel, router is not None),
        grid=(L // tm,),
        in_specs=[rows(D), rows(D), rows(RW_WIDTH), rows(RW_WIDTH), rows(RW_WIDTH), rows(SSD_WIDTH),
                  rows(FOX_WIDTH)] + [full(a) for a in small],
        out_specs=out_specs,
        out_shape=out_shape,
        compiler_params=_cparams("arbitrary"),
    )(x, hm, yn, bonus, g, y_ssd, y_fox, *small)


def _ffn_kernel(with_next, *refs):
    h_ref, x_ref, wg_ref, wu_ref, wd_ref, gf_ref = refs[:6]
    j = pl.program_id(1)
    acc_scr = refs[-1]

    @pl.when(j == 0)
    def _():
        acc_scr[...] = jnp.zeros_like(acc_scr)

    h = h_ref[...]
    gte = _dot(h, wg_ref[...])
    up = _dot(h, wu_ref[...])
    act = gte * _sigmoid(gte) * up
    acc_scr[...] += _dot(act.astype(BF16), wd_ref[...])

    @pl.when(j == pl.num_programs(1) - 1)
    def _():
        x_new = x_ref[...] + gf_ref[...] * acc_scr[...]
        if with_next:
            gain_ref, sc_ref, sh_ref, o_ref, hn_ref = refs[6:11]
            hn_ref[...] = _norm_mod(x_new, gain_ref[...], sc_ref[...], sh_ref[...]).astype(hn_ref.dtype)
        else:
            o_ref = refs[6]
        o_ref[...] = x_new


def ffn_dense(h, x, w_gu, w_down, g_f, next_norm=None, tm=1024, tf=256):
    L, D = x.shape
    Fh = w_down.shape[0]
    nf = Fh // tf
    row = pl.BlockSpec((1, D), lambda i, j: (0, 0))
    tile = pl.BlockSpec((tm, D), lambda i, j: (i, 0))
    args = [h, x, w_gu, w_gu, w_down, g_f]
    in_specs = [tile, tile,
                pl.BlockSpec((D, tf), lambda i, j: (0, j)),
                pl.BlockSpec((D, tf), lambda i, j: (0, j + nf)),
                pl.BlockSpec((tf, D), lambda i, j: (j, 0)), row]
    out_specs = [tile]
    out_shape = [jax.ShapeDtypeStruct((L, D), F32)]
    if next_norm is not None:
        args += list(next_norm)
        in_specs += [row, row, row]
        out_specs.append(tile)
        out_shape.append(jax.ShapeDtypeStruct((L, D), BF16))
    out = pl.pallas_call(
        functools.partial(_ffn_kernel, next_norm is not None),
        grid=(L // tm, nf),
        in_specs=in_specs,
        out_specs=out_specs,
        out_shape=out_shape,
        scratch_shapes=[pltpu.VMEM((tm, D), F32)],
        compiler_params=_cparams("arbitrary", "arbitrary"),
    )(*args)
    return out if next_norm is not None else (out[0], None)


def _moe_slots(slot_scr, comb_ref):
    slotval = slot_scr[...]
    comb = comb_ref[...]
    s1 = jnp.max(slotval, axis=-1, keepdims=True)
    first = slotval == s1
    c1 = jnp.sum(jnp.where(first, comb, 0.0), axis=-1, keepdims=True)
    rest = jnp.where(first, -1.0, slotval)
    s2 = jnp.max(rest, axis=-1, keepdims=True)
    c2 = jnp.sum(jnp.where(jnp.logical_and(rest == s2, rest >= 0.0), comb, 0.0), axis=-1, keepdims=True)
    return s1, s2, c1, c2


def _moe_kernel(nsb_ref, off_ref, h_ref, x_ref, comb_ref, wg_ref, wu_ref, wd_ref, gf_ref, o_ref,
                slot_scr, cslot_scr, xy_scr, acc_scr):
    i = pl.program_id(0)
    e = pl.program_id(1)
    j = pl.program_id(2)
    last_j = pl.num_programs(2) - 1
    T = h_ref.shape[0]
    S = xy_scr.shape[0]
    SB = MOE_SB

    @pl.when(jnp.logical_and(e == 0, j == 0))
    def _():
        lane = lax.broadcasted_iota(jnp.int32, (LANES, LANES), 1)
        off = jnp.zeros((LANES, LANES), F32)
        for ex in range(N_EXPERTS):
            off = jnp.where(lane == ex, off_ref[i, ex].astype(F32), off)
        before = _tril(LANES, strict=True).astype(BF16)
        ones = jnp.ones((LANES, LANES), BF16)
        for rb in range(T // LANES):
            rows = slice(rb * LANES, (rb + 1) * LANES)
            sel = comb_ref[rows, :] > 0.0
            sel_b = jnp.where(sel, 1.0, 0.0).astype(BF16)
            slot_scr[rows, :] = jnp.where(sel, _dot(before, sel_b) + off, -1.0)
            off = off + _dot(ones, sel_b)
        s1, s2, c1, c2 = _moe_slots(slot_scr, comb_ref)
        used = off_ref[i, N_EXPERTS - 1] + nsb_ref[i, N_EXPERTS - 1] * SB
        for cb in range(S // MOE_CB):
            chunk = slice(cb * MOE_CB, (cb + 1) * MOE_CB)

            @pl.when(used > cb * MOE_CB)
            def _():
                slot = (lax.broadcasted_iota(jnp.int32, (T, MOE_CB), 1) + cb * MOE_CB).astype(F32)
                w = jnp.where(s1 == slot, c1, 0.0) + jnp.where(s2 == slot, c2, 0.0)
                pt = jnp.where(w > 0.0, 1.0, 0.0).astype(BF16)
                xy_scr[chunk, :] = _dot_tn(pt, h_ref[...]).astype(BF16)
                w_hi = w.astype(BF16)
                w_lo = (w - w_hi.astype(F32)).astype(BF16)
                ones_t = jnp.ones((T, LANES), BF16)
                cslot_scr[chunk, :] = _dot_tn(w_hi, ones_t) + _dot_tn(w_lo, ones_t)

            @pl.when(used <= cb * MOE_CB)
            def _():
                xy_scr[chunk, :] = jnp.zeros((MOE_CB, xy_scr.shape[1]), BF16)
                cslot_scr[chunk, :] = jnp.zeros((MOE_CB, LANES), F32)

    base = off_ref[i, e]
    nb = nsb_ref[i, e]

    def ffn_block(r0, n_rows):
        rows = pl.ds(pl.multiple_of(base + r0, SB), n_rows)
        arows = pl.ds(pl.multiple_of(r0, SB), n_rows)
        xb = xy_scr[rows, :]
        gte = _dot(xb, wg_ref[0])
        up = _dot(xb, wu_ref[0])
        part = _dot((gte * _sigmoid(gte) * up).astype(BF16), wd_ref[0])

        @pl.when(j == 0)
        def _():
            acc_scr[arows, :] = part

        @pl.when(jnp.logical_and(j > 0, j < last_j))
        def _():
            acc_scr[arows, :] += part

        @pl.when(j == last_j)
        def _():
            xy_scr[rows, :] = ((acc_scr[arows, :] + part) * cslot_scr[rows, 0:1]).astype(BF16)

    def pair_body(p, carry):
        ffn_block(p * (2 * SB), 2 * SB)
        return carry

    lax.fori_loop(0, nb // 2, pair_body, 0)

    @pl.when(nb % 2 == 1)
    def _():
        ffn_block((nb - 1) * SB, SB)

    @pl.when(jnp.logical_and(e == pl.num_programs(1) - 1, j == last_j))
    def _():
        s1, s2, _, _ = _moe_slots(slot_scr, comb_ref)
        total = jnp.zeros((T, xy_scr.shape[1]), F32)
        for cb in range(S // MOE_CB):
            slot = (lax.broadcasted_iota(jnp.int32, (T, MOE_CB), 1) + cb * MOE_CB).astype(F32)
            pt = jnp.where(jnp.logical_or(s1 == slot, s2 == slot), 1.0, 0.0).astype(BF16)
            total = total + _dot(pt, xy_scr[cb * MOE_CB:(cb + 1) * MOE_CB, :])
        o_ref[...] = x_ref[...] + gf_ref[...] * total


def ffn_moe(h, x, comb, w_gu, w_down, g_f, tm=1024, tf=896):
    L, D = x.shape
    E, Fh, _ = w_down.shape
    nf = Fh // tf
    assert nf >= 2 and E == N_EXPERTS
    nt = L // tm
    counts = jnp.sum((comb[:, :E] > 0.0).reshape(nt, tm, E), axis=1, dtype=jnp.int32)
    nsb = (counts + (MOE_SB - 1)) // MOE_SB
    off = (jnp.cumsum(nsb, axis=1) - nsb) * MOE_SB
    idx = lambda f: (lambda i, e, j, n, o: f(i, e, j))
    grid_spec = pltpu.PrefetchScalarGridSpec(
        num_scalar_prefetch=2,
        grid=(nt, E, nf),
        in_specs=[pl.BlockSpec((tm, D), idx(lambda i, e, j: (i, 0))),
                  pl.BlockSpec((tm, D), idx(lambda i, e, j: (i, 0))),
                  pl.BlockSpec((tm, LANES), idx(lambda i, e, j: (i, 0))),
                  pl.BlockSpec((1, D, tf), idx(lambda i, e, j: (e, 0, j))),
                  pl.BlockSpec((1, D, tf), idx(lambda i, e, j: (e, 0, j + nf))),
                  pl.BlockSpec((1, tf, D), idx(lambda i, e, j: (e, j, 0))),
                  pl.BlockSpec((1, D), idx(lambda i, e, j: (0, 0)))],
        out_specs=pl.BlockSpec((tm, D), idx(lambda i, e, j: (i, 0))),
        scratch_shapes=[pltpu.VMEM((tm, LANES), F32),
                        pltpu.VMEM((TOP_K * tm + E * MOE_SB, LANES), F32),
                        pltpu.VMEM((TOP_K * tm + E * MOE_SB, D), BF16),
                        pltpu.VMEM((tm, D), F32)],
    )
    return pl.pallas_call(
        _moe_kernel,
        grid_spec=grid_spec,
        out_shape=jax.ShapeDtypeStruct((L, D), F32),
        compiler_params=_cparams("arbitrary", "arbitrary", "arbitrary"),
    )(nsb, off, h, x, comb, w_gu, w_gu, w_down, g_f)


def _seg_matrix(width, head_dim):
    idx = jnp.arange(width) // head_dim
    return (idx[:, None] == idx[None, :]).astype(BF16)


def _pad_cols(w, n):
    return jnp.pad(w, ((0, 0), (0, n - w.shape[1])))


def _pad_rows(w, n):
    return jnp.pad(w, ((0, n - w.shape[0]), (0, 0)))


def rwkv_branch(h, w_rw, p, v_first):
    r, k, v, lw, a, b, g, bonus = rw_pre(h, w_rw, p, v_first)
    return wkv7(r, lw, k, v, a, b), bonus, g, v


def fox_branch(h, w_fox, p):
    q, k, v, F = fox_pre(h, w_fox, p)
    FT = jnp.transpose(F[:, :FOX_HEADS])
    lo = fox_first_block(FT, p["bound2"])
    return fox_attention(q, k, v, FT, lo)


def kernel(x, c, ada_w, ada_b, norm_mix, norm_ffn, w_in, rw_mu, rw_w0, rw_w_up, rw_a0, rw_a_up, rw_g_up, rw_k_k, rw_k_a, rw_r_k, rw_lnx_w, rw_lnx_b, rw_v0, rw_v_down, rw_v_up, ssd_conv_w, ssd_conv_b, ssd_dt_bias, ssd_a_log, ssd_d, ssd_norm, fox_f_bias, fox_q_gain, fox_k_gain, gate_b, proj_rw, proj_ssd, proj_fox, w_out, ffn_w_gu, ffn_w_down, moe_router, moe_w_gu, moe_w_down):
    depth = w_in.shape[0]
    D = D_MODEL
    xs = x[0]
    row = lambda t: t.reshape(1, -1).astype(F32)
    seg64 = _seg_matrix(RW_WIDTH, RW_HEAD_DIM)
    rw_cols = 3 * RW_WIDTH + RW_DECAY_LORA + RW_AAA_LORA + RW_GATE_LORA
    ssd_cols = SSD_WIDTH + SSD_XBC + SSD_HEADS
    fox_cols = 3 * FOX_WIDTH + FOX_HEADS
    expand = (jnp.arange(LANES)[:, None] == (jnp.arange(SSD_WIDTH) // SSD_HEAD_DIM)[None, :]).astype(BF16)
    v_first = None
    mods = []
    for l in range(depth):
        mod = adaln_mod(c, ada_w[l], ada_b[l])
        mods.append([mod[:, i * D:(i + 1) * D] for i in range(6)])
    h = None
    for l in range(depth):
        sh_m, sc_m, g_m, sh_f, sc_f, g_f = mods[l]
        if h is None:
            h = norm_mod(xs, row(norm_mix[l]), sc_m, sh_m)

        wl = w_in[l]
        o = 0
        w_r = wl[:, o:o + rw_cols]; o += rw_cols
        w_s = wl[:, o:o + ssd_cols]; o += ssd_cols
        w_f = wl[:, o:o + fox_cols]; o += fox_cols
        w_g = wl[:, o:]
        W3 = 3 * RW_WIDTH
        o_a = W3 + RW_DECAY_LORA
        o_g = o_a + RW_AAA_LORA

        def rw_layout(t, hv):
            parts = [t[:, :W3], _pad_cols(t[:, W3:o_a], LANES), _pad_cols(t[:, o_a:o_g], LANES),
                     _pad_cols(t[:, o_g:], 2 * LANES), _pad_cols(hv, LANES)]
            return jnp.concatenate(parts, axis=1)

        if l == 0:
            hv_w = jnp.zeros((D, RW_VRES_LORA), F32)
        else:
            hv_w = rw_v_down[l - 1]
        w_rw = rw_layout(w_r, hv_w).astype(BF16)
        mu = rw_layout(rw_mu[l].reshape(1, -1), jnp.zeros((1, RW_VRES_LORA), F32))
        rw_p = dict(mu=mu, w0=row(rw_w0[l]), w_up=_pad_rows(rw_w_up[l], LANES).astype(BF16), a0=row(rw_a0[l]),
                    a_up=_pad_rows(rw_a_up[l], LANES).astype(BF16),
                    g_up=_pad_rows(rw_g_up[l], 2 * LANES).astype(BF16),
                    k_k=row(rw_k_k[l]), k_a=row(rw_k_a[l]), r_k=row(rw_r_k[l]), seg=seg64)
        if l > 0:
            rw_p.update(v0=row(rw_v0[l - 1]), v_up=_pad_rows(rw_v_up[l - 1], LANES).astype(BF16))
        yn, bonus, g_rw, v_cur = rwkv_branch(h, w_rw, rw_p, v_first)
        if l == 0:
            v_first = v_cur

        w_ssd = jnp.concatenate([w_s[:, SSD_WIDTH:SSD_WIDTH + SSD_XBC], w_s[:, :SSD_WIDTH],
                                 _pad_cols(w_s[:, SSD_WIDTH + SSD_XBC:], LANES)], axis=1).astype(BF16)
        A = -jnp.exp(ssd_a_log[l].astype(F32))
        ssd_p = dict(conv_w=ssd_conv_w[l], conv_b=row(ssd_conv_b[l]),
                     dt_bias=_pad_cols(row(ssd_dt_bias[l]), LANES),
                     A=_pad_cols(row(A), LANES), expand=expand,
                     d_skip=row(jnp.repeat(ssd_d[l], SSD_HEAD_DIM)), norm_w=row(ssd_norm[l]))
        y_ssd = ssd_mixer(h, w_ssd, ssd_p)

        w_fox = jnp.concatenate([w_f[:, :3 * FOX_WIDTH], _pad_cols(w_f[:, 3 * FOX_WIDTH:], LANES)],
                                axis=1).astype(BF16)
        bound2 = (1.02 * FOX_HEAD_DIM ** 0.5 * LOG2E) * jnp.max(jnp.abs(fox_q_gain[l])) * jnp.max(jnp.abs(fox_k_gain[l]))
        fox_p = dict(f_bias=_pad_cols(row(fox_f_bias[l]), LANES),
                     q_gain=row(jnp.tile(fox_q_gain[l], FOX_HEADS)) * (FOX_HEAD_DIM ** -0.5 * LOG2E),
                     k_gain=row(jnp.tile(fox_k_gain[l], FOX_HEADS)), seg=seg64, bound2=bound2)
        y_fox = fox_branch(h, w_fox, fox_p)

        mp = dict(w_gate=w_g.astype(BF16), gate_b=row(gate_b[l]), lnx_w=row(rw_lnx_w[l]), lnx_b=row(rw_lnx_b[l]),
                  proj_rw=proj_rw[l].astype(BF16), proj_ssd=proj_ssd[l].astype(BF16),
                  proj_fox=proj_fox[l].astype(BF16), w_out=w_out[l].astype(BF16), g_m=g_m,
                  norm_gain=row(norm_ffn[l]), sc_f=sc_f, sh_f=sh_f)

        if l % 2 == 0:
            xs, hf = merge(xs, h, yn, bonus, g_rw, y_ssd, y_fox, mp)
            next_norm = None
            if l + 1 < depth:
                next_norm = (row(norm_mix[l + 1]), mods[l + 1][1], mods[l + 1][0])
            xs, h = ffn_dense(hf, xs, ffn_w_gu[l // 2].astype(BF16), ffn_w_down[l // 2].astype(BF16), g_f,
                              next_norm)
        else:
            xs, hf, comb = merge(xs, h, yn, bonus, g_rw, y_ssd, y_fox, mp,
                                 router=_pad_cols(moe_router[l // 2], LANES))
            xs = ffn_moe(hf, xs, comb, moe_w_gu[l // 2].astype(BF16), moe_w_down[l // 2].astype(BF16), g_f)
            h = None
    return xs[None]
```

```python
import functools
import math

import jax
import jax.numpy as jnp
from jax import lax
from jax.experimental import pallas as pl
from jax.experimental.pallas import tpu as pltpu

F32 = jnp.float32
BF16 = jnp.bfloat16
HIGHEST = lax.Precision.HIGHEST

D_MODEL = 1024
LANES = 128
SUBLANES = 8

RW_HEADS = 8
RW_HEAD_DIM = 64
RW_WIDTH = RW_HEADS * RW_HEAD_DIM
RW_DECAY_LORA = 64
RW_AAA_LORA = 64
RW_VRES_LORA = 32
RW_GATE_LORA = 160
GN_EPS = 64e-5
RW_OFF_WLO = 3 * RW_WIDTH
RW_OFF_ALO = RW_OFF_WLO + LANES
RW_OFF_GLO = RW_OFF_ALO + LANES
RW_OFF_HV = RW_OFF_GLO + 2 * LANES
RW_PAD_COLS = RW_OFF_HV + LANES
RW_CHUNK = 64

SSD_HEADS = 16
SSD_HEAD_DIM = 64
SSD_WIDTH = SSD_HEADS * SSD_HEAD_DIM
SSD_GROUPS = 4
SSD_STATE = 128
SSD_CONV = 4
SSD_CHUNK = 128
SSD_XBC = SSD_WIDTH + 2 * SSD_GROUPS * SSD_STATE
SSD_GROUP_WIDTH = SSD_WIDTH // SSD_GROUPS

FOX_HEADS = 8
FOX_HEAD_DIM = 64
FOX_WIDTH = FOX_HEADS * FOX_HEAD_DIM
FOX_TQ = 512
FOX_TK = 512
LOG2E = 1.4426950408889634

FFN_DENSE = 2816
N_EXPERTS = 8
TOP_K = 2
FFN_EXPERT = 3584
MOE_SB = 128
MOE_CB = 512
EPS = 1e-6

VMEM_LIMIT = 56 * 1024 * 1024


def _cparams(*sem):
    return pltpu.CompilerParams(dimension_semantics=sem, vmem_limit_bytes=VMEM_LIMIT)


def _sigmoid(x):
    return 1.0 / (1.0 + jnp.exp(-x))


def _softplus(x):
    return jnp.maximum(x, 0.0) + jnp.log(1.0 + jnp.exp(-jnp.abs(x)))


def _dot(a, b):
    return jnp.dot(a, b, preferred_element_type=F32)


def _dot_bf16(a, b):
    return jnp.dot(a.astype(BF16), b.astype(BF16), preferred_element_type=F32)


def _dot_hi(a, b):
    return jnp.dot(a, b, precision=HIGHEST, preferred_element_type=F32)


def _dot_nt(a, b, precision=None):
    return lax.dot_general(a, b, (((1,), (1,)), ((), ())), precision=precision, preferred_element_type=F32)


def _dot_tn(a, b, precision=None):
    return lax.dot_general(a, b, (((0,), (0,)), ((), ())), precision=precision, preferred_element_type=F32)


def _split_dot(x, ones_bf16):
    return _dot(x.astype(BF16), ones_bf16)


def _split3(x):
    hi = x.astype(BF16)
    r1 = x - hi.astype(F32)
    mid = r1.astype(BF16)
    return hi, mid, (r1 - mid.astype(F32)).astype(BF16)


def _tril(n, strict=False):
    r = lax.broadcasted_iota(jnp.int32, (n, n), 0)
    c = lax.broadcasted_iota(jnp.int32, (n, n), 1)
    return (r > c) if strict else (r >= c)


def _mod_kernel(c_ref, w_ref, b_ref, o_ref):
    c = c_ref[...]
    o_ref[...] = _dot_hi(c * _sigmoid(c), w_ref[...]) + b_ref[...]


def adaln_mod(c, w, b):
    d, n = w.shape
    tn = 1024
    c8 = jnp.broadcast_to(c, (SUBLANES, d))
    out = pl.pallas_call(
        _mod_kernel,
        grid=(n // tn,),
        in_specs=[pl.BlockSpec((SUBLANES, d), lambda j: (0, 0)),
                  pl.BlockSpec((d, tn), lambda j: (0, j)),
                  pl.BlockSpec((1, tn), lambda j: (0, j))],
        out_specs=pl.BlockSpec((SUBLANES, tn), lambda j: (0, j)),
        out_shape=jax.ShapeDtypeStruct((SUBLANES, n), F32),
        compiler_params=_cparams("arbitrary"),
    )(c8, w, b.reshape(1, n))
    return out[:1]


def _norm_mod(x, gain, sc, sh):
    y = x * lax.rsqrt(jnp.mean(x * x, axis=-1, keepdims=True) + EPS)
    return y * gain * (1.0 + sc) + sh


def _norm_kernel(x_ref, gain_ref, sc_ref, sh_ref, h_ref):
    h_ref[...] = _norm_mod(x_ref[...], gain_ref[...], sc_ref[...], sh_ref[...]).astype(h_ref.dtype)


def norm_mod(x, gain, sc, sh, tm=512):
    L, d = x.shape
    row = pl.BlockSpec((1, d), lambda i: (0, 0))
    return pl.pallas_call(
        _norm_kernel,
        grid=(L // tm,),
        in_specs=[pl.BlockSpec((tm, d), lambda i: (i, 0)), row, row, row],
        out_specs=pl.BlockSpec((tm, d), lambda i: (i, 0)),
        out_shape=jax.ShapeDtypeStruct((L, d), BF16),
        compiler_params=_cparams("arbitrary"),
    )(x, gain, sc, sh)


def _rw_pre_kernel(has_vres, *refs):
    if has_vres:
        (h_ref, w_ref, mu_ref, w0_ref, wup_ref, a0_ref, aup_ref, gup_ref, kk_ref, ka_ref, rk_ref, seg_ref,
         vfirst_ref, v0_ref, vup_ref,
         r_out, k_out, v_out, lw_out, a_out, b_out, g_out, bonus_out, ext_scr) = refs
    else:
        (h_ref, w_ref, mu_ref, w0_ref, wup_ref, a0_ref, aup_ref, gup_ref, kk_ref, ka_ref, rk_ref, seg_ref,
         r_out, k_out, v_out, lw_out, a_out, b_out, g_out, bonus_out, ext_scr) = refs
    T = h_ref.shape[0]

    @pl.when(pl.program_id(0) == 0)
    def _():
        ext_scr[0:SUBLANES, :] = jnp.zeros((SUBLANES, ext_scr.shape[1]), F32)

    cur = _dot(h_ref[...], w_ref[...])
    ext_scr[SUBLANES:SUBLANES + T, :] = cur
    prev = ext_scr[SUBLANES - 1:SUBLANES - 1 + T, :]
    ext_scr[0:SUBLANES, :] = cur[T - SUBLANES:T, :]
    s = cur + (prev - cur) * mu_ref[...]

    W = RW_WIDTH
    r = s[:, 0:W]
    k = s[:, W:2 * W]
    v = s[:, 2 * W:3 * W]
    w_lo = s[:, RW_OFF_WLO:RW_OFF_WLO + LANES]
    a_lo = s[:, RW_OFF_ALO:RW_OFF_ALO + LANES]
    g_lo = s[:, RW_OFF_GLO:RW_OFF_GLO + 2 * LANES]
    seg = seg_ref[...]

    wlog = -_softplus(-(w0_ref[...] + _dot_bf16(jnp.tanh(w_lo), wup_ref[...]))) - 0.5
    a = _sigmoid(a0_ref[...] + _dot_bf16(a_lo, aup_ref[...]))
    g = _dot_bf16(_sigmoid(g_lo), gup_ref[...])
    if has_vres:
        hv = s[:, RW_OFF_HV:RW_OFF_HV + LANES]
        v = v + (vfirst_ref[...].astype(F32) - v) * _sigmoid(v0_ref[...] + _dot_bf16(hv, vup_ref[...]))
    kk = k * kk_ref[...]
    kk = kk / jnp.maximum(jnp.sqrt(_split_dot(kk * kk, seg)), 1e-12)
    k = k * (1.0 + (a - 1.0) * ka_ref[...])
    r_out[...] = r.astype(r_out.dtype)
    k_out[...] = k.astype(k_out.dtype)
    v_out[...] = v.astype(v_out.dtype)
    lw_out[...] = -jnp.exp(wlog)
    a_out[...] = (-kk).astype(a_out.dtype)
    b_out[...] = (kk * a).astype(b_out.dtype)
    g_out[...] = g.astype(g_out.dtype)
    bonus_out[...] = (_split_dot(r * k * rk_ref[...], seg) * v).astype(bonus_out.dtype)


def rw_pre(h, w_rw, p, v_first, tm=512):
    L, D = h.shape
    has_vres = v_first is not None
    full = lambda a: pl.BlockSpec(a.shape, lambda i: (0,) * a.ndim)
    rows = lambda n: pl.BlockSpec((tm, n), lambda i: (i, 0))
    args = [h, w_rw, p["mu"], p["w0"], p["w_up"], p["a0"], p["a_up"], p["g_up"], p["k_k"], p["k_a"], p["r_k"],
            p["seg"]]
    specs = [rows(D)] + [full(a) for a in args[1:]]
    if has_vres:
        args += [v_first, p["v0"], p["v_up"]]
        specs += [rows(RW_WIDTH), full(p["v0"]), full(p["v_up"])]
    out_dtypes = [BF16, BF16, BF16, F32, BF16, BF16, BF16, BF16]
    return pl.pallas_call(
        functools.partial(_rw_pre_kernel, has_vres),
        grid=(L // tm,),
        in_specs=specs,
        out_specs=[rows(RW_WIDTH)] * 8,
        out_shape=[jax.ShapeDtypeStruct((L, RW_WIDTH), dt) for dt in out_dtypes],
        scratch_shapes=[pltpu.VMEM((tm + SUBLANES, RW_PAD_COLS), F32)],
        compiler_params=_cparams("arbitrary"),
    )(*args)


def _wkv_kernel(r_ref, lw_ref, k_ref, v_ref, a_ref, b_ref, y_ref, h_scr):
    C = RW_CHUNK
    C2 = 2 * C
    n_chunks = r_ref.shape[0] // C

    @pl.when(pl.program_id(1) == 0)
    def _():
        h_scr[...] = jnp.zeros_like(h_scr)

    is_a = lax.broadcasted_iota(jnp.int32, (C, LANES), 1) < RW_HEAD_DIM
    ri = lax.broadcasted_iota(jnp.int32, (C2, C2), 0)
    ci = lax.broadcasted_iota(jnp.int32, (C2, C2), 1)
    same = (ri >= C) == (ci >= C)
    strict = jnp.logical_and(same, ri > ci)
    incl = jnp.logical_and(same, ri >= ci)
    eye2 = (ri == ci).astype(F32)
    tri_c = _tril(C).astype(BF16)

    def two(x):
        return jnp.concatenate([jnp.where(is_a, x, 0.0), jnp.where(is_a, 0.0, x)], axis=0)

    chunks = range(n_chunks)
    at2, rt2, bh2, kh2, v2, n_ab, a_ak, m_rb, m_rk, p_end = ([] for _ in range(10))
    for c in chunks:
        sl = pl.ds(c * C, C)
        lw = lw_ref[sl, :]
        lw_hi = lw.astype(BF16)
        lw_lo = (lw - lw_hi.astype(F32)).astype(BF16)
        cs = _dot(tri_c, lw_hi) + _dot(tri_c, lw_lo)
        cs_end = cs[C - 1:C, :]
        e_neg = jnp.exp(-cs)
        e_end = jnp.exp(cs_end - cs)
        a = a_ref[sl, :].astype(F32)
        b = b_ref[sl, :].astype(F32)
        k = k_ref[sl, :].astype(F32)
        at2.append(two(a * jnp.exp(cs - lw)))
        rt2.append(two(r_ref[sl, :].astype(F32) * jnp.exp(cs)))
        bh2.append(two(b * e_end))
        kh2.append(two(k * e_end))
        v2.append(two(v_ref[sl, :].astype(F32)))
        p_end.append(jnp.exp(cs_end))
        quad = _dot_nt(jnp.concatenate([at2[c], rt2[c]], axis=0).astype(BF16),
                       jnp.concatenate([two(b * e_neg), two(k * e_neg)], axis=0).astype(BF16))
        n_ab.append(jnp.where(strict, quad[0:C2, 0:C2], 0.0))
        a_ak.append(jnp.where(strict, quad[0:C2, C2:], 0.0))
        m_rb.append(jnp.where(incl, quad[C2:, 0:C2], 0.0))
        m_rk.append(jnp.where(incl, quad[C2:, C2:], 0.0))
    t_inv = [eye2 + n for n in n_ab]
    n_pow = n_ab
    n_pow = [_dot_bf16(n, n) for n in n_pow]
    for _ in range(int(math.log2(C)) - 2):
        both = [_dot_bf16(jnp.concatenate([n, t], axis=0), n) for n, t in zip(n_pow, t_inv)]
        t_inv = [t + b[C2:, :] for t, b in zip(t_inv, both)]
        n_pow = [b[:C2, :] for b in both]
    t_inv = [t + _dot_bf16(t, n) for t, n in zip(t_inv, n_pow)]
    akv = [_dot_bf16(a_ak[c], v2[c]) for c in chunks]
    au = [_dot_bf16(t_inv[c], jnp.concatenate([at2[c], akv[c]], axis=1)) for c in chunks]
    mau = [_dot_bf16(m_rb[c], au[c]) for c in chunks]
    mkv = [_dot_bf16(m_rk[c], v2[c]) for c in chunks]
    gu = [_dot_tn(bh2[c].astype(BF16), au[c].astype(BF16)) for c in chunks]
    kv = [_dot_tn(kh2[c].astype(BF16), v2[c].astype(BF16)) for c in chunks]
    inv_n = 1.0 / RW_HEAD_DIM
    h = h_scr[...]
    for c in chunks:
        rh2 = rt2[c] + mau[c][:, :LANES]
        g_mat = eye2 * p_end[c] + gu[c][:, :LANES]
        yh = _dot_bf16(jnp.concatenate([rh2, g_mat], axis=0), h)
        h = yh[C2:, :] + gu[c][:, LANES:] + kv[c]
        y2 = yh[:C2, :] + mau[c][:, LANES:] + mkv[c]
        y = y2[:C, :] + y2[C:, :]
        s_all = jnp.sum(y, axis=-1, keepdims=True)
        s_a = jnp.sum(jnp.where(is_a, y, 0.0), axis=-1, keepdims=True)
        yc = y - jnp.where(is_a, s_a, s_all - s_a) * inv_n
        sq = yc * yc
        q_all = jnp.sum(sq, axis=-1, keepdims=True)
        q_a = jnp.sum(jnp.where(is_a, sq, 0.0), axis=-1, keepdims=True)
        yn = yc * lax.rsqrt(jnp.where(is_a, q_a, q_all - q_a) * inv_n + GN_EPS)
        y_ref[pl.ds(c * C, C), :] = yn.astype(y_ref.dtype)
    h_scr[...] = h


def wkv7(r, lw, k, v, a, b, rows=1024):
    L, W = r.shape
    spec = pl.BlockSpec((rows, LANES), lambda p, i: (i, p))
    return pl.pallas_call(
        _wkv_kernel,
        grid=(W // LANES, L // rows),
        in_specs=[spec] * 6,
        out_specs=spec,
        out_shape=jax.ShapeDtypeStruct((L, W), BF16),
        scratch_shapes=[pltpu.VMEM((LANES, LANES), F32)],
        compiler_params=_cparams("arbitrary", "arbitrary"),
    )(r, lw, k, v, a, b)


def _ssd_kernel(h_ref, w_ref, cw_ref, cb_ref, bias_ref, A_ref, exp_ref, dskip_ref, nw_ref, o_ref,
                ext_scr, xbc_scr, s_scr):
    Q = SSD_CHUNK
    GW = SSD_GROUP_WIDTH
    NS = SSD_STATE
    R = h_ref.shape[0]

    @pl.when(pl.program_id(0) == 0)
    def _():
        s_scr[...] = jnp.zeros_like(s_scr)
        ext_scr[0:SUBLANES, :] = jnp.zeros((SUBLANES, ext_scr.shape[1]), F32)

    cols = _dot(h_ref[...], w_ref[...])
    ext_scr[SUBLANES:SUBLANES + R, :] = cols[:, 0:SSD_XBC]
    acc = cols[:, 0:SSD_XBC] * cw_ref[SSD_CONV - 1:SSD_CONV, :] + cb_ref[...]
    for j in range(1, SSD_CONV):
        acc = acc + ext_scr[SUBLANES - j:SUBLANES - j + R, :] * cw_ref[SSD_CONV - 1 - j:SSD_CONV - j, :]
    ext_scr[0:SUBLANES, :] = ext_scr[R:R + SUBLANES, :]
    xbc_scr[...] = acc * _sigmoid(acc)

    incl = _tril(Q)
    tri = incl.astype(BF16)
    triu = jnp.logical_not(_tril(Q, strict=True)).astype(BF16)
    expand = exp_ref[...]
    first_half = lax.broadcasted_iota(jnp.int32, (Q, LANES), 1) < SSD_HEAD_DIM
    heads_per_group = SSD_HEADS // SSD_GROUPS
    for c in range(R // Q):
        rows = slice(c * Q, (c + 1) * Q)
        xs = xbc_scr[rows, 0:SSD_WIDTH]
        z = cols[rows, SSD_XBC:SSD_XBC + SSD_WIDTH]
        dt = _softplus(cols[rows, SSD_XBC + SSD_WIDTH:] + bias_ref[...])
        a3 = _split3(dt * A_ref[...])
        a_cum = sum(_dot(tri, t) for t in a3)
        a_cumT = sum(_dot_tn(t, triu) for t in a3)
        a_end = a_cum[Q - 1:Q, :]
        dt_x = _dot(dt.astype(BF16), expand)
        eac_x = _dot(jnp.exp(a_cum).astype(BF16), expand)
        dte_x = _dot(jnp.exp(a_end - a_cum).astype(BF16), expand)
        cd_x = sum(_dot(t, expand) for t in _split3(jnp.broadcast_to(jnp.exp(a_end), (SUBLANES, LANES))))[0:1, :]
        xdt = xs * dt_x
        for g in range(SSD_GROUPS):
            Bg = xbc_scr[rows, SSD_WIDTH + g * NS:SSD_WIDTH + (g + 1) * NS].astype(BF16)
            Cg = xbc_scr[rows, SSD_WIDTH + (SSD_GROUPS + g) * NS:SSD_WIDTH + (SSD_GROUPS + g + 1) * NS].astype(BF16)
            cb = _dot_nt(Cg, Bg)
            gs = slice(g * GW, (g + 1) * GW)
            s_prev = s_scr[:, gs]
            y_g = _dot_bf16(Cg, s_prev) * eac_x[:, gs]
            pieces = []
            for pr in range(heads_per_group // 2):
                ps = slice(g * GW + pr * LANES, g * GW + (pr + 1) * LANES)
                xdt_p = xdt[:, ps]
                acc_p = None
                for e in range(2):
                    hd = g * heads_per_group + pr * 2 + e
                    seg = a_cum[:, hd:hd + 1] - a_cumT[hd:hd + 1, :]
                    m = cb * jnp.exp(jnp.where(incl, seg, -jnp.inf))
                    xm = jnp.where(first_half if e == 0 else jnp.logical_not(first_half), xdt_p, 0.0)
                    t = _dot_bf16(m, xm)
                    acc_p = t if acc_p is None else acc_p + t
                pieces.append(acc_p)
            y_g = y_g + jnp.concatenate(pieces, axis=1)
            s_scr[:, gs] = s_prev * cd_x[:, gs] +_dot_tn(Bg, (dte_x[:, gs] * xdt[:, gs]).astype(BF16))
            y_g = y_g + dskip_ref[:, gs] * xs[:, gs]
            zg = z[:, gs]
            y_g = y_g * (zg * _sigmoid(zg))
            y_g = y_g * lax.rsqrt(jnp.mean(y_g * y_g, axis=-1, keepdims=True) + EPS)
            o_ref[rows, gs] = (y_g * nw_ref[:, gs]).astype(o_ref.dtype)


def ssd_mixer(h, w_ssd, p, rows=512):
    L, D = h.shape
    N = w_ssd.shape[1]
    full = lambda a: pl.BlockSpec(a.shape, lambda i: (0,) * a.ndim)
    small = [p["conv_w"], p["conv_b"], p["dt_bias"], p["A"], p["expand"], p["d_skip"], p["norm_w"]]
    return pl.pallas_call(
        _ssd_kernel,
        grid=(L // rows,),
        in_specs=[pl.BlockSpec((rows, D), lambda i: (i, 0)), full(w_ssd)] + [full(a) for a in small],
        out_specs=pl.BlockSpec((rows, SSD_WIDTH), lambda i: (i, 0)),
        out_shape=jax.ShapeDtypeStruct((L, SSD_WIDTH), BF16),
        scratch_shapes=[pltpu.VMEM((rows + SUBLANES, SSD_XBC), F32), pltpu.VMEM((rows, SSD_XBC), F32),
                        pltpu.VMEM((SSD_STATE, SSD_WIDTH), F32)],
        compiler_params=_cparams("arbitrary"),
    )(h, w_ssd, *small)


def _fox_pre_kernel(h_ref, w_ref, fb_ref, qg_ref, kg_ref, seg_ref, q_out, k_out, v_out, f_out, carry_scr):
    T = h_ref.shape[0]
    W = FOX_WIDTH

    @pl.when(pl.program_id(0) == 0)
    def _():
        carry_scr[...] = jnp.zeros_like(carry_scr)

    cols = _dot(h_ref[...], w_ref[...])
    seg = seg_ref[...]
    q = cols[:, 0:W]
    k = cols[:, W:2 * W]
    inv_d = 1.0 / FOX_HEAD_DIM
    qn = q * lax.rsqrt(_split_dot(q * q, seg) * inv_d + EPS) * qg_ref[...]
    kn = k * lax.rsqrt(_split_dot(k * k, seg) * inv_d + EPS) * kg_ref[...]
    q_out[...] = qn.astype(q_out.dtype)
    k_out[...] = kn.astype(k_out.dtype)
    v_out[...] = cols[:, 2 * W:3 * W].astype(v_out.dtype)
    f = cols[:, 3 * W:3 * W + LANES] + fb_ref[...]
    ls = -_softplus(-f)
    cum = _dot_hi(_tril(T).astype(F32), ls) + carry_scr[0:1, :]
    f_out[...] = cum * LOG2E
    carry_scr[...] = jnp.broadcast_to(cum[T - 1:T, :], carry_scr.shape)


def fox_pre(h, w_fox, p, tm=512):
    L, D = h.shape
    full = lambda a: pl.BlockSpec(a.shape, lambda i: (0,) * a.ndim)
    rows = lambda n: pl.BlockSpec((tm, n), lambda i: (i, 0))
    small = [w_fox, p["f_bias"], p["q_gain"], p["k_gain"], p["seg"]]
    return pl.pallas_call(
        _fox_pre_kernel,
        grid=(L // tm,),
        in_specs=[rows(D)] + [full(a) for a in small],
        out_specs=[rows(FOX_WIDTH)] * 3 + [rows(LANES)],
        out_shape=[jax.ShapeDtypeStruct((L, FOX_WIDTH), BF16)] * 3 + [jax.ShapeDtypeStruct((L, LANES), F32)],
        scratch_shapes=[pltpu.VMEM((SUBLANES, LANES), F32)],
        compiler_params=_cparams("arbitrary"),
    )(h, *small)


def _fox_kernel(lo_ref, q_ref, k_ref, v_ref, fk_ref, o_ref, m_scr, acc_scr, s0_scr, s1_scr):
    tq = q_ref.shape[0]
    tk = FOX_TK
    pair = pl.program_id(0)
    qi = pl.program_id(1)
    is_a = lax.broadcasted_iota(jnp.int32, (tq, LANES), 1) < FOX_HEAD_DIM
    q = q_ref[...]
    zero = jnp.zeros_like(q)
    qs = (jnp.where(is_a, q, zero), jnp.where(is_a, zero, q))
    m_scr[...] = jnp.full(m_scr.shape, -jnp.inf, F32)
    key_is_a = lax.broadcasted_iota(jnp.int32, (tk, LANES), 1) < FOX_HEAD_DIM
    acc_scr[...] = jnp.zeros_like(acc_scr)
    rel = lax.broadcasted_iota(jnp.int32, (tq, tk), 0) - lax.broadcasted_iota(jnp.int32, (tq, tk), 1)
    last = (qi * tq) // tk

    def fill(s_ref, kb, diagonal=False):
        ks = pl.ds(pl.multiple_of(kb * tk, tk), tk)
        k_blk = k_ref[ks, :]
        for e in range(2):
            s = _dot_nt(qs[e], k_blk) - fk_ref[pl.ds(2 * pair + e, 1), ks]
            if diagonal:
                s = jnp.where(rel >= kb * tk - qi * tq, s, -jnp.inf)
            s_ref[e] = s

    def consume(s_ref, kb):
        v_blk = v_ref[pl.ds(pl.multiple_of(kb * tk, tk), tk), :]
        one = jnp.ones_like(v_blk)
        v_one = (jnp.where(key_is_a, v_blk, one), jnp.where(key_is_a, one, v_blk))
        for e in range(2):
            s = s_ref[e]
            m_old = m_scr[e]
            m_new = jnp.maximum(m_old, jnp.max(s, axis=-1, keepdims=True))
            pexp = jnp.exp2(s - m_new)
            acc_scr[e] = jnp.exp2(m_old - m_new) * acc_scr[e] + _dot(pexp.astype(BF16), v_one[e])
            m_scr[e] = m_new

    n = last - jnp.minimum(lo_ref[2 * pair, qi], lo_ref[2 * pair + 1, qi])
    fill(s0_scr, last, diagonal=True)

    def two_steps(u, carry):
        kb = last - 2 * u
        fill(s1_scr, kb - 1)
        consume(s0_scr, kb)
        fill(s0_scr, kb - 2)
        consume(s1_scr, kb - 1)
        return carry

    lax.fori_loop(0, n // 2, two_steps, 0)

    @pl.when(n % 2 == 1)
    def _():
        fill(s1_scr, last - n)
        consume(s0_scr, last - n + 1)
        consume(s1_scr, last - n)

    @pl.when(n % 2 == 0)
    def _():
        consume(s0_scr, last - n)

    acc_a = acc_scr[0]
    acc_b = acc_scr[1]
    half = FOX_HEAD_DIM
    o_ref[...] = jnp.where(is_a, acc_a / pltpu.roll(acc_a, half, 1),
                           acc_b / pltpu.roll(acc_b, half, 1)).astype(o_ref.dtype)


def fox_attention(q, k, v, FT, lo):
    L, W = q.shape
    H = FT.shape[0]
    tq = FOX_TQ
    grid_spec = pltpu.PrefetchScalarGridSpec(
        num_scalar_prefetch=1,
        grid=(W // LANES, L // tq),
        in_specs=[pl.BlockSpec((tq, LANES), lambda p, i, lo_r: (i, p)),
                  pl.BlockSpec((L, LANES), lambda p, i, lo_r: (0, p)),
                  pl.BlockSpec((L, LANES), lambda p, i, lo_r: (0, p)),
                  pl.BlockSpec((H, L), lambda p, i, lo_r: (0, 0))],
        out_specs=pl.BlockSpec((tq, LANES), lambda p, i, lo_r: (i, p)),
        scratch_shapes=[pltpu.VMEM((2, tq, 1), F32), pltpu.VMEM((2, tq, LANES), F32),
                        pltpu.VMEM((2, tq, FOX_TK), F32), pltpu.VMEM((2, tq, FOX_TK), F32)],
    )
    return pl.pallas_call(
        _fox_kernel,
        grid_spec=grid_spec,
        out_shape=jax.ShapeDtypeStruct((L, W), BF16),
        compiler_params=_cparams("arbitrary", "arbitrary"),
    )(lo, q, k, v, FT)


def fox_first_block(FT, bound2):
    f_first = FT[:, ::FOX_TQ]
    f_last = FT[:, FOX_TK - 1::FOX_TK]
    gap = f_first[:, :, None] - f_last[:, None, :] + 2.0 * bound2
    last = (jnp.arange(f_first.shape[1]) * FOX_TQ) // FOX_TK
    needed = jnp.logical_or(gap >= -152.0, jnp.arange(f_last.shape[1])[None, None, :] >= last[None, :, None])
    return jnp.argmax(needed, axis=-1).astype(jnp.int32)


def _top2_combine(h, wr):
    T = h.shape[0]
    lane = lax.broadcasted_iota(jnp.int32, (T, LANES), 1)
    h_hi = h.astype(BF16)
    h_lo = (h - h_hi.astype(F32)).astype(BF16)
    w_hi = wr.astype(BF16)
    w_lo = (wr - w_hi.astype(F32)).astype(BF16)
    logits = _dot(h_hi, w_hi) + (_dot(h_hi, w_lo) + _dot(h_lo, w_hi))
    logits = jnp.where(lane < N_EXPERTS, logits, -jnp.inf)
    m1 = jnp.max(logits, axis=-1, keepdims=True)
    i1 = jnp.min(jnp.where(logits == m1, lane, LANES), axis=-1, keepdims=True)
    rest = jnp.where(lane == i1, -jnp.inf, logits)
    m2 = jnp.max(rest, axis=-1, keepdims=True)
    i2 = jnp.min(jnp.where(rest == m2, lane, LANES), axis=-1, keepdims=True)
    e2 = jnp.exp(m2 - m1)
    return jnp.where(lane == i1, 1.0 / (1.0 + e2), 0.0) + jnp.where(lane == i2, e2 / (1.0 + e2), 0.0)


def _merge_kernel(with_router, *refs):
    (x_ref, hm_ref, yn_ref, bonus_ref, g_ref, yssd_ref, yfox_ref, wgate_ref, gb_ref, lnw_ref, lnb_ref,
     prw_ref, pssd_ref, pfox_ref, wout_ref, gm_ref, gain_ref, sc_ref, sh_ref) = refs[:19]
    D = D_MODEL
    hm = hm_ref[...]
    y_rw = (yn_ref[...] * lnw_ref[...] + lnb_ref[...] + bonus_ref[...]) * g_ref[...]
    merged = None
    for b, (y_b, proj_ref) in enumerate(((y_rw, prw_ref), (yssd_ref[...], pssd_ref), (yfox_ref[...], pfox_ref))):
        cols = slice(b * D, (b + 1) * D)
        gate = _sigmoid(_dot(hm, wgate_ref[:, cols]) + gb_ref[:, cols])
        term = gate * _dot_bf16(y_b, proj_ref[...])
        merged = term if merged is None else merged + term
    x_new = x_ref[...] + gm_ref[...] * _dot_bf16(merged, wout_ref[...])
    hf = _norm_mod(x_new, gain_ref[...], sc_ref[...], sh_ref[...])
    if with_router:
        wr_ref, o_ref, h_ref, comb_ref = refs[19:]
        comb_ref[...] = _top2_combine(hf, wr_ref[...])
    else:
        o_ref, h_ref = refs[19:]
    o_ref[...] = x_new
    h_ref[...] = hf.astype(h_ref.dtype)


def merge(x, hm, yn, bonus, g, y_ssd, y_fox, p, router=None, tm=256):
    L, D = x.shape
    full = lambda a: pl.BlockSpec(a.shape, lambda i: (0,) * a.ndim)
    rows = lambda n: pl.BlockSpec((tm, n), lambda i: (i, 0))
    small = [p["w_gate"], p["gate_b"], p["lnx_w"], p["lnx_b"], p["proj_rw"], p["proj_ssd"], p["proj_fox"],
             p["w_out"], p["g_m"], p["norm_gain"], p["sc_f"], p["sh_f"]]
    out_specs = [rows(D), rows(D)]
    out_shape = [jax.ShapeDtypeStruct((L, D), F32), jax.ShapeDtypeStruct((L, D), BF16)]
    if router is not None:
        small.append(router)
        out_specs.append(rows(LANES))
        out_shape.append(jax.ShapeDtypeStruct((L, LANES), F32))
    return pl.pallas_call(
        functools.partial(_merge_kernel, router is not None),
        grid=(L // tm,),
        in_specs=[rows(D), rows(D), rows(RW_WIDTH), rows(RW_WIDTH), rows(RW_WIDTH), rows(SSD_WIDTH),
                  rows(FOX_WIDTH)] + [full(a) for a in small],
        out_specs=out_specs,
        out_shape=out_shape,
        compiler_params=_cparams("arbitrary"),
    )(x, hm, yn, bonus, g, y_ssd, y_fox, *small)


def _ffn_kernel(with_next, *refs):
    h_ref, x_ref, wg_ref, wu_ref, wd_ref, gf_ref = refs[:6]
    j = pl.program_id(1)
    acc_scr = refs[-1]

    @pl.when(j == 0)
    def _():
        acc_scr[...] = jnp.zeros_like(acc_scr)

    h = h_ref[...]
    gte = _dot(h, wg_ref[...])
    up = _dot(h, wu_ref[...])
    act = gte * _sigmoid(gte) * up
    acc_scr[...] += _dot(act.astype(BF16), wd_ref[...])

    @pl.when(j == pl.num_programs(1) - 1)
    def _():
        x_new = x_ref[...] + gf_ref[...] * acc_scr[...]
        if with_next:
            gain_ref, sc_ref, sh_ref, o_ref, hn_ref = refs[6:11]
            hn_ref[...] = _norm_mod(x_new, gain_ref[...], sc_ref[...], sh_ref[...]).astype(hn_ref.dtype)
        else:
            o_ref = refs[6]
        o_ref[...] = x_new


def ffn_dense(h, x, w_gu, w_down, g_f, next_norm=None, tm=1024, tf=256):
    L, D = x.shape
    Fh = w_down.shape[0]
    nf = Fh // tf
    row = pl.BlockSpec((1, D), lambda i, j: (0, 0))
    tile = pl.BlockSpec((tm, D), lambda i, j: (i, 0))
    args = [h, x, w_gu, w_gu, w_down, g_f]
    in_specs = [tile, tile,
                pl.BlockSpec((D, tf), lambda i, j: (0, j)),
                pl.BlockSpec((D, tf), lambda i, j: (0, j + nf)),
                pl.BlockSpec((tf, D), lambda i, j: (j, 0)), row]
    out_specs = [tile]
    out_shape = [jax.ShapeDtypeStruct((L, D), F32)]
    if next_norm is not None:
        args += list(next_norm)
        in_specs += [row, row, row]
        out_specs.append(tile)
        out_shape.append(jax.ShapeDtypeStruct((L, D), BF16))
    out = pl.pallas_call(
        functools.partial(_ffn_kernel, next_norm is not None),
        grid=(L // tm, nf),
        in_specs=in_specs,
        out_specs=out_specs,
        out_shape=out_shape,
        scratch_shapes=[pltpu.VMEM((tm, D), F32)],
        compiler_params=_cparams("arbitrary", "arbitrary"),
    )(*args)
    return out if next_norm is not None else (out[0], None)


def _moe_slots(slot_scr, comb_ref):
    slotval = slot_scr[...]
    comb = comb_ref[...]
    s1 = jnp.max(slotval, axis=-1, keepdims=True)
    first = slotval == s1
    c1 = jnp.sum(jnp.where(first, comb, 0.0), axis=-1, keepdims=True)
    rest = jnp.where(first, -1.0, slotval)
    s2 = jnp.max(rest, axis=-1, keepdims=True)
    c2 = jnp.sum(jnp.where(jnp.logical_and(rest == s2, rest >= 0.0), comb, 0.0), axis=-1, keepdims=True)
    return s1, s2, c1, c2


def _moe_kernel(nsb_ref, off_ref, h_ref, x_ref, comb_ref, wg_ref, wu_ref, wd_ref, gf_ref, o_ref,
                slot_scr, cslot_scr, xy_scr, acc_scr):
    i = pl.program_id(0)
    e = pl.program_id(1)
    j = pl.program_id(2)
    last_j = pl.num_programs(2) - 1
    T = h_ref.shape[0]
    S = xy_scr.shape[0]
    SB = MOE_SB

    @pl.when(jnp.logical_and(e == 0, j == 0))
    def _():
        lane = lax.broadcasted_iota(jnp.int32, (LANES, LANES), 1)
        off = jnp.zeros((LANES, LANES), F32)
        for ex in range(N_EXPERTS):
            off = jnp.where(lane == ex, off_ref[i, ex].astype(F32), off)
        before = _tril(LANES, strict=True).astype(BF16)
        ones = jnp.ones((LANES, LANES), BF16)
        for rb in range(T // LANES):
            rows = slice(rb * LANES, (rb + 1) * LANES)
            sel = comb_ref[rows, :] > 0.0
            sel_b = jnp.where(sel, 1.0, 0.0).astype(BF16)
            slot_scr[rows, :] = jnp.where(sel, _dot(before, sel_b) + off, -1.0)
            off = off + _dot(ones, sel_b)
        s1, s2, c1, c2 = _moe_slots(slot_scr, comb_ref)
        used = off_ref[i, N_EXPERTS - 1] + nsb_ref[i, N_EXPERTS - 1] * SB
        for cb in range(S // MOE_CB):
            chunk = slice(cb * MOE_CB, (cb + 1) * MOE_CB)

            @pl.when(used > cb * MOE_CB)
            def _():
                slot = (lax.broadcasted_iota(jnp.int32, (T, MOE_CB), 1) + cb * MOE_CB).astype(F32)
                w = jnp.where(s1 == slot, c1, 0.0) + jnp.where(s2 == slot, c2, 0.0)
                pt = jnp.where(w > 0.0, 1.0, 0.0).astype(BF16)
                xy_scr[chunk, :] = _dot_tn(pt, h_ref[...]).astype(BF16)
                w_hi = w.astype(BF16)
                w_lo = (w - w_hi.astype(F32)).astype(BF16)
                ones_t = jnp.ones((T, LANES), BF16)
                cslot_scr[chunk, :] = _dot_tn(w_hi, ones_t) + _dot_tn(w_lo, ones_t)

            @pl.when(used <= cb * MOE_CB)
            def _():
                xy_scr[chunk, :] = jnp.zeros((MOE_CB, xy_scr.shape[1]), BF16)
                cslot_scr[chunk, :] = jnp.zeros((MOE_CB, LANES), F32)

    base = off_ref[i, e]
    nb = nsb_ref[i, e]

    def ffn_block(r0, n_rows):
        rows = pl.ds(pl.multiple_of(base + r0, SB), n_rows)
        arows = pl.ds(pl.multiple_of(r0, SB), n_rows)
        xb = xy_scr[rows, :]
        gte = _dot(xb, wg_ref[0])
        up = _dot(xb, wu_ref[0])
        part = _dot((gte * _sigmoid(gte) * up).astype(BF16), wd_ref[0])

        @pl.when(j == 0)
        def _():
            acc_scr[arows, :] = part

        @pl.when(jnp.logical_and(j > 0, j < last_j))
        def _():
            acc_scr[arows, :] += part

        @pl.when(j == last_j)
        def _():
            xy_scr[rows, :] = ((acc_scr[arows, :] + part) * cslot_scr[rows, 0:1]).astype(BF16)

    def quad_body(p, carry):
        ffn_block(p * (4 * SB), 4 * SB)
        return carry

    lax.fori_loop(0, nb // 4, quad_body, 0)
    for rest in range(1, 4):
        @pl.when(nb % 4 == rest)
        def _():
            ffn_block((nb - rest) * SB, rest * SB)

    @pl.when(jnp.logical_and(e == pl.num_programs(1) - 1, j == last_j))
    def _():
        s1, s2, _, _ = _moe_slots(slot_scr, comb_ref)
        total = jnp.zeros((T, xy_scr.shape[1]), F32)
        for cb in range(S // MOE_CB):
            slot = (lax.broadcasted_iota(jnp.int32, (T, MOE_CB), 1) + cb * MOE_CB).astype(F32)
            pt = jnp.where(jnp.logical_or(s1 == slot, s2 == slot), 1.0, 0.0).astype(BF16)
            total = total + _dot(pt, xy_scr[cb * MOE_CB:(cb + 1) * MOE_CB, :])
        o_ref[...] = x_ref[...] + gf_ref[...] * total


def ffn_moe(h, x, comb, w_gu, w_down, g_f, tm=1024, tf=896):
    L, D = x.shape
    E, Fh, _ = w_down.shape
    nf = Fh // tf
    assert nf >= 2 and E == N_EXPERTS
    nt = L // tm
    counts = jnp.sum((comb[:, :E] > 0.0).reshape(nt, tm, E), axis=1, dtype=jnp.int32)
    nsb = (counts + (MOE_SB - 1)) // MOE_SB
    off = (jnp.cumsum(nsb, axis=1) - nsb) * MOE_SB
    idx = lambda f: (lambda i, e, j, n, o: f(i, e, j))
    grid_spec = pltpu.PrefetchScalarGridSpec(
        num_scalar_prefetch=2,
        grid=(nt, E, nf),
        in_specs=[pl.BlockSpec((tm, D), idx(lambda i, e, j: (i, 0))),
                  pl.BlockSpec((tm, D), idx(lambda i, e, j: (i, 0))),
                  pl.BlockSpec((tm, LANES), idx(lambda i, e, j: (i, 0))),
                  pl.BlockSpec((1, D, tf), idx(lambda i, e, j: (e, 0, j))),
                  pl.BlockSpec((1, D, tf), idx(lambda i, e, j: (e, 0, j + nf))),
                  pl.BlockSpec((1, tf, D), idx(lambda i, e, j: (e, j, 0))),
                  pl.BlockSpec((1, D), idx(lambda i, e, j: (0, 0)))],
        out_specs=pl.BlockSpec((tm, D), idx(lambda i, e, j: (i, 0))),
        scratch_shapes=[pltpu.VMEM((tm, LANES), F32),
                        pltpu.VMEM((TOP_K * tm + E * MOE_SB, LANES), F32),
                        pltpu.VMEM((TOP_K * tm + E * MOE_SB, D), BF16),
                        pltpu.VMEM((tm, D), F32)],
    )
    return pl.pallas_call(
        _moe_kernel,
        grid_spec=grid_spec,
        out_shape=jax.ShapeDtypeStruct((L, D), F32),
        compiler_params=_cparams("arbitrary", "arbitrary", "arbitrary"),
    )(nsb, off, h, x, comb, w_gu, w_gu, w_down, g_f)


def _seg_matrix(width, head_dim):
    idx = jnp.arange(width) // head_dim
    return (idx[:, None] == idx[None, :]).astype(BF16)


def _pad_cols(w, n):
    return jnp.pad(w, ((0, 0), (0, n - w.shape[1])))


def _pad_rows(w, n):
    return jnp.pad(w, ((0, n - w.shape[0]), (0, 0)))


def rwkv_branch(h, w_rw, p, v_first):
    r, k, v, lw, a, b, g, bonus = rw_pre(h, w_rw, p, v_first)
    return wkv7(r, lw, k, v, a, b), bonus, g, v


def fox_branch(h, w_fox, p):
    q, k, v, F = fox_pre(h, w_fox, p)
    FT = jnp.transpose(F[:, :FOX_HEADS])
    lo = fox_first_block(FT, p["bound2"])
    return fox_attention(q, k, v, FT, lo)


def kernel(x, c, ada_w, ada_b, norm_mix, norm_ffn, w_in, rw_mu, rw_w0, rw_w_up, rw_a0, rw_a_up, rw_g_up, rw_k_k, rw_k_a, rw_r_k, rw_lnx_w, rw_lnx_b, rw_v0, rw_v_down, rw_v_up, ssd_conv_w, ssd_conv_b, ssd_dt_bias, ssd_a_log, ssd_d, ssd_norm, fox_f_bias, fox_q_gain, fox_k_gain, gate_b, proj_rw, proj_ssd, proj_fox, w_out, ffn_w_gu, ffn_w_down, moe_router, moe_w_gu, moe_w_down):
    depth = w_in.shape[0]
    D = D_MODEL
    xs = x[0]
    row = lambda t: t.reshape(1, -1).astype(F32)
    seg64 = _seg_matrix(RW_WIDTH, RW_HEAD_DIM)
    rw_cols = 3 * RW_WIDTH + RW_DECAY_LORA + RW_AAA_LORA + RW_GATE_LORA
    ssd_cols = SSD_WIDTH + SSD_XBC + SSD_HEADS
    fox_cols = 3 * FOX_WIDTH + FOX_HEADS
    expand = (jnp.arange(LANES)[:, None] == (jnp.arange(SSD_WIDTH) // SSD_HEAD_DIM)[None, :]).astype(BF16)
    v_first = None
    mods = []
    for l in range(depth):
        mod = adaln_mod(c, ada_w[l], ada_b[l])
        mods.append([mod[:, i * D:(i + 1) * D] for i in range(6)])
    h = None
    for l in range(depth):
        sh_m, sc_m, g_m, sh_f, sc_f, g_f = mods[l]
        if h is None:
            h = norm_mod(xs, row(norm_mix[l]), sc_m, sh_m)

        wl = w_in[l]
        o = 0
        w_r = wl[:, o:o + rw_cols]; o += rw_cols
        w_s = wl[:, o:o + ssd_cols]; o += ssd_cols
        w_f = wl[:, o:o + fox_cols]; o += fox_cols
        w_g = wl[:, o:]
        W3 = 3 * RW_WIDTH
        o_a = W3 + RW_DECAY_LORA
        o_g = o_a + RW_AAA_LORA

        def rw_layout(t, hv):
            parts = [t[:, :W3], _pad_cols(t[:, W3:o_a], LANES), _pad_cols(t[:, o_a:o_g], LANES),
                     _pad_cols(t[:, o_g:], 2 * LANES), _pad_cols(hv, LANES)]
            return jnp.concatenate(parts, axis=1)

        if l == 0:
            hv_w = jnp.zeros((D, RW_VRES_LORA), F32)
        else:
            hv_w = rw_v_down[l - 1]
        w_rw = rw_layout(w_r, hv_w).astype(BF16)
        mu = rw_layout(rw_mu[l].reshape(1, -1), jnp.zeros((1, RW_VRES_LORA), F32))
        rw_p = dict(mu=mu, w0=row(rw_w0[l]), w_up=_pad_rows(rw_w_up[l], LANES).astype(BF16), a0=row(rw_a0[l]),
                    a_up=_pad_rows(rw_a_up[l], LANES).astype(BF16),
                    g_up=_pad_rows(rw_g_up[l], 2 * LANES).astype(BF16),
                    k_k=row(rw_k_k[l]), k_a=row(rw_k_a[l]), r_k=row(rw_r_k[l]), seg=seg64)
        if l > 0:
            rw_p.update(v0=row(rw_v0[l - 1]), v_up=_pad_rows(rw_v_up[l - 1], LANES).astype(BF16))
        yn, bonus, g_rw, v_cur = rwkv_branch(h, w_rw, rw_p, v_first)
        if l == 0:
            v_first = v_cur

        w_ssd = jnp.concatenate([w_s[:, SSD_WIDTH:SSD_WIDTH + SSD_XBC], w_s[:, :SSD_WIDTH],
                                 _pad_cols(w_s[:, SSD_WIDTH + SSD_XBC:], LANES)], axis=1).astype(BF16)
        A = -jnp.exp(ssd_a_log[l].astype(F32))
        ssd_p = dict(conv_w=ssd_conv_w[l], conv_b=row(ssd_conv_b[l]),
                     dt_bias=_pad_cols(row(ssd_dt_bias[l]), LANES),
                     A=_pad_cols(row(A), LANES), expand=expand,
                     d_skip=row(jnp.repeat(ssd_d[l], SSD_HEAD_DIM)), norm_w=row(ssd_norm[l]))
        y_ssd = ssd_mixer(h, w_ssd, ssd_p)

        w_fox = jnp.concatenate([w_f[:, :3 * FOX_WIDTH], _pad_cols(w_f[:, 3 * FOX_WIDTH:], LANES)],
                                axis=1).astype(BF16)
        bound2 = (1.02 * FOX_HEAD_DIM ** 0.5 * LOG2E) * jnp.max(jnp.abs(fox_q_gain[l])) * jnp.max(jnp.abs(fox_k_gain[l]))
        fox_p = dict(f_bias=_pad_cols(row(fox_f_bias[l]), LANES),
                     q_gain=row(jnp.tile(fox_q_gain[l], FOX_HEADS)) * (FOX_HEAD_DIM ** -0.5 * LOG2E),
                     k_gain=row(jnp.tile(fox_k_gain[l], FOX_HEADS)), seg=seg64, bound2=bound2)
        y_fox = fox_branch(h, w_fox, fox_p)

        mp = dict(w_gate=w_g.astype(BF16), gate_b=row(gate_b[l]), lnx_w=row(rw_lnx_w[l]), lnx_b=row(rw_lnx_b[l]),
                  proj_rw=proj_rw[l].astype(BF16), proj_ssd=proj_ssd[l].astype(BF16),
                  proj_fox=proj_fox[l].astype(BF16), w_out=w_out[l].astype(BF16), g_m=g_m,
                  norm_gain=row(norm_ffn[l]), sc_f=sc_f, sh_f=sh_f)

        if l % 2 == 0:
            xs, hf = merge(xs, h, yn, bonus, g_rw, y_ssd, y_fox, mp)
            next_norm = None
            if l + 1 < depth:
                next_norm = (row(norm_mix[l + 1]), mods[l + 1][1], mods[l + 1][0])
            xs, h = ffn_dense(hf, xs, ffn_w_gu[l // 2].astype(BF16), ffn_w_down[l // 2].astype(BF16), g_f,
                              next_norm)
        else:
            xs, hf, comb = merge(xs, h, yn, bonus, g_rw, y_ssd, y_fox, mp,
                                 router=_pad_cols(moe_router[l // 2], LANES))
            xs = ffn_moe(hf, xs, comb, moe_w_gu[l // 2].astype(BF16), moe_w_down[l // 2].astype(BF16), g_f)
            h = None
    return xs[None]
```

```python
import functools
import math

import jax
import jax.numpy as jnp
from jax import lax
from jax.experimental import pallas as pl
from jax.experimental.pallas import tpu as pltpu

F32 = jnp.float32
BF16 = jnp.bfloat16
HIGHEST = lax.Precision.HIGHEST

D_MODEL = 1024
LANES = 128
SUBLANES = 8

RW_HEADS = 8
RW_HEAD_DIM = 64
RW_WIDTH = RW_HEADS * RW_HEAD_DIM
RW_DECAY_LORA = 64
RW_AAA_LORA = 64
RW_VRES_LORA = 32
RW_GATE_LORA = 160
GN_EPS = 64e-5
RW_OFF_WLO = 3 * RW_WIDTH
RW_OFF_ALO = RW_OFF_WLO + LANES
RW_OFF_GLO = RW_OFF_ALO + LANES
RW_OFF_HV = RW_OFF_GLO + 2 * LANES
RW_PAD_COLS = RW_OFF_HV + LANES
RW_CHUNK = 64

SSD_HEADS = 16
SSD_HEAD_DIM = 64
SSD_WIDTH = SSD_HEADS * SSD_HEAD_DIM
SSD_GROUPS = 4
SSD_STATE = 128
SSD_CONV = 4
SSD_CHUNK = 128
SSD_XBC = SSD_WIDTH + 2 * SSD_GROUPS * SSD_STATE
SSD_GROUP_WIDTH = SSD_WIDTH // SSD_GROUPS

FOX_HEADS = 8
FOX_HEAD_DIM = 64
FOX_WIDTH = FOX_HEADS * FOX_HEAD_DIM
FOX_TQ = 1024
FOX_TK = 1024
LOG2E = 1.4426950408889634

FFN_DENSE = 2816
N_EXPERTS = 8
TOP_K = 2
FFN_EXPERT = 3584
MOE_SB = 128
MOE_CB = 512
EPS = 1e-6

VMEM_LIMIT = 56 * 1024 * 1024


def _cparams(*sem):
    return pltpu.CompilerParams(dimension_semantics=sem, vmem_limit_bytes=VMEM_LIMIT)


def _sigmoid(x):
    return 1.0 / (1.0 + jnp.exp(-x))


def _softplus(x):
    return jnp.maximum(x, 0.0) + jnp.log(1.0 + jnp.exp(-jnp.abs(x)))


def _dot(a, b):
    return jnp.dot(a, b, preferred_element_type=F32)


def _dot_bf16(a, b):
    return jnp.dot(a.astype(BF16), b.astype(BF16), preferred_element_type=F32)


def _dot_hi(a, b):
    return jnp.dot(a, b, precision=HIGHEST, preferred_element_type=F32)


def _dot_nt(a, b, precision=None):
    return lax.dot_general(a, b, (((1,), (1,)), ((), ())), precision=precision, preferred_element_type=F32)


def _dot_tn(a, b, precision=None):
    return lax.dot_general(a, b, (((0,), (0,)), ((), ())), precision=precision, preferred_element_type=F32)


def _split_dot(x, ones_bf16):
    return _dot(x.astype(BF16), ones_bf16)


def _split3(x):
    hi = x.astype(BF16)
    r1 = x - hi.astype(F32)
    mid = r1.astype(BF16)
    return hi, mid, (r1 - mid.astype(F32)).astype(BF16)


def _tril(n, strict=False):
    r = lax.broadcasted_iota(jnp.int32, (n, n), 0)
    c = lax.broadcasted_iota(jnp.int32, (n, n), 1)
    return (r > c) if strict else (r >= c)


def _mod_kernel(c_ref, w_ref, b_ref, o_ref):
    c = c_ref[...]
    o_ref[...] = _dot_hi(c * _sigmoid(c), w_ref[...]) + b_ref[...]


def adaln_mod(c, w, b):
    d, n = w.shape
    tn = 1024
    c8 = jnp.broadcast_to(c, (SUBLANES, d))
    out = pl.pallas_call(
        _mod_kernel,
        grid=(n // tn,),
        in_specs=[pl.BlockSpec((SUBLANES, d), lambda j: (0, 0)),
                  pl.BlockSpec((d, tn), lambda j: (0, j)),
                  pl.BlockSpec((1, tn), lambda j: (0, j))],
        out_specs=pl.BlockSpec((SUBLANES, tn), lambda j: (0, j)),
        out_shape=jax.ShapeDtypeStruct((SUBLANES, n), F32),
        compiler_params=_cparams("arbitrary"),
    )(c8, w, b.reshape(1, n))
    return out[:1]


def _norm_mod(x, gain, sc, sh):
    y = x * lax.rsqrt(jnp.mean(x * x, axis=-1, keepdims=True) + EPS)
    return y * gain * (1.0 + sc) + sh


def _norm_kernel(x_ref, gain_ref, sc_ref, sh_ref, h_ref):
    h_ref[...] = _norm_mod(x_ref[...], gain_ref[...], sc_ref[...], sh_ref[...]).astype(h_ref.dtype)


def norm_mod(x, gain, sc, sh, tm=512):
    L, d = x.shape
    row = pl.BlockSpec((1, d), lambda i: (0, 0))
    return pl.pallas_call(
        _norm_kernel,
        grid=(L // tm,),
        in_specs=[pl.BlockSpec((tm, d), lambda i: (i, 0)), row, row, row],
        out_specs=pl.BlockSpec((tm, d), lambda i: (i, 0)),
        out_shape=jax.ShapeDtypeStruct((L, d), BF16),
        compiler_params=_cparams("arbitrary"),
    )(x, gain, sc, sh)


def _rw_pre_kernel(has_vres, *refs):
    if has_vres:
        (h_ref, w_ref, mu_ref, w0_ref, wup_ref, a0_ref, aup_ref, gup_ref, kk_ref, ka_ref, rk_ref, seg_ref,
         vfirst_ref, v0_ref, vup_ref,
         r_out, k_out, v_out, lw_out, a_out, b_out, g_out, bonus_out, ext_scr) = refs
    else:
        (h_ref, w_ref, mu_ref, w0_ref, wup_ref, a0_ref, aup_ref, gup_ref, kk_ref, ka_ref, rk_ref, seg_ref,
         r_out, k_out, v_out, lw_out, a_out, b_out, g_out, bonus_out, ext_scr) = refs
    T = h_ref.shape[0]

    @pl.when(pl.program_id(0) == 0)
    def _():
        ext_scr[0:SUBLANES, :] = jnp.zeros((SUBLANES, ext_scr.shape[1]), F32)

    cur = _dot(h_ref[...], w_ref[...])
    ext_scr[SUBLANES:SUBLANES + T, :] = cur
    prev = ext_scr[SUBLANES - 1:SUBLANES - 1 + T, :]
    ext_scr[0:SUBLANES, :] = cur[T - SUBLANES:T, :]
    s = cur + (prev - cur) * mu_ref[...]

    W = RW_WIDTH
    r = s[:, 0:W]
    k = s[:, W:2 * W]
    v = s[:, 2 * W:3 * W]
    w_lo = s[:, RW_OFF_WLO:RW_OFF_WLO + LANES]
    a_lo = s[:, RW_OFF_ALO:RW_OFF_ALO + LANES]
    g_lo = s[:, RW_OFF_GLO:RW_OFF_GLO + 2 * LANES]
    seg = seg_ref[...]

    wlog = -_softplus(-(w0_ref[...] + _dot_bf16(jnp.tanh(w_lo), wup_ref[...]))) - 0.5
    a = _sigmoid(a0_ref[...] + _dot_bf16(a_lo, aup_ref[...]))
    g = _dot_bf16(_sigmoid(g_lo), gup_ref[...])
    if has_vres:
        hv = s[:, RW_OFF_HV:RW_OFF_HV + LANES]
        v = v + (vfirst_ref[...].astype(F32) - v) * _sigmoid(v0_ref[...] + _dot_bf16(hv, vup_ref[...]))
    kk = k * kk_ref[...]
    kk = kk / jnp.maximum(jnp.sqrt(_split_dot(kk * kk, seg)), 1e-12)
    k = k * (1.0 + (a - 1.0) * ka_ref[...])
    r_out[...] = r.astype(r_out.dtype)
    k_out[...] = k.astype(k_out.dtype)
    v_out[...] = v.astype(v_out.dtype)
    lw_out[...] = -jnp.exp(wlog)
    a_out[...] = (-kk).astype(a_out.dtype)
    b_out[...] = (kk * a).astype(b_out.dtype)
    g_out[...] = g.astype(g_out.dtype)
    bonus_out[...] = (_split_dot(r * k * rk_ref[...], seg) * v).astype(bonus_out.dtype)


def rw_pre(h, w_rw, p, v_first, tm=512):
    L, D = h.shape
    has_vres = v_first is not None
    full = lambda a: pl.BlockSpec(a.shape, lambda i: (0,) * a.ndim)
    rows = lambda n: pl.BlockSpec((tm, n), lambda i: (i, 0))
    args = [h, w_rw, p["mu"], p["w0"], p["w_up"], p["a0"], p["a_up"], p["g_up"], p["k_k"], p["k_a"], p["r_k"],
            p["seg"]]
    specs = [rows(D)] + [full(a) for a in args[1:]]
    if has_vres:
        args += [v_first, p["v0"], p["v_up"]]
        specs += [rows(RW_WIDTH), full(p["v0"]), full(p["v_up"])]
    out_dtypes = [BF16, BF16, BF16, F32, BF16, BF16, BF16, BF16]
    return pl.pallas_call(
        functools.partial(_rw_pre_kernel, has_vres),
        grid=(L // tm,),
        in_specs=specs,
        out_specs=[rows(RW_WIDTH)] * 8,
        out_shape=[jax.ShapeDtypeStruct((L, RW_WIDTH), dt) for dt in out_dtypes],
        scratch_shapes=[pltpu.VMEM((tm + SUBLANES, RW_PAD_COLS), F32)],
        compiler_params=_cparams("arbitrary"),
    )(*args)


def _wkv_kernel(r_ref, lw_ref, k_ref, v_ref, a_ref, b_ref, y_ref, h_scr):
    C = RW_CHUNK
    C2 = 2 * C
    n_chunks = r_ref.shape[0] // C

    @pl.when(pl.program_id(1) == 0)
    def _():
        h_scr[...] = jnp.zeros_like(h_scr)

    is_a = lax.broadcasted_iota(jnp.int32, (C, LANES), 1) < RW_HEAD_DIM
    ri = lax.broadcasted_iota(jnp.int32, (C2, C2), 0)
    ci = lax.broadcasted_iota(jnp.int32, (C2, C2), 1)
    same = (ri >= C) == (ci >= C)
    strict = jnp.logical_and(same, ri > ci)
    incl = jnp.logical_and(same, ri >= ci)
    eye2 = (ri == ci).astype(F32)
    tri_c = _tril(C).astype(BF16)

    def two(x):
        return jnp.concatenate([jnp.where(is_a, x, 0.0), jnp.where(is_a, 0.0, x)], axis=0)

    chunks = range(n_chunks)
    at2, rt2, bh2, kh2, v2, n_ab, a_ak, m_rb, m_rk, p_end = ([] for _ in range(10))
    for c in chunks:
        sl = pl.ds(c * C, C)
        lw = lw_ref[sl, :]
        lw_hi = lw.astype(BF16)
        lw_lo = (lw - lw_hi.astype(F32)).astype(BF16)
        cs = _dot(tri_c, lw_hi) + _dot(tri_c, lw_lo)
        cs_end = cs[C - 1:C, :]
        e_neg = jnp.exp(-cs)
        e_end = jnp.exp(cs_end - cs)
        a = a_ref[sl, :].astype(F32)
        b = b_ref[sl, :].astype(F32)
        k = k_ref[sl, :].astype(F32)
        at2.append(two(a * jnp.exp(cs - lw)))
        rt2.append(two(r_ref[sl, :].astype(F32) * jnp.exp(cs)))
        bh2.append(two(b * e_end))
        kh2.append(two(k * e_end))
        v2.append(two(v_ref[sl, :].astype(F32)))
        p_end.append(jnp.exp(cs_end))
        quad = _dot_nt(jnp.concatenate([at2[c], rt2[c]], axis=0).astype(BF16),
                       jnp.concatenate([two(b * e_neg), two(k * e_neg)], axis=0).astype(BF16))
        n_ab.append(jnp.where(strict, quad[0:C2, 0:C2], 0.0))
        a_ak.append(jnp.where(strict, quad[0:C2, C2:], 0.0))
        m_rb.append(jnp.where(incl, quad[C2:, 0:C2], 0.0))
        m_rk.append(jnp.where(incl, quad[C2:, C2:], 0.0))
    t_inv = [eye2 + n for n in n_ab]
    n_pow = n_ab
    n_pow = [_dot_bf16(n, n) for n in n_pow]
    for _ in range(int(math.log2(C)) - 2):
        both = [_dot_bf16(jnp.concatenate([n, t], axis=0), n) for n, t in zip(n_pow, t_inv)]
        t_inv = [t + b[C2:, :] for t, b in zip(t_inv, both)]
        n_pow = [b[:C2, :] for b in both]
    t_inv = [t + _dot_bf16(t, n) for t, n in zip(t_inv, n_pow)]
    akv = [_dot_bf16(a_ak[c], v2[c]) for c in chunks]
    au = [_dot_bf16(t_inv[c], jnp.concatenate([at2[c], akv[c]], axis=1)) for c in chunks]
    mau = [_dot_bf16(m_rb[c], au[c]) for c in chunks]
    mkv = [_dot_bf16(m_rk[c], v2[c]) for c in chunks]
    gu = [_dot_tn(bh2[c].astype(BF16), au[c].astype(BF16)) for c in chunks]
    kv = [_dot_tn(kh2[c].astype(BF16), v2[c].astype(BF16)) for c in chunks]
    inv_n = 1.0 / RW_HEAD_DIM
    h = h_scr[...]
    for c in chunks:
        rh2 = rt2[c] + mau[c][:, :LANES]
        g_mat = eye2 * p_end[c] + gu[c][:, :LANES]
        yh = _dot_bf16(jnp.concatenate([rh2, g_mat], axis=0), h)
        h = yh[C2:, :] + gu[c][:, LANES:] + kv[c]
        y2 = yh[:C2, :] + mau[c][:, LANES:] + mkv[c]
        y = y2[:C, :] + y2[C:, :]
        s_all = jnp.sum(y, axis=-1, keepdims=True)
        s_a = jnp.sum(jnp.where(is_a, y, 0.0), axis=-1, keepdims=True)
        yc = y - jnp.where(is_a, s_a, s_all - s_a) * inv_n
        sq = yc * yc
        q_all = jnp.sum(sq, axis=-1, keepdims=True)
        q_a = jnp.sum(jnp.where(is_a, sq, 0.0), axis=-1, keepdims=True)
        yn = yc * lax.rsqrt(jnp.where(is_a, q_a, q_all - q_a) * inv_n + GN_EPS)
        y_ref[pl.ds(c * C, C), :] = yn.astype(y_ref.dtype)
    h_scr[...] = h


def wkv7(r, lw, k, v, a, b, rows=2048):
    L, W = r.shape
    spec = pl.BlockSpec((rows, LANES), lambda p, i: (i, p))
    return pl.pallas_call(
        _wkv_kernel,
        grid=(W // LANES, L // rows),
        in_specs=[spec] * 6,
        out_specs=spec,
        out_shape=jax.ShapeDtypeStruct((L, W), BF16),
        scratch_shapes=[pltpu.VMEM((LANES, LANES), F32)],
        compiler_params=_cparams("arbitrary", "arbitrary"),
    )(r, lw, k, v, a, b)


def _ssd_kernel(h_ref, w_ref, cw_ref, cb_ref, bias_ref, A_ref, exp_ref, dskip_ref, nw_ref, o_ref,
                ext_scr, xbc_scr, s_scr):
    Q = SSD_CHUNK
    GW = SSD_GROUP_WIDTH
    NS = SSD_STATE
    R = h_ref.shape[0]

    @pl.when(pl.program_id(0) == 0)
    def _():
        s_scr[...] = jnp.zeros_like(s_scr)
        ext_scr[0:SUBLANES, :] = jnp.zeros((SUBLANES, ext_scr.shape[1]), F32)

    cols = _dot(h_ref[...], w_ref[...])
    ext_scr[SUBLANES:SUBLANES + R, :] = cols[:, 0:SSD_XBC]
    acc = cols[:, 0:SSD_XBC] * cw_ref[SSD_CONV - 1:SSD_CONV, :] + cb_ref[...]
    for j in range(1, SSD_CONV):
        acc = acc + ext_scr[SUBLANES - j:SUBLANES - j + R, :] * cw_ref[SSD_CONV - 1 - j:SSD_CONV - j, :]
    ext_scr[0:SUBLANES, :] = ext_scr[R:R + SUBLANES, :]
    xbc_scr[...] = acc * _sigmoid(acc)

    incl = _tril(Q)
    tri = incl.astype(BF16)
    triu = jnp.logical_not(_tril(Q, strict=True)).astype(BF16)
    expand = exp_ref[...]
    first_half = lax.broadcasted_iota(jnp.int32, (Q, LANES), 1) < SSD_HEAD_DIM
    heads_per_group = SSD_HEADS // SSD_GROUPS
    for c in range(R // Q):
        rows = slice(c * Q, (c + 1) * Q)
        xs = xbc_scr[rows, 0:SSD_WIDTH]
        z = cols[rows, SSD_XBC:SSD_XBC + SSD_WIDTH]
        dt = _softplus(cols[rows, SSD_XBC + SSD_WIDTH:] + bias_ref[...])
        a3 = _split3(dt * A_ref[...])
        a_cum = sum(_dot(tri, t) for t in a3)
        a_cumT = sum(_dot_tn(t, triu) for t in a3)
        a_end = a_cum[Q - 1:Q, :]
        dt_x = _dot(dt.astype(BF16), expand)
        eac_x = _dot(jnp.exp(a_cum).astype(BF16), expand)
        dte_x = _dot(jnp.exp(a_end - a_cum).astype(BF16), expand)
        cd_x = sum(_dot(t, expand) for t in _split3(jnp.broadcast_to(jnp.exp(a_end), (SUBLANES, LANES))))[0:1, :]
        xdt = xs * dt_x
        for g in range(SSD_GROUPS):
            Bg = xbc_scr[rows, SSD_WIDTH + g * NS:SSD_WIDTH + (g + 1) * NS].astype(BF16)
            Cg = xbc_scr[rows, SSD_WIDTH + (SSD_GROUPS + g) * NS:SSD_WIDTH + (SSD_GROUPS + g + 1) * NS].astype(BF16)
            cb = _dot_nt(Cg, Bg)
            gs = slice(g * GW, (g + 1) * GW)
            s_prev = s_scr[:, gs]
            y_g = _dot_bf16(Cg, s_prev) * eac_x[:, gs]
            pieces = []
            for pr in range(heads_per_group // 2):
                ps = slice(g * GW + pr * LANES, g * GW + (pr + 1) * LANES)
                xdt_p = xdt[:, ps]
                acc_p = None
                for e in range(2):
                    hd = g * heads_per_group + pr * 2 + e
                    seg = a_cum[:, hd:hd + 1] - a_cumT[hd:hd + 1, :]
                    m = cb * jnp.exp(jnp.where(incl, seg, -jnp.inf))
                    xm = jnp.where(first_half if e == 0 else jnp.logical_not(first_half), xdt_p, 0.0)
                    t = _dot_bf16(m, xm)
                    acc_p = t if acc_p is None else acc_p + t
                pieces.append(acc_p)
            y_g = y_g + jnp.concatenate(pieces, axis=1)
            s_scr[:, gs] = s_prev * cd_x[:, gs] +_dot_tn(Bg, (dte_x[:, gs] * xdt[:, gs]).astype(BF16))
            y_g = y_g + dskip_ref[:, gs] * xs[:, gs]
            zg = z[:, gs]
            y_g = y_g * (zg * _sigmoid(zg))
            y_g = y_g * lax.rsqrt(jnp.mean(y_g * y_g, axis=-1, keepdims=True) + EPS)
            o_ref[rows, gs] = (y_g * nw_ref[:, gs]).astype(o_ref.dtype)


def ssd_mixer(h, w_ssd, p, rows=512):
    L, D = h.shape
    N = w_ssd.shape[1]
    full = lambda a: pl.BlockSpec(a.shape, lambda i: (0,) * a.ndim)
    small = [p["conv_w"], p["conv_b"], p["dt_bias"], p["A"], p["expand"], p["d_skip"], p["norm_w"]]
    return pl.pallas_call(
        _ssd_kernel,
        grid=(L // rows,),
        in_specs=[pl.BlockSpec((rows, D), lambda i: (i, 0)), full(w_ssd)] + [full(a) for a in small],
        out_specs=pl.BlockSpec((rows, SSD_WIDTH), lambda i: (i, 0)),
        out_shape=jax.ShapeDtypeStruct((L, SSD_WIDTH), BF16),
        scratch_shapes=[pltpu.VMEM((rows + SUBLANES, SSD_XBC), F32), pltpu.VMEM((rows, SSD_XBC), F32),
                        pltpu.VMEM((SSD_STATE, SSD_WIDTH), F32)],
        compiler_params=_cparams("arbitrary"),
    )(h, w_ssd, *small)


def _fox_pre_kernel(h_ref, w_ref, fb_ref, qg_ref, kg_ref, seg_ref, q_out, k_out, v_out, f_out, carry_scr):
    T = h_ref.shape[0]
    W = FOX_WIDTH

    @pl.when(pl.program_id(0) == 0)
    def _():
        carry_scr[...] = jnp.zeros_like(carry_scr)

    cols = _dot(h_ref[...], w_ref[...])
    seg = seg_ref[...]
    q = cols[:, 0:W]
    k = cols[:, W:2 * W]
    inv_d = 1.0 / FOX_HEAD_DIM
    qn = q * lax.rsqrt(_split_dot(q * q, seg) * inv_d + EPS) * qg_ref[...]
    kn = k * lax.rsqrt(_split_dot(k * k, seg) * inv_d + EPS) * kg_ref[...]
    q_out[...] = qn.astype(q_out.dtype)
    k_out[...] = kn.astype(k_out.dtype)
    v_out[...] = cols[:, 2 * W:3 * W].astype(v_out.dtype)
    f = cols[:, 3 * W:3 * W + LANES] + fb_ref[...]
    ls = -_softplus(-f)
    cum = _dot_hi(_tril(T).astype(F32), ls) + carry_scr[0:1, :]
    f_out[...] = cum * LOG2E
    carry_scr[...] = jnp.broadcast_to(cum[T - 1:T, :], carry_scr.shape)


def fox_pre(h, w_fox, p, tm=512):
    L, D = h.shape
    full = lambda a: pl.BlockSpec(a.shape, lambda i: (0,) * a.ndim)
    rows = lambda n: pl.BlockSpec((tm, n), lambda i: (i, 0))
    small = [w_fox, p["f_bias"], p["q_gain"], p["k_gain"], p["seg"]]
    return pl.pallas_call(
        _fox_pre_kernel,
        grid=(L // tm,),
        in_specs=[rows(D)] + [full(a) for a in small],
        out_specs=[rows(FOX_WIDTH)] * 3 + [rows(LANES)],
        out_shape=[jax.ShapeDtypeStruct((L, FOX_WIDTH), BF16)] * 3 + [jax.ShapeDtypeStruct((L, LANES), F32)],
        scratch_shapes=[pltpu.VMEM((SUBLANES, LANES), F32)],
        compiler_params=_cparams("arbitrary"),
    )(h, *small)


def _fox_kernel(lo_ref, q_ref, k_ref, v_ref, fk_ref, o_ref, m_scr, acc_scr, s0_scr, s1_scr):
    tq = q_ref.shape[0]
    tk = FOX_TK
    pair = pl.program_id(0)
    qi = pl.program_id(1)
    is_a = lax.broadcasted_iota(jnp.int32, (tq, LANES), 1) < FOX_HEAD_DIM
    q = q_ref[...]
    zero = jnp.zeros_like(q)
    qs = (jnp.where(is_a, q, zero), jnp.where(is_a, zero, q))
    m_scr[...] = jnp.full(m_scr.shape, -jnp.inf, F32)
    key_is_a = lax.broadcasted_iota(jnp.int32, (tk, LANES), 1) < FOX_HEAD_DIM
    acc_scr[...] = jnp.zeros_like(acc_scr)
    rel = lax.broadcasted_iota(jnp.int32, (tq, tk), 0) - lax.broadcasted_iota(jnp.int32, (tq, tk), 1)
    last = (qi * tq) // tk

    def fill(s_ref, kb, diagonal=False):
        ks = pl.ds(pl.multiple_of(kb * tk, tk), tk)
        k_blk = k_ref[ks, :]
        for e in range(2):
            s = _dot_nt(qs[e], k_blk) - fk_ref[pl.ds(2 * pair + e, 1), ks]
            if diagonal:
                s = jnp.where(rel >= kb * tk - qi * tq, s, -jnp.inf)
            s_ref[e] = s

    def consume(s_ref, kb):
        v_blk = v_ref[pl.ds(pl.multiple_of(kb * tk, tk), tk), :]
        one = jnp.ones_like(v_blk)
        v_one = (jnp.where(key_is_a, v_blk, one), jnp.where(key_is_a, one, v_blk))
        for e in range(2):
            s = s_ref[e]
            m_old = m_scr[e]
            m_new = jnp.maximum(m_old, jnp.max(s, axis=-1, keepdims=True))
            pexp = jnp.exp2(s - m_new)
            acc_scr[e] = jnp.exp2(m_old - m_new) * acc_scr[e] + _dot(pexp.astype(BF16), v_one[e])
            m_scr[e] = m_new

    n = last - jnp.minimum(lo_ref[2 * pair, qi], lo_ref[2 * pair + 1, qi])
    fill(s0_scr, last, diagonal=True)

    def two_steps(u, carry):
        kb = last - 2 * u
        fill(s1_scr, kb - 1)
        consume(s0_scr, kb)
        fill(s0_scr, kb - 2)
        consume(s1_scr, kb - 1)
        return carry

    lax.fori_loop(0, n // 2, two_steps, 0)

    @pl.when(n % 2 == 1)
    def _():
        fill(s1_scr, last - n)
        consume(s0_scr, last - n + 1)
        consume(s1_scr, last - n)

    @pl.when(n % 2 == 0)
    def _():
        consume(s0_scr, last - n)

    acc_a = acc_scr[0]
    acc_b = acc_scr[1]
    half = FOX_HEAD_DIM
    o_ref[...] = jnp.where(is_a, acc_a / pltpu.roll(acc_a, half, 1),
                           acc_b / pltpu.roll(acc_b, half, 1)).astype(o_ref.dtype)


def fox_attention(q, k, v, FT, lo):
    L, W = q.shape
    H = FT.shape[0]
    tq = FOX_TQ
    grid_spec = pltpu.PrefetchScalarGridSpec(
        num_scalar_prefetch=1,
        grid=(W // LANES, L // tq),
        in_specs=[pl.BlockSpec((tq, LANES), lambda p, i, lo_r: (i, p)),
                  pl.BlockSpec((L, LANES), lambda p, i, lo_r: (0, p)),
                  pl.BlockSpec((L, LANES), lambda p, i, lo_r: (0, p)),
                  pl.BlockSpec((H, L), lambda p, i, lo_r: (0, 0))],
        out_specs=pl.BlockSpec((tq, LANES), lambda p, i, lo_r: (i, p)),
        scratch_shapes=[pltpu.VMEM((2, tq, 1), F32), pltpu.VMEM((2, tq, LANES), F32),
                        pltpu.VMEM((2, tq, FOX_TK), F32), pltpu.VMEM((2, tq, FOX_TK), F32)],
    )
    return pl.pallas_call(
        _fox_kernel,
        grid_spec=grid_spec,
        out_shape=jax.ShapeDtypeStruct((L, W), BF16),
        compiler_params=_cparams("arbitrary", "arbitrary"),
    )(lo, q, k, v, FT)


def fox_first_block(FT, bound2):
    f_first = FT[:, ::FOX_TQ]
    f_last = FT[:, FOX_TK - 1::FOX_TK]
    gap = f_first[:, :, None] - f_last[:, None, :] + 2.0 * bound2
    last = (jnp.arange(f_first.shape[1]) * FOX_TQ) // FOX_TK
    needed = jnp.logical_or(gap >= -152.0, jnp.arange(f_last.shape[1])[None, None, :] >= last[None, :, None])
    return jnp.argmax(needed, axis=-1).astype(jnp.int32)


def _top2_combine(h, wr):
    T = h.shape[0]
    lane = lax.broadcasted_iota(jnp.int32, (T, LANES), 1)
    h_hi = h.astype(BF16)
    h_lo = (h - h_hi.astype(F32)).astype(BF16)
    w_hi = wr.astype(BF16)
    w_lo = (wr - w_hi.astype(F32)).astype(BF16)
    logits = _dot(h_hi, w_hi) + (_dot(h_hi, w_lo) + _dot(h_lo, w_hi))
    logits = jnp.where(lane < N_EXPERTS, logits, -jnp.inf)
    m1 = jnp.max(logits, axis=-1, keepdims=True)
    i1 = jnp.min(jnp.where(logits == m1, lane, LANES), axis=-1, keepdims=True)
    rest = jnp.where(lane == i1, -jnp.inf, logits)
    m2 = jnp.max(rest, axis=-1, keepdims=True)
    i2 = jnp.min(jnp.where(rest == m2, lane, LANES), axis=-1, keepdims=True)
    e2 = jnp.exp(m2 - m1)
    return jnp.where(lane == i1, 1.0 / (1.0 + e2), 0.0) + jnp.where(lane == i2, e2 / (1.0 + e2), 0.0)


def _merge_kernel(with_router, *refs):
    (x_ref, hm_ref, yn_ref, bonus_ref, g_ref, yssd_ref, yfox_ref, wgate_ref, gb_ref, lnw_ref, lnb_ref,
     prw_ref, pssd_ref, pfox_ref, wout_ref, gm_ref, gain_ref, sc_ref, sh_ref) = refs[:19]
    D = D_MODEL
    hm = hm_ref[...]
    y_rw = (yn_ref[...] * lnw_ref[...] + lnb_ref[...] + bonus_ref[...]) * g_ref[...]
    merged = None
    for b, (y_b, proj_ref) in enumerate(((y_rw, prw_ref), (yssd_ref[...], pssd_ref), (yfox_ref[...], pfox_ref))):
        cols = slice(b * D, (b + 1) * D)
        gate = _sigmoid(_dot(hm, wgate_ref[:, cols]) + gb_ref[:, cols])
        term = gate * _dot_bf16(y_b, proj_ref[...])
        merged = term if merged is None else merged + term
    x_new = x_ref[...] + gm_ref[...] * _dot_bf16(merged, wout_ref[...])
    hf = _norm_mod(x_new, gain_ref[...], sc_ref[...], sh_ref[...])
    if with_router:
        wr_ref, o_ref, h_ref, comb_ref = refs[19:]
        comb_ref[...] = _top2_combine(hf, wr_ref[...])
    else:
        o_ref, h_ref = refs[19:]
    o_ref[...] = x_new
    h_ref[...] = hf.astype(h_ref.dtype)


def merge(x, hm, yn, bonus, g, y_ssd, y_fox, p, router=None, tm=256):
    L, D = x.shape
    full = lambda a: pl.BlockSpec(a.shape, lambda i: (0,) * a.ndim)
    rows = lambda n: pl.BlockSpec((tm, n), lambda i: (i, 0))
    small = [p["w_gate"], p["gate_b"], p["lnx_w"], p["lnx_b"], p["proj_rw"], p["proj_ssd"], p["proj_fox"],
             p["w_out"], p["g_m"], p["norm_gain"], p["sc_f"], p["sh_f"]]
    out_specs = [rows(D), rows(D)]
    out_shape = [jax.ShapeDtypeStruct((L, D), F32), jax.ShapeDtypeStruct((L, D), BF16)]
    if router is not None:
        small.append(router)
        out_specs.append(rows(LANES))
        out_shape.append(jax.ShapeDtypeStruct((L, LANES), F32))
    return pl.pallas_call(
        functools.partial(_merge_kernel, router is not None),
        grid=(L // tm,),
        in_specs=[rows(D), rows(D), rows(RW_WIDTH), rows(RW_WIDTH), rows(RW_WIDTH), rows(SSD_WIDTH),
                  rows(FOX_WIDTH)] + [full(a) for a in small],
        out_specs=out_specs,
        out_shape=out_shape,
        compiler_params=_cparams("arbitrary"),
    )(x, hm, yn, bonus, g, y_ssd, y_fox, *small)


def _ffn_kernel(with_next, *refs):
    h_ref, x_ref, wg_ref, wu_ref, wd_ref, gf_ref = refs[:6]
    j = pl.program_id(1)
    acc_scr = refs[-1]

    @pl.when(j == 0)
    def _():
        acc_scr[...] = jnp.zeros_like(acc_scr)

    h = h_ref[...]
    gte = _dot(h, wg_ref[...])
    up = _dot(h, wu_ref[...])
    act = gte * _sigmoid(gte) * up
    acc_scr[...] += _dot(act.astype(BF16), wd_ref[...])

    @pl.when(j == pl.num_programs(1) - 1)
    def _():
        x_new = x_ref[...] + gf_ref[...] * acc_scr[...]
        if with_next:
            gain_ref, sc_ref, sh_ref, o_ref, hn_ref = refs[6:11]
            hn_ref[...] = _norm_mod(x_new, gain_ref[...], sc_ref[...], sh_ref[...]).astype(hn_ref.dtype)
        else:
            o_ref = refs[6]
        o_ref[...] = x_new


def ffn_dense(h, x, w_gu, w_down, g_f, next_norm=None, tm=1024, tf=256):
    L, D = x.shape
    Fh = w_down.shape[0]
    nf = Fh // tf
    row = pl.BlockSpec((1, D), lambda i, j: (0, 0))
    tile = pl.BlockSpec((tm, D), lambda i, j: (i, 0))
    args = [h, x, w_gu, w_gu, w_down, g_f]
    in_specs = [tile, tile,
                pl.BlockSpec((D, tf), lambda i, j: (0, j)),
                pl.BlockSpec((D, tf), lambda i, j: (0, j + nf)),
                pl.BlockSpec((tf, D), lambda i, j: (j, 0)), row]
    out_specs = [tile]
    out_shape = [jax.ShapeDtypeStruct((L, D), F32)]
    if next_norm is not None:
        args += list(next_norm)
        in_specs += [row, row, row]
        out_specs.append(tile)
        out_shape.append(jax.ShapeDtypeStruct((L, D), BF16))
    out = pl.pallas_call(
        functools.partial(_ffn_kernel, next_norm is not None),
        grid=(L // tm, nf),
        in_specs=in_specs,
        out_specs=out_specs,
        out_shape=out_shape,
        scratch_shapes=[pltpu.VMEM((tm, D), F32)],
        compiler_params=_cparams("arbitrary", "arbitrary"),
    )(*args)
    return out if next_norm is not None else (out[0], None)


def _moe_slots(slot_scr, comb_ref):
    slotval = slot_scr[...]
    comb = comb_ref[...]
    s1 = jnp.max(slotval, axis=-1, keepdims=True)
    first = slotval == s1
    c1 = jnp.sum(jnp.where(first, comb, 0.0), axis=-1, keepdims=True)
    rest = jnp.where(first, -1.0, slotval)
    s2 = jnp.max(rest, axis=-1, keepdims=True)
    c2 = jnp.sum(jnp.where(jnp.logical_and(rest == s2, rest >= 0.0), comb, 0.0), axis=-1, keepdims=True)
    return s1, s2, c1, c2


def _moe_kernel(nsb_ref, off_ref, h_ref, x_ref, comb_ref, wg_ref, wu_ref, wd_ref, gf_ref, o_ref,
                slot_scr, cslot_scr, xy_scr, acc_scr):
    i = pl.program_id(0)
    e = pl.program_id(1)
    j = pl.program_id(2)
    last_j = pl.num_programs(2) - 1
    T = h_ref.shape[0]
    S = xy_scr.shape[0]
    SB = MOE_SB

    @pl.when(jnp.logical_and(e == 0, j == 0))
    def _():
        lane = lax.broadcasted_iota(jnp.int32, (LANES, LANES), 1)
        off = jnp.zeros((LANES, LANES), F32)
        for ex in range(N_EXPERTS):
            off = jnp.where(lane == ex, off_ref[i, ex].astype(F32), off)
        before = _tril(LANES, strict=True).astype(BF16)
        ones = jnp.ones((LANES, LANES), BF16)
        for rb in range(T // LANES):
            rows = slice(rb * LANES, (rb + 1) * LANES)
            sel = comb_ref[rows, :] > 0.0
            sel_b = jnp.where(sel, 1.0, 0.0).astype(BF16)
            slot_scr[rows, :] = jnp.where(sel, _dot(before, sel_b) + off, -1.0)
            off = off + _dot(ones, sel_b)
        s1, s2, c1, c2 = _moe_slots(slot_scr, comb_ref)
        used = off_ref[i, N_EXPERTS - 1] + nsb_ref[i, N_EXPERTS - 1] * SB
        for cb in range(S // MOE_CB):
            chunk = slice(cb * MOE_CB, (cb + 1) * MOE_CB)

            @pl.when(used > cb * MOE_CB)
            def _():
                slot = (lax.broadcasted_iota(jnp.int32, (T, MOE_CB), 1) + cb * MOE_CB).astype(F32)
                w = jnp.where(s1 == slot, c1, 0.0) + jnp.where(s2 == slot, c2, 0.0)
                pt = jnp.where(w > 0.0, 1.0, 0.0).astype(BF16)
                xy_scr[chunk, :] = _dot_tn(pt, h_ref[...]).astype(BF16)
                w_hi = w.astype(BF16)
                w_lo = (w - w_hi.astype(F32)).astype(BF16)
                ones_t = jnp.ones((T, LANES), BF16)
                cslot_scr[chunk, :] = _dot_tn(w_hi, ones_t) + _dot_tn(w_lo, ones_t)

            @pl.when(used <= cb * MOE_CB)
            def _():
                xy_scr[chunk, :] = jnp.zeros((MOE_CB, xy_scr.shape[1]), BF16)
                cslot_scr[chunk, :] = jnp.zeros((MOE_CB, LANES), F32)

    base = off_ref[i, e]
    nb = nsb_ref[i, e]

    def ffn_block(r0, n_rows):
        rows = pl.ds(pl.multiple_of(base + r0, SB), n_rows)
        arows = pl.ds(pl.multiple_of(r0, SB), n_rows)
        xb = xy_scr[rows, :]
        gte = _dot(xb, wg_ref[0])
        up = _dot(xb, wu_ref[0])
        part = _dot((gte * _sigmoid(gte) * up).astype(BF16), wd_ref[0])

        @pl.when(j == 0)
        def _():
            acc_scr[arows, :] = part

        @pl.when(jnp.logical_and(j > 0, j < last_j))
        def _():
            acc_scr[arows, :] += part

        @pl.when(j == last_j)
        def _():
            xy_scr[rows, :] = ((acc_scr[arows, :] + part) * cslot_scr[rows, 0:1]).astype(BF16)

    def quad_body(p, carry):
        ffn_block(p * (4 * SB), 4 * SB)
        return carry

    lax.fori_loop(0, nb // 4, quad_body, 0)
    for rest in range(1, 4):
        @pl.when(nb % 4 == rest)
        def _():
            ffn_block((nb - rest) * SB, rest * SB)

    @pl.when(jnp.logical_and(e == pl.num_programs(1) - 1, j == last_j))
    def _():
        s1, s2, _, _ = _moe_slots(slot_scr, comb_ref)
        total = jnp.zeros((T, xy_scr.shape[1]), F32)
        for cb in range(S // MOE_CB):
            slot = (lax.broadcasted_iota(jnp.int32, (T, MOE_CB), 1) + cb * MOE_CB).astype(F32)
            pt = jnp.where(jnp.logical_or(s1 == slot, s2 == slot), 1.0, 0.0).astype(BF16)
            total = total + _dot(pt, xy_scr[cb * MOE_CB:(cb + 1) * MOE_CB, :])
        o_ref[...] = x_ref[...] + gf_ref[...] * total


def ffn_moe(h, x, comb, w_gu, w_down, g_f, tm=1024, tf=896):
    L, D = x.shape
    E, Fh, _ = w_down.shape
    nf = Fh // tf
    assert nf >= 2 and E == N_EXPERTS
    nt = L // tm
    counts = jnp.sum((comb[:, :E] > 0.0).reshape(nt, tm, E), axis=1, dtype=jnp.int32)
    nsb = (counts + (MOE_SB - 1)) // MOE_SB
    off = (jnp.cumsum(nsb, axis=1) - nsb) * MOE_SB
    idx = lambda f: (lambda i, e, j, n, o: f(i, e, j))
    grid_spec = pltpu.PrefetchScalarGridSpec(
        num_scalar_prefetch=2,
        grid=(nt, E, nf),
        in_specs=[pl.BlockSpec((tm, D), idx(lambda i, e, j: (i, 0))),
                  pl.BlockSpec((tm, D), idx(lambda i, e, j: (i, 0))),
                  pl.BlockSpec((tm, LANES), idx(lambda i, e, j: (i, 0))),
                  pl.BlockSpec((1, D, tf), idx(lambda i, e, j: (e, 0, j))),
                  pl.BlockSpec((1, D, tf), idx(lambda i, e, j: (e, 0, j + nf))),
                  pl.BlockSpec((1, tf, D), idx(lambda i, e, j: (e, j, 0))),
                  pl.BlockSpec((1, D), idx(lambda i, e, j: (0, 0)))],
        out_specs=pl.BlockSpec((tm, D), idx(lambda i, e, j: (i, 0))),
        scratch_shapes=[pltpu.VMEM((tm, LANES), F32),
                        pltpu.VMEM((TOP_K * tm + E * MOE_SB, LANES), F32),
                        pltpu.VMEM((TOP_K * tm + E * MOE_SB, D), BF16),
                        pltpu.VMEM((tm, D), F32)],
    )
    return pl.pallas_call(
        _moe_kernel,
        grid_spec=grid_spec,
        out_shape=jax.ShapeDtypeStruct((L, D), F32),
        compiler_params=_cparams("arbitrary", "arbitrary", "arbitrary"),
    )(nsb, off, h, x, comb, w_gu, w_gu, w_down, g_f)


def _seg_matrix(width, head_dim):
    idx = jnp.arange(width) // head_dim
    return (idx[:, None] == idx[None, :]).astype(BF16)


def _pad_cols(w, n):
    return jnp.pad(w, ((0, 0), (0, n - w.shape[1])))


def _pad_rows(w, n):
    return jnp.pad(w, ((0, n - w.shape[0]), (0, 0)))


def rwkv_branch(h, w_rw, p, v_first):
    r, k, v, lw, a, b, g, bonus = rw_pre(h, w_rw, p, v_first)
    return wkv7(r, lw, k, v, a, b), bonus, g, v


def fox_branch(h, w_fox, p):
    q, k, v, F = fox_pre(h, w_fox, p)
    FT = jnp.transpose(F[:, :FOX_HEADS])
    lo = fox_first_block(FT, p["bound2"])
    return fox_attention(q, k, v, FT, lo)


def kernel(x, c, ada_w, ada_b, norm_mix, norm_ffn, w_in, rw_mu, rw_w0, rw_w_up, rw_a0, rw_a_up, rw_g_up, rw_k_k, rw_k_a, rw_r_k, rw_lnx_w, rw_lnx_b, rw_v0, rw_v_down, rw_v_up, ssd_conv_w, ssd_conv_b, ssd_dt_bias, ssd_a_log, ssd_d, ssd_norm, fox_f_bias, fox_q_gain, fox_k_gain, gate_b, proj_rw, proj_ssd, proj_fox, w_out, ffn_w_gu, ffn_w_down, moe_router, moe_w_gu, moe_w_down):
    depth = w_in.shape[0]
    D = D_MODEL
    xs = x[0]
    row = lambda t: t.reshape(1, -1).astype(F32)
    seg64 = _seg_matrix(RW_WIDTH, RW_HEAD_DIM)
    rw_cols = 3 * RW_WIDTH + RW_DECAY_LORA + RW_AAA_LORA + RW_GATE_LORA
    ssd_cols = SSD_WIDTH + SSD_XBC + SSD_HEADS
    fox_cols = 3 * FOX_WIDTH + FOX_HEADS
    expand = (jnp.arange(LANES)[:, None] == (jnp.arange(SSD_WIDTH) // SSD_HEAD_DIM)[None, :]).astype(BF16)
    v_first = None
    mods = []
    for l in range(depth):
        mod = adaln_mod(c, ada_w[l], ada_b[l])
        mods.append([mod[:, i * D:(i + 1) * D] for i in range(6)])
    h = None
    for l in range(depth):
        sh_m, sc_m, g_m, sh_f, sc_f, g_f = mods[l]
        if h is None:
            h = norm_mod(xs, row(norm_mix[l]), sc_m, sh_m)

        wl = w_in[l]
        o = 0
        w_r = wl[:, o:o + rw_cols]; o += rw_cols
        w_s = wl[:, o:o + ssd_cols]; o += ssd_cols
        w_f = wl[:, o:o + fox_cols]; o += fox_cols
        w_g = wl[:, o:]
        W3 = 3 * RW_WIDTH
        o_a = W3 + RW_DECAY_LORA
        o_g = o_a + RW_AAA_LORA

        def rw_layout(t, hv):
            parts = [t[:, :W3], _pad_cols(t[:, W3:o_a], LANES), _pad_cols(t[:, o_a:o_g], LANES),
                     _pad_cols(t[:, o_g:], 2 * LANES), _pad_cols(hv, LANES)]
            return jnp.concatenate(parts, axis=1)

        if l == 0:
            hv_w = jnp.zeros((D, RW_VRES_LORA), F32)
        else:
            hv_w = rw_v_down[l - 1]
        w_rw = rw_layout(w_r, hv_w).astype(BF16)
        mu = rw_layout(rw_mu[l].reshape(1, -1), jnp.zeros((1, RW_VRES_LORA), F32))
        rw_p = dict(mu=mu, w0=row(rw_w0[l]), w_up=_pad_rows(rw_w_up[l], LANES).astype(BF16), a0=row(rw_a0[l]),
                    a_up=_pad_rows(rw_a_up[l], LANES).astype(BF16),
                    g_up=_pad_rows(rw_g_up[l], 2 * LANES).astype(BF16),
                    k_k=row(rw_k_k[l]), k_a=row(rw_k_a[l]), r_k=row(rw_r_k[l]), seg=seg64)
        if l > 0:
            rw_p.update(v0=row(rw_v0[l - 1]), v_up=_pad_rows(rw_v_up[l - 1], LANES).astype(BF16))
        yn, bonus, g_rw, v_cur = rwkv_branch(h, w_rw, rw_p, v_first)
        if l == 0:
            v_first = v_cur

        w_ssd = jnp.concatenate([w_s[:, SSD_WIDTH:SSD_WIDTH + SSD_XBC], w_s[:, :SSD_WIDTH],
                                 _pad_cols(w_s[:, SSD_WIDTH + SSD_XBC:], LANES)], axis=1).astype(BF16)
        A = -jnp.exp(ssd_a_log[l].astype(F32))
        ssd_p = dict(conv_w=ssd_conv_w[l], conv_b=row(ssd_conv_b[l]),
                     dt_bias=_pad_cols(row(ssd_dt_bias[l]), LANES),
                     A=_pad_cols(row(A), LANES), expand=expand,
                     d_skip=row(jnp.repeat(ssd_d[l], SSD_HEAD_DIM)), norm_w=row(ssd_norm[l]))
        y_ssd = ssd_mixer(h, w_ssd, ssd_p)

        w_fox = jnp.concatenate([w_f[:, :3 * FOX_WIDTH], _pad_cols(w_f[:, 3 * FOX_WIDTH:], LANES)],
                                axis=1).astype(BF16)
        bound2 = (1.02 * FOX_HEAD_DIM ** 0.5 * LOG2E) * jnp.max(jnp.abs(fox_q_gain[l])) * jnp.max(jnp.abs(fox_k_gain[l]))
        fox_p = dict(f_bias=_pad_cols(row(fox_f_bias[l]), LANES),
                     q_gain=row(jnp.tile(fox_q_gain[l], FOX_HEADS)) * (FOX_HEAD_DIM ** -0.5 * LOG2E),
                     k_gain=row(jnp.tile(fox_k_gain[l], FOX_HEADS)), seg=seg64, bound2=bound2)
        y_fox = fox_branch(h, w_fox, fox_p)

        mp = dict(w_gate=w_g.astype(BF16), gate_b=row(gate_b[l]), lnx_w=row(rw_lnx_w[l]), lnx_b=row(rw_lnx_b[l]),
                  proj_rw=proj_rw[l].astype(BF16), proj_ssd=proj_ssd[l].astype(BF16),
                  proj_fox=proj_fox[l].astype(BF16), w_out=w_out[l].astype(BF16), g_m=g_m,
                  norm_gain=row(norm_ffn[l]), sc_f=sc_f, sh_f=sh_f)

        if l % 2 == 0:
            xs, hf = merge(xs, h, yn, bonus, g_rw, y_ssd, y_fox, mp)
            next_norm = None
            if l + 1 < depth:
                next_norm = (row(norm_mix[l + 1]), mods[l + 1][1], mods[l + 1][0])
            xs, h = ffn_dense(hf, xs, ffn_w_gu[l // 2].astype(BF16), ffn_w_down[l // 2].astype(BF16), g_f,
                              next_norm)
        else:
            xs, hf, comb = merge(xs, h, yn, bonus, g_rw, y_ssd, y_fox, mp,
                                 router=_pad_cols(moe_router[l // 2], LANES))
            xs = ffn_moe(hf, xs, comb, moe_w_gu[l // 2].astype(BF16), moe_w_down[l // 2].astype(BF16), g_f)
            h = None
    return xs[None]
```

```python
import functools
import math

import jax
import jax.numpy as jnp
from jax import lax
from jax.experimental import pallas as pl
from jax.experimental.pallas import tpu as pltpu

F32 = jnp.float32
BF16 = jnp.bfloat16
HIGHEST = lax.Precision.HIGHEST

D_MODEL = 1024
LANES = 128
SUBLANES = 8

RW_HEADS = 8
RW_HEAD_DIM = 64
RW_WIDTH = RW_HEADS * RW_HEAD_DIM
RW_DECAY_LORA = 64
RW_AAA_LORA = 64
RW_VRES_LORA = 32
RW_GATE_LORA = 160
GN_EPS = 64e-5
RW_OFF_WLO = 3 * RW_WIDTH
RW_OFF_ALO = RW_OFF_WLO + LANES
RW_OFF_GLO = RW_OFF_ALO + LANES
RW_OFF_HV = RW_OFF_GLO + 2 * LANES
RW_PAD_COLS = RW_OFF_HV + LANES
RW_CHUNK = 64

SSD_HEADS = 16
SSD_HEAD_DIM = 64
SSD_WIDTH = SSD_HEADS * SSD_HEAD_DIM
SSD_GROUPS = 4
SSD_STATE = 128
SSD_CONV = 4
SSD_CHUNK = 128
SSD_XBC = SSD_WIDTH + 2 * SSD_GROUPS * SSD_STATE
SSD_GROUP_WIDTH = SSD_WIDTH // SSD_GROUPS

FOX_HEADS = 8
FOX_HEAD_DIM = 64
FOX_WIDTH = FOX_HEADS * FOX_HEAD_DIM
FOX_TQ = 512
FOX_TK = 512
LOG2E = 1.4426950408889634

FFN_DENSE = 2816
N_EXPERTS = 8
TOP_K = 2
FFN_EXPERT = 3584
MOE_SB = 64
MOE_CB = 512
EPS = 1e-6

VMEM_LIMIT = 56 * 1024 * 1024


def _cparams(*sem):
    return pltpu.CompilerParams(dimension_semantics=sem, vmem_limit_bytes=VMEM_LIMIT)


def _sigmoid(x):
    return 1.0 / (1.0 + jnp.exp(-x))


def _softplus(x):
    return jnp.maximum(x, 0.0) + jnp.log(1.0 + jnp.exp(-jnp.abs(x)))


def _dot(a, b):
    return jnp.dot(a, b, preferred_element_type=F32)


def _dot_bf16(a, b):
    return jnp.dot(a.astype(BF16), b.astype(BF16), preferred_element_type=F32)


def _dot_hi(a, b):
    return jnp.dot(a, b, precision=HIGHEST, preferred_element_type=F32)


def _dot_nt(a, b, precision=None):
    return lax.dot_general(a, b, (((1,), (1,)), ((), ())), precision=precision, preferred_element_type=F32)


def _dot_tn(a, b, precision=None):
    return lax.dot_general(a, b, (((0,), (0,)), ((), ())), precision=precision, preferred_element_type=F32)


def _split_dot(x, ones_bf16):
    return _dot(x.astype(BF16), ones_bf16)


def _split3(x):
    hi = x.astype(BF16)
    r1 = x - hi.astype(F32)
    mid = r1.astype(BF16)
    return hi, mid, (r1 - mid.astype(F32)).astype(BF16)


def _tril(n, strict=False):
    r = lax.broadcasted_iota(jnp.int32, (n, n), 0)
    c = lax.broadcasted_iota(jnp.int32, (n, n), 1)
    return (r > c) if strict else (r >= c)


def _mod_kernel(c_ref, w_ref, b_ref, o_ref):
    c = c_ref[...]
    o_ref[...] = _dot_hi(c * _sigmoid(c), w_ref[...]) + b_ref[...]


def adaln_mod(c, w, b):
    d, n = w.shape
    tn = 1024
    c8 = jnp.broadcast_to(c, (SUBLANES, d))
    out = pl.pallas_call(
        _mod_kernel,
        grid=(n // tn,),
        in_specs=[pl.BlockSpec((SUBLANES, d), lambda j: (0, 0)),
                  pl.BlockSpec((d, tn), lambda j: (0, j)),
                  pl.BlockSpec((1, tn), lambda j: (0, j))],
        out_specs=pl.BlockSpec((SUBLANES, tn), lambda j: (0, j)),
        out_shape=jax.ShapeDtypeStruct((SUBLANES, n), F32),
        compiler_params=_cparams("arbitrary"),
    )(c8, w, b.reshape(1, n))
    return out[:1]


def _norm_mod(x, gain, sc, sh):
    y = x * lax.rsqrt(jnp.mean(x * x, axis=-1, keepdims=True) + EPS)
    return y * gain * (1.0 + sc) + sh


def _norm_kernel(x_ref, gain_ref, sc_ref, sh_ref, h_ref):
    h_ref[...] = _norm_mod(x_ref[...], gain_ref[...], sc_ref[...], sh_ref[...]).astype(h_ref.dtype)


def norm_mod(x, gain, sc, sh, tm=512):
    L, d = x.shape
    row = pl.BlockSpec((1, d), lambda i: (0, 0))
    return pl.pallas_call(
        _norm_kernel,
        grid=(L // tm,),
        in_specs=[pl.BlockSpec((tm, d), lambda i: (i, 0)), row, row, row],
        out_specs=pl.BlockSpec((tm, d), lambda i: (i, 0)),
        out_shape=jax.ShapeDtypeStruct((L, d), BF16),
        compiler_params=_cparams("arbitrary"),
    )(x, gain, sc, sh)


def _rw_pre_kernel(has_vres, *refs):
    if has_vres:
        (h_ref, w_ref, mu_ref, w0_ref, wup_ref, a0_ref, aup_ref, gup_ref, kk_ref, ka_ref, rk_ref, seg_ref,
         vfirst_ref, v0_ref, vup_ref,
         r_out, k_out, v_out, lw_out, a_out, b_out, g_out, bonus_out, ext_scr) = refs
    else:
        (h_ref, w_ref, mu_ref, w0_ref, wup_ref, a0_ref, aup_ref, gup_ref, kk_ref, ka_ref, rk_ref, seg_ref,
         r_out, k_out, v_out, lw_out, a_out, b_out, g_out, bonus_out, ext_scr) = refs
    T = h_ref.shape[0]

    @pl.when(pl.program_id(0) == 0)
    def _():
        ext_scr[0:SUBLANES, :] = jnp.zeros((SUBLANES, ext_scr.shape[1]), F32)

    cur = _dot(h_ref[...], w_ref[...])
    ext_scr[SUBLANES:SUBLANES + T, :] = cur
    prev = ext_scr[SUBLANES - 1:SUBLANES - 1 + T, :]
    ext_scr[0:SUBLANES, :] = cur[T - SUBLANES:T, :]
    s = cur + (prev - cur) * mu_ref[...]

    W = RW_WIDTH
    r = s[:, 0:W]
    k = s[:, W:2 * W]
    v = s[:, 2 * W:3 * W]
    w_lo = s[:, RW_OFF_WLO:RW_OFF_WLO + LANES]
    a_lo = s[:, RW_OFF_ALO:RW_OFF_ALO + LANES]
    g_lo = s[:, RW_OFF_GLO:RW_OFF_GLO + 2 * LANES]
    seg = seg_ref[...]

    wlog = -_softplus(-(w0_ref[...] + _dot_bf16(jnp.tanh(w_lo), wup_ref[...]))) - 0.5
    a = _sigmoid(a0_ref[...] + _dot_bf16(a_lo, aup_ref[...]))
    g = _dot_bf16(_sigmoid(g_lo), gup_ref[...])
    if has_vres:
        hv = s[:, RW_OFF_HV:RW_OFF_HV + LANES]
        v = v + (vfirst_ref[...].astype(F32) - v) * _sigmoid(v0_ref[...] + _dot_bf16(hv, vup_ref[...]))
    kk = k * kk_ref[...]
    kk = kk / jnp.maximum(jnp.sqrt(_split_dot(kk * kk, seg)), 1e-12)
    k = k * (1.0 + (a - 1.0) * ka_ref[...])
    r_out[...] = r.astype(r_out.dtype)
    k_out[...] = k.astype(k_out.dtype)
    v_out[...] = v.astype(v_out.dtype)
    lw_out[...] = -jnp.exp(wlog)
    a_out[...] = (-kk).astype(a_out.dtype)
    b_out[...] = (kk * a).astype(b_out.dtype)
    g_out[...] = g.astype(g_out.dtype)
    bonus_out[...] = (_split_dot(r * k * rk_ref[...], seg) * v).astype(bonus_out.dtype)


def rw_pre(h, w_rw, p, v_first, tm=512):
    L, D = h.shape
    has_vres = v_first is not None
    full = lambda a: pl.BlockSpec(a.shape, lambda i: (0,) * a.ndim)
    rows = lambda n: pl.BlockSpec((tm, n), lambda i: (i, 0))
    args = [h, w_rw, p["mu"], p["w0"], p["w_up"], p["a0"], p["a_up"], p["g_up"], p["k_k"], p["k_a"], p["r_k"],
            p["seg"]]
    specs = [rows(D)] + [full(a) for a in args[1:]]
    if has_vres:
        args += [v_first, p["v0"], p["v_up"]]
        specs += [rows(RW_WIDTH), full(p["v0"]), full(p["v_up"])]
    out_dtypes = [BF16, BF16, BF16, F32, BF16, BF16, BF16, BF16]
    return pl.pallas_call(
        functools.partial(_rw_pre_kernel, has_vres),
        grid=(L // tm,),
        in_specs=specs,
        out_specs=[rows(RW_WIDTH)] * 8,
        out_shape=[jax.ShapeDtypeStruct((L, RW_WIDTH), dt) for dt in out_dtypes],
        scratch_shapes=[pltpu.VMEM((tm + SUBLANES, RW_PAD_COLS), F32)],
        compiler_params=_cparams("arbitrary"),
    )(*args)


def _wkv_kernel(r_ref, lw_ref, k_ref, v_ref, a_ref, b_ref, y_ref, h_scr):
    C = RW_CHUNK
    C2 = 2 * C
    n_chunks = r_ref.shape[0] // C

    @pl.when(pl.program_id(1) == 0)
    def _():
        h_scr[...] = jnp.zeros_like(h_scr)

    is_a = lax.broadcasted_iota(jnp.int32, (C, LANES), 1) < RW_HEAD_DIM
    ri = lax.broadcasted_iota(jnp.int32, (C2, C2), 0)
    ci = lax.broadcasted_iota(jnp.int32, (C2, C2), 1)
    same = (ri >= C) == (ci >= C)
    strict = jnp.logical_and(same, ri > ci)
    incl = jnp.logical_and(same, ri >= ci)
    eye2 = (ri == ci).astype(F32)
    tri_c = _tril(C).astype(BF16)

    def two(x):
        return jnp.concatenate([jnp.where(is_a, x, 0.0), jnp.where(is_a, 0.0, x)], axis=0)

    chunks = range(n_chunks)
    at2, rt2, bh2, kh2, v2, n_ab, a_ak, m_rb, m_rk, p_end = ([] for _ in range(10))
    for c in chunks:
        sl = pl.ds(c * C, C)
        lw = lw_ref[sl, :]
        lw_hi = lw.astype(BF16)
        lw_lo = (lw - lw_hi.astype(F32)).astype(BF16)
        cs = _dot(tri_c, lw_hi) + _dot(tri_c, lw_lo)
        cs_end = cs[C - 1:C, :]
        e_neg = jnp.exp(-cs)
        e_end = jnp.exp(cs_end - cs)
        a = a_ref[sl, :].astype(F32)
        b = b_ref[sl, :].astype(F32)
        k = k_ref[sl, :].astype(F32)
        at2.append(two(a * jnp.exp(cs - lw)))
        rt2.append(two(r_ref[sl, :].astype(F32) * jnp.exp(cs)))
        bh2.append(two(b * e_end))
        kh2.append(two(k * e_end))
        v2.append(two(v_ref[sl, :].astype(F32)))
        p_end.append(jnp.exp(cs_end))
        quad = _dot_nt(jnp.concatenate([at2[c], rt2[c]], axis=0).astype(BF16),
                       jnp.concatenate([two(b * e_neg), two(k * e_neg)], axis=0).astype(BF16))
        n_ab.append(jnp.where(strict, quad[0:C2, 0:C2], 0.0))
        a_ak.append(jnp.where(strict, quad[0:C2, C2:], 0.0))
        m_rb.append(jnp.where(incl, quad[C2:, 0:C2], 0.0))
        m_rk.append(jnp.where(incl, quad[C2:, C2:], 0.0))
    t_inv = [eye2 + n for n in n_ab]
    n_pow = n_ab
    n_pow = [_dot_bf16(n, n) for n in n_pow]
    for _ in range(int(math.log2(C)) - 2):
        both = [_dot_bf16(jnp.concatenate([n, t], axis=0), n) for n, t in zip(n_pow, t_inv)]
        t_inv = [t + b[C2:, :] for t, b in zip(t_inv, both)]
        n_pow = [b[:C2, :] for b in both]
    t_inv = [t + _dot_bf16(t, n) for t, n in zip(t_inv, n_pow)]
    akv = [_dot_bf16(a_ak[c], v2[c]) for c in chunks]
    au = [_dot_bf16(t_inv[c], jnp.concatenate([at2[c], akv[c]], axis=1)) for c in chunks]
    mau = [_dot_bf16(m_rb[c], au[c]) for c in chunks]
    mkv = [_dot_bf16(m_rk[c], v2[c]) for c in chunks]
    gu = [_dot_tn(bh2[c].astype(BF16), au[c].astype(BF16)) for c in chunks]
    kv = [_dot_tn(kh2[c].astype(BF16), v2[c].astype(BF16)) for c in chunks]
    inv_n = 1.0 / RW_HEAD_DIM
    h = h_scr[...]
    for c in chunks:
        rh2 = rt2[c] + mau[c][:, :LANES]
        g_mat = eye2 * p_end[c] + gu[c][:, :LANES]
        yh = _dot_bf16(jnp.concatenate([rh2, g_mat], axis=0), h)
        h = yh[C2:, :] + gu[c][:, LANES:] + kv[c]
        y2 = yh[:C2, :] + mau[c][:, LANES:] + mkv[c]
        y = y2[:C, :] + y2[C:, :]
        s_all = jnp.sum(y, axis=-1, keepdims=True)
        s_a = jnp.sum(jnp.where(is_a, y, 0.0), axis=-1, keepdims=True)
        yc = y - jnp.where(is_a, s_a, s_all - s_a) * inv_n
        sq = yc * yc
        q_all = jnp.sum(sq, axis=-1, keepdims=True)
        q_a = jnp.sum(jnp.where(is_a, sq, 0.0), axis=-1, keepdims=True)
        yn = yc * lax.rsqrt(jnp.where(is_a, q_a, q_all - q_a) * inv_n + GN_EPS)
        y_ref[pl.ds(c * C, C), :] = yn.astype(y_ref.dtype)
    h_scr[...] = h


def wkv7(r, lw, k, v, a, b, rows=2048):
    L, W = r.shape
    spec = pl.BlockSpec((rows, LANES), lambda p, i: (i, p))
    return pl.pallas_call(
        _wkv_kernel,
        grid=(W // LANES, L // rows),
        in_specs=[spec] * 6,
        out_specs=spec,
        out_shape=jax.ShapeDtypeStruct((L, W), BF16),
        scratch_shapes=[pltpu.VMEM((LANES, LANES), F32)],
        compiler_params=_cparams("arbitrary", "arbitrary"),
    )(r, lw, k, v, a, b)


def _ssd_kernel(h_ref, w_ref, cw_ref, cb_ref, bias_ref, A_ref, exp_ref, dskip_ref, nw_ref, o_ref,
                ext_scr, xbc_scr, s_scr):
    Q = SSD_CHUNK
    GW = SSD_GROUP_WIDTH
    NS = SSD_STATE
    R = h_ref.shape[0]

    @pl.when(pl.program_id(0) == 0)
    def _():
        s_scr[...] = jnp.zeros_like(s_scr)
        ext_scr[0:SUBLANES, :] = jnp.zeros((SUBLANES, ext_scr.shape[1]), F32)

    cols = _dot(h_ref[...], w_ref[...])
    ext_scr[SUBLANES:SUBLANES + R, :] = cols[:, 0:SSD_XBC]
    acc = cols[:, 0:SSD_XBC] * cw_ref[SSD_CONV - 1:SSD_CONV, :] + cb_ref[...]
    for j in range(1, SSD_CONV):
        acc = acc + ext_scr[SUBLANES - j:SUBLANES - j + R, :] * cw_ref[SSD_CONV - 1 - j:SSD_CONV - j, :]
    ext_scr[0:SUBLANES, :] = ext_scr[R:R + SUBLANES, :]
    xbc_scr[...] = acc * _sigmoid(acc)

    incl = _tril(Q)
    tri = incl.astype(BF16)
    triu = jnp.logical_not(_tril(Q, strict=True)).astype(BF16)
    expand = exp_ref[...]
    first_half = lax.broadcasted_iota(jnp.int32, (Q, LANES), 1) < SSD_HEAD_DIM
    heads_per_group = SSD_HEADS // SSD_GROUPS
    for c in range(R // Q):
        rows = slice(c * Q, (c + 1) * Q)
        xs = xbc_scr[rows, 0:SSD_WIDTH]
        z = cols[rows, SSD_XBC:SSD_XBC + SSD_WIDTH]
        dt = _softplus(cols[rows, SSD_XBC + SSD_WIDTH:] + bias_ref[...])
        a3 = _split3(dt * A_ref[...])
        a_cum = sum(_dot(tri, t) for t in a3)
        a_cumT = sum(_dot_tn(t, triu) for t in a3)
        a_end = a_cum[Q - 1:Q, :]
        dt_x = _dot(dt.astype(BF16), expand)
        eac_x = _dot(jnp.exp(a_cum).astype(BF16), expand)
        dte_x = _dot(jnp.exp(a_end - a_cum).astype(BF16), expand)
        cd_x = sum(_dot(t, expand) for t in _split3(jnp.broadcast_to(jnp.exp(a_end), (SUBLANES, LANES))))[0:1, :]
        xdt = xs * dt_x
        for g in range(SSD_GROUPS):
            Bg = xbc_scr[rows, SSD_WIDTH + g * NS:SSD_WIDTH + (g + 1) * NS].astype(BF16)
            Cg = xbc_scr[rows, SSD_WIDTH + (SSD_GROUPS + g) * NS:SSD_WIDTH + (SSD_GROUPS + g + 1) * NS].astype(BF16)
            cb = _dot_nt(Cg, Bg)
            gs = slice(g * GW, (g + 1) * GW)
            s_prev = s_scr[:, gs]
            y_g = _dot_bf16(Cg, s_prev) * eac_x[:, gs]
            pieces = []
            for pr in range(heads_per_group // 2):
                ps = slice(g * GW + pr * LANES, g * GW + (pr + 1) * LANES)
                xdt_p = xdt[:, ps]
                acc_p = None
                for e in range(2):
                    hd = g * heads_per_group + pr * 2 + e
                    seg = a_cum[:, hd:hd + 1] - a_cumT[hd:hd + 1, :]
                    m = cb * jnp.exp(jnp.where(incl, seg, -jnp.inf))
                    xm = jnp.where(first_half if e == 0 else jnp.logical_not(first_half), xdt_p, 0.0)
                    t = _dot_bf16(m, xm)
                    acc_p = t if acc_p is None else acc_p + t
                pieces.append(acc_p)
            y_g = y_g + jnp.concatenate(pieces, axis=1)
            s_scr[:, gs] = s_prev * cd_x[:, gs] +_dot_tn(Bg, (dte_x[:, gs] * xdt[:, gs]).astype(BF16))
            y_g = y_g + dskip_ref[:, gs] * xs[:, gs]
            zg = z[:, gs]
            y_g = y_g * (zg * _sigmoid(zg))
            y_g = y_g * lax.rsqrt(jnp.mean(y_g * y_g, axis=-1, keepdims=True) + EPS)
            o_ref[rows, gs] = (y_g * nw_ref[:, gs]).astype(o_ref.dtype)


def ssd_mixer(h, w_ssd, p, rows=512):
    L, D = h.shape
    N = w_ssd.shape[1]
    full = lambda a: pl.BlockSpec(a.shape, lambda i: (0,) * a.ndim)
    small = [p["conv_w"], p["conv_b"], p["dt_bias"], p["A"], p["expand"], p["d_skip"], p["norm_w"]]
    return pl.pallas_call(
        _ssd_kernel,
        grid=(L // rows,),
        in_specs=[pl.BlockSpec((rows, D), lambda i: (i, 0)), full(w_ssd)] + [full(a) for a in small],
        out_specs=pl.BlockSpec((rows, SSD_WIDTH), lambda i: (i, 0)),
        out_shape=jax.ShapeDtypeStruct((L, SSD_WIDTH), BF16),
        scratch_shapes=[pltpu.VMEM((rows + SUBLANES, SSD_XBC), F32), pltpu.VMEM((rows, SSD_XBC), F32),
                        pltpu.VMEM((SSD_STATE, SSD_WIDTH), F32)],
        compiler_params=_cparams("arbitrary"),
    )(h, w_ssd, *small)


def _fox_pre_kernel(h_ref, w_ref, fb_ref, qg_ref, kg_ref, seg_ref, q_out, k_out, v_out, f_out, carry_scr):
    T = h_ref.shape[0]
    W = FOX_WIDTH

    @pl.when(pl.program_id(0) == 0)
    def _():
        carry_scr[...] = jnp.zeros_like(carry_scr)

    cols = _dot(h_ref[...], w_ref[...])
    seg = seg_ref[...]
    q = cols[:, 0:W]
    k = cols[:, W:2 * W]
    inv_d = 1.0 / FOX_HEAD_DIM
    qn = q * lax.rsqrt(_split_dot(q * q, seg) * inv_d + EPS) * qg_ref[...]
    kn = k * lax.rsqrt(_split_dot(k * k, seg) * inv_d + EPS) * kg_ref[...]
    q_out[...] = qn.astype(q_out.dtype)
    k_out[...] = kn.astype(k_out.dtype)
    v_out[...] = cols[:, 2 * W:3 * W].astype(v_out.dtype)
    f = cols[:, 3 * W:3 * W + LANES] + fb_ref[...]
    ls = -_softplus(-f)
    cum = _dot_hi(_tril(T).astype(F32), ls) + carry_scr[0:1, :]
    f_out[...] = cum * LOG2E
    carry_scr[...] = jnp.broadcast_to(cum[T - 1:T, :], carry_scr.shape)


def fox_pre(h, w_fox, p, tm=512):
    L, D = h.shape
    full = lambda a: pl.BlockSpec(a.shape, lambda i: (0,) * a.ndim)
    rows = lambda n: pl.BlockSpec((tm, n), lambda i: (i, 0))
    small = [w_fox, p["f_bias"], p["q_gain"], p["k_gain"], p["seg"]]
    return pl.pallas_call(
        _fox_pre_kernel,
        grid=(L // tm,),
        in_specs=[rows(D)] + [full(a) for a in small],
        out_specs=[rows(FOX_WIDTH)] * 3 + [rows(LANES)],
        out_shape=[jax.ShapeDtypeStruct((L, FOX_WIDTH), BF16)] * 3 + [jax.ShapeDtypeStruct((L, LANES), F32)],
        scratch_shapes=[pltpu.VMEM((SUBLANES, LANES), F32)],
        compiler_params=_cparams("arbitrary"),
    )(h, *small)


def _fox_kernel(lo_ref, q_ref, k_ref, v_ref, fk_ref, o_ref, m_scr, acc_scr, s0_scr, s1_scr):
    tq = q_ref.shape[0]
    tk = FOX_TK
    pair = pl.program_id(0)
    qi = pl.program_id(1)
    is_a = lax.broadcasted_iota(jnp.int32, (tq, LANES), 1) < FOX_HEAD_DIM
    q = q_ref[...]
    zero = jnp.zeros_like(q)
    qs = (jnp.where(is_a, q, zero), jnp.where(is_a, zero, q))
    m_scr[...] = jnp.full(m_scr.shape, -jnp.inf, F32)
    key_is_a = lax.broadcasted_iota(jnp.int32, (tk, LANES), 1) < FOX_HEAD_DIM
    acc_scr[...] = jnp.zeros_like(acc_scr)
    rel = lax.broadcasted_iota(jnp.int32, (tq, tk), 0) - lax.broadcasted_iota(jnp.int32, (tq, tk), 1)
    last = (qi * tq) // tk

    def fill(s_ref, kb, diagonal=False):
        ks = pl.ds(pl.multiple_of(kb * tk, tk), tk)
        k_blk = k_ref[ks, :]
        for e in range(2):
            s = _dot_nt(qs[e], k_blk) - fk_ref[pl.ds(2 * pair + e, 1), ks]
            if diagonal:
                s = jnp.where(rel >= kb * tk - qi * tq, s, -jnp.inf)
            s_ref[e] = s

    def consume(s_ref, kb):
        v_blk = v_ref[pl.ds(pl.multiple_of(kb * tk, tk), tk), :]
        one = jnp.ones_like(v_blk)
        v_one = (jnp.where(key_is_a, v_blk, one), jnp.where(key_is_a, one, v_blk))
        for e in range(2):
            s = s_ref[e]
            m_old = m_scr[e]
            m_new = jnp.maximum(m_old, jnp.max(s, axis=-1, keepdims=True))
            pexp = jnp.exp2(s - m_new)
            acc_scr[e] = jnp.exp2(m_old - m_new) * acc_scr[e] + _dot(pexp.astype(BF16), v_one[e])
            m_scr[e] = m_new

    n = last - jnp.minimum(lo_ref[2 * pair, qi], lo_ref[2 * pair + 1, qi])
    fill(s0_scr, last, diagonal=True)

    def two_steps(u, carry):
        kb = last - 2 * u
        fill(s1_scr, kb - 1)
        consume(s0_scr, kb)
        fill(s0_scr, kb - 2)
        consume(s1_scr, kb - 1)
        return carry

    lax.fori_loop(0, n // 2, two_steps, 0)

    @pl.when(n % 2 == 1)
    def _():
        fill(s1_scr, last - n)
        consume(s0_scr, last - n + 1)
        consume(s1_scr, last - n)

    @pl.when(n % 2 == 0)
    def _():
        consume(s0_scr, last - n)

    acc_a = acc_scr[0]
    acc_b = acc_scr[1]
    half = FOX_HEAD_DIM
    o_ref[...] = jnp.where(is_a, acc_a / pltpu.roll(acc_a, half, 1),
                           acc_b / pltpu.roll(acc_b, half, 1)).astype(o_ref.dtype)


def fox_attention(q, k, v, FT, lo):
    L, W = q.shape
    H = FT.shape[0]
    tq = FOX_TQ
    grid_spec = pltpu.PrefetchScalarGridSpec(
        num_scalar_prefetch=1,
        grid=(W // LANES, L // tq),
        in_specs=[pl.BlockSpec((tq, LANES), lambda p, i, lo_r: (i, p)),
                  pl.BlockSpec((L, LANES), lambda p, i, lo_r: (0, p)),
                  pl.BlockSpec((L, LANES), lambda p, i, lo_r: (0, p)),
                  pl.BlockSpec((H, L), lambda p, i, lo_r: (0, 0))],
        out_specs=pl.BlockSpec((tq, LANES), lambda p, i, lo_r: (i, p)),
        scratch_shapes=[pltpu.VMEM((2, tq, 1), F32), pltpu.VMEM((2, tq, LANES), F32),
                        pltpu.VMEM((2, tq, FOX_TK), F32), pltpu.VMEM((2, tq, FOX_TK), F32)],
    )
    return pl.pallas_call(
        _fox_kernel,
        grid_spec=grid_spec,
        out_shape=jax.ShapeDtypeStruct((L, W), BF16),
        compiler_params=_cparams("arbitrary", "arbitrary"),
    )(lo, q, k, v, FT)


def fox_first_block(FT, bound2):
    f_first = FT[:, ::FOX_TQ]
    f_last = FT[:, FOX_TK - 1::FOX_TK]
    gap = f_first[:, :, None] - f_last[:, None, :] + 2.0 * bound2
    last = (jnp.arange(f_first.shape[1]) * FOX_TQ) // FOX_TK
    needed = jnp.logical_or(gap >= -152.0, jnp.arange(f_last.shape[1])[None, None, :] >= last[None, :, None])
    return jnp.argmax(needed, axis=-1).astype(jnp.int32)


def _top2_combine(h, wr):
    T = h.shape[0]
    lane = lax.broadcasted_iota(jnp.int32, (T, LANES), 1)
    h_hi = h.astype(BF16)
    h_lo = (h - h_hi.astype(F32)).astype(BF16)
    w_hi = wr.astype(BF16)
    w_lo = (wr - w_hi.astype(F32)).astype(BF16)
    logits = _dot(h_hi, w_hi) + (_dot(h_hi, w_lo) + _dot(h_lo, w_hi))
    logits = jnp.where(lane < N_EXPERTS, logits, -jnp.inf)
    m1 = jnp.max(logits, axis=-1, keepdims=True)
    i1 = jnp.min(jnp.where(logits == m1, lane, LANES), axis=-1, keepdims=True)
    rest = jnp.where(lane == i1, -jnp.inf, logits)
    m2 = jnp.max(rest, axis=-1, keepdims=True)
    i2 = jnp.min(jnp.where(rest == m2, lane, LANES), axis=-1, keepdims=True)
    e2 = jnp.exp(m2 - m1)
    return jnp.where(lane == i1, 1.0 / (1.0 + e2), 0.0) + jnp.where(lane == i2, e2 / (1.0 + e2), 0.0)


def _merge_kernel(with_router, *refs):
    (x_ref, hm_ref, yn_ref, bonus_ref, g_ref, yssd_ref, yfox_ref, wgate_ref, gb_ref, lnw_ref, lnb_ref,
     prw_ref, pssd_ref, pfox_ref, wout_ref, gm_ref, gain_ref, sc_ref, sh_ref) = refs[:19]
    D = D_MODEL
    hm = hm_ref[...]
    y_rw = (yn_ref[...] * lnw_ref[...] + lnb_ref[...] + bonus_ref[...]) * g_ref[...]
    merged = None
    for b, (y_b, proj_ref) in enumerate(((y_rw, prw_ref), (yssd_ref[...], pssd_ref), (yfox_ref[...], pfox_ref))):
        cols = slice(b * D, (b + 1) * D)
        gate = _sigmoid(_dot(hm, wgate_ref[:, cols]) + gb_ref[:, cols])
        term = gate * _dot_bf16(y_b, proj_ref[...])
        merged = term if merged is None else merged + term
    x_new = x_ref[...] + gm_ref[...] * _dot_bf16(merged, wout_ref[...])
    hf = _norm_mod(x_new, gain_ref[...], sc_ref[...], sh_ref[...])
    if with_router:
        wr_ref, o_ref, h_ref, comb_ref = refs[19:]
        comb_ref[...] = _top2_combine(hf, wr_ref[...])
    else:
        o_ref, h_ref = refs[19:]
    o_ref[...] = x_new
    h_ref[...] = hf.astype(h_ref.dtype)


def merge(x, hm, yn, bonus, g, y_ssd, y_fox, p, router=None, tm=256):
    L, D = x.shape
    full = lambda a: pl.BlockSpec(a.shape, lambda i: (0,) * a.ndim)
    rows = lambda n: pl.BlockSpec((tm, n), lambda i: (i, 0))
    small = [p["w_gate"], p["gate_b"], p["lnx_w"], p["lnx_b"], p["proj_rw"], p["proj_ssd"], p["proj_fox"],
             p["w_out"], p["g_m"], p["norm_gain"], p["sc_f"], p["sh_f"]]
    out_specs = [rows(D), rows(D)]
    out_shape = [jax.ShapeDtypeStruct((L, D), F32), jax.ShapeDtypeStruct((L, D), BF16)]
    if router is not None:
        small.append(router)
        out_specs.append(rows(LANES))
        out_shape.append(jax.ShapeDtypeStruct((L, LANES), F32))
    return pl.pallas_call(
        functools.partial(_merge_kernel, router is not None),
        grid=(L // tm,),
        in_specs=[rows(D), rows(D), rows(RW_WIDTH), rows(RW_WIDTH), rows(RW_WIDTH), rows(SSD_WIDTH),
                  rows(FOX_WIDTH)] + [full(a) for a in small],
        out_specs=out_specs,
        out_shape=out_shape,
        compiler_params=_cparams("arbitrary"),
    )(x, hm, yn, bonus, g, y_ssd, y_fox, *small)


def _ffn_kernel(with_next, *refs):
    h_ref, x_ref, wg_ref, wu_ref, wd_ref, gf_ref = refs[:6]
    j = pl.program_id(1)
    acc_scr = refs[-1]

    @pl.when(j == 0)
    def _():
        acc_scr[...] = jnp.zeros_like(acc_scr)

    h = h_ref[...]
    gte = _dot(h, wg_ref[...])
    up = _dot(h, wu_ref[...])
    act = gte * _sigmoid(gte) * up
    acc_scr[...] += _dot(act.astype(BF16), wd_ref[...])

    @pl.when(j == pl.num_programs(1) - 1)
    def _():
        x_new = x_ref[...] + gf_ref[...] * acc_scr[...]
        if with_next:
            gain_ref, sc_ref, sh_ref, o_ref, hn_ref = refs[6:11]
            hn_ref[...] = _norm_mod(x_new, gain_ref[...], sc_ref[...], sh_ref[...]).astype(hn_ref.dtype)
        else:
            o_ref = refs[6]
        o_ref[...] = x_new


def ffn_dense(h, x, w_gu, w_down, g_f, next_norm=None, tm=512, tf=1408):
    L, D = x.shape
    Fh = w_down.shape[0]
    nf = Fh // tf
    row = pl.BlockSpec((1, D), lambda i, j: (0, 0))
    tile = pl.BlockSpec((tm, D), lambda i, j: (i, 0))
    args = [h, x, w_gu, w_gu, w_down, g_f]
    in_specs = [tile, tile,
                pl.BlockSpec((D, tf), lambda i, j: (0, j)),
                pl.BlockSpec((D, tf), lambda i, j: (0, j + nf)),
                pl.BlockSpec((tf, D), lambda i, j: (j, 0)), row]
    out_specs = [tile]
    out_shape = [jax.ShapeDtypeStruct((L, D), F32)]
    if next_norm is not None:
        args += list(next_norm)
        in_specs += [row, row, row]
        out_specs.append(tile)
        out_shape.append(jax.ShapeDtypeStruct((L, D), BF16))
    out = pl.pallas_call(
        functools.partial(_ffn_kernel, next_norm is not None),
        grid=(L // tm, nf),
        in_specs=in_specs,
        out_specs=out_specs,
        out_shape=out_shape,
        scratch_shapes=[pltpu.VMEM((tm, D), F32)],
        compiler_params=_cparams("arbitrary", "arbitrary"),
    )(*args)
    return out if next_norm is not None else (out[0], None)


def _moe_slots(slot_scr, comb_ref):
    slotval = slot_scr[...]
    comb = comb_ref[...]
    s1 = jnp.max(slotval, axis=-1, keepdims=True)
    first = slotval == s1
    c1 = jnp.sum(jnp.where(first, comb, 0.0), axis=-1, keepdims=True)
    rest = jnp.where(first, -1.0, slotval)
    s2 = jnp.max(rest, axis=-1, keepdims=True)
    c2 = jnp.sum(jnp.where(jnp.logical_and(rest == s2, rest >= 0.0), comb, 0.0), axis=-1, keepdims=True)
    return s1, s2, c1, c2


def _moe_kernel(nsb_ref, off_ref, h_ref, x_ref, comb_ref, wg_ref, wu_ref, wd_ref, gf_ref, o_ref,
                slot_scr, cslot_scr, xy_scr, acc_scr):
    i = pl.program_id(0)
    e = pl.program_id(1)
    j = pl.program_id(2)
    last_j = pl.num_programs(2) - 1
    T = h_ref.shape[0]
    S = xy_scr.shape[0]
    SB = MOE_SB

    @pl.when(jnp.logical_and(e == 0, j == 0))
    def _():
        lane = lax.broadcasted_iota(jnp.int32, (LANES, LANES), 1)
        off = jnp.zeros((LANES, LANES), F32)
        for ex in range(N_EXPERTS):
            off = jnp.where(lane == ex, off_ref[i, ex].astype(F32), off)
        before = _tril(LANES, strict=True).astype(BF16)
        ones = jnp.ones((LANES, LANES), BF16)
        for rb in range(T // LANES):
            rows = slice(rb * LANES, (rb + 1) * LANES)
            sel = comb_ref[rows, :] > 0.0
            sel_b = jnp.where(sel, 1.0, 0.0).astype(BF16)
            slot_scr[rows, :] = jnp.where(sel, _dot(before, sel_b) + off, -1.0)
            off = off + _dot(ones, sel_b)
        s1, s2, c1, c2 = _moe_slots(slot_scr, comb_ref)
        used = off_ref[i, N_EXPERTS - 1] + nsb_ref[i, N_EXPERTS - 1] * SB
        for cb in range(S // MOE_CB):
            chunk = slice(cb * MOE_CB, (cb + 1) * MOE_CB)

            @pl.when(used > cb * MOE_CB)
            def _():
                slot = (lax.broadcasted_iota(jnp.int32, (T, MOE_CB), 1) + cb * MOE_CB).astype(F32)
                w = jnp.where(s1 == slot, c1, 0.0) + jnp.where(s2 == slot, c2, 0.0)
                pt = jnp.where(w > 0.0, 1.0, 0.0).astype(BF16)
                xy_scr[chunk, :] = _dot_tn(pt, h_ref[...]).astype(BF16)
                w_hi = w.astype(BF16)
                w_lo = (w - w_hi.astype(F32)).astype(BF16)
                ones_t = jnp.ones((T, LANES), BF16)
                cslot_scr[chunk, :] = _dot_tn(w_hi, ones_t) + _dot_tn(w_lo, ones_t)

            @pl.when(used <= cb * MOE_CB)
            def _():
                xy_scr[chunk, :] = jnp.zeros((MOE_CB, xy_scr.shape[1]), BF16)
                cslot_scr[chunk, :] = jnp.zeros((MOE_CB, LANES), F32)

    base = off_ref[i, e]
    nb = nsb_ref[i, e]

    def ffn_block(r0, n_rows):
        rows = pl.ds(pl.multiple_of(base + r0, SB), n_rows)
        arows = pl.ds(pl.multiple_of(r0, SB), n_rows)
        xb = xy_scr[rows, :]
        gte = _dot(xb, wg_ref[0])
        up = _dot(xb, wu_ref[0])
        part = _dot((gte * _sigmoid(gte) * up).astype(BF16), wd_ref[0])

        @pl.when(j == 0)
        def _():
            acc_scr[arows, :] = part

        @pl.when(jnp.logical_and(j > 0, j < last_j))
        def _():
            acc_scr[arows, :] += part

        @pl.when(j == last_j)
        def _():
            xy_scr[rows, :] = ((acc_scr[arows, :] + part) * cslot_scr[rows, 0:1]).astype(BF16)

    def big_body(p, carry):
        ffn_block(p * (8 * SB), 8 * SB)
        return carry

    lax.fori_loop(0, nb // 8, big_body, 0)
    rest = nb % 8
    for bit in (4, 2, 1):
        @pl.when((rest & bit) != 0)
        def _():
            ffn_block((nb - rest + (rest & ~(2 * bit - 1))) * SB, bit * SB)

    @pl.when(jnp.logical_and(e == pl.num_programs(1) - 1, j == last_j))
    def _():
        s1, s2, _, _ = _moe_slots(slot_scr, comb_ref)
        total = jnp.zeros((T, xy_scr.shape[1]), F32)
        for cb in range(S // MOE_CB):
            slot = (lax.broadcasted_iota(jnp.int32, (T, MOE_CB), 1) + cb * MOE_CB).astype(F32)
            pt = jnp.where(jnp.logical_or(s1 == slot, s2 == slot), 1.0, 0.0).astype(BF16)
            total = total + _dot(pt, xy_scr[cb * MOE_CB:(cb + 1) * MOE_CB, :])
        o_ref[...] = x_ref[...] + gf_ref[...] * total


def ffn_moe(h, x, comb, w_gu, w_down, g_f, tm=1024, tf=896):
    L, D = x.shape
    E, Fh, _ = w_down.shape
    nf = Fh // tf
    assert nf >= 2 and E == N_EXPERTS
    nt = L // tm
    counts = jnp.sum((comb[:, :E] > 0.0).reshape(nt, tm, E), axis=1, dtype=jnp.int32)
    nsb = (counts + (MOE_SB - 1)) // MOE_SB
    off = (jnp.cumsum(nsb, axis=1) - nsb) * MOE_SB
    idx = lambda f: (lambda i, e, j, n, o: f(i, e, j))
    grid_spec = pltpu.PrefetchScalarGridSpec(
        num_scalar_prefetch=2,
        grid=(nt, E, nf),
        in_specs=[pl.BlockSpec((tm, D), idx(lambda i, e, j: (i, 0))),
                  pl.BlockSpec((tm, D), idx(lambda i, e, j: (i, 0))),
                  pl.BlockSpec((tm, LANES), idx(lambda i, e, j: (i, 0))),
                  pl.BlockSpec((1, D, tf), idx(lambda i, e, j: (e, 0, j))),
                  pl.BlockSpec((1, D, tf), idx(lambda i, e, j: (e, 0, j + nf))),
                  pl.BlockSpec((1, tf, D), idx(lambda i, e, j: (e, j, 0))),
                  pl.BlockSpec((1, D), idx(lambda i, e, j: (0, 0)))],
        out_specs=pl.BlockSpec((tm, D), idx(lambda i, e, j: (i, 0))),
        scratch_shapes=[pltpu.VMEM((tm, LANES), F32),
                        pltpu.VMEM((TOP_K * tm + E * MOE_SB, LANES), F32),
                        pltpu.VMEM((TOP_K * tm + E * MOE_SB, D), BF16),
                        pltpu.VMEM((tm, D), F32)],
    )
    return pl.pallas_call(
        _moe_kernel,
        grid_spec=grid_spec,
        out_shape=jax.ShapeDtypeStruct((L, D), F32),
        compiler_params=_cparams("arbitrary", "arbitrary", "arbitrary"),
    )(nsb, off, h, x, comb, w_gu, w_gu, w_down, g_f)


def _seg_matrix(width, head_dim):
    idx = jnp.arange(width) // head_dim
    return (idx[:, None] == idx[None, :]).astype(BF16)


def _pad_cols(w, n):
    return jnp.pad(w, ((0, 0), (0, n - w.shape[1])))


def _pad_rows(w, n):
    return jnp.pad(w, ((0, n - w.shape[0]), (0, 0)))


def rwkv_branch(h, w_rw, p, v_first):
    r, k, v, lw, a, b, g, bonus = rw_pre(h, w_rw, p, v_first)
    return wkv7(r, lw, k, v, a, b), bonus, g, v


def fox_branch(h, w_fox, p):
    q, k, v, F = fox_pre(h, w_fox, p)
    FT = jnp.transpose(F[:, :FOX_HEADS])
    lo = fox_first_block(FT, p["bound2"])
    return fox_attention(q, k, v, FT, lo)


def kernel(x, c, ada_w, ada_b, norm_mix, norm_ffn, w_in, rw_mu, rw_w0, rw_w_up, rw_a0, rw_a_up, rw_g_up, rw_k_k, rw_k_a, rw_r_k, rw_lnx_w, rw_lnx_b, rw_v0, rw_v_down, rw_v_up, ssd_conv_w, ssd_conv_b, ssd_dt_bias, ssd_a_log, ssd_d, ssd_norm, fox_f_bias, fox_q_gain, fox_k_gain, gate_b, proj_rw, proj_ssd, proj_fox, w_out, ffn_w_gu, ffn_w_down, moe_router, moe_w_gu, moe_w_down):
    depth = w_in.shape[0]
    D = D_MODEL
    xs = x[0]
    row = lambda t: t.reshape(1, -1).astype(F32)
    seg64 = _seg_matrix(RW_WIDTH, RW_HEAD_DIM)
    rw_cols = 3 * RW_WIDTH + RW_DECAY_LORA + RW_AAA_LORA + RW_GATE_LORA
    ssd_cols = SSD_WIDTH + SSD_XBC + SSD_HEADS
    fox_cols = 3 * FOX_WIDTH + FOX_HEADS
    expand = (jnp.arange(LANES)[:, None] == (jnp.arange(SSD_WIDTH) // SSD_HEAD_DIM)[None, :]).astype(BF16)
    v_first = None
    mods = []
    for l in range(depth):
        mod = adaln_mod(c, ada_w[l], ada_b[l])
        mods.append([mod[:, i * D:(i + 1) * D] for i in range(6)])
    h = None
    for l in range(depth):
        sh_m, sc_m, g_m, sh_f, sc_f, g_f = mods[l]
        if h is None:
            h = norm_mod(xs, row(norm_mix[l]), sc_m, sh_m)

        wl = w_in[l]
        o = 0
        w_r = wl[:, o:o + rw_cols]; o += rw_cols
        w_s = wl[:, o:o + ssd_cols]; o += ssd_cols
        w_f = wl[:, o:o + fox_cols]; o += fox_cols
        w_g = wl[:, o:]
        W3 = 3 * RW_WIDTH
        o_a = W3 + RW_DECAY_LORA
        o_g = o_a + RW_AAA_LORA

        def rw_layout(t, hv):
            parts = [t[:, :W3], _pad_cols(t[:, W3:o_a], LANES), _pad_cols(t[:, o_a:o_g], LANES),
                     _pad_cols(t[:, o_g:], 2 * LANES), _pad_cols(hv, LANES)]
            return jnp.concatenate(parts, axis=1)

        if l == 0:
            hv_w = jnp.zeros((D, RW_VRES_LORA), F32)
        else:
            hv_w = rw_v_down[l - 1]
        w_rw = rw_layout(w_r, hv_w).astype(BF16)
        mu = rw_layout(rw_mu[l].reshape(1, -1), jnp.zeros((1, RW_VRES_LORA), F32))
        rw_p = dict(mu=mu, w0=row(rw_w0[l]), w_up=_pad_rows(rw_w_up[l], LANES).astype(BF16), a0=row(rw_a0[l]),
                    a_up=_pad_rows(rw_a_up[l], LANES).astype(BF16),
                    g_up=_pad_rows(rw_g_up[l], 2 * LANES).astype(BF16),
                    k_k=row(rw_k_k[l]), k_a=row(rw_k_a[l]), r_k=row(rw_r_k[l]), seg=seg64)
        if l > 0:
            rw_p.update(v0=row(rw_v0[l - 1]), v_up=_pad_rows(rw_v_up[l - 1], LANES).astype(BF16))
        yn, bonus, g_rw, v_cur = rwkv_branch(h, w_rw, rw_p, v_first)
        if l == 0:
            v_first = v_cur

        w_ssd = jnp.concatenate([w_s[:, SSD_WIDTH:SSD_WIDTH + SSD_XBC], w_s[:, :SSD_WIDTH],
                                 _pad_cols(w_s[:, SSD_WIDTH + SSD_XBC:], LANES)], axis=1).astype(BF16)
        A = -jnp.exp(ssd_a_log[l].astype(F32))
        ssd_p = dict(conv_w=ssd_conv_w[l], conv_b=row(ssd_conv_b[l]),
                     dt_bias=_pad_cols(row(ssd_dt_bias[l]), LANES),
                     A=_pad_cols(row(A), LANES), expand=expand,
                     d_skip=row(jnp.repeat(ssd_d[l], SSD_HEAD_DIM)), norm_w=row(ssd_norm[l]))
        y_ssd = ssd_mixer(h, w_ssd, ssd_p)

        w_fox = jnp.concatenate([w_f[:, :3 * FOX_WIDTH], _pad_cols(w_f[:, 3 * FOX_WIDTH:], LANES)],
                                axis=1).astype(BF16)
        bound2 = (1.02 * FOX_HEAD_DIM ** 0.5 * LOG2E) * jnp.max(jnp.abs(fox_q_gain[l])) * jnp.max(jnp.abs(fox_k_gain[l]))
        fox_p = dict(f_bias=_pad_cols(row(fox_f_bias[l]), LANES),
                     q_gain=row(jnp.tile(fox_q_gain[l], FOX_HEADS)) * (FOX_HEAD_DIM ** -0.5 * LOG2E),
                     k_gain=row(jnp.tile(fox_k_gain[l], FOX_HEADS)), seg=seg64, bound2=bound2)
        y_fox = fox_branch(h, w_fox, fox_p)

        mp = dict(w_gate=w_g.astype(BF16), gate_b=row(gate_b[l]), lnx_w=row(rw_lnx_w[l]), lnx_b=row(rw_lnx_b[l]),
                  proj_rw=proj_rw[l].astype(BF16), proj_ssd=proj_ssd[l].astype(BF16),
                  proj_fox=proj_fox[l].astype(BF16), w_out=w_out[l].astype(BF16), g_m=g_m,
                  norm_gain=row(norm_ffn[l]), sc_f=sc_f, sh_f=sh_f)

        if l % 2 == 0:
            xs, hf = merge(xs, h, yn, bonus, g_rw, y_ssd, y_fox, mp)
            next_norm = None
            if l + 1 < depth:
                next_norm = (row(norm_mix[l + 1]), mods[l + 1][1], mods[l + 1][0])
            xs, h = ffn_dense(hf, xs, ffn_w_gu[l // 2].astype(BF16), ffn_w_down[l // 2].astype(BF16), g_f,
                              next_norm)
        else:
            xs, hf, comb = merge(xs, h, yn, bonus, g_rw, y_ssd, y_fox, mp,
                                 router=_pad_cols(moe_router[l // 2], LANES))
            xs = ffn_moe(hf, xs, comb, moe_w_gu[l // 2].astype(BF16), moe_w_down[l // 2].astype(BF16), g_f)
            h = None
    return xs[None]
```

```python
import functools
import math

import jax
import jax.numpy as jnp
from jax import lax
from jax.experimental import pallas as pl
from jax.experimental.pallas import tpu as pltpu

F32 = jnp.float32
BF16 = jnp.bfloat16
HIGHEST = lax.Precision.HIGHEST

D_MODEL = 1024
LANES = 128
SUBLANES = 8

RW_HEADS = 8
RW_HEAD_DIM = 64
RW_WIDTH = RW_HEADS * RW_HEAD_DIM
RW_DECAY_LORA = 64
RW_AAA_LORA = 64
RW_VRES_LORA = 32
RW_GATE_LORA = 160
GN_EPS = 64e-5
RW_OFF_WLO = 3 * RW_WIDTH
RW_OFF_ALO = RW_OFF_WLO + LANES
RW_OFF_GLO = RW_OFF_ALO + LANES
RW_OFF_HV = RW_OFF_GLO + 2 * LANES
RW_PAD_COLS = RW_OFF_HV + LANES
RW_CHUNK = 64

SSD_HEADS = 16
SSD_HEAD_DIM = 64
SSD_WIDTH = SSD_HEADS * SSD_HEAD_DIM
SSD_GROUPS = 4
SSD_STATE = 128
SSD_CONV = 4
SSD_CHUNK = 128
SSD_XBC = SSD_WIDTH + 2 * SSD_GROUPS * SSD_STATE
SSD_GROUP_WIDTH = SSD_WIDTH // SSD_GROUPS

FOX_HEADS = 8
FOX_HEAD_DIM = 64
FOX_WIDTH = FOX_HEADS * FOX_HEAD_DIM
FOX_TQ = 512
FOX_TK = 512
LOG2E = 1.4426950408889634

FFN_DENSE = 2816
N_EXPERTS = 8
TOP_K = 2
FFN_EXPERT = 3584
MOE_SB = 128
MOE_CB = 512
EPS = 1e-6

VMEM_LIMIT = 56 * 1024 * 1024


def _cparams(*sem):
    return pltpu.CompilerParams(dimension_semantics=sem, vmem_limit_bytes=VMEM_LIMIT)


def _sigmoid(x):
    return 1.0 / (1.0 + jnp.exp(-x))


def _softplus(x):
    return jnp.maximum(x, 0.0) + jnp.log(1.0 + jnp.exp(-jnp.abs(x)))


def _dot(a, b):
    return jnp.dot(a, b, preferred_element_type=F32)


def _dot_bf16(a, b):
    return jnp.dot(a.astype(BF16), b.astype(BF16), preferred_element_type=F32)


def _dot_hi(a, b):
    return jnp.dot(a, b, precision=HIGHEST, preferred_element_type=F32)


def _dot_nt(a, b, precision=None):
    return lax.dot_general(a, b, (((1,), (1,)), ((), ())), precision=precision, preferred_element_type=F32)


def _dot_tn(a, b, precision=None):
    return lax.dot_general(a, b, (((0,), (0,)), ((), ())), precision=precision, preferred_element_type=F32)


def _split_dot(x, ones_bf16):
    return _dot(x.astype(BF16), ones_bf16)


def _split3(x):
    hi = x.astype(BF16)
    r1 = x - hi.astype(F32)
    mid = r1.astype(BF16)
    return hi, mid, (r1 - mid.astype(F32)).astype(BF16)


def _tril(n, strict=False):
    r = lax.broadcasted_iota(jnp.int32, (n, n), 0)
    c = lax.broadcasted_iota(jnp.int32, (n, n), 1)
    return (r > c) if strict else (r >= c)


def _mod_kernel(c_ref, w_ref, b_ref, o_ref):
    c = c_ref[...]
    o_ref[...] = _dot_hi(c * _sigmoid(c), w_ref[...]) + b_ref[...]


def adaln_mod(c, w, b):
    d, n = w.shape
    tn = 1024
    c8 = jnp.broadcast_to(c, (SUBLANES, d))
    out = pl.pallas_call(
        _mod_kernel,
        grid=(n // tn,),
        in_specs=[pl.BlockSpec((SUBLANES, d), lambda j: (0, 0)),
                  pl.BlockSpec((d, tn), lambda j: (0, j)),
                  pl.BlockSpec((1, tn), lambda j: (0, j))],
        out_specs=pl.BlockSpec((SUBLANES, tn), lambda j: (0, j)),
        out_shape=jax.ShapeDtypeStruct((SUBLANES, n), F32),
        compiler_params=_cparams("arbitrary"),
    )(c8, w, b.reshape(1, n))
    return out[:1]


def _norm_mod(x, gain, sc, sh):
    y = x * lax.rsqrt(jnp.mean(x * x, axis=-1, keepdims=True) + EPS)
    return y * gain * (1.0 + sc) + sh


def _norm_kernel(x_ref, gain_ref, sc_ref, sh_ref, h_ref):
    h_ref[...] = _norm_mod(x_ref[...], gain_ref[...], sc_ref[...], sh_ref[...]).astype(h_ref.dtype)


def norm_mod(x, gain, sc, sh, tm=512):
    L, d = x.shape
    row = pl.BlockSpec((1, d), lambda i: (0, 0))
    return pl.pallas_call(
        _norm_kernel,
        grid=(L // tm,),
        in_specs=[pl.BlockSpec((tm, d), lambda i: (i, 0)), row, row, row],
        out_specs=pl.BlockSpec((tm, d), lambda i: (i, 0)),
        out_shape=jax.ShapeDtypeStruct((L, d), BF16),
        compiler_params=_cparams("arbitrary"),
    )(x, gain, sc, sh)


def _rw_pre_kernel(has_vres, *refs):
    if has_vres:
        (h_ref, w_ref, mu_ref, w0_ref, wup_ref, a0_ref, aup_ref, gup_ref, kk_ref, ka_ref, rk_ref, seg_ref,
         vfirst_ref, v0_ref, vup_ref,
         r_out, k_out, v_out, lw_out, a_out, b_out, g_out, bonus_out, ext_scr) = refs
    else:
        (h_ref, w_ref, mu_ref, w0_ref, wup_ref, a0_ref, aup_ref, gup_ref, kk_ref, ka_ref, rk_ref, seg_ref,
         r_out, k_out, v_out, lw_out, a_out, b_out, g_out, bonus_out, ext_scr) = refs
    T = h_ref.shape[0]

    @pl.when(pl.program_id(0) == 0)
    def _():
        ext_scr[0:SUBLANES, :] = jnp.zeros((SUBLANES, ext_scr.shape[1]), F32)

    cur = _dot(h_ref[...], w_ref[...])
    ext_scr[SUBLANES:SUBLANES + T, :] = cur
    prev = ext_scr[SUBLANES - 1:SUBLANES - 1 + T, :]
    ext_scr[0:SUBLANES, :] = cur[T - SUBLANES:T, :]
    s = cur + (prev - cur) * mu_ref[...]

    W = RW_WIDTH
    r = s[:, 0:W]
    k = s[:, W:2 * W]
    v = s[:, 2 * W:3 * W]
    w_lo = s[:, RW_OFF_WLO:RW_OFF_WLO + LANES]
    a_lo = s[:, RW_OFF_ALO:RW_OFF_ALO + LANES]
    g_lo = s[:, RW_OFF_GLO:RW_OFF_GLO + 2 * LANES]
    seg = seg_ref[...]

    wlog = -_softplus(-(w0_ref[...] + _dot_bf16(jnp.tanh(w_lo), wup_ref[...]))) - 0.5
    a = _sigmoid(a0_ref[...] + _dot_bf16(a_lo, aup_ref[...]))
    g = _dot_bf16(_sigmoid(g_lo), gup_ref[...])
    if has_vres:
        hv = s[:, RW_OFF_HV:RW_OFF_HV + LANES]
        v = v + (vfirst_ref[...].astype(F32) - v) * _sigmoid(v0_ref[...] + _dot_bf16(hv, vup_ref[...]))
    kk = k * kk_ref[...]
    kk = kk / jnp.maximum(jnp.sqrt(_split_dot(kk * kk, seg)), 1e-12)
    k = k * (1.0 + (a - 1.0) * ka_ref[...])
    r_out[...] = r.astype(r_out.dtype)
    k_out[...] = k.astype(k_out.dtype)
    v_out[...] = v.astype(v_out.dtype)
    lw_out[...] = -jnp.exp(wlog)
    a_out[...] = (-kk).astype(a_out.dtype)
    b_out[...] = (kk * a).astype(b_out.dtype)
    g_out[...] = g.astype(g_out.dtype)
    bonus_out[...] = (_split_dot(r * k * rk_ref[...], seg) * v).astype(bonus_out.dtype)


def rw_pre(h, w_rw, p, v_first, tm=512):
    L, D = h.shape
    has_vres = v_first is not None
    full = lambda a: pl.BlockSpec(a.shape, lambda i: (0,) * a.ndim)
    rows = lambda n: pl.BlockSpec((tm, n), lambda i: (i, 0))
    args = [h, w_rw, p["mu"], p["w0"], p["w_up"], p["a0"], p["a_up"], p["g_up"], p["k_k"], p["k_a"], p["r_k"],
            p["seg"]]
    specs = [rows(D)] + [full(a) for a in args[1:]]
    if has_vres:
        args += [v_first, p["v0"], p["v_up"]]
        specs += [rows(RW_WIDTH), full(p["v0"]), full(p["v_up"])]
    out_dtypes = [BF16, BF16, BF16, F32, BF16, BF16, BF16, BF16]
    return pl.pallas_call(
        functools.partial(_rw_pre_kernel, has_vres),
        grid=(L // tm,),
        in_specs=specs,
        out_specs=[rows(RW_WIDTH)] * 8,
        out_shape=[jax.ShapeDtypeStruct((L, RW_WIDTH), dt) for dt in out_dtypes],
        scratch_shapes=[pltpu.VMEM((tm + SUBLANES, RW_PAD_COLS), F32)],
        compiler_params=_cparams("arbitrary"),
    )(*args)


def _wkv_kernel(r_ref, lw_ref, k_ref, v_ref, a_ref, b_ref, y_ref, h_scr):
    C = RW_CHUNK
    C2 = 2 * C
    n_chunks = r_ref.shape[0] // C

    @pl.when(pl.program_id(1) == 0)
    def _():
        h_scr[...] = jnp.zeros_like(h_scr)

    is_a = lax.broadcasted_iota(jnp.int32, (C, LANES), 1) < RW_HEAD_DIM
    ri = lax.broadcasted_iota(jnp.int32, (C2, C2), 0)
    ci = lax.broadcasted_iota(jnp.int32, (C2, C2), 1)
    same = (ri >= C) == (ci >= C)
    strict = jnp.logical_and(same, ri > ci)
    incl = jnp.logical_and(same, ri >= ci)
    eye2 = (ri == ci).astype(F32)
    tri_c = _tril(C).astype(BF16)

    def two(x):
        return jnp.concatenate([jnp.where(is_a, x, 0.0), jnp.where(is_a, 0.0, x)], axis=0)

    chunks = range(n_chunks)
    at2, rt2, bh2, kh2, v2, n_ab, a_ak, m_rb, m_rk, p_end = ([] for _ in range(10))
    for c in chunks:
        sl = pl.ds(c * C, C)
        lw = lw_ref[sl, :]
        lw_hi = lw.astype(BF16)
        lw_lo = (lw - lw_hi.astype(F32)).astype(BF16)
        cs = _dot(tri_c, lw_hi) + _dot(tri_c, lw_lo)
        cs_end = cs[C - 1:C, :]
        e_neg = jnp.exp(-cs)
        e_end = jnp.exp(cs_end - cs)
        a = a_ref[sl, :].astype(F32)
        b = b_ref[sl, :].astype(F32)
        k = k_ref[sl, :].astype(F32)
        at2.append(two(a * jnp.exp(cs - lw)))
        rt2.append(two(r_ref[sl, :].astype(F32) * jnp.exp(cs)))
        bh2.append(two(b * e_end))
        kh2.append(two(k * e_end))
        v2.append(two(v_ref[sl, :].astype(F32)))
        p_end.append(jnp.exp(cs_end))
        quad = _dot_nt(jnp.concatenate([at2[c], rt2[c]], axis=0).astype(BF16),
                       jnp.concatenate([two(b * e_neg), two(k * e_neg)], axis=0).astype(BF16))
        n_ab.append(jnp.where(strict, quad[0:C2, 0:C2], 0.0))
        a_ak.append(jnp.where(strict, quad[0:C2, C2:], 0.0))
        m_rb.append(jnp.where(incl, quad[C2:, 0:C2], 0.0))
        m_rk.append(jnp.where(incl, quad[C2:, C2:], 0.0))
    t_inv = [eye2 + n for n in n_ab]
    n_pow = n_ab
    n_pow = [_dot_bf16(n, n) for n in n_pow]
    for _ in range(int(math.log2(C)) - 2):
        both = [_dot_bf16(jnp.concatenate([n, t], axis=0), n) for n, t in zip(n_pow, t_inv)]
        t_inv = [t + b[C2:, :] for t, b in zip(t_inv, both)]
        n_pow = [b[:C2, :] for b in both]
    t_inv = [t + _dot_bf16(t, n) for t, n in zip(t_inv, n_pow)]
    akv = [_dot_bf16(a_ak[c], v2[c]) for c in chunks]
    au = [_dot_bf16(t_inv[c], jnp.concatenate([at2[c], akv[c]], axis=1)) for c in chunks]
    mau = [_dot_bf16(m_rb[c], au[c]) for c in chunks]
    mkv = [_dot_bf16(m_rk[c], v2[c]) for c in chunks]
    gu = [_dot_tn(bh2[c].astype(BF16), au[c].astype(BF16)) for c in chunks]
    kv = [_dot_tn(kh2[c].astype(BF16), v2[c].astype(BF16)) for c in chunks]
    inv_n = 1.0 / RW_HEAD_DIM
    h = h_scr[...]
    for c in chunks:
        rh2 = rt2[c] + mau[c][:, :LANES]
        g_mat = eye2 * p_end[c] + gu[c][:, :LANES]
        yh = _dot_bf16(jnp.concatenate([rh2, g_mat], axis=0), h)
        h = yh[C2:, :] + gu[c][:, LANES:] + kv[c]
        y2 = yh[:C2, :] + mau[c][:, LANES:] + mkv[c]
        y = y2[:C, :] + y2[C:, :]
        s_all = jnp.sum(y, axis=-1, keepdims=True)
        s_a = jnp.sum(jnp.where(is_a, y, 0.0), axis=-1, keepdims=True)
        yc = y - jnp.where(is_a, s_a, s_all - s_a) * inv_n
        sq = yc * yc
        q_all = jnp.sum(sq, axis=-1, keepdims=True)
        q_a = jnp.sum(jnp.where(is_a, sq, 0.0), axis=-1, keepdims=True)
        yn = yc * lax.rsqrt(jnp.where(is_a, q_a, q_all - q_a) * inv_n + GN_EPS)
        y_ref[pl.ds(c * C, C), :] = yn.astype(y_ref.dtype)
    h_scr[...] = h


def wkv7(r, lw, k, v, a, b, rows=2048):
    L, W = r.shape
    spec = pl.BlockSpec((rows, LANES), lambda p, i: (i, p))
    return pl.pallas_call(
        _wkv_kernel,
        grid=(W // LANES, L // rows),
        in_specs=[spec] * 6,
        out_specs=spec,
        out_shape=jax.ShapeDtypeStruct((L, W), BF16),
        scratch_shapes=[pltpu.VMEM((LANES, LANES), F32)],
        compiler_params=_cparams("arbitrary", "arbitrary"),
    )(r, lw, k, v, a, b)


def _ssd_kernel(h_ref, w_ref, cw_ref, cb_ref, bias_ref, A_ref, exp_ref, dskip_ref, nw_ref, o_ref,
                ext_scr, xbc_scr, s_scr):
    Q = SSD_CHUNK
    GW = SSD_GROUP_WIDTH
    NS = SSD_STATE
    R = h_ref.shape[0]

    @pl.when(pl.program_id(0) == 0)
    def _():
        s_scr[...] = jnp.zeros_like(s_scr)
        ext_scr[0:SUBLANES, :] = jnp.zeros((SUBLANES, ext_scr.shape[1]), F32)

    cols = _dot(h_ref[...], w_ref[...])
    ext_scr[SUBLANES:SUBLANES + R, :] = cols[:, 0:SSD_XBC]
    acc = cols[:, 0:SSD_XBC] * cw_ref[SSD_CONV - 1:SSD_CONV, :] + cb_ref[...]
    for j in range(1, SSD_CONV):
        acc = acc + ext_scr[SUBLANES - j:SUBLANES - j + R, :] * cw_ref[SSD_CONV - 1 - j:SSD_CONV - j, :]
    ext_scr[0:SUBLANES, :] = ext_scr[R:R + SUBLANES, :]
    xbc_scr[...] = acc * _sigmoid(acc)

    incl = _tril(Q)
    tri = incl.astype(BF16)
    triu = jnp.logical_not(_tril(Q, strict=True)).astype(BF16)
    expand = exp_ref[...]
    first_half = lax.broadcasted_iota(jnp.int32, (Q, LANES), 1) < SSD_HEAD_DIM
    heads_per_group = SSD_HEADS // SSD_GROUPS
    for c in range(R // Q):
        rows = slice(c * Q, (c + 1) * Q)
        xs = xbc_scr[rows, 0:SSD_WIDTH]
        z = cols[rows, SSD_XBC:SSD_XBC + SSD_WIDTH]
        dt = _softplus(cols[rows, SSD_XBC + SSD_WIDTH:] + bias_ref[...])
        a3 = _split3(dt * A_ref[...])
        a_cum = sum(_dot(tri, t) for t in a3)
        a_cumT = sum(_dot_tn(t, triu) for t in a3)
        a_end = a_cum[Q - 1:Q, :]
        dt_x = _dot(dt.astype(BF16), expand)
        eac_x = _dot(jnp.exp(a_cum).astype(BF16), expand)
        dte_x = _dot(jnp.exp(a_end - a_cum).astype(BF16), expand)
        cd_x = sum(_dot(t, expand) for t in _split3(jnp.broadcast_to(jnp.exp(a_end), (SUBLANES, LANES))))[0:1, :]
        xdt = xs * dt_x
        for g in range(SSD_GROUPS):
            Bg = xbc_scr[rows, SSD_WIDTH + g * NS:SSD_WIDTH + (g + 1) * NS].astype(BF16)
            Cg = xbc_scr[rows, SSD_WIDTH + (SSD_GROUPS + g) * NS:SSD_WIDTH + (SSD_GROUPS + g + 1) * NS].astype(BF16)
            cb = _dot_nt(Cg, Bg)
            gs = slice(g * GW, (g + 1) * GW)
            s_prev = s_scr[:, gs]
            y_g = _dot_bf16(Cg, s_prev) * eac_x[:, gs]
            pieces = []
            for pr in range(heads_per_group // 2):
                ps = slice(g * GW + pr * LANES, g * GW + (pr + 1) * LANES)
                xdt_p = xdt[:, ps]
                acc_p = None
                for e in range(2):
                    hd = g * heads_per_group + pr * 2 + e
                    seg = a_cum[:, hd:hd + 1] - a_cumT[hd:hd + 1, :]
                    m = cb * jnp.exp(jnp.where(incl, seg, -jnp.inf))
                    xm = jnp.where(first_half if e == 0 else jnp.logical_not(first_half), xdt_p, 0.0)
                    t = _dot_bf16(m, xm)
                    acc_p = t if acc_p is None else acc_p + t
                pieces.append(acc_p)
            y_g = y_g + jnp.concatenate(pieces, axis=1)
            s_scr[:, gs] = s_prev * cd_x[:, gs] +_dot_tn(Bg, (dte_x[:, gs] * xdt[:, gs]).astype(BF16))
            y_g = y_g + dskip_ref[:, gs] * xs[:, gs]
            zg = z[:, gs]
            y_g = y_g * (zg * _sigmoid(zg))
            y_g = y_g * lax.rsqrt(jnp.mean(y_g * y_g, axis=-1, keepdims=True) + EPS)
            o_ref[rows, gs] = (y_g * nw_ref[:, gs]).astype(o_ref.dtype)


def ssd_mixer(h, w_ssd, p, rows=512):
    L, D = h.shape
    N = w_ssd.shape[1]
    full = lambda a: pl.BlockSpec(a.shape, lambda i: (0,) * a.ndim)
    small = [p["conv_w"], p["conv_b"], p["dt_bias"], p["A"], p["expand"], p["d_skip"], p["norm_w"]]
    return pl.pallas_call(
        _ssd_kernel,
        grid=(L // rows,),
        in_specs=[pl.BlockSpec((rows, D), lambda i: (i, 0)), full(w_ssd)] + [full(a) for a in small],
        out_specs=pl.BlockSpec((rows, SSD_WIDTH), lambda i: (i, 0)),
        out_shape=jax.ShapeDtypeStruct((L, SSD_WIDTH), BF16),
        scratch_shapes=[pltpu.VMEM((rows + SUBLANES, SSD_XBC), F32), pltpu.VMEM((rows, SSD_XBC), F32),
                        pltpu.VMEM((SSD_STATE, SSD_WIDTH), F32)],
        compiler_params=_cparams("arbitrary"),
    )(h, w_ssd, *small)


def _fox_pre_kernel(h_ref, w_ref, fb_ref, qg_ref, kg_ref, seg_ref, q_out, k_out, v_out, f_out, carry_scr):
    T = h_ref.shape[0]
    W = FOX_WIDTH

    @pl.when(pl.program_id(0) == 0)
    def _():
        carry_scr[...] = jnp.zeros_like(carry_scr)

    cols = _dot(h_ref[...], w_ref[...])
    seg = seg_ref[...]
    q = cols[:, 0:W]
    k = cols[:, W:2 * W]
    inv_d = 1.0 / FOX_HEAD_DIM
    qn = q * lax.rsqrt(_split_dot(q * q, seg) * inv_d + EPS) * qg_ref[...]
    kn = k * lax.rsqrt(_split_dot(k * k, seg) * inv_d + EPS) * kg_ref[...]
    q_out[...] = qn.astype(q_out.dtype)
    k_out[...] = kn.astype(k_out.dtype)
    v_out[...] = cols[:, 2 * W:3 * W].astype(v_out.dtype)
    f = cols[:, 3 * W:3 * W + LANES] + fb_ref[...]
    ls = -_softplus(-f)
    cum = _dot_hi(_tril(T).astype(F32), ls) + carry_scr[0:1, :]
    f_out[...] = cum * LOG2E
    carry_scr[...] = jnp.broadcast_to(cum[T - 1:T, :], carry_scr.shape)


def fox_pre(h, w_fox, p, tm=512):
    L, D = h.shape
    full = lambda a: pl.BlockSpec(a.shape, lambda i: (0,) * a.ndim)
    rows = lambda n: pl.BlockSpec((tm, n), lambda i: (i, 0))
    small = [w_fox, p["f_bias"], p["q_gain"], p["k_gain"], p["seg"]]
    return pl.pallas_call(
        _fox_pre_kernel,
        grid=(L // tm,),
        in_specs=[rows(D)] + [full(a) for a in small],
        out_specs=[rows(FOX_WIDTH)] * 3 + [rows(LANES)],
        out_shape=[jax.ShapeDtypeStruct((L, FOX_WIDTH), BF16)] * 3 + [jax.ShapeDtypeStruct((L, LANES), F32)],
        scratch_shapes=[pltpu.VMEM((SUBLANES, LANES), F32)],
        compiler_params=_cparams("arbitrary"),
    )(h, *small)


def _fox_kernel(lo_ref, q_ref, k_ref, v_ref, fk_ref, o_ref, m_scr, acc_scr, s0_scr, s1_scr):
    tq = q_ref.shape[0]
    tk = FOX_TK
    pair = pl.program_id(0)
    qi = pl.program_id(1)
    is_a = lax.broadcasted_iota(jnp.int32, (tq, LANES), 1) < FOX_HEAD_DIM
    q = q_ref[...]
    zero = jnp.zeros_like(q)
    qs = (jnp.where(is_a, q, zero), jnp.where(is_a, zero, q))
    m_scr[...] = jnp.full(m_scr.shape, -jnp.inf, F32)
    key_is_a = lax.broadcasted_iota(jnp.int32, (tk, LANES), 1) < FOX_HEAD_DIM
    acc_scr[...] = jnp.zeros_like(acc_scr)
    rel = lax.broadcasted_iota(jnp.int32, (tq, tk), 0) - lax.broadcasted_iota(jnp.int32, (tq, tk), 1)
    last = (qi * tq) // tk

    def fill(s_ref, kb, diagonal=False):
        ks = pl.ds(pl.multiple_of(kb * tk, tk), tk)
        k_blk = k_ref[ks, :]
        for e in range(2):
            s = _dot_nt(qs[e], k_blk) - fk_ref[pl.ds(2 * pair + e, 1), ks]
            if diagonal:
                s = jnp.where(rel >= kb * tk - qi * tq, s, -jnp.inf)
            s_ref[e] = s

    def consume(s_ref, kb):
        v_blk = v_ref[pl.ds(pl.multiple_of(kb * tk, tk), tk), :]
        one = jnp.ones_like(v_blk)
        v_one = (jnp.where(key_is_a, v_blk, one), jnp.where(key_is_a, one, v_blk))
        for e in range(2):
            s = s_ref[e]
            m_old = m_scr[e]
            m_new = jnp.maximum(m_old, jnp.max(s, axis=-1, keepdims=True))
            pexp = jnp.exp2(s - m_new)
            acc_scr[e] = jnp.exp2(m_old - m_new) * acc_scr[e] + _dot(pexp.astype(BF16), v_one[e])
            m_scr[e] = m_new

    n = last - jnp.minimum(lo_ref[2 * pair, qi], lo_ref[2 * pair + 1, qi])
    fill(s0_scr, last, diagonal=True)

    def two_steps(u, carry):
        kb = last - 2 * u
        fill(s1_scr, kb - 1)
        consume(s0_scr, kb)
        fill(s0_scr, kb - 2)
        consume(s1_scr, kb - 1)
        return carry

    lax.fori_loop(0, n // 2, two_steps, 0)

    @pl.when(n % 2 == 1)
    def _():
        fill(s1_scr, last - n)
        consume(s0_scr, last - n + 1)
        consume(s1_scr, last - n)

    @pl.when(n % 2 == 0)
    def _():
        consume(s0_scr, last - n)

    acc_a = acc_scr[0]
    acc_b = acc_scr[1]
    half = FOX_HEAD_DIM
    o_ref[...] = jnp.where(is_a, acc_a / pltpu.roll(acc_a, half, 1),
                           acc_b / pltpu.roll(acc_b, half, 1)).astype(o_ref.dtype)


def fox_attention(q, k, v, FT, lo):
    L, W = q.shape
    H = FT.shape[0]
    tq = FOX_TQ
    grid_spec = pltpu.PrefetchScalarGridSpec(
        num_scalar_prefetch=1,
        grid=(W // LANES, L // tq),
        in_specs=[pl.BlockSpec((tq, LANES), lambda p, i, lo_r: (i, p)),
                  pl.BlockSpec((L, LANES), lambda p, i, lo_r: (0, p)),
                  pl.BlockSpec((L, LANES), lambda p, i, lo_r: (0, p)),
                  pl.BlockSpec((H, L), lambda p, i, lo_r: (0, 0))],
        out_specs=pl.BlockSpec((tq, LANES), lambda p, i, lo_r: (i, p)),
        scratch_shapes=[pltpu.VMEM((2, tq, 1), F32), pltpu.VMEM((2, tq, LANES), F32),
                        pltpu.VMEM((2, tq, FOX_TK), F32), pltpu.VMEM((2, tq, FOX_TK), F32)],
    )
    return pl.pallas_call(
        _fox_kernel,
        grid_spec=grid_spec,
        out_shape=jax.ShapeDtypeStruct((L, W), BF16),
        compiler_params=_cparams("arbitrary", "arbitrary"),
    )(lo, q, k, v, FT)


def fox_first_block(FT, bound2):
    f_first = FT[:, ::FOX_TQ]
    f_last = FT[:, FOX_TK - 1::FOX_TK]
    gap = f_first[:, :, None] - f_last[:, None, :] + 2.0 * bound2
    last = (jnp.arange(f_first.shape[1]) * FOX_TQ) // FOX_TK
    needed = jnp.logical_or(gap >= -152.0, jnp.arange(f_last.shape[1])[None, None, :] >= last[None, :, None])
    return jnp.argmax(needed, axis=-1).astype(jnp.int32)


def _top2_combine(h, wr):
    T = h.shape[0]
    lane = lax.broadcasted_iota(jnp.int32, (T, LANES), 1)
    h_hi = h.astype(BF16)
    h_lo = (h - h_hi.astype(F32)).astype(BF16)
    w_hi = wr.astype(BF16)
    w_lo = (wr - w_hi.astype(F32)).astype(BF16)
    logits = _dot(h_hi, w_hi) + (_dot(h_hi, w_lo) + _dot(h_lo, w_hi))
    logits = jnp.where(lane < N_EXPERTS, logits, -jnp.inf)
    m1 = jnp.max(logits, axis=-1, keepdims=True)
    i1 = jnp.min(jnp.where(logits == m1, lane, LANES), axis=-1, keepdims=True)
    rest = jnp.where(lane == i1, -jnp.inf, logits)
    m2 = jnp.max(rest, axis=-1, keepdims=True)
    i2 = jnp.min(jnp.where(rest == m2, lane, LANES), axis=-1, keepdims=True)
    e2 = jnp.exp(m2 - m1)
    return jnp.where(lane == i1, 1.0 / (1.0 + e2), 0.0) + jnp.where(lane == i2, e2 / (1.0 + e2), 0.0)


def _merge_kernel(with_router, *refs):
    (x_ref, hm_ref, yn_ref, bonus_ref, g_ref, yssd_ref, yfox_ref, wgate_ref, gb_ref, lnw_ref, lnb_ref,
     prw_ref, pssd_ref, pfox_ref, wout_ref, gm_ref, gain_ref, sc_ref, sh_ref) = refs[:19]
    D = D_MODEL
    hm = hm_ref[...]
    y_rw = (yn_ref[...] * lnw_ref[...] + lnb_ref[...] + bonus_ref[...]) * g_ref[...]
    merged = None
    for b, (y_b, proj_ref) in enumerate(((y_rw, prw_ref), (yssd_ref[...], pssd_ref), (yfox_ref[...], pfox_ref))):
        cols = slice(b * D, (b + 1) * D)
        gate = _sigmoid(_dot(hm, wgate_ref[:, cols]) + gb_ref[:, cols])
        term = gate * _dot_bf16(y_b, proj_ref[...])
        merged = term if merged is None else merged + term
    x_new = x_ref[...] + gm_ref[...] * _dot_bf16(merged, wout_ref[...])
    hf = _norm_mod(x_new, gain_ref[...], sc_ref[...], sh_ref[...])
    if with_router:
        wr_ref, o_ref, h_ref, comb_ref = refs[19:]
        comb_ref[...] = _top2_combine(hf, wr_ref[...])
    else:
        o_ref, h_ref = refs[19:]
    o_ref[...] = x_new
    h_ref[...] = hf.astype(h_ref.dtype)


def merge(x, hm, yn, bonus, g, y_ssd, y_fox, p, router=None, tm=256):
    L, D = x.shape
    full = lambda a: pl.BlockSpec(a.shape, lambda i: (0,) * a.ndim)
    rows = lambda n: pl.BlockSpec((tm, n), lambda i: (i, 0))
    small = [p["w_gate"], p["gate_b"], p["lnx_w"], p["lnx_b"], p["proj_rw"], p["proj_ssd"], p["proj_fox"],
             p["w_out"], p["g_m"], p["norm_gain"], p["sc_f"], p["sh_f"]]
    out_specs = [rows(D), rows(D)]
    out_shape = [jax.ShapeDtypeStruct((L, D), F32), jax.ShapeDtypeStruct((L, D), BF16)]
    if router is not None:
        small.append(router)
        out_specs.append(rows(LANES))
        out_shape.append(jax.ShapeDtypeStruct((L, LANES), F32))
    return pl.pallas_call(
        functools.partial(_merge_kernel, router is not None),
        grid=(L // tm,),
        in_specs=[rows(D), rows(D), rows(RW_WIDTH), rows(RW_WIDTH), rows(RW_WIDTH), rows(SSD_WIDTH),
                  rows(FOX_WIDTH)] + [full(a) for a in small],
        out_specs=out_specs,
        out_shape=out_shape,
        compiler_params=_cparams("arbitrary"),
    )(x, hm, yn, bonus, g, y_ssd, y_fox, *small)


def _ffn_kernel(with_next, *refs):
    h_ref, x_ref, wg_ref, wu_ref, wd_ref, gf_ref = refs[:6]
    j = pl.program_id(1)
    acc_scr = refs[-1]

    @pl.when(j == 0)
    def _():
        acc_scr[...] = jnp.zeros_like(acc_scr)

    h = h_ref[...]
    gte = _dot(h, wg_ref[...])
    up = _dot(h, wu_ref[...])
    act = gte * _sigmoid(gte) * up
    acc_scr[...] += _dot(act.astype(BF16), wd_ref[...])

    @pl.when(j == pl.num_programs(1) - 1)
    def _():
        x_new = x_ref[...] + gf_ref[...] * acc_scr[...]
        if with_next:
            gain_ref, sc_ref, sh_ref, o_ref, hn_ref = refs[6:11]
            hn_ref[...] = _norm_mod(x_new, gain_ref[...], sc_ref[...], sh_ref[...]).astype(hn_ref.dtype)
        else:
            o_ref = refs[6]
        o_ref[...] = x_new


def ffn_dense(h, x, w_gu, w_down, g_f, next_norm=None, tm=512, tf=1408):
    L, D = x.shape
    Fh = w_down.shape[0]
    nf = Fh // tf
    row = pl.BlockSpec((1, D), lambda i, j: (0, 0))
    tile = pl.BlockSpec((tm, D), lambda i, j: (i, 0))
    args = [h, x, w_gu, w_gu, w_down, g_f]
    in_specs = [tile, tile,
                pl.BlockSpec((D, tf), lambda i, j: (0, j)),
                pl.BlockSpec((D, tf), lambda i, j: (0, j + nf)),
                pl.BlockSpec((tf, D), lambda i, j: (j, 0)), row]
    out_specs = [tile]
    out_shape = [jax.ShapeDtypeStruct((L, D), F32)]
    if next_norm is not None:
        args += list(next_norm)
        in_specs += [row, row, row]
        out_specs.append(tile)
        out_shape.append(jax.ShapeDtypeStruct((L, D), BF16))
    out = pl.pallas_call(
        functools.partial(_ffn_kernel, next_norm is not None),
        grid=(L // tm, nf),
        in_specs=in_specs,
        out_specs=out_specs,
        out_shape=out_shape,
        scratch_shapes=[pltpu.VMEM((tm, D), F32)],
        compiler_params=_cparams("arbitrary", "arbitrary"),
    )(*args)
    return out if next_norm is not None else (out[0], None)


def _moe_slots(slot_scr, comb_ref):
    slotval = slot_scr[...]
    comb = comb_ref[...]
    s1 = jnp.max(slotval, axis=-1, keepdims=True)
    first = slotval == s1
    c1 = jnp.sum(jnp.where(first, comb, 0.0), axis=-1, keepdims=True)
    rest = jnp.where(first, -1.0, slotval)
    s2 = jnp.max(rest, axis=-1, keepdims=True)
    c2 = jnp.sum(jnp.where(jnp.logical_and(rest == s2, rest >= 0.0), comb, 0.0), axis=-1, keepdims=True)
    return s1, s2, c1, c2


def _moe_kernel(nsb_ref, off_ref, h_ref, x_ref, comb_ref, wg_ref, wu_ref, wd_ref, gf_ref, o_ref,
                slot_scr, cslot_scr, xy_scr, acc_scr):
    i = pl.program_id(0)
    e = pl.program_id(1)
    j = pl.program_id(2)
    last_j = pl.num_programs(2) - 1
    T = h_ref.shape[0]
    S = xy_scr.shape[0]
    SB = MOE_SB

    @pl.when(jnp.logical_and(e == 0, j == 0))
    def _():
        lane = lax.broadcasted_iota(jnp.int32, (LANES, LANES), 1)
        off = jnp.zeros((LANES, LANES), F32)
        for ex in range(N_EXPERTS):
            off = jnp.where(lane == ex, off_ref[i, ex].astype(F32), off)
        before = _tril(LANES, strict=True).astype(BF16)
        ones = jnp.ones((LANES, LANES), BF16)
        for rb in range(T // LANES):
            rows = slice(rb * LANES, (rb + 1) * LANES)
            sel = comb_ref[rows, :] > 0.0
            sel_b = jnp.where(sel, 1.0, 0.0).astype(BF16)
            slot_scr[rows, :] = jnp.where(sel, _dot(before, sel_b) + off, -1.0)
            off = off + _dot(ones, sel_b)
        s1, s2, c1, c2 = _moe_slots(slot_scr, comb_ref)
        used = off_ref[i, N_EXPERTS - 1] + nsb_ref[i, N_EXPERTS - 1] * SB
        for cb in range(S // MOE_CB):
            chunk = slice(cb * MOE_CB, (cb + 1) * MOE_CB)

            @pl.when(used > cb * MOE_CB)
            def _():
                slot = (lax.broadcasted_iota(jnp.int32, (T, MOE_CB), 1) + cb * MOE_CB).astype(F32)
                w = jnp.where(s1 == slot, c1, 0.0) + jnp.where(s2 == slot, c2, 0.0)
                pt = jnp.where(w > 0.0, 1.0, 0.0).astype(BF16)
                xy_scr[chunk, :] = _dot_tn(pt, h_ref[...]).astype(BF16)
                w_hi = w.astype(BF16)
                w_lo = (w - w_hi.astype(F32)).astype(BF16)
                ones_t = jnp.ones((T, LANES), BF16)
                cslot_scr[chunk, :] = _dot_tn(w_hi, ones_t) + _dot_tn(w_lo, ones_t)

            @pl.when(used <= cb * MOE_CB)
            def _():
                xy_scr[chunk, :] = jnp.zeros((MOE_CB, xy_scr.shape[1]), BF16)
                cslot_scr[chunk, :] = jnp.zeros((MOE_CB, LANES), F32)

    base = off_ref[i, e]
    nb = nsb_ref[i, e]

    def ffn_block(r0, n_rows):
        rows = pl.ds(pl.multiple_of(base + r0, SB), n_rows)
        arows = pl.ds(pl.multiple_of(r0, SB), n_rows)
        xb = xy_scr[rows, :]
        gte = _dot(xb, wg_ref[0])
        up = _dot(xb, wu_ref[0])
        part = _dot((gte * _sigmoid(gte) * up).astype(BF16), wd_ref[0])

        @pl.when(j == 0)
        def _():
            acc_scr[arows, :] = part

        @pl.when(jnp.logical_and(j > 0, j < last_j))
        def _():
            acc_scr[arows, :] += part

        @pl.when(j == last_j)
        def _():
            xy_scr[rows, :] = ((acc_scr[arows, :] + part) * cslot_scr[rows, 0:1]).astype(BF16)

    def quad_body(p, carry):
        ffn_block(p * (4 * SB), 4 * SB)
        return carry

    lax.fori_loop(0, nb // 4, quad_body, 0)
    for rest in range(1, 4):
        @pl.when(nb % 4 == rest)
        def _():
            ffn_block((nb - rest) * SB, rest * SB)

    @pl.when(jnp.logical_and(e == pl.num_programs(1) - 1, j == last_j))
    def _():
        s1, s2, _, _ = _moe_slots(slot_scr, comb_ref)
        total = jnp.zeros((T, xy_scr.shape[1]), F32)
        for cb in range(S // MOE_CB):
            slot = (lax.broadcasted_iota(jnp.int32, (T, MOE_CB), 1) + cb * MOE_CB).astype(F32)
            pt = jnp.where(jnp.logical_or(s1 == slot, s2 == slot), 1.0, 0.0).astype(BF16)
            total = total + _dot(pt, xy_scr[cb * MOE_CB:(cb + 1) * MOE_CB, :])
        o_ref[...] = x_ref[...] + gf_ref[...] * total


def ffn_moe(h, x, comb, w_gu, w_down, g_f, tm=1024, tf=896):
    L, D = x.shape
    E, Fh, _ = w_down.shape
    nf = Fh // tf
    assert nf >= 2 and E == N_EXPERTS
    nt = L // tm
    counts = jnp.sum((comb[:, :E] > 0.0).reshape(nt, tm, E), axis=1, dtype=jnp.int32)
    nsb = (counts + (MOE_SB - 1)) // MOE_SB
    off = (jnp.cumsum(nsb, axis=1) - nsb) * MOE_SB
    idx = lambda f: (lambda i, e, j, n, o: f(i, e, j))
    grid_spec = pltpu.PrefetchScalarGridSpec(
        num_scalar_prefetch=2,
        grid=(nt, E, nf),
        in_specs=[pl.BlockSpec((tm, D), idx(lambda i, e, j: (i, 0))),
                  pl.BlockSpec((tm, D), idx(lambda i, e, j: (i, 0))),
                  pl.BlockSpec((tm, LANES), idx(lambda i, e, j: (i, 0))),
                  pl.BlockSpec((1, D, tf), idx(lambda i, e, j: (e, 0, j))),
                  pl.BlockSpec((1, D, tf), idx(lambda i, e, j: (e, 0, j + nf))),
                  pl.BlockSpec((1, tf, D), idx(lambda i, e, j: (e, j, 0))),
                  pl.BlockSpec((1, D), idx(lambda i, e, j: (0, 0)))],
        out_specs=pl.BlockSpec((tm, D), idx(lambda i, e, j: (i, 0))),
        scratch_shapes=[pltpu.VMEM((tm, LANES), F32),
                        pltpu.VMEM((TOP_K * tm + E * MOE_SB, LANES), F32),
                        pltpu.VMEM((TOP_K * tm + E * MOE_SB, D), BF16),
                        pltpu.VMEM((tm, D), F32)],
    )
    return pl.pallas_call(
        _moe_kernel,
        grid_spec=grid_spec,
        out_shape=jax.ShapeDtypeStruct((L, D), F32),
        compiler_params=_cparams("arbitrary", "arbitrary", "arbitrary"),
    )(nsb, off, h, x, comb, w_gu, w_gu, w_down, g_f)


def _seg_matrix(width, head_dim):
    idx = jnp.arange(width) // head_dim
    return (idx[:, None] == idx[None, :]).astype(BF16)


def _pad_cols(w, n):
    return jnp.pad(w, ((0, 0), (0, n - w.shape[1])))


def _pad_rows(w, n):
    return jnp.pad(w, ((0, n - w.shape[0]), (0, 0)))


def rwkv_branch(h, w_rw, p, v_first):
    r, k, v, lw, a, b, g, bonus = rw_pre(h, w_rw, p, v_first)
    return wkv7(r, lw, k, v, a, b), bonus, g, v


def fox_branch(h, w_fox, p):
    q, k, v, F = fox_pre(h, w_fox, p)
    FT = jnp.transpose(F[:, :FOX_HEADS])
    lo = fox_first_block(FT, p["bound2"])
    return fox_attention(q, k, v, FT, lo)


def kernel(x, c, ada_w, ada_b, norm_mix, norm_ffn, w_in, rw_mu, rw_w0, rw_w_up, rw_a0, rw_a_up, rw_g_up, rw_k_k, rw_k_a, rw_r_k, rw_lnx_w, rw_lnx_b, rw_v0, rw_v_down, rw_v_up, ssd_conv_w, ssd_conv_b, ssd_dt_bias, ssd_a_log, ssd_d, ssd_norm, fox_f_bias, fox_q_gain, fox_k_gain, gate_b, proj_rw, proj_ssd, proj_fox, w_out, ffn_w_gu, ffn_w_down, moe_router, moe_w_gu, moe_w_down):
    depth = w_in.shape[0]
    D = D_MODEL
    xs = x[0]
    row = lambda t: t.reshape(1, -1).astype(F32)
    seg64 = _seg_matrix(RW_WIDTH, RW_HEAD_DIM)
    rw_cols = 3 * RW_WIDTH + RW_DECAY_LORA + RW_AAA_LORA + RW_GATE_LORA
    ssd_cols = SSD_WIDTH + SSD_XBC + SSD_HEADS
    fox_cols = 3 * FOX_WIDTH + FOX_HEADS
    expand = (jnp.arange(LANES)[:, None] == (jnp.arange(SSD_WIDTH) // SSD_HEAD_DIM)[None, :]).astype(BF16)
    v_first = None
    mods = []
    for l in range(depth):
        mod = adaln_mod(c, ada_w[l], ada_b[l])
        mods.append([mod[:, i * D:(i + 1) * D] for i in range(6)])
    h = None
    for l in range(depth):
        sh_m, sc_m, g_m, sh_f, sc_f, g_f = mods[l]
        if h is None:
            h = norm_mod(xs, row(norm_mix[l]), sc_m, sh_m)

        wl = w_in[l]
        o = 0
        w_r = wl[:, o:o + rw_cols]; o += rw_cols
        w_s = wl[:, o:o + ssd_cols]; o += ssd_cols
        w_f = wl[:, o:o + fox_cols]; o += fox_cols
        w_g = wl[:, o:]
        W3 = 3 * RW_WIDTH
        o_a = W3 + RW_DECAY_LORA
        o_g = o_a + RW_AAA_LORA

        def rw_layout(t, hv):
            parts = [t[:, :W3], _pad_cols(t[:, W3:o_a], LANES), _pad_cols(t[:, o_a:o_g], LANES),
                     _pad_cols(t[:, o_g:], 2 * LANES), _pad_cols(hv, LANES)]
            return jnp.concatenate(parts, axis=1)

        if l == 0:
            hv_w = jnp.zeros((D, RW_VRES_LORA), F32)
        else:
            hv_w = rw_v_down[l - 1]
        w_rw = rw_layout(w_r, hv_w).astype(BF16)
        mu = rw_layout(rw_mu[l].reshape(1, -1), jnp.zeros((1, RW_VRES_LORA), F32))
        rw_p = dict(mu=mu, w0=row(rw_w0[l]), w_up=_pad_rows(rw_w_up[l], LANES).astype(BF16), a0=row(rw_a0[l]),
                    a_up=_pad_rows(rw_a_up[l], LANES).astype(BF16),
                    g_up=_pad_rows(rw_g_up[l], 2 * LANES).astype(BF16),
                    k_k=row(rw_k_k[l]), k_a=row(rw_k_a[l]), r_k=row(rw_r_k[l]), seg=seg64)
        if l > 0:
            rw_p.update(v0=row(rw_v0[l - 1]), v_up=_pad_rows(rw_v_up[l - 1], LANES).astype(BF16))
        yn, bonus, g_rw, v_cur = rwkv_branch(h, w_rw, rw_p, v_first)
        if l == 0:
            v_first = v_cur

        w_ssd = jnp.concatenate([w_s[:, SSD_WIDTH:SSD_WIDTH + SSD_XBC], w_s[:, :SSD_WIDTH],
                                 _pad_cols(w_s[:, SSD_WIDTH + SSD_XBC:], LANES)], axis=1).astype(BF16)
        A = -jnp.exp(ssd_a_log[l].astype(F32))
        ssd_p = dict(conv_w=ssd_conv_w[l], conv_b=row(ssd_conv_b[l]),
                     dt_bias=_pad_cols(row(ssd_dt_bias[l]), LANES),
                     A=_pad_cols(row(A), LANES), expand=expand,
                     d_skip=row(jnp.repeat(ssd_d[l], SSD_HEAD_DIM)), norm_w=row(ssd_norm[l]))
        y_ssd = ssd_mixer(h, w_ssd, ssd_p)

        w_fox = jnp.concatenate([w_f[:, :3 * FOX_WIDTH], _pad_cols(w_f[:, 3 * FOX_WIDTH:], LANES)],
                                axis=1).astype(BF16)
        bound2 = (1.02 * FOX_HEAD_DIM ** 0.5 * LOG2E) * jnp.max(jnp.abs(fox_q_gain[l])) * jnp.max(jnp.abs(fox_k_gain[l]))
        fox_p = dict(f_bias=_pad_cols(row(fox_f_bias[l]), LANES),
                     q_gain=row(jnp.tile(fox_q_gain[l], FOX_HEADS)) * (FOX_HEAD_DIM ** -0.5 * LOG2E),
                     k_gain=row(jnp.tile(fox_k_gain[l], FOX_HEADS)), seg=seg64, bound2=bound2)
        y_fox = fox_branch(h, w_fox, fox_p)

        mp = dict(w_gate=w_g.astype(BF16), gate_b=row(gate_b[l]), lnx_w=row(rw_lnx_w[l]), lnx_b=row(rw_lnx_b[l]),
                  proj_rw=proj_rw[l].astype(BF16), proj_ssd=proj_ssd[l].astype(BF16),
                  proj_fox=proj_fox[l].astype(BF16), w_out=w_out[l].astype(BF16), g_m=g_m,
                  norm_gain=row(norm_ffn[l]), sc_f=sc_f, sh_f=sh_f)

        if l % 2 == 0:
            xs, hf = merge(xs, h, yn, bonus, g_rw, y_ssd, y_fox, mp)
            next_norm = None
            if l + 1 < depth:
                next_norm = (row(norm_mix[l + 1]), mods[l + 1][1], mods[l + 1][0])
            xs, h = ffn_dense(hf, xs, ffn_w_gu[l // 2].astype(BF16), ffn_w_down[l // 2].astype(BF16), g_f,
                              next_norm)
        else:
            xs, hf, comb = merge(xs, h, yn, bonus, g_rw, y_ssd, y_fox, mp,
                                 router=_pad_cols(moe_router[l // 2], LANES))
            xs = ffn_moe(hf, xs, comb, moe_w_gu[l // 2].astype(BF16), moe_w_down[l // 2].astype(BF16), g_f)
            h = None
    return xs[None]
```

```python
import functools
import math

import jax
import jax.numpy as jnp
from jax import lax
from jax.experimental import pallas as pl
from jax.experimental.pallas import tpu as pltpu

F32 = jnp.float32
BF16 = jnp.bfloat16
HIGHEST = lax.Precision.HIGHEST

D_MODEL = 1024
LANES = 128
SUBLANES = 8

RW_HEADS = 8
RW_HEAD_DIM = 64
RW_WIDTH = RW_HEADS * RW_HEAD_DIM
RW_DECAY_LORA = 64
RW_AAA_LORA = 64
RW_VRES_LORA = 32
RW_GATE_LORA = 160
GN_EPS = 64e-5
RW_OFF_WLO = 3 * RW_WIDTH
RW_OFF_ALO = RW_OFF_WLO + LANES
RW_OFF_GLO = RW_OFF_ALO + LANES
RW_OFF_HV = RW_OFF_GLO + 2 * LANES
RW_PAD_COLS = RW_OFF_HV + LANES
RW_CHUNK = 64

SSD_HEADS = 16
SSD_HEAD_DIM = 64
SSD_WIDTH = SSD_HEADS * SSD_HEAD_DIM
SSD_GROUPS = 4
SSD_STATE = 128
SSD_CONV = 4
SSD_CHUNK = 128
SSD_XBC = SSD_WIDTH + 2 * SSD_GROUPS * SSD_STATE
SSD_GROUP_WIDTH = SSD_WIDTH // SSD_GROUPS

FOX_HEADS = 8
FOX_HEAD_DIM = 64
FOX_WIDTH = FOX_HEADS * FOX_HEAD_DIM
FOX_TQ = 512
FOX_TK = 512
LOG2E = 1.4426950408889634

FFN_DENSE = 2816
N_EXPERTS = 8
TOP_K = 2
FFN_EXPERT = 3584
MOE_SB = 128
MOE_CB = 512
EPS = 1e-6

VMEM_LIMIT = 56 * 1024 * 1024


def _cparams(*sem):
    return pltpu.CompilerParams(dimension_semantics=sem, vmem_limit_bytes=VMEM_LIMIT)


def _sigmoid(x):
    return 1.0 / (1.0 + jnp.exp(-x))


def _softplus(x):
    return jnp.maximum(x, 0.0) + jnp.log(1.0 + jnp.exp(-jnp.abs(x)))


def _dot(a, b):
    return jnp.dot(a, b, preferred_element_type=F32)


def _dot_bf16(a, b):
    return jnp.dot(a.astype(BF16), b.astype(BF16), preferred_element_type=F32)


def _dot_hi(a, b):
    return jnp.dot(a, b, precision=HIGHEST, preferred_element_type=F32)


def _dot_nt(a, b, precision=None):
    return lax.dot_general(a, b, (((1,), (1,)), ((), ())), precision=precision, preferred_element_type=F32)


def _dot_tn(a, b, precision=None):
    return lax.dot_general(a, b, (((0,), (0,)), ((), ())), precision=precision, preferred_element_type=F32)


def _split_dot(x, ones_bf16):
    return _dot(x.astype(BF16), ones_bf16)


def _split3(x):
    hi = x.astype(BF16)
    r1 = x - hi.astype(F32)
    mid = r1.astype(BF16)
    return hi, mid, (r1 - mid.astype(F32)).astype(BF16)


def _tril(n, strict=False):
    r = lax.broadcasted_iota(jnp.int32, (n, n), 0)
    c = lax.broadcasted_iota(jnp.int32, (n, n), 1)
    return (r > c) if strict else (r >= c)


def _mod_kernel(c_ref, w_ref, b_ref, o_ref):
    c = c_ref[...]
    o_ref[...] = _dot_hi(c * _sigmoid(c), w_ref[...]) + b_ref[...]


def adaln_mod(c, w, b):
    d, n = w.shape
    tn = 1024
    c8 = jnp.broadcast_to(c, (SUBLANES, d))
    out = pl.pallas_call(
        _mod_kernel,
        grid=(n // tn,),
        in_specs=[pl.BlockSpec((SUBLANES, d), lambda j: (0, 0)),
                  pl.BlockSpec((d, tn), lambda j: (0, j)),
                  pl.BlockSpec((1, tn), lambda j: (0, j))],
        out_specs=pl.BlockSpec((SUBLANES, tn), lambda j: (0, j)),
        out_shape=jax.ShapeDtypeStruct((SUBLANES, n), F32),
        compiler_params=_cparams("arbitrary"),
    )(c8, w, b.reshape(1, n))
    return out[:1]


def _norm_mod(x, gain, sc, sh):
    y = x * lax.rsqrt(jnp.mean(x * x, axis=-1, keepdims=True) + EPS)
    return y * gain * (1.0 + sc) + sh


def _norm_kernel(x_ref, gain_ref, sc_ref, sh_ref, h_ref):
    h_ref[...] = _norm_mod(x_ref[...], gain_ref[...], sc_ref[...], sh_ref[...]).astype(h_ref.dtype)


def norm_mod(x, gain, sc, sh, tm=512):
    L, d = x.shape
    row = pl.BlockSpec((1, d), lambda i: (0, 0))
    return pl.pallas_call(
        _norm_kernel,
        grid=(L // tm,),
        in_specs=[pl.BlockSpec((tm, d), lambda i: (i, 0)), row, row, row],
        out_specs=pl.BlockSpec((tm, d), lambda i: (i, 0)),
        out_shape=jax.ShapeDtypeStruct((L, d), BF16),
        compiler_params=_cparams("arbitrary"),
    )(x, gain, sc, sh)


def _rw_pre_kernel(has_vres, *refs):
    if has_vres:
        (h_ref, w_ref, mu_ref, w0_ref, wup_ref, a0_ref, aup_ref, gup_ref, kk_ref, ka_ref, rk_ref, seg_ref,
         vfirst_ref, v0_ref, vup_ref,
         r_out, k_out, v_out, lw_out, a_out, b_out, g_out, bonus_out, ext_scr) = refs
    else:
        (h_ref, w_ref, mu_ref, w0_ref, wup_ref, a0_ref, aup_ref, gup_ref, kk_ref, ka_ref, rk_ref, seg_ref,
         r_out, k_out, v_out, lw_out, a_out, b_out, g_out, bonus_out, ext_scr) = refs
    T = h_ref.shape[0]

    @pl.when(pl.program_id(0) == 0)
    def _():
        ext_scr[0:SUBLANES, :] = jnp.zeros((SUBLANES, ext_scr.shape[1]), F32)

    cur = _dot(h_ref[...], w_ref[...])
    ext_scr[SUBLANES:SUBLANES + T, :] = cur
    prev = ext_scr[SUBLANES - 1:SUBLANES - 1 + T, :]
    ext_scr[0:SUBLANES, :] = cur[T - SUBLANES:T, :]
    s = cur + (prev - cur) * mu_ref[...]

    W = RW_WIDTH
    r = s[:, 0:W]
    k = s[:, W:2 * W]
    v = s[:, 2 * W:3 * W]
    w_lo = s[:, RW_OFF_WLO:RW_OFF_WLO + LANES]
    a_lo = s[:, RW_OFF_ALO:RW_OFF_ALO + LANES]
    g_lo = s[:, RW_OFF_GLO:RW_OFF_GLO + 2 * LANES]
    seg = seg_ref[...]

    wlog = -_softplus(-(w0_ref[...] + _dot_bf16(jnp.tanh(w_lo), wup_ref[...]))) - 0.5
    a = _sigmoid(a0_ref[...] + _dot_bf16(a_lo, aup_ref[...]))
    g = _dot_bf16(_sigmoid(g_lo), gup_ref[...])
    if has_vres:
        hv = s[:, RW_OFF_HV:RW_OFF_HV + LANES]
        v = v + (vfirst_ref[...].astype(F32) - v) * _sigmoid(v0_ref[...] + _dot_bf16(hv, vup_ref[...]))
    kk = k * kk_ref[...]
    kk = kk / jnp.maximum(jnp.sqrt(_split_dot(kk * kk, seg)), 1e-12)
    k = k * (1.0 + (a - 1.0) * ka_ref[...])
    r_out[...] = r.astype(r_out.dtype)
    k_out[...] = k.astype(k_out.dtype)
    v_out[...] = v.astype(v_out.dtype)
    lw_out[...] = -jnp.exp(wlog)
    a_out[...] = (-kk).astype(a_out.dtype)
    b_out[...] = (kk * a).astype(b_out.dtype)
    g_out[...] = g.astype(g_out.dtype)
    bonus_out[...] = (_split_dot(r * k * rk_ref[...], seg) * v).astype(bonus_out.dtype)


def rw_pre(h, w_rw, p, v_first, tm=512):
    L, D = h.shape
    has_vres = v_first is not None
    full = lambda a: pl.BlockSpec(a.shape, lambda i: (0,) * a.ndim)
    rows = lambda n: pl.BlockSpec((tm, n), lambda i: (i, 0))
    args = [h, w_rw, p["mu"], p["w0"], p["w_up"], p["a0"], p["a_up"], p["g_up"], p["k_k"], p["k_a"], p["r_k"],
            p["seg"]]
    specs = [rows(D)] + [full(a) for a in args[1:]]
    if has_vres:
        args += [v_first, p["v0"], p["v_up"]]
        specs += [rows(RW_WIDTH), full(p["v0"]), full(p["v_up"])]
    out_dtypes = [BF16, BF16, BF16, F32, BF16, BF16, BF16, BF16]
    return pl.pallas_call(
        functools.partial(_rw_pre_kernel, has_vres),
        grid=(L // tm,),
        in_specs=specs,
        out_specs=[rows(RW_WIDTH)] * 8,
        out_shape=[jax.ShapeDtypeStruct((L, RW_WIDTH), dt) for dt in out_dtypes],
        scratch_shapes=[pltpu.VMEM((tm + SUBLANES, RW_PAD_COLS), F32)],
        compiler_params=_cparams("arbitrary"),
    )(*args)


def _wkv_kernel(r_ref, lw_ref, k_ref, v_ref, a_ref, b_ref, y_ref, h_scr):
    C = RW_CHUNK
    C2 = 2 * C
    n_chunks = r_ref.shape[0] // C

    @pl.when(pl.program_id(1) == 0)
    def _():
        h_scr[...] = jnp.zeros_like(h_scr)

    is_a = lax.broadcasted_iota(jnp.int32, (C, LANES), 1) < RW_HEAD_DIM
    ri = lax.broadcasted_iota(jnp.int32, (C2, C2), 0)
    ci = lax.broadcasted_iota(jnp.int32, (C2, C2), 1)
    same = (ri >= C) == (ci >= C)
    strict = jnp.logical_and(same, ri > ci)
    incl = jnp.logical_and(same, ri >= ci)
    eye2 = (ri == ci).astype(F32)
    tri_c = _tril(C).astype(BF16)

    def two(x):
        return jnp.concatenate([jnp.where(is_a, x, 0.0), jnp.where(is_a, 0.0, x)], axis=0)

    chunks = range(n_chunks)
    at2, rt2, bh2, kh2, v2, n_ab, a_ak, m_rb, m_rk, p_end = ([] for _ in range(10))
    for c in chunks:
        sl = pl.ds(c * C, C)
        lw = lw_ref[sl, :]
        lw_hi = lw.astype(BF16)
        lw_lo = (lw - lw_hi.astype(F32)).astype(BF16)
        cs = _dot(tri_c, lw_hi) + _dot(tri_c, lw_lo)
        cs_end = cs[C - 1:C, :]
        e_neg = jnp.exp(-cs)
        e_end = jnp.exp(cs_end - cs)
        a = a_ref[sl, :].astype(F32)
        b = b_ref[sl, :].astype(F32)
        k = k_ref[sl, :].astype(F32)
        at2.append(two(a * jnp.exp(cs - lw)))
        rt2.append(two(r_ref[sl, :].astype(F32) * jnp.exp(cs)))
        bh2.append(two(b * e_end))
        kh2.append(two(k * e_end))
        v2.append(two(v_ref[sl, :].astype(F32)))
        p_end.append(jnp.exp(cs_end))
        quad = _dot_nt(jnp.concatenate([at2[c], rt2[c]], axis=0).astype(BF16),
                       jnp.concatenate([two(b * e_neg), two(k * e_neg)], axis=0).astype(BF16))
        n_ab.append(jnp.where(strict, quad[0:C2, 0:C2], 0.0))
        a_ak.append(jnp.where(strict, quad[0:C2, C2:], 0.0))
        m_rb.append(jnp.where(incl, quad[C2:, 0:C2], 0.0))
        m_rk.append(jnp.where(incl, quad[C2:, C2:], 0.0))
    t_inv = [eye2 + n for n in n_ab]
    n_pow = n_ab
    n_pow = [_dot_bf16(n, n) for n in n_pow]
    for _ in range(int(math.log2(C)) - 2):
        both = [_dot_bf16(jnp.concatenate([n, t], axis=0), n) for n, t in zip(n_pow, t_inv)]
        t_inv = [t + b[C2:, :] for t, b in zip(t_inv, both)]
        n_pow = [b[:C2, :] for b in both]
    t_inv = [t + _dot_bf16(t, n) for t, n in zip(t_inv, n_pow)]
    akv = [_dot_bf16(a_ak[c], v2[c]) for c in chunks]
    au = [_dot_bf16(t_inv[c], jnp.concatenate([at2[c], akv[c]], axis=1)) for c in chunks]
    mau = [_dot_bf16(m_rb[c], au[c]) for c in chunks]
    mkv = [_dot_bf16(m_rk[c], v2[c]) for c in chunks]
    gu = [_dot_tn(bh2[c].astype(BF16), au[c].astype(BF16)) for c in chunks]
    kv = [_dot_tn(kh2[c].astype(BF16), v2[c].astype(BF16)) for c in chunks]
    inv_n = 1.0 / RW_HEAD_DIM
    h = h_scr[...]
    for c in chunks:
        rh2 = rt2[c] + mau[c][:, :LANES]
        g_mat = eye2 * p_end[c] + gu[c][:, :LANES]
        yh = _dot_bf16(jnp.concatenate([rh2, g_mat], axis=0), h)
        h = yh[C2:, :] + gu[c][:, LANES:] + kv[c]
        y2 = yh[:C2, :] + mau[c][:, LANES:] + mkv[c]
        y = y2[:C, :] + y2[C:, :]
        s_all = jnp.sum(y, axis=-1, keepdims=True)
        s_a = jnp.sum(jnp.where(is_a, y, 0.0), axis=-1, keepdims=True)
        yc = y - jnp.where(is_a, s_a, s_all - s_a) * inv_n
        sq = yc * yc
        q_all = jnp.sum(sq, axis=-1, keepdims=True)
        q_a = jnp.sum(jnp.where(is_a, sq, 0.0), axis=-1, keepdims=True)
        yn = yc * lax.rsqrt(jnp.where(is_a, q_a, q_all - q_a) * inv_n + GN_EPS)
        y_ref[pl.ds(c * C, C), :] = yn.astype(y_ref.dtype)
    h_scr[...] = h


def wkv7(r, lw, k, v, a, b, rows=2048):
    L, W = r.shape
    spec = pl.BlockSpec((rows, LANES), lambda p, i: (i, p))
    return pl.pallas_call(
        _wkv_kernel,
        grid=(W // LANES, L // rows),
        in_specs=[spec] * 6,
        out_specs=spec,
        out_shape=jax.ShapeDtypeStruct((L, W), BF16),
        scratch_shapes=[pltpu.VMEM((LANES, LANES), F32)],
        compiler_params=_cparams("arbitrary", "arbitrary"),
    )(r, lw, k, v, a, b)


def _ssd_kernel(h_ref, w_ref, cw_ref, cb_ref, bias_ref, A_ref, exp_ref, dskip_ref, nw_ref, o_ref,
                ext_scr, xbc_scr, s_scr):
    Q = SSD_CHUNK
    GW = SSD_GROUP_WIDTH
    NS = SSD_STATE
    R = h_ref.shape[0]

    @pl.when(pl.program_id(0) == 0)
    def _():
        s_scr[...] = jnp.zeros_like(s_scr)
        ext_scr[0:SUBLANES, :] = jnp.zeros((SUBLANES, ext_scr.shape[1]), F32)

    cols = _dot(h_ref[...], w_ref[...])
    ext_scr[SUBLANES:SUBLANES + R, :] = cols[:, 0:SSD_XBC]
    acc = cols[:, 0:SSD_XBC] * cw_ref[SSD_CONV - 1:SSD_CONV, :] + cb_ref[...]
    for j in range(1, SSD_CONV):
        acc = acc + ext_scr[SUBLANES - j:SUBLANES - j + R, :] * cw_ref[SSD_CONV - 1 - j:SSD_CONV - j, :]
    ext_scr[0:SUBLANES, :] = ext_scr[R:R + SUBLANES, :]
    xbc_scr[...] = acc * _sigmoid(acc)

    incl = _tril(Q)
    tri = incl.astype(BF16)
    triu = jnp.logical_not(_tril(Q, strict=True)).astype(BF16)
    expand = exp_ref[...]
    first_half = lax.broadcasted_iota(jnp.int32, (Q, LANES), 1) < SSD_HEAD_DIM
    heads_per_group = SSD_HEADS // SSD_GROUPS
    for c in range(R // Q):
        rows = slice(c * Q, (c + 1) * Q)
        xs = xbc_scr[rows, 0:SSD_WIDTH]
        z = cols[rows, SSD_XBC:SSD_XBC + SSD_WIDTH]
        dt = _softplus(cols[rows, SSD_XBC + SSD_WIDTH:] + bias_ref[...])
        a3 = _split3(dt * A_ref[...])
        a_cum = sum(_dot(tri, t) for t in a3)
        a_cumT = sum(_dot_tn(t, triu) for t in a3)
        a_end = a_cum[Q - 1:Q, :]
        dt_x = _dot(dt.astype(BF16), expand)
        eac_x = _dot(jnp.exp(a_cum).astype(BF16), expand)
        dte_x = _dot(jnp.exp(a_end - a_cum).astype(BF16), expand)
        cd_x = sum(_dot(t, expand) for t in _split3(jnp.broadcast_to(jnp.exp(a_end), (SUBLANES, LANES))))[0:1, :]
        xdt = xs * dt_x
        for g in range(SSD_GROUPS):
            Bg = xbc_scr[rows, SSD_WIDTH + g * NS:SSD_WIDTH + (g + 1) * NS].astype(BF16)
            Cg = xbc_scr[rows, SSD_WIDTH + (SSD_GROUPS + g) * NS:SSD_WIDTH + (SSD_GROUPS + g + 1) * NS].astype(BF16)
            cb = _dot_nt(Cg, Bg)
            gs = slice(g * GW, (g + 1) * GW)
            s_prev = s_scr[:, gs]
            y_g = _dot_bf16(Cg, s_prev) * eac_x[:, gs]
            pieces = []
            for pr in range(heads_per_group // 2):
                ps = slice(g * GW + pr * LANES, g * GW + (pr + 1) * LANES)
                xdt_p = xdt[:, ps]
                acc_p = None
                for e in range(2):
                    hd = g * heads_per_group + pr * 2 + e
                    seg = a_cum[:, hd:hd + 1] - a_cumT[hd:hd + 1, :]
                    m = cb * jnp.exp(jnp.where(incl, seg, -jnp.inf))
                    xm = jnp.where(first_half if e == 0 else jnp.logical_not(first_half), xdt_p, 0.0)
                    t = _dot_bf16(m, xm)
                    acc_p = t if acc_p is None else acc_p + t
                pieces.append(acc_p)
            y_g = y_g + jnp.concatenate(pieces, axis=1)
            s_scr[:, gs] = s_prev * cd_x[:, gs] +_dot_tn(Bg, (dte_x[:, gs] * xdt[:, gs]).astype(BF16))
            y_g = y_g + dskip_ref[:, gs] * xs[:, gs]
            zg = z[:, gs]
            y_g = y_g * (zg * _sigmoid(zg))
            y_g = y_g * lax.rsqrt(jnp.mean(y_g * y_g, axis=-1, keepdims=True) + EPS)
            o_ref[rows, gs] = (y_g * nw_ref[:, gs]).astype(o_ref.dtype)


def ssd_mixer(h, w_ssd, p, rows=512):
    L, D = h.shape
    N = w_ssd.shape[1]
    full = lambda a: pl.BlockSpec(a.shape, lambda i: (0,) * a.ndim)
    small = [p["conv_w"], p["conv_b"], p["dt_bias"], p["A"], p["expand"], p["d_skip"], p["norm_w"]]
    return pl.pallas_call(
        _ssd_kernel,
        grid=(L // rows,),
        in_specs=[pl.BlockSpec((rows, D), lambda i: (i, 0)), full(w_ssd)] + [full(a) for a in small],
        out_specs=pl.BlockSpec((rows, SSD_WIDTH), lambda i: (i, 0)),
        out_shape=jax.ShapeDtypeStruct((L, SSD_WIDTH), BF16),
        scratch_shapes=[pltpu.VMEM((rows + SUBLANES, SSD_XBC), F32), pltpu.VMEM((rows, SSD_XBC), F32),
                        pltpu.VMEM((SSD_STATE, SSD_WIDTH), F32)],
        compiler_params=_cparams("arbitrary"),
    )(h, w_ssd, *small)


def _fox_pre_kernel(h_ref, w_ref, fb_ref, qg_ref, kg_ref, seg_ref, q_out, k_out, v_out, f_out, carry_scr):
    T = h_ref.shape[0]
    W = FOX_WIDTH

    @pl.when(pl.program_id(0) == 0)
    def _():
        carry_scr[...] = jnp.zeros_like(carry_scr)

    cols = _dot(h_ref[...], w_ref[...])
    seg = seg_ref[...]
    q = cols[:, 0:W]
    k = cols[:, W:2 * W]
    inv_d = 1.0 / FOX_HEAD_DIM
    qn = q * lax.rsqrt(_split_dot(q * q, seg) * inv_d + EPS) * qg_ref[...]
    kn = k * lax.rsqrt(_split_dot(k * k, seg) * inv_d + EPS) * kg_ref[...]
    q_out[...] = qn.astype(q_out.dtype)
    k_out[...] = kn.astype(k_out.dtype)
    v_out[...] = cols[:, 2 * W:3 * W].astype(v_out.dtype)
    f = cols[:, 3 * W:3 * W + LANES] + fb_ref[...]
    ls = -_softplus(-f)
    cum = _dot_hi(_tril(T).astype(F32), ls) + carry_scr[0:1, :]
    f_out[...] = cum * LOG2E
    carry_scr[...] = jnp.broadcast_to(cum[T - 1:T, :], carry_scr.shape)


def fox_pre(h, w_fox, p, tm=512):
    L, D = h.shape
    full = lambda a: pl.BlockSpec(a.shape, lambda i: (0,) * a.ndim)
    rows = lambda n: pl.BlockSpec((tm, n), lambda i: (i, 0))
    small = [w_fox, p["f_bias"], p["q_gain"], p["k_gain"], p["seg"]]
    return pl.pallas_call(
        _fox_pre_kernel,
        grid=(L // tm,),
        in_specs=[rows(D)] + [full(a) for a in small],
        out_specs=[rows(FOX_WIDTH)] * 3 + [rows(LANES)],
        out_shape=[jax.ShapeDtypeStruct((L, FOX_WIDTH), BF16)] * 3 + [jax.ShapeDtypeStruct((L, LANES), F32)],
        scratch_shapes=[pltpu.VMEM((SUBLANES, LANES), F32)],
        compiler_params=_cparams("arbitrary"),
    )(h, *small)


def _fox_kernel(lo_ref, q_ref, k_ref, v_ref, fk_ref, o_ref, m_scr, acc_scr, s0_scr, s1_scr):
    tq = q_ref.shape[0]
    tk = FOX_TK
    pair = pl.program_id(0)
    qi = pl.program_id(1)
    is_a = lax.broadcasted_iota(jnp.int32, (tq, LANES), 1) < FOX_HEAD_DIM
    q = q_ref[...]
    zero = jnp.zeros_like(q)
    qs = (jnp.where(is_a, q, zero), jnp.where(is_a, zero, q))
    m_scr[...] = jnp.full(m_scr.shape, -jnp.inf, F32)
    key_is_a = lax.broadcasted_iota(jnp.int32, (tk, LANES), 1) < FOX_HEAD_DIM
    acc_scr[...] = jnp.zeros_like(acc_scr)
    rel = lax.broadcasted_iota(jnp.int32, (tq, tk), 0) - lax.broadcasted_iota(jnp.int32, (tq, tk), 1)
    last = (qi * tq) // tk

    def fill(s_ref, kb, diagonal=False):
        ks = pl.ds(pl.multiple_of(kb * tk, tk), tk)
        k_blk = k_ref[ks, :]
        for e in range(2):
            s = _dot_nt(qs[e], k_blk) - fk_ref[pl.ds(2 * pair + e, 1), ks]
            if diagonal:
                s = jnp.where(rel >= kb * tk - qi * tq, s, -jnp.inf)
            s_ref[e] = s

    def consume(s_ref, kb):
        v_blk = v_ref[pl.ds(pl.multiple_of(kb * tk, tk), tk), :]
        one = jnp.ones_like(v_blk)
        v_one = (jnp.where(key_is_a, v_blk, one), jnp.where(key_is_a, one, v_blk))
        for e in range(2):
            s = s_ref[e]
            m_old = m_scr[e]
            m_new = jnp.maximum(m_old, jnp.max(s, axis=-1, keepdims=True))
            pexp = jnp.exp2(s - m_new)
            acc_scr[e] = jnp.exp2(m_old - m_new) * acc_scr[e] + _dot(pexp.astype(BF16), v_one[e])
            m_scr[e] = m_new

    n = last - jnp.minimum(lo_ref[2 * pair, qi], lo_ref[2 * pair + 1, qi])
    fill(s0_scr, last, diagonal=True)

    def two_steps(u, carry):
        kb = last - 2 * u
        fill(s1_scr, kb - 1)
        consume(s0_scr, kb)
        fill(s0_scr, kb - 2)
        consume(s1_scr, kb - 1)
        return carry

    lax.fori_loop(0, n // 2, two_steps, 0)

    @pl.when(n % 2 == 1)
    def _():
        fill(s1_scr, last - n)
        consume(s0_scr, last - n + 1)
        consume(s1_scr, last - n)

    @pl.when(n % 2 == 0)
    def _():
        consume(s0_scr, last - n)

    acc_a = acc_scr[0]
    acc_b = acc_scr[1]
    half = FOX_HEAD_DIM
    o_ref[...] = jnp.where(is_a, acc_a / pltpu.roll(acc_a, half, 1),
                           acc_b / pltpu.roll(acc_b, half, 1)).astype(o_ref.dtype)


def fox_attention(q, k, v, FT, lo):
    L, W = q.shape
    H = FT.shape[0]
    tq = FOX_TQ
    grid_spec = pltpu.PrefetchScalarGridSpec(
        num_scalar_prefetch=1,
        grid=(W // LANES, L // tq),
        in_specs=[pl.BlockSpec((tq, LANES), lambda p, i, lo_r: (i, p)),
                  pl.BlockSpec((L, LANES), lambda p, i, lo_r: (0, p)),
                  pl.BlockSpec((L, LANES), lambda p, i, lo_r: (0, p)),
                  pl.BlockSpec((H, L), lambda p, i, lo_r: (0, 0))],
        out_specs=pl.BlockSpec((tq, LANES), lambda p, i, lo_r: (i, p)),
        scratch_shapes=[pltpu.VMEM((2, tq, 1), F32), pltpu.VMEM((2, tq, LANES), F32),
                        pltpu.VMEM((2, tq, FOX_TK), F32), pltpu.VMEM((2, tq, FOX_TK), F32)],
    )
    return pl.pallas_call(
        _fox_kernel,
        grid_spec=grid_spec,
        out_shape=jax.ShapeDtypeStruct((L, W), BF16),
        compiler_params=_cparams("arbitrary", "arbitrary"),
    )(lo, q, k, v, FT)


def fox_first_block(FT, bound2):
    f_first = FT[:, ::FOX_TQ]
    f_last = FT[:, FOX_TK - 1::FOX_TK]
    gap = f_first[:, :, None] - f_last[:, None, :] + 2.0 * bound2
    last = (jnp.arange(f_first.shape[1]) * FOX_TQ) // FOX_TK
    needed = jnp.logical_or(gap >= -152.0, jnp.arange(f_last.shape[1])[None, None, :] >= last[None, :, None])
    return jnp.sum(jnp.cumsum(needed.astype(jnp.int32), axis=-1) == 0, axis=-1).astype(jnp.int32)


def _top2_combine(h, wr):
    T = h.shape[0]
    lane = lax.broadcasted_iota(jnp.int32, (T, LANES), 1)
    h_hi = h.astype(BF16)
    h_lo = (h - h_hi.astype(F32)).astype(BF16)
    w_hi = wr.astype(BF16)
    w_lo = (wr - w_hi.astype(F32)).astype(BF16)
    logits = _dot(h_hi, w_hi) + (_dot(h_hi, w_lo) + _dot(h_lo, w_hi))
    logits = jnp.where(lane < N_EXPERTS, logits, -jnp.inf)
    m1 = jnp.max(logits, axis=-1, keepdims=True)
    i1 = jnp.min(jnp.where(logits == m1, lane, LANES), axis=-1, keepdims=True)
    rest = jnp.where(lane == i1, -jnp.inf, logits)
    m2 = jnp.max(rest, axis=-1, keepdims=True)
    i2 = jnp.min(jnp.where(rest == m2, lane, LANES), axis=-1, keepdims=True)
    e2 = jnp.exp(m2 - m1)
    return jnp.where(lane == i1, 1.0 / (1.0 + e2), 0.0) + jnp.where(lane == i2, e2 / (1.0 + e2), 0.0)


def _merge_kernel(with_router, *refs):
    (x_ref, hm_ref, yn_ref, bonus_ref, g_ref, yssd_ref, yfox_ref, wgate_ref, gb_ref, lnw_ref, lnb_ref,
     prw_ref, pssd_ref, pfox_ref, wout_ref, gm_ref, gain_ref, sc_ref, sh_ref) = refs[:19]
    D = D_MODEL
    hm = hm_ref[...]
    y_rw = (yn_ref[...] * lnw_ref[...] + lnb_ref[...] + bonus_ref[...]) * g_ref[...]
    merged = None
    for b, (y_b, proj_ref) in enumerate(((y_rw, prw_ref), (yssd_ref[...], pssd_ref), (yfox_ref[...], pfox_ref))):
        cols = slice(b * D, (b + 1) * D)
        gate = _sigmoid(_dot(hm, wgate_ref[:, cols]) + gb_ref[:, cols])
        term = gate * _dot_bf16(y_b, proj_ref[...])
        merged = term if merged is None else merged + term
    x_new = x_ref[...] + gm_ref[...] * _dot_bf16(merged, wout_ref[...])
    hf = _norm_mod(x_new, gain_ref[...], sc_ref[...], sh_ref[...])
    if with_router:
        wr_ref, o_ref, h_ref, comb_ref = refs[19:]
        comb_ref[...] = _top2_combine(hf, wr_ref[...])
    else:
        o_ref, h_ref = refs[19:]
    o_ref[...] = x_new
    h_ref[...] = hf.astype(h_ref.dtype)


def merge(x, hm, yn, bonus, g, y_ssd, y_fox, p, router=None, tm=256):
    L, D = x.shape
    full = lambda a: pl.BlockSpec(a.shape, lambda i: (0,) * a.ndim)
    rows = lambda n: pl.BlockSpec((tm, n), lambda i: (i, 0))
    small = [p["w_gate"], p["gate_b"], p["lnx_w"], p["lnx_b"], p["proj_rw"], p["proj_ssd"], p["proj_fox"],
             p["w_out"], p["g_m"], p["norm_gain"], p["sc_f"], p["sh_f"]]
    out_specs = [rows(D), rows(D)]
    out_shape = [jax.ShapeDtypeStruct((L, D), F32), jax.ShapeDtypeStruct((L, D), BF16)]
    if router is not None:
        small.append(router)
        out_specs.append(rows(LANES))
        out_shape.append(jax.ShapeDtypeStruct((L, LANES), F32))
    return pl.pallas_call(
        functools.partial(_merge_kernel, router is not None),
        grid=(L // tm,),
        in_specs=[rows(D), rows(D), rows(RW_WIDTH), rows(RW_WIDTH), rows(RW_WIDTH), rows(SSD_WIDTH),
                  rows(FOX_WIDTH)] + [full(a) for a in small],
        out_specs=out_specs,
        out_shape=out_shape,
        compiler_params=_cparams("arbitrary"),
    )(x, hm, yn, bonus, g, y_ssd, y_fox, *small)


def _ffn_kernel(with_next, *refs):
    h_ref, x_ref, wg_ref, wu_ref, wd_ref, gf_ref = refs[:6]
    j = pl.program_id(1)
    acc_scr = refs[-1]

    @pl.when(j == 0)
    def _():
        acc_scr[...] = jnp.zeros_like(acc_scr)

    h = h_ref[...]
    gte = _dot(h, wg_ref[...])
    up = _dot(h, wu_ref[...])
    act = gte * _sigmoid(gte) * up
    acc_scr[...] += _dot(act.astype(BF16), wd_ref[...])

    @pl.when(j == pl.num_programs(1) - 1)
    def _():
        x_new = x_ref[...] + gf_ref[...] * acc_scr[...]
        if with_next:
            gain_ref, sc_ref, sh_ref, o_ref, hn_ref = refs[6:11]
            hn_ref[...] = _norm_mod(x_new, gain_ref[...], sc_ref[...], sh_ref[...]).astype(hn_ref.dtype)
        else:
            o_ref = refs[6]
        o_ref[...] = x_new


def ffn_dense(h, x, w_gu, w_down, g_f, next_norm=None, tm=512, tf=1408):
    L, D = x.shape
    Fh = w_down.shape[0]
    nf = Fh // tf
    row = pl.BlockSpec((1, D), lambda i, j: (0, 0))
    tile = pl.BlockSpec((tm, D), lambda i, j: (i, 0))
    args = [h, x, w_gu, w_gu, w_down, g_f]
    in_specs = [tile, tile,
                pl.BlockSpec((D, tf), lambda i, j: (0, j)),
                pl.BlockSpec((D, tf), lambda i, j: (0, j + nf)),
                pl.BlockSpec((tf, D), lambda i, j: (j, 0)), row]
    out_specs = [tile]
    out_shape = [jax.ShapeDtypeStruct((L, D), F32)]
    if next_norm is not None:
        args += list(next_norm)
        in_specs += [row, row, row]
        out_specs.append(tile)
        out_shape.append(jax.ShapeDtypeStruct((L, D), BF16))
    out = pl.pallas_call(
        functools.partial(_ffn_kernel, next_norm is not None),
        grid=(L // tm, nf),
        in_specs=in_specs,
        out_specs=out_specs,
        out_shape=out_shape,
        scratch_shapes=[pltpu.VMEM((tm, D), F32)],
        compiler_params=_cparams("arbitrary", "arbitrary"),
    )(*args)
    return out if next_norm is not None else (out[0], None)


def _moe_slots(slot_scr, comb_ref):
    slotval = slot_scr[...]
    comb = comb_ref[...]
    s1 = jnp.max(slotval, axis=-1, keepdims=True)
    first = slotval == s1
    c1 = jnp.sum(jnp.where(first, comb, 0.0), axis=-1, keepdims=True)
    rest = jnp.where(first, -1.0, slotval)
    s2 = jnp.max(rest, axis=-1, keepdims=True)
    c2 = jnp.sum(jnp.where(jnp.logical_and(rest == s2, rest >= 0.0), comb, 0.0), axis=-1, keepdims=True)
    return s1, s2, c1, c2


def _moe_kernel(nsb_ref, off_ref, h_ref, x_ref, comb_ref, wg_ref, wu_ref, wd_ref, gf_ref, o_ref,
                slot_scr, cslot_scr, xy_scr, acc_scr):
    i = pl.program_id(0)
    e = pl.program_id(1)
    j = pl.program_id(2)
    last_j = pl.num_programs(2) - 1
    T = h_ref.shape[0]
    S = xy_scr.shape[0]
    SB = MOE_SB

    @pl.when(jnp.logical_and(e == 0, j == 0))
    def _():
        lane = lax.broadcasted_iota(jnp.int32, (LANES, LANES), 1)
        off = jnp.zeros((LANES, LANES), F32)
        for ex in range(N_EXPERTS):
            off = jnp.where(lane == ex, off_ref[i, ex].astype(F32), off)
        before = _tril(LANES, strict=True).astype(BF16)
        ones = jnp.ones((LANES, LANES), BF16)
        for rb in range(T // LANES):
            rows = slice(rb * LANES, (rb + 1) * LANES)
            sel = comb_ref[rows, :] > 0.0
            sel_b = jnp.where(sel, 1.0, 0.0).astype(BF16)
            slot_scr[rows, :] = jnp.where(sel, _dot(before, sel_b) + off, -1.0)
            off = off + _dot(ones, sel_b)
        s1, s2, c1, c2 = _moe_slots(slot_scr, comb_ref)
        used = off_ref[i, N_EXPERTS - 1] + nsb_ref[i, N_EXPERTS - 1] * SB
        for cb in range(S // MOE_CB):
            chunk = slice(cb * MOE_CB, (cb + 1) * MOE_CB)

            @pl.when(used > cb * MOE_CB)
            def _():
                slot = (lax.broadcasted_iota(jnp.int32, (T, MOE_CB), 1) + cb * MOE_CB).astype(F32)
                w = jnp.where(s1 == slot, c1, 0.0) + jnp.where(s2 == slot, c2, 0.0)
                pt = jnp.where(w > 0.0, 1.0, 0.0).astype(BF16)
                xy_scr[chunk, :] = _dot_tn(pt, h_ref[...]).astype(BF16)
                w_hi = w.astype(BF16)
                w_lo = (w - w_hi.astype(F32)).astype(BF16)
                ones_t = jnp.ones((T, LANES), BF16)
                cslot_scr[chunk, :] = _dot_tn(w_hi, ones_t) + _dot_tn(w_lo, ones_t)

            @pl.when(used <= cb * MOE_CB)
            def _():
                xy_scr[chunk, :] = jnp.zeros((MOE_CB, xy_scr.shape[1]), BF16)
                cslot_scr[chunk, :] = jnp.zeros((MOE_CB, LANES), F32)

    base = off_ref[i, e]
    nb = nsb_ref[i, e]

    def ffn_block(r0, n_rows):
        rows = pl.ds(pl.multiple_of(base + r0, SB), n_rows)
        arows = pl.ds(pl.multiple_of(r0, SB), n_rows)
        xb = xy_scr[rows, :]
        gte = _dot(xb, wg_ref[0])
        up = _dot(xb, wu_ref[0])
        part = _dot((gte * _sigmoid(gte) * up).astype(BF16), wd_ref[0])

        @pl.when(j == 0)
        def _():
            acc_scr[arows, :] = part

        @pl.when(jnp.logical_and(j > 0, j < last_j))
        def _():
            acc_scr[arows, :] += part

        @pl.when(j == last_j)
        def _():
            xy_scr[rows, :] = ((acc_scr[arows, :] + part) * cslot_scr[rows, 0:1]).astype(BF16)

    def quad_body(p, carry):
        ffn_block(p * (4 * SB), 4 * SB)
        return carry

    lax.fori_loop(0, nb // 4, quad_body, 0)
    for rest in range(1, 4):
        @pl.when(nb % 4 == rest)
        def _():
            ffn_block((nb - rest) * SB, rest * SB)

    @pl.when(jnp.logical_and(e == pl.num_programs(1) - 1, j == last_j))
    def _():
        s1, s2, _, _ = _moe_slots(slot_scr, comb_ref)
        total = jnp.zeros((T, xy_scr.shape[1]), F32)
        for cb in range(S // MOE_CB):
            slot = (lax.broadcasted_iota(jnp.int32, (T, MOE_CB), 1) + cb * MOE_CB).astype(F32)
            pt = jnp.where(jnp.logical_or(s1 == slot, s2 == slot), 1.0, 0.0).astype(BF16)
            total = total + _dot(pt, xy_scr[cb * MOE_CB:(cb + 1) * MOE_CB, :])
        o_ref[...] = x_ref[...] + gf_ref[...] * total


def ffn_moe(h, x, comb, w_gu, w_down, g_f, tm=1024, tf=896):
    L, D = x.shape
    E, Fh, _ = w_down.shape
    nf = Fh // tf
    assert nf >= 2 and E == N_EXPERTS
    nt = L // tm
    counts = jnp.sum((comb[:, :E] > 0.0).reshape(nt, tm, E), axis=1, dtype=jnp.int32)
    nsb = (counts + (MOE_SB - 1)) // MOE_SB
    off = (jnp.cumsum(nsb, axis=1) - nsb) * MOE_SB
    idx = lambda f: (lambda i, e, j, n, o: f(i, e, j))
    grid_spec = pltpu.PrefetchScalarGridSpec(
        num_scalar_prefetch=2,
        grid=(nt, E, nf),
        in_specs=[pl.BlockSpec((tm, D), idx(lambda i, e, j: (i, 0))),
                  pl.BlockSpec((tm, D), idx(lambda i, e, j: (i, 0))),
                  pl.BlockSpec((tm, LANES), idx(lambda i, e, j: (i, 0))),
                  pl.BlockSpec((1, D, tf), idx(lambda i, e, j: (e, 0, j))),
                  pl.BlockSpec((1, D, tf), idx(lambda i, e, j: (e, 0, j + nf))),
                  pl.BlockSpec((1, tf, D), idx(lambda i, e, j: (e, j, 0))),
                  pl.BlockSpec((1, D), idx(lambda i, e, j: (0, 0)))],
        out_specs=pl.BlockSpec((tm, D), idx(lambda i, e, j: (i, 0))),
        scratch_shapes=[pltpu.VMEM((tm, LANES), F32),
                        pltpu.VMEM((TOP_K * tm + E * MOE_SB, LANES), F32),
                        pltpu.VMEM((TOP_K * tm + E * MOE_SB, D), BF16),
                        pltpu.VMEM((tm, D), F32)],
    )
    return pl.pallas_call(
        _moe_kernel,
        grid_spec=grid_spec,
        out_shape=jax.ShapeDtypeStruct((L, D), F32),
        compiler_params=_cparams("arbitrary", "arbitrary", "arbitrary"),
    )(nsb, off, h, x, comb, w_gu, w_gu, w_down, g_f)


def _seg_matrix(width, head_dim):
    idx = jnp.arange(width) // head_dim
    return (idx[:, None] == idx[None, :]).astype(BF16)


def _pad_cols(w, n):
    return jnp.pad(w, ((0, 0), (0, n - w.shape[1])))


def _pad_rows(w, n):
    return jnp.pad(w, ((0, n - w.shape[0]), (0, 0)))


def rwkv_branch(h, w_rw, p, v_first):
    r, k, v, lw, a, b, g, bonus = rw_pre(h, w_rw, p, v_first)
    return wkv7(r, lw, k, v, a, b), bonus, g, v


def fox_branch(h, w_fox, p):
    q, k, v, F = fox_pre(h, w_fox, p)
    FT = jnp.transpose(F[:, :FOX_HEADS])
    lo = fox_first_block(FT, p["bound2"])
    return fox_attention(q, k, v, FT, lo)


def kernel(x, c, ada_w, ada_b, norm_mix, norm_ffn, w_in, rw_mu, rw_w0, rw_w_up, rw_a0, rw_a_up, rw_g_up, rw_k_k, rw_k_a, rw_r_k, rw_lnx_w, rw_lnx_b, rw_v0, rw_v_down, rw_v_up, ssd_conv_w, ssd_conv_b, ssd_dt_bias, ssd_a_log, ssd_d, ssd_norm, fox_f_bias, fox_q_gain, fox_k_gain, gate_b, proj_rw, proj_ssd, proj_fox, w_out, ffn_w_gu, ffn_w_down, moe_router, moe_w_gu, moe_w_down):
    depth = w_in.shape[0]
    D = D_MODEL
    xs = x[0]
    row = lambda t: t.reshape(1, -1).astype(F32)
    seg64 = _seg_matrix(RW_WIDTH, RW_HEAD_DIM)
    rw_cols = 3 * RW_WIDTH + RW_DECAY_LORA + RW_AAA_LORA + RW_GATE_LORA
    ssd_cols = SSD_WIDTH + SSD_XBC + SSD_HEADS
    fox_cols = 3 * FOX_WIDTH + FOX_HEADS
    expand = (jnp.arange(LANES)[:, None] == (jnp.arange(SSD_WIDTH) // SSD_HEAD_DIM)[None, :]).astype(BF16)
    v_first = None
    mods = []
    for l in range(depth):
        mod = adaln_mod(c, ada_w[l], ada_b[l])
        mods.append([mod[:, i * D:(i + 1) * D] for i in range(6)])
    h = None
    for l in range(depth):
        sh_m, sc_m, g_m, sh_f, sc_f, g_f = mods[l]
        if h is None:
            h = norm_mod(xs, row(norm_mix[l]), sc_m, sh_m)

        wl = w_in[l]
        o = 0
        w_r = wl[:, o:o + rw_cols]; o += rw_cols
        w_s = wl[:, o:o + ssd_cols]; o += ssd_cols
        w_f = wl[:, o:o + fox_cols]; o += fox_cols
        w_g = wl[:, o:]
        W3 = 3 * RW_WIDTH
        o_a = W3 + RW_DECAY_LORA
        o_g = o_a + RW_AAA_LORA

        def rw_layout(t, hv):
            parts = [t[:, :W3], _pad_cols(t[:, W3:o_a], LANES), _pad_cols(t[:, o_a:o_g], LANES),
                     _pad_cols(t[:, o_g:], 2 * LANES), _pad_cols(hv, LANES)]
            return jnp.concatenate(parts, axis=1)

        if l == 0:
            hv_w = jnp.zeros((D, RW_VRES_LORA), F32)
        else:
            hv_w = rw_v_down[l - 1]
        w_rw = rw_layout(w_r, hv_w).astype(BF16)
        mu = rw_layout(rw_mu[l].reshape(1, -1), jnp.zeros((1, RW_VRES_LORA), F32))
        rw_p = dict(mu=mu, w0=row(rw_w0[l]), w_up=_pad_rows(rw_w_up[l], LANES).astype(BF16), a0=row(rw_a0[l]),
                    a_up=_pad_rows(rw_a_up[l], LANES).astype(BF16),
                    g_up=_pad_rows(rw_g_up[l], 2 * LANES).astype(BF16),
                    k_k=row(rw_k_k[l]), k_a=row(rw_k_a[l]), r_k=row(rw_r_k[l]), seg=seg64)
        if l > 0:
            rw_p.update(v0=row(rw_v0[l - 1]), v_up=_pad_rows(rw_v_up[l - 1], LANES).astype(BF16))
        yn, bonus, g_rw, v_cur = rwkv_branch(h, w_rw, rw_p, v_first)
        if l == 0:
            v_first = v_cur

        w_ssd = jnp.concatenate([w_s[:, SSD_WIDTH:SSD_WIDTH + SSD_XBC], w_s[:, :SSD_WIDTH],
                                 _pad_cols(w_s[:, SSD_WIDTH + SSD_XBC:], LANES)], axis=1).astype(BF16)
        A = -jnp.exp(ssd_a_log[l].astype(F32))
        ssd_p = dict(conv_w=ssd_conv_w[l], conv_b=row(ssd_conv_b[l]),
                     dt_bias=_pad_cols(row(ssd_dt_bias[l]), LANES),
                     A=_pad_cols(row(A), LANES), expand=expand,
                     d_skip=row(jnp.repeat(ssd_d[l], SSD_HEAD_DIM)), norm_w=row(ssd_norm[l]))
        y_ssd = ssd_mixer(h, w_ssd, ssd_p)

        w_fox = jnp.concatenate([w_f[:, :3 * FOX_WIDTH], _pad_cols(w_f[:, 3 * FOX_WIDTH:], LANES)],
                                axis=1).astype(BF16)
        bound2 = (1.02 * FOX_HEAD_DIM ** 0.5 * LOG2E) * jnp.max(jnp.abs(fox_q_gain[l])) * jnp.max(jnp.abs(fox_k_gain[l]))
        fox_p = dict(f_bias=_pad_cols(row(fox_f_bias[l]), LANES),
                     q_gain=row(jnp.tile(fox_q_gain[l], FOX_HEADS)) * (FOX_HEAD_DIM ** -0.5 * LOG2E),
                     k_gain=row(jnp.tile(fox_k_gain[l], FOX_HEADS)), seg=seg64, bound2=bound2)
        y_fox = fox_branch(h, w_fox, fox_p)

        mp = dict(w_gate=w_g.astype(BF16), gate_b=row(gate_b[l]), lnx_w=row(rw_lnx_w[l]), lnx_b=row(rw_lnx_b[l]),
                  proj_rw=proj_rw[l].astype(BF16), proj_ssd=proj_ssd[l].astype(BF16),
                  proj_fox=proj_fox[l].astype(BF16), w_out=w_out[l].astype(BF16), g_m=g_m,
                  norm_gain=row(norm_ffn[l]), sc_f=sc_f, sh_f=sh_f)

        if l % 2 == 0:
            xs, hf = merge(xs, h, yn, bonus, g_rw, y_ssd, y_fox, mp)
            next_norm = None
            if l + 1 < depth:
                next_norm = (row(norm_mix[l + 1]), mods[l + 1][1], mods[l + 1][0])
            xs, h = ffn_dense(hf, xs, ffn_w_gu[l // 2].astype(BF16), ffn_w_down[l // 2].astype(BF16), g_f,
                              next_norm)
        else:
            xs, hf, comb = merge(xs, h, yn, bonus, g_rw, y_ssd, y_fox, mp,
                                 router=_pad_cols(moe_router[l // 2], LANES))
            xs = ffn_moe(hf, xs, comb, moe_w_gu[l // 2].astype(BF16), moe_w_down[l // 2].astype(BF16), g_f)
            h = None
    return xs[None]
```

```python
import functools
import math

import jax
import jax.numpy as jnp
from jax import lax
from jax.experimental import pallas as pl
from jax.experimental.pallas import tpu as pltpu

F32 = jnp.float32
BF16 = jnp.bfloat16
HIGHEST = lax.Precision.HIGHEST

D_MODEL = 1024
LANES = 128
SUBLANES = 8

RW_HEADS = 8
RW_HEAD_DIM = 64
RW_WIDTH = RW_HEADS * RW_HEAD_DIM
RW_DECAY_LORA = 64
RW_AAA_LORA = 64
RW_VRES_LORA = 32
RW_GATE_LORA = 160
GN_EPS = 64e-5
RW_OFF_WLO = 3 * RW_WIDTH
RW_OFF_ALO = RW_OFF_WLO + LANES
RW_OFF_GLO = RW_OFF_ALO + LANES
RW_OFF_HV = RW_OFF_GLO + 2 * LANES
RW_PAD_COLS = RW_OFF_HV + LANES
RW_CHUNK = 64

SSD_HEADS = 16
SSD_HEAD_DIM = 64
SSD_WIDTH = SSD_HEADS * SSD_HEAD_DIM
SSD_GROUPS = 4
SSD_STATE = 128
SSD_CONV = 4
SSD_CHUNK = 128
SSD_XBC = SSD_WIDTH + 2 * SSD_GROUPS * SSD_STATE
SSD_GROUP_WIDTH = SSD_WIDTH // SSD_GROUPS

FOX_HEADS = 8
FOX_HEAD_DIM = 64
FOX_WIDTH = FOX_HEADS * FOX_HEAD_DIM
FOX_TQ = 512
FOX_TK = 512
LOG2E = 1.4426950408889634

FFN_DENSE = 2816
N_EXPERTS = 8
TOP_K = 2
FFN_EXPERT = 3584
MOE_SB = 128
MOE_CB = 512
EPS = 1e-6

VMEM_LIMIT = 56 * 1024 * 1024


def _cparams(*sem):
    return pltpu.CompilerParams(dimension_semantics=sem, vmem_limit_bytes=VMEM_LIMIT)


def _sigmoid(x):
    return 1.0 / (1.0 + jnp.exp(-x))


def _softplus(x):
    return jnp.maximum(x, 0.0) + jnp.log(1.0 + jnp.exp(-jnp.abs(x)))


def _dot(a, b):
    return jnp.dot(a, b, preferred_element_type=F32)


def _dot_bf16(a, b):
    return jnp.dot(a.astype(BF16), b.astype(BF16), preferred_element_type=F32)


def _dot_hi(a, b):
    return jnp.dot(a, b, precision=HIGHEST, preferred_element_type=F32)


def _dot_nt(a, b, precision=None):
    return lax.dot_general(a, b, (((1,), (1,)), ((), ())), precision=precision, preferred_element_type=F32)


def _dot_tn(a, b, precision=None):
    return lax.dot_general(a, b, (((0,), (0,)), ((), ())), precision=precision, preferred_element_type=F32)


def _split_dot(x, ones_bf16):
    return _dot(x.astype(BF16), ones_bf16)


def _split3(x):
    hi = x.astype(BF16)
    r1 = x - hi.astype(F32)
    mid = r1.astype(BF16)
    return hi, mid, (r1 - mid.astype(F32)).astype(BF16)


def _tril(n, strict=False):
    r = lax.broadcasted_iota(jnp.int32, (n, n), 0)
    c = lax.broadcasted_iota(jnp.int32, (n, n), 1)
    return (r > c) if strict else (r >= c)


def _mod_kernel(c_ref, w_ref, b_ref, o_ref):
    c = c_ref[...]
    o_ref[...] = _dot_hi(c * _sigmoid(c), w_ref[...]) + b_ref[...]


def adaln_mod(c, w, b):
    d, n = w.shape
    tn = 1024
    c8 = jnp.broadcast_to(c, (SUBLANES, d))
    out = pl.pallas_call(
        _mod_kernel,
        grid=(n // tn,),
        in_specs=[pl.BlockSpec((SUBLANES, d), lambda j: (0, 0)),
                  pl.BlockSpec((d, tn), lambda j: (0, j)),
                  pl.BlockSpec((1, tn), lambda j: (0, j))],
        out_specs=pl.BlockSpec((SUBLANES, tn), lambda j: (0, j)),
        out_shape=jax.ShapeDtypeStruct((SUBLANES, n), F32),
        compiler_params=_cparams("arbitrary"),
    )(c8, w, b.reshape(1, n))
    return out[:1]


def _norm_mod(x, gain, sc, sh):
    y = x * lax.rsqrt(jnp.mean(x * x, axis=-1, keepdims=True) + EPS)
    return y * gain * (1.0 + sc) + sh


def _norm_kernel(x_ref, gain_ref, sc_ref, sh_ref, h_ref):
    h_ref[...] = _norm_mod(x_ref[...], gain_ref[...], sc_ref[...], sh_ref[...]).astype(h_ref.dtype)


def norm_mod(x, gain, sc, sh, tm=512):
    L, d = x.shape
    row = pl.BlockSpec((1, d), lambda i: (0, 0))
    return pl.pallas_call(
        _norm_kernel,
        grid=(L // tm,),
        in_specs=[pl.BlockSpec((tm, d), lambda i: (i, 0)), row, row, row],
        out_specs=pl.BlockSpec((tm, d), lambda i: (i, 0)),
        out_shape=jax.ShapeDtypeStruct((L, d), BF16),
        compiler_params=_cparams("arbitrary"),
    )(x, gain, sc, sh)


def _rw_pre_kernel(has_vres, *refs):
    if has_vres:
        (h_ref, w_ref, mu_ref, w0_ref, wup_ref, a0_ref, aup_ref, gup_ref, kk_ref, ka_ref, rk_ref, seg_ref,
         vfirst_ref, v0_ref, vup_ref,
         r_out, k_out, v_out, lw_out, a_out, b_out, g_out, bonus_out, ext_scr) = refs
    else:
        (h_ref, w_ref, mu_ref, w0_ref, wup_ref, a0_ref, aup_ref, gup_ref, kk_ref, ka_ref, rk_ref, seg_ref,
         r_out, k_out, v_out, lw_out, a_out, b_out, g_out, bonus_out, ext_scr) = refs
    T = h_ref.shape[0]

    @pl.when(pl.program_id(0) == 0)
    def _():
        ext_scr[0:SUBLANES, :] = jnp.zeros((SUBLANES, ext_scr.shape[1]), F32)

    cur = _dot(h_ref[...], w_ref[...])
    ext_scr[SUBLANES:SUBLANES + T, :] = cur
    prev = ext_scr[SUBLANES - 1:SUBLANES - 1 + T, :]
    ext_scr[0:SUBLANES, :] = cur[T - SUBLANES:T, :]
    s = cur + (prev - cur) * mu_ref[...]

    W = RW_WIDTH
    r = s[:, 0:W]
    k = s[:, W:2 * W]
    v = s[:, 2 * W:3 * W]
    w_lo = s[:, RW_OFF_WLO:RW_OFF_WLO + LANES]
    a_lo = s[:, RW_OFF_ALO:RW_OFF_ALO + LANES]
    g_lo = s[:, RW_OFF_GLO:RW_OFF_GLO + 2 * LANES]
    seg = seg_ref[...]

    wlog = -_softplus(-(w0_ref[...] + _dot_bf16(jnp.tanh(w_lo), wup_ref[...]))) - 0.5
    a = _sigmoid(a0_ref[...] + _dot_bf16(a_lo, aup_ref[...]))
    g = _dot_bf16(_sigmoid(g_lo), gup_ref[...])
    if has_vres:
        hv = s[:, RW_OFF_HV:RW_OFF_HV + LANES]
        v = v + (vfirst_ref[...].astype(F32) - v) * _sigmoid(v0_ref[...] + _dot_bf16(hv, vup_ref[...]))
    kk = k * kk_ref[...]
    kk = kk / jnp.maximum(jnp.sqrt(_split_dot(kk * kk, seg)), 1e-12)
    k = k * (1.0 + (a - 1.0) * ka_ref[...])
    r_out[...] = r.astype(r_out.dtype)
    k_out[...] = k.astype(k_out.dtype)
    v_out[...] = v.astype(v_out.dtype)
    lw_out[...] = -jnp.exp(wlog)
    a_out[...] = (-kk).astype(a_out.dtype)
    b_out[...] = (kk * a).astype(b_out.dtype)
    g_out[...] = g.astype(g_out.dtype)
    bonus_out[...] = (_split_dot(r * k * rk_ref[...], seg) * v).astype(bonus_out.dtype)


def rw_pre(h, w_rw, p, v_first, tm=512):
    L, D = h.shape
    has_vres = v_first is not None
    full = lambda a: pl.BlockSpec(a.shape, lambda i: (0,) * a.ndim)
    rows = lambda n: pl.BlockSpec((tm, n), lambda i: (i, 0))
    args = [h, w_rw, p["mu"], p["w0"], p["w_up"], p["a0"], p["a_up"], p["g_up"], p["k_k"], p["k_a"], p["r_k"],
            p["seg"]]
    specs = [rows(D)] + [full(a) for a in args[1:]]
    if has_vres:
        args += [v_first, p["v0"], p["v_up"]]
        specs += [rows(RW_WIDTH), full(p["v0"]), full(p["v_up"])]
    out_dtypes = [BF16, BF16, BF16, F32, BF16, BF16, BF16, BF16]
    return pl.pallas_call(
        functools.partial(_rw_pre_kernel, has_vres),
        grid=(L // tm,),
        in_specs=specs,
        out_specs=[rows(RW_WIDTH)] * 8,
        out_shape=[jax.ShapeDtypeStruct((L, RW_WIDTH), dt) for dt in out_dtypes],
        scratch_shapes=[pltpu.VMEM((tm + SUBLANES, RW_PAD_COLS), F32)],
        compiler_params=_cparams("arbitrary"),
    )(*args)


def _wkv_kernel(r_ref, lw_ref, k_ref, v_ref, a_ref, b_ref, y_ref, h_scr):
    C = RW_CHUNK
    C2 = 2 * C
    n_chunks = r_ref.shape[0] // C

    @pl.when(pl.program_id(1) == 0)
    def _():
        h_scr[...] = jnp.zeros_like(h_scr)

    is_a = lax.broadcasted_iota(jnp.int32, (C, LANES), 1) < RW_HEAD_DIM
    ri = lax.broadcasted_iota(jnp.int32, (C2, C2), 0)
    ci = lax.broadcasted_iota(jnp.int32, (C2, C2), 1)
    same = (ri >= C) == (ci >= C)
    strict = jnp.logical_and(same, ri > ci)
    incl = jnp.logical_and(same, ri >= ci)
    eye2 = (ri == ci).astype(F32)
    tri_c = _tril(C).astype(BF16)

    def two(x):
        return jnp.concatenate([jnp.where(is_a, x, 0.0), jnp.where(is_a, 0.0, x)], axis=0)

    chunks = range(n_chunks)
    at2, rt2, bh2, kh2, v2, n_ab, a_ak, m_rb, m_rk, p_end = ([] for _ in range(10))
    for c in chunks:
        sl = pl.ds(c * C, C)
        lw = lw_ref[sl, :]
        lw_hi = lw.astype(BF16)
        lw_lo = (lw - lw_hi.astype(F32)).astype(BF16)
        cs = _dot(tri_c, lw_hi) + _dot(tri_c, lw_lo)
        cs_end = cs[C - 1:C, :]
        e_neg = jnp.exp(-cs)
        e_end = jnp.exp(cs_end - cs)
        a = a_ref[sl, :].astype(F32)
        b = b_ref[sl, :].astype(F32)
        k = k_ref[sl, :].astype(F32)
        at2.append(two(a * jnp.exp(cs - lw)))
        rt2.append(two(r_ref[sl, :].astype(F32) * jnp.exp(cs)))
        bh2.append(two(b * e_end))
        kh2.append(two(k * e_end))
        v2.append(two(v_ref[sl, :].astype(F32)))
        p_end.append(jnp.exp(cs_end))
        quad = _dot_nt(jnp.concatenate([at2[c], rt2[c]], axis=0).astype(BF16),
                       jnp.concatenate([two(b * e_neg), two(k * e_neg)], axis=0).astype(BF16))
        n_ab.append(jnp.where(strict, quad[0:C2, 0:C2], 0.0))
        a_ak.append(jnp.where(strict, quad[0:C2, C2:], 0.0))
        m_rb.append(jnp.where(incl, quad[C2:, 0:C2], 0.0))
        m_rk.append(jnp.where(incl, quad[C2:, C2:], 0.0))
    t_inv = [eye2 + n for n in n_ab]
    n_pow = n_ab
    n_pow = [_dot_bf16(n, n) for n in n_pow]
    for _ in range(int(math.log2(C)) - 2):
        both = [_dot_bf16(jnp.concatenate([n, t], axis=0), n) for n, t in zip(n_pow, t_inv)]
        t_inv = [t + b[C2:, :] for t, b in zip(t_inv, both)]
        n_pow = [b[:C2, :] for b in both]
    t_inv = [t + _dot_bf16(t, n) for t, n in zip(t_inv, n_pow)]
    akv = [_dot_bf16(a_ak[c], v2[c]) for c in chunks]
    au = [_dot_bf16(t_inv[c], jnp.concatenate([at2[c], akv[c]], axis=1)) for c in chunks]
    mau = [_dot_bf16(m_rb[c], au[c]) for c in chunks]
    mkv = [_dot_bf16(m_rk[c], v2[c]) for c in chunks]
    gu = [_dot_tn(bh2[c].astype(BF16), au[c].astype(BF16)) for c in chunks]
    kv = [_dot_tn(kh2[c].astype(BF16), v2[c].astype(BF16)) for c in chunks]
    inv_n = 1.0 / RW_HEAD_DIM
    h = h_scr[...]
    for c in chunks:
        rh2 = rt2[c] + mau[c][:, :LANES]
        g_mat = eye2 * p_end[c] + gu[c][:, :LANES]
        yh = _dot_bf16(jnp.concatenate([rh2, g_mat], axis=0), h)
        h = yh[C2:, :] + gu[c][:, LANES:] + kv[c]
        y2 = yh[:C2, :] + mau[c][:, LANES:] + mkv[c]
        y = y2[:C, :] + y2[C:, :]
        s_all = jnp.sum(y, axis=-1, keepdims=True)
        s_a = jnp.sum(jnp.where(is_a, y, 0.0), axis=-1, keepdims=True)
        yc = y - jnp.where(is_a, s_a, s_all - s_a) * inv_n
        sq = yc * yc
        q_all = jnp.sum(sq, axis=-1, keepdims=True)
        q_a = jnp.sum(jnp.where(is_a, sq, 0.0), axis=-1, keepdims=True)
        yn = yc * lax.rsqrt(jnp.where(is_a, q_a, q_all - q_a) * inv_n + GN_EPS)
        y_ref[pl.ds(c * C, C), :] = yn.astype(y_ref.dtype)
    h_scr[...] = h


def wkv7(r, lw, k, v, a, b, rows=2048):
    L, W = r.shape
    spec = pl.BlockSpec((rows, LANES), lambda p, i: (i, p))
    return pl.pallas_call(
        _wkv_kernel,
        grid=(W // LANES, L // rows),
        in_specs=[spec] * 6,
        out_specs=spec,
        out_shape=jax.ShapeDtypeStruct((L, W), BF16),
        scratch_shapes=[pltpu.VMEM((LANES, LANES), F32)],
        compiler_params=_cparams("arbitrary", "arbitrary"),
    )(r, lw, k, v, a, b)


def _ssd_kernel(h_ref, w_ref, cw_ref, cb_ref, bias_ref, A_ref, exp_ref, dskip_ref, nw_ref, o_ref,
                ext_scr, xbc_scr, s_scr):
    Q = SSD_CHUNK
    GW = SSD_GROUP_WIDTH
    NS = SSD_STATE
    R = h_ref.shape[0]

    @pl.when(pl.program_id(0) == 0)
    def _():
        s_scr[...] = jnp.zeros_like(s_scr)
        ext_scr[0:SUBLANES, :] = jnp.zeros((SUBLANES, ext_scr.shape[1]), F32)

    cols = _dot(h_ref[...], w_ref[...])
    ext_scr[SUBLANES:SUBLANES + R, :] = cols[:, 0:SSD_XBC]
    acc = cols[:, 0:SSD_XBC] * cw_ref[SSD_CONV - 1:SSD_CONV, :] + cb_ref[...]
    for j in range(1, SSD_CONV):
        acc = acc + ext_scr[SUBLANES - j:SUBLANES - j + R, :] * cw_ref[SSD_CONV - 1 - j:SSD_CONV - j, :]
    ext_scr[0:SUBLANES, :] = ext_scr[R:R + SUBLANES, :]
    xbc_scr[...] = acc * _sigmoid(acc)

    incl = _tril(Q)
    tri = incl.astype(BF16)
    triu = jnp.logical_not(_tril(Q, strict=True)).astype(BF16)
    expand = exp_ref[...]
    first_half = lax.broadcasted_iota(jnp.int32, (Q, LANES), 1) < SSD_HEAD_DIM
    heads_per_group = SSD_HEADS // SSD_GROUPS
    for c in range(R // Q):
        rows = slice(c * Q, (c + 1) * Q)
        xs = xbc_scr[rows, 0:SSD_WIDTH]
        z = cols[rows, SSD_XBC:SSD_XBC + SSD_WIDTH]
        dt = _softplus(cols[rows, SSD_XBC + SSD_WIDTH:] + bias_ref[...])
        a3 = _split3(dt * A_ref[...])
        a_cum = sum(_dot(tri, t) for t in a3)
        a_cumT = sum(_dot_tn(t, triu) for t in a3)
        a_end = a_cum[Q - 1:Q, :]
        dt_x = _dot(dt.astype(BF16), expand)
        eac_x = _dot(jnp.exp(a_cum).astype(BF16), expand)
        dte_x = _dot(jnp.exp(a_end - a_cum).astype(BF16), expand)
        cd_x = sum(_dot(t, expand) for t in _split3(jnp.broadcast_to(jnp.exp(a_end), (SUBLANES, LANES))))[0:1, :]
        xdt = xs * dt_x
        for g in range(SSD_GROUPS):
            Bg = xbc_scr[rows, SSD_WIDTH + g * NS:SSD_WIDTH + (g + 1) * NS].astype(BF16)
            Cg = xbc_scr[rows, SSD_WIDTH + (SSD_GROUPS + g) * NS:SSD_WIDTH + (SSD_GROUPS + g + 1) * NS].astype(BF16)
            cb = _dot_nt(Cg, Bg)
            gs = slice(g * GW, (g + 1) * GW)
            s_prev = s_scr[:, gs]
            y_g = _dot_bf16(Cg, s_prev) * eac_x[:, gs]
            pieces = []
            for pr in range(heads_per_group // 2):
                ps = slice(g * GW + pr * LANES, g * GW + (pr + 1) * LANES)
                xdt_p = xdt[:, ps]
                acc_p = None
                for e in range(2):
                    hd = g * heads_per_group + pr * 2 + e
                    seg = a_cum[:, hd:hd + 1] - a_cumT[hd:hd + 1, :]
                    m = cb * jnp.exp(jnp.where(incl, seg, -jnp.inf))
                    xm = jnp.where(first_half if e == 0 else jnp.logical_not(first_half), xdt_p, 0.0)
                    t = _dot_bf16(m, xm)
                    acc_p = t if acc_p is None else acc_p + t
                pieces.append(acc_p)
            y_g = y_g + jnp.concatenate(pieces, axis=1)
            s_scr[:, gs] = s_prev * cd_x[:, gs] +_dot_tn(Bg, (dte_x[:, gs] * xdt[:, gs]).astype(BF16))
            y_g = y_g + dskip_ref[:, gs] * xs[:, gs]
            zg = z[:, gs]
            y_g = y_g * (zg * _sigmoid(zg))
            y_g = y_g * lax.rsqrt(jnp.mean(y_g * y_g, axis=-1, keepdims=True) + EPS)
            o_ref[rows, gs] = (y_g * nw_ref[:, gs]).astype(o_ref.dtype)


def ssd_mixer(h, w_ssd, p, rows=512):
    L, D = h.shape
    N = w_ssd.shape[1]
    full = lambda a: pl.BlockSpec(a.shape, lambda i: (0,) * a.ndim)
    small = [p["conv_w"], p["conv_b"], p["dt_bias"], p["A"], p["expand"], p["d_skip"], p["norm_w"]]
    return pl.pallas_call(
        _ssd_kernel,
        grid=(L // rows,),
        in_specs=[pl.BlockSpec((rows, D), lambda i: (i, 0)), full(w_ssd)] + [full(a) for a in small],
        out_specs=pl.BlockSpec((rows, SSD_WIDTH), lambda i: (i, 0)),
        out_shape=jax.ShapeDtypeStruct((L, SSD_WIDTH), BF16),
        scratch_shapes=[pltpu.VMEM((rows + SUBLANES, SSD_XBC), F32), pltpu.VMEM((rows, SSD_XBC), F32),
                        pltpu.VMEM((SSD_STATE, SSD_WIDTH), F32)],
        compiler_params=_cparams("arbitrary"),
    )(h, w_ssd, *small)


def _fox_pre_kernel(h_ref, w_ref, fb_ref, qg_ref, kg_ref, seg_ref, q_out, k_out, v_out, f_out, carry_scr):
    T = h_ref.shape[0]
    W = FOX_WIDTH

    @pl.when(pl.program_id(0) == 0)
    def _():
        carry_scr[...] = jnp.zeros_like(carry_scr)

    cols = _dot(h_ref[...], w_ref[...])
    seg = seg_ref[...]
    q = cols[:, 0:W]
    k = cols[:, W:2 * W]
    inv_d = 1.0 / FOX_HEAD_DIM
    qn = q * lax.rsqrt(_split_dot(q * q, seg) * inv_d + EPS) * qg_ref[...]
    kn = k * lax.rsqrt(_split_dot(k * k, seg) * inv_d + EPS) * kg_ref[...]
    q_out[...] = qn.astype(q_out.dtype)
    k_out[...] = kn.astype(k_out.dtype)
    v_out[...] = cols[:, 2 * W:3 * W].astype(v_out.dtype)
    f = cols[:, 3 * W:3 * W + LANES] + fb_ref[...]
    ls = -_softplus(-f)
    cum = _dot_hi(_tril(T).astype(F32), ls) + carry_scr[0:1, :]
    f_out[...] = cum * LOG2E
    carry_scr[...] = jnp.broadcast_to(cum[T - 1:T, :], carry_scr.shape)


def fox_pre(h, w_fox, p, tm=512):
    L, D = h.shape
    full = lambda a: pl.BlockSpec(a.shape, lambda i: (0,) * a.ndim)
    rows = lambda n: pl.BlockSpec((tm, n), lambda i: (i, 0))
    small = [w_fox, p["f_bias"], p["q_gain"], p["k_gain"], p["seg"]]
    return pl.pallas_call(
        _fox_pre_kernel,
        grid=(L // tm,),
        in_specs=[rows(D)] + [full(a) for a in small],
        out_specs=[rows(FOX_WIDTH)] * 3 + [rows(LANES)],
        out_shape=[jax.ShapeDtypeStruct((L, FOX_WIDTH), BF16)] * 3 + [jax.ShapeDtypeStruct((L, LANES), F32)],
        scratch_shapes=[pltpu.VMEM((SUBLANES, LANES), F32)],
        compiler_params=_cparams("arbitrary"),
    )(h, *small)


def _fox_kernel(lo_ref, q_ref, k_ref, v_ref, fk_ref, o_ref, m_scr, acc_scr, s0_scr, s1_scr):
    tq = q_ref.shape[0]
    tk = FOX_TK
    pair = pl.program_id(0)
    qi = pl.program_id(1)
    is_a = lax.broadcasted_iota(jnp.int32, (tq, LANES), 1) < FOX_HEAD_DIM
    q = q_ref[...]
    zero = jnp.zeros_like(q)
    qs = (jnp.where(is_a, q, zero), jnp.where(is_a, zero, q))
    m_scr[...] = jnp.full(m_scr.shape, -jnp.inf, F32)
    key_is_a = lax.broadcasted_iota(jnp.int32, (tk, LANES), 1) < FOX_HEAD_DIM
    acc_scr[...] = jnp.zeros_like(acc_scr)
    rel = lax.broadcasted_iota(jnp.int32, (tq, tk), 0) - lax.broadcasted_iota(jnp.int32, (tq, tk), 1)
    last = (qi * tq) // tk

    def fill(s_ref, kb, diagonal=False):
        ks = pl.ds(pl.multiple_of(kb * tk, tk), tk)
        k_blk = k_ref[ks, :]
        for e in range(2):
            s = _dot_nt(qs[e], k_blk) - fk_ref[pl.ds(2 * pair + e, 1), ks]
            if diagonal:
                s = jnp.where(rel >= kb * tk - qi * tq, s, -jnp.inf)
            s_ref[e] = s

    def consume(s_ref, kb):
        v_blk = v_ref[pl.ds(pl.multiple_of(kb * tk, tk), tk), :]
        one = jnp.ones_like(v_blk)
        v_one = (jnp.where(key_is_a, v_blk, one), jnp.where(key_is_a, one, v_blk))
        for e in range(2):
            s = s_ref[e]
            m_old = m_scr[e]
            m_new = jnp.maximum(m_old, jnp.max(s, axis=-1, keepdims=True))
            pexp = jnp.exp2(s - m_new)
            acc_scr[e] = jnp.exp2(m_old - m_new) * acc_scr[e] + _dot(pexp.astype(BF16), v_one[e])
            m_scr[e] = m_new

    lo_a = lo_ref[2 * pair, qi]
    lo_b = lo_ref[2 * pair + 1, qi]
    n = last - jnp.maximum(lo_a, lo_b)
    fill(s0_scr, last, diagonal=True)

    def two_steps(u, carry):
        kb = last - 2 * u
        fill(s1_scr, kb - 1)
        consume(s0_scr, kb)
        fill(s0_scr, kb - 2)
        consume(s1_scr, kb - 1)
        return carry

    lax.fori_loop(0, n // 2, two_steps, 0)

    @pl.when(n % 2 == 1)
    def _():
        fill(s1_scr, last - n)
        consume(s0_scr, last - n + 1)
        consume(s1_scr, last - n)

    @pl.when(n % 2 == 0)
    def _():
        consume(s0_scr, last - n)

    e_far = jnp.where(lo_b < lo_a, 1, 0)
    mine_q = is_a == (e_far == 0)
    mine_k = key_is_a == (e_far == 0)
    q_far = jnp.where(mine_q, q, zero)

    def far_step(i, carry):
        ks = pl.ds(pl.multiple_of((last - n - 1 - i) * tk, tk), tk)
        v_blk = v_ref[ks, :]
        s = _dot_nt(q_far, k_ref[ks, :]) - fk_ref[pl.ds(2 * pair + e_far, 1), ks]
        m_old = m_scr[e_far]
        m_new = jnp.maximum(m_old, jnp.max(s, axis=-1, keepdims=True))
        pexp = jnp.exp2(s - m_new)
        acc_scr[e_far] = (jnp.exp2(m_old - m_new) * acc_scr[e_far]
                          + _dot(pexp.astype(BF16), jnp.where(mine_k, v_blk, jnp.ones_like(v_blk))))
        m_scr[e_far] = m_new
        return carry

    lax.fori_loop(0, jnp.abs(lo_a - lo_b), far_step, 0)

    acc_a = acc_scr[0]
    acc_b = acc_scr[1]
    half = FOX_HEAD_DIM
    o_ref[...] = jnp.where(is_a, acc_a / pltpu.roll(acc_a, half, 1),
                           acc_b / pltpu.roll(acc_b, half, 1)).astype(o_ref.dtype)


def fox_attention(q, k, v, FT, lo):
    L, W = q.shape
    H = FT.shape[0]
    tq = FOX_TQ
    grid_spec = pltpu.PrefetchScalarGridSpec(
        num_scalar_prefetch=1,
        grid=(W // LANES, L // tq),
        in_specs=[pl.BlockSpec((tq, LANES), lambda p, i, lo_r: (i, p)),
                  pl.BlockSpec((L, LANES), lambda p, i, lo_r: (0, p)),
                  pl.BlockSpec((L, LANES), lambda p, i, lo_r: (0, p)),
                  pl.BlockSpec((H, L), lambda p, i, lo_r: (0, 0))],
        out_specs=pl.BlockSpec((tq, LANES), lambda p, i, lo_r: (i, p)),
        scratch_shapes=[pltpu.VMEM((2, tq, 1), F32), pltpu.VMEM((2, tq, LANES), F32),
                        pltpu.VMEM((2, tq, FOX_TK), F32), pltpu.VMEM((2, tq, FOX_TK), F32)],
    )
    return pl.pallas_call(
        _fox_kernel,
        grid_spec=grid_spec,
        out_shape=jax.ShapeDtypeStruct((L, W), BF16),
        compiler_params=_cparams("arbitrary", "arbitrary"),
    )(lo, q, k, v, FT)


def fox_first_block(FT, bound2):
    f_first = FT[:, ::FOX_TQ]
    f_last = FT[:, FOX_TK - 1::FOX_TK]
    gap = f_first[:, :, None] - f_last[:, None, :] + 2.0 * bound2
    last = (jnp.arange(f_first.shape[1]) * FOX_TQ) // FOX_TK
    needed = jnp.logical_or(gap >= -152.0, jnp.arange(f_last.shape[1])[None, None, :] >= last[None, :, None])
    return jnp.sum(jnp.cumsum(needed.astype(jnp.int32), axis=-1) == 0, axis=-1).astype(jnp.int32)


def _top2_combine(h, wr):
    T = h.shape[0]
    lane = lax.broadcasted_iota(jnp.int32, (T, LANES), 1)
    h_hi = h.astype(BF16)
    h_lo = (h - h_hi.astype(F32)).astype(BF16)
    w_hi = wr.astype(BF16)
    w_lo = (wr - w_hi.astype(F32)).astype(BF16)
    logits = _dot(h_hi, w_hi) + (_dot(h_hi, w_lo) + _dot(h_lo, w_hi))
    logits = jnp.where(lane < N_EXPERTS, logits, -jnp.inf)
    m1 = jnp.max(logits, axis=-1, keepdims=True)
    i1 = jnp.min(jnp.where(logits == m1, lane, LANES), axis=-1, keepdims=True)
    rest = jnp.where(lane == i1, -jnp.inf, logits)
    m2 = jnp.max(rest, axis=-1, keepdims=True)
    i2 = jnp.min(jnp.where(rest == m2, lane, LANES), axis=-1, keepdims=True)
    e2 = jnp.exp(m2 - m1)
    return jnp.where(lane == i1, 1.0 / (1.0 + e2), 0.0) + jnp.where(lane == i2, e2 / (1.0 + e2), 0.0)


def _merge_kernel(with_router, *refs):
    (x_ref, hm_ref, yn_ref, bonus_ref, g_ref, yssd_ref, yfox_ref, wgate_ref, gb_ref, lnw_ref, lnb_ref,
     prw_ref, pssd_ref, pfox_ref, wout_ref, gm_ref, gain_ref, sc_ref, sh_ref) = refs[:19]
    D = D_MODEL
    hm = hm_ref[...]
    y_rw = (yn_ref[...] * lnw_ref[...] + lnb_ref[...] + bonus_ref[...]) * g_ref[...]
    merged = None
    for b, (y_b, proj_ref) in enumerate(((y_rw, prw_ref), (yssd_ref[...], pssd_ref), (yfox_ref[...], pfox_ref))):
        cols = slice(b * D, (b + 1) * D)
        gate = _sigmoid(_dot(hm, wgate_ref[:, cols]) + gb_ref[:, cols])
        term = gate * _dot_bf16(y_b, proj_ref[...])
        merged = term if merged is None else merged + term
    x_new = x_ref[...] + gm_ref[...] * _dot_bf16(merged, wout_ref[...])
    hf = _norm_mod(x_new, gain_ref[...], sc_ref[...], sh_ref[...])
    if with_router:
        wr_ref, o_ref, h_ref, comb_ref = refs[19:]
        comb_ref[...] = _top2_combine(hf, wr_ref[...])
    else:
        o_ref, h_ref = refs[19:]
    o_ref[...] = x_new
    h_ref[...] = hf.astype(h_ref.dtype)


def merge(x, hm, yn, bonus, g, y_ssd, y_fox, p, router=None, tm=256):
    L, D = x.shape
    full = lambda a: pl.BlockSpec(a.shape, lambda i: (0,) * a.ndim)
    rows = lambda n: pl.BlockSpec((tm, n), lambda i: (i, 0))
    small = [p["w_gate"], p["gate_b"], p["lnx_w"], p["lnx_b"], p["proj_rw"], p["proj_ssd"], p["proj_fox"],
             p["w_out"], p["g_m"], p["norm_gain"], p["sc_f"], p["sh_f"]]
    out_specs = [rows(D), rows(D)]
    out_shape = [jax.ShapeDtypeStruct((L, D), F32), jax.ShapeDtypeStruct((L, D), BF16)]
    if router is not None:
        small.append(router)
        out_specs.append(rows(LANES))
        out_shape.append(jax.ShapeDtypeStruct((L, LANES), F32))
    return pl.pallas_call(
        functools.partial(_merge_kernel, router is not None),
        grid=(L // tm,),
        in_specs=[rows(D), rows(D), rows(RW_WIDTH), rows(RW_WIDTH), rows(RW_WIDTH), rows(SSD_WIDTH),
                  rows(FOX_WIDTH)] + [full(a) for a in small],
        out_specs=out_specs,
        out_shape=out_shape,
        compiler_params=_cparams("arbitrary"),
    )(x, hm, yn, bonus, g, y_ssd, y_fox, *small)


def _ffn_kernel(with_next, *refs):
    h_ref, x_ref, wg_ref, wu_ref, wd_ref, gf_ref = refs[:6]
    j = pl.program_id(1)
    acc_scr = refs[-1]

    @pl.when(j == 0)
    def _():
        acc_scr[...] = jnp.zeros_like(acc_scr)

    h = h_ref[...]
    gte = _dot(h, wg_ref[...])
    up = _dot(h, wu_ref[...])
    act = gte * _sigmoid(gte) * up
    acc_scr[...] += _dot(act.astype(BF16), wd_ref[...])

    @pl.when(j == pl.num_programs(1) - 1)
    def _():
        x_new = x_ref[...] + gf_ref[...] * acc_scr[...]
        if with_next:
            gain_ref, sc_ref, sh_ref, o_ref, hn_ref = refs[6:11]
            hn_ref[...] = _norm_mod(x_new, gain_ref[...], sc_ref[...], sh_ref[...]).astype(hn_ref.dtype)
        else:
            o_ref = refs[6]
        o_ref[...] = x_new


def ffn_dense(h, x, w_gu, w_down, g_f, next_norm=None, tm=512, tf=1408):
    L, D = x.shape
    Fh = w_down.shape[0]
    nf = Fh // tf
    row = pl.BlockSpec((1, D), lambda i, j: (0, 0))
    tile = pl.BlockSpec((tm, D), lambda i, j: (i, 0))
    args = [h, x, w_gu, w_gu, w_down, g_f]
    in_specs = [tile, tile,
                pl.BlockSpec((D, tf), lambda i, j: (0, j)),
                pl.BlockSpec((D, tf), lambda i, j: (0, j + nf)),
                pl.BlockSpec((tf, D), lambda i, j: (j, 0)), row]
    out_specs = [tile]
    out_shape = [jax.ShapeDtypeStruct((L, D), F32)]
    if next_norm is not None:
        args += list(next_norm)
        in_specs += [row, row, row]
        out_specs.append(tile)
        out_shape.append(jax.ShapeDtypeStruct((L, D), BF16))
    out = pl.pallas_call(
        functools.partial(_ffn_kernel, next_norm is not None),
        grid=(L // tm, nf),
        in_specs=in_specs,
        out_specs=out_specs,
        out_shape=out_shape,
        scratch_shapes=[pltpu.VMEM((tm, D), F32)],
        compiler_params=_cparams("arbitrary", "arbitrary"),
    )(*args)
    return out if next_norm is not None else (out[0], None)


def _moe_slots(slot_scr, comb_ref):
    slotval = slot_scr[...]
    comb = comb_ref[...]
    s1 = jnp.max(slotval, axis=-1, keepdims=True)
    first = slotval == s1
    c1 = jnp.sum(jnp.where(first, comb, 0.0), axis=-1, keepdims=True)
    rest = jnp.where(first, -1.0, slotval)
    s2 = jnp.max(rest, axis=-1, keepdims=True)
    c2 = jnp.sum(jnp.where(jnp.logical_and(rest == s2, rest >= 0.0), comb, 0.0), axis=-1, keepdims=True)
    return s1, s2, c1, c2


def _moe_kernel(nsb_ref, off_ref, h_ref, x_ref, comb_ref, wg_ref, wu_ref, wd_ref, gf_ref, o_ref,
                slot_scr, cslot_scr, xy_scr, acc_scr):
    i = pl.program_id(0)
    e = pl.program_id(1)
    j = pl.program_id(2)
    last_j = pl.num_programs(2) - 1
    T = h_ref.shape[0]
    S = xy_scr.shape[0]
    SB = MOE_SB

    @pl.when(jnp.logical_and(e == 0, j == 0))
    def _():
        lane = lax.broadcasted_iota(jnp.int32, (LANES, LANES), 1)
        off = jnp.zeros((LANES, LANES), F32)
        for ex in range(N_EXPERTS):
            off = jnp.where(lane == ex, off_ref[i, ex].astype(F32), off)
        before = _tril(LANES, strict=True).astype(BF16)
        ones = jnp.ones((LANES, LANES), BF16)
        for rb in range(T // LANES):
            rows = slice(rb * LANES, (rb + 1) * LANES)
            sel = comb_ref[rows, :] > 0.0
            sel_b = jnp.where(sel, 1.0, 0.0).astype(BF16)
            slot_scr[rows, :] = jnp.where(sel, _dot(before, sel_b) + off, -1.0)
            off = off + _dot(ones, sel_b)
        s1, s2, c1, c2 = _moe_slots(slot_scr, comb_ref)
        used = off_ref[i, N_EXPERTS - 1] + nsb_ref[i, N_EXPERTS - 1] * SB
        for cb in range(S // MOE_CB):
            chunk = slice(cb * MOE_CB, (cb + 1) * MOE_CB)

            @pl.when(used > cb * MOE_CB)
            def _():
                slot = (lax.broadcasted_iota(jnp.int32, (T, MOE_CB), 1) + cb * MOE_CB).astype(F32)
                w = jnp.where(s1 == slot, c1, 0.0) + jnp.where(s2 == slot, c2, 0.0)
                pt = jnp.where(w > 0.0, 1.0, 0.0).astype(BF16)
                xy_scr[chunk, :] = _dot_tn(pt, h_ref[...]).astype(BF16)
                w_hi = w.astype(BF16)
                w_lo = (w - w_hi.astype(F32)).astype(BF16)
                ones_t = jnp.ones((T, LANES), BF16)
                cslot_scr[chunk, :] = _dot_tn(w_hi, ones_t) + _dot_tn(w_lo, ones_t)

            @pl.when(used <= cb * MOE_CB)
            def _():
                xy_scr[chunk, :] = jnp.zeros((MOE_CB, xy_scr.shape[1]), BF16)
                cslot_scr[chunk, :] = jnp.zeros((MOE_CB, LANES), F32)

    base = off_ref[i, e]
    nb = nsb_ref[i, e]

    def ffn_block(r0, n_rows):
        rows = pl.ds(pl.multiple_of(base + r0, SB), n_rows)
        arows = pl.ds(pl.multiple_of(r0, SB), n_rows)
        xb = xy_scr[rows, :]
        gte = _dot(xb, wg_ref[0])
        up = _dot(xb, wu_ref[0])
        part = _dot((gte * _sigmoid(gte) * up).astype(BF16), wd_ref[0])

        @pl.when(j == 0)
        def _():
            acc_scr[arows, :] = part

        @pl.when(jnp.logical_and(j > 0, j < last_j))
        def _():
            acc_scr[arows, :] += part

        @pl.when(j == last_j)
        def _():
            xy_scr[rows, :] = ((acc_scr[arows, :] + part) * cslot_scr[rows, 0:1]).astype(BF16)

    def quad_body(p, carry):
        ffn_block(p * (4 * SB), 4 * SB)
        return carry

    lax.fori_loop(0, nb // 4, quad_body, 0)
    for rest in range(1, 4):
        @pl.when(nb % 4 == rest)
        def _():
            ffn_block((nb - rest) * SB, rest * SB)

    @pl.when(jnp.logical_and(e == pl.num_programs(1) - 1, j == last_j))
    def _():
        s1, s2, _, _ = _moe_slots(slot_scr, comb_ref)
        total = jnp.zeros((T, xy_scr.shape[1]), F32)
        for cb in range(S // MOE_CB):
            slot = (lax.broadcasted_iota(jnp.int32, (T, MOE_CB), 1) + cb * MOE_CB).astype(F32)
            pt = jnp.where(jnp.logical_or(s1 == slot, s2 == slot), 1.0, 0.0).astype(BF16)
            total = total + _dot(pt, xy_scr[cb * MOE_CB:(cb + 1) * MOE_CB, :])
        o_ref[...] = x_ref[...] + gf_ref[...] * total


def ffn_moe(h, x, comb, w_gu, w_down, g_f, tm=1024, tf=896):
    L, D = x.shape
    E, Fh, _ = w_down.shape
    nf = Fh // tf
    assert nf >= 2 and E == N_EXPERTS
    nt = L // tm
    counts = jnp.sum((comb[:, :E] > 0.0).reshape(nt, tm, E), axis=1, dtype=jnp.int32)
    nsb = (counts + (MOE_SB - 1)) // MOE_SB
    off = (jnp.cumsum(nsb, axis=1) - nsb) * MOE_SB
    idx = lambda f: (lambda i, e, j, n, o: f(i, e, j))
    grid_spec = pltpu.PrefetchScalarGridSpec(
        num_scalar_prefetch=2,
        grid=(nt, E, nf),
        in_specs=[pl.BlockSpec((tm, D), idx(lambda i, e, j: (i, 0))),
                  pl.BlockSpec((tm, D), idx(lambda i, e, j: (i, 0))),
                  pl.BlockSpec((tm, LANES), idx(lambda i, e, j: (i, 0))),
                  pl.BlockSpec((1, D, tf), idx(lambda i, e, j: (e, 0, j))),
                  pl.BlockSpec((1, D, tf), idx(lambda i, e, j: (e, 0, j + nf))),
                  pl.BlockSpec((1, tf, D), idx(lambda i, e, j: (e, j, 0))),
                  pl.BlockSpec((1, D), idx(lambda i, e, j: (0, 0)))],
        out_specs=pl.BlockSpec((tm, D), idx(lambda i, e, j: (i, 0))),
        scratch_shapes=[pltpu.VMEM((tm, LANES), F32),
                        pltpu.VMEM((TOP_K * tm + E * MOE_SB, LANES), F32),
                        pltpu.VMEM((TOP_K * tm + E * MOE_SB, D), BF16),
                        pltpu.VMEM((tm, D), F32)],
    )
    return pl.pallas_call(
        _moe_kernel,
        grid_spec=grid_spec,
        out_shape=jax.ShapeDtypeStruct((L, D), F32),
        compiler_params=_cparams("arbitrary", "arbitrary", "arbitrary"),
    )(nsb, off, h, x, comb, w_gu, w_gu, w_down, g_f)


def _seg_matrix(width, head_dim):
    idx = jnp.arange(width) // head_dim
    return (idx[:, None] == idx[None, :]).astype(BF16)


def _pad_cols(w, n):
    return jnp.pad(w, ((0, 0), (0, n - w.shape[1])))


def _pad_rows(w, n):
    return jnp.pad(w, ((0, n - w.shape[0]), (0, 0)))


def rwkv_branch(h, w_rw, p, v_first):
    r, k, v, lw, a, b, g, bonus = rw_pre(h, w_rw, p, v_first)
    return wkv7(r, lw, k, v, a, b), bonus, g, v


def fox_branch(h, w_fox, p):
    q, k, v, F = fox_pre(h, w_fox, p)
    FT = jnp.transpose(F[:, :FOX_HEADS])
    lo = fox_first_block(FT, p["bound2"])
    return fox_attention(q, k, v, FT, lo)


def kernel(x, c, ada_w, ada_b, norm_mix, norm_ffn, w_in, rw_mu, rw_w0, rw_w_up, rw_a0, rw_a_up, rw_g_up, rw_k_k, rw_k_a, rw_r_k, rw_lnx_w, rw_lnx_b, rw_v0, rw_v_down, rw_v_up, ssd_conv_w, ssd_conv_b, ssd_dt_bias, ssd_a_log, ssd_d, ssd_norm, fox_f_bias, fox_q_gain, fox_k_gain, gate_b, proj_rw, proj_ssd, proj_fox, w_out, ffn_w_gu, ffn_w_down, moe_router, moe_w_gu, moe_w_down):
    depth = w_in.shape[0]
    D = D_MODEL
    xs = x[0]
    row = lambda t: t.reshape(1, -1).astype(F32)
    seg64 = _seg_matrix(RW_WIDTH, RW_HEAD_DIM)
    rw_cols = 3 * RW_WIDTH + RW_DECAY_LORA + RW_AAA_LORA + RW_GATE_LORA
    ssd_cols = SSD_WIDTH + SSD_XBC + SSD_HEADS
    fox_cols = 3 * FOX_WIDTH + FOX_HEADS
    expand = (jnp.arange(LANES)[:, None] == (jnp.arange(SSD_WIDTH) // SSD_HEAD_DIM)[None, :]).astype(BF16)
    v_first = None
    mods = []
    for l in range(depth):
        mod = adaln_mod(c, ada_w[l], ada_b[l])
        mods.append([mod[:, i * D:(i + 1) * D] for i in range(6)])
    h = None
    for l in range(depth):
        sh_m, sc_m, g_m, sh_f, sc_f, g_f = mods[l]
        if h is None:
            h = norm_mod(xs, row(norm_mix[l]), sc_m, sh_m)

        wl = w_in[l]
        o = 0
        w_r = wl[:, o:o + rw_cols]; o += rw_cols
        w_s = wl[:, o:o + ssd_cols]; o += ssd_cols
        w_f = wl[:, o:o + fox_cols]; o += fox_cols
        w_g = wl[:, o:]
        W3 = 3 * RW_WIDTH
        o_a = W3 + RW_DECAY_LORA
        o_g = o_a + RW_AAA_LORA

        def rw_layout(t, hv):
            parts = [t[:, :W3], _pad_cols(t[:, W3:o_a], LANES), _pad_cols(t[:, o_a:o_g], LANES),
                     _pad_cols(t[:, o_g:], 2 * LANES), _pad_cols(hv, LANES)]
            return jnp.concatenate(parts, axis=1)

        if l == 0:
            hv_w = jnp.zeros((D, RW_VRES_LORA), F32)
        else:
            hv_w = rw_v_down[l - 1]
        w_rw = rw_layout(w_r, hv_w).astype(BF16)
        mu = rw_layout(rw_mu[l].reshape(1, -1), jnp.zeros((1, RW_VRES_LORA), F32))
        rw_p = dict(mu=mu, w0=row(rw_w0[l]), w_up=_pad_rows(rw_w_up[l], LANES).astype(BF16), a0=row(rw_a0[l]),
                    a_up=_pad_rows(rw_a_up[l], LANES).astype(BF16),
                    g_up=_pad_rows(rw_g_up[l], 2 * LANES).astype(BF16),
                    k_k=row(rw_k_k[l]), k_a=row(rw_k_a[l]), r_k=row(rw_r_k[l]), seg=seg64)
        if l > 0:
            rw_p.update(v0=row(rw_v0[l - 1]), v_up=_pad_rows(rw_v_up[l - 1], LANES).astype(BF16))
        yn, bonus, g_rw, v_cur = rwkv_branch(h, w_rw, rw_p, v_first)
        if l == 0:
            v_first = v_cur

        w_ssd = jnp.concatenate([w_s[:, SSD_WIDTH:SSD_WIDTH + SSD_XBC], w_s[:, :SSD_WIDTH],
                                 _pad_cols(w_s[:, SSD_WIDTH + SSD_XBC:], LANES)], axis=1).astype(BF16)
        A = -jnp.exp(ssd_a_log[l].astype(F32))
        ssd_p = dict(conv_w=ssd_conv_w[l], conv_b=row(ssd_conv_b[l]),
                     dt_bias=_pad_cols(row(ssd_dt_bias[l]), LANES),
                     A=_pad_cols(row(A), LANES), expand=expand,
                     d_skip=row(jnp.repeat(ssd_d[l], SSD_HEAD_DIM)), norm_w=row(ssd_norm[l]))
        y_ssd = ssd_mixer(h, w_ssd, ssd_p)

        w_fox = jnp.concatenate([w_f[:, :3 * FOX_WIDTH], _pad_cols(w_f[:, 3 * FOX_WIDTH:], LANES)],
                                axis=1).astype(BF16)
        bound2 = (1.02 * FOX_HEAD_DIM ** 0.5 * LOG2E) * jnp.max(jnp.abs(fox_q_gain[l])) * jnp.max(jnp.abs(fox_k_gain[l]))
        fox_p = dict(f_bias=_pad_cols(row(fox_f_bias[l]), LANES),
                     q_gain=row(jnp.tile(fox_q_gain[l], FOX_HEADS)) * (FOX_HEAD_DIM ** -0.5 * LOG2E),
                     k_gain=row(jnp.tile(fox_k_gain[l], FOX_HEADS)), seg=seg64, bound2=bound2)
        y_fox = fox_branch(h, w_fox, fox_p)

        mp = dict(w_gate=w_g.astype(BF16), gate_b=row(gate_b[l]), lnx_w=row(rw_lnx_w[l]), lnx_b=row(rw_lnx_b[l]),
                  proj_rw=proj_rw[l].astype(BF16), proj_ssd=proj_ssd[l].astype(BF16),
                  proj_fox=proj_fox[l].astype(BF16), w_out=w_out[l].astype(BF16), g_m=g_m,
                  norm_gain=row(norm_ffn[l]), sc_f=sc_f, sh_f=sh_f)

        if l % 2 == 0:
            xs, hf = merge(xs, h, yn, bonus, g_rw, y_ssd, y_fox, mp)
            next_norm = None
            if l + 1 < depth:
                next_norm = (row(norm_mix[l + 1]), mods[l + 1][1], mods[l + 1][0])
            xs, h = ffn_dense(hf, xs, ffn_w_gu[l // 2].astype(BF16), ffn_w_down[l // 2].astype(BF16), g_f,
                              next_norm)
        else:
            xs, hf, comb = merge(xs, h, yn, bonus, g_rw, y_ssd, y_fox, mp,
                                 router=_pad_cols(moe_router[l // 2], LANES))
            xs = ffn_moe(hf, xs, comb, moe_w_gu[l // 2].astype(BF16), moe_w_down[l // 2].astype(BF16), g_f)
            h = None
    return xs[None]
```

```python
import functools
import math

import jax
import jax.numpy as jnp
from jax import lax
from jax.experimental import pallas as pl
from jax.experimental.pallas import tpu as pltpu

F32 = jnp.float32
BF16 = jnp.bfloat16
HIGHEST = lax.Precision.HIGHEST

D_MODEL = 1024
LANES = 128
SUBLANES = 8

RW_HEADS = 8
RW_HEAD_DIM = 64
RW_WIDTH = RW_HEADS * RW_HEAD_DIM
RW_DECAY_LORA = 64
RW_AAA_LORA = 64
RW_VRES_LORA = 32
RW_GATE_LORA = 160
GN_EPS = 64e-5
RW_OFF_WLO = 3 * RW_WIDTH
RW_OFF_ALO = RW_OFF_WLO + LANES
RW_OFF_GLO = RW_OFF_ALO + LANES
RW_OFF_HV = RW_OFF_GLO + 2 * LANES
RW_PAD_COLS = RW_OFF_HV + LANES
RW_CHUNK = 64

SSD_HEADS = 16
SSD_HEAD_DIM = 64
SSD_WIDTH = SSD_HEADS * SSD_HEAD_DIM
SSD_GROUPS = 4
SSD_STATE = 128
SSD_CONV = 4
SSD_CHUNK = 128
SSD_XBC = SSD_WIDTH + 2 * SSD_GROUPS * SSD_STATE
SSD_GROUP_WIDTH = SSD_WIDTH // SSD_GROUPS

FOX_HEADS = 8
FOX_HEAD_DIM = 64
FOX_WIDTH = FOX_HEADS * FOX_HEAD_DIM
FOX_TQ = 512
FOX_TK = 512
LOG2E = 1.4426950408889634

FFN_DENSE = 2816
N_EXPERTS = 8
TOP_K = 2
FFN_EXPERT = 3584
MOE_SB = 128
MOE_CB = 512
EPS = 1e-6

VMEM_LIMIT = 56 * 1024 * 1024


def _cparams(*sem):
    return pltpu.CompilerParams(dimension_semantics=sem, vmem_limit_bytes=VMEM_LIMIT)


def _sigmoid(x):
    return 1.0 / (1.0 + jnp.exp(-x))


def _softplus(x):
    return jnp.maximum(x, 0.0) + jnp.log(1.0 + jnp.exp(-jnp.abs(x)))


def _dot(a, b):
    return jnp.dot(a, b, preferred_element_type=F32)


def _dot_bf16(a, b):
    return jnp.dot(a.astype(BF16), b.astype(BF16), preferred_element_type=F32)


def _dot_hi(a, b):
    return jnp.dot(a, b, precision=HIGHEST, preferred_element_type=F32)


def _dot_nt(a, b, precision=None):
    return lax.dot_general(a, b, (((1,), (1,)), ((), ())), precision=precision, preferred_element_type=F32)


def _dot_tn(a, b, precision=None):
    return lax.dot_general(a, b, (((0,), (0,)), ((), ())), precision=precision, preferred_element_type=F32)


def _split_dot(x, ones_bf16):
    return _dot(x.astype(BF16), ones_bf16)


def _split3(x):
    hi = x.astype(BF16)
    r1 = x - hi.astype(F32)
    mid = r1.astype(BF16)
    return hi, mid, (r1 - mid.astype(F32)).astype(BF16)


def _tril(n, strict=False):
    r = lax.broadcasted_iota(jnp.int32, (n, n), 0)
    c = lax.broadcasted_iota(jnp.int32, (n, n), 1)
    return (r > c) if strict else (r >= c)


def _mod_kernel(c_ref, w_ref, b_ref, o_ref):
    c = c_ref[...]
    o_ref[...] = _dot_hi(c * _sigmoid(c), w_ref[...]) + b_ref[...]


def adaln_mod(c, w, b):
    d, n = w.shape
    tn = 1024
    c8 = jnp.broadcast_to(c, (SUBLANES, d))
    out = pl.pallas_call(
        _mod_kernel,
        grid=(n // tn,),
        in_specs=[pl.BlockSpec((SUBLANES, d), lambda j: (0, 0)),
                  pl.BlockSpec((d, tn), lambda j: (0, j)),
                  pl.BlockSpec((1, tn), lambda j: (0, j))],
        out_specs=pl.BlockSpec((SUBLANES, tn), lambda j: (0, j)),
        out_shape=jax.ShapeDtypeStruct((SUBLANES, n), F32),
        compiler_params=_cparams("arbitrary"),
    )(c8, w, b.reshape(1, n))
    return out[:1]


def _norm_mod(x, gain, sc, sh):
    y = x * lax.rsqrt(jnp.mean(x * x, axis=-1, keepdims=True) + EPS)
    return y * gain * (1.0 + sc) + sh


def _norm_kernel(x_ref, gain_ref, sc_ref, sh_ref, h_ref):
    h_ref[...] = _norm_mod(x_ref[...], gain_ref[...], sc_ref[...], sh_ref[...]).astype(h_ref.dtype)


def norm_mod(x, gain, sc, sh, tm=512):
    L, d = x.shape
    row = pl.BlockSpec((1, d), lambda i: (0, 0))
    return pl.pallas_call(
        _norm_kernel,
        grid=(L // tm,),
        in_specs=[pl.BlockSpec((tm, d), lambda i: (i, 0)), row, row, row],
        out_specs=pl.BlockSpec((tm, d), lambda i: (i, 0)),
        out_shape=jax.ShapeDtypeStruct((L, d), BF16),
        compiler_params=_cparams("arbitrary"),
    )(x, gain, sc, sh)


def _rw_pre_kernel(has_vres, *refs):
    if has_vres:
        (h_ref, w_ref, mu_ref, w0_ref, wup_ref, a0_ref, aup_ref, gup_ref, kk_ref, ka_ref, rk_ref, seg_ref,
         vfirst_ref, v0_ref, vup_ref,
         r_out, k_out, v_out, lw_out, a_out, b_out, g_out, bonus_out, ext_scr) = refs
    else:
        (h_ref, w_ref, mu_ref, w0_ref, wup_ref, a0_ref, aup_ref, gup_ref, kk_ref, ka_ref, rk_ref, seg_ref,
         r_out, k_out, v_out, lw_out, a_out, b_out, g_out, bonus_out, ext_scr) = refs
    T = h_ref.shape[0]

    @pl.when(pl.program_id(0) == 0)
    def _():
        ext_scr[0:SUBLANES, :] = jnp.zeros((SUBLANES, ext_scr.shape[1]), F32)

    cur = _dot(h_ref[...], w_ref[...])
    ext_scr[SUBLANES:SUBLANES + T, :] = cur
    prev = ext_scr[SUBLANES - 1:SUBLANES - 1 + T, :]
    ext_scr[0:SUBLANES, :] = cur[T - SUBLANES:T, :]
    s = cur + (prev - cur) * mu_ref[...]

    W = RW_WIDTH
    r = s[:, 0:W]
    k = s[:, W:2 * W]
    v = s[:, 2 * W:3 * W]
    w_lo = s[:, RW_OFF_WLO:RW_OFF_WLO + LANES]
    a_lo = s[:, RW_OFF_ALO:RW_OFF_ALO + LANES]
    g_lo = s[:, RW_OFF_GLO:RW_OFF_GLO + 2 * LANES]
    seg = seg_ref[...]

    wlog = -_softplus(-(w0_ref[...] + _dot_bf16(jnp.tanh(w_lo), wup_ref[...]))) - 0.5
    a = _sigmoid(a0_ref[...] + _dot_bf16(a_lo, aup_ref[...]))
    g = _dot_bf16(_sigmoid(g_lo), gup_ref[...])
    if has_vres:
        hv = s[:, RW_OFF_HV:RW_OFF_HV + LANES]
        v = v + (vfirst_ref[...].astype(F32) - v) * _sigmoid(v0_ref[...] + _dot_bf16(hv, vup_ref[...]))
    kk = k * kk_ref[...]
    kk = kk / jnp.maximum(jnp.sqrt(_split_dot(kk * kk, seg)), 1e-12)
    k = k * (1.0 + (a - 1.0) * ka_ref[...])
    r_out[...] = r.astype(r_out.dtype)
    k_out[...] = k.astype(k_out.dtype)
    v_out[...] = v.astype(v_out.dtype)
    lw_out[...] = -jnp.exp(wlog)
    a_out[...] = (-kk).astype(a_out.dtype)
    b_out[...] = (kk * a).astype(b_out.dtype)
    g_out[...] = g.astype(g_out.dtype)
    bonus_out[...] = (_split_dot(r * k * rk_ref[...], seg) * v).astype(bonus_out.dtype)


def rw_pre(h, w_rw, p, v_first, tm=512):
    L, D = h.shape
    has_vres = v_first is not None
    full = lambda a: pl.BlockSpec(a.shape, lambda i: (0,) * a.ndim)
    rows = lambda n: pl.BlockSpec((tm, n), lambda i: (i, 0))
    args = [h, w_rw, p["mu"], p["w0"], p["w_up"], p["a0"], p["a_up"], p["g_up"], p["k_k"], p["k_a"], p["r_k"],
            p["seg"]]
    specs = [rows(D)] + [full(a) for a in args[1:]]
    if has_vres:
        args += [v_first, p["v0"], p["v_up"]]
        specs += [rows(RW_WIDTH), full(p["v0"]), full(p["v_up"])]
    out_dtypes = [BF16, BF16, BF16, F32, BF16, BF16, BF16, BF16]
    return pl.pallas_call(
        functools.partial(_rw_pre_kernel, has_vres),
        grid=(L // tm,),
        in_specs=specs,
        out_specs=[rows(RW_WIDTH)] * 8,
        out_shape=[jax.ShapeDtypeStruct((L, RW_WIDTH), dt) for dt in out_dtypes],
        scratch_shapes=[pltpu.VMEM((tm + SUBLANES, RW_PAD_COLS), F32)],
        compiler_params=_cparams("arbitrary"),
    )(*args)


def _wkv_kernel(r_ref, lw_ref, k_ref, v_ref, a_ref, b_ref, y_ref, h_scr):
    C = RW_CHUNK
    C2 = 2 * C
    n_chunks = r_ref.shape[0] // C

    @pl.when(pl.program_id(1) == 0)
    def _():
        h_scr[...] = jnp.zeros_like(h_scr)

    is_a = lax.broadcasted_iota(jnp.int32, (C, LANES), 1) < RW_HEAD_DIM
    ri = lax.broadcasted_iota(jnp.int32, (C2, C2), 0)
    ci = lax.broadcasted_iota(jnp.int32, (C2, C2), 1)
    same = (ri >= C) == (ci >= C)
    strict = jnp.logical_and(same, ri > ci)
    incl = jnp.logical_and(same, ri >= ci)
    eye2 = (ri == ci).astype(F32)
    tri_c = _tril(C).astype(BF16)

    def two(x):
        return jnp.concatenate([jnp.where(is_a, x, 0.0), jnp.where(is_a, 0.0, x)], axis=0)

    chunks = range(n_chunks)
    at2, rt2, bh2, kh2, v2, n_ab, a_ak, m_rb, m_rk, p_end = ([] for _ in range(10))
    for c in chunks:
        sl = pl.ds(c * C, C)
        lw = lw_ref[sl, :]
        lw_hi = lw.astype(BF16)
        lw_lo = (lw - lw_hi.astype(F32)).astype(BF16)
        cs = _dot(tri_c, lw_hi) + _dot(tri_c, lw_lo)
        cs_end = cs[C - 1:C, :]
        e_neg = jnp.exp(-cs)
        e_end = jnp.exp(cs_end - cs)
        a = a_ref[sl, :].astype(F32)
        b = b_ref[sl, :].astype(F32)
        k = k_ref[sl, :].astype(F32)
        at2.append(two(a * jnp.exp(cs - lw)))
        rt2.append(two(r_ref[sl, :].astype(F32) * jnp.exp(cs)))
        bh2.append(two(b * e_end))
        kh2.append(two(k * e_end))
        v2.append(two(v_ref[sl, :].astype(F32)))
        p_end.append(jnp.exp(cs_end))
        quad = _dot_nt(jnp.concatenate([at2[c], rt2[c]], axis=0).astype(BF16),
                       jnp.concatenate([two(b * e_neg), two(k * e_neg)], axis=0).astype(BF16))
        n_ab.append(jnp.where(strict, quad[0:C2, 0:C2], 0.0))
        a_ak.append(jnp.where(strict, quad[0:C2, C2:], 0.0))
        m_rb.append(jnp.where(incl, quad[C2:, 0:C2], 0.0))
        m_rk.append(jnp.where(incl, quad[C2:, C2:], 0.0))
    t_inv = [eye2 + n for n in n_ab]
    n_pow = n_ab
    n_pow = [_dot_bf16(n, n) for n in n_pow]
    for _ in range(int(math.log2(C)) - 2):
        both = [_dot_bf16(jnp.concatenate([n, t], axis=0), n) for n, t in zip(n_pow, t_inv)]
        t_inv = [t + b[C2:, :] for t, b in zip(t_inv, both)]
        n_pow = [b[:C2, :] for b in both]
    t_inv = [t + _dot_bf16(t, n) for t, n in zip(t_inv, n_pow)]
    akv = [_dot_bf16(a_ak[c], v2[c]) for c in chunks]
    au = [_dot_bf16(t_inv[c], jnp.concatenate([at2[c], akv[c]], axis=1)) for c in chunks]
    mau = [_dot_bf16(m_rb[c], au[c]) for c in chunks]
    mkv = [_dot_bf16(m_rk[c], v2[c]) for c in chunks]
    gu = [_dot_tn(bh2[c].astype(BF16), au[c].astype(BF16)) for c in chunks]
    kv = [_dot_tn(kh2[c].astype(BF16), v2[c].astype(BF16)) for c in chunks]
    inv_n = 1.0 / RW_HEAD_DIM
    h = h_scr[...]
    for c in chunks:
        rh2 = rt2[c] + mau[c][:, :LANES]
        g_mat = eye2 * p_end[c] + gu[c][:, :LANES]
        yh = _dot_bf16(jnp.concatenate([rh2, g_mat], axis=0), h)
        h = yh[C2:, :] + gu[c][:, LANES:] + kv[c]
        y2 = yh[:C2, :] + mau[c][:, LANES:] + mkv[c]
        y = y2[:C, :] + y2[C:, :]
        s_all = jnp.sum(y, axis=-1, keepdims=True)
        s_a = jnp.sum(jnp.where(is_a, y, 0.0), axis=-1, keepdims=True)
        yc = y - jnp.where(is_a, s_a, s_all - s_a) * inv_n
        sq = yc * yc
        q_all = jnp.sum(sq, axis=-1, keepdims=True)
        q_a = jnp.sum(jnp.where(is_a, sq, 0.0), axis=-1, keepdims=True)
        yn = yc * lax.rsqrt(jnp.where(is_a, q_a, q_all - q_a) * inv_n + GN_EPS)
        y_ref[pl.ds(c * C, C), :] = yn.astype(y_ref.dtype)
    h_scr[...] = h


def wkv7(r, lw, k, v, a, b, rows=2048):
    L, W = r.shape
    spec = pl.BlockSpec((rows, LANES), lambda p, i: (i, p))
    return pl.pallas_call(
        _wkv_kernel,
        grid=(W // LANES, L // rows),
        in_specs=[spec] * 6,
        out_specs=spec,
        out_shape=jax.ShapeDtypeStruct((L, W), BF16),
        scratch_shapes=[pltpu.VMEM((LANES, LANES), F32)],
        compiler_params=_cparams("arbitrary", "arbitrary"),
    )(r, lw, k, v, a, b)


def _ssd_kernel(h_ref, w_ref, cw_ref, cb_ref, bias_ref, A_ref, exp_ref, dskip_ref, nw_ref, o_ref,
                ext_scr, xbc_scr, s_scr):
    Q = SSD_CHUNK
    GW = SSD_GROUP_WIDTH
    NS = SSD_STATE
    R = h_ref.shape[0]

    @pl.when(pl.program_id(0) == 0)
    def _():
        s_scr[...] = jnp.zeros_like(s_scr)
        ext_scr[0:SUBLANES, :] = jnp.zeros((SUBLANES, ext_scr.shape[1]), F32)

    cols = _dot(h_ref[...], w_ref[...])
    ext_scr[SUBLANES:SUBLANES + R, :] = cols[:, 0:SSD_XBC]
    acc = cols[:, 0:SSD_XBC] * cw_ref[SSD_CONV - 1:SSD_CONV, :] + cb_ref[...]
    for j in range(1, SSD_CONV):
        acc = acc + ext_scr[SUBLANES - j:SUBLANES - j + R, :] * cw_ref[SSD_CONV - 1 - j:SSD_CONV - j, :]
    ext_scr[0:SUBLANES, :] = ext_scr[R:R + SUBLANES, :]
    xbc_scr[...] = acc * _sigmoid(acc)

    incl = _tril(Q)
    tri = incl.astype(BF16)
    triu = jnp.logical_not(_tril(Q, strict=True)).astype(BF16)
    expand = exp_ref[...]
    first_half = lax.broadcasted_iota(jnp.int32, (Q, LANES), 1) < SSD_HEAD_DIM
    heads_per_group = SSD_HEADS // SSD_GROUPS
    for c in range(R // Q):
        rows = slice(c * Q, (c + 1) * Q)
        xs = xbc_scr[rows, 0:SSD_WIDTH]
        z = cols[rows, SSD_XBC:SSD_XBC + SSD_WIDTH]
        dt = _softplus(cols[rows, SSD_XBC + SSD_WIDTH:] + bias_ref[...])
        a3 = _split3(dt * A_ref[...])
        a_cum = sum(_dot(tri, t) for t in a3)
        a_cumT = sum(_dot_tn(t, triu) for t in a3)
        a_end = a_cum[Q - 1:Q, :]
        dt_x = _dot(dt.astype(BF16), expand)
        eac_x = _dot(jnp.exp(a_cum).astype(BF16), expand)
        dte_x = _dot(jnp.exp(a_end - a_cum).astype(BF16), expand)
        cd_x = sum(_dot(t, expand) for t in _split3(jnp.broadcast_to(jnp.exp(a_end), (SUBLANES, LANES))))[0:1, :]
        xdt = xs * dt_x
        for g in range(SSD_GROUPS):
            Bg = xbc_scr[rows, SSD_WIDTH + g * NS:SSD_WIDTH + (g + 1) * NS].astype(BF16)
            Cg = xbc_scr[rows, SSD_WIDTH + (SSD_GROUPS + g) * NS:SSD_WIDTH + (SSD_GROUPS + g + 1) * NS].astype(BF16)
            cb = _dot_nt(Cg, Bg)
            gs = slice(g * GW, (g + 1) * GW)
            s_prev = s_scr[:, gs]
            y_g = _dot_bf16(Cg, s_prev) * eac_x[:, gs]
            pieces = []
            for pr in range(heads_per_group // 2):
                ps = slice(g * GW + pr * LANES, g * GW + (pr + 1) * LANES)
                xdt_p = xdt[:, ps]
                acc_p = None
                for e in range(2):
                    hd = g * heads_per_group + pr * 2 + e
                    seg = a_cum[:, hd:hd + 1] - a_cumT[hd:hd + 1, :]
                    m = cb * jnp.exp(jnp.where(incl, seg, -jnp.inf))
                    xm = jnp.where(first_half if e == 0 else jnp.logical_not(first_half), xdt_p, 0.0)
                    t = _dot_bf16(m, xm)
                    acc_p = t if acc_p is None else acc_p + t
                pieces.append(acc_p)
            y_g = y_g + jnp.concatenate(pieces, axis=1)
            s_scr[:, gs] = s_prev * cd_x[:, gs] +_dot_tn(Bg, (dte_x[:, gs] * xdt[:, gs]).astype(BF16))
            y_g = y_g + dskip_ref[:, gs] * xs[:, gs]
            zg = z[:, gs]
            y_g = y_g * (zg * _sigmoid(zg))
            y_g = y_g * lax.rsqrt(jnp.mean(y_g * y_g, axis=-1, keepdims=True) + EPS)
            o_ref[rows, gs] = (y_g * nw_ref[:, gs]).astype(o_ref.dtype)


def ssd_mixer(h, w_ssd, p, rows=512):
    L, D = h.shape
    N = w_ssd.shape[1]
    full = lambda a: pl.BlockSpec(a.shape, lambda i: (0,) * a.ndim)
    small = [p["conv_w"], p["conv_b"], p["dt_bias"], p["A"], p["expand"], p["d_skip"], p["norm_w"]]
    return pl.pallas_call(
        _ssd_kernel,
        grid=(L // rows,),
        in_specs=[pl.BlockSpec((rows, D), lambda i: (i, 0)), full(w_ssd)] + [full(a) for a in small],
        out_specs=pl.BlockSpec((rows, SSD_WIDTH), lambda i: (i, 0)),
        out_shape=jax.ShapeDtypeStruct((L, SSD_WIDTH), BF16),
        scratch_shapes=[pltpu.VMEM((rows + SUBLANES, SSD_XBC), F32), pltpu.VMEM((rows, SSD_XBC), F32),
                        pltpu.VMEM((SSD_STATE, SSD_WIDTH), F32)],
        compiler_params=_cparams("arbitrary"),
    )(h, w_ssd, *small)


def _fox_pre_kernel(h_ref, w_ref, fb_ref, qg_ref, kg_ref, seg_ref, q_out, k_out, v_out, f_out, carry_scr):
    T = h_ref.shape[0]
    W = FOX_WIDTH

    @pl.when(pl.program_id(0) == 0)
    def _():
        carry_scr[...] = jnp.zeros_like(carry_scr)

    cols = _dot(h_ref[...], w_ref[...])
    seg = seg_ref[...]
    q = cols[:, 0:W]
    k = cols[:, W:2 * W]
    inv_d = 1.0 / FOX_HEAD_DIM
    qn = q * lax.rsqrt(_split_dot(q * q, seg) * inv_d + EPS) * qg_ref[...]
    kn = k * lax.rsqrt(_split_dot(k * k, seg) * inv_d + EPS) * kg_ref[...]
    q_out[...] = qn.astype(q_out.dtype)
    k_out[...] = kn.astype(k_out.dtype)
    v_out[...] = cols[:, 2 * W:3 * W].astype(v_out.dtype)
    f = cols[:, 3 * W:3 * W + LANES] + fb_ref[...]
    ls = -_softplus(-f)
    cum = _dot_hi(_tril(T).astype(F32), ls) + carry_scr[0:1, :]
    f_out[...] = cum * LOG2E
    carry_scr[...] = jnp.broadcast_to(cum[T - 1:T, :], carry_scr.shape)


def fox_pre(h, w_fox, p, tm=512):
    L, D = h.shape
    full = lambda a: pl.BlockSpec(a.shape, lambda i: (0,) * a.ndim)
    rows = lambda n: pl.BlockSpec((tm, n), lambda i: (i, 0))
    small = [w_fox, p["f_bias"], p["q_gain"], p["k_gain"], p["seg"]]
    return pl.pallas_call(
        _fox_pre_kernel,
        grid=(L // tm,),
        in_specs=[rows(D)] + [full(a) for a in small],
        out_specs=[rows(FOX_WIDTH)] * 3 + [rows(LANES)],
        out_shape=[jax.ShapeDtypeStruct((L, FOX_WIDTH), BF16)] * 3 + [jax.ShapeDtypeStruct((L, LANES), F32)],
        scratch_shapes=[pltpu.VMEM((SUBLANES, LANES), F32)],
        compiler_params=_cparams("arbitrary"),
    )(h, *small)


def _fox_kernel(lo_ref, q_ref, k_ref, v_ref, fk_ref, o_ref, m_scr, acc_scr, s0_scr, s1_scr):
    tq = q_ref.shape[0]
    tk = FOX_TK
    pair = pl.program_id(0)
    qi = pl.program_id(1)
    is_a = lax.broadcasted_iota(jnp.int32, (tq, LANES), 1) < FOX_HEAD_DIM
    q = q_ref[...]
    zero = jnp.zeros_like(q)
    qs = (jnp.where(is_a, q, zero), jnp.where(is_a, zero, q))
    m_scr[...] = jnp.full(m_scr.shape, -jnp.inf, F32)
    key_is_a = lax.broadcasted_iota(jnp.int32, (tk, LANES), 1) < FOX_HEAD_DIM
    acc_scr[...] = jnp.zeros_like(acc_scr)
    rel = lax.broadcasted_iota(jnp.int32, (tq, tk), 0) - lax.broadcasted_iota(jnp.int32, (tq, tk), 1)
    last = (qi * tq) // tk

    def fill(s_ref, kb, diagonal=False):
        ks = pl.ds(pl.multiple_of(kb * tk, tk), tk)
        k_blk = k_ref[ks, :]
        for e in range(2):
            s = _dot_nt(qs[e], k_blk) - fk_ref[pl.ds(2 * pair + e, 1), ks]
            if diagonal:
                s = jnp.where(rel >= kb * tk - qi * tq, s, -jnp.inf)
            s_ref[e] = s

    def consume(s_ref, kb):
        v_blk = v_ref[pl.ds(pl.multiple_of(kb * tk, tk), tk), :]
        one = jnp.ones_like(v_blk)
        v_one = (jnp.where(key_is_a, v_blk, one), jnp.where(key_is_a, one, v_blk))
        for e in range(2):
            s = s_ref[e]
            m_old = m_scr[e]
            m_new = jnp.maximum(m_old, jnp.max(s, axis=-1, keepdims=True))
            pexp = jnp.exp2(s - m_new)
            acc_scr[e] = jnp.exp2(m_old - m_new) * acc_scr[e] + _dot(pexp.astype(BF16), v_one[e])
            m_scr[e] = m_new

    n = last - jnp.minimum(lo_ref[2 * pair, qi], lo_ref[2 * pair + 1, qi])
    fill(s0_scr, last, diagonal=True)

    def two_steps(u, carry):
        kb = last - 2 * u
        fill(s1_scr, kb - 1)
        consume(s0_scr, kb)
        fill(s0_scr, kb - 2)
        consume(s1_scr, kb - 1)
        return carry

    lax.fori_loop(0, n // 2, two_steps, 0)

    @pl.when(n % 2 == 1)
    def _():
        fill(s1_scr, last - n)
        consume(s0_scr, last - n + 1)
        consume(s1_scr, last - n)

    @pl.when(n % 2 == 0)
    def _():
        consume(s0_scr, last - n)

    acc_a = acc_scr[0]
    acc_b = acc_scr[1]
    half = FOX_HEAD_DIM
    o_ref[...] = jnp.where(is_a, acc_a / pltpu.roll(acc_a, half, 1),
                           acc_b / pltpu.roll(acc_b, half, 1)).astype(o_ref.dtype)


def fox_attention(q, k, v, FT, lo):
    L, W = q.shape
    H = FT.shape[0]
    tq = FOX_TQ
    grid_spec = pltpu.PrefetchScalarGridSpec(
        num_scalar_prefetch=1,
        grid=(W // LANES, L // tq),
        in_specs=[pl.BlockSpec((tq, LANES), lambda p, i, lo_r: (i, p)),
                  pl.BlockSpec((L, LANES), lambda p, i, lo_r: (0, p)),
                  pl.BlockSpec((L, LANES), lambda p, i, lo_r: (0, p)),
                  pl.BlockSpec((H, L), lambda p, i, lo_r: (0, 0))],
        out_specs=pl.BlockSpec((tq, LANES), lambda p, i, lo_r: (i, p)),
        scratch_shapes=[pltpu.VMEM((2, tq, 1), F32), pltpu.VMEM((2, tq, LANES), F32),
                        pltpu.VMEM((2, tq, FOX_TK), F32), pltpu.VMEM((2, tq, FOX_TK), F32)],
    )
    return pl.pallas_call(
        _fox_kernel,
        grid_spec=grid_spec,
        out_shape=jax.ShapeDtypeStruct((L, W), BF16),
        compiler_params=_cparams("arbitrary", "arbitrary"),
    )(lo, q, k, v, FT)


def fox_first_block(FT, bound2):
    f_first = FT[:, ::FOX_TQ]
    f_last = FT[:, FOX_TK - 1::FOX_TK]
    gap = f_first[:, :, None] - f_last[:, None, :] + 2.0 * bound2
    last = (jnp.arange(f_first.shape[1]) * FOX_TQ) // FOX_TK
    needed = jnp.logical_or(gap >= -152.0, jnp.arange(f_last.shape[1])[None, None, :] >= last[None, :, None])
    return jnp.sum(jnp.cumsum(needed.astype(jnp.int32), axis=-1) == 0, axis=-1).astype(jnp.int32)


def _top2_combine(h, wr):
    T = h.shape[0]
    lane = lax.broadcasted_iota(jnp.int32, (T, LANES), 1)
    h_hi = h.astype(BF16)
    h_lo = (h - h_hi.astype(F32)).astype(BF16)
    w_hi = wr.astype(BF16)
    w_lo = (wr - w_hi.astype(F32)).astype(BF16)
    logits = _dot(h_hi, w_hi) + (_dot(h_hi, w_lo) + _dot(h_lo, w_hi))
    logits = jnp.where(lane < N_EXPERTS, logits, -jnp.inf)
    m1 = jnp.max(logits, axis=-1, keepdims=True)
    i1 = jnp.min(jnp.where(logits == m1, lane, LANES), axis=-1, keepdims=True)
    rest = jnp.where(lane == i1, -jnp.inf, logits)
    m2 = jnp.max(rest, axis=-1, keepdims=True)
    i2 = jnp.min(jnp.where(rest == m2, lane, LANES), axis=-1, keepdims=True)
    e2 = jnp.exp(m2 - m1)
    return jnp.where(lane == i1, 1.0 / (1.0 + e2), 0.0) + jnp.where(lane == i2, e2 / (1.0 + e2), 0.0)


def _merge_kernel(with_router, *refs):
    (x_ref, hm_ref, yn_ref, bonus_ref, g_ref, yssd_ref, yfox_ref, wgate_ref, gb_ref, lnw_ref, lnb_ref,
     prw_ref, pssd_ref, pfox_ref, wout_ref, gm_ref, gain_ref, sc_ref, sh_ref) = refs[:19]
    D = D_MODEL
    hm = hm_ref[...]
    y_rw = (yn_ref[...] * lnw_ref[...] + lnb_ref[...] + bonus_ref[...]) * g_ref[...]
    merged = None
    for b, (y_b, proj_ref) in enumerate(((y_rw, prw_ref), (yssd_ref[...], pssd_ref), (yfox_ref[...], pfox_ref))):
        cols = slice(b * D, (b + 1) * D)
        gate = _sigmoid(_dot(hm, wgate_ref[:, cols]) + gb_ref[:, cols])
        term = gate * _dot_bf16(y_b, proj_ref[...])
        merged = term if merged is None else merged + term
    x_new = x_ref[...] + gm_ref[...] * _dot_bf16(merged, wout_ref[...])
    hf = _norm_mod(x_new, gain_ref[...], sc_ref[...], sh_ref[...])
    if with_router:
        wr_ref, o_ref, h_ref, comb_ref = refs[19:]
        comb_ref[...] = _top2_combine(hf, wr_ref[...])
    else:
        o_ref, h_ref = refs[19:]
    o_ref[...] = x_new
    h_ref[...] = hf.astype(h_ref.dtype)


def merge(x, hm, yn, bonus, g, y_ssd, y_fox, p, router=None, tm=256):
    L, D = x.shape
    full = lambda a: pl.BlockSpec(a.shape, lambda i: (0,) * a.ndim)
    rows = lambda n: pl.BlockSpec((tm, n), lambda i: (i, 0))
    small = [p["w_gate"], p["gate_b"], p["lnx_w"], p["lnx_b"], p["proj_rw"], p["proj_ssd"], p["proj_fox"],
             p["w_out"], p["g_m"], p["norm_gain"], p["sc_f"], p["sh_f"]]
    out_specs = [rows(D), rows(D)]
    out_shape = [jax.ShapeDtypeStruct((L, D), F32), jax.ShapeDtypeStruct((L, D), BF16)]
    if router is not None:
        small.append(router)
        out_specs.append(rows(LANES))
        out_shape.append(jax.ShapeDtypeStruct((L, LANES), F32))
    return pl.pallas_call(
        functools.partial(_merge_kernel, router is not None),
        grid=(L // tm,),
        in_specs=[rows(D), rows(D), rows(RW_WIDTH), rows(RW_WIDTH), rows(RW_WIDTH), rows(SSD_WIDTH),
                  rows(FOX_WIDTH)] + [full(a) for a in small],
        out_specs=out_specs,
        out_shape=out_shape,
        compiler_params=_cparams("arbitrary"),
    )(x, hm, yn, bonus, g, y_ssd, y_fox, *small)


def _ffn_kernel(with_next, *refs):
    h_ref, x_ref, wg_ref, wu_ref, wd_ref, gf_ref = refs[:6]
    j = pl.program_id(1)
    acc_scr = refs[-1]

    @pl.when(j == 0)
    def _():
        acc_scr[...] = jnp.zeros_like(acc_scr)

    h = h_ref[...]
    gte = _dot(h, wg_ref[...])
    up = _dot(h, wu_ref[...])
    act = gte * _sigmoid(gte) * up
    acc_scr[...] += _dot(act.astype(BF16), wd_ref[...])

    @pl.when(j == pl.num_programs(1) - 1)
    def _():
        x_new = x_ref[...] + gf_ref[...] * acc_scr[...]
        if with_next:
            gain_ref, sc_ref, sh_ref, o_ref, hn_ref = refs[6:11]
            hn_ref[...] = _norm_mod(x_new, gain_ref[...], sc_ref[...], sh_ref[...]).astype(hn_ref.dtype)
        else:
            o_ref = refs[6]
        o_ref[...] = x_new


def ffn_dense(h, x, w_gu, w_down, g_f, next_norm=None, tm=512, tf=1408):
    L, D = x.shape
    Fh = w_down.shape[0]
    nf = Fh // tf
    row = pl.BlockSpec((1, D), lambda i, j: (0, 0))
    tile = pl.BlockSpec((tm, D), lambda i, j: (i, 0))
    args = [h, x, w_gu, w_gu, w_down, g_f]
    in_specs = [tile, tile,
                pl.BlockSpec((D, tf), lambda i, j: (0, j)),
                pl.BlockSpec((D, tf), lambda i, j: (0, j + nf)),
                pl.BlockSpec((tf, D), lambda i, j: (j, 0)), row]
    out_specs = [tile]
    out_shape = [jax.ShapeDtypeStruct((L, D), F32)]
    if next_norm is not None:
        args += list(next_norm)
        in_specs += [row, row, row]
        out_specs.append(tile)
        out_shape.append(jax.ShapeDtypeStruct((L, D), BF16))
    out = pl.pallas_call(
        functools.partial(_ffn_kernel, next_norm is not None),
        grid=(L // tm, nf),
        in_specs=in_specs,
        out_specs=out_specs,
        out_shape=out_shape,
        scratch_shapes=[pltpu.VMEM((tm, D), F32)],
        compiler_params=_cparams("arbitrary", "arbitrary"),
    )(*args)
    return out if next_norm is not None else (out[0], None)


def _moe_slots(slot_scr, comb_ref):
    slotval = slot_scr[...]
    comb = comb_ref[...]
    s1 = jnp.max(slotval, axis=-1, keepdims=True)
    first = slotval == s1
    c1 = jnp.sum(jnp.where(first, comb, 0.0), axis=-1, keepdims=True)
    rest = jnp.where(first, -1.0, slotval)
    s2 = jnp.max(rest, axis=-1, keepdims=True)
    c2 = jnp.sum(jnp.where(jnp.logical_and(rest == s2, rest >= 0.0), comb, 0.0), axis=-1, keepdims=True)
    return s1, s2, c1, c2


def _moe_kernel(nsb_ref, off_ref, h_ref, x_ref, comb_ref, wg_ref, wu_ref, wd_ref, gf_ref, o_ref,
                slot_scr, cslot_scr, xy_scr, acc_scr):
    i = pl.program_id(0)
    e = pl.program_id(1)
    j = pl.program_id(2)
    last_j = pl.num_programs(2) - 1
    T = h_ref.shape[0]
    S = xy_scr.shape[0]
    SB = MOE_SB

    @pl.when(jnp.logical_and(e == 0, j == 0))
    def _():
        lane = lax.broadcasted_iota(jnp.int32, (LANES, LANES), 1)
        off = jnp.zeros((LANES, LANES), F32)
        for ex in range(N_EXPERTS):
            off = jnp.where(lane == ex, off_ref[i, ex].astype(F32), off)
        before = _tril(LANES, strict=True).astype(BF16)
        ones = jnp.ones((LANES, LANES), BF16)
        for rb in range(T // LANES):
            rows = slice(rb * LANES, (rb + 1) * LANES)
            sel = comb_ref[rows, :] > 0.0
            sel_b = jnp.where(sel, 1.0, 0.0).astype(BF16)
            slot_scr[rows, :] = jnp.where(sel, _dot(before, sel_b) + off, -1.0)
            off = off + _dot(ones, sel_b)
        s1, s2, c1, c2 = _moe_slots(slot_scr, comb_ref)
        used = off_ref[i, N_EXPERTS - 1] + nsb_ref[i, N_EXPERTS - 1] * SB
        for cb in range(S // MOE_CB):
            chunk = slice(cb * MOE_CB, (cb + 1) * MOE_CB)

            @pl.when(used > cb * MOE_CB)
            def _():
                slot = (lax.broadcasted_iota(jnp.int32, (T, MOE_CB), 1) + cb * MOE_CB).astype(F32)
                w = jnp.where(s1 == slot, c1, 0.0) + jnp.where(s2 == slot, c2, 0.0)
                pt = jnp.where(w > 0.0, 1.0, 0.0).astype(BF16)
                xy_scr[chunk, :] = _dot_tn(pt, h_ref[...]).astype(BF16)
                w_hi = w.astype(BF16)
                w_lo = (w - w_hi.astype(F32)).astype(BF16)
                ones_t = jnp.ones((T, LANES), BF16)
                cslot_scr[chunk, :] = _dot_tn(w_hi, ones_t) + _dot_tn(w_lo, ones_t)

            @pl.when(used <= cb * MOE_CB)
            def _():
                xy_scr[chunk, :] = jnp.zeros((MOE_CB, xy_scr.shape[1]), BF16)
                cslot_scr[chunk, :] = jnp.zeros((MOE_CB, LANES), F32)

    base = off_ref[i, e]
    nb = nsb_ref[i, e]

    def ffn_block(r0, n_rows):
        rows = pl.ds(pl.multiple_of(base + r0, SB), n_rows)
        arows = pl.ds(pl.multiple_of(r0, SB), n_rows)
        xb = xy_scr[rows, :]
        gte = _dot(xb, wg_ref[0])
        up = _dot(xb, wu_ref[0])
        part = _dot((gte * _sigmoid(gte) * up).astype(BF16), wd_ref[0])

        @pl.when(j == 0)
        def _():
            acc_scr[arows, :] = part

        @pl.when(jnp.logical_and(j > 0, j < last_j))
        def _():
            acc_scr[arows, :] += part

        @pl.when(j == last_j)
        def _():
            xy_scr[rows, :] = ((acc_scr[arows, :] + part) * cslot_scr[rows, 0:1]).astype(BF16)

    def quad_body(p, carry):
        ffn_block(p * (4 * SB), 4 * SB)
        return carry

    lax.fori_loop(0, nb // 4, quad_body, 0)
    for rest in range(1, 4):
        @pl.when(nb % 4 == rest)
        def _():
            ffn_block((nb - rest) * SB, rest * SB)

    @pl.when(jnp.logical_and(e == pl.num_programs(1) - 1, j == last_j))
    def _():
        s1, s2, _, _ = _moe_slots(slot_scr, comb_ref)
        total = jnp.zeros((T, xy_scr.shape[1]), F32)
        for cb in range(S // MOE_CB):
            slot = (lax.broadcasted_iota(jnp.int32, (T, MOE_CB), 1) + cb * MOE_CB).astype(F32)
            pt = jnp.where(jnp.logical_or(s1 == slot, s2 == slot), 1.0, 0.0).astype(BF16)
            total = total + _dot(pt, xy_scr[cb * MOE_CB:(cb + 1) * MOE_CB, :])
        o_ref[...] = x_ref[...] + gf_ref[...] * total


def ffn_moe(h, x, comb, w_gu, w_down, g_f, tm=1024, tf=896):
    L, D = x.shape
    E, Fh, _ = w_down.shape
    nf = Fh // tf
    assert nf >= 2 and E == N_EXPERTS
    nt = L // tm
    counts = jnp.sum((comb[:, :E] > 0.0).reshape(nt, tm, E), axis=1, dtype=jnp.int32)
    nsb = (counts + (MOE_SB - 1)) // MOE_SB
    off = (jnp.cumsum(nsb, axis=1) - nsb) * MOE_SB
    idx = lambda f: (lambda i, e, j, n, o: f(i, e, j))
    grid_spec = pltpu.PrefetchScalarGridSpec(
        num_scalar_prefetch=2,
        grid=(nt, E, nf),
        in_specs=[pl.BlockSpec((tm, D), idx(lambda i, e, j: (i, 0))),
                  pl.BlockSpec((tm, D), idx(lambda i, e, j: (i, 0))),
                  pl.BlockSpec((tm, LANES), idx(lambda i, e, j: (i, 0))),
                  pl.BlockSpec((1, D, tf), idx(lambda i, e, j: (e, 0, j))),
                  pl.BlockSpec((1, D, tf), idx(lambda i, e, j: (e, 0, j + nf))),
                  pl.BlockSpec((1, tf, D), idx(lambda i, e, j: (e, j, 0))),
                  pl.BlockSpec((1, D), idx(lambda i, e, j: (0, 0)))],
        out_specs=pl.BlockSpec((tm, D), idx(lambda i, e, j: (i, 0))),
        scratch_shapes=[pltpu.VMEM((tm, LANES), F32),
                        pltpu.VMEM((TOP_K * tm + E * MOE_SB, LANES), F32),
                        pltpu.VMEM((TOP_K * tm + E * MOE_SB, D), BF16),
                        pltpu.VMEM((tm, D), F32)],
    )
    return pl.pallas_call(
        _moe_kernel,
        grid_spec=grid_spec,
        out_shape=jax.ShapeDtypeStruct((L, D), F32),
        compiler_params=_cparams("arbitrary", "arbitrary", "arbitrary"),
    )(nsb, off, h, x, comb, w_gu, w_gu, w_down, g_f)


def _seg_matrix(width, head_dim):
    idx = jnp.arange(width) // head_dim
    return (idx[:, None] == idx[None, :]).astype(BF16)


def _pad_cols(w, n):
    return jnp.pad(w, ((0, 0), (0, n - w.shape[1])))


def _pad_rows(w, n):
    return jnp.pad(w, ((0, n - w.shape[0]), (0, 0)))


def rwkv_branch(h, w_rw, p, v_first):
    r, k, v, lw, a, b, g, bonus = rw_pre(h, w_rw, p, v_first)
    return wkv7(r, lw, k, v, a, b), bonus, g, v


def fox_branch(h, w_fox, p):
    q, k, v, F = fox_pre(h, w_fox, p)
    FT = jnp.transpose(F[:, :FOX_HEADS])
    lo = fox_first_block(FT, p["bound2"])
    return fox_attention(q, k, v, FT, lo)


def kernel(x, c, ada_w, ada_b, norm_mix, norm_ffn, w_in, rw_mu, rw_w0, rw_w_up, rw_a0, rw_a_up, rw_g_up, rw_k_k, rw_k_a, rw_r_k, rw_lnx_w, rw_lnx_b, rw_v0, rw_v_down, rw_v_up, ssd_conv_w, ssd_conv_b, ssd_dt_bias, ssd_a_log, ssd_d, ssd_norm, fox_f_bias, fox_q_gain, fox_k_gain, gate_b, proj_rw, proj_ssd, proj_fox, w_out, ffn_w_gu, ffn_w_down, moe_router, moe_w_gu, moe_w_down):
    depth = w_in.shape[0]
    D = D_MODEL
    xs = x[0]
    row = lambda t: t.reshape(1, -1).astype(F32)
    seg64 = _seg_matrix(RW_WIDTH, RW_HEAD_DIM)
    rw_cols = 3 * RW_WIDTH + RW_DECAY_LORA + RW_AAA_LORA + RW_GATE_LORA
    ssd_cols = SSD_WIDTH + SSD_XBC + SSD_HEADS
    fox_cols = 3 * FOX_WIDTH + FOX_HEADS
    expand = (jnp.arange(LANES)[:, None] == (jnp.arange(SSD_WIDTH) // SSD_HEAD_DIM)[None, :]).astype(BF16)
    v_first = None
    mods = []
    for l in range(depth):
        mod = adaln_mod(c, ada_w[l], ada_b[l])
        mods.append([mod[:, i * D:(i + 1) * D] for i in range(6)])
    h = None
    for l in range(depth):
        sh_m, sc_m, g_m, sh_f, sc_f, g_f = mods[l]
        if h is None:
            h = norm_mod(xs, row(norm_mix[l]), sc_m, sh_m)

        wl = w_in[l]
        o = 0
        w_r = wl[:, o:o + rw_cols]; o += rw_cols
        w_s = wl[:, o:o + ssd_cols]; o += ssd_cols
        w_f = wl[:, o:o + fox_cols]; o += fox_cols
        w_g = wl[:, o:]
        W3 = 3 * RW_WIDTH
        o_a = W3 + RW_DECAY_LORA
        o_g = o_a + RW_AAA_LORA

        def rw_layout(t, hv):
            parts = [t[:, :W3], _pad_cols(t[:, W3:o_a], LANES), _pad_cols(t[:, o_a:o_g], LANES),
                     _pad_cols(t[:, o_g:], 2 * LANES), _pad_cols(hv, LANES)]
            return jnp.concatenate(parts, axis=1)

        if l == 0:
            hv_w = jnp.zeros((D, RW_VRES_LORA), F32)
        else:
            hv_w = rw_v_down[l - 1]
        w_rw = rw_layout(w_r, hv_w).astype(BF16)
        mu = rw_layout(rw_mu[l].reshape(1, -1), jnp.zeros((1, RW_VRES_LORA), F32))
        rw_p = dict(mu=mu, w0=row(rw_w0[l]), w_up=_pad_rows(rw_w_up[l], LANES).astype(BF16), a0=row(rw_a0[l]),
                    a_up=_pad_rows(rw_a_up[l], LANES).astype(BF16),
                    g_up=_pad_rows(rw_g_up[l], 2 * LANES).astype(BF16),
                    k_k=row(rw_k_k[l]), k_a=row(rw_k_a[l]), r_k=row(rw_r_k[l]), seg=seg64)
        if l > 0:
            rw_p.update(v0=row(rw_v0[l - 1]), v_up=_pad_rows(rw_v_up[l - 1], LANES).astype(BF16))
        yn, bonus, g_rw, v_cur = rwkv_branch(h, w_rw, rw_p, v_first)
        if l == 0:
            v_first = v_cur

        w_ssd = jnp.concatenate([w_s[:, SSD_WIDTH:SSD_WIDTH + SSD_XBC], w_s[:, :SSD_WIDTH],
                                 _pad_cols(w_s[:, SSD_WIDTH + SSD_XBC:], LANES)], axis=1).astype(BF16)
        A = -jnp.exp(ssd_a_log[l].astype(F32))
        ssd_p = dict(conv_w=ssd_conv_w[l], conv_b=row(ssd_conv_b[l]),
                     dt_bias=_pad_cols(row(ssd_dt_bias[l]), LANES),
                     A=_pad_cols(row(A), LANES), expand=expand,
                     d_skip=row(jnp.repeat(ssd_d[l], SSD_HEAD_DIM)), norm_w=row(ssd_norm[l]))
        y_ssd = ssd_mixer(h, w_ssd, ssd_p)

        fb = fox_f_bias[l]
        hid = jnp.arange(FOX_HEADS)
        ahead = jnp.logical_or(fb[None, :] < fb[:, None],
                               jnp.logical_and(fb[None, :] == fb[:, None], hid[None, :] < hid[:, None]))
        rank = jnp.sum(ahead, axis=1)
        head_order = jnp.sum(jnp.where(rank[None, :] == hid[:, None], hid[None, :], 0), axis=1)
        chan_order = (head_order[:, None] * FOX_HEAD_DIM + jnp.arange(FOX_HEAD_DIM)[None, :]).reshape(-1)
        w_fox = jnp.concatenate(
            [jnp.take(w_f[:, i * FOX_WIDTH:(i + 1) * FOX_WIDTH], chan_order, axis=1) for i in range(3)]
            + [_pad_cols(jnp.take(w_f[:, 3 * FOX_WIDTH:], head_order, axis=1), LANES)], axis=1).astype(BF16)
        f_bias_sorted = jnp.take(fox_f_bias[l], head_order)
        proj_fox_sorted = jnp.take(proj_fox[l], chan_order, axis=0)
        bound2 = (1.02 * FOX_HEAD_DIM ** 0.5 * LOG2E) * jnp.max(jnp.abs(fox_q_gain[l])) * jnp.max(jnp.abs(fox_k_gain[l]))
        fox_p = dict(f_bias=_pad_cols(row(f_bias_sorted), LANES),
                     q_gain=row(jnp.tile(fox_q_gain[l], FOX_HEADS)) * (FOX_HEAD_DIM ** -0.5 * LOG2E),
                     k_gain=row(jnp.tile(fox_k_gain[l], FOX_HEADS)), seg=seg64, bound2=bound2)
        y_fox = fox_branch(h, w_fox, fox_p)

        mp = dict(w_gate=w_g.astype(BF16), gate_b=row(gate_b[l]), lnx_w=row(rw_lnx_w[l]), lnx_b=row(rw_lnx_b[l]),
                  proj_rw=proj_rw[l].astype(BF16), proj_ssd=proj_ssd[l].astype(BF16),
                  proj_fox=proj_fox_sorted.astype(BF16), w_out=w_out[l].astype(BF16), g_m=g_m,
                  norm_gain=row(norm_ffn[l]), sc_f=sc_f, sh_f=sh_f)

        if l % 2 == 0:
            xs, hf = merge(xs, h, yn, bonus, g_rw, y_ssd, y_fox, mp)
            next_norm = None
            if l + 1 < depth:
                next_norm = (row(norm_mix[l + 1]), mods[l + 1][1], mods[l + 1][0])
            xs, h = ffn_dense(hf, xs, ffn_w_gu[l // 2].astype(BF16), ffn_w_down[l // 2].astype(BF16), g_f,
                              next_norm)
        else:
            xs, hf, comb = merge(xs, h, yn, bonus, g_rw, y_ssd, y_fox, mp,
                                 router=_pad_cols(moe_router[l // 2], LANES))
            xs = ffn_moe(hf, xs, comb, moe_w_gu[l // 2].astype(BF16), moe_w_down[l // 2].astype(BF16), g_f)
            h = None
    return xs[None]
```

```python
import functools
import math

import jax
import jax.numpy as jnp
from jax import lax
from jax.experimental import pallas as pl
from jax.experimental.pallas import tpu as pltpu

F32 = jnp.float32
BF16 = jnp.bfloat16
HIGHEST = lax.Precision.HIGHEST

D_MODEL = 1024
LANES = 128
SUBLANES = 8

RW_HEADS = 8
RW_HEAD_DIM = 64
RW_WIDTH = RW_HEADS * RW_HEAD_DIM
RW_DECAY_LORA = 64
RW_AAA_LORA = 64
RW_VRES_LORA = 32
RW_GATE_LORA = 160
GN_EPS = 64e-5
RW_OFF_WLO = 3 * RW_WIDTH
RW_OFF_ALO = RW_OFF_WLO + LANES
RW_OFF_GLO = RW_OFF_ALO + LANES
RW_OFF_HV = RW_OFF_GLO + 2 * LANES
RW_PAD_COLS = RW_OFF_HV + LANES
RW_CHUNK = 64

SSD_HEADS = 16
SSD_HEAD_DIM = 64
SSD_WIDTH = SSD_HEADS * SSD_HEAD_DIM
SSD_GROUPS = 4
SSD_STATE = 128
SSD_CONV = 4
SSD_CHUNK = 128
SSD_XBC = SSD_WIDTH + 2 * SSD_GROUPS * SSD_STATE
SSD_GROUP_WIDTH = SSD_WIDTH // SSD_GROUPS

FOX_HEADS = 8
FOX_HEAD_DIM = 64
FOX_WIDTH = FOX_HEADS * FOX_HEAD_DIM
FOX_TQ = 512
FOX_TK = 512
LOG2E = 1.4426950408889634

FFN_DENSE = 2816
N_EXPERTS = 8
TOP_K = 2
FFN_EXPERT = 3584
MOE_SB = 128
MOE_CB = 512
EPS = 1e-6

VMEM_LIMIT = 56 * 1024 * 1024


def _cparams(*sem):
    return pltpu.CompilerParams(dimension_semantics=sem, vmem_limit_bytes=VMEM_LIMIT)


def _sigmoid(x):
    return 1.0 / (1.0 + jnp.exp(-x))


def _softplus(x):
    return jnp.maximum(x, 0.0) + jnp.log(1.0 + jnp.exp(-jnp.abs(x)))


def _dot(a, b):
    return jnp.dot(a, b, preferred_element_type=F32)


def _dot_bf16(a, b):
    return jnp.dot(a.astype(BF16), b.astype(BF16), preferred_element_type=F32)


def _dot_hi(a, b):
    return jnp.dot(a, b, precision=HIGHEST, preferred_element_type=F32)


def _dot_nt(a, b, precision=None):
    return lax.dot_general(a, b, (((1,), (1,)), ((), ())), precision=precision, preferred_element_type=F32)


def _dot_tn(a, b, precision=None):
    return lax.dot_general(a, b, (((0,), (0,)), ((), ())), precision=precision, preferred_element_type=F32)


def _split_dot(x, ones_bf16):
    return _dot(x.astype(BF16), ones_bf16)


def _split3(x):
    hi = x.astype(BF16)
    r1 = x - hi.astype(F32)
    mid = r1.astype(BF16)
    return hi, mid, (r1 - mid.astype(F32)).astype(BF16)


def _tril(n, strict=False):
    r = lax.broadcasted_iota(jnp.int32, (n, n), 0)
    c = lax.broadcasted_iota(jnp.int32, (n, n), 1)
    return (r > c) if strict else (r >= c)


def _mod_kernel(c_ref, w_ref, b_ref, o_ref):
    c = c_ref[...]
    o_ref[...] = _dot_hi(c * _sigmoid(c), w_ref[...]) + b_ref[...]


def adaln_mod(c, w, b):
    d, n = w.shape
    tn = 1024
    c8 = jnp.broadcast_to(c, (SUBLANES, d))
    out = pl.pallas_call(
        _mod_kernel,
        grid=(n // tn,),
        in_specs=[pl.BlockSpec((SUBLANES, d), lambda j: (0, 0)),
                  pl.BlockSpec((d, tn), lambda j: (0, j)),
                  pl.BlockSpec((1, tn), lambda j: (0, j))],
        out_specs=pl.BlockSpec((SUBLANES, tn), lambda j: (0, j)),
        out_shape=jax.ShapeDtypeStruct((SUBLANES, n), F32),
        compiler_params=_cparams("arbitrary"),
    )(c8, w, b.reshape(1, n))
    return out[:1]


def _norm_mod(x, gain, sc, sh):
    y = x * lax.rsqrt(jnp.mean(x * x, axis=-1, keepdims=True) + EPS)
    return y * gain * (1.0 + sc) + sh


def _norm_kernel(x_ref, gain_ref, sc_ref, sh_ref, h_ref):
    h_ref[...] = _norm_mod(x_ref[...], gain_ref[...], sc_ref[...], sh_ref[...]).astype(h_ref.dtype)


def norm_mod(x, gain, sc, sh, tm=512):
    L, d = x.shape
    row = pl.BlockSpec((1, d), lambda i: (0, 0))
    return pl.pallas_call(
        _norm_kernel,
        grid=(L // tm,),
        in_specs=[pl.BlockSpec((tm, d), lambda i: (i, 0)), row, row, row],
        out_specs=pl.BlockSpec((tm, d), lambda i: (i, 0)),
        out_shape=jax.ShapeDtypeStruct((L, d), BF16),
        compiler_params=_cparams("arbitrary"),
    )(x, gain, sc, sh)


def _rw_pre_kernel(has_vres, *refs):
    if has_vres:
        (h_ref, w_ref, mu_ref, w0_ref, wup_ref, a0_ref, aup_ref, gup_ref, kk_ref, ka_ref, rk_ref, seg_ref,
         vfirst_ref, v0_ref, vup_ref,
         r_out, k_out, v_out, lw_out, a_out, b_out, g_out, bonus_out, ext_scr) = refs
    else:
        (h_ref, w_ref, mu_ref, w0_ref, wup_ref, a0_ref, aup_ref, gup_ref, kk_ref, ka_ref, rk_ref, seg_ref,
         r_out, k_out, v_out, lw_out, a_out, b_out, g_out, bonus_out, ext_scr) = refs
    T = h_ref.shape[0]

    @pl.when(pl.program_id(0) == 0)
    def _():
        ext_scr[0:SUBLANES, :] = jnp.zeros((SUBLANES, ext_scr.shape[1]), F32)

    cur = _dot(h_ref[...], w_ref[...])
    ext_scr[SUBLANES:SUBLANES + T, :] = cur
    prev = ext_scr[SUBLANES - 1:SUBLANES - 1 + T, :]
    ext_scr[0:SUBLANES, :] = cur[T - SUBLANES:T, :]
    s = cur + (prev - cur) * mu_ref[...]

    W = RW_WIDTH
    r = s[:, 0:W]
    k = s[:, W:2 * W]
    v = s[:, 2 * W:3 * W]
    w_lo = s[:, RW_OFF_WLO:RW_OFF_WLO + LANES]
    a_lo = s[:, RW_OFF_ALO:RW_OFF_ALO + LANES]
    g_lo = s[:, RW_OFF_GLO:RW_OFF_GLO + 2 * LANES]
    seg = seg_ref[...]

    wlog = -_softplus(-(w0_ref[...] + _dot_bf16(jnp.tanh(w_lo), wup_ref[...]))) - 0.5
    a = _sigmoid(a0_ref[...] + _dot_bf16(a_lo, aup_ref[...]))
    g = _dot_bf16(_sigmoid(g_lo), gup_ref[...])
    if has_vres:
        hv = s[:, RW_OFF_HV:RW_OFF_HV + LANES]
        v = v + (vfirst_ref[...].astype(F32) - v) * _sigmoid(v0_ref[...] + _dot_bf16(hv, vup_ref[...]))
    kk = k * kk_ref[...]
    kk = kk / jnp.maximum(jnp.sqrt(_split_dot(kk * kk, seg)), 1e-12)
    k = k * (1.0 + (a - 1.0) * ka_ref[...])
    r_out[...] = r.astype(r_out.dtype)
    k_out[...] = k.astype(k_out.dtype)
    v_out[...] = v.astype(v_out.dtype)
    lw_out[...] = -jnp.exp(wlog)
    a_out[...] = (-kk).astype(a_out.dtype)
    b_out[...] = (kk * a).astype(b_out.dtype)
    g_out[...] = g.astype(g_out.dtype)
    bonus_out[...] = (_split_dot(r * k * rk_ref[...], seg) * v).astype(bonus_out.dtype)


def rw_pre(h, w_rw, p, v_first, tm=512):
    L, D = h.shape
    has_vres = v_first is not None
    full = lambda a: pl.BlockSpec(a.shape, lambda i: (0,) * a.ndim)
    rows = lambda n: pl.BlockSpec((tm, n), lambda i: (i, 0))
    args = [h, w_rw, p["mu"], p["w0"], p["w_up"], p["a0"], p["a_up"], p["g_up"], p["k_k"], p["k_a"], p["r_k"],
            p["seg"]]
    specs = [rows(D)] + [full(a) for a in args[1:]]
    if has_vres:
        args += [v_first, p["v0"], p["v_up"]]
        specs += [rows(RW_WIDTH), full(p["v0"]), full(p["v_up"])]
    out_dtypes = [BF16, BF16, BF16, F32, BF16, BF16, BF16, BF16]
    return pl.pallas_call(
        functools.partial(_rw_pre_kernel, has_vres),
        grid=(L // tm,),
        in_specs=specs,
        out_specs=[rows(RW_WIDTH)] * 8,
        out_shape=[jax.ShapeDtypeStruct((L, RW_WIDTH), dt) for dt in out_dtypes],
        scratch_shapes=[pltpu.VMEM((tm + SUBLANES, RW_PAD_COLS), F32)],
        compiler_params=_cparams("arbitrary"),
    )(*args)


def _wkv_kernel(r_ref, lw_ref, k_ref, v_ref, a_ref, b_ref, y_ref, h_scr):
    C = RW_CHUNK
    C2 = 2 * C
    n_chunks = r_ref.shape[0] // C

    @pl.when(pl.program_id(1) == 0)
    def _():
        h_scr[...] = jnp.zeros_like(h_scr)

    is_a = lax.broadcasted_iota(jnp.int32, (C, LANES), 1) < RW_HEAD_DIM
    ri = lax.broadcasted_iota(jnp.int32, (C2, C2), 0)
    ci = lax.broadcasted_iota(jnp.int32, (C2, C2), 1)
    same = (ri >= C) == (ci >= C)
    strict = jnp.logical_and(same, ri > ci)
    incl = jnp.logical_and(same, ri >= ci)
    eye2 = (ri == ci).astype(F32)
    tri_c = _tril(C).astype(BF16)

    def two(x):
        return jnp.concatenate([jnp.where(is_a, x, 0.0), jnp.where(is_a, 0.0, x)], axis=0)

    chunks = range(n_chunks)
    at2, rt2, bh2, kh2, v2, n_ab, a_ak, m_rb, m_rk, p_end = ([] for _ in range(10))
    for c in chunks:
        sl = pl.ds(c * C, C)
        lw = lw_ref[sl, :]
        lw_hi = lw.astype(BF16)
        lw_lo = (lw - lw_hi.astype(F32)).astype(BF16)
        cs = _dot(tri_c, lw_hi) + _dot(tri_c, lw_lo)
        cs_end = cs[C - 1:C, :]
        e_neg = jnp.exp(-cs)
        e_end = jnp.exp(cs_end - cs)
        a = a_ref[sl, :].astype(F32)
        b = b_ref[sl, :].astype(F32)
        k = k_ref[sl, :].astype(F32)
        at2.append(two(a * jnp.exp(cs - lw)))
        rt2.append(two(r_ref[sl, :].astype(F32) * jnp.exp(cs)))
        bh2.append(two(b * e_end))
        kh2.append(two(k * e_end))
        v2.append(two(v_ref[sl, :].astype(F32)))
        p_end.append(jnp.exp(cs_end))
        quad = _dot_nt(jnp.concatenate([at2[c], rt2[c]], axis=0).astype(BF16),
                       jnp.concatenate([two(b * e_neg), two(k * e_neg)], axis=0).astype(BF16))
        n_ab.append(jnp.where(strict, quad[0:C2, 0:C2], 0.0))
        a_ak.append(jnp.where(strict, quad[0:C2, C2:], 0.0))
        m_rb.append(jnp.where(incl, quad[C2:, 0:C2], 0.0))
        m_rk.append(jnp.where(incl, quad[C2:, C2:], 0.0))
    t_inv = [eye2 + n for n in n_ab]
    n_pow = n_ab
    n_pow = [_dot_bf16(n, n) for n in n_pow]
    for _ in range(int(math.log2(C)) - 2):
        both = [_dot_bf16(jnp.concatenate([n, t], axis=0), n) for n, t in zip(n_pow, t_inv)]
        t_inv = [t + b[C2:, :] for t, b in zip(t_inv, both)]
        n_pow = [b[:C2, :] for b in both]
    t_inv = [t + _dot_bf16(t, n) for t, n in zip(t_inv, n_pow)]
    akv = [_dot_bf16(a_ak[c], v2[c]) for c in chunks]
    au = [_dot_bf16(t_inv[c], jnp.concatenate([at2[c], akv[c]], axis=1)) for c in chunks]
    mau = [_dot_bf16(m_rb[c], au[c]) for c in chunks]
    mkv = [_dot_bf16(m_rk[c], v2[c]) for c in chunks]
    gu = [_dot_tn(bh2[c].astype(BF16), au[c].astype(BF16)) for c in chunks]
    kv = [_dot_tn(kh2[c].astype(BF16), v2[c].astype(BF16)) for c in chunks]
    inv_n = 1.0 / RW_HEAD_DIM
    h = h_scr[...]
    for c in chunks:
        rh2 = rt2[c] + mau[c][:, :LANES]
        g_mat = eye2 * p_end[c] + gu[c][:, :LANES]
        yh = _dot_bf16(jnp.concatenate([rh2, g_mat], axis=0), h)
        h = yh[C2:, :] + gu[c][:, LANES:] + kv[c]
        y2 = yh[:C2, :] + mau[c][:, LANES:] + mkv[c]
        y = y2[:C, :] + y2[C:, :]
        s_all = jnp.sum(y, axis=-1, keepdims=True)
        s_a = jnp.sum(jnp.where(is_a, y, 0.0), axis=-1, keepdims=True)
        yc = y - jnp.where(is_a, s_a, s_all - s_a) * inv_n
        sq = yc * yc
        q_all = jnp.sum(sq, axis=-1, keepdims=True)
        q_a = jnp.sum(jnp.where(is_a, sq, 0.0), axis=-1, keepdims=True)
        yn = yc * lax.rsqrt(jnp.where(is_a, q_a, q_all - q_a) * inv_n + GN_EPS)
        y_ref[pl.ds(c * C, C), :] = yn.astype(y_ref.dtype)
    h_scr[...] = h


def wkv7(r, lw, k, v, a, b, rows=2048):
    L, W = r.shape
    spec = pl.BlockSpec((rows, LANES), lambda p, i: (i, p))
    return pl.pallas_call(
        _wkv_kernel,
        grid=(W // LANES, L // rows),
        in_specs=[spec] * 6,
        out_specs=spec,
        out_shape=jax.ShapeDtypeStruct((L, W), BF16),
        scratch_shapes=[pltpu.VMEM((LANES, LANES), F32)],
        compiler_params=_cparams("arbitrary", "arbitrary"),
    )(r, lw, k, v, a, b)


def _ssd_kernel(h_ref, w_ref, cw_ref, cb_ref, bias_ref, A_ref, exp_ref, dskip_ref, nw_ref, o_ref,
                ext_scr, xbc_scr, s_scr):
    Q = SSD_CHUNK
    GW = SSD_GROUP_WIDTH
    NS = SSD_STATE
    R = h_ref.shape[0]

    @pl.when(pl.program_id(0) == 0)
    def _():
        s_scr[...] = jnp.zeros_like(s_scr)
        ext_scr[0:SUBLANES, :] = jnp.zeros((SUBLANES, ext_scr.shape[1]), F32)

    cols = _dot(h_ref[...], w_ref[...])
    ext_scr[SUBLANES:SUBLANES + R, :] = cols[:, 0:SSD_XBC]
    acc = cols[:, 0:SSD_XBC] * cw_ref[SSD_CONV - 1:SSD_CONV, :] + cb_ref[...]
    for j in range(1, SSD_CONV):
        acc = acc + ext_scr[SUBLANES - j:SUBLANES - j + R, :] * cw_ref[SSD_CONV - 1 - j:SSD_CONV - j, :]
    ext_scr[0:SUBLANES, :] = ext_scr[R:R + SUBLANES, :]
    xbc_scr[...] = acc * _sigmoid(acc)

    incl = _tril(Q)
    tri = incl.astype(BF16)
    triu = jnp.logical_not(_tril(Q, strict=True)).astype(BF16)
    expand = exp_ref[...]
    first_half = lax.broadcasted_iota(jnp.int32, (Q, LANES), 1) < SSD_HEAD_DIM
    heads_per_group = SSD_HEADS // SSD_GROUPS
    for c in range(R // Q):
        rows = slice(c * Q, (c + 1) * Q)
        xs = xbc_scr[rows, 0:SSD_WIDTH]
        z = cols[rows, SSD_XBC:SSD_XBC + SSD_WIDTH]
        dt = _softplus(cols[rows, SSD_XBC + SSD_WIDTH:] + bias_ref[...])
        a3 = _split3(dt * A_ref[...])
        a_cum = sum(_dot(tri, t) for t in a3)
        a_cumT = sum(_dot_tn(t, triu) for t in a3)
        a_end = a_cum[Q - 1:Q, :]
        dt_x = _dot(dt.astype(BF16), expand)
        eac_x = _dot(jnp.exp(a_cum).astype(BF16), expand)
        dte_x = _dot(jnp.exp(a_end - a_cum).astype(BF16), expand)
        cd_x = sum(_dot(t, expand) for t in _split3(jnp.broadcast_to(jnp.exp(a_end), (SUBLANES, LANES))))[0:1, :]
        xdt = xs * dt_x
        for g in range(SSD_GROUPS):
            Bg = xbc_scr[rows, SSD_WIDTH + g * NS:SSD_WIDTH + (g + 1) * NS].astype(BF16)
            Cg = xbc_scr[rows, SSD_WIDTH + (SSD_GROUPS + g) * NS:SSD_WIDTH + (SSD_GROUPS + g + 1) * NS].astype(BF16)
            cb = _dot_nt(Cg, Bg)
            gs = slice(g * GW, (g + 1) * GW)
            s_prev = s_scr[:, gs]
            y_g = _dot_bf16(Cg, s_prev) * eac_x[:, gs]
            pieces = []
            for pr in range(heads_per_group // 2):
                ps = slice(g * GW + pr * LANES, g * GW + (pr + 1) * LANES)
                xdt_p = xdt[:, ps]
                acc_p = None
                for e in range(2):
                    hd = g * heads_per_group + pr * 2 + e
                    seg = a_cum[:, hd:hd + 1] - a_cumT[hd:hd + 1, :]
                    m = cb * jnp.exp(jnp.where(incl, seg, -jnp.inf))
                    xm = jnp.where(first_half if e == 0 else jnp.logical_not(first_half), xdt_p, 0.0)
                    t = _dot_bf16(m, xm)
                    acc_p = t if acc_p is None else acc_p + t
                pieces.append(acc_p)
            y_g = y_g + jnp.concatenate(pieces, axis=1)
            s_scr[:, gs] = s_prev * cd_x[:, gs] +_dot_tn(Bg, (dte_x[:, gs] * xdt[:, gs]).astype(BF16))
            y_g = y_g + dskip_ref[:, gs] * xs[:, gs]
            zg = z[:, gs]
            y_g = y_g * (zg * _sigmoid(zg))
            y_g = y_g * lax.rsqrt(jnp.mean(y_g * y_g, axis=-1, keepdims=True) + EPS)
            o_ref[rows, gs] = (y_g * nw_ref[:, gs]).astype(o_ref.dtype)


def ssd_mixer(h, w_ssd, p, rows=512):
    L, D = h.shape
    N = w_ssd.shape[1]
    full = lambda a: pl.BlockSpec(a.shape, lambda i: (0,) * a.ndim)
    small = [p["conv_w"], p["conv_b"], p["dt_bias"], p["A"], p["expand"], p["d_skip"], p["norm_w"]]
    return pl.pallas_call(
        _ssd_kernel,
        grid=(L // rows,),
        in_specs=[pl.BlockSpec((rows, D), lambda i: (i, 0)), full(w_ssd)] + [full(a) for a in small],
        out_specs=pl.BlockSpec((rows, SSD_WIDTH), lambda i: (i, 0)),
        out_shape=jax.ShapeDtypeStruct((L, SSD_WIDTH), BF16),
        scratch_shapes=[pltpu.VMEM((rows + SUBLANES, SSD_XBC), F32), pltpu.VMEM((rows, SSD_XBC), F32),
                        pltpu.VMEM((SSD_STATE, SSD_WIDTH), F32)],
        compiler_params=_cparams("arbitrary"),
    )(h, w_ssd, *small)


def _fox_pre_kernel(h_ref, w_ref, fb_ref, qg_ref, kg_ref, seg_ref, q_out, k_out, v_out, f_out, carry_scr):
    T = h_ref.shape[0]
    W = FOX_WIDTH

    @pl.when(pl.program_id(0) == 0)
    def _():
        carry_scr[...] = jnp.zeros_like(carry_scr)

    cols = _dot(h_ref[...], w_ref[...])
    seg = seg_ref[...]
    q = cols[:, 0:W]
    k = cols[:, W:2 * W]
    inv_d = 1.0 / FOX_HEAD_DIM
    qn = q * lax.rsqrt(_split_dot(q * q, seg) * inv_d + EPS) * qg_ref[...]
    kn = k * lax.rsqrt(_split_dot(k * k, seg) * inv_d + EPS) * kg_ref[...]
    q_out[...] = qn.astype(q_out.dtype)
    k_out[...] = kn.astype(k_out.dtype)
    v_out[...] = cols[:, 2 * W:3 * W].astype(v_out.dtype)
    f = cols[:, 3 * W:3 * W + LANES] + fb_ref[...]
    ls = -_softplus(-f)
    cum = _dot_hi(_tril(T).astype(F32), ls) + carry_scr[0:1, :]
    f_out[...] = cum * LOG2E
    carry_scr[...] = jnp.broadcast_to(cum[T - 1:T, :], carry_scr.shape)


def fox_pre(h, w_fox, p, tm=512):
    L, D = h.shape
    full = lambda a: pl.BlockSpec(a.shape, lambda i: (0,) * a.ndim)
    rows = lambda n: pl.BlockSpec((tm, n), lambda i: (i, 0))
    small = [w_fox, p["f_bias"], p["q_gain"], p["k_gain"], p["seg"]]
    return pl.pallas_call(
        _fox_pre_kernel,
        grid=(L // tm,),
        in_specs=[rows(D)] + [full(a) for a in small],
        out_specs=[rows(FOX_WIDTH)] * 3 + [rows(LANES)],
        out_shape=[jax.ShapeDtypeStruct((L, FOX_WIDTH), BF16)] * 3 + [jax.ShapeDtypeStruct((L, LANES), F32)],
        scratch_shapes=[pltpu.VMEM((SUBLANES, LANES), F32)],
        compiler_params=_cparams("arbitrary"),
    )(h, *small)


def _fox_kernel(lo_ref, q_ref, k_ref, v_ref, fk_ref, o_ref, m_scr, acc_scr, s0_scr, s1_scr):
    tq = q_ref.shape[0]
    tk = FOX_TK
    pair = pl.program_id(0)
    qi = pl.program_id(1)
    is_a = lax.broadcasted_iota(jnp.int32, (tq, LANES), 1) < FOX_HEAD_DIM
    q = q_ref[...]
    zero = jnp.zeros_like(q)
    qs = (jnp.where(is_a, q, zero), jnp.where(is_a, zero, q))
    m_scr[...] = jnp.full(m_scr.shape, -jnp.inf, F32)
    key_is_a = lax.broadcasted_iota(jnp.int32, (tk, LANES), 1) < FOX_HEAD_DIM
    acc_scr[...] = jnp.zeros_like(acc_scr)
    rel = lax.broadcasted_iota(jnp.int32, (tq, tk), 0) - lax.broadcasted_iota(jnp.int32, (tq, tk), 1)
    last = (qi * tq) // tk

    def fill(s_ref, kb, diagonal=False):
        ks = pl.ds(pl.multiple_of(kb * tk, tk), tk)
        k_blk = k_ref[ks, :]
        for e in range(2):
            s = _dot_nt(qs[e], k_blk) - fk_ref[pl.ds(2 * pair + e, 1), ks]
            if diagonal:
                s = jnp.where(rel >= kb * tk - qi * tq, s, -jnp.inf)
            s_ref[e] = s

    def consume(s_ref, kb):
        v_blk = v_ref[pl.ds(pl.multiple_of(kb * tk, tk), tk), :]
        one = jnp.ones_like(v_blk)
        v_one = (jnp.where(key_is_a, v_blk, one), jnp.where(key_is_a, one, v_blk))
        for e in range(2):
            s = s_ref[e]
            m_old = m_scr[e]
            m_new = jnp.maximum(m_old, jnp.max(s, axis=-1, keepdims=True))
            pexp = jnp.exp2(s - m_new)
            acc_scr[e] = jnp.exp2(m_old - m_new) * acc_scr[e] + _dot(pexp.astype(BF16), v_one[e])
            m_scr[e] = m_new

    n = last - jnp.minimum(lo_ref[2 * pair, qi], lo_ref[2 * pair + 1, qi])
    fill(s0_scr, last, diagonal=True)

    def two_steps(u, carry):
        kb = last - 2 * u
        fill(s1_scr, kb - 1)
        consume(s0_scr, kb)
        fill(s0_scr, kb - 2)
        consume(s1_scr, kb - 1)
        return carry

    lax.fori_loop(0, n // 2, two_steps, 0)

    @pl.when(n % 2 == 1)
    def _():
        fill(s1_scr, last - n)
        consume(s0_scr, last - n + 1)
        consume(s1_scr, last - n)

    @pl.when(n % 2 == 0)
    def _():
        consume(s0_scr, last - n)

    acc_a = acc_scr[0]
    acc_b = acc_scr[1]
    half = FOX_HEAD_DIM
    o_ref[...] = jnp.where(is_a, acc_a / pltpu.roll(acc_a, half, 1),
                           acc_b / pltpu.roll(acc_b, half, 1)).astype(o_ref.dtype)


def fox_attention(q, k, v, FT, lo):
    L, W = q.shape
    H = FT.shape[0]
    tq = FOX_TQ
    grid_spec = pltpu.PrefetchScalarGridSpec(
        num_scalar_prefetch=1,
        grid=(W // LANES, L // tq),
        in_specs=[pl.BlockSpec((tq, LANES), lambda p, i, lo_r: (i, p)),
                  pl.BlockSpec((L, LANES), lambda p, i, lo_r: (0, p)),
                  pl.BlockSpec((L, LANES), lambda p, i, lo_r: (0, p)),
                  pl.BlockSpec((H, L), lambda p, i, lo_r: (0, 0))],
        out_specs=pl.BlockSpec((tq, LANES), lambda p, i, lo_r: (i, p)),
        scratch_shapes=[pltpu.VMEM((2, tq, 1), F32), pltpu.VMEM((2, tq, LANES), F32),
                        pltpu.VMEM((2, tq, FOX_TK), F32), pltpu.VMEM((2, tq, FOX_TK), F32)],
    )
    return pl.pallas_call(
        _fox_kernel,
        grid_spec=grid_spec,
        out_shape=jax.ShapeDtypeStruct((L, W), BF16),
        compiler_params=_cparams("arbitrary", "arbitrary"),
    )(lo, q, k, v, FT)


def fox_first_block(FT, bound2):
    f_first = FT[:, ::FOX_TQ]
    f_last = FT[:, FOX_TK - 1::FOX_TK]
    gap = f_first[:, :, None] - f_last[:, None, :] + 2.0 * bound2
    last = (jnp.arange(f_first.shape[1]) * FOX_TQ) // FOX_TK
    needed = jnp.logical_or(gap >= -152.0, jnp.arange(f_last.shape[1])[None, None, :] >= last[None, :, None])
    return jnp.sum(jnp.cumsum(needed.astype(jnp.int32), axis=-1) == 0, axis=-1).astype(jnp.int32)


def _top2_combine(h, wr):
    T = h.shape[0]
    lane = lax.broadcasted_iota(jnp.int32, (T, LANES), 1)
    h_hi = h.astype(BF16)
    h_lo = (h - h_hi.astype(F32)).astype(BF16)
    w_hi = wr.astype(BF16)
    w_lo = (wr - w_hi.astype(F32)).astype(BF16)
    logits = _dot(h_hi, w_hi) + (_dot(h_hi, w_lo) + _dot(h_lo, w_hi))
    logits = jnp.where(lane < N_EXPERTS, logits, -jnp.inf)
    m1 = jnp.max(logits, axis=-1, keepdims=True)
    i1 = jnp.min(jnp.where(logits == m1, lane, LANES), axis=-1, keepdims=True)
    rest = jnp.where(lane == i1, -jnp.inf, logits)
    m2 = jnp.max(rest, axis=-1, keepdims=True)
    i2 = jnp.min(jnp.where(rest == m2, lane, LANES), axis=-1, keepdims=True)
    e2 = jnp.exp(m2 - m1)
    return jnp.where(lane == i1, 1.0 / (1.0 + e2), 0.0) + jnp.where(lane == i2, e2 / (1.0 + e2), 0.0)


def _merge_kernel(with_router, *refs):
    (x_ref, hm_ref, yn_ref, bonus_ref, g_ref, yssd_ref, yfox_ref, wgate_ref, gb_ref, lnw_ref, lnb_ref,
     prw_ref, pssd_ref, pfox_ref, wout_ref, gm_ref, gain_ref, sc_ref, sh_ref) = refs[:19]
    D = D_MODEL
    hm = hm_ref[...]
    y_rw = (yn_ref[...] * lnw_ref[...] + lnb_ref[...] + bonus_ref[...]) * g_ref[...]
    merged = None
    for b, (y_b, proj_ref) in enumerate(((y_rw, prw_ref), (yssd_ref[...], pssd_ref), (yfox_ref[...], pfox_ref))):
        cols = slice(b * D, (b + 1) * D)
        gate = _sigmoid(_dot(hm, wgate_ref[:, cols]) + gb_ref[:, cols])
        term = gate * _dot_bf16(y_b, proj_ref[...])
        merged = term if merged is None else merged + term
    x_new = x_ref[...] + gm_ref[...] * _dot_bf16(merged, wout_ref[...])
    hf = _norm_mod(x_new, gain_ref[...], sc_ref[...], sh_ref[...])
    if with_router:
        wr_ref, o_ref, h_ref, comb_ref = refs[19:]
        comb_ref[...] = _top2_combine(hf, wr_ref[...])
    else:
        o_ref, h_ref = refs[19:]
    o_ref[...] = x_new
    h_ref[...] = hf.astype(h_ref.dtype)


def merge(x, hm, yn, bonus, g, y_ssd, y_fox, p, router=None, tm=256):
    L, D = x.shape
    full = lambda a: pl.BlockSpec(a.shape, lambda i: (0,) * a.ndim)
    rows = lambda n: pl.BlockSpec((tm, n), lambda i: (i, 0))
    small = [p["w_gate"], p["gate_b"], p["lnx_w"], p["lnx_b"], p["proj_rw"], p["proj_ssd"], p["proj_fox"],
             p["w_out"], p["g_m"], p["norm_gain"], p["sc_f"], p["sh_f"]]
    out_specs = [rows(D), rows(D)]
    out_shape = [jax.ShapeDtypeStruct((L, D), F32), jax.ShapeDtypeStruct((L, D), BF16)]
    if router is not None:
        small.append(router)
        out_specs.append(rows(LANES))
        out_shape.append(jax.ShapeDtypeStruct((L, LANES), F32))
    return pl.pallas_call(
        functools.partial(_merge_kernel, router is not None),
        grid=(L // tm,),
        in_specs=[rows(D), rows(D), rows(RW_WIDTH), rows(RW_WIDTH), rows(RW_WIDTH), rows(SSD_WIDTH),
                  rows(FOX_WIDTH)] + [full(a) for a in small],
        out_specs=out_specs,
        out_shape=out_shape,
        compiler_params=_cparams("arbitrary"),
    )(x, hm, yn, bonus, g, y_ssd, y_fox, *small)


def _ffn_kernel(with_next, *refs):
    h_ref, x_ref, wg_ref, wu_ref, wd_ref, gf_ref = refs[:6]
    j = pl.program_id(1)
    acc_scr = refs[-1]

    @pl.when(j == 0)
    def _():
        acc_scr[...] = jnp.zeros_like(acc_scr)

    h = h_ref[...]
    gte = _dot(h, wg_ref[...])
    up = _dot(h, wu_ref[...])
    act = gte * _sigmoid(gte) * up
    acc_scr[...] += _dot(act.astype(BF16), wd_ref[...])

    @pl.when(j == pl.num_programs(1) - 1)
    def _():
        x_new = x_ref[...] + gf_ref[...] * acc_scr[...]
        if with_next:
            gain_ref, sc_ref, sh_ref, o_ref, hn_ref = refs[6:11]
            hn_ref[...] = _norm_mod(x_new, gain_ref[...], sc_ref[...], sh_ref[...]).astype(hn_ref.dtype)
        else:
            o_ref = refs[6]
        o_ref[...] = x_new


def ffn_dense(h, x, w_gu, w_down, g_f, next_norm=None, tm=512, tf=1408):
    L, D = x.shape
    Fh = w_down.shape[0]
    nf = Fh // tf
    row = pl.BlockSpec((1, D), lambda i, j: (0, 0))
    tile = pl.BlockSpec((tm, D), lambda i, j: (i, 0))
    args = [h, x, w_gu, w_gu, w_down, g_f]
    in_specs = [tile, tile,
                pl.BlockSpec((D, tf), lambda i, j: (0, j)),
                pl.BlockSpec((D, tf), lambda i, j: (0, j + nf)),
                pl.BlockSpec((tf, D), lambda i, j: (j, 0)), row]
    out_specs = [tile]
    out_shape = [jax.ShapeDtypeStruct((L, D), F32)]
    if next_norm is not None:
        args += list(next_norm)
        in_specs += [row, row, row]
        out_specs.append(tile)
        out_shape.append(jax.ShapeDtypeStruct((L, D), BF16))
    out = pl.pallas_call(
        functools.partial(_ffn_kernel, next_norm is not None),
        grid=(L // tm, nf),
        in_specs=in_specs,
        out_specs=out_specs,
        out_shape=out_shape,
        scratch_shapes=[pltpu.VMEM((tm, D), F32)],
        compiler_params=_cparams("arbitrary", "arbitrary"),
    )(*args)
    return out if next_norm is not None else (out[0], None)


def _moe_slots(slot_scr, comb_ref):
    slotval = slot_scr[...]
    comb = comb_ref[...]
    s1 = jnp.max(slotval, axis=-1, keepdims=True)
    first = slotval == s1
    c1 = jnp.sum(jnp.where(first, comb, 0.0), axis=-1, keepdims=True)
    rest = jnp.where(first, -1.0, slotval)
    s2 = jnp.max(rest, axis=-1, keepdims=True)
    c2 = jnp.sum(jnp.where(jnp.logical_and(rest == s2, rest >= 0.0), comb, 0.0), axis=-1, keepdims=True)
    return s1, s2, c1, c2


def _moe_kernel(nsb_ref, off_ref, h_ref, x_ref, comb_ref, wg_ref, wu_ref, wd_ref, gf_ref, o_ref,
                slot_scr, cslot_scr, xy_scr, acc_scr):
    i = pl.program_id(0)
    e = pl.program_id(1)
    j = pl.program_id(2)
    last_j = pl.num_programs(2) - 1
    T = h_ref.shape[0]
    S = xy_scr.shape[0]
    SB = MOE_SB

    @pl.when(jnp.logical_and(e == 0, j == 0))
    def _():
        lane = lax.broadcasted_iota(jnp.int32, (LANES, LANES), 1)
        off = jnp.zeros((LANES, LANES), F32)
        for ex in range(N_EXPERTS):
            off = jnp.where(lane == ex, off_ref[i, ex].astype(F32), off)
        before = _tril(LANES, strict=True).astype(BF16)
        ones = jnp.ones((LANES, LANES), BF16)
        for rb in range(T // LANES):
            rows = slice(rb * LANES, (rb + 1) * LANES)
            sel = comb_ref[rows, :] > 0.0
            sel_b = jnp.where(sel, 1.0, 0.0).astype(BF16)
            slot_scr[rows, :] = jnp.where(sel, _dot(before, sel_b) + off, -1.0)
            off = off + _dot(ones, sel_b)
        s1, s2, c1, c2 = _moe_slots(slot_scr, comb_ref)
        used = off_ref[i, N_EXPERTS - 1] + nsb_ref[i, N_EXPERTS - 1] * SB
        for cb in range(S // MOE_CB):
            chunk = slice(cb * MOE_CB, (cb + 1) * MOE_CB)

            @pl.when(used > cb * MOE_CB)
            def _():
                slot = (lax.broadcasted_iota(jnp.int32, (T, MOE_CB), 1) + cb * MOE_CB).astype(F32)
                w = jnp.where(s1 == slot, c1, 0.0) + jnp.where(s2 == slot, c2, 0.0)
                pt = jnp.where(w > 0.0, 1.0, 0.0).astype(BF16)
                xy_scr[chunk, :] = _dot_tn(pt, h_ref[...]).astype(BF16)
                w_hi = w.astype(BF16)
                w_lo = (w - w_hi.astype(F32)).astype(BF16)
                ones_t = jnp.ones((T, LANES), BF16)
                cslot_scr[chunk, :] = _dot_tn(w_hi, ones_t) + _dot_tn(w_lo, ones_t)

            @pl.when(used <= cb * MOE_CB)
            def _():
                xy_scr[chunk, :] = jnp.zeros((MOE_CB, xy_scr.shape[1]), BF16)
                cslot_scr[chunk, :] = jnp.zeros((MOE_CB, LANES), F32)

    base = off_ref[i, e]
    nb = nsb_ref[i, e]

    def ffn_block(r0, n_rows):
        rows = pl.ds(pl.multiple_of(base + r0, SB), n_rows)
        arows = pl.ds(pl.multiple_of(r0, SB), n_rows)
        xb = xy_scr[rows, :]
        gte = _dot(xb, wg_ref[0])
        up = _dot(xb, wu_ref[0])
        part = _dot((gte * _sigmoid(gte) * up).astype(BF16), wd_ref[0])

        @pl.when(j == 0)
        def _():
            acc_scr[arows, :] = part

        @pl.when(jnp.logical_and(j > 0, j < last_j))
        def _():
            acc_scr[arows, :] += part

        @pl.when(j == last_j)
        def _():
            xy_scr[rows, :] = ((acc_scr[arows, :] + part) * cslot_scr[rows, 0:1]).astype(BF16)

    def quad_body(p, carry):
        ffn_block(p * (4 * SB), 4 * SB)
        return carry

    lax.fori_loop(0, nb // 4, quad_body, 0)
    for rest in range(1, 4):
        @pl.when(nb % 4 == rest)
        def _():
            ffn_block((nb - rest) * SB, rest * SB)

    @pl.when(jnp.logical_and(e == pl.num_programs(1) - 1, j == last_j))
    def _():
        s1, s2, _, _ = _moe_slots(slot_scr, comb_ref)
        total = jnp.zeros((T, xy_scr.shape[1]), F32)
        for cb in range(S // MOE_CB):
            slot = (lax.broadcasted_iota(jnp.int32, (T, MOE_CB), 1) + cb * MOE_CB).astype(F32)
            pt = jnp.where(jnp.logical_or(s1 == slot, s2 == slot), 1.0, 0.0).astype(BF16)
            total = total + _dot(pt, xy_scr[cb * MOE_CB:(cb + 1) * MOE_CB, :])
        o_ref[...] = x_ref[...] + gf_ref[...] * total


def ffn_moe(h, x, comb, w_gu, w_down, g_f, tm=1024, tf=896):
    L, D = x.shape
    E, Fh, _ = w_down.shape
    nf = Fh // tf
    assert nf >= 2 and E == N_EXPERTS
    nt = L // tm
    counts = jnp.sum((comb[:, :E] > 0.0).reshape(nt, tm, E), axis=1, dtype=jnp.int32)
    nsb = (counts + (MOE_SB - 1)) // MOE_SB
    off = (jnp.cumsum(nsb, axis=1) - nsb) * MOE_SB
    idx = lambda f: (lambda i, e, j, n, o: f(i, e, j))
    grid_spec = pltpu.PrefetchScalarGridSpec(
        num_scalar_prefetch=2,
        grid=(nt, E, nf),
        in_specs=[pl.BlockSpec((tm, D), idx(lambda i, e, j: (i, 0))),
                  pl.BlockSpec((tm, D), idx(lambda i, e, j: (i, 0))),
                  pl.BlockSpec((tm, LANES), idx(lambda i, e, j: (i, 0))),
                  pl.BlockSpec((1, D, tf), idx(lambda i, e, j: (e, 0, j))),
                  pl.BlockSpec((1, D, tf), idx(lambda i, e, j: (e, 0, j + nf))),
                  pl.BlockSpec((1, tf, D), idx(lambda i, e, j: (e, j, 0))),
                  pl.BlockSpec((1, D), idx(lambda i, e, j: (0, 0)))],
        out_specs=pl.BlockSpec((tm, D), idx(lambda i, e, j: (i, 0))),
        scratch_shapes=[pltpu.VMEM((tm, LANES), F32),
                        pltpu.VMEM((TOP_K * tm + E * MOE_SB, LANES), F32),
                        pltpu.VMEM((TOP_K * tm + E * MOE_SB, D), BF16),
                        pltpu.VMEM((tm, D), F32)],
    )
    return pl.pallas_call(
        _moe_kernel,
        grid_spec=grid_spec,
        out_shape=jax.ShapeDtypeStruct((L, D), F32),
        compiler_params=_cparams("arbitrary", "arbitrary", "arbitrary"),
    )(nsb, off, h, x, comb, w_gu, w_gu, w_down, g_f)


def _seg_matrix(width, head_dim):
    idx = jnp.arange(width) // head_dim
    return (idx[:, None] == idx[None, :]).astype(BF16)


def _pad_cols(w, n):
    return jnp.pad(w, ((0, 0), (0, n - w.shape[1])))


def _pad_rows(w, n):
    return jnp.pad(w, ((0, n - w.shape[0]), (0, 0)))


def rwkv_branch(h, w_rw, p, v_first):
    r, k, v, lw, a, b, g, bonus = rw_pre(h, w_rw, p, v_first)
    return wkv7(r, lw, k, v, a, b), bonus, g, v


def fox_branch(h, w_fox, p):
    q, k, v, F = fox_pre(h, w_fox, p)
    FT = jnp.transpose(F[:, :FOX_HEADS])
    lo = fox_first_block(FT, p["bound2"])
    return fox_attention(q, k, v, FT, lo)


def kernel(x, c, ada_w, ada_b, norm_mix, norm_ffn, w_in, rw_mu, rw_w0, rw_w_up, rw_a0, rw_a_up, rw_g_up, rw_k_k, rw_k_a, rw_r_k, rw_lnx_w, rw_lnx_b, rw_v0, rw_v_down, rw_v_up, ssd_conv_w, ssd_conv_b, ssd_dt_bias, ssd_a_log, ssd_d, ssd_norm, fox_f_bias, fox_q_gain, fox_k_gain, gate_b, proj_rw, proj_ssd, proj_fox, w_out, ffn_w_gu, ffn_w_down, moe_router, moe_w_gu, moe_w_down):
    depth = w_in.shape[0]
    D = D_MODEL
    xs = x[0]
    row = lambda t: t.reshape(1, -1).astype(F32)
    seg64 = _seg_matrix(RW_WIDTH, RW_HEAD_DIM)
    rw_cols = 3 * RW_WIDTH + RW_DECAY_LORA + RW_AAA_LORA + RW_GATE_LORA
    ssd_cols = SSD_WIDTH + SSD_XBC + SSD_HEADS
    fox_cols = 3 * FOX_WIDTH + FOX_HEADS
    expand = (jnp.arange(LANES)[:, None] == (jnp.arange(SSD_WIDTH) // SSD_HEAD_DIM)[None, :]).astype(BF16)
    v_first = None
    mods = []
    for l in range(depth):
        mod = adaln_mod(c, ada_w[l], ada_b[l])
        mods.append([mod[:, i * D:(i + 1) * D] for i in range(6)])
    h = None
    for l in range(depth):
        sh_m, sc_m, g_m, sh_f, sc_f, g_f = mods[l]
        if h is None:
            h = norm_mod(xs, row(norm_mix[l]), sc_m, sh_m)

        wl = w_in[l]
        o = 0
        w_r = wl[:, o:o + rw_cols]; o += rw_cols
        w_s = wl[:, o:o + ssd_cols]; o += ssd_cols
        w_f = wl[:, o:o + fox_cols]; o += fox_cols
        w_g = wl[:, o:]
        W3 = 3 * RW_WIDTH
        o_a = W3 + RW_DECAY_LORA
        o_g = o_a + RW_AAA_LORA

        def rw_layout(t, hv):
            parts = [t[:, :W3], _pad_cols(t[:, W3:o_a], LANES), _pad_cols(t[:, o_a:o_g], LANES),
                     _pad_cols(t[:, o_g:], 2 * LANES), _pad_cols(hv, LANES)]
            return jnp.concatenate(parts, axis=1)

        if l == 0:
            hv_w = jnp.zeros((D, RW_VRES_LORA), F32)
        else:
            hv_w = rw_v_down[l - 1]
        w_rw = rw_layout(w_r, hv_w).astype(BF16)
        mu = rw_layout(rw_mu[l].reshape(1, -1), jnp.zeros((1, RW_VRES_LORA), F32))
        rw_p = dict(mu=mu, w0=row(rw_w0[l]), w_up=_pad_rows(rw_w_up[l], LANES).astype(BF16), a0=row(rw_a0[l]),
                    a_up=_pad_rows(rw_a_up[l], LANES).astype(BF16),
                    g_up=_pad_rows(rw_g_up[l], 2 * LANES).astype(BF16),
                    k_k=row(rw_k_k[l]), k_a=row(rw_k_a[l]), r_k=row(rw_r_k[l]), seg=seg64)
        if l > 0:
            rw_p.update(v0=row(rw_v0[l - 1]), v_up=_pad_rows(rw_v_up[l - 1], LANES).astype(BF16))
        yn, bonus, g_rw, v_cur = rwkv_branch(h, w_rw, rw_p, v_first)
        if l == 0:
            v_first = v_cur

        w_ssd = jnp.concatenate([w_s[:, SSD_WIDTH:SSD_WIDTH + SSD_XBC], w_s[:, :SSD_WIDTH],
                                 _pad_cols(w_s[:, SSD_WIDTH + SSD_XBC:], LANES)], axis=1).astype(BF16)
        A = -jnp.exp(ssd_a_log[l].astype(F32))
        ssd_p = dict(conv_w=ssd_conv_w[l], conv_b=row(ssd_conv_b[l]),
                     dt_bias=_pad_cols(row(ssd_dt_bias[l]), LANES),
                     A=_pad_cols(row(A), LANES), expand=expand,
                     d_skip=row(jnp.repeat(ssd_d[l], SSD_HEAD_DIM)), norm_w=row(ssd_norm[l]))
        y_ssd = ssd_mixer(h, w_ssd, ssd_p)

        fb = fox_f_bias[l]
        hid = jnp.arange(FOX_HEADS)
        ahead = jnp.logical_or(fb[None, :] < fb[:, None],
                               jnp.logical_and(fb[None, :] == fb[:, None], hid[None, :] < hid[:, None]))
        rank = jnp.sum(ahead, axis=1)
        head_order = jnp.sum(jnp.where(rank[None, :] == hid[:, None], hid[None, :], 0), axis=1)
        by_head = lambda t: jnp.take(t.reshape(D, FOX_HEADS, FOX_HEAD_DIM), head_order, axis=1).reshape(D, FOX_WIDTH)
        w_fox = jnp.concatenate(
            [by_head(w_f[:, i * FOX_WIDTH:(i + 1) * FOX_WIDTH]) for i in range(3)]
            + [_pad_cols(jnp.take(w_f[:, 3 * FOX_WIDTH:], head_order, axis=1), LANES)], axis=1).astype(BF16)
        f_bias_sorted = jnp.take(fox_f_bias[l], head_order)
        proj_fox_sorted = jnp.take(proj_fox[l].reshape(FOX_HEADS, FOX_HEAD_DIM, D), head_order,
                                   axis=0).reshape(FOX_WIDTH, D)
        bound2 = (1.02 * FOX_HEAD_DIM ** 0.5 * LOG2E) * jnp.max(jnp.abs(fox_q_gain[l])) * jnp.max(jnp.abs(fox_k_gain[l]))
        fox_p = dict(f_bias=_pad_cols(row(f_bias_sorted), LANES),
                     q_gain=row(jnp.tile(fox_q_gain[l], FOX_HEADS)) * (FOX_HEAD_DIM ** -0.5 * LOG2E),
                     k_gain=row(jnp.tile(fox_k_gain[l], FOX_HEADS)), seg=seg64, bound2=bound2)
        y_fox = fox_branch(h, w_fox, fox_p)

        mp = dict(w_gate=w_g.astype(BF16), gate_b=row(gate_b[l]), lnx_w=row(rw_lnx_w[l]), lnx_b=row(rw_lnx_b[l]),
                  proj_rw=proj_rw[l].astype(BF16), proj_ssd=proj_ssd[l].astype(BF16),
                  proj_fox=proj_fox_sorted.astype(BF16), w_out=w_out[l].astype(BF16), g_m=g_m,
                  norm_gain=row(norm_ffn[l]), sc_f=sc_f, sh_f=sh_f)

        if l % 2 == 0:
            xs, hf = merge(xs, h, yn, bonus, g_rw, y_ssd, y_fox, mp)
            next_norm = None
            if l + 1 < depth:
                next_norm = (row(norm_mix[l + 1]), mods[l + 1][1], mods[l + 1][0])
            xs, h = ffn_dense(hf, xs, ffn_w_gu[l // 2].astype(BF16), ffn_w_down[l // 2].astype(BF16), g_f,
                              next_norm)
        else:
            xs, hf, comb = merge(xs, h, yn, bonus, g_rw, y_ssd, y_fox, mp,
                                 router=_pad_cols(moe_router[l // 2], LANES))
            xs = ffn_moe(hf, xs, comb, moe_w_gu[l // 2].astype(BF16), moe_w_down[l // 2].astype(BF16), g_f)
            h = None
    return xs[None]
```

```python
import functools
import math

import jax
import jax.numpy as jnp
from jax import lax
from jax.experimental import pallas as pl
from jax.experimental.pallas import tpu as pltpu

F32 = jnp.float32
BF16 = jnp.bfloat16
HIGHEST = lax.Precision.HIGHEST

D_MODEL = 1024
LANES = 128
SUBLANES = 8

RW_HEADS = 8
RW_HEAD_DIM = 64
RW_WIDTH = RW_HEADS * RW_HEAD_DIM
RW_DECAY_LORA = 64
RW_AAA_LORA = 64
RW_VRES_LORA = 32
RW_GATE_LORA = 160
GN_EPS = 64e-5
RW_OFF_WLO = 3 * RW_WIDTH
RW_OFF_ALO = RW_OFF_WLO + LANES
RW_OFF_GLO = RW_OFF_ALO + LANES
RW_OFF_HV = RW_OFF_GLO + 2 * LANES
RW_PAD_COLS = RW_OFF_HV + LANES
RW_CHUNK = 64

SSD_HEADS = 16
SSD_HEAD_DIM = 64
SSD_WIDTH = SSD_HEADS * SSD_HEAD_DIM
SSD_GROUPS = 4
SSD_STATE = 128
SSD_CONV = 4
SSD_CHUNK = 128
SSD_XBC = SSD_WIDTH + 2 * SSD_GROUPS * SSD_STATE
SSD_GROUP_WIDTH = SSD_WIDTH // SSD_GROUPS

FOX_HEADS = 8
FOX_HEAD_DIM = 64
FOX_WIDTH = FOX_HEADS * FOX_HEAD_DIM
FOX_TQ = 512
FOX_TK = 512
LOG2E = 1.4426950408889634

FFN_DENSE = 2816
N_EXPERTS = 8
TOP_K = 2
FFN_EXPERT = 3584
MOE_SB = 128
MOE_CB = 512
EPS = 1e-6

VMEM_LIMIT = 56 * 1024 * 1024


def _cparams(*sem):
    return pltpu.CompilerParams(dimension_semantics=sem, vmem_limit_bytes=VMEM_LIMIT)


def _sigmoid(x):
    return 1.0 / (1.0 + jnp.exp(-x))


def _softplus(x):
    return jnp.maximum(x, 0.0) + jnp.log(1.0 + jnp.exp(-jnp.abs(x)))


def _dot(a, b):
    return jnp.dot(a, b, preferred_element_type=F32)


def _dot_bf16(a, b):
    return jnp.dot(a.astype(BF16), b.astype(BF16), preferred_element_type=F32)


def _dot_hi(a, b):
    return jnp.dot(a, b, precision=HIGHEST, preferred_element_type=F32)


def _dot_nt(a, b, precision=None):
    return lax.dot_general(a, b, (((1,), (1,)), ((), ())), precision=precision, preferred_element_type=F32)


def _dot_tn(a, b, precision=None):
    return lax.dot_general(a, b, (((0,), (0,)), ((), ())), precision=precision, preferred_element_type=F32)


def _split_dot(x, ones_bf16):
    return _dot(x.astype(BF16), ones_bf16)


def _split3(x):
    hi = x.astype(BF16)
    r1 = x - hi.astype(F32)
    mid = r1.astype(BF16)
    return hi, mid, (r1 - mid.astype(F32)).astype(BF16)


def _tril(n, strict=False):
    r = lax.broadcasted_iota(jnp.int32, (n, n), 0)
    c = lax.broadcasted_iota(jnp.int32, (n, n), 1)
    return (r > c) if strict else (r >= c)


def _mod_kernel(c_ref, w_ref, b_ref, o_ref):
    c = c_ref[...]
    o_ref[...] = _dot_hi(c * _sigmoid(c), w_ref[...]) + b_ref[...]


def adaln_mod(c, w, b):
    d, n = w.shape
    tn = 1024
    c8 = jnp.broadcast_to(c, (SUBLANES, d))
    out = pl.pallas_call(
        _mod_kernel,
        grid=(n // tn,),
        in_specs=[pl.BlockSpec((SUBLANES, d), lambda j: (0, 0)),
                  pl.BlockSpec((d, tn), lambda j: (0, j)),
                  pl.BlockSpec((1, tn), lambda j: (0, j))],
        out_specs=pl.BlockSpec((SUBLANES, tn), lambda j: (0, j)),
        out_shape=jax.ShapeDtypeStruct((SUBLANES, n), F32),
        compiler_params=_cparams("arbitrary"),
    )(c8, w, b.reshape(1, n))
    return out[:1]


def _norm_mod(x, gain, sc, sh):
    y = x * lax.rsqrt(jnp.mean(x * x, axis=-1, keepdims=True) + EPS)
    return y * gain * (1.0 + sc) + sh


def _norm_kernel(x_ref, gain_ref, sc_ref, sh_ref, h_ref):
    h_ref[...] = _norm_mod(x_ref[...], gain_ref[...], sc_ref[...], sh_ref[...]).astype(h_ref.dtype)


def norm_mod(x, gain, sc, sh, tm=512):
    L, d = x.shape
    row = pl.BlockSpec((1, d), lambda i: (0, 0))
    return pl.pallas_call(
        _norm_kernel,
        grid=(L // tm,),
        in_specs=[pl.BlockSpec((tm, d), lambda i: (i, 0)), row, row, row],
        out_specs=pl.BlockSpec((tm, d), lambda i: (i, 0)),
        out_shape=jax.ShapeDtypeStruct((L, d), BF16),
        compiler_params=_cparams("arbitrary"),
    )(x, gain, sc, sh)


def _rw_pre_kernel(has_vres, *refs):
    if has_vres:
        (h_ref, w_ref, mu_ref, w0_ref, wup_ref, a0_ref, aup_ref, gup_ref, kk_ref, ka_ref, rk_ref, seg_ref,
         vfirst_ref, v0_ref, vup_ref,
         r_out, k_out, v_out, lw_out, a_out, b_out, g_out, bonus_out, ext_scr) = refs
    else:
        (h_ref, w_ref, mu_ref, w0_ref, wup_ref, a0_ref, aup_ref, gup_ref, kk_ref, ka_ref, rk_ref, seg_ref,
         r_out, k_out, v_out, lw_out, a_out, b_out, g_out, bonus_out, ext_scr) = refs
    T = h_ref.shape[0]

    @pl.when(pl.program_id(0) == 0)
    def _():
        ext_scr[0:SUBLANES, :] = jnp.zeros((SUBLANES, ext_scr.shape[1]), F32)

    cur = _dot(h_ref[...], w_ref[...])
    ext_scr[SUBLANES:SUBLANES + T, :] = cur
    prev = ext_scr[SUBLANES - 1:SUBLANES - 1 + T, :]
    ext_scr[0:SUBLANES, :] = cur[T - SUBLANES:T, :]
    s = cur + (prev - cur) * mu_ref[...]

    W = RW_WIDTH
    r = s[:, 0:W]
    k = s[:, W:2 * W]
    v = s[:, 2 * W:3 * W]
    w_lo = s[:, RW_OFF_WLO:RW_OFF_WLO + LANES]
    a_lo = s[:, RW_OFF_ALO:RW_OFF_ALO + LANES]
    g_lo = s[:, RW_OFF_GLO:RW_OFF_GLO + 2 * LANES]
    seg = seg_ref[...]

    wlog = -_softplus(-(w0_ref[...] + _dot_bf16(jnp.tanh(w_lo), wup_ref[...]))) - 0.5
    a = _sigmoid(a0_ref[...] + _dot_bf16(a_lo, aup_ref[...]))
    g = _dot_bf16(_sigmoid(g_lo), gup_ref[...])
    if has_vres:
        hv = s[:, RW_OFF_HV:RW_OFF_HV + LANES]
        v = v + (vfirst_ref[...].astype(F32) - v) * _sigmoid(v0_ref[...] + _dot_bf16(hv, vup_ref[...]))
    kk = k * kk_ref[...]
    kk = kk / jnp.maximum(jnp.sqrt(_split_dot(kk * kk, seg)), 1e-12)
    k = k * (1.0 + (a - 1.0) * ka_ref[...])
    r_out[...] = r.astype(r_out.dtype)
    k_out[...] = k.astype(k_out.dtype)
    v_out[...] = v.astype(v_out.dtype)
    lw_out[...] = -jnp.exp(wlog)
    a_out[...] = (-kk).astype(a_out.dtype)
    b_out[...] = (kk * a).astype(b_out.dtype)
    g_out[...] = g.astype(g_out.dtype)
    bonus_out[...] = (_split_dot(r * k * rk_ref[...], seg) * v).astype(bonus_out.dtype)


def rw_pre(h, w_rw, p, v_first, tm=512):
    L, D = h.shape
    has_vres = v_first is not None
    full = lambda a: pl.BlockSpec(a.shape, lambda i: (0,) * a.ndim)
    rows = lambda n: pl.BlockSpec((tm, n), lambda i: (i, 0))
    args = [h, w_rw, p["mu"], p["w0"], p["w_up"], p["a0"], p["a_up"], p["g_up"], p["k_k"], p["k_a"], p["r_k"],
            p["seg"]]
    specs = [rows(D)] + [full(a) for a in args[1:]]
    if has_vres:
        args += [v_first, p["v0"], p["v_up"]]
        specs += [rows(RW_WIDTH), full(p["v0"]), full(p["v_up"])]
    out_dtypes = [BF16, BF16, BF16, F32, BF16, BF16, BF16, BF16]
    return pl.pallas_call(
        functools.partial(_rw_pre_kernel, has_vres),
        grid=(L // tm,),
        in_specs=specs,
        out_specs=[rows(RW_WIDTH)] * 8,
        out_shape=[jax.ShapeDtypeStruct((L, RW_WIDTH), dt) for dt in out_dtypes],
        scratch_shapes=[pltpu.VMEM((tm + SUBLANES, RW_PAD_COLS), F32)],
        compiler_params=_cparams("arbitrary"),
    )(*args)


def _wkv_kernel(r_ref, lw_ref, k_ref, v_ref, a_ref, b_ref, y_ref, h_scr):
    C = RW_CHUNK
    C2 = 2 * C
    n_chunks = r_ref.shape[0] // C

    @pl.when(pl.program_id(1) == 0)
    def _():
        h_scr[...] = jnp.zeros_like(h_scr)

    is_a = lax.broadcasted_iota(jnp.int32, (C, LANES), 1) < RW_HEAD_DIM
    ri = lax.broadcasted_iota(jnp.int32, (C2, C2), 0)
    ci = lax.broadcasted_iota(jnp.int32, (C2, C2), 1)
    same = (ri >= C) == (ci >= C)
    strict = jnp.logical_and(same, ri > ci)
    incl = jnp.logical_and(same, ri >= ci)
    eye2 = (ri == ci).astype(F32)
    tri_c = _tril(C).astype(BF16)

    def two(x):
        return jnp.concatenate([jnp.where(is_a, x, 0.0), jnp.where(is_a, 0.0, x)], axis=0)

    chunks = range(n_chunks)
    at2, rt2, bh2, kh2, v2, n_ab, a_ak, m_rb, m_rk, p_end = ([] for _ in range(10))
    for c in chunks:
        sl = pl.ds(c * C, C)
        lw = lw_ref[sl, :]
        lw_hi = lw.astype(BF16)
        lw_lo = (lw - lw_hi.astype(F32)).astype(BF16)
        cs = _dot(tri_c, lw_hi) + _dot(tri_c, lw_lo)
        cs_end = cs[C - 1:C, :]
        e_neg = jnp.exp(-cs)
        e_end = jnp.exp(cs_end - cs)
        a = a_ref[sl, :].astype(F32)
        b = b_ref[sl, :].astype(F32)
        k = k_ref[sl, :].astype(F32)
        at2.append(two(a * jnp.exp(cs - lw)))
        rt2.append(two(r_ref[sl, :].astype(F32) * jnp.exp(cs)))
        bh2.append(two(b * e_end))
        kh2.append(two(k * e_end))
        v2.append(two(v_ref[sl, :].astype(F32)))
        p_end.append(jnp.exp(cs_end))
        quad = _dot_nt(jnp.concatenate([at2[c], rt2[c]], axis=0).astype(BF16),
                       jnp.concatenate([two(b * e_neg), two(k * e_neg)], axis=0).astype(BF16))
        n_ab.append(jnp.where(strict, quad[0:C2, 0:C2], 0.0))
        a_ak.append(jnp.where(strict, quad[0:C2, C2:], 0.0))
        m_rb.append(jnp.where(incl, quad[C2:, 0:C2], 0.0))
        m_rk.append(jnp.where(incl, quad[C2:, C2:], 0.0))
    t_inv = [eye2 + n for n in n_ab]
    n_pow = n_ab
    n_pow = [_dot_bf16(n, n) for n in n_pow]
    for _ in range(int(math.log2(C)) - 2):
        both = [_dot_bf16(jnp.concatenate([n, t], axis=0), n) for n, t in zip(n_pow, t_inv)]
        t_inv = [t + b[C2:, :] for t, b in zip(t_inv, both)]
        n_pow = [b[:C2, :] for b in both]
    t_inv = [t + _dot_bf16(t, n) for t, n in zip(t_inv, n_pow)]
    akv = [_dot_bf16(a_ak[c], v2[c]) for c in chunks]
    au = [_dot_bf16(t_inv[c], jnp.concatenate([at2[c], akv[c]], axis=1)) for c in chunks]
    mau = [_dot_bf16(m_rb[c], au[c]) for c in chunks]
    mkv = [_dot_bf16(m_rk[c], v2[c]) for c in chunks]
    gu = [_dot_tn(bh2[c].astype(BF16), au[c].astype(BF16)) for c in chunks]
    kv = [_dot_tn(kh2[c].astype(BF16), v2[c].astype(BF16)) for c in chunks]
    inv_n = 1.0 / RW_HEAD_DIM
    h = h_scr[...]
    for c in chunks:
        rh2 = rt2[c] + mau[c][:, :LANES]
        g_mat = eye2 * p_end[c] + gu[c][:, :LANES]
        yh = _dot_bf16(jnp.concatenate([rh2, g_mat], axis=0), h)
        h = yh[C2:, :] + gu[c][:, LANES:] + kv[c]
        y2 = yh[:C2, :] + mau[c][:, LANES:] + mkv[c]
        y = y2[:C, :] + y2[C:, :]
        s_all = jnp.sum(y, axis=-1, keepdims=True)
        s_a = jnp.sum(jnp.where(is_a, y, 0.0), axis=-1, keepdims=True)
        yc = y - jnp.where(is_a, s_a, s_all - s_a) * inv_n
        sq = yc * yc
        q_all = jnp.sum(sq, axis=-1, keepdims=True)
        q_a = jnp.sum(jnp.where(is_a, sq, 0.0), axis=-1, keepdims=True)
        yn = yc * lax.rsqrt(jnp.where(is_a, q_a, q_all - q_a) * inv_n + GN_EPS)
        y_ref[pl.ds(c * C, C), :] = yn.astype(y_ref.dtype)
    h_scr[...] = h


def wkv7(r, lw, k, v, a, b, rows=4096):
    L, W = r.shape
    spec = pl.BlockSpec((rows, LANES), lambda p, i: (i, p))
    return pl.pallas_call(
        _wkv_kernel,
        grid=(W // LANES, L // rows),
        in_specs=[spec] * 6,
        out_specs=spec,
        out_shape=jax.ShapeDtypeStruct((L, W), BF16),
        scratch_shapes=[pltpu.VMEM((LANES, LANES), F32)],
        compiler_params=_cparams("arbitrary", "arbitrary"),
    )(r, lw, k, v, a, b)


def _ssd_kernel(h_ref, w_ref, cw_ref, cb_ref, bias_ref, A_ref, exp_ref, dskip_ref, nw_ref, o_ref,
                ext_scr, xbc_scr, s_scr):
    Q = SSD_CHUNK
    GW = SSD_GROUP_WIDTH
    NS = SSD_STATE
    R = h_ref.shape[0]

    @pl.when(pl.program_id(0) == 0)
    def _():
        s_scr[...] = jnp.zeros_like(s_scr)
        ext_scr[0:SUBLANES, :] = jnp.zeros((SUBLANES, ext_scr.shape[1]), F32)

    cols = _dot(h_ref[...], w_ref[...])
    ext_scr[SUBLANES:SUBLANES + R, :] = cols[:, 0:SSD_XBC]
    acc = cols[:, 0:SSD_XBC] * cw_ref[SSD_CONV - 1:SSD_CONV, :] + cb_ref[...]
    for j in range(1, SSD_CONV):
        acc = acc + ext_scr[SUBLANES - j:SUBLANES - j + R, :] * cw_ref[SSD_CONV - 1 - j:SSD_CONV - j, :]
    ext_scr[0:SUBLANES, :] = ext_scr[R:R + SUBLANES, :]
    xbc_scr[...] = acc * _sigmoid(acc)

    incl = _tril(Q)
    tri = incl.astype(BF16)
    triu = jnp.logical_not(_tril(Q, strict=True)).astype(BF16)
    expand = exp_ref[...]
    first_half = lax.broadcasted_iota(jnp.int32, (Q, LANES), 1) < SSD_HEAD_DIM
    heads_per_group = SSD_HEADS // SSD_GROUPS
    for c in range(R // Q):
        rows = slice(c * Q, (c + 1) * Q)
        xs = xbc_scr[rows, 0:SSD_WIDTH]
        z = cols[rows, SSD_XBC:SSD_XBC + SSD_WIDTH]
        dt = _softplus(cols[rows, SSD_XBC + SSD_WIDTH:] + bias_ref[...])
        a3 = _split3(dt * A_ref[...])
        a_cum = sum(_dot(tri, t) for t in a3)
        a_cumT = sum(_dot_tn(t, triu) for t in a3)
        a_end = a_cum[Q - 1:Q, :]
        dt_x = _dot(dt.astype(BF16), expand)
        eac_x = _dot(jnp.exp(a_cum).astype(BF16), expand)
        dte_x = _dot(jnp.exp(a_end - a_cum).astype(BF16), expand)
        cd_x = sum(_dot(t, expand) for t in _split3(jnp.broadcast_to(jnp.exp(a_end), (SUBLANES, LANES))))[0:1, :]
        xdt = xs * dt_x
        for g in range(SSD_GROUPS):
            Bg = xbc_scr[rows, SSD_WIDTH + g * NS:SSD_WIDTH + (g + 1) * NS].astype(BF16)
            Cg = xbc_scr[rows, SSD_WIDTH + (SSD_GROUPS + g) * NS:SSD_WIDTH + (SSD_GROUPS + g + 1) * NS].astype(BF16)
            cb = _dot_nt(Cg, Bg)
            gs = slice(g * GW, (g + 1) * GW)
            s_prev = s_scr[:, gs]
            y_g = _dot_bf16(Cg, s_prev) * eac_x[:, gs]
            pieces = []
            for pr in range(heads_per_group // 2):
                ps = slice(g * GW + pr * LANES, g * GW + (pr + 1) * LANES)
                xdt_p = xdt[:, ps]
                acc_p = None
                for e in range(2):
                    hd = g * heads_per_group + pr * 2 + e
                    seg = a_cum[:, hd:hd + 1] - a_cumT[hd:hd + 1, :]
                    m = cb * jnp.exp(jnp.where(incl, seg, -jnp.inf))
                    xm = jnp.where(first_half if e == 0 else jnp.logical_not(first_half), xdt_p, 0.0)
                    t = _dot_bf16(m, xm)
                    acc_p = t if acc_p is None else acc_p + t
                pieces.append(acc_p)
            y_g = y_g + jnp.concatenate(pieces, axis=1)
            s_scr[:, gs] = s_prev * cd_x[:, gs] +_dot_tn(Bg, (dte_x[:, gs] * xdt[:, gs]).astype(BF16))
            y_g = y_g + dskip_ref[:, gs] * xs[:, gs]
            zg = z[:, gs]
            y_g = y_g * (zg * _sigmoid(zg))
            y_g = y_g * lax.rsqrt(jnp.mean(y_g * y_g, axis=-1, keepdims=True) + EPS)
            o_ref[rows, gs] = (y_g * nw_ref[:, gs]).astype(o_ref.dtype)


def ssd_mixer(h, w_ssd, p, rows=512):
    L, D = h.shape
    N = w_ssd.shape[1]
    full = lambda a: pl.BlockSpec(a.shape, lambda i: (0,) * a.ndim)
    small = [p["conv_w"], p["conv_b"], p["dt_bias"], p["A"], p["expand"], p["d_skip"], p["norm_w"]]
    return pl.pallas_call(
        _ssd_kernel,
        grid=(L // rows,),
        in_specs=[pl.BlockSpec((rows, D), lambda i: (i, 0)), full(w_ssd)] + [full(a) for a in small],
        out_specs=pl.BlockSpec((rows, SSD_WIDTH), lambda i: (i, 0)),
        out_shape=jax.ShapeDtypeStruct((L, SSD_WIDTH), BF16),
        scratch_shapes=[pltpu.VMEM((rows + SUBLANES, SSD_XBC), F32), pltpu.VMEM((rows, SSD_XBC), F32),
                        pltpu.VMEM((SSD_STATE, SSD_WIDTH), F32)],
        compiler_params=_cparams("arbitrary"),
    )(h, w_ssd, *small)


def _fox_pre_kernel(h_ref, w_ref, fb_ref, qg_ref, kg_ref, seg_ref, q_out, k_out, v_out, f_out, carry_scr):
    T = h_ref.shape[0]
    W = FOX_WIDTH

    @pl.when(pl.program_id(0) == 0)
    def _():
        carry_scr[...] = jnp.zeros_like(carry_scr)

    cols = _dot(h_ref[...], w_ref[...])
    seg = seg_ref[...]
    q = cols[:, 0:W]
    k = cols[:, W:2 * W]
    inv_d = 1.0 / FOX_HEAD_DIM
    qn = q * lax.rsqrt(_split_dot(q * q, seg) * inv_d + EPS) * qg_ref[...]
    kn = k * lax.rsqrt(_split_dot(k * k, seg) * inv_d + EPS) * kg_ref[...]
    q_out[...] = qn.astype(q_out.dtype)
    k_out[...] = kn.astype(k_out.dtype)
    v_out[...] = cols[:, 2 * W:3 * W].astype(v_out.dtype)
    f = cols[:, 3 * W:3 * W + LANES] + fb_ref[...]
    ls = -_softplus(-f)
    cum = _dot_hi(_tril(T).astype(F32), ls) + carry_scr[0:1, :]
    f_out[...] = cum * LOG2E
    carry_scr[...] = jnp.broadcast_to(cum[T - 1:T, :], carry_scr.shape)


def fox_pre(h, w_fox, p, tm=512):
    L, D = h.shape
    full = lambda a: pl.BlockSpec(a.shape, lambda i: (0,) * a.ndim)
    rows = lambda n: pl.BlockSpec((tm, n), lambda i: (i, 0))
    small = [w_fox, p["f_bias"], p["q_gain"], p["k_gain"], p["seg"]]
    return pl.pallas_call(
        _fox_pre_kernel,
        grid=(L // tm,),
        in_specs=[rows(D)] + [full(a) for a in small],
        out_specs=[rows(FOX_WIDTH)] * 3 + [rows(LANES)],
        out_shape=[jax.ShapeDtypeStruct((L, FOX_WIDTH), BF16)] * 3 + [jax.ShapeDtypeStruct((L, LANES), F32)],
        scratch_shapes=[pltpu.VMEM((SUBLANES, LANES), F32)],
        compiler_params=_cparams("arbitrary"),
    )(h, *small)


def _fox_kernel(lo_ref, q_ref, k_ref, v_ref, fk_ref, o_ref, m_scr, acc_scr, s0_scr, s1_scr):
    tq = q_ref.shape[0]
    tk = FOX_TK
    pair = pl.program_id(0)
    qi = pl.program_id(1)
    is_a = lax.broadcasted_iota(jnp.int32, (tq, LANES), 1) < FOX_HEAD_DIM
    q = q_ref[...]
    zero = jnp.zeros_like(q)
    qs = (jnp.where(is_a, q, zero), jnp.where(is_a, zero, q))
    m_scr[...] = jnp.full(m_scr.shape, -jnp.inf, F32)
    key_is_a = lax.broadcasted_iota(jnp.int32, (tk, LANES), 1) < FOX_HEAD_DIM
    acc_scr[...] = jnp.zeros_like(acc_scr)
    rel = lax.broadcasted_iota(jnp.int32, (tq, tk), 0) - lax.broadcasted_iota(jnp.int32, (tq, tk), 1)
    last = (qi * tq) // tk

    def fill(s_ref, kb, diagonal=False):
        ks = pl.ds(pl.multiple_of(kb * tk, tk), tk)
        k_blk = k_ref[ks, :]
        for e in range(2):
            s = _dot_nt(qs[e], k_blk) - fk_ref[pl.ds(2 * pair + e, 1), ks]
            if diagonal:
                s = jnp.where(rel >= kb * tk - qi * tq, s, -jnp.inf)
            s_ref[e] = s

    def consume(s_ref, kb):
        v_blk = v_ref[pl.ds(pl.multiple_of(kb * tk, tk), tk), :]
        one = jnp.ones_like(v_blk)
        v_one = (jnp.where(key_is_a, v_blk, one), jnp.where(key_is_a, one, v_blk))
        for e in range(2):
            s = s_ref[e]
            m_old = m_scr[e]
            m_new = jnp.maximum(m_old, jnp.max(s, axis=-1, keepdims=True))
            pexp = jnp.exp2(s - m_new)
            acc_scr[e] = jnp.exp2(m_old - m_new) * acc_scr[e] + _dot(pexp.astype(BF16), v_one[e])
            m_scr[e] = m_new

    n = last - jnp.minimum(lo_ref[2 * pair, qi], lo_ref[2 * pair + 1, qi])
    fill(s0_scr, last, diagonal=True)

    def two_steps(u, carry):
        kb = last - 2 * u
        fill(s1_scr, kb - 1)
        consume(s0_scr, kb)
        fill(s0_scr, kb - 2)
        consume(s1_scr, kb - 1)
        return carry

    lax.fori_loop(0, n // 2, two_steps, 0)

    @pl.when(n % 2 == 1)
    def _():
        fill(s1_scr, last - n)
        consume(s0_scr, last - n + 1)
        consume(s1_scr, last - n)

    @pl.when(n % 2 == 0)
    def _():
        consume(s0_scr, last - n)

    acc_a = acc_scr[0]
    acc_b = acc_scr[1]
    half = FOX_HEAD_DIM
    o_ref[...] = jnp.where(is_a, acc_a / pltpu.roll(acc_a, half, 1),
                           acc_b / pltpu.roll(acc_b, half, 1)).astype(o_ref.dtype)


def fox_attention(q, k, v, FT, lo):
    L, W = q.shape
    H = FT.shape[0]
    tq = FOX_TQ
    grid_spec = pltpu.PrefetchScalarGridSpec(
        num_scalar_prefetch=1,
        grid=(W // LANES, L // tq),
        in_specs=[pl.BlockSpec((tq, LANES), lambda p, i, lo_r: (i, p)),
                  pl.BlockSpec((L, LANES), lambda p, i, lo_r: (0, p)),
                  pl.BlockSpec((L, LANES), lambda p, i, lo_r: (0, p)),
                  pl.BlockSpec((H, L), lambda p, i, lo_r: (0, 0))],
        out_specs=pl.BlockSpec((tq, LANES), lambda p, i, lo_r: (i, p)),
        scratch_shapes=[pltpu.VMEM((2, tq, 1), F32), pltpu.VMEM((2, tq, LANES), F32),
                        pltpu.VMEM((2, tq, FOX_TK), F32), pltpu.VMEM((2, tq, FOX_TK), F32)],
    )
    return pl.pallas_call(
        _fox_kernel,
        grid_spec=grid_spec,
        out_shape=jax.ShapeDtypeStruct((L, W), BF16),
        compiler_params=_cparams("arbitrary", "arbitrary"),
    )(lo, q, k, v, FT)


def fox_first_block(FT, bound2):
    f_first = FT[:, ::FOX_TQ]
    f_last = FT[:, FOX_TK - 1::FOX_TK]
    gap = f_first[:, :, None] - f_last[:, None, :] + 2.0 * bound2
    last = (jnp.arange(f_first.shape[1]) * FOX_TQ) // FOX_TK
    needed = jnp.logical_or(gap >= -152.0, jnp.arange(f_last.shape[1])[None, None, :] >= last[None, :, None])
    return jnp.sum(jnp.cumsum(needed.astype(jnp.int32), axis=-1) == 0, axis=-1).astype(jnp.int32)


def _top2_combine(h, wr):
    T = h.shape[0]
    lane = lax.broadcasted_iota(jnp.int32, (T, LANES), 1)
    h_hi = h.astype(BF16)
    h_lo = (h - h_hi.astype(F32)).astype(BF16)
    w_hi = wr.astype(BF16)
    w_lo = (wr - w_hi.astype(F32)).astype(BF16)
    logits = _dot(h_hi, w_hi) + (_dot(h_hi, w_lo) + _dot(h_lo, w_hi))
    logits = jnp.where(lane < N_EXPERTS, logits, -jnp.inf)
    m1 = jnp.max(logits, axis=-1, keepdims=True)
    i1 = jnp.min(jnp.where(logits == m1, lane, LANES), axis=-1, keepdims=True)
    rest = jnp.where(lane == i1, -jnp.inf, logits)
    m2 = jnp.max(rest, axis=-1, keepdims=True)
    i2 = jnp.min(jnp.where(rest == m2, lane, LANES), axis=-1, keepdims=True)
    e2 = jnp.exp(m2 - m1)
    return jnp.where(lane == i1, 1.0 / (1.0 + e2), 0.0) + jnp.where(lane == i2, e2 / (1.0 + e2), 0.0)


def _merge_kernel(with_router, *refs):
    (x_ref, hm_ref, yn_ref, bonus_ref, g_ref, yssd_ref, yfox_ref, wgate_ref, gb_ref, lnw_ref, lnb_ref,
     prw_ref, pssd_ref, pfox_ref, wout_ref, gm_ref, gain_ref, sc_ref, sh_ref) = refs[:19]
    D = D_MODEL
    hm = hm_ref[...]
    y_rw = (yn_ref[...] * lnw_ref[...] + lnb_ref[...] + bonus_ref[...]) * g_ref[...]
    merged = None
    for b, (y_b, proj_ref) in enumerate(((y_rw, prw_ref), (yssd_ref[...], pssd_ref), (yfox_ref[...], pfox_ref))):
        cols = slice(b * D, (b + 1) * D)
        gate = _sigmoid(_dot(hm, wgate_ref[:, cols]) + gb_ref[:, cols])
        term = gate * _dot_bf16(y_b, proj_ref[...])
        merged = term if merged is None else merged + term
    x_new = x_ref[...] + gm_ref[...] * _dot_bf16(merged, wout_ref[...])
    hf = _norm_mod(x_new, gain_ref[...], sc_ref[...], sh_ref[...])
    if with_router:
        wr_ref, o_ref, h_ref, comb_ref = refs[19:]
        comb_ref[...] = _top2_combine(hf, wr_ref[...])
    else:
        o_ref, h_ref = refs[19:]
    o_ref[...] = x_new
    h_ref[...] = hf.astype(h_ref.dtype)


def merge(x, hm, yn, bonus, g, y_ssd, y_fox, p, router=None, tm=512):
    L, D = x.shape
    full = lambda a: pl.BlockSpec(a.shape, lambda i: (0,) * a.ndim, pipeline_mode=pl.Buffered(1))
    rows = lambda n: pl.BlockSpec((tm, n), lambda i: (i, 0))
    small = [p["w_gate"], p["gate_b"], p["lnx_w"], p["lnx_b"], p["proj_rw"], p["proj_ssd"], p["proj_fox"],
             p["w_out"], p["g_m"], p["norm_gain"], p["sc_f"], p["sh_f"]]
    out_specs = [rows(D), rows(D)]
    out_shape = [jax.ShapeDtypeStruct((L, D), F32), jax.ShapeDtypeStruct((L, D), BF16)]
    if router is not None:
        small.append(router)
        out_specs.append(rows(LANES))
        out_shape.append(jax.ShapeDtypeStruct((L, LANES), F32))
    return pl.pallas_call(
        functools.partial(_merge_kernel, router is not None),
        grid=(L // tm,),
        in_specs=[rows(D), rows(D), rows(RW_WIDTH), rows(RW_WIDTH), rows(RW_WIDTH), rows(SSD_WIDTH),
                  rows(FOX_WIDTH)] + [full(a) for a in small],
        out_specs=out_specs,
        out_shape=out_shape,
        compiler_params=_cparams("arbitrary"),
    )(x, hm, yn, bonus, g, y_ssd, y_fox, *small)


def _ffn_kernel(with_next, *refs):
    h_ref, x_ref, wg_ref, wu_ref, wd_ref, gf_ref = refs[:6]
    j = pl.program_id(1)
    acc_scr = refs[-1]

    @pl.when(j == 0)
    def _():
        acc_scr[...] = jnp.zeros_like(acc_scr)

    h = h_ref[...]
    gte = _dot(h, wg_ref[...])
    up = _dot(h, wu_ref[...])
    act = gte * _sigmoid(gte) * up
    acc_scr[...] += _dot(act.astype(BF16), wd_ref[...])

    @pl.when(j == pl.num_programs(1) - 1)
    def _():
        x_new = x_ref[...] + gf_ref[...] * acc_scr[...]
        if with_next:
            gain_ref, sc_ref, sh_ref, o_ref, hn_ref = refs[6:11]
            hn_ref[...] = _norm_mod(x_new, gain_ref[...], sc_ref[...], sh_ref[...]).astype(hn_ref.dtype)
        else:
            o_ref = refs[6]
        o_ref[...] = x_new


def ffn_dense(h, x, w_gu, w_down, g_f, next_norm=None, tm=512, tf=1408):
    L, D = x.shape
    Fh = w_down.shape[0]
    nf = Fh // tf
    row = pl.BlockSpec((1, D), lambda i, j: (0, 0))
    tile = pl.BlockSpec((tm, D), lambda i, j: (i, 0))
    args = [h, x, w_gu, w_gu, w_down, g_f]
    in_specs = [tile, tile,
                pl.BlockSpec((D, tf), lambda i, j: (0, j)),
                pl.BlockSpec((D, tf), lambda i, j: (0, j + nf)),
                pl.BlockSpec((tf, D), lambda i, j: (j, 0)), row]
    out_specs = [tile]
    out_shape = [jax.ShapeDtypeStruct((L, D), F32)]
    if next_norm is not None:
        args += list(next_norm)
        in_specs += [row, row, row]
        out_specs.append(tile)
        out_shape.append(jax.ShapeDtypeStruct((L, D), BF16))
    out = pl.pallas_call(
        functools.partial(_ffn_kernel, next_norm is not None),
        grid=(L // tm, nf),
        in_specs=in_specs,
        out_specs=out_specs,
        out_shape=out_shape,
        scratch_shapes=[pltpu.VMEM((tm, D), F32)],
        compiler_params=_cparams("arbitrary", "arbitrary"),
    )(*args)
    return out if next_norm is not None else (out[0], None)


def _moe_slots(slot_scr, comb_ref):
    slotval = slot_scr[...]
    comb = comb_ref[...]
    s1 = jnp.max(slotval, axis=-1, keepdims=True)
    first = slotval == s1
    c1 = jnp.sum(jnp.where(first, comb, 0.0), axis=-1, keepdims=True)
    rest = jnp.where(first, -1.0, slotval)
    s2 = jnp.max(rest, axis=-1, keepdims=True)
    c2 = jnp.sum(jnp.where(jnp.logical_and(rest == s2, rest >= 0.0), comb, 0.0), axis=-1, keepdims=True)
    return s1, s2, c1, c2


def _moe_kernel(nsb_ref, off_ref, h_ref, x_ref, comb_ref, wg_ref, wu_ref, wd_ref, gf_ref, o_ref,
                slot_scr, cslot_scr, xy_scr, acc_scr):
    i = pl.program_id(0)
    e = pl.program_id(1)
    j = pl.program_id(2)
    last_j = pl.num_programs(2) - 1
    T = h_ref.shape[0]
    S = xy_scr.shape[0]
    SB = MOE_SB

    @pl.when(jnp.logical_and(e == 0, j == 0))
    def _():
        lane = lax.broadcasted_iota(jnp.int32, (LANES, LANES), 1)
        off = jnp.zeros((LANES, LANES), F32)
        for ex in range(N_EXPERTS):
            off = jnp.where(lane == ex, off_ref[i, ex].astype(F32), off)
        before = _tril(LANES, strict=True).astype(BF16)
        ones = jnp.ones((LANES, LANES), BF16)
        for rb in range(T // LANES):
            rows = slice(rb * LANES, (rb + 1) * LANES)
            sel = comb_ref[rows, :] > 0.0
            sel_b = jnp.where(sel, 1.0, 0.0).astype(BF16)
            slot_scr[rows, :] = jnp.where(sel, _dot(before, sel_b) + off, -1.0)
            off = off + _dot(ones, sel_b)
        s1, s2, c1, c2 = _moe_slots(slot_scr, comb_ref)
        used = off_ref[i, N_EXPERTS - 1] + nsb_ref[i, N_EXPERTS - 1] * SB
        for cb in range(S // MOE_CB):
            chunk = slice(cb * MOE_CB, (cb + 1) * MOE_CB)

            @pl.when(used > cb * MOE_CB)
            def _():
                slot = (lax.broadcasted_iota(jnp.int32, (T, MOE_CB), 1) + cb * MOE_CB).astype(F32)
                w = jnp.where(s1 == slot, c1, 0.0) + jnp.where(s2 == slot, c2, 0.0)
                pt = jnp.where(w > 0.0, 1.0, 0.0).astype(BF16)
                xy_scr[chunk, :] = _dot_tn(pt, h_ref[...]).astype(BF16)
                w_hi = w.astype(BF16)
                w_lo = (w - w_hi.astype(F32)).astype(BF16)
                ones_t = jnp.ones((T, LANES), BF16)
                cslot_scr[chunk, :] = _dot_tn(w_hi, ones_t) + _dot_tn(w_lo, ones_t)

            @pl.when(used <= cb * MOE_CB)
            def _():
                xy_scr[chunk, :] = jnp.zeros((MOE_CB, xy_scr.shape[1]), BF16)
                cslot_scr[chunk, :] = jnp.zeros((MOE_CB, LANES), F32)

    base = off_ref[i, e]
    nb = nsb_ref[i, e]

    def ffn_block(r0, n_rows):
        rows = pl.ds(pl.multiple_of(base + r0, SB), n_rows)
        arows = pl.ds(pl.multiple_of(r0, SB), n_rows)
        xb = xy_scr[rows, :]
        gte = _dot(xb, wg_ref[0])
        up = _dot(xb, wu_ref[0])
        part = _dot((gte * _sigmoid(gte) * up).astype(BF16), wd_ref[0])

        @pl.when(j == 0)
        def _():
            acc_scr[arows, :] = part

        @pl.when(jnp.logical_and(j > 0, j < last_j))
        def _():
            acc_scr[arows, :] += part

        @pl.when(j == last_j)
        def _():
            xy_scr[rows, :] = ((acc_scr[arows, :] + part) * cslot_scr[rows, 0:1]).astype(BF16)

    def quad_body(p, carry):
        ffn_block(p * (4 * SB), 4 * SB)
        return carry

    lax.fori_loop(0, nb // 4, quad_body, 0)
    for rest in range(1, 4):
        @pl.when(nb % 4 == rest)
        def _():
            ffn_block((nb - rest) * SB, rest * SB)

    @pl.when(jnp.logical_and(e == pl.num_programs(1) - 1, j == last_j))
    def _():
        s1, s2, _, _ = _moe_slots(slot_scr, comb_ref)
        total = jnp.zeros((T, xy_scr.shape[1]), F32)
        for cb in range(S // MOE_CB):
            slot = (lax.broadcasted_iota(jnp.int32, (T, MOE_CB), 1) + cb * MOE_CB).astype(F32)
            pt = jnp.where(jnp.logical_or(s1 == slot, s2 == slot), 1.0, 0.0).astype(BF16)
            total = total + _dot(pt, xy_scr[cb * MOE_CB:(cb + 1) * MOE_CB, :])
        o_ref[...] = x_ref[...] + gf_ref[...] * total


def ffn_moe(h, x, comb, w_gu, w_down, g_f, tm=1024, tf=896):
    L, D = x.shape
    E, Fh, _ = w_down.shape
    nf = Fh // tf
    assert nf >= 2 and E == N_EXPERTS
    nt = L // tm
    counts = jnp.sum((comb[:, :E] > 0.0).reshape(nt, tm, E), axis=1, dtype=jnp.int32)
    nsb = (counts + (MOE_SB - 1)) // MOE_SB
    off = (jnp.cumsum(nsb, axis=1) - nsb) * MOE_SB
    idx = lambda f: (lambda i, e, j, n, o: f(i, e, j))
    grid_spec = pltpu.PrefetchScalarGridSpec(
        num_scalar_prefetch=2,
        grid=(nt, E, nf),
        in_specs=[pl.BlockSpec((tm, D), idx(lambda i, e, j: (i, 0))),
                  pl.BlockSpec((tm, D), idx(lambda i, e, j: (i, 0))),
                  pl.BlockSpec((tm, LANES), idx(lambda i, e, j: (i, 0))),
                  pl.BlockSpec((1, D, tf), idx(lambda i, e, j: (e, 0, j))),
                  pl.BlockSpec((1, D, tf), idx(lambda i, e, j: (e, 0, j + nf))),
                  pl.BlockSpec((1, tf, D), idx(lambda i, e, j: (e, j, 0))),
                  pl.BlockSpec((1, D), idx(lambda i, e, j: (0, 0)))],
        out_specs=pl.BlockSpec((tm, D), idx(lambda i, e, j: (i, 0))),
        scratch_shapes=[pltpu.VMEM((tm, LANES), F32),
                        pltpu.VMEM((TOP_K * tm + E * MOE_SB, LANES), F32),
                        pltpu.VMEM((TOP_K * tm + E * MOE_SB, D), BF16),
                        pltpu.VMEM((tm, D), F32)],
    )
    return pl.pallas_call(
        _moe_kernel,
        grid_spec=grid_spec,
        out_shape=jax.ShapeDtypeStruct((L, D), F32),
        compiler_params=_cparams("arbitrary", "arbitrary", "arbitrary"),
    )(nsb, off, h, x, comb, w_gu, w_gu, w_down, g_f)


def _seg_matrix(width, head_dim):
    idx = jnp.arange(width) // head_dim
    return (idx[:, None] == idx[None, :]).astype(BF16)


def _pad_cols(w, n):
    return jnp.pad(w, ((0, 0), (0, n - w.shape[1])))


def _pad_rows(w, n):
    return jnp.pad(w, ((0, n - w.shape[0]), (0, 0)))


def rwkv_branch(h, w_rw, p, v_first):
    r, k, v, lw, a, b, g, bonus = rw_pre(h, w_rw, p, v_first)
    return wkv7(r, lw, k, v, a, b), bonus, g, v


def fox_branch(h, w_fox, p):
    q, k, v, F = fox_pre(h, w_fox, p)
    FT = jnp.transpose(F[:, :FOX_HEADS])
    lo = fox_first_block(FT, p["bound2"])
    return fox_attention(q, k, v, FT, lo)


def kernel(x, c, ada_w, ada_b, norm_mix, norm_ffn, w_in, rw_mu, rw_w0, rw_w_up, rw_a0, rw_a_up, rw_g_up, rw_k_k, rw_k_a, rw_r_k, rw_lnx_w, rw_lnx_b, rw_v0, rw_v_down, rw_v_up, ssd_conv_w, ssd_conv_b, ssd_dt_bias, ssd_a_log, ssd_d, ssd_norm, fox_f_bias, fox_q_gain, fox_k_gain, gate_b, proj_rw, proj_ssd, proj_fox, w_out, ffn_w_gu, ffn_w_down, moe_router, moe_w_gu, moe_w_down):
    depth = w_in.shape[0]
    D = D_MODEL
    xs = x[0]
    row = lambda t: t.reshape(1, -1).astype(F32)
    seg64 = _seg_matrix(RW_WIDTH, RW_HEAD_DIM)
    rw_cols = 3 * RW_WIDTH + RW_DECAY_LORA + RW_AAA_LORA + RW_GATE_LORA
    ssd_cols = SSD_WIDTH + SSD_XBC + SSD_HEADS
    fox_cols = 3 * FOX_WIDTH + FOX_HEADS
    expand = (jnp.arange(LANES)[:, None] == (jnp.arange(SSD_WIDTH) // SSD_HEAD_DIM)[None, :]).astype(BF16)
    v_first = None
    mods = []
    for l in range(depth):
        mod = adaln_mod(c, ada_w[l], ada_b[l])
        mods.append([mod[:, i * D:(i + 1) * D] for i in range(6)])
    h = None
    for l in range(depth):
        sh_m, sc_m, g_m, sh_f, sc_f, g_f = mods[l]
        if h is None:
            h = norm_mod(xs, row(norm_mix[l]), sc_m, sh_m)

        wl = w_in[l]
        o = 0
        w_r = wl[:, o:o + rw_cols]; o += rw_cols
        w_s = wl[:, o:o + ssd_cols]; o += ssd_cols
        w_f = wl[:, o:o + fox_cols]; o += fox_cols
        w_g = wl[:, o:]
        W3 = 3 * RW_WIDTH
        o_a = W3 + RW_DECAY_LORA
        o_g = o_a + RW_AAA_LORA

        def rw_layout(t, hv):
            parts = [t[:, :W3], _pad_cols(t[:, W3:o_a], LANES), _pad_cols(t[:, o_a:o_g], LANES),
                     _pad_cols(t[:, o_g:], 2 * LANES), _pad_cols(hv, LANES)]
            return jnp.concatenate(parts, axis=1)

        if l == 0:
            hv_w = jnp.zeros((D, RW_VRES_LORA), F32)
        else:
            hv_w = rw_v_down[l - 1]
        w_rw = rw_layout(w_r, hv_w).astype(BF16)
        mu = rw_layout(rw_mu[l].reshape(1, -1), jnp.zeros((1, RW_VRES_LORA), F32))
        rw_p = dict(mu=mu, w0=row(rw_w0[l]), w_up=_pad_rows(rw_w_up[l], LANES).astype(BF16), a0=row(rw_a0[l]),
                    a_up=_pad_rows(rw_a_up[l], LANES).astype(BF16),
                    g_up=_pad_rows(rw_g_up[l], 2 * LANES).astype(BF16),
                    k_k=row(rw_k_k[l]), k_a=row(rw_k_a[l]), r_k=row(rw_r_k[l]), seg=seg64)
        if l > 0:
            rw_p.update(v0=row(rw_v0[l - 1]), v_up=_pad_rows(rw_v_up[l - 1], LANES).astype(BF16))
        yn, bonus, g_rw, v_cur = rwkv_branch(h, w_rw, rw_p, v_first)
        if l == 0:
            v_first = v_cur

        w_ssd = jnp.concatenate([w_s[:, SSD_WIDTH:SSD_WIDTH + SSD_XBC], w_s[:, :SSD_WIDTH],
                                 _pad_cols(w_s[:, SSD_WIDTH + SSD_XBC:], LANES)], axis=1).astype(BF16)
        A = -jnp.exp(ssd_a_log[l].astype(F32))
        ssd_p = dict(conv_w=ssd_conv_w[l], conv_b=row(ssd_conv_b[l]),
                     dt_bias=_pad_cols(row(ssd_dt_bias[l]), LANES),
                     A=_pad_cols(row(A), LANES), expand=expand,
                     d_skip=row(jnp.repeat(ssd_d[l], SSD_HEAD_DIM)), norm_w=row(ssd_norm[l]))
        y_ssd = ssd_mixer(h, w_ssd, ssd_p)

        fb = fox_f_bias[l]
        hid = jnp.arange(FOX_HEADS)
        ahead = jnp.logical_or(fb[None, :] < fb[:, None],
                               jnp.logical_and(fb[None, :] == fb[:, None], hid[None, :] < hid[:, None]))
        rank = jnp.sum(ahead, axis=1)
        head_order = jnp.sum(jnp.where(rank[None, :] == hid[:, None], hid[None, :], 0), axis=1)
        by_head = lambda t: jnp.take(t.reshape(D, FOX_HEADS, FOX_HEAD_DIM), head_order, axis=1).reshape(D, FOX_WIDTH)
        w_fox = jnp.concatenate(
            [by_head(w_f[:, i * FOX_WIDTH:(i + 1) * FOX_WIDTH]) for i in range(3)]
            + [_pad_cols(jnp.take(w_f[:, 3 * FOX_WIDTH:], head_order, axis=1), LANES)], axis=1).astype(BF16)
        f_bias_sorted = jnp.take(fox_f_bias[l], head_order)
        proj_fox_sorted = jnp.take(proj_fox[l].reshape(FOX_HEADS, FOX_HEAD_DIM, D), head_order,
                                   axis=0).reshape(FOX_WIDTH, D)
        bound2 = (1.02 * FOX_HEAD_DIM ** 0.5 * LOG2E) * jnp.max(jnp.abs(fox_q_gain[l])) * jnp.max(jnp.abs(fox_k_gain[l]))
        fox_p = dict(f_bias=_pad_cols(row(f_bias_sorted), LANES),
                     q_gain=row(jnp.tile(fox_q_gain[l], FOX_HEADS)) * (FOX_HEAD_DIM ** -0.5 * LOG2E),
                     k_gain=row(jnp.tile(fox_k_gain[l], FOX_HEADS)), seg=seg64, bound2=bound2)
        y_fox = fox_branch(h, w_fox, fox_p)

        mp = dict(w_gate=w_g.astype(BF16), gate_b=row(gate_b[l]), lnx_w=row(rw_lnx_w[l]), lnx_b=row(rw_lnx_b[l]),
                  proj_rw=proj_rw[l].astype(BF16), proj_ssd=proj_ssd[l].astype(BF16),
                  proj_fox=proj_fox_sorted.astype(BF16), w_out=w_out[l].astype(BF16), g_m=g_m,
                  norm_gain=row(norm_ffn[l]), sc_f=sc_f, sh_f=sh_f)

        if l % 2 == 0:
            xs, hf = merge(xs, h, yn, bonus, g_rw, y_ssd, y_fox, mp)
            next_norm = None
            if l + 1 < depth:
                next_norm = (row(norm_mix[l + 1]), mods[l + 1][1], mods[l + 1][0])
            xs, h = ffn_dense(hf, xs, ffn_w_gu[l // 2].astype(BF16), ffn_w_down[l // 2].astype(BF16), g_f,
                              next_norm)
        else:
            xs, hf, comb = merge(xs, h, yn, bonus, g_rw, y_ssd, y_fox, mp,
                                 router=_pad_cols(moe_router[l // 2], LANES))
            xs = ffn_moe(hf, xs, comb, moe_w_gu[l // 2].astype(BF16), moe_w_down[l // 2].astype(BF16), g_f)
            h = None
    return xs[None]
```

```python
import functools
import math

import jax
import jax.numpy as jnp
from jax import lax
from jax.experimental import pallas as pl
from jax.experimental.pallas import tpu as pltpu

F32 = jnp.float32
BF16 = jnp.bfloat16
HIGHEST = lax.Precision.HIGHEST

D_MODEL = 1024
LANES = 128
SUBLANES = 8

RW_HEADS = 8
RW_HEAD_DIM = 64
RW_WIDTH = RW_HEADS * RW_HEAD_DIM
RW_DECAY_LORA = 64
RW_AAA_LORA = 64
RW_VRES_LORA = 32
RW_GATE_LORA = 160
GN_EPS = 64e-5
RW_OFF_WLO = 3 * RW_WIDTH
RW_OFF_ALO = RW_OFF_WLO + LANES
RW_OFF_GLO = RW_OFF_ALO + LANES
RW_OFF_HV = RW_OFF_GLO + 2 * LANES
RW_PAD_COLS = RW_OFF_HV + LANES
RW_CHUNK = 64

SSD_HEADS = 16
SSD_HEAD_DIM = 64
SSD_WIDTH = SSD_HEADS * SSD_HEAD_DIM
SSD_GROUPS = 4
SSD_STATE = 128
SSD_CONV = 4
SSD_CHUNK = 128
SSD_XBC = SSD_WIDTH + 2 * SSD_GROUPS * SSD_STATE
SSD_GROUP_WIDTH = SSD_WIDTH // SSD_GROUPS

FOX_HEADS = 8
FOX_HEAD_DIM = 64
FOX_WIDTH = FOX_HEADS * FOX_HEAD_DIM
FOX_TQ = 512
FOX_TK = 512
LOG2E = 1.4426950408889634

FFN_DENSE = 2816
N_EXPERTS = 8
TOP_K = 2
FFN_EXPERT = 3584
MOE_SB = 64
MOE_MAXB = 8
MOE_CB = 512
EPS = 1e-6

VMEM_LIMIT = 56 * 1024 * 1024


def _cparams(*sem):
    return pltpu.CompilerParams(dimension_semantics=sem, vmem_limit_bytes=VMEM_LIMIT)


def _sigmoid(x):
    return 1.0 / (1.0 + jnp.exp(-x))


def _softplus(x):
    return jnp.maximum(x, 0.0) + jnp.log(1.0 + jnp.exp(-jnp.abs(x)))


def _dot(a, b):
    return jnp.dot(a, b, preferred_element_type=F32)


def _dot_bf16(a, b):
    return jnp.dot(a.astype(BF16), b.astype(BF16), preferred_element_type=F32)


def _dot_hi(a, b):
    return jnp.dot(a, b, precision=HIGHEST, preferred_element_type=F32)


def _dot_nt(a, b, precision=None):
    return lax.dot_general(a, b, (((1,), (1,)), ((), ())), precision=precision, preferred_element_type=F32)


def _dot_tn(a, b, precision=None):
    return lax.dot_general(a, b, (((0,), (0,)), ((), ())), precision=precision, preferred_element_type=F32)


def _split_dot(x, ones_bf16):
    return _dot(x.astype(BF16), ones_bf16)


def _split3(x):
    hi = x.astype(BF16)
    r1 = x - hi.astype(F32)
    mid = r1.astype(BF16)
    return hi, mid, (r1 - mid.astype(F32)).astype(BF16)


def _tril(n, strict=False):
    r = lax.broadcasted_iota(jnp.int32, (n, n), 0)
    c = lax.broadcasted_iota(jnp.int32, (n, n), 1)
    return (r > c) if strict else (r >= c)


def _mod_kernel(c_ref, w_ref, b_ref, o_ref):
    c = c_ref[...]
    o_ref[...] = _dot_hi(c * _sigmoid(c), w_ref[...]) + b_ref[...]


def adaln_mod(c, w, b):
    d, n = w.shape
    tn = 1024
    c8 = jnp.broadcast_to(c, (SUBLANES, d))
    out = pl.pallas_call(
        _mod_kernel,
        grid=(n // tn,),
        in_specs=[pl.BlockSpec((SUBLANES, d), lambda j: (0, 0)),
                  pl.BlockSpec((d, tn), lambda j: (0, j)),
                  pl.BlockSpec((1, tn), lambda j: (0, j))],
        out_specs=pl.BlockSpec((SUBLANES, tn), lambda j: (0, j)),
        out_shape=jax.ShapeDtypeStruct((SUBLANES, n), F32),
        compiler_params=_cparams("arbitrary"),
    )(c8, w, b.reshape(1, n))
    return out[:1]


def _norm_mod(x, gain, sc, sh):
    y = x * lax.rsqrt(jnp.mean(x * x, axis=-1, keepdims=True) + EPS)
    return y * gain * (1.0 + sc) + sh


def _norm_kernel(x_ref, gain_ref, sc_ref, sh_ref, h_ref):
    h_ref[...] = _norm_mod(x_ref[...], gain_ref[...], sc_ref[...], sh_ref[...]).astype(h_ref.dtype)


def norm_mod(x, gain, sc, sh, tm=512):
    L, d = x.shape
    row = pl.BlockSpec((1, d), lambda i: (0, 0))
    return pl.pallas_call(
        _norm_kernel,
        grid=(L // tm,),
        in_specs=[pl.BlockSpec((tm, d), lambda i: (i, 0)), row, row, row],
        out_specs=pl.BlockSpec((tm, d), lambda i: (i, 0)),
        out_shape=jax.ShapeDtypeStruct((L, d), BF16),
        compiler_params=_cparams("arbitrary"),
    )(x, gain, sc, sh)


def _rw_pre_kernel(has_vres, *refs):
    if has_vres:
        (h_ref, w_ref, mu_ref, w0_ref, wup_ref, a0_ref, aup_ref, gup_ref, kk_ref, ka_ref, rk_ref, seg_ref,
         vfirst_ref, v0_ref, vup_ref,
         r_out, k_out, v_out, lw_out, a_out, b_out, g_out, bonus_out, ext_scr) = refs
    else:
        (h_ref, w_ref, mu_ref, w0_ref, wup_ref, a0_ref, aup_ref, gup_ref, kk_ref, ka_ref, rk_ref, seg_ref,
         r_out, k_out, v_out, lw_out, a_out, b_out, g_out, bonus_out, ext_scr) = refs
    T = h_ref.shape[0]

    @pl.when(pl.program_id(0) == 0)
    def _():
        ext_scr[0:SUBLANES, :] = jnp.zeros((SUBLANES, ext_scr.shape[1]), F32)

    cur = _dot(h_ref[...], w_ref[...])
    ext_scr[SUBLANES:SUBLANES + T, :] = cur
    prev = ext_scr[SUBLANES - 1:SUBLANES - 1 + T, :]
    ext_scr[0:SUBLANES, :] = cur[T - SUBLANES:T, :]
    s = cur + (prev - cur) * mu_ref[...]

    W = RW_WIDTH
    r = s[:, 0:W]
    k = s[:, W:2 * W]
    v = s[:, 2 * W:3 * W]
    w_lo = s[:, RW_OFF_WLO:RW_OFF_WLO + LANES]
    a_lo = s[:, RW_OFF_ALO:RW_OFF_ALO + LANES]
    g_lo = s[:, RW_OFF_GLO:RW_OFF_GLO + 2 * LANES]
    seg = seg_ref[...]

    wlog = -_softplus(-(w0_ref[...] + _dot_bf16(jnp.tanh(w_lo), wup_ref[...]))) - 0.5
    a = _sigmoid(a0_ref[...] + _dot_bf16(a_lo, aup_ref[...]))
    g = _dot_bf16(_sigmoid(g_lo), gup_ref[...])
    if has_vres:
        hv = s[:, RW_OFF_HV:RW_OFF_HV + LANES]
        v = v + (vfirst_ref[...].astype(F32) - v) * _sigmoid(v0_ref[...] + _dot_bf16(hv, vup_ref[...]))
    kk = k * kk_ref[...]
    kk = kk / jnp.maximum(jnp.sqrt(_split_dot(kk * kk, seg)), 1e-12)
    k = k * (1.0 + (a - 1.0) * ka_ref[...])
    r_out[...] = r.astype(r_out.dtype)
    k_out[...] = k.astype(k_out.dtype)
    v_out[...] = v.astype(v_out.dtype)
    lw_out[...] = -jnp.exp(wlog)
    a_out[...] = (-kk).astype(a_out.dtype)
    b_out[...] = (kk * a).astype(b_out.dtype)
    g_out[...] = g.astype(g_out.dtype)
    bonus_out[...] = (_split_dot(r * k * rk_ref[...], seg) * v).astype(bonus_out.dtype)


def rw_pre(h, w_rw, p, v_first, tm=512):
    L, D = h.shape
    has_vres = v_first is not None
    full = lambda a: pl.BlockSpec(a.shape, lambda i: (0,) * a.ndim)
    rows = lambda n: pl.BlockSpec((tm, n), lambda i: (i, 0))
    args = [h, w_rw, p["mu"], p["w0"], p["w_up"], p["a0"], p["a_up"], p["g_up"], p["k_k"], p["k_a"], p["r_k"],
            p["seg"]]
    specs = [rows(D)] + [full(a) for a in args[1:]]
    if has_vres:
        args += [v_first, p["v0"], p["v_up"]]
        specs += [rows(RW_WIDTH), full(p["v0"]), full(p["v_up"])]
    out_dtypes = [BF16, BF16, BF16, F32, BF16, BF16, BF16, BF16]
    return pl.pallas_call(
        functools.partial(_rw_pre_kernel, has_vres),
        grid=(L // tm,),
        in_specs=specs,
        out_specs=[rows(RW_WIDTH)] * 8,
        out_shape=[jax.ShapeDtypeStruct((L, RW_WIDTH), dt) for dt in out_dtypes],
        scratch_shapes=[pltpu.VMEM((tm + SUBLANES, RW_PAD_COLS), F32)],
        compiler_params=_cparams("arbitrary"),
    )(*args)


def _wkv_kernel(r_ref, lw_ref, k_ref, v_ref, a_ref, b_ref, y_ref, h_scr):
    C = RW_CHUNK
    C2 = 2 * C
    n_chunks = r_ref.shape[0] // C

    @pl.when(pl.program_id(1) == 0)
    def _():
        h_scr[...] = jnp.zeros_like(h_scr)

    is_a = lax.broadcasted_iota(jnp.int32, (C, LANES), 1) < RW_HEAD_DIM
    ri = lax.broadcasted_iota(jnp.int32, (C2, C2), 0)
    ci = lax.broadcasted_iota(jnp.int32, (C2, C2), 1)
    same = (ri >= C) == (ci >= C)
    strict = jnp.logical_and(same, ri > ci)
    incl = jnp.logical_and(same, ri >= ci)
    eye2 = (ri == ci).astype(F32)
    tri_c = _tril(C).astype(BF16)

    def two(x):
        return jnp.concatenate([jnp.where(is_a, x, 0.0), jnp.where(is_a, 0.0, x)], axis=0)

    chunks = range(n_chunks)
    at2, rt2, bh2, kh2, v2, n_ab, a_ak, m_rb, m_rk, p_end = ([] for _ in range(10))
    for c in chunks:
        sl = pl.ds(c * C, C)
        lw = lw_ref[sl, :]
        lw_hi = lw.astype(BF16)
        lw_lo = (lw - lw_hi.astype(F32)).astype(BF16)
        cs = _dot(tri_c, lw_hi) + _dot(tri_c, lw_lo)
        cs_end = cs[C - 1:C, :]
        e_neg = jnp.exp(-cs)
        e_end = jnp.exp(cs_end - cs)
        a = a_ref[sl, :].astype(F32)
        b = b_ref[sl, :].astype(F32)
        k = k_ref[sl, :].astype(F32)
        at2.append(two(a * jnp.exp(cs - lw)))
        rt2.append(two(r_ref[sl, :].astype(F32) * jnp.exp(cs)))
        bh2.append(two(b * e_end))
        kh2.append(two(k * e_end))
        v2.append(two(v_ref[sl, :].astype(F32)))
        p_end.append(jnp.exp(cs_end))
        quad = _dot_nt(jnp.concatenate([at2[c], rt2[c]], axis=0).astype(BF16),
                       jnp.concatenate([two(b * e_neg), two(k * e_neg)], axis=0).astype(BF16))
        n_ab.append(jnp.where(strict, quad[0:C2, 0:C2], 0.0))
        a_ak.append(jnp.where(strict, quad[0:C2, C2:], 0.0))
        m_rb.append(jnp.where(incl, quad[C2:, 0:C2], 0.0))
        m_rk.append(jnp.where(incl, quad[C2:, C2:], 0.0))
    t_inv = [eye2 + n for n in n_ab]
    n_pow = n_ab
    n_pow = [_dot_bf16(n, n) for n in n_pow]
    for _ in range(int(math.log2(C)) - 2):
        both = [_dot_bf16(jnp.concatenate([n, t], axis=0), n) for n, t in zip(n_pow, t_inv)]
        t_inv = [t + b[C2:, :] for t, b in zip(t_inv, both)]
        n_pow = [b[:C2, :] for b in both]
    t_inv = [t + _dot_bf16(t, n) for t, n in zip(t_inv, n_pow)]
    akv = [_dot_bf16(a_ak[c], v2[c]) for c in chunks]
    au = [_dot_bf16(t_inv[c], jnp.concatenate([at2[c], akv[c]], axis=1)) for c in chunks]
    mau = [_dot_bf16(m_rb[c], au[c]) for c in chunks]
    mkv = [_dot_bf16(m_rk[c], v2[c]) for c in chunks]
    gu = [_dot_tn(bh2[c].astype(BF16), au[c].astype(BF16)) for c in chunks]
    kv = [_dot_tn(kh2[c].astype(BF16), v2[c].astype(BF16)) for c in chunks]
    inv_n = 1.0 / RW_HEAD_DIM
    h = h_scr[...]
    for c in chunks:
        rh2 = rt2[c] + mau[c][:, :LANES]
        g_mat = eye2 * p_end[c] + gu[c][:, :LANES]
        yh = _dot_bf16(jnp.concatenate([rh2, g_mat], axis=0), h)
        h = yh[C2:, :] + gu[c][:, LANES:] + kv[c]
        y2 = yh[:C2, :] + mau[c][:, LANES:] + mkv[c]
        y = y2[:C, :] + y2[C:, :]
        s_all = jnp.sum(y, axis=-1, keepdims=True)
        s_a = jnp.sum(jnp.where(is_a, y, 0.0), axis=-1, keepdims=True)
        yc = y - jnp.where(is_a, s_a, s_all - s_a) * inv_n
        sq = yc * yc
        q_all = jnp.sum(sq, axis=-1, keepdims=True)
        q_a = jnp.sum(jnp.where(is_a, sq, 0.0), axis=-1, keepdims=True)
        yn = yc * lax.rsqrt(jnp.where(is_a, q_a, q_all - q_a) * inv_n + GN_EPS)
        y_ref[pl.ds(c * C, C), :] = yn.astype(y_ref.dtype)
    h_scr[...] = h


def wkv7(r, lw, k, v, a, b, rows=2048):
    L, W = r.shape
    spec = pl.BlockSpec((rows, LANES), lambda p, i: (i, p))
    return pl.pallas_call(
        _wkv_kernel,
        grid=(W // LANES, L // rows),
        in_specs=[spec] * 6,
        out_specs=spec,
        out_shape=jax.ShapeDtypeStruct((L, W), BF16),
        scratch_shapes=[pltpu.VMEM((LANES, LANES), F32)],
        compiler_params=_cparams("arbitrary", "arbitrary"),
    )(r, lw, k, v, a, b)


def _ssd_kernel(h_ref, w_ref, cw_ref, cb_ref, bias_ref, A_ref, exp_ref, dskip_ref, nw_ref, o_ref,
                ext_scr, xbc_scr, s_scr):
    Q = SSD_CHUNK
    GW = SSD_GROUP_WIDTH
    NS = SSD_STATE
    R = h_ref.shape[0]

    @pl.when(pl.program_id(0) == 0)
    def _():
        s_scr[...] = jnp.zeros_like(s_scr)
        ext_scr[0:SUBLANES, :] = jnp.zeros((SUBLANES, ext_scr.shape[1]), F32)

    cols = _dot(h_ref[...], w_ref[...])
    ext_scr[SUBLANES:SUBLANES + R, :] = cols[:, 0:SSD_XBC]
    acc = cols[:, 0:SSD_XBC] * cw_ref[SSD_CONV - 1:SSD_CONV, :] + cb_ref[...]
    for j in range(1, SSD_CONV):
        acc = acc + ext_scr[SUBLANES - j:SUBLANES - j + R, :] * cw_ref[SSD_CONV - 1 - j:SSD_CONV - j, :]
    ext_scr[0:SUBLANES, :] = ext_scr[R:R + SUBLANES, :]
    xbc_scr[...] = acc * _sigmoid(acc)

    incl = _tril(Q)
    tri = incl.astype(BF16)
    triu = jnp.logical_not(_tril(Q, strict=True)).astype(BF16)
    expand = exp_ref[...]
    first_half = lax.broadcasted_iota(jnp.int32, (Q, LANES), 1) < SSD_HEAD_DIM
    heads_per_group = SSD_HEADS // SSD_GROUPS
    for c in range(R // Q):
        rows = slice(c * Q, (c + 1) * Q)
        xs = xbc_scr[rows, 0:SSD_WIDTH]
        z = cols[rows, SSD_XBC:SSD_XBC + SSD_WIDTH]
        dt = _softplus(cols[rows, SSD_XBC + SSD_WIDTH:] + bias_ref[...])
        a3 = _split3(dt * A_ref[...])
        a_cum = sum(_dot(tri, t) for t in a3)
        a_cumT = sum(_dot_tn(t, triu) for t in a3)
        a_end = a_cum[Q - 1:Q, :]
        dt_x = _dot(dt.astype(BF16), expand)
        eac_x = _dot(jnp.exp(a_cum).astype(BF16), expand)
        dte_x = _dot(jnp.exp(a_end - a_cum).astype(BF16), expand)
        cd_x = sum(_dot(t, expand) for t in _split3(jnp.broadcast_to(jnp.exp(a_end), (SUBLANES, LANES))))[0:1, :]
        xdt = xs * dt_x
        for g in range(SSD_GROUPS):
            Bg = xbc_scr[rows, SSD_WIDTH + g * NS:SSD_WIDTH + (g + 1) * NS].astype(BF16)
            Cg = xbc_scr[rows, SSD_WIDTH + (SSD_GROUPS + g) * NS:SSD_WIDTH + (SSD_GROUPS + g + 1) * NS].astype(BF16)
            cb = _dot_nt(Cg, Bg)
            gs = slice(g * GW, (g + 1) * GW)
            s_prev = s_scr[:, gs]
            y_g = _dot_bf16(Cg, s_prev) * eac_x[:, gs]
            pieces = []
            for pr in range(heads_per_group // 2):
                ps = slice(g * GW + pr * LANES, g * GW + (pr + 1) * LANES)
                xdt_p = xdt[:, ps]
                acc_p = None
                for e in range(2):
                    hd = g * heads_per_group + pr * 2 + e
                    seg = a_cum[:, hd:hd + 1] - a_cumT[hd:hd + 1, :]
                    m = cb * jnp.exp(jnp.where(incl, seg, -jnp.inf))
                    xm = jnp.where(first_half if e == 0 else jnp.logical_not(first_half), xdt_p, 0.0)
                    t = _dot_bf16(m, xm)
                    acc_p = t if acc_p is None else acc_p + t
                pieces.append(acc_p)
            y_g = y_g + jnp.concatenate(pieces, axis=1)
            s_scr[:, gs] = s_prev * cd_x[:, gs] +_dot_tn(Bg, (dte_x[:, gs] * xdt[:, gs]).astype(BF16))
            y_g = y_g + dskip_ref[:, gs] * xs[:, gs]
            zg = z[:, gs]
            y_g = y_g * (zg * _sigmoid(zg))
            y_g = y_g * lax.rsqrt(jnp.mean(y_g * y_g, axis=-1, keepdims=True) + EPS)
            o_ref[rows, gs] = (y_g * nw_ref[:, gs]).astype(o_ref.dtype)


def ssd_mixer(h, w_ssd, p, rows=512):
    L, D = h.shape
    N = w_ssd.shape[1]
    full = lambda a: pl.BlockSpec(a.shape, lambda i: (0,) * a.ndim)
    small = [p["conv_w"], p["conv_b"], p["dt_bias"], p["A"], p["expand"], p["d_skip"], p["norm_w"]]
    return pl.pallas_call(
        _ssd_kernel,
        grid=(L // rows,),
        in_specs=[pl.BlockSpec((rows, D), lambda i: (i, 0)), full(w_ssd)] + [full(a) for a in small],
        out_specs=pl.BlockSpec((rows, SSD_WIDTH), lambda i: (i, 0)),
        out_shape=jax.ShapeDtypeStruct((L, SSD_WIDTH), BF16),
        scratch_shapes=[pltpu.VMEM((rows + SUBLANES, SSD_XBC), F32), pltpu.VMEM((rows, SSD_XBC), F32),
                        pltpu.VMEM((SSD_STATE, SSD_WIDTH), F32)],
        compiler_params=_cparams("arbitrary"),
    )(h, w_ssd, *small)


def _fox_pre_kernel(h_ref, w_ref, fb_ref, qg_ref, kg_ref, seg_ref, q_out, k_out, v_out, f_out, carry_scr):
    T = h_ref.shape[0]
    W = FOX_WIDTH

    @pl.when(pl.program_id(0) == 0)
    def _():
        carry_scr[...] = jnp.zeros_like(carry_scr)

    cols = _dot(h_ref[...], w_ref[...])
    seg = seg_ref[...]
    q = cols[:, 0:W]
    k = cols[:, W:2 * W]
    inv_d = 1.0 / FOX_HEAD_DIM
    qn = q * lax.rsqrt(_split_dot(q * q, seg) * inv_d + EPS) * qg_ref[...]
    kn = k * lax.rsqrt(_split_dot(k * k, seg) * inv_d + EPS) * kg_ref[...]
    q_out[...] = qn.astype(q_out.dtype)
    k_out[...] = kn.astype(k_out.dtype)
    v_out[...] = cols[:, 2 * W:3 * W].astype(v_out.dtype)
    f = cols[:, 3 * W:3 * W + LANES] + fb_ref[...]
    ls = -_softplus(-f)
    cum = _dot_hi(_tril(T).astype(F32), ls) + carry_scr[0:1, :]
    f_out[...] = cum * LOG2E
    carry_scr[...] = jnp.broadcast_to(cum[T - 1:T, :], carry_scr.shape)


def fox_pre(h, w_fox, p, tm=512):
    L, D = h.shape
    full = lambda a: pl.BlockSpec(a.shape, lambda i: (0,) * a.ndim)
    rows = lambda n: pl.BlockSpec((tm, n), lambda i: (i, 0))
    small = [w_fox, p["f_bias"], p["q_gain"], p["k_gain"], p["seg"]]
    return pl.pallas_call(
        _fox_pre_kernel,
        grid=(L // tm,),
        in_specs=[rows(D)] + [full(a) for a in small],
        out_specs=[rows(FOX_WIDTH)] * 3 + [rows(LANES)],
        out_shape=[jax.ShapeDtypeStruct((L, FOX_WIDTH), BF16)] * 3 + [jax.ShapeDtypeStruct((L, LANES), F32)],
        scratch_shapes=[pltpu.VMEM((SUBLANES, LANES), F32)],
        compiler_params=_cparams("arbitrary"),
    )(h, *small)


def _fox_kernel(lo_ref, q_ref, k_ref, v_ref, fk_ref, o_ref, m_scr, acc_scr, s0_scr, s1_scr):
    tq = q_ref.shape[0]
    tk = FOX_TK
    pair = pl.program_id(0)
    qi = pl.program_id(1)
    is_a = lax.broadcasted_iota(jnp.int32, (tq, LANES), 1) < FOX_HEAD_DIM
    q = q_ref[...]
    zero = jnp.zeros_like(q)
    qs = (jnp.where(is_a, q, zero), jnp.where(is_a, zero, q))
    m_scr[...] = jnp.full(m_scr.shape, -jnp.inf, F32)
    key_is_a = lax.broadcasted_iota(jnp.int32, (tk, LANES), 1) < FOX_HEAD_DIM
    acc_scr[...] = jnp.zeros_like(acc_scr)
    rel = lax.broadcasted_iota(jnp.int32, (tq, tk), 0) - lax.broadcasted_iota(jnp.int32, (tq, tk), 1)
    last = (qi * tq) // tk

    def fill(s_ref, kb, diagonal=False):
        ks = pl.ds(pl.multiple_of(kb * tk, tk), tk)
        k_blk = k_ref[ks, :]
        for e in range(2):
            s = _dot_nt(qs[e], k_blk) - fk_ref[pl.ds(2 * pair + e, 1), ks]
            if diagonal:
                s = jnp.where(rel >= kb * tk - qi * tq, s, -jnp.inf)
            s_ref[e] = s

    def consume(s_ref, kb):
        v_blk = v_ref[pl.ds(pl.multiple_of(kb * tk, tk), tk), :]
        one = jnp.ones_like(v_blk)
        v_one = (jnp.where(key_is_a, v_blk, one), jnp.where(key_is_a, one, v_blk))
        for e in range(2):
            s = s_ref[e]
            m_old = m_scr[e]
            m_new = jnp.maximum(m_old, jnp.max(s, axis=-1, keepdims=True))
            pexp = jnp.exp2(s - m_new)
            acc_scr[e] = jnp.exp2(m_old - m_new) * acc_scr[e] + _dot(pexp.astype(BF16), v_one[e])
            m_scr[e] = m_new

    n = last - jnp.minimum(lo_ref[2 * pair, qi], lo_ref[2 * pair + 1, qi])
    fill(s0_scr, last, diagonal=True)

    def two_steps(u, carry):
        kb = last - 2 * u
        fill(s1_scr, kb - 1)
        consume(s0_scr, kb)
        fill(s0_scr, kb - 2)
        consume(s1_scr, kb - 1)
        return carry

    lax.fori_loop(0, n // 2, two_steps, 0)

    @pl.when(n % 2 == 1)
    def _():
        fill(s1_scr, last - n)
        consume(s0_scr, last - n + 1)
        consume(s1_scr, last - n)

    @pl.when(n % 2 == 0)
    def _():
        consume(s0_scr, last - n)

    acc_a = acc_scr[0]
    acc_b = acc_scr[1]
    half = FOX_HEAD_DIM
    o_ref[...] = jnp.where(is_a, acc_a / pltpu.roll(acc_a, half, 1),
                           acc_b / pltpu.roll(acc_b, half, 1)).astype(o_ref.dtype)


def fox_attention(q, k, v, FT, lo):
    L, W = q.shape
    H = FT.shape[0]
    tq = FOX_TQ
    grid_spec = pltpu.PrefetchScalarGridSpec(
        num_scalar_prefetch=1,
        grid=(W // LANES, L // tq),
        in_specs=[pl.BlockSpec((tq, LANES), lambda p, i, lo_r: (i, p)),
                  pl.BlockSpec((L, LANES), lambda p, i, lo_r: (0, p)),
                  pl.BlockSpec((L, LANES), lambda p, i, lo_r: (0, p)),
                  pl.BlockSpec((H, L), lambda p, i, lo_r: (0, 0))],
        out_specs=pl.BlockSpec((tq, LANES), lambda p, i, lo_r: (i, p)),
        scratch_shapes=[pltpu.VMEM((2, tq, 1), F32), pltpu.VMEM((2, tq, LANES), F32),
                        pltpu.VMEM((2, tq, FOX_TK), F32), pltpu.VMEM((2, tq, FOX_TK), F32)],
    )
    return pl.pallas_call(
        _fox_kernel,
        grid_spec=grid_spec,
        out_shape=jax.ShapeDtypeStruct((L, W), BF16),
        compiler_params=_cparams("arbitrary", "arbitrary"),
    )(lo, q, k, v, FT)


def fox_first_block(FT, bound2):
    f_first = FT[:, ::FOX_TQ]
    f_last = FT[:, FOX_TK - 1::FOX_TK]
    gap = f_first[:, :, None] - f_last[:, None, :] + 2.0 * bound2
    last = (jnp.arange(f_first.shape[1]) * FOX_TQ) // FOX_TK
    needed = jnp.logical_or(gap >= -152.0, jnp.arange(f_last.shape[1])[None, None, :] >= last[None, :, None])
    return jnp.sum(jnp.cumsum(needed.astype(jnp.int32), axis=-1) == 0, axis=-1).astype(jnp.int32)


def _top2_combine(h, wr):
    T = h.shape[0]
    lane = lax.broadcasted_iota(jnp.int32, (T, LANES), 1)
    h_hi = h.astype(BF16)
    h_lo = (h - h_hi.astype(F32)).astype(BF16)
    w_hi = wr.astype(BF16)
    w_lo = (wr - w_hi.astype(F32)).astype(BF16)
    logits = _dot(h_hi, w_hi) + (_dot(h_hi, w_lo) + _dot(h_lo, w_hi))
    logits = jnp.where(lane < N_EXPERTS, logits, -jnp.inf)
    m1 = jnp.max(logits, axis=-1, keepdims=True)
    i1 = jnp.min(jnp.where(logits == m1, lane, LANES), axis=-1, keepdims=True)
    rest = jnp.where(lane == i1, -jnp.inf, logits)
    m2 = jnp.max(rest, axis=-1, keepdims=True)
    i2 = jnp.min(jnp.where(rest == m2, lane, LANES), axis=-1, keepdims=True)
    e2 = jnp.exp(m2 - m1)
    return jnp.where(lane == i1, 1.0 / (1.0 + e2), 0.0) + jnp.where(lane == i2, e2 / (1.0 + e2), 0.0)


def _merge_kernel(with_router, *refs):
    (x_ref, hm_ref, yn_ref, bonus_ref, g_ref, yssd_ref, yfox_ref, wgate_ref, gb_ref, lnw_ref, lnb_ref,
     prw_ref, pssd_ref, pfox_ref, wout_ref, gm_ref, gain_ref, sc_ref, sh_ref) = refs[:19]
    D = D_MODEL
    hm = hm_ref[...]
    y_rw = (yn_ref[...] * lnw_ref[...] + lnb_ref[...] + bonus_ref[...]) * g_ref[...]
    merged = None
    for b, (y_b, proj_ref) in enumerate(((y_rw, prw_ref), (yssd_ref[...], pssd_ref), (yfox_ref[...], pfox_ref))):
        cols = slice(b * D, (b + 1) * D)
        gate = _sigmoid(_dot(hm, wgate_ref[:, cols]) + gb_ref[:, cols])
        term = gate * _dot_bf16(y_b, proj_ref[...])
        merged = term if merged is None else merged + term
    x_new = x_ref[...] + gm_ref[...] * _dot_bf16(merged, wout_ref[...])
    hf = _norm_mod(x_new, gain_ref[...], sc_ref[...], sh_ref[...])
    if with_router:
        wr_ref, o_ref, h_ref, comb_ref = refs[19:]
        comb_ref[...] = _top2_combine(hf, wr_ref[...])
    else:
        o_ref, h_ref = refs[19:]
    o_ref[...] = x_new
    h_ref[...] = hf.astype(h_ref.dtype)


def merge(x, hm, yn, bonus, g, y_ssd, y_fox, p, router=None, tm=256):
    L, D = x.shape
    full = lambda a: pl.BlockSpec(a.shape, lambda i: (0,) * a.ndim)
    rows = lambda n: pl.BlockSpec((tm, n), lambda i: (i, 0))
    small = [p["w_gate"], p["gate_b"], p["lnx_w"], p["lnx_b"], p["proj_rw"], p["proj_ssd"], p["proj_fox"],
             p["w_out"], p["g_m"], p["norm_gain"], p["sc_f"], p["sh_f"]]
    out_specs = [rows(D), rows(D)]
    out_shape = [jax.ShapeDtypeStruct((L, D), F32), jax.ShapeDtypeStruct((L, D), BF16)]
    if router is not None:
        small.append(router)
        out_specs.append(rows(LANES))
        out_shape.append(jax.ShapeDtypeStruct((L, LANES), F32))
    return pl.pallas_call(
        functools.partial(_merge_kernel, router is not None),
        grid=(L // tm,),
        in_specs=[rows(D), rows(D), rows(RW_WIDTH), rows(RW_WIDTH), rows(RW_WIDTH), rows(SSD_WIDTH),
                  rows(FOX_WIDTH)] + [full(a) for a in small],
        out_specs=out_specs,
        out_shape=out_shape,
        compiler_params=_cparams("arbitrary"),
    )(x, hm, yn, bonus, g, y_ssd, y_fox, *small)


def _ffn_kernel(with_next, *refs):
    h_ref, x_ref, wg_ref, wu_ref, wd_ref, gf_ref = refs[:6]
    j = pl.program_id(1)
    acc_scr = refs[-1]

    @pl.when(j == 0)
    def _():
        acc_scr[...] = jnp.zeros_like(acc_scr)

    h = h_ref[...]
    gte = _dot(h, wg_ref[...])
    up = _dot(h, wu_ref[...])
    act = gte * _sigmoid(gte) * up
    acc_scr[...] += _dot(act.astype(BF16), wd_ref[...])

    @pl.when(j == pl.num_programs(1) - 1)
    def _():
        x_new = x_ref[...] + gf_ref[...] * acc_scr[...]
        if with_next:
            gain_ref, sc_ref, sh_ref, o_ref, hn_ref = refs[6:11]
            hn_ref[...] = _norm_mod(x_new, gain_ref[...], sc_ref[...], sh_ref[...]).astype(hn_ref.dtype)
        else:
            o_ref = refs[6]
        o_ref[...] = x_new


def ffn_dense(h, x, w_gu, w_down, g_f, next_norm=None, tm=512, tf=1408):
    L, D = x.shape
    Fh = w_down.shape[0]
    nf = Fh // tf
    row = pl.BlockSpec((1, D), lambda i, j: (0, 0))
    tile = pl.BlockSpec((tm, D), lambda i, j: (i, 0))
    args = [h, x, w_gu, w_gu, w_down, g_f]
    in_specs = [tile, tile,
                pl.BlockSpec((D, tf), lambda i, j: (0, j)),
                pl.BlockSpec((D, tf), lambda i, j: (0, j + nf)),
                pl.BlockSpec((tf, D), lambda i, j: (j, 0)), row]
    out_specs = [tile]
    out_shape = [jax.ShapeDtypeStruct((L, D), F32)]
    if next_norm is not None:
        args += list(next_norm)
        in_specs += [row, row, row]
        out_specs.append(tile)
        out_shape.append(jax.ShapeDtypeStruct((L, D), BF16))
    out = pl.pallas_call(
        functools.partial(_ffn_kernel, next_norm is not None),
        grid=(L // tm, nf),
        in_specs=in_specs,
        out_specs=out_specs,
        out_shape=out_shape,
        scratch_shapes=[pltpu.VMEM((tm, D), F32)],
        compiler_params=_cparams("arbitrary", "arbitrary"),
    )(*args)
    return out if next_norm is not None else (out[0], None)


def _moe_slots(slot_scr, comb_ref):
    slotval = slot_scr[...]
    comb = comb_ref[...]
    s1 = jnp.max(slotval, axis=-1, keepdims=True)
    first = slotval == s1
    c1 = jnp.sum(jnp.where(first, comb, 0.0), axis=-1, keepdims=True)
    rest = jnp.where(first, -1.0, slotval)
    s2 = jnp.max(rest, axis=-1, keepdims=True)
    c2 = jnp.sum(jnp.where(jnp.logical_and(rest == s2, rest >= 0.0), comb, 0.0), axis=-1, keepdims=True)
    return s1, s2, c1, c2


def _moe_kernel(nsb_ref, off_ref, h_ref, x_ref, comb_ref, wg_ref, wu_ref, wd_ref, gf_ref, o_ref,
                slot_scr, cslot_scr, xy_scr, acc_scr):
    i = pl.program_id(0)
    e = pl.program_id(1)
    j = pl.program_id(2)
    last_j = pl.num_programs(2) - 1
    T = h_ref.shape[0]
    S = xy_scr.shape[0]
    SB = MOE_SB

    @pl.when(jnp.logical_and(e == 0, j == 0))
    def _():
        lane = lax.broadcasted_iota(jnp.int32, (LANES, LANES), 1)
        off = jnp.zeros((LANES, LANES), F32)
        for ex in range(N_EXPERTS):
            off = jnp.where(lane == ex, off_ref[i, ex].astype(F32), off)
        before = _tril(LANES, strict=True).astype(BF16)
        ones = jnp.ones((LANES, LANES), BF16)
        for rb in range(T // LANES):
            rows = slice(rb * LANES, (rb + 1) * LANES)
            sel = comb_ref[rows, :] > 0.0
            sel_b = jnp.where(sel, 1.0, 0.0).astype(BF16)
            slot_scr[rows, :] = jnp.where(sel, _dot(before, sel_b) + off, -1.0)
            off = off + _dot(ones, sel_b)
        s1, s2, c1, c2 = _moe_slots(slot_scr, comb_ref)
        used = off_ref[i, N_EXPERTS - 1] + nsb_ref[i, N_EXPERTS - 1] * SB
        for cb in range(S // MOE_CB):
            chunk = slice(cb * MOE_CB, (cb + 1) * MOE_CB)

            @pl.when(used > cb * MOE_CB)
            def _():
                slot = (lax.broadcasted_iota(jnp.int32, (T, MOE_CB), 1) + cb * MOE_CB).astype(F32)
                w = jnp.where(s1 == slot, c1, 0.0) + jnp.where(s2 == slot, c2, 0.0)
                pt = jnp.where(w > 0.0, 1.0, 0.0).astype(BF16)
                xy_scr[chunk, :] = _dot_tn(pt, h_ref[...]).astype(BF16)
                w_hi = w.astype(BF16)
                w_lo = (w - w_hi.astype(F32)).astype(BF16)
                ones_t = jnp.ones((T, LANES), BF16)
                cslot_scr[chunk, :] = _dot_tn(w_hi, ones_t) + _dot_tn(w_lo, ones_t)

            @pl.when(used <= cb * MOE_CB)
            def _():
                xy_scr[chunk, :] = jnp.zeros((MOE_CB, xy_scr.shape[1]), BF16)
                cslot_scr[chunk, :] = jnp.zeros((MOE_CB, LANES), F32)

    base = off_ref[i, e]
    nb = nsb_ref[i, e]

    def ffn_block(r0, n_rows):
        rows = pl.ds(pl.multiple_of(base + r0, SB), n_rows)
        arows = pl.ds(pl.multiple_of(r0, SB), n_rows)
        xb = xy_scr[rows, :]
        gte = _dot(xb, wg_ref[0])
        up = _dot(xb, wu_ref[0])
        part = _dot((gte * _sigmoid(gte) * up).astype(BF16), wd_ref[0])

        @pl.when(j == 0)
        def _():
            acc_scr[arows, :] = part

        @pl.when(jnp.logical_and(j > 0, j < last_j))
        def _():
            acc_scr[arows, :] += part

        @pl.when(j == last_j)
        def _():
            xy_scr[rows, :] = ((acc_scr[arows, :] + part) * cslot_scr[rows, 0:1]).astype(BF16)

    def big_body(p, carry):
        ffn_block(p * (MOE_MAXB * SB), MOE_MAXB * SB)
        return carry

    lax.fori_loop(0, nb // MOE_MAXB, big_body, 0)
    for rest in range(1, MOE_MAXB):
        @pl.when(nb % MOE_MAXB == rest)
        def _():
            ffn_block((nb - rest) * SB, rest * SB)

    @pl.when(jnp.logical_and(e == pl.num_programs(1) - 1, j == last_j))
    def _():
        s1, s2, _, _ = _moe_slots(slot_scr, comb_ref)
        total = jnp.zeros((T, xy_scr.shape[1]), F32)
        for cb in range(S // MOE_CB):
            slot = (lax.broadcasted_iota(jnp.int32, (T, MOE_CB), 1) + cb * MOE_CB).astype(F32)
            pt = jnp.where(jnp.logical_or(s1 == slot, s2 == slot), 1.0, 0.0).astype(BF16)
            total = total + _dot(pt, xy_scr[cb * MOE_CB:(cb + 1) * MOE_CB, :])
        o_ref[...] = x_ref[...] + gf_ref[...] * total


def ffn_moe(h, x, comb, w_gu, w_down, g_f, tm=1024, tf=896):
    L, D = x.shape
    E, Fh, _ = w_down.shape
    nf = Fh // tf
    assert nf >= 2 and E == N_EXPERTS
    nt = L // tm
    counts = jnp.sum((comb[:, :E] > 0.0).reshape(nt, tm, E), axis=1, dtype=jnp.int32)
    nsb = (counts + (MOE_SB - 1)) // MOE_SB
    off = (jnp.cumsum(nsb, axis=1) - nsb) * MOE_SB
    idx = lambda f: (lambda i, e, j, n, o: f(i, e, j))
    grid_spec = pltpu.PrefetchScalarGridSpec(
        num_scalar_prefetch=2,
        grid=(nt, E, nf),
        in_specs=[pl.BlockSpec((tm, D), idx(lambda i, e, j: (i, 0))),
                  pl.BlockSpec((tm, D), idx(lambda i, e, j: (i, 0))),
                  pl.BlockSpec((tm, LANES), idx(lambda i, e, j: (i, 0))),
                  pl.BlockSpec((1, D, tf), idx(lambda i, e, j: (e, 0, j))),
                  pl.BlockSpec((1, D, tf), idx(lambda i, e, j: (e, 0, j + nf))),
                  pl.BlockSpec((1, tf, D), idx(lambda i, e, j: (e, j, 0))),
                  pl.BlockSpec((1, D), idx(lambda i, e, j: (0, 0)))],
        out_specs=pl.BlockSpec((tm, D), idx(lambda i, e, j: (i, 0))),
        scratch_shapes=[pltpu.VMEM((tm, LANES), F32),
                        pltpu.VMEM((TOP_K * tm + E * MOE_SB, LANES), F32),
                        pltpu.VMEM((TOP_K * tm + E * MOE_SB, D), BF16),
                        pltpu.VMEM((tm, D), F32)],
    )
    return pl.pallas_call(
        _moe_kernel,
        grid_spec=grid_spec,
        out_shape=jax.ShapeDtypeStruct((L, D), F32),
        compiler_params=_cparams("arbitrary", "arbitrary", "arbitrary"),
    )(nsb, off, h, x, comb, w_gu, w_gu, w_down, g_f)


def _seg_matrix(width, head_dim):
    idx = jnp.arange(width) // head_dim
    return (idx[:, None] == idx[None, :]).astype(BF16)


def _pad_cols(w, n):
    return jnp.pad(w, ((0, 0), (0, n - w.shape[1])))


def _pad_rows(w, n):
    return jnp.pad(w, ((0, n - w.shape[0]), (0, 0)))


def rwkv_branch(h, w_rw, p, v_first):
    r, k, v, lw, a, b, g, bonus = rw_pre(h, w_rw, p, v_first)
    return wkv7(r, lw, k, v, a, b), bonus, g, v


def fox_branch(h, w_fox, p):
    q, k, v, F = fox_pre(h, w_fox, p)
    FT = jnp.transpose(F[:, :FOX_HEADS])
    lo = fox_first_block(FT, p["bound2"])
    return fox_attention(q, k, v, FT, lo)


def kernel(x, c, ada_w, ada_b, norm_mix, norm_ffn, w_in, rw_mu, rw_w0, rw_w_up, rw_a0, rw_a_up, rw_g_up, rw_k_k, rw_k_a, rw_r_k, rw_lnx_w, rw_lnx_b, rw_v0, rw_v_down, rw_v_up, ssd_conv_w, ssd_conv_b, ssd_dt_bias, ssd_a_log, ssd_d, ssd_norm, fox_f_bias, fox_q_gain, fox_k_gain, gate_b, proj_rw, proj_ssd, proj_fox, w_out, ffn_w_gu, ffn_w_down, moe_router, moe_w_gu, moe_w_down):
    depth = w_in.shape[0]
    D = D_MODEL
    xs = x[0]
    row = lambda t: t.reshape(1, -1).astype(F32)
    seg64 = _seg_matrix(RW_WIDTH, RW_HEAD_DIM)
    rw_cols = 3 * RW_WIDTH + RW_DECAY_LORA + RW_AAA_LORA + RW_GATE_LORA
    ssd_cols = SSD_WIDTH + SSD_XBC + SSD_HEADS
    fox_cols = 3 * FOX_WIDTH + FOX_HEADS
    expand = (jnp.arange(LANES)[:, None] == (jnp.arange(SSD_WIDTH) // SSD_HEAD_DIM)[None, :]).astype(BF16)
    v_first = None
    mods = []
    for l in range(depth):
        mod = adaln_mod(c, ada_w[l], ada_b[l])
        mods.append([mod[:, i * D:(i + 1) * D] for i in range(6)])
    h = None
    for l in range(depth):
        sh_m, sc_m, g_m, sh_f, sc_f, g_f = mods[l]
        if h is None:
            h = norm_mod(xs, row(norm_mix[l]), sc_m, sh_m)

        wl = w_in[l]
        o = 0
        w_r = wl[:, o:o + rw_cols]; o += rw_cols
        w_s = wl[:, o:o + ssd_cols]; o += ssd_cols
        w_f = wl[:, o:o + fox_cols]; o += fox_cols
        w_g = wl[:, o:]
        W3 = 3 * RW_WIDTH
        o_a = W3 + RW_DECAY_LORA
        o_g = o_a + RW_AAA_LORA

        def rw_layout(t, hv):
            parts = [t[:, :W3], _pad_cols(t[:, W3:o_a], LANES), _pad_cols(t[:, o_a:o_g], LANES),
                     _pad_cols(t[:, o_g:], 2 * LANES), _pad_cols(hv, LANES)]
            return jnp.concatenate(parts, axis=1)

        if l == 0:
            hv_w = jnp.zeros((D, RW_VRES_LORA), F32)
        else:
            hv_w = rw_v_down[l - 1]
        w_rw = rw_layout(w_r, hv_w).astype(BF16)
        mu = rw_layout(rw_mu[l].reshape(1, -1), jnp.zeros((1, RW_VRES_LORA), F32))
        rw_p = dict(mu=mu, w0=row(rw_w0[l]), w_up=_pad_rows(rw_w_up[l], LANES).astype(BF16), a0=row(rw_a0[l]),
                    a_up=_pad_rows(rw_a_up[l], LANES).astype(BF16),
                    g_up=_pad_rows(rw_g_up[l], 2 * LANES).astype(BF16),
                    k_k=row(rw_k_k[l]), k_a=row(rw_k_a[l]), r_k=row(rw_r_k[l]), seg=seg64)
        if l > 0:
            rw_p.update(v0=row(rw_v0[l - 1]), v_up=_pad_rows(rw_v_up[l - 1], LANES).astype(BF16))
        yn, bonus, g_rw, v_cur = rwkv_branch(h, w_rw, rw_p, v_first)
        if l == 0:
            v_first = v_cur

        w_ssd = jnp.concatenate([w_s[:, SSD_WIDTH:SSD_WIDTH + SSD_XBC], w_s[:, :SSD_WIDTH],
                                 _pad_cols(w_s[:, SSD_WIDTH + SSD_XBC:], LANES)], axis=1).astype(BF16)
        A = -jnp.exp(ssd_a_log[l].astype(F32))
        ssd_p = dict(conv_w=ssd_conv_w[l], conv_b=row(ssd_conv_b[l]),
                     dt_bias=_pad_cols(row(ssd_dt_bias[l]), LANES),
                     A=_pad_cols(row(A), LANES), expand=expand,
                     d_skip=row(jnp.repeat(ssd_d[l], SSD_HEAD_DIM)), norm_w=row(ssd_norm[l]))
        y_ssd = ssd_mixer(h, w_ssd, ssd_p)

        fb = fox_f_bias[l]
        hid = jnp.arange(FOX_HEADS)
        ahead = jnp.logical_or(fb[None, :] < fb[:, None],
                               jnp.logical_and(fb[None, :] == fb[:, None], hid[None, :] < hid[:, None]))
        rank = jnp.sum(ahead, axis=1)
        head_order = jnp.sum(jnp.where(rank[None, :] == hid[:, None], hid[None, :], 0), axis=1)
        by_head = lambda t: jnp.take(t.reshape(D, FOX_HEADS, FOX_HEAD_DIM), head_order, axis=1).reshape(D, FOX_WIDTH)
        w_fox = jnp.concatenate(
            [by_head(w_f[:, i * FOX_WIDTH:(i + 1) * FOX_WIDTH]) for i in range(3)]
            + [_pad_cols(jnp.take(w_f[:, 3 * FOX_WIDTH:], head_order, axis=1), LANES)], axis=1).astype(BF16)
        f_bias_sorted = jnp.take(fox_f_bias[l], head_order)
        proj_fox_sorted = jnp.take(proj_fox[l].reshape(FOX_HEADS, FOX_HEAD_DIM, D), head_order,
                                   axis=0).reshape(FOX_WIDTH, D)
        bound2 = (1.02 * FOX_HEAD_DIM ** 0.5 * LOG2E) * jnp.max(jnp.abs(fox_q_gain[l])) * jnp.max(jnp.abs(fox_k_gain[l]))
        fox_p = dict(f_bias=_pad_cols(row(f_bias_sorted), LANES),
                     q_gain=row(jnp.tile(fox_q_gain[l], FOX_HEADS)) * (FOX_HEAD_DIM ** -0.5 * LOG2E),
                     k_gain=row(jnp.tile(fox_k_gain[l], FOX_HEADS)), seg=seg64, bound2=bound2)
        y_fox = fox_branch(h, w_fox, fox_p)

        mp = dict(w_gate=w_g.astype(BF16), gate_b=row(gate_b[l]), lnx_w=row(rw_lnx_w[l]), lnx_b=row(rw_lnx_b[l]),
                  proj_rw=proj_rw[l].astype(BF16), proj_ssd=proj_ssd[l].astype(BF16),
                  proj_fox=proj_fox_sorted.astype(BF16), w_out=w_out[l].astype(BF16), g_m=g_m,
                  norm_gain=row(norm_ffn[l]), sc_f=sc_f, sh_f=sh_f)

        if l % 2 == 0:
            xs, hf = merge(xs, h, yn, bonus, g_rw, y_ssd, y_fox, mp)
            next_norm = None
            if l + 1 < depth:
                next_norm = (row(norm_mix[l + 1]), mods[l + 1][1], mods[l + 1][0])
            xs, h = ffn_dense(hf, xs, ffn_w_gu[l // 2].astype(BF16), ffn_w_down[l // 2].astype(BF16), g_f,
                              next_norm)
        else:
            xs, hf, comb = merge(xs, h, yn, bonus, g_rw, y_ssd, y_fox, mp,
                                 router=_pad_cols(moe_router[l // 2], LANES))
            xs = ffn_moe(hf, xs, comb, moe_w_gu[l // 2].astype(BF16), moe_w_down[l // 2].astype(BF16), g_f)
            h = None
    return xs[None]
```

```python
import functools
import math

import jax
import jax.numpy as jnp
from jax import lax
from jax.experimental import pallas as pl
from jax.experimental.pallas import tpu as pltpu

F32 = jnp.float32
BF16 = jnp.bfloat16
HIGHEST = lax.Precision.HIGHEST

D_MODEL = 1024
LANES = 128
SUBLANES = 8

RW_HEADS = 8
RW_HEAD_DIM = 64
RW_WIDTH = RW_HEADS * RW_HEAD_DIM
RW_DECAY_LORA = 64
RW_AAA_LORA = 64
RW_VRES_LORA = 32
RW_GATE_LORA = 160
GN_EPS = 64e-5
RW_OFF_WLO = 3 * RW_WIDTH
RW_OFF_ALO = RW_OFF_WLO + LANES
RW_OFF_GLO = RW_OFF_ALO + LANES
RW_OFF_HV = RW_OFF_GLO + 2 * LANES
RW_PAD_COLS = RW_OFF_HV + LANES
RW_CHUNK = 64

SSD_HEADS = 16
SSD_HEAD_DIM = 64
SSD_WIDTH = SSD_HEADS * SSD_HEAD_DIM
SSD_GROUPS = 4
SSD_STATE = 128
SSD_CONV = 4
SSD_CHUNK = 128
SSD_XBC = SSD_WIDTH + 2 * SSD_GROUPS * SSD_STATE
SSD_GROUP_WIDTH = SSD_WIDTH // SSD_GROUPS

FOX_HEADS = 8
FOX_HEAD_DIM = 64
FOX_WIDTH = FOX_HEADS * FOX_HEAD_DIM
FOX_TQ = 512
FOX_TK = 512
LOG2E = 1.4426950408889634

FFN_DENSE = 2816
N_EXPERTS = 8
TOP_K = 2
FFN_EXPERT = 3584
MOE_SB = 32
MOE_MAXB = 16
MOE_CB = 512
EPS = 1e-6

VMEM_LIMIT = 56 * 1024 * 1024


def _cparams(*sem):
    return pltpu.CompilerParams(dimension_semantics=sem, vmem_limit_bytes=VMEM_LIMIT)


def _sigmoid(x):
    return 1.0 / (1.0 + jnp.exp(-x))


def _softplus(x):
    return jnp.maximum(x, 0.0) + jnp.log(1.0 + jnp.exp(-jnp.abs(x)))


def _dot(a, b):
    return jnp.dot(a, b, preferred_element_type=F32)


def _dot_bf16(a, b):
    return jnp.dot(a.astype(BF16), b.astype(BF16), preferred_element_type=F32)


def _dot_hi(a, b):
    return jnp.dot(a, b, precision=HIGHEST, preferred_element_type=F32)


def _dot_nt(a, b, precision=None):
    return lax.dot_general(a, b, (((1,), (1,)), ((), ())), precision=precision, preferred_element_type=F32)


def _dot_tn(a, b, precision=None):
    return lax.dot_general(a, b, (((0,), (0,)), ((), ())), precision=precision, preferred_element_type=F32)


def _split_dot(x, ones_bf16):
    return _dot(x.astype(BF16), ones_bf16)


def _split3(x):
    hi = x.astype(BF16)
    r1 = x - hi.astype(F32)
    mid = r1.astype(BF16)
    return hi, mid, (r1 - mid.astype(F32)).astype(BF16)


def _tril(n, strict=False):
    r = lax.broadcasted_iota(jnp.int32, (n, n), 0)
    c = lax.broadcasted_iota(jnp.int32, (n, n), 1)
    return (r > c) if strict else (r >= c)


def _mod_kernel(c_ref, w_ref, b_ref, o_ref):
    c = c_ref[...]
    o_ref[...] = _dot_hi(c * _sigmoid(c), w_ref[...]) + b_ref[...]


def adaln_mod(c, w, b):
    d, n = w.shape
    tn = 1024
    c8 = jnp.broadcast_to(c, (SUBLANES, d))
    out = pl.pallas_call(
        _mod_kernel,
        grid=(n // tn,),
        in_specs=[pl.BlockSpec((SUBLANES, d), lambda j: (0, 0)),
                  pl.BlockSpec((d, tn), lambda j: (0, j)),
                  pl.BlockSpec((1, tn), lambda j: (0, j))],
        out_specs=pl.BlockSpec((SUBLANES, tn), lambda j: (0, j)),
        out_shape=jax.ShapeDtypeStruct((SUBLANES, n), F32),
        compiler_params=_cparams("arbitrary"),
    )(c8, w, b.reshape(1, n))
    return out[:1]


def _norm_mod(x, gain, sc, sh):
    y = x * lax.rsqrt(jnp.mean(x * x, axis=-1, keepdims=True) + EPS)
    return y * gain * (1.0 + sc) + sh


def _norm_kernel(x_ref, gain_ref, sc_ref, sh_ref, h_ref):
    h_ref[...] = _norm_mod(x_ref[...], gain_ref[...], sc_ref[...], sh_ref[...]).astype(h_ref.dtype)


def norm_mod(x, gain, sc, sh, tm=512):
    L, d = x.shape
    row = pl.BlockSpec((1, d), lambda i: (0, 0))
    return pl.pallas_call(
        _norm_kernel,
        grid=(L // tm,),
        in_specs=[pl.BlockSpec((tm, d), lambda i: (i, 0)), row, row, row],
        out_specs=pl.BlockSpec((tm, d), lambda i: (i, 0)),
        out_shape=jax.ShapeDtypeStruct((L, d), BF16),
        compiler_params=_cparams("arbitrary"),
    )(x, gain, sc, sh)


def _rw_pre_kernel(has_vres, *refs):
    if has_vres:
        (h_ref, w_ref, mu_ref, w0_ref, wup_ref, a0_ref, aup_ref, gup_ref, kk_ref, ka_ref, rk_ref, seg_ref,
         vfirst_ref, v0_ref, vup_ref,
         r_out, k_out, v_out, lw_out, a_out, b_out, g_out, bonus_out, ext_scr) = refs
    else:
        (h_ref, w_ref, mu_ref, w0_ref, wup_ref, a0_ref, aup_ref, gup_ref, kk_ref, ka_ref, rk_ref, seg_ref,
         r_out, k_out, v_out, lw_out, a_out, b_out, g_out, bonus_out, ext_scr) = refs
    T = h_ref.shape[0]

    @pl.when(pl.program_id(0) == 0)
    def _():
        ext_scr[0:SUBLANES, :] = jnp.zeros((SUBLANES, ext_scr.shape[1]), F32)

    cur = _dot(h_ref[...], w_ref[...])
    ext_scr[SUBLANES:SUBLANES + T, :] = cur
    prev = ext_scr[SUBLANES - 1:SUBLANES - 1 + T, :]
    ext_scr[0:SUBLANES, :] = cur[T - SUBLANES:T, :]
    s = cur + (prev - cur) * mu_ref[...]

    W = RW_WIDTH
    r = s[:, 0:W]
    k = s[:, W:2 * W]
    v = s[:, 2 * W:3 * W]
    w_lo = s[:, RW_OFF_WLO:RW_OFF_WLO + LANES]
    a_lo = s[:, RW_OFF_ALO:RW_OFF_ALO + LANES]
    g_lo = s[:, RW_OFF_GLO:RW_OFF_GLO + 2 * LANES]
    seg = seg_ref[...]

    wlog = -_softplus(-(w0_ref[...] + _dot_bf16(jnp.tanh(w_lo), wup_ref[...]))) - 0.5
    a = _sigmoid(a0_ref[...] + _dot_bf16(a_lo, aup_ref[...]))
    g = _dot_bf16(_sigmoid(g_lo), gup_ref[...])
    if has_vres:
        hv = s[:, RW_OFF_HV:RW_OFF_HV + LANES]
        v = v + (vfirst_ref[...].astype(F32) - v) * _sigmoid(v0_ref[...] + _dot_bf16(hv, vup_ref[...]))
    kk = k * kk_ref[...]
    kk = kk / jnp.maximum(jnp.sqrt(_split_dot(kk * kk, seg)), 1e-12)
    k = k * (1.0 + (a - 1.0) * ka_ref[...])
    r_out[...] = r.astype(r_out.dtype)
    k_out[...] = k.astype(k_out.dtype)
    v_out[...] = v.astype(v_out.dtype)
    lw_out[...] = -jnp.exp(wlog)
    a_out[...] = (-kk).astype(a_out.dtype)
    b_out[...] = (kk * a).astype(b_out.dtype)
    g_out[...] = g.astype(g_out.dtype)
    bonus_out[...] = (_split_dot(r * k * rk_ref[...], seg) * v).astype(bonus_out.dtype)


def rw_pre(h, w_rw, p, v_first, tm=512):
    L, D = h.shape
    has_vres = v_first is not None
    full = lambda a: pl.BlockSpec(a.shape, lambda i: (0,) * a.ndim)
    rows = lambda n: pl.BlockSpec((tm, n), lambda i: (i, 0))
    args = [h, w_rw, p["mu"], p["w0"], p["w_up"], p["a0"], p["a_up"], p["g_up"], p["k_k"], p["k_a"], p["r_k"],
            p["seg"]]
    specs = [rows(D)] + [full(a) for a in args[1:]]
    if has_vres:
        args += [v_first, p["v0"], p["v_up"]]
        specs += [rows(RW_WIDTH), full(p["v0"]), full(p["v_up"])]
    out_dtypes = [BF16, BF16, BF16, F32, BF16, BF16, BF16, BF16]
    return pl.pallas_call(
        functools.partial(_rw_pre_kernel, has_vres),
        grid=(L // tm,),
        in_specs=specs,
        out_specs=[rows(RW_WIDTH)] * 8,
        out_shape=[jax.ShapeDtypeStruct((L, RW_WIDTH), dt) for dt in out_dtypes],
        scratch_shapes=[pltpu.VMEM((tm + SUBLANES, RW_PAD_COLS), F32)],
        compiler_params=_cparams("arbitrary"),
    )(*args)


def _wkv_kernel(r_ref, lw_ref, k_ref, v_ref, a_ref, b_ref, y_ref, h_scr):
    C = RW_CHUNK
    C2 = 2 * C
    n_chunks = r_ref.shape[0] // C

    @pl.when(pl.program_id(1) == 0)
    def _():
        h_scr[...] = jnp.zeros_like(h_scr)

    is_a = lax.broadcasted_iota(jnp.int32, (C, LANES), 1) < RW_HEAD_DIM
    ri = lax.broadcasted_iota(jnp.int32, (C2, C2), 0)
    ci = lax.broadcasted_iota(jnp.int32, (C2, C2), 1)
    same = (ri >= C) == (ci >= C)
    strict = jnp.logical_and(same, ri > ci)
    incl = jnp.logical_and(same, ri >= ci)
    eye2 = (ri == ci).astype(F32)
    tri_c = _tril(C).astype(BF16)

    def two(x):
        return jnp.concatenate([jnp.where(is_a, x, 0.0), jnp.where(is_a, 0.0, x)], axis=0)

    chunks = range(n_chunks)
    at2, rt2, bh2, kh2, v2, n_ab, a_ak, m_rb, m_rk, p_end = ([] for _ in range(10))
    for c in chunks:
        sl = pl.ds(c * C, C)
        lw = lw_ref[sl, :]
        lw_hi = lw.astype(BF16)
        lw_lo = (lw - lw_hi.astype(F32)).astype(BF16)
        cs = _dot(tri_c, lw_hi) + _dot(tri_c, lw_lo)
        cs_end = cs[C - 1:C, :]
        e_neg = jnp.exp(-cs)
        e_end = jnp.exp(cs_end - cs)
        a = a_ref[sl, :].astype(F32)
        b = b_ref[sl, :].astype(F32)
        k = k_ref[sl, :].astype(F32)
        at2.append(two(a * jnp.exp(cs - lw)))
        rt2.append(two(r_ref[sl, :].astype(F32) * jnp.exp(cs)))
        bh2.append(two(b * e_end))
        kh2.append(two(k * e_end))
        v2.append(two(v_ref[sl, :].astype(F32)))
        p_end.append(jnp.exp(cs_end))
        quad = _dot_nt(jnp.concatenate([at2[c], rt2[c]], axis=0).astype(BF16),
                       jnp.concatenate([two(b * e_neg), two(k * e_neg)], axis=0).astype(BF16))
        n_ab.append(jnp.where(strict, quad[0:C2, 0:C2], 0.0))
        a_ak.append(jnp.where(strict, quad[0:C2, C2:], 0.0))
        m_rb.append(jnp.where(incl, quad[C2:, 0:C2], 0.0))
        m_rk.append(jnp.where(incl, quad[C2:, C2:], 0.0))
    t_inv = [eye2 + n for n in n_ab]
    n_pow = n_ab
    n_pow = [_dot_bf16(n, n) for n in n_pow]
    for _ in range(int(math.log2(C)) - 2):
        both = [_dot_bf16(jnp.concatenate([n, t], axis=0), n) for n, t in zip(n_pow, t_inv)]
        t_inv = [t + b[C2:, :] for t, b in zip(t_inv, both)]
        n_pow = [b[:C2, :] for b in both]
    t_inv = [t + _dot_bf16(t, n) for t, n in zip(t_inv, n_pow)]
    akv = [_dot_bf16(a_ak[c], v2[c]) for c in chunks]
    au = [_dot_bf16(t_inv[c], jnp.concatenate([at2[c], akv[c]], axis=1)) for c in chunks]
    mau = [_dot_bf16(m_rb[c], au[c]) for c in chunks]
    mkv = [_dot_bf16(m_rk[c], v2[c]) for c in chunks]
    gu = [_dot_tn(bh2[c].astype(BF16), au[c].astype(BF16)) for c in chunks]
    kv = [_dot_tn(kh2[c].astype(BF16), v2[c].astype(BF16)) for c in chunks]
    inv_n = 1.0 / RW_HEAD_DIM
    h = h_scr[...]
    for c in chunks:
        rh2 = rt2[c] + mau[c][:, :LANES]
        g_mat = eye2 * p_end[c] + gu[c][:, :LANES]
        yh = _dot_bf16(jnp.concatenate([rh2, g_mat], axis=0), h)
        h = yh[C2:, :] + gu[c][:, LANES:] + kv[c]
        y2 = yh[:C2, :] + mau[c][:, LANES:] + mkv[c]
        y = y2[:C, :] + y2[C:, :]
        s_all = jnp.sum(y, axis=-1, keepdims=True)
        s_a = jnp.sum(jnp.where(is_a, y, 0.0), axis=-1, keepdims=True)
        yc = y - jnp.where(is_a, s_a, s_all - s_a) * inv_n
        sq = yc * yc
        q_all = jnp.sum(sq, axis=-1, keepdims=True)
        q_a = jnp.sum(jnp.where(is_a, sq, 0.0), axis=-1, keepdims=True)
        yn = yc * lax.rsqrt(jnp.where(is_a, q_a, q_all - q_a) * inv_n + GN_EPS)
        y_ref[pl.ds(c * C, C), :] = yn.astype(y_ref.dtype)
    h_scr[...] = h


def wkv7(r, lw, k, v, a, b, rows=2048):
    L, W = r.shape
    spec = pl.BlockSpec((rows, LANES), lambda p, i: (i, p))
    return pl.pallas_call(
        _wkv_kernel,
        grid=(W // LANES, L // rows),
        in_specs=[spec] * 6,
        out_specs=spec,
        out_shape=jax.ShapeDtypeStruct((L, W), BF16),
        scratch_shapes=[pltpu.VMEM((LANES, LANES), F32)],
        compiler_params=_cparams("arbitrary", "arbitrary"),
    )(r, lw, k, v, a, b)


def _ssd_kernel(h_ref, w_ref, cw_ref, cb_ref, bias_ref, A_ref, exp_ref, dskip_ref, nw_ref, o_ref,
                ext_scr, xbc_scr, s_scr):
    Q = SSD_CHUNK
    GW = SSD_GROUP_WIDTH
    NS = SSD_STATE
    R = h_ref.shape[0]

    @pl.when(pl.program_id(0) == 0)
    def _():
        s_scr[...] = jnp.zeros_like(s_scr)
        ext_scr[0:SUBLANES, :] = jnp.zeros((SUBLANES, ext_scr.shape[1]), F32)

    cols = _dot(h_ref[...], w_ref[...])
    ext_scr[SUBLANES:SUBLANES + R, :] = cols[:, 0:SSD_XBC]
    acc = cols[:, 0:SSD_XBC] * cw_ref[SSD_CONV - 1:SSD_CONV, :] + cb_ref[...]
    for j in range(1, SSD_CONV):
        acc = acc + ext_scr[SUBLANES - j:SUBLANES - j + R, :] * cw_ref[SSD_CONV - 1 - j:SSD_CONV - j, :]
    ext_scr[0:SUBLANES, :] = ext_scr[R:R + SUBLANES, :]
    xbc_scr[...] = acc * _sigmoid(acc)

    incl = _tril(Q)
    tri = incl.astype(BF16)
    triu = jnp.logical_not(_tril(Q, strict=True)).astype(BF16)
    expand = exp_ref[...]
    first_half = lax.broadcasted_iota(jnp.int32, (Q, LANES), 1) < SSD_HEAD_DIM
    heads_per_group = SSD_HEADS // SSD_GROUPS
    for c in range(R // Q):
        rows = slice(c * Q, (c + 1) * Q)
        xs = xbc_scr[rows, 0:SSD_WIDTH]
        z = cols[rows, SSD_XBC:SSD_XBC + SSD_WIDTH]
        dt = _softplus(cols[rows, SSD_XBC + SSD_WIDTH:] + bias_ref[...])
        a3 = _split3(dt * A_ref[...])
        a_cum = sum(_dot(tri, t) for t in a3)
        a_cumT = sum(_dot_tn(t, triu) for t in a3)
        a_end = a_cum[Q - 1:Q, :]
        dt_x = _dot(dt.astype(BF16), expand)
        eac_x = _dot(jnp.exp(a_cum).astype(BF16), expand)
        dte_x = _dot(jnp.exp(a_end - a_cum).astype(BF16), expand)
        cd_x = sum(_dot(t, expand) for t in _split3(jnp.broadcast_to(jnp.exp(a_end), (SUBLANES, LANES))))[0:1, :]
        xdt = xs * dt_x
        for g in range(SSD_GROUPS):
            Bg = xbc_scr[rows, SSD_WIDTH + g * NS:SSD_WIDTH + (g + 1) * NS].astype(BF16)
            Cg = xbc_scr[rows, SSD_WIDTH + (SSD_GROUPS + g) * NS:SSD_WIDTH + (SSD_GROUPS + g + 1) * NS].astype(BF16)
            cb = _dot_nt(Cg, Bg)
            gs = slice(g * GW, (g + 1) * GW)
            s_prev = s_scr[:, gs]
            y_g = _dot_bf16(Cg, s_prev) * eac_x[:, gs]
            pieces = []
            for pr in range(heads_per_group // 2):
                ps = slice(g * GW + pr * LANES, g * GW + (pr + 1) * LANES)
                xdt_p = xdt[:, ps]
                acc_p = None
                for e in range(2):
                    hd = g * heads_per_group + pr * 2 + e
                    seg = a_cum[:, hd:hd + 1] - a_cumT[hd:hd + 1, :]
                    m = cb * jnp.exp(jnp.where(incl, seg, -jnp.inf))
                    xm = jnp.where(first_half if e == 0 else jnp.logical_not(first_half), xdt_p, 0.0)
                    t = _dot_bf16(m, xm)
                    acc_p = t if acc_p is None else acc_p + t
                pieces.append(acc_p)
            y_g = y_g + jnp.concatenate(pieces, axis=1)
            s_scr[:, gs] = s_prev * cd_x[:, gs] +_dot_tn(Bg, (dte_x[:, gs] * xdt[:, gs]).astype(BF16))
            y_g = y_g + dskip_ref[:, gs] * xs[:, gs]
            zg = z[:, gs]
            y_g = y_g * (zg * _sigmoid(zg))
            y_g = y_g * lax.rsqrt(jnp.mean(y_g * y_g, axis=-1, keepdims=True) + EPS)
            o_ref[rows, gs] = (y_g * nw_ref[:, gs]).astype(o_ref.dtype)


def ssd_mixer(h, w_ssd, p, rows=512):
    L, D = h.shape
    N = w_ssd.shape[1]
    full = lambda a: pl.BlockSpec(a.shape, lambda i: (0,) * a.ndim)
    small = [p["conv_w"], p["conv_b"], p["dt_bias"], p["A"], p["expand"], p["d_skip"], p["norm_w"]]
    return pl.pallas_call(
        _ssd_kernel,
        grid=(L // rows,),
        in_specs=[pl.BlockSpec((rows, D), lambda i: (i, 0)), full(w_ssd)] + [full(a) for a in small],
        out_specs=pl.BlockSpec((rows, SSD_WIDTH), lambda i: (i, 0)),
        out_shape=jax.ShapeDtypeStruct((L, SSD_WIDTH), BF16),
        scratch_shapes=[pltpu.VMEM((rows + SUBLANES, SSD_XBC), F32), pltpu.VMEM((rows, SSD_XBC), F32),
                        pltpu.VMEM((SSD_STATE, SSD_WIDTH), F32)],
        compiler_params=_cparams("arbitrary"),
    )(h, w_ssd, *small)


def _fox_pre_kernel(h_ref, w_ref, fb_ref, qg_ref, kg_ref, seg_ref, q_out, k_out, v_out, f_out, carry_scr):
    T = h_ref.shape[0]
    W = FOX_WIDTH

    @pl.when(pl.program_id(0) == 0)
    def _():
        carry_scr[...] = jnp.zeros_like(carry_scr)

    cols = _dot(h_ref[...], w_ref[...])
    seg = seg_ref[...]
    q = cols[:, 0:W]
    k = cols[:, W:2 * W]
    inv_d = 1.0 / FOX_HEAD_DIM
    qn = q * lax.rsqrt(_split_dot(q * q, seg) * inv_d + EPS) * qg_ref[...]
    kn = k * lax.rsqrt(_split_dot(k * k, seg) * inv_d + EPS) * kg_ref[...]
    q_out[...] = qn.astype(q_out.dtype)
    k_out[...] = kn.astype(k_out.dtype)
    v_out[...] = cols[:, 2 * W:3 * W].astype(v_out.dtype)
    f = cols[:, 3 * W:3 * W + LANES] + fb_ref[...]
    ls = -_softplus(-f)
    cum = _dot_hi(_tril(T).astype(F32), ls) + carry_scr[0:1, :]
    f_out[...] = cum * LOG2E
    carry_scr[...] = jnp.broadcast_to(cum[T - 1:T, :], carry_scr.shape)


def fox_pre(h, w_fox, p, tm=512):
    L, D = h.shape
    full = lambda a: pl.BlockSpec(a.shape, lambda i: (0,) * a.ndim)
    rows = lambda n: pl.BlockSpec((tm, n), lambda i: (i, 0))
    small = [w_fox, p["f_bias"], p["q_gain"], p["k_gain"], p["seg"]]
    return pl.pallas_call(
        _fox_pre_kernel,
        grid=(L // tm,),
        in_specs=[rows(D)] + [full(a) for a in small],
        out_specs=[rows(FOX_WIDTH)] * 3 + [rows(LANES)],
        out_shape=[jax.ShapeDtypeStruct((L, FOX_WIDTH), BF16)] * 3 + [jax.ShapeDtypeStruct((L, LANES), F32)],
        scratch_shapes=[pltpu.VMEM((SUBLANES, LANES), F32)],
        compiler_params=_cparams("arbitrary"),
    )(h, *small)


def _fox_kernel(lo_ref, q_ref, k_ref, v_ref, fk_ref, o_ref, m_scr, acc_scr, s0_scr, s1_scr):
    tq = q_ref.shape[0]
    tk = FOX_TK
    pair = pl.program_id(0)
    qi = pl.program_id(1)
    is_a = lax.broadcasted_iota(jnp.int32, (tq, LANES), 1) < FOX_HEAD_DIM
    q = q_ref[...]
    zero = jnp.zeros_like(q)
    qs = (jnp.where(is_a, q, zero), jnp.where(is_a, zero, q))
    m_scr[...] = jnp.full(m_scr.shape, -jnp.inf, F32)
    key_is_a = lax.broadcasted_iota(jnp.int32, (tk, LANES), 1) < FOX_HEAD_DIM
    acc_scr[...] = jnp.zeros_like(acc_scr)
    rel = lax.broadcasted_iota(jnp.int32, (tq, tk), 0) - lax.broadcasted_iota(jnp.int32, (tq, tk), 1)
    last = (qi * tq) // tk

    def fill(s_ref, kb, diagonal=False):
        ks = pl.ds(pl.multiple_of(kb * tk, tk), tk)
        k_blk = k_ref[ks, :]
        for e in range(2):
            s = _dot_nt(qs[e], k_blk) - fk_ref[pl.ds(2 * pair + e, 1), ks]
            if diagonal:
                s = jnp.where(rel >= kb * tk - qi * tq, s, -jnp.inf)
            s_ref[e] = s

    def consume(s_ref, kb):
        v_blk = v_ref[pl.ds(pl.multiple_of(kb * tk, tk), tk), :]
        one = jnp.ones_like(v_blk)
        v_one = (jnp.where(key_is_a, v_blk, one), jnp.where(key_is_a, one, v_blk))
        for e in range(2):
            s = s_ref[e]
            m_old = m_scr[e]
            m_new = jnp.maximum(m_old, jnp.max(s, axis=-1, keepdims=True))
            pexp = jnp.exp2(s - m_new)
            acc_scr[e] = jnp.exp2(m_old - m_new) * acc_scr[e] + _dot(pexp.astype(BF16), v_one[e])
            m_scr[e] = m_new

    n = last - jnp.minimum(lo_ref[2 * pair, qi], lo_ref[2 * pair + 1, qi])
    fill(s0_scr, last, diagonal=True)

    def two_steps(u, carry):
        kb = last - 2 * u
        fill(s1_scr, kb - 1)
        consume(s0_scr, kb)
        fill(s0_scr, kb - 2)
        consume(s1_scr, kb - 1)
        return carry

    lax.fori_loop(0, n // 2, two_steps, 0)

    @pl.when(n % 2 == 1)
    def _():
        fill(s1_scr, last - n)
        consume(s0_scr, last - n + 1)
        consume(s1_scr, last - n)

    @pl.when(n % 2 == 0)
    def _():
        consume(s0_scr, last - n)

    acc_a = acc_scr[0]
    acc_b = acc_scr[1]
    half = FOX_HEAD_DIM
    o_ref[...] = jnp.where(is_a, acc_a / pltpu.roll(acc_a, half, 1),
                           acc_b / pltpu.roll(acc_b, half, 1)).astype(o_ref.dtype)


def fox_attention(q, k, v, FT, lo):
    L, W = q.shape
    H = FT.shape[0]
    tq = FOX_TQ
    grid_spec = pltpu.PrefetchScalarGridSpec(
        num_scalar_prefetch=1,
        grid=(W // LANES, L // tq),
        in_specs=[pl.BlockSpec((tq, LANES), lambda p, i, lo_r: (i, p)),
                  pl.BlockSpec((L, LANES), lambda p, i, lo_r: (0, p)),
                  pl.BlockSpec((L, LANES), lambda p, i, lo_r: (0, p)),
                  pl.BlockSpec((H, L), lambda p, i, lo_r: (0, 0))],
        out_specs=pl.BlockSpec((tq, LANES), lambda p, i, lo_r: (i, p)),
        scratch_shapes=[pltpu.VMEM((2, tq, 1), F32), pltpu.VMEM((2, tq, LANES), F32),
                        pltpu.VMEM((2, tq, FOX_TK), F32), pltpu.VMEM((2, tq, FOX_TK), F32)],
    )
    return pl.pallas_call(
        _fox_kernel,
        grid_spec=grid_spec,
        out_shape=jax.ShapeDtypeStruct((L, W), BF16),
        compiler_params=_cparams("arbitrary", "arbitrary"),
    )(lo, q, k, v, FT)


def fox_first_block(FT, bound2):
    f_first = FT[:, ::FOX_TQ]
    f_last = FT[:, FOX_TK - 1::FOX_TK]
    gap = f_first[:, :, None] - f_last[:, None, :] + 2.0 * bound2
    last = (jnp.arange(f_first.shape[1]) * FOX_TQ) // FOX_TK
    needed = jnp.logical_or(gap >= -152.0, jnp.arange(f_last.shape[1])[None, None, :] >= last[None, :, None])
    return jnp.sum(jnp.cumsum(needed.astype(jnp.int32), axis=-1) == 0, axis=-1).astype(jnp.int32)


def _top2_combine(h, wr):
    T = h.shape[0]
    lane = lax.broadcasted_iota(jnp.int32, (T, LANES), 1)
    h_hi = h.astype(BF16)
    h_lo = (h - h_hi.astype(F32)).astype(BF16)
    w_hi = wr.astype(BF16)
    w_lo = (wr - w_hi.astype(F32)).astype(BF16)
    logits = _dot(h_hi, w_hi) + (_dot(h_hi, w_lo) + _dot(h_lo, w_hi))
    logits = jnp.where(lane < N_EXPERTS, logits, -jnp.inf)
    m1 = jnp.max(logits, axis=-1, keepdims=True)
    i1 = jnp.min(jnp.where(logits == m1, lane, LANES), axis=-1, keepdims=True)
    rest = jnp.where(lane == i1, -jnp.inf, logits)
    m2 = jnp.max(rest, axis=-1, keepdims=True)
    i2 = jnp.min(jnp.where(rest == m2, lane, LANES), axis=-1, keepdims=True)
    e2 = jnp.exp(m2 - m1)
    return jnp.where(lane == i1, 1.0 / (1.0 + e2), 0.0) + jnp.where(lane == i2, e2 / (1.0 + e2), 0.0)


def _merge_kernel(with_router, *refs):
    (x_ref, hm_ref, yn_ref, bonus_ref, g_ref, yssd_ref, yfox_ref, wgate_ref, gb_ref, lnw_ref, lnb_ref,
     prw_ref, pssd_ref, pfox_ref, wout_ref, gm_ref, gain_ref, sc_ref, sh_ref) = refs[:19]
    D = D_MODEL
    hm = hm_ref[...]
    y_rw = (yn_ref[...] * lnw_ref[...] + lnb_ref[...] + bonus_ref[...]) * g_ref[...]
    merged = None
    for b, (y_b, proj_ref) in enumerate(((y_rw, prw_ref), (yssd_ref[...], pssd_ref), (yfox_ref[...], pfox_ref))):
        cols = slice(b * D, (b + 1) * D)
        gate = _sigmoid(_dot(hm, wgate_ref[:, cols]) + gb_ref[:, cols])
        term = gate * _dot_bf16(y_b, proj_ref[...])
        merged = term if merged is None else merged + term
    x_new = x_ref[...] + gm_ref[...] * _dot_bf16(merged, wout_ref[...])
    hf = _norm_mod(x_new, gain_ref[...], sc_ref[...], sh_ref[...])
    if with_router:
        wr_ref, o_ref, h_ref, comb_ref = refs[19:]
        comb_ref[...] = _top2_combine(hf, wr_ref[...])
    else:
        o_ref, h_ref = refs[19:]
    o_ref[...] = x_new
    h_ref[...] = hf.astype(h_ref.dtype)


def merge(x, hm, yn, bonus, g, y_ssd, y_fox, p, router=None, tm=256):
    L, D = x.shape
    full = lambda a: pl.BlockSpec(a.shape, lambda i: (0,) * a.ndim)
    rows = lambda n: pl.BlockSpec((tm, n), lambda i: (i, 0))
    small = [p["w_gate"], p["gate_b"], p["lnx_w"], p["lnx_b"], p["proj_rw"], p["proj_ssd"], p["proj_fox"],
             p["w_out"], p["g_m"], p["norm_gain"], p["sc_f"], p["sh_f"]]
    out_specs = [rows(D), rows(D)]
    out_shape = [jax.ShapeDtypeStruct((L, D), F32), jax.ShapeDtypeStruct((L, D), BF16)]
    if router is not None:
        small.append(router)
        out_specs.append(rows(LANES))
        out_shape.append(jax.ShapeDtypeStruct((L, LANES), F32))
    return pl.pallas_call(
        functools.partial(_merge_kernel, router is not None),
        grid=(L // tm,),
        in_specs=[rows(D), rows(D), rows(RW_WIDTH), rows(RW_WIDTH), rows(RW_WIDTH), rows(SSD_WIDTH),
                  rows(FOX_WIDTH)] + [full(a) for a in small],
        out_specs=out_specs,
        out_shape=out_shape,
        compiler_params=_cparams("arbitrary"),
    )(x, hm, yn, bonus, g, y_ssd, y_fox, *small)


def _ffn_kernel(with_next, *refs):
    h_ref, x_ref, wg_ref, wu_ref, wd_ref, gf_ref = refs[:6]
    j = pl.program_id(1)
    acc_scr = refs[-1]

    @pl.when(j == 0)
    def _():
        acc_scr[...] = jnp.zeros_like(acc_scr)

    h = h_ref[...]
    gte = _dot(h, wg_ref[...])
    up = _dot(h, wu_ref[...])
    act = gte * _sigmoid(gte) * up
    acc_scr[...] += _dot(act.astype(BF16), wd_ref[...])

    @pl.when(j == pl.num_programs(1) - 1)
    def _():
        x_new = x_ref[...] + gf_ref[...] * acc_scr[...]
        if with_next:
            gain_ref, sc_ref, sh_ref, o_ref, hn_ref = refs[6:11]
            hn_ref[...] = _norm_mod(x_new, gain_ref[...], sc_ref[...], sh_ref[...]).astype(hn_ref.dtype)
        else:
            o_ref = refs[6]
        o_ref[...] = x_new


def ffn_dense(h, x, w_gu, w_down, g_f, next_norm=None, tm=512, tf=1408):
    L, D = x.shape
    Fh = w_down.shape[0]
    nf = Fh // tf
    row = pl.BlockSpec((1, D), lambda i, j: (0, 0))
    tile = pl.BlockSpec((tm, D), lambda i, j: (i, 0))
    args = [h, x, w_gu, w_gu, w_down, g_f]
    in_specs = [tile, tile,
                pl.BlockSpec((D, tf), lambda i, j: (0, j)),
                pl.BlockSpec((D, tf), lambda i, j: (0, j + nf)),
                pl.BlockSpec((tf, D), lambda i, j: (j, 0)), row]
    out_specs = [tile]
    out_shape = [jax.ShapeDtypeStruct((L, D), F32)]
    if next_norm is not None:
        args += list(next_norm)
        in_specs += [row, row, row]
        out_specs.append(tile)
        out_shape.append(jax.ShapeDtypeStruct((L, D), BF16))
    out = pl.pallas_call(
        functools.partial(_ffn_kernel, next_norm is not None),
        grid=(L // tm, nf),
        in_specs=in_specs,
        out_specs=out_specs,
        out_shape=out_shape,
        scratch_shapes=[pltpu.VMEM((tm, D), F32)],
        compiler_params=_cparams("arbitrary", "arbitrary"),
    )(*args)
    return out if next_norm is not None else (out[0], None)


def _moe_slots(slot_scr, comb_ref):
    slotval = slot_scr[...]
    comb = comb_ref[...]
    s1 = jnp.max(slotval, axis=-1, keepdims=True)
    first = slotval == s1
    c1 = jnp.sum(jnp.where(first, comb, 0.0), axis=-1, keepdims=True)
    rest = jnp.where(first, -1.0, slotval)
    s2 = jnp.max(rest, axis=-1, keepdims=True)
    c2 = jnp.sum(jnp.where(jnp.logical_and(rest == s2, rest >= 0.0), comb, 0.0), axis=-1, keepdims=True)
    return s1, s2, c1, c2


def _moe_kernel(nsb_ref, off_ref, h_ref, x_ref, comb_ref, wg_ref, wu_ref, wd_ref, gf_ref, o_ref,
                slot_scr, cslot_scr, xy_scr, acc_scr):
    i = pl.program_id(0)
    e = pl.program_id(1)
    j = pl.program_id(2)
    last_j = pl.num_programs(2) - 1
    T = h_ref.shape[0]
    S = xy_scr.shape[0]
    SB = MOE_SB

    @pl.when(jnp.logical_and(e == 0, j == 0))
    def _():
        lane = lax.broadcasted_iota(jnp.int32, (LANES, LANES), 1)
        off = jnp.zeros((LANES, LANES), F32)
        for ex in range(N_EXPERTS):
            off = jnp.where(lane == ex, off_ref[i, ex].astype(F32), off)
        before = _tril(LANES, strict=True).astype(BF16)
        ones = jnp.ones((LANES, LANES), BF16)
        for rb in range(T // LANES):
            rows = slice(rb * LANES, (rb + 1) * LANES)
            sel = comb_ref[rows, :] > 0.0
            sel_b = jnp.where(sel, 1.0, 0.0).astype(BF16)
            slot_scr[rows, :] = jnp.where(sel, _dot(before, sel_b) + off, -1.0)
            off = off + _dot(ones, sel_b)
        s1, s2, c1, c2 = _moe_slots(slot_scr, comb_ref)
        used = off_ref[i, N_EXPERTS - 1] + nsb_ref[i, N_EXPERTS - 1] * SB
        for cb in range(S // MOE_CB):
            chunk = slice(cb * MOE_CB, (cb + 1) * MOE_CB)

            @pl.when(used > cb * MOE_CB)
            def _():
                slot = (lax.broadcasted_iota(jnp.int32, (T, MOE_CB), 1) + cb * MOE_CB).astype(F32)
                w = jnp.where(s1 == slot, c1, 0.0) + jnp.where(s2 == slot, c2, 0.0)
                pt = jnp.where(w > 0.0, 1.0, 0.0).astype(BF16)
                xy_scr[chunk, :] = _dot_tn(pt, h_ref[...]).astype(BF16)
                w_hi = w.astype(BF16)
                w_lo = (w - w_hi.astype(F32)).astype(BF16)
                ones_t = jnp.ones((T, LANES), BF16)
                cslot_scr[chunk, :] = _dot_tn(w_hi, ones_t) + _dot_tn(w_lo, ones_t)

            @pl.when(used <= cb * MOE_CB)
            def _():
                xy_scr[chunk, :] = jnp.zeros((MOE_CB, xy_scr.shape[1]), BF16)
                cslot_scr[chunk, :] = jnp.zeros((MOE_CB, LANES), F32)

    base = off_ref[i, e]
    nb = nsb_ref[i, e]

    def ffn_block(r0, n_rows):
        rows = pl.ds(pl.multiple_of(base + r0, SB), n_rows)
        arows = pl.ds(pl.multiple_of(r0, SB), n_rows)
        xb = xy_scr[rows, :]
        gte = _dot(xb, wg_ref[0])
        up = _dot(xb, wu_ref[0])
        part = _dot((gte * _sigmoid(gte) * up).astype(BF16), wd_ref[0])

        @pl.when(j == 0)
        def _():
            acc_scr[arows, :] = part

        @pl.when(jnp.logical_and(j > 0, j < last_j))
        def _():
            acc_scr[arows, :] += part

        @pl.when(j == last_j)
        def _():
            xy_scr[rows, :] = ((acc_scr[arows, :] + part) * cslot_scr[rows, 0:1]).astype(BF16)

    def big_body(p, carry):
        ffn_block(p * (MOE_MAXB * SB), MOE_MAXB * SB)
        return carry

    lax.fori_loop(0, nb // MOE_MAXB, big_body, 0)
    for rest in range(1, MOE_MAXB):
        @pl.when(nb % MOE_MAXB == rest)
        def _():
            ffn_block((nb - rest) * SB, rest * SB)

    @pl.when(jnp.logical_and(e == pl.num_programs(1) - 1, j == last_j))
    def _():
        s1, s2, _, _ = _moe_slots(slot_scr, comb_ref)
        total = jnp.zeros((T, xy_scr.shape[1]), F32)
        for cb in range(S // MOE_CB):
            slot = (lax.broadcasted_iota(jnp.int32, (T, MOE_CB), 1) + cb * MOE_CB).astype(F32)
            pt = jnp.where(jnp.logical_or(s1 == slot, s2 == slot), 1.0, 0.0).astype(BF16)
            total = total + _dot(pt, xy_scr[cb * MOE_CB:(cb + 1) * MOE_CB, :])
        o_ref[...] = x_ref[...] + gf_ref[...] * total


def ffn_moe(h, x, comb, w_gu, w_down, g_f, tm=1024, tf=896):
    L, D = x.shape
    E, Fh, _ = w_down.shape
    nf = Fh // tf
    assert nf >= 2 and E == N_EXPERTS
    nt = L // tm
    counts = jnp.sum((comb[:, :E] > 0.0).reshape(nt, tm, E), axis=1, dtype=jnp.int32)
    nsb = (counts + (MOE_SB - 1)) // MOE_SB
    off = (jnp.cumsum(nsb, axis=1) - nsb) * MOE_SB
    n_slots = -(-(TOP_K * tm + E * MOE_SB) // MOE_CB) * MOE_CB
    idx = lambda f: (lambda i, e, j, n, o: f(i, e, j))
    grid_spec = pltpu.PrefetchScalarGridSpec(
        num_scalar_prefetch=2,
        grid=(nt, E, nf),
        in_specs=[pl.BlockSpec((tm, D), idx(lambda i, e, j: (i, 0))),
                  pl.BlockSpec((tm, D), idx(lambda i, e, j: (i, 0))),
                  pl.BlockSpec((tm, LANES), idx(lambda i, e, j: (i, 0))),
                  pl.BlockSpec((1, D, tf), idx(lambda i, e, j: (e, 0, j))),
                  pl.BlockSpec((1, D, tf), idx(lambda i, e, j: (e, 0, j + nf))),
                  pl.BlockSpec((1, tf, D), idx(lambda i, e, j: (e, j, 0))),
                  pl.BlockSpec((1, D), idx(lambda i, e, j: (0, 0)))],
        out_specs=pl.BlockSpec((tm, D), idx(lambda i, e, j: (i, 0))),
        scratch_shapes=[pltpu.VMEM((tm, LANES), F32),
                        pltpu.VMEM((n_slots, LANES), F32),
                        pltpu.VMEM((n_slots, D), BF16),
                        pltpu.VMEM((tm, D), F32)],
    )
    return pl.pallas_call(
        _moe_kernel,
        grid_spec=grid_spec,
        out_shape=jax.ShapeDtypeStruct((L, D), F32),
        compiler_params=_cparams("arbitrary", "arbitrary", "arbitrary"),
    )(nsb, off, h, x, comb, w_gu, w_gu, w_down, g_f)


def _seg_matrix(width, head_dim):
    idx = jnp.arange(width) // head_dim
    return (idx[:, None] == idx[None, :]).astype(BF16)


def _pad_cols(w, n):
    return jnp.pad(w, ((0, 0), (0, n - w.shape[1])))


def _pad_rows(w, n):
    return jnp.pad(w, ((0, n - w.shape[0]), (0, 0)))


def rwkv_branch(h, w_rw, p, v_first):
    r, k, v, lw, a, b, g, bonus = rw_pre(h, w_rw, p, v_first)
    return wkv7(r, lw, k, v, a, b), bonus, g, v


def fox_branch(h, w_fox, p):
    q, k, v, F = fox_pre(h, w_fox, p)
    FT = jnp.transpose(F[:, :FOX_HEADS])
    lo = fox_first_block(FT, p["bound2"])
    return fox_attention(q, k, v, FT, lo)


def kernel(x, c, ada_w, ada_b, norm_mix, norm_ffn, w_in, rw_mu, rw_w0, rw_w_up, rw_a0, rw_a_up, rw_g_up, rw_k_k, rw_k_a, rw_r_k, rw_lnx_w, rw_lnx_b, rw_v0, rw_v_down, rw_v_up, ssd_conv_w, ssd_conv_b, ssd_dt_bias, ssd_a_log, ssd_d, ssd_norm, fox_f_bias, fox_q_gain, fox_k_gain, gate_b, proj_rw, proj_ssd, proj_fox, w_out, ffn_w_gu, ffn_w_down, moe_router, moe_w_gu, moe_w_down):
    depth = w_in.shape[0]
    D = D_MODEL
    xs = x[0]
    row = lambda t: t.reshape(1, -1).astype(F32)
    seg64 = _seg_matrix(RW_WIDTH, RW_HEAD_DIM)
    rw_cols = 3 * RW_WIDTH + RW_DECAY_LORA + RW_AAA_LORA + RW_GATE_LORA
    ssd_cols = SSD_WIDTH + SSD_XBC + SSD_HEADS
    fox_cols = 3 * FOX_WIDTH + FOX_HEADS
    expand = (jnp.arange(LANES)[:, None] == (jnp.arange(SSD_WIDTH) // SSD_HEAD_DIM)[None, :]).astype(BF16)
    v_first = None
    mods = []
    for l in range(depth):
        mod = adaln_mod(c, ada_w[l], ada_b[l])
        mods.append([mod[:, i * D:(i + 1) * D] for i in range(6)])
    h = None
    for l in range(depth):
        sh_m, sc_m, g_m, sh_f, sc_f, g_f = mods[l]
        if h is None:
            h = norm_mod(xs, row(norm_mix[l]), sc_m, sh_m)

        wl = w_in[l]
        o = 0
        w_r = wl[:, o:o + rw_cols]; o += rw_cols
        w_s = wl[:, o:o + ssd_cols]; o += ssd_cols
        w_f = wl[:, o:o + fox_cols]; o += fox_cols
        w_g = wl[:, o:]
        W3 = 3 * RW_WIDTH
        o_a = W3 + RW_DECAY_LORA
        o_g = o_a + RW_AAA_LORA

        def rw_layout(t, hv):
            parts = [t[:, :W3], _pad_cols(t[:, W3:o_a], LANES), _pad_cols(t[:, o_a:o_g], LANES),
                     _pad_cols(t[:, o_g:], 2 * LANES), _pad_cols(hv, LANES)]
            return jnp.concatenate(parts, axis=1)

        if l == 0:
            hv_w = jnp.zeros((D, RW_VRES_LORA), F32)
        else:
            hv_w = rw_v_down[l - 1]
        w_rw = rw_layout(w_r, hv_w).astype(BF16)
        mu = rw_layout(rw_mu[l].reshape(1, -1), jnp.zeros((1, RW_VRES_LORA), F32))
        rw_p = dict(mu=mu, w0=row(rw_w0[l]), w_up=_pad_rows(rw_w_up[l], LANES).astype(BF16), a0=row(rw_a0[l]),
                    a_up=_pad_rows(rw_a_up[l], LANES).astype(BF16),
                    g_up=_pad_rows(rw_g_up[l], 2 * LANES).astype(BF16),
                    k_k=row(rw_k_k[l]), k_a=row(rw_k_a[l]), r_k=row(rw_r_k[l]), seg=seg64)
        if l > 0:
            rw_p.update(v0=row(rw_v0[l - 1]), v_up=_pad_rows(rw_v_up[l - 1], LANES).astype(BF16))
        yn, bonus, g_rw, v_cur = rwkv_branch(h, w_rw, rw_p, v_first)
        if l == 0:
            v_first = v_cur

        w_ssd = jnp.concatenate([w_s[:, SSD_WIDTH:SSD_WIDTH + SSD_XBC], w_s[:, :SSD_WIDTH],
                                 _pad_cols(w_s[:, SSD_WIDTH + SSD_XBC:], LANES)], axis=1).astype(BF16)
        A = -jnp.exp(ssd_a_log[l].astype(F32))
        ssd_p = dict(conv_w=ssd_conv_w[l], conv_b=row(ssd_conv_b[l]),
                     dt_bias=_pad_cols(row(ssd_dt_bias[l]), LANES),
                     A=_pad_cols(row(A), LANES), expand=expand,
                     d_skip=row(jnp.repeat(ssd_d[l], SSD_HEAD_DIM)), norm_w=row(ssd_norm[l]))
        y_ssd = ssd_mixer(h, w_ssd, ssd_p)

        fb = fox_f_bias[l]
        hid = jnp.arange(FOX_HEADS)
        ahead = jnp.logical_or(fb[None, :] < fb[:, None],
                               jnp.logical_and(fb[None, :] == fb[:, None], hid[None, :] < hid[:, None]))
        rank = jnp.sum(ahead, axis=1)
        head_order = jnp.sum(jnp.where(rank[None, :] == hid[:, None], hid[None, :], 0), axis=1)
        by_head = lambda t: jnp.take(t.reshape(D, FOX_HEADS, FOX_HEAD_DIM), head_order, axis=1).reshape(D, FOX_WIDTH)
        w_fox = jnp.concatenate(
            [by_head(w_f[:, i * FOX_WIDTH:(i + 1) * FOX_WIDTH]) for i in range(3)]
            + [_pad_cols(jnp.take(w_f[:, 3 * FOX_WIDTH:], head_order, axis=1), LANES)], axis=1).astype(BF16)
        f_bias_sorted = jnp.take(fox_f_bias[l], head_order)
        proj_fox_sorted = jnp.take(proj_fox[l].reshape(FOX_HEADS, FOX_HEAD_DIM, D), head_order,
                                   axis=0).reshape(FOX_WIDTH, D)
        bound2 = (1.02 * FOX_HEAD_DIM ** 0.5 * LOG2E) * jnp.max(jnp.abs(fox_q_gain[l])) * jnp.max(jnp.abs(fox_k_gain[l]))
        fox_p = dict(f_bias=_pad_cols(row(f_bias_sorted), LANES),
                     q_gain=row(jnp.tile(fox_q_gain[l], FOX_HEADS)) * (FOX_HEAD_DIM ** -0.5 * LOG2E),
                     k_gain=row(jnp.tile(fox_k_gain[l], FOX_HEADS)), seg=seg64, bound2=bound2)
        y_fox = fox_branch(h, w_fox, fox_p)

        mp = dict(w_gate=w_g.astype(BF16), gate_b=row(gate_b[l]), lnx_w=row(rw_lnx_w[l]), lnx_b=row(rw_lnx_b[l]),
                  proj_rw=proj_rw[l].astype(BF16), proj_ssd=proj_ssd[l].astype(BF16),
                  proj_fox=proj_fox_sorted.astype(BF16), w_out=w_out[l].astype(BF16), g_m=g_m,
                  norm_gain=row(norm_ffn[l]), sc_f=sc_f, sh_f=sh_f)

        if l % 2 == 0:
            xs, hf = merge(xs, h, yn, bonus, g_rw, y_ssd, y_fox, mp)
            next_norm = None
            if l + 1 < depth:
                next_norm = (row(norm_mix[l + 1]), mods[l + 1][1], mods[l + 1][0])
            xs, h = ffn_dense(hf, xs, ffn_w_gu[l // 2].astype(BF16), ffn_w_down[l // 2].astype(BF16), g_f,
                              next_norm)
        else:
            xs, hf, comb = merge(xs, h, yn, bonus, g_rw, y_ssd, y_fox, mp,
                                 router=_pad_cols(moe_router[l // 2], LANES))
            xs = ffn_moe(hf, xs, comb, moe_w_gu[l // 2].astype(BF16), moe_w_down[l // 2].astype(BF16), g_f)
            h = None
    return xs[None]
```
